```python
import math
import jax, jax.numpy as jnp
from jax import lax
import numpy as np

D_MODEL = 2048
BATCH = 8
SEQ = 4096
DEPTH = 4

GRID_W = 64
N_MIXERS = 2
HEAD_DIM = 128
A_HEADS = 16
A_KV_HEADS = 4
ROPE_THETA = 10000.0
A_Q_BLOCK = 128
A_QKV_WIDTH = (A_HEADS + 2 * A_KV_HEADS) * HEAD_DIM
B_GROUPS = ((128, 1), (512, 4), (2048, 16))
B_HEADS_PER_GROUP = 8
B_Q_BLOCK = 64
B_WIDTH = len(B_GROUPS) * B_HEADS_PER_GROUP * HEAD_DIM
B_QKV_WIDTH = 3 * B_WIDTH
REL_BUCKETS = 32
REL_MAX_DISTANCE = 1024
D_FF = 5632
CONV_WIDTH = 3
EPS = 1e-6
NEG_INF = -1e30
N_A_LAYERS = (DEPTH + 1) // 2
N_B_LAYERS = DEPTH // 2

kernel_name = "hybrid_axial_gqa_dilated_convffn_encoder"


def rms_norm(x, gain):
    xf = x.astype(jnp.float32)
    y = xf * lax.rsqrt(jnp.mean(xf * xf, axis=-1, keepdims=True) + EPS) * gain.astype(jnp.float32)
    return y.astype(x.dtype)


def axial_rope(x):
    seq = x.shape[1]
    rows = seq // GRID_W
    row_ids = jnp.repeat(jnp.arange(rows, dtype=jnp.float32), GRID_W)
    col_ids = jnp.tile(jnp.arange(GRID_W, dtype=jnp.float32), rows)
    half = HEAD_DIM // 2
    quarter = half // 2
    inv_freq = ROPE_THETA ** (-jnp.arange(quarter, dtype=jnp.float32) / quarter)

    def rot(xs, pos):
        ang = pos[:, None] * inv_freq[None, :]
        cos = jnp.cos(ang)[None, :, None, :]
        sin = jnp.sin(ang)[None, :, None, :]
        x1, x2 = xs[..., :quarter], xs[..., quarter:]
        return jnp.concatenate([x1 * cos - x2 * sin, x2 * cos + x1 * sin], axis=-1)

    return jnp.concatenate([rot(x[..., :half], row_ids), rot(x[..., half:], col_ids)], axis=-1)


def mixer_a(h, w_qkv, w_o, q_gain, k_gain):
    b, s, _ = h.shape
    qkv = h @ w_qkv
    nq = A_HEADS * HEAD_DIM
    nk = A_KV_HEADS * HEAD_DIM
    q = qkv[..., :nq].reshape(b, s, A_HEADS, HEAD_DIM)
    k = qkv[..., nq:nq + nk].reshape(b, s, A_KV_HEADS, HEAD_DIM)
    v = qkv[..., nq + nk:].reshape(b, s, A_KV_HEADS, HEAD_DIM)
    q = axial_rope(rms_norm(q.astype(jnp.float32), q_gain)).astype(h.dtype)
    k = axial_rope(rms_norm(k.astype(jnp.float32), k_gain)).astype(h.dtype)
    grp = A_HEADS // A_KV_HEADS
    nblk = s // A_Q_BLOCK
    qb = q.reshape(b, nblk, A_Q_BLOCK, A_KV_HEADS, grp, HEAD_DIM).transpose(1, 0, 2, 3, 4, 5)
    scale = HEAD_DIM ** -0.5

    def attend(q_blk):
        logits = jnp.einsum('bqkgd,bskd->bkgqs', q_blk, k).astype(jnp.float32) * scale
        p = jax.nn.softmax(logits, axis=-1).astype(v.dtype)
        return jnp.einsum('bkgqs,bskd->bqkgd', p, v)

    o = lax.map(attend, qb)
    o = o.transpose(1, 0, 2, 3, 4, 5).reshape(b, s, nq)
    return o @ w_o


def t5_bucket(rel):
    nb = REL_BUCKETS // 2
    max_exact = nb // 2
    base = jnp.where(rel > 0, nb, 0)
    n = jnp.abs(rel)
    nf = jnp.maximum(n, 1).astype(jnp.float32)
    large = max_exact + (jnp.log(nf / max_exact) / math.log(REL_MAX_DISTANCE / max_exact)
                         * (nb - max_exact)).astype(jnp.int32)
    large = jnp.minimum(large, nb - 1)
    return base + jnp.where(n < max_exact, n, large)


def dilated_group(q, k, v, rel_bias_g, window, dilation):
    b, s, h, d = q.shape
    half_span = window // (2 * dilation)
    L = s // dilation
    nblk = -(-L // B_Q_BLOCK)
    Lp = nblk * B_Q_BLOCK
    kv_len = B_Q_BLOCK + 2 * half_span
    qs = q.reshape(b, L, dilation, h, d)
    ks = k.reshape(b, L, dilation, h, d)
    vs = v.reshape(b, L, dilation, h, d)
    qs = jnp.pad(qs, ((0, 0), (0, Lp - L), (0, 0), (0, 0), (0, 0)))
    pad_kv = ((0, 0), (half_span, Lp - L + half_span), (0, 0), (0, 0), (0, 0))
    ks = jnp.pad(ks, pad_kv)
    vs = jnp.pad(vs, pad_kv)
    blk_start = jnp.arange(nblk) * B_Q_BLOCK
    key_idx = blk_start[:, None] + jnp.arange(kv_len)[None, :]
    kb = ks[:, key_idx]
    vb = vs[:, key_idx]
    qb = qs.reshape(b, nblk, B_Q_BLOCK, dilation, h, d)
    rel = jnp.arange(kv_len)[None, :] - half_span - jnp.arange(B_Q_BLOCK)[:, None]
    bias = rel_bias_g[t5_bucket(rel * dilation)].astype(jnp.float32).transpose(2, 0, 1)
    key_pos = key_idx - half_span
    valid = (jnp.abs(rel) <= half_span)[None] & ((key_pos >= 0) & (key_pos < L))[:, None, :]
    scale = d ** -0.5
    logits = jnp.einsum('bnqchd,bnkchd->bnchqk', qb, kb).astype(jnp.float32) * scale + bias[None, None, None]
    logits = jnp.where(valid[None, :, None, None], logits, NEG_INF)
    m = jnp.max(logits, axis=-1, keepdims=True)
    p = jnp.exp(logits - m)
    l = jnp.sum(p, axis=-1, keepdims=True)
    o = jnp.einsum('bnchqk,bnkchd->bnqchd', p.astype(v.dtype), vb).astype(jnp.float32)
    l_t = l[..., 0].transpose(0, 1, 4, 2, 3)
    o = o / l_t[..., None]
    log_z = (m[..., 0] + jnp.log(l[..., 0])).transpose(0, 1, 4, 2, 3)
    o = o.reshape(b, Lp, dilation, h, d)[:, :L].reshape(b, s, h, d)
    log_z = log_z.reshape(b, Lp, dilation, h)[:, :L].reshape(b, s, h)
    return o, log_z


def mixer_b(h, w_qkv, w_o, rel_bias):
    b, s, _ = h.shape
    n_g = len(B_GROUPS)
    hg = B_HEADS_PER_GROUP
    qkv = (h @ w_qkv).reshape(b, s, n_g, 3, hg, HEAD_DIM)
    outs, log_zs = [], []
    for g, (window, dil) in enumerate(B_GROUPS):
        o, lz = dilated_group(qkv[:, :, g, 0], qkv[:, :, g, 1], qkv[:, :, g, 2],
                              rel_bias[:, g * hg:(g + 1) * hg], window, dil)
        outs.append(o)
        log_zs.append(lz)
    wts = jax.nn.softmax(jnp.stack(log_zs, axis=0), axis=0)
    y = jnp.concatenate([wts[g][..., None] * outs[g] for g in range(n_g)], axis=2)
    y = y.reshape(b, s, B_WIDTH).astype(h.dtype)
    return y @ w_o


def conv_ffn(h, w_up, conv_w, conv_b, w_down):
    u = h @ w_up
    c = u.shape[-1]
    pad = CONV_WIDTH // 2
    u = lax.conv_general_dilated(u, conv_w[:, None, :].astype(u.dtype), window_strides=(1,),
                                 padding=((pad, pad),), dimension_numbers=('NWC', 'WIO', 'NWC'),
                                 feature_group_count=c) + conv_b
    gate, val = jnp.split(u, 2, axis=-1)
    return (jax.nn.silu(gate) * val) @ w_down


def _fwd_setup_inputs(seed: int = 0) -> dict:
    key = jax.random.key(seed)
    ks = jax.random.split(key, 16)
    f32 = jnp.float32
    nrm = lambda k, shape, sc: jax.random.normal(k, shape, f32) * sc
    return {
        "x": jax.random.normal(ks[0], (BATCH, SEQ, D_MODEL), f32),
        "a_w_qkv": nrm(ks[1], (N_A_LAYERS, D_MODEL, A_QKV_WIDTH), D_MODEL ** -0.5),
        "a_w_o": nrm(ks[2], (N_A_LAYERS, A_HEADS * HEAD_DIM, D_MODEL), (A_HEADS * HEAD_DIM) ** -0.5),
        "a_q_gain": 1.0 + nrm(ks[3], (N_A_LAYERS, HEAD_DIM), 0.02),
        "a_k_gain": 1.0 + nrm(ks[4], (N_A_LAYERS, HEAD_DIM), 0.02),
        "b_w_qkv": nrm(ks[5], (N_B_LAYERS, D_MODEL, B_QKV_WIDTH), D_MODEL ** -0.5),
        "b_w_o": nrm(ks[6], (N_B_LAYERS, B_WIDTH, D_MODEL), B_WIDTH ** -0.5),
        "rel_bias": nrm(ks[7], (REL_BUCKETS, len(B_GROUPS) * B_HEADS_PER_GROUP), 0.5),
        "mix_norm": 1.0 + nrm(ks[8], (DEPTH, D_MODEL), 0.02),
        "ffn_norm": 1.0 + nrm(ks[9], (DEPTH, D_MODEL), 0.02),
        "w_up": nrm(ks[10], (DEPTH, D_MODEL, 2 * D_FF), D_MODEL ** -0.5),
        "conv_w": nrm(ks[11], (DEPTH, CONV_WIDTH, 2 * D_FF), CONV_WIDTH ** -0.5),
        "conv_b": nrm(ks[12], (DEPTH, 2 * D_FF), 0.01),
        "w_down": nrm(ks[13], (DEPTH, D_FF, D_MODEL), D_FF ** -0.5),
        "final_norm": 1.0 + nrm(ks[14], (D_MODEL,), 0.02),
    }


def _fwd_reference(x, a_w_qkv, a_w_o, a_q_gain, a_k_gain, b_w_qkv, b_w_o, rel_bias,
              mix_norm, ffn_norm, w_up, conv_w, conv_b, w_down, final_norm):
    h = x
    for i in range(DEPTH):
        hn = rms_norm(h, mix_norm[i])
        j = i // N_MIXERS
        if i % N_MIXERS == 0:
            h = h + mixer_a(hn, a_w_qkv[j], a_w_o[j], a_q_gain[j], a_k_gain[j])
        else:
            h = h + mixer_b(hn, b_w_qkv[j], b_w_o[j], rel_bias)
        h = h + conv_ffn(rms_norm(h, ffn_norm[i]), w_up[i], conv_w[i], conv_b[i], w_down[i])
    return rms_norm(h, final_norm)


import jax as _jax
import jax.numpy as _jnp

TWIN_FORMAT = 'train_step'
FWD_PARAMS = ['x', 'a_w_qkv', 'a_w_o', 'a_q_gain', 'a_k_gain', 'b_w_qkv', 'b_w_o', 'rel_bias', 'mix_norm', 'ffn_norm', 'w_up', 'conv_w', 'conv_b', 'w_down', 'final_norm']
TWIN_WEIGHTS = ['a_w_qkv', 'a_w_o', 'a_q_gain', 'a_k_gain', 'b_w_qkv', 'b_w_o', 'rel_bias', 'mix_norm', 'ffn_norm', 'w_up', 'conv_w', 'conv_b', 'w_down', 'final_norm']
TWIN_DIFF_INPUT = 'x'
TWIN_INPUTS = ['x', 'a_w_qkv', 'a_w_o', 'a_q_gain', 'a_k_gain', 'b_w_qkv', 'b_w_o', 'rel_bias', 'mix_norm', 'ffn_norm', 'w_up', 'conv_w', 'conv_b', 'w_down', 'final_norm', 'loss_target', 'm_a_w_qkv', 'm_a_w_o', 'm_a_q_gain', 'm_a_k_gain', 'm_b_w_qkv', 'm_b_w_o', 'm_rel_bias', 'm_mix_norm', 'm_ffn_norm', 'm_w_up', 'm_conv_w', 'm_conv_b', 'm_w_down', 'm_final_norm', 'v_a_w_qkv', 'v_a_w_o', 'v_a_q_gain', 'v_a_k_gain', 'v_b_w_qkv', 'v_b_w_o', 'v_rel_bias', 'v_mix_norm', 'v_ffn_norm', 'v_w_up', 'v_conv_w', 'v_conv_b', 'v_w_down', 'v_final_norm']
TWIN_OUTPUTS = ['loss', 'grad_x', 'grad_a_w_qkv', 'grad_a_w_o', 'grad_a_q_gain', 'grad_a_k_gain', 'grad_b_w_qkv', 'grad_b_w_o', 'grad_rel_bias', 'grad_mix_norm', 'grad_ffn_norm', 'grad_w_up', 'grad_conv_w', 'grad_conv_b', 'grad_w_down', 'grad_final_norm', 'delta_a_w_qkv', 'delta_a_w_o', 'delta_a_q_gain', 'delta_a_k_gain', 'delta_b_w_qkv', 'delta_b_w_o', 'delta_rel_bias', 'delta_mix_norm', 'delta_ffn_norm', 'delta_w_up', 'delta_conv_w', 'delta_conv_b', 'delta_w_down', 'delta_final_norm', 'new_m_a_w_qkv', 'new_m_a_w_o', 'new_m_a_q_gain', 'new_m_a_k_gain', 'new_m_b_w_qkv', 'new_m_b_w_o', 'new_m_rel_bias', 'new_m_mix_norm', 'new_m_ffn_norm', 'new_m_w_up', 'new_m_conv_w', 'new_m_conv_b', 'new_m_w_down', 'new_m_final_norm', 'new_v_a_w_qkv', 'new_v_a_w_o', 'new_v_a_q_gain', 'new_v_a_k_gain', 'new_v_b_w_qkv', 'new_v_b_w_o', 'new_v_rel_bias', 'new_v_mix_norm', 'new_v_ffn_norm', 'new_v_w_up', 'new_v_conv_w', 'new_v_conv_b', 'new_v_w_down', 'new_v_final_norm']
TWIN_LEAF_KINDS = {'loss': 'loss', 'grad_x': 'grad_x', 'grad_a_w_qkv': 'grad_w', 'grad_a_w_o': 'grad_w', 'grad_a_q_gain': 'grad_w', 'grad_a_k_gain': 'grad_w', 'grad_b_w_qkv': 'grad_w', 'grad_b_w_o': 'grad_w', 'grad_rel_bias': 'grad_w', 'grad_mix_norm': 'grad_w', 'grad_ffn_norm': 'grad_w', 'grad_w_up': 'grad_w', 'grad_conv_w': 'grad_w', 'grad_conv_b': 'grad_w', 'grad_w_down': 'grad_w', 'grad_final_norm': 'grad_w', 'delta_a_w_qkv': 'delta_w', 'delta_a_w_o': 'delta_w', 'delta_a_q_gain': 'delta_w', 'delta_a_k_gain': 'delta_w', 'delta_b_w_qkv': 'delta_w', 'delta_b_w_o': 'delta_w', 'delta_rel_bias': 'delta_w', 'delta_mix_norm': 'delta_w', 'delta_ffn_norm': 'delta_w', 'delta_w_up': 'delta_w', 'delta_conv_w': 'delta_w', 'delta_conv_b': 'delta_w', 'delta_w_down': 'delta_w', 'delta_final_norm': 'delta_w', 'new_m_a_w_qkv': 'new_m', 'new_m_a_w_o': 'new_m', 'new_m_a_q_gain': 'new_m', 'new_m_a_k_gain': 'new_m', 'new_m_b_w_qkv': 'new_m', 'new_m_b_w_o': 'new_m', 'new_m_rel_bias': 'new_m', 'new_m_mix_norm': 'new_m', 'new_m_ffn_norm': 'new_m', 'new_m_w_up': 'new_m', 'new_m_conv_w': 'new_m', 'new_m_conv_b': 'new_m', 'new_m_w_down': 'new_m', 'new_m_final_norm': 'new_m', 'new_v_a_w_qkv': 'new_v', 'new_v_a_w_o': 'new_v', 'new_v_a_q_gain': 'new_v', 'new_v_a_k_gain': 'new_v', 'new_v_b_w_qkv': 'new_v', 'new_v_b_w_o': 'new_v', 'new_v_rel_bias': 'new_v', 'new_v_mix_norm': 'new_v', 'new_v_ffn_norm': 'new_v', 'new_v_w_up': 'new_v', 'new_v_conv_w': 'new_v', 'new_v_conv_b': 'new_v', 'new_v_w_down': 'new_v', 'new_v_final_norm': 'new_v'}


def _forward(args):
    return _fwd_reference(*[args[k] for k in FWD_PARAMS])


def _output_shape():
    out = _jax.eval_shape(lambda: _forward(_fwd_setup_inputs(0)))
    return out.shape, out.dtype

N_MICROBATCH = 1
ADAM_LR = 0.001
ADAM_B1 = 0.9
ADAM_B2 = 0.999
ADAM_EPS = 1e-08
ADAM_WD = 0.01
ADAM_STEP = 10
PER_EXAMPLE_BATCH_AXIS = {'x': 0, 'loss_target': 0}
SHARED_INPUTS = []
_WEIGHT_DTYPES = {'a_w_qkv': _jnp.float32, 'a_w_o': _jnp.float32, 'a_q_gain': _jnp.float32, 'a_k_gain': _jnp.float32, 'b_w_qkv': _jnp.float32, 'b_w_o': _jnp.float32, 'rel_bias': _jnp.float32, 'mix_norm': _jnp.float32, 'ffn_norm': _jnp.float32, 'w_up': _jnp.float32, 'conv_w': _jnp.float32, 'conv_b': _jnp.float32, 'w_down': _jnp.float32, 'final_norm': _jnp.float32}
MOMENT_SCALE = {'a_w_qkv': 1.581350e-02, 'a_w_o': 1.200385e-02, 'a_q_gain': 4.080750e-02, 'a_k_gain': 4.224986e-02, 'b_w_qkv': 5.137137e-03, 'b_w_o': 6.756824e-03, 'rel_bias': 1.516448e-02, 'mix_norm': 1.587003e-02, 'ffn_norm': 6.716219e-02, 'w_up': 2.878561e-02, 'conv_w': 2.846429e-02, 'conv_b': 2.892063e-02, 'w_down': 4.691664e-02, 'final_norm': 1.598699e+01}


def _to_microbatches(a, axis):
    t = _jnp.moveaxis(a, axis, 0)
    t = t.reshape((N_MICROBATCH, t.shape[0] // N_MICROBATCH) + t.shape[1:])
    return _jnp.moveaxis(t, 1, axis + 1)


def setup_inputs(seed: int = 0) -> dict:
    inp = _fwd_setup_inputs(seed)
    key = _jax.random.fold_in(_jax.random.key(seed), 7919)
    shape, _ = _output_shape()
    out = dict(inp)
    out["loss_target"] = _jax.random.normal(_jax.random.fold_in(key, 0), shape, _jnp.float32)
    for i, name in enumerate(TWIN_WEIGHTS):
        w = inp[name].astype(_jnp.float32)
        if MOMENT_SCALE is None:
            s = _jnp.sqrt(_jnp.mean(_jnp.square(w)) + 1e-30)
        else:
            s = MOMENT_SCALE[name]
        km, kv = _jax.random.split(_jax.random.fold_in(key, i + 1))
        out[name] = w
        out["m_" + name] = s * _jax.random.normal(km, w.shape, _jnp.float32)
        out["v_" + name] = (s * s) * _jax.random.uniform(kv, w.shape, _jnp.float32, 0.5, 1.5)
    if N_MICROBATCH > 1:
        for name, axis in PER_EXAMPLE_BATCH_AXIS.items():
            out[name] = _to_microbatches(out[name], axis)
    return {'x': out['x'], 'a_w_qkv': out['a_w_qkv'], 'a_w_o': out['a_w_o'], 'a_q_gain': out['a_q_gain'], 'a_k_gain': out['a_k_gain'], 'b_w_qkv': out['b_w_qkv'], 'b_w_o': out['b_w_o'], 'rel_bias': out['rel_bias'], 'mix_norm': out['mix_norm'], 'ffn_norm': out['ffn_norm'], 'w_up': out['w_up'], 'conv_w': out['conv_w'], 'conv_b': out['conv_b'], 'w_down': out['w_down'], 'final_norm': out['final_norm'], 'loss_target': out['loss_target'], 'm_a_w_qkv': out['m_a_w_qkv'], 'm_a_w_o': out['m_a_w_o'], 'm_a_q_gain': out['m_a_q_gain'], 'm_a_k_gain': out['m_a_k_gain'], 'm_b_w_qkv': out['m_b_w_qkv'], 'm_b_w_o': out['m_b_w_o'], 'm_rel_bias': out['m_rel_bias'], 'm_mix_norm': out['m_mix_norm'], 'm_ffn_norm': out['m_ffn_norm'], 'm_w_up': out['m_w_up'], 'm_conv_w': out['m_conv_w'], 'm_conv_b': out['m_conv_b'], 'm_w_down': out['m_w_down'], 'm_final_norm': out['m_final_norm'], 'v_a_w_qkv': out['v_a_w_qkv'], 'v_a_w_o': out['v_a_w_o'], 'v_a_q_gain': out['v_a_q_gain'], 'v_a_k_gain': out['v_a_k_gain'], 'v_b_w_qkv': out['v_b_w_qkv'], 'v_b_w_o': out['v_b_w_o'], 'v_rel_bias': out['v_rel_bias'], 'v_mix_norm': out['v_mix_norm'], 'v_ffn_norm': out['v_ffn_norm'], 'v_w_up': out['v_w_up'], 'v_conv_w': out['v_conv_w'], 'v_conv_b': out['v_conv_b'], 'v_w_down': out['v_w_down'], 'v_final_norm': out['v_final_norm']}


def _loss(weights, diff, rest, loss_target):
    with _jax.named_scope("forward"):
        args = {**rest, TWIN_DIFF_INPUT: diff, **{k: w.astype(_WEIGHT_DTYPES[k]) for k, w in weights.items()}}
        y = _forward(args)
    with _jax.named_scope("loss_head"):
        err = _jnp.square(y.astype(_jnp.float32) - loss_target)
        return 0.5 * _jnp.sum(_jnp.mean(err, axis=-1)) if err.ndim else 0.5 * err


def _adamw(w, g, m, v):
    m = ADAM_B1 * m + (1.0 - ADAM_B1) * g
    v = ADAM_B2 * v + (1.0 - ADAM_B2) * _jnp.square(g)
    m_hat = m / (1.0 - ADAM_B1 ** ADAM_STEP)
    v_hat = v / (1.0 - ADAM_B2 ** ADAM_STEP)
    delta = -ADAM_LR * (m_hat / (_jnp.sqrt(v_hat) + ADAM_EPS) + ADAM_WD * w)
    return delta, m, v


def reference(x, a_w_qkv, a_w_o, a_q_gain, a_k_gain, b_w_qkv, b_w_o, rel_bias, mix_norm, ffn_norm, w_up, conv_w, conv_b, w_down, final_norm, loss_target, m_a_w_qkv, m_a_w_o, m_a_q_gain, m_a_k_gain, m_b_w_qkv, m_b_w_o, m_rel_bias, m_mix_norm, m_ffn_norm, m_w_up, m_conv_w, m_conv_b, m_w_down, m_final_norm, v_a_w_qkv, v_a_w_o, v_a_q_gain, v_a_k_gain, v_b_w_qkv, v_b_w_o, v_rel_bias, v_mix_norm, v_ffn_norm, v_w_up, v_conv_w, v_conv_b, v_w_down, v_final_norm):
    given = dict(x=x, a_w_qkv=a_w_qkv, a_w_o=a_w_o, a_q_gain=a_q_gain, a_k_gain=a_k_gain, b_w_qkv=b_w_qkv, b_w_o=b_w_o, rel_bias=rel_bias, mix_norm=mix_norm, ffn_norm=ffn_norm, w_up=w_up, conv_w=conv_w, conv_b=conv_b, w_down=w_down, final_norm=final_norm, loss_target=loss_target, m_a_w_qkv=m_a_w_qkv, m_a_w_o=m_a_w_o, m_a_q_gain=m_a_q_gain, m_a_k_gain=m_a_k_gain, m_b_w_qkv=m_b_w_qkv, m_b_w_o=m_b_w_o, m_rel_bias=m_rel_bias, m_mix_norm=m_mix_norm, m_ffn_norm=m_ffn_norm, m_w_up=m_w_up, m_conv_w=m_conv_w, m_conv_b=m_conv_b, m_w_down=m_w_down, m_final_norm=m_final_norm, v_a_w_qkv=v_a_w_qkv, v_a_w_o=v_a_w_o, v_a_q_gain=v_a_q_gain, v_a_k_gain=v_a_k_gain, v_b_w_qkv=v_b_w_qkv, v_b_w_o=v_b_w_o, v_rel_bias=v_rel_bias, v_mix_norm=v_mix_norm, v_ffn_norm=v_ffn_norm, v_w_up=v_w_up, v_conv_w=v_conv_w, v_conv_b=v_conv_b, v_w_down=v_w_down, v_final_norm=v_final_norm)
    weights = {n: given[n] for n in TWIN_WEIGHTS}
    shared = {n: given[n] for n in SHARED_INPUTS}
    per_example = {n: given[n] for n in ['x']}
    grad_fn = _jax.value_and_grad(_loss, argnums=(0, 1))

    def one_microbatch(ex, loss_target):
        ex = dict(ex)
        diff = ex.pop(TWIN_DIFF_INPUT)
        return grad_fn(weights, diff, {**shared, **ex}, loss_target)

    if N_MICROBATCH == 1:
        loss, (grad_w, grad_x) = one_microbatch(per_example, given["loss_target"])
    else:
        def body(carry, xs):
            loss_sum, grad_sum = carry
            l_k, (gw_k, gx_k) = one_microbatch(xs[0], xs[1])
            with _jax.named_scope("update"):
                return (loss_sum + l_k, _jax.tree.map(_jnp.add, grad_sum, gw_k)), gx_k

        init = (_jnp.zeros((), _jnp.float32), _jax.tree.map(_jnp.zeros_like, weights))
        (loss, grad_w), grad_x = _jax.lax.scan(body, init, (per_example, given["loss_target"]))
    with _jax.named_scope("update"):
        delta_w, new_m, new_v = {}, {}, {}
        for n in TWIN_WEIGHTS:
            delta_w[n], new_m[n], new_v[n] = _adamw(weights[n], grad_w[n], given["m_" + n], given["v_" + n])
    return (loss, grad_x, *[grad_w[n] for n in TWIN_WEIGHTS], *[delta_w[n] for n in TWIN_WEIGHTS],
            *[new_m[n] for n in TWIN_WEIGHTS], *[new_v[n] for n in TWIN_WEIGHTS])
```

```python
import functools
import math

import jax
import jax.numpy as jnp
from jax import lax
from jax.experimental import pallas as pl
from jax.experimental.pallas import tpu as pltpu

F32 = jnp.float32
BF16 = jnp.bfloat16
MESH = pl.DeviceIdType.MESH

N_DEV = 8
LANES = 128
HEAD_DIM = 128
VMEM_LIMIT = 56 * 1024 * 1024
GRID_W = 64
ROPE_THETA = 10000.0
A_KV_HEADS = 4
B_GROUPS = ((128, 1), (512, 4), (2048, 16))
B_HEADS_PER_GROUP = 8
REL_BUCKETS = 32
REL_MAX_DISTANCE = 1024
EPS = 1e-6
NEG_INF = -1e30
ADAM_LR = 0.001
ADAM_B1 = 0.9
ADAM_B2 = 0.999
ADAM_EPS = 1e-08
ADAM_WD = 0.01
ADAM_STEP = 10

ROW_TILE = 256
MM_TM = 1024
MM_TK = 512
A_BQ = 512
A_BK = 512
B_BQ = 256

NN = (((1,), (0,)), ((), ()))
NT = (((1,), (1,)), ((), ()))
TN = (((0,), (0,)), ((), ()))


def _tile(n, pref):
    return pref if n % pref == 0 else n


def _params(sem):
    return pltpu.CompilerParams(dimension_semantics=sem, vmem_limit_bytes=VMEM_LIMIT)


def _dot(a, b, dims):
    return lax.dot_general(a, b, dims, preferred_element_type=F32)


def _mm(name, a, b, *, grid, a_blk, a_map, b_blk, b_map, o_blk, o_map, out_shape, out_dtype, dims,
        res=None):
    nk = grid[2]
    acc_shape = tuple(d for d in o_blk if d is not None)

    def body(*refs):
        if res is None:
            a_ref, b_ref, o_ref, acc = refs
            r_ref = None
        else:
            a_ref, b_ref, r_ref, o_ref, acc = refs
        k = pl.program_id(2)

        @pl.when(k == 0)
        def _():
            acc[...] = jnp.zeros_like(acc)

        acc[...] += _dot(a_ref[...].astype(BF16), b_ref[...].astype(BF16), dims)

        @pl.when(k == nk - 1)
        def _():
            r = acc[...]
            if r_ref is not None:
                r = r + r_ref[...]
            o_ref[...] = r.astype(out_dtype)

    in_specs = [pl.BlockSpec(a_blk, a_map), pl.BlockSpec(b_blk, b_map)]
    args = [a, b]
    if res is not None:
        in_specs.append(pl.BlockSpec(o_blk, o_map))
        args.append(res)
    return pl.pallas_call(
        body, name=name, grid=grid, in_specs=in_specs, out_specs=pl.BlockSpec(o_blk, o_map),
        out_shape=jax.ShapeDtypeStruct(out_shape, out_dtype),
        scratch_shapes=[pltpu.VMEM(acc_shape, F32)],
        compiler_params=_params(("parallel", "parallel", "arbitrary")),
    )(*args)


def mm_col_fwd(name, a, wg, layer, out_dtype, split=1):
    m, kdim = a.shape
    n_dev, _, _, w = wg.shape
    tm, tk = _tile(m, MM_TM), _tile(kdim, MM_TK)
    per = n_dev // split
    if split == 1:
        o_blk, o_map, o_shape = (tm, w), (lambda i, j, k: (i, j)), (m, n_dev * w)
    else:
        o_blk, o_map, o_shape = (None, tm, w), (lambda i, j, k: (j // per, i, j % per)), (split, m, per * w)
    return _mm(name, a, wg, grid=(m // tm, n_dev, kdim // tk),
               a_blk=(tm, tk), a_map=lambda i, j, k: (i, k),
               b_blk=(None, None, tk, w), b_map=lambda i, j, k: (j, layer, k, 0),
               o_blk=o_blk, o_map=o_map, out_shape=o_shape, out_dtype=out_dtype, dims=NN)


def mm_col_dx(name, dy, wg, layer, split=1):
    n_dev, _, kdim, w = wg.shape
    m = dy.shape[-2]
    tm, tk = _tile(m, MM_TM), _tile(kdim, MM_TK)
    per = n_dev // split
    if split == 1:
        a_blk, a_map = (tm, w), (lambda i, j, k: (i, k))
    else:
        a_blk, a_map = (None, tm, w), (lambda i, j, k: (k // per, i, k % per))
    return _mm(name, dy, wg, grid=(m // tm, kdim // tk, n_dev),
               a_blk=a_blk, a_map=a_map,
               b_blk=(None, None, tk, w), b_map=lambda i, j, k: (k, layer, j, 0),
               o_blk=(tm, tk), o_map=lambda i, j, k: (i, j), out_shape=(m, kdim), out_dtype=F32, dims=NT)


def mm_col_dw(name, x, dy, w, split=1):
    m, kdim = x.shape
    tm, tk = _tile(m, MM_TM), _tile(kdim, MM_TK)
    per = N_DEV // split
    if split == 1:
        b_blk, b_map = (tm, w), (lambda i, j, k: (k, j))
    else:
        b_blk, b_map = (None, tm, w), (lambda i, j, k: (j // per, k, j % per))
    return _mm(name, x, dy, grid=(kdim // tk, N_DEV, m // tm),
               a_blk=(tm, tk), a_map=lambda i, j, k: (k, i),
               b_blk=b_blk, b_map=b_map,
               o_blk=(None, tk, w), o_map=lambda i, j, k: (j, i, 0),
               out_shape=(N_DEV, kdim, w), out_dtype=BF16, dims=TN)


def mm_row_fwd(name, a, wg, layer, res):
    m, kdim = a.shape
    n = wg.shape[2]
    tm, tk, tn = _tile(m, MM_TM), _tile(kdim, MM_TK), _tile(n, 1024)
    return _mm(name, a, wg, grid=(m // tm, n // tn, kdim // tk),
               a_blk=(tm, tk), a_map=lambda i, j, k: (i, k),
               b_blk=(None, tk, tn), b_map=lambda i, j, k: (layer, k, j),
               o_blk=(tm, tn), o_map=lambda i, j, k: (i, j), out_shape=(m, n), out_dtype=F32, dims=NN,
               res=res)


def mm_row_dx(name, dy, wg, layer):
    m, n = dy.shape
    kdim = wg.shape[1]
    tm, tk, tn = _tile(m, MM_TM), _tile(kdim, MM_TK), _tile(n, 1024)
    return _mm(name, dy, wg, grid=(m // tm, kdim // tk, n // tn),
               a_blk=(tm, tn), a_map=lambda i, j, k: (i, k),
               b_blk=(None, tk, tn), b_map=lambda i, j, k: (layer, j, k),
               o_blk=(tm, tk), o_map=lambda i, j, k: (i, j), out_shape=(m, kdim), out_dtype=F32, dims=NT)


def mm_row_dw(name, x, dy):
    m, kdim = x.shape
    n = dy.shape[1]
    tm, tk, tn = _tile(m, MM_TM), _tile(kdim, MM_TK), _tile(n, 1024)
    return _mm(name, x, dy, grid=(kdim // tk, n // tn, m // tm),
               a_blk=(tm, tk), a_map=lambda i, j, k: (k, i),
               b_blk=(tm, tn), b_map=lambda i, j, k: (k, j),
               o_blk=(tk, tn), o_map=lambda i, j, k: (i, j), out_shape=(kdim, n), out_dtype=BF16, dims=TN)


def _rows(d, tm):
    return pl.BlockSpec((tm, d), lambda i: (i, 0))


def _vec(d):
    return pl.BlockSpec((1, d), lambda i: (0, 0))


def rms_fwd(name, h, gain):
    t, d = h.shape
    tm = _tile(t, ROW_TILE)

    def body(h_ref, g_ref, o_ref):
        x = h_ref[...]
        rstd = lax.rsqrt(jnp.mean(x * x, axis=-1, keepdims=True) + EPS)
        o_ref[...] = (x * rstd * g_ref[...]).astype(BF16)

    return pl.pallas_call(
        body, name=name, grid=(t // tm,), in_specs=[_rows(d, tm), _vec(d)], out_specs=_rows(d, tm),
        out_shape=jax.ShapeDtypeStruct((t, d), BF16), compiler_params=_params(("parallel",)),
    )(h, gain.reshape(1, d))


def rms_bwd(name, h, gain, dy, dres):
    t, d = h.shape
    tm = _tile(t, ROW_TILE)

    def body(h_ref, g_ref, dy_ref, r_ref, dh_ref, dg_ref):
        @pl.when(pl.program_id(0) == 0)
        def _():
            dg_ref[...] = jnp.zeros_like(dg_ref)

        x = h_ref[...]
        rstd = lax.rsqrt(jnp.mean(x * x, axis=-1, keepdims=True) + EPS)
        xhat = x * rstd
        dyv = dy_ref[...]
        dxhat = dyv * g_ref[...]
        dh_ref[...] = r_ref[...] + rstd * (dxhat - xhat * jnp.mean(dxhat * xhat, axis=-1, keepdims=True))
        dg_ref[...] += jnp.sum(dyv * xhat, axis=0, keepdims=True)

    return pl.pallas_call(
        body, name=name, grid=(t // tm,),
        in_specs=[_rows(d, tm), _vec(d), _rows(d, tm), _rows(d, tm)],
        out_specs=[_rows(d, tm), _vec(d)],
        out_shape=[jax.ShapeDtypeStruct((t, d), F32), jax.ShapeDtypeStruct((1, d), F32)],
        compiler_params=_params(("arbitrary",)),
    )(h, gain.reshape(1, d), dy, dres)


def loss_head(name, h, gain, target):
    t, d = h.shape
    tm = _tile(t, ROW_TILE)

    def body(h_ref, g_ref, t_ref, dh_ref, dg_ref, loss_ref):
        @pl.when(pl.program_id(0) == 0)
        def _():
            dg_ref[...] = jnp.zeros_like(dg_ref)
            loss_ref[...] = jnp.zeros_like(loss_ref)

        x = h_ref[...]
        rstd = lax.rsqrt(jnp.mean(x * x, axis=-1, keepdims=True) + EPS)
        xhat = x * rstd
        err = xhat * g_ref[...] - t_ref[...]
        row = jnp.mean(err * err, axis=-1, keepdims=True)
        loss_ref[...] += 0.5 * jnp.sum(row, axis=0, keepdims=True)
        dyv = err * (1.0 / d)
        dxhat = dyv * g_ref[...]
        dh_ref[...] = rstd * (dxhat - xhat * jnp.mean(dxhat * xhat, axis=-1, keepdims=True))
        dg_ref[...] += jnp.sum(dyv * xhat, axis=0, keepdims=True)

    return pl.pallas_call(
        body, name=name, grid=(t // tm,),
        in_specs=[_rows(d, tm), _vec(d), _rows(d, tm)],
        out_specs=[_rows(d, tm), _vec(d), pl.BlockSpec((1, 1), lambda i: (0, 0))],
        out_shape=[jax.ShapeDtypeStruct((t, d), F32), jax.ShapeDtypeStruct((1, d), F32),
                   jax.ShapeDtypeStruct((1, 1), F32)],
        compiler_params=_params(("arbitrary",)),
    )(h, gain.reshape(1, d), target)


def rope_tables(seq):
    pos = jnp.arange(seq, dtype=jnp.int32)
    row_ids = (pos // GRID_W).astype(F32)
    col_ids = (pos % GRID_W).astype(F32)
    quarter = HEAD_DIM // 4
    inv_freq = ROPE_THETA ** (-jnp.arange(quarter, dtype=F32) / quarter)
    ar = row_ids[:, None] * inv_freq[None, :]
    ac = col_ids[:, None] * inv_freq[None, :]
    cos = jnp.concatenate([jnp.cos(ar), jnp.cos(ar), jnp.cos(ac), jnp.cos(ac)], axis=-1)
    sin = jnp.concatenate([-jnp.sin(ar), jnp.sin(ar), -jnp.sin(ac), jnp.sin(ac)], axis=-1)
    return cos, sin


def _swap_quarters(x):
    lane = lax.broadcasted_iota(jnp.int32, x.shape, 1)
    q = HEAD_DIM // 4
    return jnp.where((lane % (2 * q)) < q, pltpu.roll(x, HEAD_DIM - q, 1), pltpu.roll(x, q, 1))


def qk_prep_fwd(name, qkv, q_gain, k_gain, cos, sin, n_q, n_kv):
    t, width = qkv.shape
    tm = _tile(t, ROW_TILE)

    def body(x_ref, qg_ref, kg_ref, c_ref, s_ref, o_ref):
        c, s = c_ref[...], s_ref[...]
        for hd in range(n_q + n_kv):
            sl = slice(hd * HEAD_DIM, (hd + 1) * HEAD_DIM)
            x = x_ref[:, sl]
            g = qg_ref[...] if hd < n_q else kg_ref[...]
            xn = x * lax.rsqrt(jnp.mean(x * x, axis=-1, keepdims=True) + EPS) * g
            o_ref[:, sl] = (xn * c + _swap_quarters(xn) * s).astype(BF16)
        vs = slice((n_q + n_kv) * HEAD_DIM, width)
        o_ref[:, vs] = x_ref[:, vs].astype(BF16)

    return pl.pallas_call(
        body, name=name, grid=(t // tm,),
        in_specs=[_rows(width, tm), _vec(HEAD_DIM), _vec(HEAD_DIM), _rows(HEAD_DIM, tm), _rows(HEAD_DIM, tm)],
        out_specs=_rows(width, tm), out_shape=jax.ShapeDtypeStruct((t, width), BF16),
        compiler_params=_params(("parallel",)),
    )(qkv, q_gain.reshape(1, HEAD_DIM), k_gain.reshape(1, HEAD_DIM), cos, sin)


def qk_prep_bwd(name, qkv, dq, dk, dv, q_gain, k_gain, cos, sin, n_q, n_kv):
    t, width = qkv.shape
    tm = _tile(t, ROW_TILE)

    def body(x_ref, dq_ref, dk_ref, dv_ref, qg_ref, kg_ref, c_ref, s_ref, o_ref, dg_ref):
        @pl.when(pl.program_id(0) == 0)
        def _():
            dg_ref[...] = jnp.zeros_like(dg_ref)

        c, s = c_ref[...], s_ref[...]
        dgq = jnp.zeros((1, HEAD_DIM), F32)
        dgk = jnp.zeros((1, HEAD_DIM), F32)
        for hd in range(n_q + n_kv):
            sl = slice(hd * HEAD_DIM, (hd + 1) * HEAD_DIM)
            x = x_ref[:, sl]
            if hd < n_q:
                g, dout = qg_ref[...], dq_ref[:, sl]
            else:
                ks = slice((hd - n_q) * HEAD_DIM, (hd - n_q + 1) * HEAD_DIM)
                g, dout = kg_ref[...], dk_ref[:, ks]
            rstd = lax.rsqrt(jnp.mean(x * x, axis=-1, keepdims=True) + EPS)
            xhat = x * rstd
            dxn = dout * c + _swap_quarters(dout * s)
            part = jnp.sum(dxn * xhat, axis=0, keepdims=True)
            if hd < n_q:
                dgq = dgq + part
            else:
                dgk = dgk + part
            dxhat = dxn * g
            o_ref[:, sl] = (rstd * (dxhat - xhat * jnp.mean(dxhat * xhat, axis=-1, keepdims=True))).astype(BF16)
        o_ref[:, slice((n_q + n_kv) * HEAD_DIM, width)] = dv_ref[...].astype(BF16)
        dg_ref[0:1, :] += dgq
        dg_ref[1:2, :] += dgk

    kvw = n_kv * HEAD_DIM
    return pl.pallas_call(
        body, name=name, grid=(t // tm,),
        in_specs=[_rows(width, tm), _rows(n_q * HEAD_DIM, tm), _rows(kvw, tm), _rows(kvw, tm),
                  _vec(HEAD_DIM), _vec(HEAD_DIM), _rows(HEAD_DIM, tm), _rows(HEAD_DIM, tm)],
        out_specs=[_rows(width, tm), pl.BlockSpec((2, HEAD_DIM), lambda i: (0, 0))],
        out_shape=[jax.ShapeDtypeStruct((t, width), BF16), jax.ShapeDtypeStruct((2, HEAD_DIM), F32)],
        compiler_params=_params(("arbitrary",)),
    )(qkv, dq, dk, dv, q_gain.reshape(1, HEAD_DIM), k_gain.reshape(1, HEAD_DIM), cos, sin)


def _lanes(x, width):
    return jnp.tile(x, (1, width // LANES))


def _hs(hd):
    return slice(hd * HEAD_DIM, (hd + 1) * HEAD_DIM)


def attn_fwd(name, q, k, v, bias, *, grid, q_spec, k_spec, v_spec, b_spec, o_spec, valid, nh, shared_kv,
             bq, bk, o_shape, o_dtype):
    ns = grid[2]
    scale = HEAD_DIM ** -0.5

    def body(*refs):
        if bias is None:
            q_ref, k_ref, v_ref, o_ref, lse_ref, m_s, l_s, acc_s = refs
            b_ref = None
        else:
            q_ref, k_ref, v_ref, b_ref, o_ref, lse_ref, m_s, l_s, acc_s = refs
        step = pl.program_id(2)

        @pl.when(step == 0)
        def _():
            m_s[...] = jnp.full_like(m_s, -jnp.inf)
            l_s[...] = jnp.zeros_like(l_s)
            acc_s[...] = jnp.zeros_like(acc_s)

        @pl.when(valid(pl.program_id(1), step))
        def _():
            for hd in range(nh):
                kh = _hs(0 if shared_kv else hd)
                s = _dot(q_ref[:, _hs(hd)], k_ref[:, kh], NT) * scale
                if b_ref is not None:
                    s = s + b_ref[hd]
                m_prev = m_s[hd]
                m_new = jnp.maximum(m_prev, jnp.max(s, axis=-1, keepdims=True))
                alpha = jnp.exp(m_prev - m_new)
                p = jnp.exp(s - _lanes(m_new, bk))
                l_s[hd] = alpha * l_s[hd] + jnp.sum(p, axis=-1, keepdims=True)
                acc_s[hd] = alpha * acc_s[hd] + _dot(p.astype(BF16), v_ref[:, kh], NN)
                m_s[hd] = m_new

        @pl.when(step == ns - 1)
        def _():
            for hd in range(nh):
                o_ref[:, _hs(hd)] = (acc_s[hd] / l_s[hd]).astype(o_dtype)
                lse_ref[:, _hs(hd)] = m_s[hd] + jnp.log(l_s[hd])

    in_specs = [q_spec, k_spec, v_spec] + ([] if bias is None else [b_spec])
    args = [q, k, v] + ([] if bias is None else [bias])
    stat = pltpu.VMEM((nh, bq, LANES), F32)
    return pl.pallas_call(
        body, name=name, grid=grid, in_specs=in_specs, out_specs=[o_spec, o_spec],
        out_shape=[jax.ShapeDtypeStruct(o_shape, o_dtype), jax.ShapeDtypeStruct(o_shape, F32)],
        scratch_shapes=[stat, stat, stat],
        compiler_params=_params(("parallel", "parallel", "arbitrary")),
    )(*args)


def _probs(q_ref, k_ref, v_ref, do_ref, lse_ref, dlt_ref, b, hd, kh, bk, scale):
    s = _dot(q_ref[:, _hs(hd)], k_ref[:, kh], NT) * scale
    if b is not None:
        s = s + b
    p = jnp.exp(s - _lanes(lse_ref[:, _hs(hd)], bk))
    dp = _dot(do_ref[:, _hs(hd)], v_ref[:, kh], NT)
    ds = p * (dp - _lanes(dlt_ref[:, _hs(hd)], bk))
    return p, ds


def attn_bwd_dq(name, q, k, v, do, lse, dlt, *, grid, q_spec, k_spec, v_spec, nh, bq, bk, o_shape):
    ns = grid[2]
    scale = HEAD_DIM ** -0.5

    def body(q_ref, k_ref, v_ref, do_ref, lse_ref, dlt_ref, dq_ref, acc_s):
        step = pl.program_id(2)

        @pl.when(step == 0)
        def _():
            acc_s[...] = jnp.zeros_like(acc_s)

        for hd in range(nh):
            _, ds = _probs(q_ref, k_ref, v_ref, do_ref, lse_ref, dlt_ref, None, hd, _hs(0), bk, scale)
            acc_s[hd] += _dot((ds * scale).astype(BF16), k_ref[:, _hs(0)], NN)

        @pl.when(step == ns - 1)
        def _():
            for hd in range(nh):
                dq_ref[:, _hs(hd)] = acc_s[hd]

    return pl.pallas_call(
        body, name=name, grid=grid, in_specs=[q_spec, k_spec, v_spec, q_spec, q_spec, q_spec],
        out_specs=q_spec, out_shape=jax.ShapeDtypeStruct(o_shape, F32),
        scratch_shapes=[pltpu.VMEM((nh, bq, LANES), F32)],
        compiler_params=_params(("parallel", "parallel", "arbitrary")),
    )(q, k, v, do, lse, dlt)


def _always(i, s):
    return s >= 0


def row_delta(name, do, o, n_heads):
    t, width = do.shape
    tm = _tile(t, ROW_TILE)

    def body(do_ref, o_ref, dl_ref, dob_ref):
        for hd in range(n_heads):
            d = do_ref[:, _hs(hd)]
            s = jnp.sum(d * o_ref[:, _hs(hd)].astype(F32), axis=-1, keepdims=True)
            dl_ref[:, _hs(hd)] = jnp.broadcast_to(s, (tm, HEAD_DIM))
            dob_ref[:, _hs(hd)] = d.astype(BF16)

    return pl.pallas_call(
        body, name=name, grid=(t // tm,), in_specs=[_rows(width, tm), _rows(width, tm)],
        out_specs=[_rows(width, tm), _rows(width, tm)],
        out_shape=[jax.ShapeDtypeStruct((t, width), F32), jax.ShapeDtypeStruct((t, width), BF16)],
        compiler_params=_params(("parallel",)),
    )(do, o)


def _a_specs(n_q, n_kv, bq, bk, q_major):
    grp = n_q // n_kv
    if q_major:
        qm, km = (lambda b, i, s: (i, b)), (lambda b, i, s: (s, n_q + b))
        vm = lambda b, i, s: (s, n_q + n_kv + b)
    else:
        qm, km = (lambda b, i, s: (s, b)), (lambda b, i, s: (i, n_q + b))
        vm = lambda b, i, s: (i, n_q + n_kv + b)
    return (pl.BlockSpec((bq, grp * HEAD_DIM), qm), pl.BlockSpec((bk, HEAD_DIM), km),
            pl.BlockSpec((bk, HEAD_DIM), vm))


def mixer_a_fwd(qkv_r, n_q, n_kv):
    t = qkv_r.shape[0]
    bq, bk = _tile(t, A_BQ), _tile(t, A_BK)
    q_spec, k_spec, v_spec = _a_specs(n_q, n_kv, bq, bk, True)
    return attn_fwd("a_attn_fwd", qkv_r, qkv_r, qkv_r, None, grid=(n_kv, t // bq, t // bk),
                    q_spec=q_spec, k_spec=k_spec, v_spec=v_spec, b_spec=None, o_spec=q_spec, valid=_always,
                    nh=n_q // n_kv, shared_kv=True, bq=bq, bk=bk, o_shape=(t, n_q * HEAD_DIM), o_dtype=BF16)


def mixer_a_bwd(qkv_r, do_b, lse, dlt, n_q, n_kv):
    t = qkv_r.shape[0]
    bq, bk = _tile(t, A_BQ), _tile(t, A_BK)
    grp = n_q // n_kv
    q_spec, k_spec, v_spec = _a_specs(n_q, n_kv, bq, bk, True)
    dq = attn_bwd_dq("a_attn_dq", qkv_r, qkv_r, qkv_r, do_b, lse, dlt, grid=(n_kv, t // bq, t // bk),
                     q_spec=q_spec, k_spec=k_spec, v_spec=v_spec, nh=grp, bq=bq, bk=bk,
                     o_shape=(t, n_q * HEAD_DIM))
    q_spec, k_spec, v_spec = _a_specs(n_q, n_kv, bq, bk, False)
    o_spec = pl.BlockSpec((bk, HEAD_DIM), lambda b, i, s: (i, b))
    dk, dv = _attn_bwd_dkv_out(qkv_r, do_b, lse, dlt, grid=(n_kv, t // bk, t // bq), q_spec=q_spec,
                               k_spec=k_spec, v_spec=v_spec, o_spec=o_spec, grp=grp, bq=bq, bk=bk,
                               o_shape=(t, n_kv * HEAD_DIM))
    return dq, dk, dv


def _attn_bwd_dkv_out(qkv_r, do_b, lse, dlt, *, grid, q_spec, k_spec, v_spec, o_spec, grp, bq, bk, o_shape):
    ns = grid[2]
    scale = HEAD_DIM ** -0.5

    def body(q_ref, k_ref, v_ref, do_ref, lse_ref, dlt_ref, dk_ref, dv_ref, dk_s, dv_s):
        step = pl.program_id(2)

        @pl.when(step == 0)
        def _():
            dk_s[...] = jnp.zeros_like(dk_s)
            dv_s[...] = jnp.zeros_like(dv_s)

        for hd in range(grp):
            p, ds = _probs(q_ref, k_ref, v_ref, do_ref, lse_ref, dlt_ref, None, hd, _hs(0), bk, scale)
            dv_s[...] += _dot(p.astype(BF16), do_ref[:, _hs(hd)], TN)
            dk_s[...] += _dot((ds * scale).astype(BF16), q_ref[:, _hs(hd)], TN)

        @pl.when(step == ns - 1)
        def _():
            dk_ref[...] = dk_s[...]
            dv_ref[...] = dv_s[...]

    acc = pltpu.VMEM((bk, HEAD_DIM), F32)
    return pl.pallas_call(
        body, name="a_attn_dkv", grid=grid, in_specs=[q_spec, k_spec, v_spec, q_spec, q_spec, q_spec],
        out_specs=[o_spec, o_spec], out_shape=[jax.ShapeDtypeStruct(o_shape, F32)] * 2,
        scratch_shapes=[acc, acc], compiler_params=_params(("parallel", "parallel", "arbitrary")),
    )(qkv_r, qkv_r, qkv_r, do_b, lse, dlt)


def t5_bucket(rel):
    nb = REL_BUCKETS // 2
    max_exact = nb // 2
    base = jnp.where(rel > 0, nb, 0)
    n = jnp.abs(rel)
    nf = jnp.maximum(n, 1).astype(F32)
    large = max_exact + (jnp.log(nf / max_exact) / math.log(REL_MAX_DISTANCE / max_exact)
                         * (nb - max_exact)).astype(jnp.int32)
    large = jnp.minimum(large, nb - 1)
    return base + jnp.where(n < max_exact, n, large)


def band_tables(rel_bias_g, half_span, dil, bq):
    a = jnp.arange(bq)[:, None]
    b = jnp.arange(bq)[None, :]
    rel = jnp.stack([(s - 1) * bq + b - a for s in range(3)])
    ok = jnp.abs(rel) <= half_span
    bucket = t5_bucket(rel * dil)
    bias = jnp.where(ok[None], rel_bias_g.T[:, bucket], NEG_INF)
    return bias.astype(F32), jnp.where(ok, bucket, -1).astype(jnp.int32)


def _b_geometry(t, dil, g, n_groups):
    hg = B_HEADS_PER_GROUP
    length = t // dil
    bq = _tile(length, B_BQ)
    nblk = length // bq
    gw = hg * HEAD_DIM
    per_tok = 3 * n_groups
    return hg, length, bq, nblk, gw, per_tok


def mixer_b_group_fwd(qkv, bias, dil, g, n_groups):
    t = qkv.shape[0]
    hg, length, bq, nblk, gw, per_tok = _b_geometry(t, dil, g, n_groups)
    view = qkv.reshape(length, dil * qkv.shape[1])
    col = lambda c, which: c * per_tok + 3 * g + which
    kblk = lambda i, s: jnp.clip(i - 1 + s, 0, nblk - 1)
    spec = lambda which, streamed: pl.BlockSpec(
        (bq, gw), (lambda c, i, s: (kblk(i, s), col(c, which))) if streamed else (lambda c, i, s: (i, col(c, which))))
    valid = lambda i, s: (i - 1 + s >= 0) & (i - 1 + s < nblk)
    o, lz = attn_fwd(f"b_attn_fwd_d{dil}", view, view, view, bias, grid=(dil, nblk, 3),
                     q_spec=spec(0, False), k_spec=spec(1, True), v_spec=spec(2, True),
                     b_spec=pl.BlockSpec((hg, None, bq, bq), lambda c, i, s: (0, s, 0, 0)),
                     o_spec=pl.BlockSpec((bq, gw), lambda c, i, s: (i, c)), valid=valid, nh=hg,
                     shared_kv=False, bq=bq, bk=bq, o_shape=(length, dil * gw), o_dtype=F32)
    return o.reshape(t, gw), lz.reshape(t, gw)


def mixer_b_group_bwd(qkv, bias, do_g, lz_g, dlt_g, dil, g, n_groups):
    t = qkv.shape[0]
    hg, length, bq, nblk, gw, per_tok = _b_geometry(t, dil, g, n_groups)
    view = qkv.reshape(length, dil * qkv.shape[1])
    dov, lzv, dlv = (x.reshape(length, dil * gw) for x in (do_g, lz_g, dlt_g))
    col = lambda c, which: c * per_tok + 3 * g + which
    nbr = lambda i, s: jnp.clip(i - 1 + s, 0, nblk - 1)
    valid = lambda i, s: (i - 1 + s >= 0) & (i - 1 + s < nblk)
    q_spec = pl.BlockSpec((bq, gw), lambda c, i, s: (i, col(c, 0)))
    k_spec = pl.BlockSpec((bq, gw), lambda c, i, s: (nbr(i, s), col(c, 1)))
    v_spec = pl.BlockSpec((bq, gw), lambda c, i, s: (nbr(i, s), col(c, 2)))
    stat = pl.BlockSpec((bq, gw), lambda c, i, s: (i, c))
    dq, dbias = _band_bwd_dq(f"b_attn_dq_d{dil}", view, dov, lzv, dlv, bias, grid=(dil, nblk, 3),
                             q_spec=q_spec, k_spec=k_spec, v_spec=v_spec, stat_spec=stat,
                             b_spec=pl.BlockSpec((hg, None, bq, bq), lambda c, i, s: (0, s, 0, 0)),
                             valid=valid, nh=hg, bq=bq, o_shape=(length, dil * gw))
    q_spec = pl.BlockSpec((bq, gw), lambda c, i, s: (nbr(i, s), col(c, 0)))
    k_spec = pl.BlockSpec((bq, gw), lambda c, i, s: (i, col(c, 1)))
    v_spec = pl.BlockSpec((bq, gw), lambda c, i, s: (i, col(c, 2)))
    stat = pl.BlockSpec((bq, gw), lambda c, i, s: (nbr(i, s), c))
    dk, dv = _band_bwd_dkv(f"b_attn_dkv_d{dil}", view, dov, lzv, dlv, bias, grid=(dil, nblk, 3),
                           q_spec=q_spec, k_spec=k_spec, v_spec=v_spec, stat_spec=stat,
                           b_spec=pl.BlockSpec((hg, None, bq, bq), lambda c, i, s: (0, 2 - s, 0, 0)),
                           o_spec=pl.BlockSpec((bq, gw), lambda c, i, s: (i, c)),
                           valid=valid, nh=hg, bq=bq, o_shape=(length, dil * gw))
    return dq.reshape(t, gw), dk.reshape(t, gw), dv.reshape(t, gw), dbias


def _band_bwd_dq(name, view, do, lse, dlt, bias, *, grid, q_spec, k_spec, v_spec, stat_spec, b_spec, valid,
                 nh, bq, o_shape):
    scale = HEAD_DIM ** -0.5
    bias_shape = (nh, 3, bq, bq)

    def body(q_ref, k_ref, v_ref, do_ref, lse_ref, dlt_ref, b_ref, dq_ref, db_ref, acc_s):
        step = pl.program_id(2)

        @pl.when((pl.program_id(0) == 0) & (pl.program_id(1) == 0) & (step == 0))
        def _():
            db_ref[...] = jnp.zeros_like(db_ref)

        @pl.when(step == 0)
        def _():
            acc_s[...] = jnp.zeros_like(acc_s)

        @pl.when(valid(pl.program_id(1), step))
        def _():
            for hd in range(nh):
                _, ds = _probs(q_ref, k_ref, v_ref, do_ref, lse_ref, dlt_ref, b_ref[hd], hd, _hs(hd), bq, scale)
                db_ref[hd, step] += ds
                acc_s[hd] += _dot((ds * scale).astype(BF16), k_ref[:, _hs(hd)], NN)

        @pl.when(step == 2)
        def _():
            for hd in range(nh):
                dq_ref[:, _hs(hd)] = acc_s[hd].astype(BF16)

    return pl.pallas_call(
        body, name=name, grid=grid,
        in_specs=[q_spec, k_spec, v_spec, stat_spec, stat_spec, stat_spec, b_spec],
        out_specs=[stat_spec, pl.BlockSpec(bias_shape, lambda c, i, s: (0, 0, 0, 0))],
        out_shape=[jax.ShapeDtypeStruct(o_shape, BF16), jax.ShapeDtypeStruct(bias_shape, F32)],
        scratch_shapes=[pltpu.VMEM((nh, bq, LANES), F32)], compiler_params=_params(("arbitrary",) * 3),
    )(view, view, view, do, lse, dlt, bias)


def _band_bwd_dkv(name, view, do, lse, dlt, bias, *, grid, q_spec, k_spec, v_spec, stat_spec, b_spec, o_spec,
                  valid, nh, bq, o_shape):
    scale = HEAD_DIM ** -0.5

    def body(q_ref, k_ref, v_ref, do_ref, lse_ref, dlt_ref, b_ref, dk_ref, dv_ref, dk_s, dv_s):
        step = pl.program_id(2)

        @pl.when(step == 0)
        def _():
            dk_s[...] = jnp.zeros_like(dk_s)
            dv_s[...] = jnp.zeros_like(dv_s)

        @pl.when(valid(pl.program_id(1), step))
        def _():
            for hd in range(nh):
                p, ds = _probs(q_ref, k_ref, v_ref, do_ref, lse_ref, dlt_ref, b_ref[hd], hd, _hs(hd), bq, scale)
                dv_s[hd] += _dot(p.astype(BF16), do_ref[:, _hs(hd)], TN)
                dk_s[hd] += _dot((ds * scale).astype(BF16), q_ref[:, _hs(hd)], TN)

        @pl.when(step == 2)
        def _():
            for hd in range(nh):
                dk_ref[:, _hs(hd)] = dk_s[hd].astype(BF16)
                dv_ref[:, _hs(hd)] = dv_s[hd].astype(BF16)

    acc = pltpu.VMEM((nh, bq, LANES), F32)
    return pl.pallas_call(
        body, name=name, grid=grid,
        in_specs=[q_spec, k_spec, v_spec, stat_spec, stat_spec, stat_spec, b_spec],
        out_specs=[o_spec, o_spec], out_shape=[jax.ShapeDtypeStruct(o_shape, BF16)] * 2,
        scratch_shapes=[acc, acc], compiler_params=_params(("parallel", "parallel", "arbitrary")),
    )(view, view, view, do, lse, dlt, bias)


def bias_bucket_sums(name, dbias, bucket):
    nh, _, bq, _ = dbias.shape
    db2 = dbias.reshape(nh, 3 * bq, bq)
    bk2 = bucket.reshape(3 * bq, bq)

    def body(db_ref, bk_ref, o_ref):
        row = lax.broadcasted_iota(jnp.int32, (nh, LANES), 0)
        lane = lax.broadcasted_iota(jnp.int32, (nh, LANES), 1)
        out = jnp.zeros((nh, LANES), F32)
        bkt = bk_ref[...]
        for hd in range(nh):
            x = db_ref[hd]
            for r in range(REL_BUCKETS):
                part = jnp.sum(jnp.where(bkt == r, x, 0.0), axis=1, keepdims=True)
                tot = jnp.sum(part, axis=0, keepdims=True)
                out = out + jnp.where((row == hd) & (lane == r), tot, 0.0)
        o_ref[...] = out

    return pl.pallas_call(
        body, name=name, out_shape=jax.ShapeDtypeStruct((nh, LANES), F32),
        compiler_params=pltpu.CompilerParams(vmem_limit_bytes=VMEM_LIMIT),
    )(db2, bk2)


def combine_fwd(name, outs, lzs):
    n_g = len(outs)
    t, gw = outs[0].shape
    tm = _tile(t, ROW_TILE)

    def body(*refs):
        o_refs, lz_refs, y_ref = refs[:n_g], refs[n_g:2 * n_g], refs[2 * n_g]
        lz = [r[...] for r in lz_refs]
        mx = functools.reduce(jnp.maximum, lz)
        e = [jnp.exp(x - mx) for x in lz]
        den = functools.reduce(lambda a, b: a + b, e)
        for g in range(n_g):
            y_ref[:, g * gw:(g + 1) * gw] = (e[g] / den * o_refs[g][...]).astype(BF16)

    return pl.pallas_call(
        body, name=name, grid=(t // tm,), in_specs=[_rows(gw, tm)] * (2 * n_g), out_specs=_rows(n_g * gw, tm),
        out_shape=jax.ShapeDtypeStruct((t, n_g * gw), BF16), compiler_params=_params(("parallel",)),
    )(*outs, *lzs)


def combine_bwd(name, dy, outs, lzs):
    n_g = len(outs)
    t, gw = outs[0].shape
    tm = _tile(t, ROW_TILE)
    nh = gw // HEAD_DIM

    def body(*refs):
        dy_ref = refs[0]
        o_refs, lz_refs = refs[1:1 + n_g], refs[1 + n_g:1 + 2 * n_g]
        do_refs, dl_refs = refs[1 + 2 * n_g:1 + 3 * n_g], refs[1 + 3 * n_g:]
        lz = [r[...] for r in lz_refs]
        mx = functools.reduce(jnp.maximum, lz)
        e = [jnp.exp(x - mx) for x in lz]
        den = functools.reduce(lambda a, b: a + b, e)
        wts = [x / den for x in e]
        for g in range(n_g):
            do_refs[g][...] = (wts[g] * dy_ref[:, g * gw:(g + 1) * gw]).astype(BF16)
        for hd in range(nh):
            mix = jnp.zeros((tm, HEAD_DIM), F32)
            for g in range(n_g):
                prod = dy_ref[:, g * gw + hd * HEAD_DIM:g * gw + (hd + 1) * HEAD_DIM] * o_refs[g][:, _hs(hd)]
                dw = jnp.broadcast_to(jnp.sum(prod, axis=-1, keepdims=True), (tm, HEAD_DIM))
                mix = mix + wts[g][:, _hs(hd)] * dw
            for g in range(n_g):
                dl_refs[g][:, _hs(hd)] = wts[g][:, _hs(hd)] * mix

    return pl.pallas_call(
        body, name=name, grid=(t // tm,),
        in_specs=[_rows(n_g * gw, tm)] + [_rows(gw, tm)] * (2 * n_g),
        out_specs=[_rows(gw, tm)] * (2 * n_g),
        out_shape=[jax.ShapeDtypeStruct((t, gw), BF16)] * n_g + [jax.ShapeDtypeStruct((t, gw), F32)] * n_g,
        compiler_params=_params(("parallel",)),
    )(dy, *outs, *lzs)


def _conv3(u, w_ref, b):
    t = u.shape[0]
    row = lax.broadcasted_iota(jnp.int32, u.shape, 0)
    prev = jnp.where(row == 0, 0.0, pltpu.roll(u, 1, 0))
    nxt = jnp.where(row == t - 1, 0.0, pltpu.roll(u, t - 1, 0))
    out = w_ref[0:1, :] * prev + w_ref[1:2, :] * u + w_ref[2:3, :] * nxt
    return out if b is None else out + b


def _conv3_t(d, w_ref):
    t = d.shape[0]
    row = lax.broadcasted_iota(jnp.int32, d.shape, 0)
    prev = jnp.where(row == 0, 0.0, pltpu.roll(d, 1, 0))
    nxt = jnp.where(row == t - 1, 0.0, pltpu.roll(d, t - 1, 0))
    return w_ref[0:1, :] * nxt + w_ref[1:2, :] * d + w_ref[2:3, :] * prev


def conv_act_fwd(name, u2, cw2, cb2):
    _, t, dff = u2.shape
    tn = LANES

    def body(u_ref, w_ref, b_ref, o_ref):
        cg = _conv3(u_ref[0], w_ref.at[0], b_ref[0])
        cv = _conv3(u_ref[1], w_ref.at[1], b_ref[1])
        o_ref[...] = (cg * jax.nn.sigmoid(cg) * cv).astype(BF16)

    return pl.pallas_call(
        body, name=name, grid=(dff // tn,),
        in_specs=[pl.BlockSpec((2, t, tn), lambda j: (0, 0, j)), pl.BlockSpec((2, 3, tn), lambda j: (0, 0, j)),
                  pl.BlockSpec((2, 1, tn), lambda j: (0, 0, j))],
        out_specs=pl.BlockSpec((t, tn), lambda j: (0, j)), out_shape=jax.ShapeDtypeStruct((t, dff), BF16),
        compiler_params=_params(("parallel",)),
    )(u2, cw2, cb2)


def conv_act_bwd(name, u2, cw2, cb2, dact):
    _, t, dff = u2.shape
    tn = LANES

    def body(u_ref, w_ref, b_ref, d_ref, du_ref, dw_ref):
        d = d_ref[...]
        ug, uv = u_ref[0], u_ref[1]
        cg = _conv3(ug, w_ref.at[0], b_ref[0])
        cv = _conv3(uv, w_ref.at[1], b_ref[1])
        sg = jax.nn.sigmoid(cg)
        dcv = d * (cg * sg)
        dcg = d * cv * (sg * (1.0 + cg * (1.0 - sg)))
        du_ref[0] = _conv3_t(dcg, w_ref.at[0]).astype(BF16)
        du_ref[1] = _conv3_t(dcv, w_ref.at[1]).astype(BF16)
        row = lax.broadcasted_iota(jnp.int32, ug.shape, 0)
        for half, (dc, u) in enumerate(((dcg, ug), (dcv, uv))):
            prev = jnp.where(row == 0, 0.0, pltpu.roll(u, 1, 0))
            nxt = jnp.where(row == t - 1, 0.0, pltpu.roll(u, t - 1, 0))
            for tap, x in enumerate((prev, u, nxt)):
                dw_ref[half, tap:tap + 1, :] = jnp.sum(dc * x, axis=0, keepdims=True)
            dw_ref[half, 3:4, :] = jnp.sum(dc, axis=0, keepdims=True)
            dw_ref[half, 4:8, :] = jnp.zeros((4, tn), F32)

    return pl.pallas_call(
        body, name=name, grid=(dff // tn,),
        in_specs=[pl.BlockSpec((2, t, tn), lambda j: (0, 0, j)), pl.BlockSpec((2, 3, tn), lambda j: (0, 0, j)),
                  pl.BlockSpec((2, 1, tn), lambda j: (0, 0, j)), pl.BlockSpec((t, tn), lambda j: (0, j))],
        out_specs=[pl.BlockSpec((2, t, tn), lambda j: (0, 0, j)), pl.BlockSpec((2, 8, tn), lambda j: (0, 0, j))],
        out_shape=[jax.ShapeDtypeStruct((2, t, dff), BF16), jax.ShapeDtypeStruct((2, 8, dff), F32)],
        compiler_params=_params(("parallel",)),
    )(u2, cw2, cb2, dact)


def _place():
    x, y, c = lax.axis_index("x"), lax.axis_index("y"), lax.axis_index("c")
    chips = [(1 - x, y), (x, 1 - y), (1 - x, 1 - y)]
    return x, y, c, chips


ANY = pl.BlockSpec(memory_space=pl.ANY)


def all_gather(name, shards, axes):
    n = len(shards)

    def body(*refs):
        src, out = refs[:n], refs[n:2 * n]
        send, recv, loc = refs[2 * n:]
        x, y, c, chips = _place()
        sibling = (x, y, 1 - c)

        def slot(a, px, py, pc):
            idx = 4 * px + 2 * py + pc
            return out[a].at[idx] if axes[a] == 0 else out[a].at[:, idx]

        def copy(a, k, block, to, from_src=False):
            return pltpu.make_async_remote_copy(
                src_ref=src[a] if from_src else slot(a, *block), dst_ref=slot(a, *block),
                send_sem=send.at[a, k], recv_sem=recv.at[a, k], device_id=to, device_id_type=MESH)

        mine = [pltpu.make_async_copy(src[a], slot(a, x, y, c), loc.at[a]) for a in range(n)]
        for cp in mine:
            cp.start()
        first = []
        for a in range(n):
            first.append(copy(a, 0, (x, y, c), sibling, True))
            first += [copy(a, 1 + j, (x, y, c), (*chip, c), True) for j, chip in enumerate(chips)]
        for cp in first:
            cp.start()
        passed = []
        for j, chip in enumerate(chips):
            for a in range(n):
                copy(a, 1 + j, (*chip, c), (x, y, c)).wait_recv()
                cp = copy(a, 4 + j, (*chip, c), sibling)
                cp.start()
                passed.append(cp)
        for a in range(n):
            copy(a, 0, sibling, (x, y, c)).wait_recv()
            for j, chip in enumerate(chips):
                copy(a, 4 + j, (*chip, 1 - c), (x, y, c)).wait_recv()
        for cp in first + passed:
            cp.wait_send()
        for cp in mine:
            cp.wait()

    def gshape(s, ax):
        return s.shape[:ax] + (N_DEV,) + s.shape[ax:]

    return pl.pallas_call(
        body, name=name, in_specs=[ANY] * n, out_specs=[ANY] * n,
        out_shape=[jax.ShapeDtypeStruct(gshape(s, ax), s.dtype) for s, ax in zip(shards, axes)],
        scratch_shapes=[pltpu.SemaphoreType.DMA((n, 7)), pltpu.SemaphoreType.DMA((n, 7)),
                        pltpu.SemaphoreType.DMA((n,))],
    )(*shards)


def scatter_to_sibling(name, grads):
    n = len(grads)

    def body(*refs):
        src, own, got = refs[:n], refs[n:2 * n], refs[2 * n:3 * n]
        send, recv, loc = refs[3 * n:]
        x, y, c, _ = _place()
        sibling = (x, y, 1 - c)
        local, remote = [], []
        for a in range(n):
            for q in range(4):
                local.append(pltpu.make_async_copy(src[a].at[:, 2 * q + c], own[a].at[:, q], loc.at[a, q]))
                remote.append(pltpu.make_async_remote_copy(
                    src_ref=src[a].at[:, 2 * q + 1 - c], dst_ref=got[a].at[:, q], send_sem=send.at[a, q],
                    recv_sem=recv.at[a, q], device_id=sibling, device_id_type=MESH))
        for cp in local + remote:
            cp.start()
        for cp in remote:
            cp.wait()
        for cp in local:
            cp.wait()

    def half(g):
        return jax.ShapeDtypeStruct(g.shape[:1] + (4,) + g.shape[2:], g.dtype)

    res = pl.pallas_call(
        body, name=name, in_specs=[ANY] * n, out_specs=[ANY] * (2 * n),
        out_shape=[half(g) for g in grads] * 2,
        scratch_shapes=[pltpu.SemaphoreType.DMA((n, 4)), pltpu.SemaphoreType.DMA((n, 4)),
                        pltpu.SemaphoreType.DMA((n, 4))],
    )(*grads)
    return res[:n], res[n:]


def scatter_to_chips(name, parts):
    n = len(parts)

    def body(*refs):
        src, own, got = refs[:n], refs[n:2 * n], refs[2 * n:3 * n]
        send, recv, loc = refs[3 * n:]
        x, y, c, chips = _place()
        local, remote = [], []
        for a in range(n):
            local.append(pltpu.make_async_copy(src[a].at[:, 2 * x + y], own[a], loc.at[a]))
            for j, (px, py) in enumerate(chips):
                remote.append(pltpu.make_async_remote_copy(
                    src_ref=src[a].at[:, 2 * px + py], dst_ref=got[a].at[:, j], send_sem=send.at[a, j],
                    recv_sem=recv.at[a, j], device_id=(px, py, c), device_id_type=MESH))
        for cp in local + remote:
            cp.start()
        for cp in remote:
            cp.wait()
        for cp in local:
            cp.wait()

    res = pl.pallas_call(
        body, name=name, in_specs=[ANY] * n, out_specs=[ANY] * (2 * n),
        out_shape=[jax.ShapeDtypeStruct(p.shape[:1] + p.shape[2:], p.dtype) for p in parts]
        + [jax.ShapeDtypeStruct(p.shape[:1] + (3,) + p.shape[2:], p.dtype) for p in parts],
        scratch_shapes=[pltpu.SemaphoreType.DMA((n, 3)), pltpu.SemaphoreType.DMA((n, 3)),
                        pltpu.SemaphoreType.DMA((n,))],
    )(*parts)
    return res[:n], res[n:]


def all_reduce_small(name, vec):
    rows, m = vec.shape

    def body(x_ref, o_ref, buf, send, recv):
        x, y, c, chips = _place()
        sibling = (x, y, 1 - c)

        def blk(px, py, pc):
            return buf.at[pl.ds(pl.multiple_of((4 * px + 2 * py + pc) * rows, rows), rows), :]

        def copy(k, block, to):
            return pltpu.make_async_remote_copy(src_ref=blk(*block), dst_ref=blk(*block), send_sem=send.at[k],
                                                recv_sem=recv.at[k], device_id=to, device_id_type=MESH)

        blk(x, y, c)[...] = x_ref[...]
        first = [copy(0, (x, y, c), sibling)] + [copy(1 + j, (x, y, c), (*chip, c)) for j, chip in enumerate(chips)]
        for cp in first:
            cp.start()
        passed = [copy(4 + j, (*chip, c), sibling) for j, chip in enumerate(chips)]
        for j, chip in enumerate(chips):
            copy(1 + j, (*chip, c), (x, y, c)).wait_recv()
            passed[j].start()
        copy(0, sibling, (x, y, c)).wait_recv()
        for j, chip in enumerate(chips):
            copy(4 + j, (*chip, 1 - c), (x, y, c)).wait_recv()
        for cp in first + passed:
            cp.wait_send()
        tot = buf[0:rows, :]
        for dev in range(1, N_DEV):
            tot = tot + buf[dev * rows:(dev + 1) * rows, :]
        o_ref[...] = tot

    return pl.pallas_call(
        body, name=name, in_specs=[pl.BlockSpec(memory_space=pltpu.VMEM)],
        out_specs=pl.BlockSpec(memory_space=pltpu.VMEM), out_shape=jax.ShapeDtypeStruct((rows, m), F32),
        scratch_shapes=[pltpu.VMEM((N_DEV * rows, m), F32), pltpu.SemaphoreType.DMA((7,)),
                        pltpu.SemaphoreType.DMA((7,))],
        compiler_params=pltpu.CompilerParams(vmem_limit_bytes=VMEM_LIMIT),
    )(vec)


def _as2d(a):
    return a.reshape(-1, a.shape[-1])


def _ew_tiles(rows, cols):
    tr = rows
    for cand in (512, 256, 128, 64, 32, 16):
        if rows % cand == 0 and cand * cols * 4 <= (1 << 20):
            tr = cand
            break
    return tr


def add_pair(name, a, b):
    a2, b2 = _as2d(a), _as2d(b)
    rows, cols = a2.shape
    tr = _ew_tiles(rows, cols)
    spec = pl.BlockSpec((tr, cols), lambda i: (i, 0))

    def body(a_ref, b_ref, o_ref):
        o_ref[...] = (a_ref[...].astype(F32) + b_ref[...].astype(F32)).astype(BF16)

    out = pl.pallas_call(
        body, name=name, grid=(rows // tr,), in_specs=[spec, spec], out_specs=spec,
        out_shape=jax.ShapeDtypeStruct((rows, cols), BF16), compiler_params=_params(("parallel",)),
    )(a2, b2)
    return out.reshape(a.shape)


def _adamw_math(w, g, m, v):
    m = ADAM_B1 * m + (1.0 - ADAM_B1) * g
    v = ADAM_B2 * v + (1.0 - ADAM_B2) * (g * g)
    m_hat = m / (1.0 - ADAM_B1 ** ADAM_STEP)
    v_hat = v / (1.0 - ADAM_B2 ** ADAM_STEP)
    delta = -ADAM_LR * (m_hat / (jnp.sqrt(v_hat) + ADAM_EPS) + ADAM_WD * w)
    return delta, m, v


def adamw_sharded(name, own, got, w, m, v):
    lyr, kdim, ncol = own.shape
    tr = _ew_tiles(kdim, ncol)
    spec = pl.BlockSpec((None, tr, ncol), lambda l, i: (l, i, 0))
    gspec = pl.BlockSpec((None, 3, tr, ncol), lambda l, i: (l, 0, i, 0))

    def body(o_ref, g_ref, w_ref, m_ref, v_ref, go_ref, d_ref, mo_ref, vo_ref):
        g = o_ref[...].astype(F32)
        for j in range(3):
            g = g + g_ref[j].astype(F32)
        d, mn, vn = _adamw_math(w_ref[...], g, m_ref[...], v_ref[...])
        go_ref[...] = g
        d_ref[...] = d
        mo_ref[...] = mn
        vo_ref[...] = vn

    return pl.pallas_call(
        body, name=name, grid=(lyr, kdim // tr), in_specs=[spec, gspec, spec, spec, spec], out_specs=[spec] * 4,
        out_shape=[jax.ShapeDtypeStruct((lyr, kdim, ncol), F32)] * 4,
        compiler_params=_params(("parallel", "parallel")),
    )(own, got, w, m, v)


def adamw_small(name, g, w, m, v):
    def body(g_ref, w_ref, m_ref, v_ref, d_ref, mo_ref, vo_ref):
        d, mn, vn = _adamw_math(w_ref[...], g_ref[...], m_ref[...], v_ref[...])
        d_ref[...] = d
        mo_ref[...] = mn
        vo_ref[...] = vn

    vm = pl.BlockSpec(memory_space=pltpu.VMEM)
    return pl.pallas_call(
        body, name=name, in_specs=[vm] * 4, out_specs=[vm] * 3,
        out_shape=[jax.ShapeDtypeStruct(g.shape, F32)] * 3,
        compiler_params=pltpu.CompilerParams(vmem_limit_bytes=VMEM_LIMIT),
    )(g, w, m, v)


def _pack(parts, width):
    flat = jnp.concatenate([p.reshape(-1).astype(F32) for p in parts])
    pad = (-flat.shape[0]) % width
    return jnp.pad(flat, (0, pad)).reshape(-1, width) if pad else flat.reshape(-1, width)


def _unpack(packed, shapes):
    flat = packed.reshape(-1)
    out, off = [], 0
    for s in shapes:
        size = math.prod(s)
        out.append(flat[off:off + size].reshape(s))
        off += size
    return out


def _local_step(h, target, gathered, params):
    ga_qkv, ga_o, gb_qkv, gb_o, g_up, g_down, cw_full = gathered
    a_q_gain, a_k_gain, rel_bias, mix_norm, ffn_norm, conv_b, final_norm = params
    t, d = h.shape
    depth = mix_norm.shape[0]
    n_groups = len(B_GROUPS)
    hg = B_HEADS_PER_GROUP
    n_kv = A_KV_HEADS
    w_a, w_b, w_u = ga_qkv.shape[3], gb_qkv.shape[3], g_up.shape[3]
    n_q = w_a * N_DEV // HEAD_DIM - 2 * n_kv
    dff = g_down.shape[1]
    n_a, n_b = ga_qkv.shape[1], gb_qkv.shape[1]
    cb_full = conv_b.reshape(depth, 2, 1, dff)

    cos, sin = rope_tables(t)
    tables = [band_tables(rel_bias[:, g * hg:(g + 1) * hg], win // (2 * dil), dil, _tile(t // dil, B_BQ))
              for g, (win, dil) in enumerate(B_GROUPS)]

    saved = []
    for i in range(depth):
        j = i // 2
        s = {"h_in": h}
        hn = rms_fwd("mix_norm_fwd", h, mix_norm[i])
        s["hn"] = hn
        if i % 2 == 0:
            qkv = mm_col_fwd("a_qkv_fwd", hn, ga_qkv, j, F32)
            qkv_r = qk_prep_fwd("a_qk_prep_fwd", qkv, a_q_gain[j], a_k_gain[j], cos, sin, n_q, n_kv)
            o, lse = mixer_a_fwd(qkv_r, n_q, n_kv)
            s.update(qkv=qkv, qkv_r=qkv_r, o=o, lse=lse)
            h = mm_row_fwd("a_out_fwd", o, ga_o, j, h)
        else:
            qkv = mm_col_fwd("b_qkv_fwd", hn, gb_qkv, j, BF16)
            outs, lzs = [], []
            for g, (win, dil) in enumerate(B_GROUPS):
                o_g, lz_g = mixer_b_group_fwd(qkv, tables[g][0], dil, g, n_groups)
                outs.append(o_g)
                lzs.append(lz_g)
            y = combine_fwd("b_combine_fwd", outs, lzs)
            s.update(qkv=qkv, outs=outs, lzs=lzs, y=y)
            h = mm_row_fwd("b_out_fwd", y, gb_o, j, h)
        s["h_mid"] = h
        hn2 = rms_fwd("ffn_norm_fwd", h, ffn_norm[i])
        u2 = mm_col_fwd("ffn_up_fwd", hn2, g_up, i, F32, split=2)
        act = conv_act_fwd("ffn_conv_act_fwd", u2, cw_full[i], cb_full[i])
        s.update(hn2=hn2, u2=u2, act=act)
        h = mm_row_fwd("ffn_down_fwd", act, g_down, i, h)
        saved.append(s)

    dh, d_final, loss_part = loss_head("loss_head", h, final_norm, target)

    d_mix, d_ffn, d_cw, d_cb = [None] * depth, [None] * depth, [None] * depth, [None] * depth
    d_qg, d_kg = [None] * n_a, [None] * n_a
    d_rel = jnp.zeros((n_groups * hg, LANES), F32)
    gw_up, gw_down = [None] * depth, [None] * depth
    ga_qkv_d, ga_o_d, gb_qkv_d, gb_o_d = {}, {}, {}, {}
    for i in reversed(range(depth)):
        j = i // 2
        s = saved[i]
        dact = mm_row_dx("ffn_down_dx", dh, g_down, i)
        gw_down[i] = mm_row_dw("ffn_down_dw", s["act"], dh)
        du2, dcw = conv_act_bwd("ffn_conv_act_bwd", s["u2"], cw_full[i], cb_full[i], dact)
        d_cw[i] = dcw[:, 0:3, :].transpose(1, 0, 2).reshape(3, 2 * dff)
        d_cb[i] = dcw[:, 3, :].reshape(2 * dff)
        gw_up[i] = mm_col_dw("ffn_up_dw", s["hn2"], du2, w_u, split=2)
        dhn2 = mm_col_dx("ffn_up_dx", du2, g_up, i, split=2)
        dh, d_ffn[i] = rms_bwd("ffn_norm_bwd", s["h_mid"], ffn_norm[i], dhn2, dh)
        if i % 2 == 0:
            do = mm_row_dx("a_out_dx", dh, ga_o, j)
            ga_o_d[j] = mm_row_dw("a_out_dw", s["o"], dh)
            dlt, do_b = row_delta("a_delta", do, s["o"], n_q)
            dq, dk, dv = mixer_a_bwd(s["qkv_r"], do_b, s["lse"], dlt, n_q, n_kv)
            dqkv, dgain = qk_prep_bwd("a_qk_prep_bwd", s["qkv"], dq, dk, dv, a_q_gain[j], a_k_gain[j], cos, sin,
                                      n_q, n_kv)
            d_qg[j], d_kg[j] = dgain[0], dgain[1]
            ga_qkv_d[j] = mm_col_dw("a_qkv_dw", s["hn"], dqkv, w_a)
            dhn = mm_col_dx("a_qkv_dx", dqkv, ga_qkv, j)
        else:
            dy = mm_row_dx("b_out_dx", dh, gb_o, j)
            gb_o_d[j] = mm_row_dw("b_out_dw", s["y"], dh)
            res = combine_bwd("b_combine_bwd", dy, s["outs"], s["lzs"])
            dos, dlts = res[:n_groups], res[n_groups:]
            pieces, rel_rows = [], []
            for g, (win, dil) in enumerate(B_GROUPS):
                dq, dk, dv, dbias = mixer_b_group_bwd(s["qkv"], tables[g][0], dos[g], s["lzs"][g], dlts[g], dil, g,
                                                      n_groups)
                pieces += [dq, dk, dv]
                rel_rows.append(bias_bucket_sums(f"b_bias_sums_d{dil}", dbias, tables[g][1]))
            d_rel = d_rel + jnp.concatenate(rel_rows, axis=0)
            dqkv = jnp.concatenate(pieces, axis=1)
            gb_qkv_d[j] = mm_col_dw("b_qkv_dw", s["hn"], dqkv, w_b)
            dhn = mm_col_dx("b_qkv_dx", dqkv, gb_qkv, j)
        dh, d_mix[i] = rms_bwd("mix_norm_bwd", s["h_in"], mix_norm[i], dhn, dh)

    full = [
        jnp.stack([ga_qkv_d[l] for l in range(n_a)]),
        jnp.stack([ga_o_d[l].reshape(N_DEV, -1, d) for l in range(n_a)]),
        jnp.stack([gb_qkv_d[l] for l in range(n_b)]),
        jnp.stack([gb_o_d[l].reshape(N_DEV, -1, d) for l in range(n_b)]),
        jnp.stack(gw_up),
        jnp.stack([g.reshape(N_DEV, -1, d) for g in gw_down]),
    ]
    d_rel_bias = d_rel[:, :REL_BUCKETS].T
    small_g = [jnp.stack(d_qg), jnp.stack(d_kg), d_rel_bias, jnp.concatenate(d_mix, 0), jnp.concatenate(d_ffn, 0),
               jnp.stack(d_cb), d_final.reshape(-1), jnp.stack(d_cw), loss_part]
    return dh, full, small_g


def kernel(x, a_w_qkv, a_w_o, a_q_gain, a_k_gain, b_w_qkv, b_w_o, rel_bias, mix_norm, ffn_norm, w_up, conv_w, conv_b, w_down, final_norm, loss_target, m_a_w_qkv, m_a_w_o, m_a_q_gain, m_a_k_gain, m_b_w_qkv, m_b_w_o, m_rel_bias, m_mix_norm, m_ffn_norm, m_w_up, m_conv_w, m_conv_b, m_w_down, m_final_norm, v_a_w_qkv, v_a_w_o, v_a_q_gain, v_a_k_gain, v_b_w_qkv, v_b_w_o, v_rel_bias, v_mix_norm, v_ffn_norm, v_w_up, v_conv_w, v_conv_b, v_w_down, v_final_norm):
    d = x.shape[2]
    depth = mix_norm.shape[0]
    dff = w_down.shape[1] * N_DEV
    w_u = w_up.shape[2]

    big = [a_w_qkv, a_w_o, b_w_qkv, b_w_o, w_up, w_down]
    col_sharded = [True, False, True, False, True, False]
    shards = [w.astype(BF16) for w in big] + [conv_w]
    axes = [0 if c else 1 for c in col_sharded] + [0]
    ga_qkv, ga_o, gb_qkv, gb_o, g_up, g_down, g_cw = all_gather("gather_weights", shards, axes)
    ga_o = ga_o.reshape(ga_o.shape[0], -1, d)
    gb_o = gb_o.reshape(gb_o.shape[0], -1, d)
    g_down = g_down.reshape(depth, dff, d)
    cw_full = g_cw.transpose(1, 2, 0, 3).reshape(depth, 3, 2, dff).transpose(0, 2, 1, 3)

    dh, full, small_g = _local_step(x[0], loss_target[0], (ga_qkv, ga_o, gb_qkv, gb_o, g_up, g_down, cw_full),
                                    (a_q_gain, a_k_gain, rel_bias, mix_norm, ffn_norm, conv_b, final_norm))
    grad_x = dh[None]

    own1, got1 = scatter_to_sibling("grads_to_sibling", full)
    chip_sums = [add_pair(f"chip_sum_{a}", own1[a], got1[a]) for a in range(len(full))]
    own2, got2 = scatter_to_chips("grads_to_chips", chip_sums)
    moments = [(m_a_w_qkv, v_a_w_qkv), (m_a_w_o, v_a_w_o), (m_b_w_qkv, v_b_w_qkv), (m_b_w_o, v_b_w_o),
               (m_w_up, v_w_up), (m_w_down, v_w_down)]
    big_out = [adamw_sharded(f"adamw_{a}", own2[a], got2[a], big[a], *moments[a]) for a in range(len(full))]

    width = 2048
    packed = _pack(small_g, N_DEV * width).reshape(-1, N_DEV, width)
    n_rows = packed.shape[0]
    packed = packed.transpose(1, 0, 2).reshape(N_DEV, n_rows * width)
    red = all_reduce_small("small_all_reduce", packed)
    red = red.reshape(N_DEV, n_rows, width).transpose(1, 0, 2)
    (g_qg, g_kg, g_rel, g_mix, g_ffn, g_cb, g_fin, g_cw_all, loss) = _unpack(red, [p.shape for p in small_g])
    idx = 4 * lax.axis_index("x") + 2 * lax.axis_index("y") + lax.axis_index("c")
    g_cw_mine = lax.dynamic_slice_in_dim(g_cw_all, idx * w_u, w_u, axis=2)

    small_w = [a_q_gain, a_k_gain, rel_bias, mix_norm, ffn_norm, conv_b, final_norm, conv_w]
    small_m = [m_a_q_gain, m_a_k_gain, m_rel_bias, m_mix_norm, m_ffn_norm, m_conv_b, m_final_norm, m_conv_w]
    small_v = [v_a_q_gain, v_a_k_gain, v_rel_bias, v_mix_norm, v_ffn_norm, v_conv_b, v_final_norm, v_conv_w]
    small_grads = [g_qg, g_kg, g_rel, g_mix, g_ffn, g_cb, g_fin, g_cw_mine]
    shapes = [w.shape for w in small_w]
    pad_rows = (-_pack(small_w, width).shape[0]) % 8

    def pk8(parts):
        p = _pack(parts, width)
        return jnp.pad(p, ((0, pad_rows), (0, 0))) if pad_rows else p

    sd, sm, sv = adamw_small("adamw_small", pk8(small_grads), pk8(small_w), pk8(small_m), pk8(small_v))
    sd, sm, sv = _unpack(sd, shapes), _unpack(sm, shapes), _unpack(sv, shapes)

    names = ["a_w_qkv", "a_w_o", "a_q_gain", "a_k_gain", "b_w_qkv", "b_w_o", "rel_bias", "mix_norm", "ffn_norm",
             "w_up", "conv_w", "conv_b", "w_down", "final_norm"]
    big_names = ["a_w_qkv", "a_w_o", "b_w_qkv", "b_w_o", "w_up", "w_down"]
    small_names = ["a_q_gain", "a_k_gain", "rel_bias", "mix_norm", "ffn_norm", "conv_b", "final_norm", "conv_w"]
    grads, deltas, new_m, new_v = {}, {}, {}, {}
    for a, nm in enumerate(big_names):
        grads[nm], deltas[nm], new_m[nm], new_v[nm] = big_out[a]
    for a, nm in enumerate(small_names):
        grads[nm] = small_grads[a].reshape(shapes[a])
        deltas[nm], new_m[nm], new_v[nm] = sd[a], sm[a], sv[a]
    return (loss.reshape(()), grad_x, *[grads[n] for n in names], *[deltas[n] for n in names],
            *[new_m[n] for n in names], *[new_v[n] for n in names])
```

```python
import functools
import math

import jax
import jax.numpy as jnp
from jax import lax
from jax.experimental import pallas as pl
from jax.experimental.pallas import tpu as pltpu

F32 = jnp.float32
BF16 = jnp.bfloat16
MESH = pl.DeviceIdType.MESH

N_DEV = 8
LANES = 128
HEAD_DIM = 128
VMEM_LIMIT = 56 * 1024 * 1024
GRID_W = 64
ROPE_THETA = 10000.0
A_KV_HEADS = 4
B_GROUPS = ((128, 1), (512, 4), (2048, 16))
B_HEADS_PER_GROUP = 8
REL_BUCKETS = 32
REL_MAX_DISTANCE = 1024
EPS = 1e-6
NEG_INF = -1e30
ADAM_LR = 0.001
ADAM_B1 = 0.9
ADAM_B2 = 0.999
ADAM_EPS = 1e-08
ADAM_WD = 0.01
ADAM_STEP = 10

ROW_TILE = 256
MM_TM = 1024
MM_TK = 2048
A_BQ = 512
A_BK = 512
B_BQ = 256

NN = (((1,), (0,)), ((), ()))
NT = (((1,), (1,)), ((), ()))
TN = (((0,), (0,)), ((), ()))


def _tile(n, pref):
    return pref if n % pref == 0 else n


def _div_tile(n, pref):
    for cand in range(pref - pref % LANES, 0, -LANES):
        if n % cand == 0:
            return cand
    return n


def _params(sem):
    return pltpu.CompilerParams(dimension_semantics=sem, vmem_limit_bytes=VMEM_LIMIT)


def _dot(a, b, dims):
    return lax.dot_general(a, b, dims, preferred_element_type=F32)


def _mm(name, a, b, *, grid, a_blk, a_map, b_blk, b_map, o_blk, o_map, out_shape, out_dtype, dims,
        res=None):
    nk = grid[2]
    acc_shape = tuple(d for d in o_blk if d is not None)

    def body(*refs):
        if res is None:
            a_ref, b_ref, o_ref, acc = refs
            r_ref = None
        else:
            a_ref, b_ref, r_ref, o_ref, acc = refs
        k = pl.program_id(2)

        @pl.when(k == 0)
        def _():
            acc[...] = jnp.zeros_like(acc)

        acc[...] += _dot(a_ref[...].astype(BF16), b_ref[...].astype(BF16), dims)

        @pl.when(k == nk - 1)
        def _():
            r = acc[...]
            if r_ref is not None:
                r = r + r_ref[...]
            o_ref[...] = r.astype(out_dtype)

    in_specs = [pl.BlockSpec(a_blk, a_map), pl.BlockSpec(b_blk, b_map)]
    args = [a, b]
    if res is not None:
        in_specs.append(pl.BlockSpec(o_blk, o_map))
        args.append(res)
    return pl.pallas_call(
        body, name=name, grid=grid, in_specs=in_specs, out_specs=pl.BlockSpec(o_blk, o_map),
        out_shape=jax.ShapeDtypeStruct(out_shape, out_dtype),
        scratch_shapes=[pltpu.VMEM(acc_shape, F32)],
        compiler_params=_params(("parallel", "parallel", "arbitrary")),
    )(*args)


def mm_col_fwd(name, a, wg, layer, out_dtype, split=1):
    m, kdim = a.shape
    n_dev, _, _, w = wg.shape
    tm, tk = _tile(m, MM_TM), _div_tile(kdim, MM_TK)
    per = n_dev // split
    if split == 1:
        o_blk, o_map, o_shape = (tm, w), (lambda i, j, k: (i, j)), (m, n_dev * w)
    else:
        o_blk, o_map, o_shape = (None, tm, w), (lambda i, j, k: (j // per, i, j % per)), (split, m, per * w)
    return _mm(name, a, wg, grid=(m // tm, n_dev, kdim // tk),
               a_blk=(tm, tk), a_map=lambda i, j, k: (i, k),
               b_blk=(None, None, tk, w), b_map=lambda i, j, k: (j, layer, k, 0),
               o_blk=o_blk, o_map=o_map, out_shape=o_shape, out_dtype=out_dtype, dims=NN)


def mm_col_dx(name, dy, wg, layer, split=1):
    n_dev, _, kdim, w = wg.shape
    m = dy.shape[-2]
    tm, tk = _tile(m, MM_TM), _div_tile(kdim, MM_TK)
    per = n_dev // split
    if split == 1:
        a_blk, a_map = (tm, w), (lambda i, j, k: (i, k))
    else:
        a_blk, a_map = (None, tm, w), (lambda i, j, k: (k // per, i, k % per))
    return _mm(name, dy, wg, grid=(m // tm, kdim // tk, n_dev),
               a_blk=a_blk, a_map=a_map,
               b_blk=(None, None, tk, w), b_map=lambda i, j, k: (k, layer, j, 0),
               o_blk=(tm, tk), o_map=lambda i, j, k: (i, j), out_shape=(m, kdim), out_dtype=F32, dims=NT)


def mm_col_dw(name, x, dy, w, split=1):
    m, kdim = x.shape
    tm, tk = _tile(m, MM_TM), _div_tile(kdim, MM_TK)
    per = N_DEV // split
    if split == 1:
        b_blk, b_map = (tm, w), (lambda i, j, k: (k, j))
    else:
        b_blk, b_map = (None, tm, w), (lambda i, j, k: (j // per, k, j % per))
    return _mm(name, x, dy, grid=(kdim // tk, N_DEV, m // tm),
               a_blk=(tm, tk), a_map=lambda i, j, k: (k, i),
               b_blk=b_blk, b_map=b_map,
               o_blk=(None, tk, w), o_map=lambda i, j, k: (j, i, 0),
               out_shape=(N_DEV, kdim, w), out_dtype=BF16, dims=TN)


def mm_row_fwd(name, a, wg, layer, res):
    m, kdim = a.shape
    n = wg.shape[2]
    tm, tk, tn = _tile(m, MM_TM), _div_tile(kdim, MM_TK), _tile(n, 1024)
    return _mm(name, a, wg, grid=(m // tm, n // tn, kdim // tk),
               a_blk=(tm, tk), a_map=lambda i, j, k: (i, k),
               b_blk=(None, tk, tn), b_map=lambda i, j, k: (layer, k, j),
               o_blk=(tm, tn), o_map=lambda i, j, k: (i, j), out_shape=(m, n), out_dtype=F32, dims=NN,
               res=res)


def mm_row_dx(name, dy, wg, layer):
    m, n = dy.shape
    kdim = wg.shape[1]
    tm, tk, tn = _tile(m, MM_TM), _div_tile(kdim, MM_TK), _tile(n, 1024)
    return _mm(name, dy, wg, grid=(m // tm, kdim // tk, n // tn),
               a_blk=(tm, tn), a_map=lambda i, j, k: (i, k),
               b_blk=(None, tk, tn), b_map=lambda i, j, k: (layer, j, k),
               o_blk=(tm, tk), o_map=lambda i, j, k: (i, j), out_shape=(m, kdim), out_dtype=F32, dims=NT)


def mm_row_dw(name, x, dy):
    m, kdim = x.shape
    n = dy.shape[1]
    tm, tk, tn = _tile(m, MM_TM), _div_tile(kdim, MM_TK), _tile(n, 1024)
    return _mm(name, x, dy, grid=(kdim // tk, n // tn, m // tm),
               a_blk=(tm, tk), a_map=lambda i, j, k: (k, i),
               b_blk=(tm, tn), b_map=lambda i, j, k: (k, j),
               o_blk=(tk, tn), o_map=lambda i, j, k: (i, j), out_shape=(kdim, n), out_dtype=BF16, dims=TN)


def _rows(d, tm):
    return pl.BlockSpec((tm, d), lambda i: (i, 0))


def _vec(d):
    return pl.BlockSpec((1, d), lambda i: (0, 0))


def rms_fwd(name, h, gain):
    t, d = h.shape
    tm = _tile(t, ROW_TILE)

    def body(h_ref, g_ref, o_ref):
        x = h_ref[...]
        rstd = lax.rsqrt(jnp.mean(x * x, axis=-1, keepdims=True) + EPS)
        o_ref[...] = (x * rstd * g_ref[...]).astype(BF16)

    return pl.pallas_call(
        body, name=name, grid=(t // tm,), in_specs=[_rows(d, tm), _vec(d)], out_specs=_rows(d, tm),
        out_shape=jax.ShapeDtypeStruct((t, d), BF16), compiler_params=_params(("parallel",)),
    )(h, gain.reshape(1, d))


def rms_bwd(name, h, gain, dy, dres):
    t, d = h.shape
    tm = _tile(t, ROW_TILE)

    def body(h_ref, g_ref, dy_ref, r_ref, dh_ref, dg_ref):
        @pl.when(pl.program_id(0) == 0)
        def _():
            dg_ref[...] = jnp.zeros_like(dg_ref)

        x = h_ref[...]
        rstd = lax.rsqrt(jnp.mean(x * x, axis=-1, keepdims=True) + EPS)
        xhat = x * rstd
        dyv = dy_ref[...]
        dxhat = dyv * g_ref[...]
        dh_ref[...] = r_ref[...] + rstd * (dxhat - xhat * jnp.mean(dxhat * xhat, axis=-1, keepdims=True))
        dg_ref[...] += jnp.sum(dyv * xhat, axis=0, keepdims=True)

    return pl.pallas_call(
        body, name=name, grid=(t // tm,),
        in_specs=[_rows(d, tm), _vec(d), _rows(d, tm), _rows(d, tm)],
        out_specs=[_rows(d, tm), _vec(d)],
        out_shape=[jax.ShapeDtypeStruct((t, d), F32), jax.ShapeDtypeStruct((1, d), F32)],
        compiler_params=_params(("arbitrary",)),
    )(h, gain.reshape(1, d), dy, dres)


def loss_head(name, h, gain, target):
    t, d = h.shape
    tm = _tile(t, ROW_TILE)

    def body(h_ref, g_ref, t_ref, dh_ref, dg_ref, loss_ref):
        @pl.when(pl.program_id(0) == 0)
        def _():
            dg_ref[...] = jnp.zeros_like(dg_ref)
            loss_ref[...] = jnp.zeros_like(loss_ref)

        x = h_ref[...]
        rstd = lax.rsqrt(jnp.mean(x * x, axis=-1, keepdims=True) + EPS)
        xhat = x * rstd
        err = xhat * g_ref[...] - t_ref[...]
        row = jnp.mean(err * err, axis=-1, keepdims=True)
        loss_ref[...] += 0.5 * jnp.sum(row, axis=0, keepdims=True)
        dyv = err * (1.0 / d)
        dxhat = dyv * g_ref[...]
        dh_ref[...] = rstd * (dxhat - xhat * jnp.mean(dxhat * xhat, axis=-1, keepdims=True))
        dg_ref[...] += jnp.sum(dyv * xhat, axis=0, keepdims=True)

    return pl.pallas_call(
        body, name=name, grid=(t // tm,),
        in_specs=[_rows(d, tm), _vec(d), _rows(d, tm)],
        out_specs=[_rows(d, tm), _vec(d), pl.BlockSpec((1, 1), lambda i: (0, 0))],
        out_shape=[jax.ShapeDtypeStruct((t, d), F32), jax.ShapeDtypeStruct((1, d), F32),
                   jax.ShapeDtypeStruct((1, 1), F32)],
        compiler_params=_params(("arbitrary",)),
    )(h, gain.reshape(1, d), target)


def rope_tables(seq):
    pos = jnp.arange(seq, dtype=jnp.int32)
    row_ids = (pos // GRID_W).astype(F32)
    col_ids = (pos % GRID_W).astype(F32)
    quarter = HEAD_DIM // 4
    inv_freq = ROPE_THETA ** (-jnp.arange(quarter, dtype=F32) / quarter)
    ar = row_ids[:, None] * inv_freq[None, :]
    ac = col_ids[:, None] * inv_freq[None, :]
    cos = jnp.concatenate([jnp.cos(ar), jnp.cos(ar), jnp.cos(ac), jnp.cos(ac)], axis=-1)
    sin = jnp.concatenate([-jnp.sin(ar), jnp.sin(ar), -jnp.sin(ac), jnp.sin(ac)], axis=-1)
    return cos, sin


def _swap_quarters(x):
    lane = lax.broadcasted_iota(jnp.int32, x.shape, 1)
    q = HEAD_DIM // 4
    return jnp.where((lane % (2 * q)) < q, pltpu.roll(x, HEAD_DIM - q, 1), pltpu.roll(x, q, 1))


def qk_prep_fwd(name, qkv, q_gain, k_gain, cos, sin, n_q, n_kv):
    t, width = qkv.shape
    tm = _tile(t, ROW_TILE)

    def body(x_ref, qg_ref, kg_ref, c_ref, s_ref, o_ref):
        c, s = c_ref[...], s_ref[...]
        for hd in range(n_q + n_kv):
            sl = slice(hd * HEAD_DIM, (hd + 1) * HEAD_DIM)
            x = x_ref[:, sl]
            g = qg_ref[...] if hd < n_q else kg_ref[...]
            xn = x * lax.rsqrt(jnp.mean(x * x, axis=-1, keepdims=True) + EPS) * g
            o_ref[:, sl] = (xn * c + _swap_quarters(xn) * s).astype(BF16)
        vs = slice((n_q + n_kv) * HEAD_DIM, width)
        o_ref[:, vs] = x_ref[:, vs].astype(BF16)

    return pl.pallas_call(
        body, name=name, grid=(t // tm,),
        in_specs=[_rows(width, tm), _vec(HEAD_DIM), _vec(HEAD_DIM), _rows(HEAD_DIM, tm), _rows(HEAD_DIM, tm)],
        out_specs=_rows(width, tm), out_shape=jax.ShapeDtypeStruct((t, width), BF16),
        compiler_params=_params(("parallel",)),
    )(qkv, q_gain.reshape(1, HEAD_DIM), k_gain.reshape(1, HEAD_DIM), cos, sin)


def qk_prep_bwd(name, qkv, dq, dk, dv, q_gain, k_gain, cos, sin, n_q, n_kv):
    t, width = qkv.shape
    tm = _tile(t, ROW_TILE)

    def body(x_ref, dq_ref, dk_ref, dv_ref, qg_ref, kg_ref, c_ref, s_ref, o_ref, dg_ref):
        @pl.when(pl.program_id(0) == 0)
        def _():
            dg_ref[...] = jnp.zeros_like(dg_ref)

        c, s = c_ref[...], s_ref[...]
        dgq = jnp.zeros((1, HEAD_DIM), F32)
        dgk = jnp.zeros((1, HEAD_DIM), F32)
        for hd in range(n_q + n_kv):
            sl = slice(hd * HEAD_DIM, (hd + 1) * HEAD_DIM)
            x = x_ref[:, sl]
            if hd < n_q:
                g, dout = qg_ref[...], dq_ref[:, sl]
            else:
                ks = slice((hd - n_q) * HEAD_DIM, (hd - n_q + 1) * HEAD_DIM)
                g, dout = kg_ref[...], dk_ref[:, ks]
            rstd = lax.rsqrt(jnp.mean(x * x, axis=-1, keepdims=True) + EPS)
            xhat = x * rstd
            dxn = dout * c + _swap_quarters(dout * s)
            part = jnp.sum(dxn * xhat, axis=0, keepdims=True)
            if hd < n_q:
                dgq = dgq + part
            else:
                dgk = dgk + part
            dxhat = dxn * g
            o_ref[:, sl] = (rstd * (dxhat - xhat * jnp.mean(dxhat * xhat, axis=-1, keepdims=True))).astype(BF16)
        o_ref[:, slice((n_q + n_kv) * HEAD_DIM, width)] = dv_ref[...].astype(BF16)
        dg_ref[0:1, :] += dgq
        dg_ref[1:2, :] += dgk

    kvw = n_kv * HEAD_DIM
    return pl.pallas_call(
        body, name=name, grid=(t // tm,),
        in_specs=[_rows(width, tm), _rows(n_q * HEAD_DIM, tm), _rows(kvw, tm), _rows(kvw, tm),
                  _vec(HEAD_DIM), _vec(HEAD_DIM), _rows(HEAD_DIM, tm), _rows(HEAD_DIM, tm)],
        out_specs=[_rows(width, tm), pl.BlockSpec((2, HEAD_DIM), lambda i: (0, 0))],
        out_shape=[jax.ShapeDtypeStruct((t, width), BF16), jax.ShapeDtypeStruct((2, HEAD_DIM), F32)],
        compiler_params=_params(("arbitrary",)),
    )(qkv, dq, dk, dv, q_gain.reshape(1, HEAD_DIM), k_gain.reshape(1, HEAD_DIM), cos, sin)


def _lanes(x, width):
    return jnp.tile(x, (1, width // LANES))


def _hs(hd):
    return slice(hd * HEAD_DIM, (hd + 1) * HEAD_DIM)


def attn_fwd(name, q, k, v, bias, *, grid, q_spec, k_spec, v_spec, b_spec, o_spec, valid, nh, shared_kv,
             bq, bk, o_shape, o_dtype):
    ns = grid[2]
    scale = HEAD_DIM ** -0.5

    def body(*refs):
        if bias is None:
            q_ref, k_ref, v_ref, o_ref, lse_ref, m_s, l_s, acc_s = refs
            b_ref = None
        else:
            q_ref, k_ref, v_ref, b_ref, o_ref, lse_ref, m_s, l_s, acc_s = refs
        step = pl.program_id(2)

        @pl.when(step == 0)
        def _():
            m_s[...] = jnp.full_like(m_s, -jnp.inf)
            l_s[...] = jnp.zeros_like(l_s)
            acc_s[...] = jnp.zeros_like(acc_s)

        @pl.when(valid(pl.program_id(1), step))
        def _():
            for hd in range(nh):
                kh = _hs(0 if shared_kv else hd)
                s = _dot(q_ref[:, _hs(hd)], k_ref[:, kh], NT) * scale
                if b_ref is not None:
                    s = s + b_ref[hd]
                m_prev = m_s[hd]
                m_new = jnp.maximum(m_prev, jnp.max(s, axis=-1, keepdims=True))
                alpha = jnp.exp(m_prev - m_new)
                p = jnp.exp(s - _lanes(m_new, bk))
                l_s[hd] = alpha * l_s[hd] + jnp.sum(p, axis=-1, keepdims=True)
                acc_s[hd] = alpha * acc_s[hd] + _dot(p.astype(BF16), v_ref[:, kh], NN)
                m_s[hd] = m_new

        @pl.when(step == ns - 1)
        def _():
            for hd in range(nh):
                o_ref[:, _hs(hd)] = (acc_s[hd] / l_s[hd]).astype(o_dtype)
                lse_ref[:, _hs(hd)] = m_s[hd] + jnp.log(l_s[hd])

    in_specs = [q_spec, k_spec, v_spec] + ([] if bias is None else [b_spec])
    args = [q, k, v] + ([] if bias is None else [bias])
    stat = pltpu.VMEM((nh, bq, LANES), F32)
    return pl.pallas_call(
        body, name=name, grid=grid, in_specs=in_specs, out_specs=[o_spec, o_spec],
        out_shape=[jax.ShapeDtypeStruct(o_shape, o_dtype), jax.ShapeDtypeStruct(o_shape, F32)],
        scratch_shapes=[stat, stat, stat],
        compiler_params=_params(("parallel", "parallel", "arbitrary")),
    )(*args)


def _probs(q_ref, k_ref, v_ref, do_ref, lse_ref, dlt_ref, b, hd, kh, bk, scale):
    s = _dot(q_ref[:, _hs(hd)], k_ref[:, kh], NT) * scale
    if b is not None:
        s = s + b
    p = jnp.exp(s - _lanes(lse_ref[:, _hs(hd)], bk))
    dp = _dot(do_ref[:, _hs(hd)], v_ref[:, kh], NT)
    ds = p * (dp - _lanes(dlt_ref[:, _hs(hd)], bk))
    return p, ds


def attn_bwd_dq(name, q, k, v, do, lse, dlt, *, grid, q_spec, k_spec, v_spec, nh, bq, bk, o_shape):
    ns = grid[2]
    scale = HEAD_DIM ** -0.5

    def body(q_ref, k_ref, v_ref, do_ref, lse_ref, dlt_ref, dq_ref, acc_s):
        step = pl.program_id(2)

        @pl.when(step == 0)
        def _():
            acc_s[...] = jnp.zeros_like(acc_s)

        for hd in range(nh):
            _, ds = _probs(q_ref, k_ref, v_ref, do_ref, lse_ref, dlt_ref, None, hd, _hs(0), bk, scale)
            acc_s[hd] += _dot((ds * scale).astype(BF16), k_ref[:, _hs(0)], NN)

        @pl.when(step == ns - 1)
        def _():
            for hd in range(nh):
                dq_ref[:, _hs(hd)] = acc_s[hd]

    return pl.pallas_call(
        body, name=name, grid=grid, in_specs=[q_spec, k_spec, v_spec, q_spec, q_spec, q_spec],
        out_specs=q_spec, out_shape=jax.ShapeDtypeStruct(o_shape, F32),
        scratch_shapes=[pltpu.VMEM((nh, bq, LANES), F32)],
        compiler_params=_params(("parallel", "parallel", "arbitrary")),
    )(q, k, v, do, lse, dlt)


def _always(i, s):
    return s >= 0


def row_delta(name, do, o, n_heads):
    t, width = do.shape
    tm = _tile(t, ROW_TILE)

    def body(do_ref, o_ref, dl_ref, dob_ref):
        for hd in range(n_heads):
            d = do_ref[:, _hs(hd)]
            s = jnp.sum(d * o_ref[:, _hs(hd)].astype(F32), axis=-1, keepdims=True)
            dl_ref[:, _hs(hd)] = jnp.broadcast_to(s, (tm, HEAD_DIM))
            dob_ref[:, _hs(hd)] = d.astype(BF16)

    return pl.pallas_call(
        body, name=name, grid=(t // tm,), in_specs=[_rows(width, tm), _rows(width, tm)],
        out_specs=[_rows(width, tm), _rows(width, tm)],
        out_shape=[jax.ShapeDtypeStruct((t, width), F32), jax.ShapeDtypeStruct((t, width), BF16)],
        compiler_params=_params(("parallel",)),
    )(do, o)


def _a_specs(n_q, n_kv, bq, bk, q_major):
    grp = n_q // n_kv
    if q_major:
        qm, km = (lambda b, i, s: (i, b)), (lambda b, i, s: (s, n_q + b))
        vm = lambda b, i, s: (s, n_q + n_kv + b)
    else:
        qm, km = (lambda b, i, s: (s, b)), (lambda b, i, s: (i, n_q + b))
        vm = lambda b, i, s: (i, n_q + n_kv + b)
    return (pl.BlockSpec((bq, grp * HEAD_DIM), qm), pl.BlockSpec((bk, HEAD_DIM), km),
            pl.BlockSpec((bk, HEAD_DIM), vm))


def mixer_a_fwd(qkv_r, n_q, n_kv):
    t = qkv_r.shape[0]
    bq, bk = _tile(t, A_BQ), _tile(t, A_BK)
    q_spec, k_spec, v_spec = _a_specs(n_q, n_kv, bq, bk, True)
    return attn_fwd("a_attn_fwd", qkv_r, qkv_r, qkv_r, None, grid=(n_kv, t // bq, t // bk),
                    q_spec=q_spec, k_spec=k_spec, v_spec=v_spec, b_spec=None, o_spec=q_spec, valid=_always,
                    nh=n_q // n_kv, shared_kv=True, bq=bq, bk=bk, o_shape=(t, n_q * HEAD_DIM), o_dtype=BF16)


def mixer_a_bwd(qkv_r, do_b, lse, dlt, n_q, n_kv):
    t = qkv_r.shape[0]
    bq, bk = _tile(t, A_BQ), _tile(t, A_BK)
    grp = n_q // n_kv
    q_spec, k_spec, v_spec = _a_specs(n_q, n_kv, bq, bk, True)
    dq = attn_bwd_dq("a_attn_dq", qkv_r, qkv_r, qkv_r, do_b, lse, dlt, grid=(n_kv, t // bq, t // bk),
                     q_spec=q_spec, k_spec=k_spec, v_spec=v_spec, nh=grp, bq=bq, bk=bk,
                     o_shape=(t, n_q * HEAD_DIM))
    q_spec, k_spec, v_spec = _a_specs(n_q, n_kv, bq, bk, False)
    o_spec = pl.BlockSpec((bk, HEAD_DIM), lambda b, i, s: (i, b))
    dk, dv = _attn_bwd_dkv_out(qkv_r, do_b, lse, dlt, grid=(n_kv, t // bk, t // bq), q_spec=q_spec,
                               k_spec=k_spec, v_spec=v_spec, o_spec=o_spec, grp=grp, bq=bq, bk=bk,
                               o_shape=(t, n_kv * HEAD_DIM))
    return dq, dk, dv


def _attn_bwd_dkv_out(qkv_r, do_b, lse, dlt, *, grid, q_spec, k_spec, v_spec, o_spec, grp, bq, bk, o_shape):
    ns = grid[2]
    scale = HEAD_DIM ** -0.5

    def body(q_ref, k_ref, v_ref, do_ref, lse_ref, dlt_ref, dk_ref, dv_ref, dk_s, dv_s):
        step = pl.program_id(2)

        @pl.when(step == 0)
        def _():
            dk_s[...] = jnp.zeros_like(dk_s)
            dv_s[...] = jnp.zeros_like(dv_s)

        for hd in range(grp):
            p, ds = _probs(q_ref, k_ref, v_ref, do_ref, lse_ref, dlt_ref, None, hd, _hs(0), bk, scale)
            dv_s[...] += _dot(p.astype(BF16), do_ref[:, _hs(hd)], TN)
            dk_s[...] += _dot((ds * scale).astype(BF16), q_ref[:, _hs(hd)], TN)

        @pl.when(step == ns - 1)
        def _():
            dk_ref[...] = dk_s[...]
            dv_ref[...] = dv_s[...]

    acc = pltpu.VMEM((bk, HEAD_DIM), F32)
    return pl.pallas_call(
        body, name="a_attn_dkv", grid=grid, in_specs=[q_spec, k_spec, v_spec, q_spec, q_spec, q_spec],
        out_specs=[o_spec, o_spec], out_shape=[jax.ShapeDtypeStruct(o_shape, F32)] * 2,
        scratch_shapes=[acc, acc], compiler_params=_params(("parallel", "parallel", "arbitrary")),
    )(qkv_r, qkv_r, qkv_r, do_b, lse, dlt)


def t5_bucket(rel):
    nb = REL_BUCKETS // 2
    max_exact = nb // 2
    base = jnp.where(rel > 0, nb, 0)
    n = jnp.abs(rel)
    nf = jnp.maximum(n, 1).astype(F32)
    large = max_exact + (jnp.log(nf / max_exact) / math.log(REL_MAX_DISTANCE / max_exact)
                         * (nb - max_exact)).astype(jnp.int32)
    large = jnp.minimum(large, nb - 1)
    return base + jnp.where(n < max_exact, n, large)


def band_tables(rel_bias_g, half_span, dil, bq):
    a = jnp.arange(bq)[:, None]
    b = jnp.arange(bq)[None, :]
    rel = jnp.stack([(s - 1) * bq + b - a for s in range(3)])
    ok = jnp.abs(rel) <= half_span
    bucket = t5_bucket(rel * dil)
    bias = jnp.zeros((rel_bias_g.shape[1],) + rel.shape, F32)
    for r in range(REL_BUCKETS):
        bias = bias + jnp.where(bucket[None] == r, rel_bias_g[r][:, None, None, None], 0.0)
    return jnp.where(ok[None], bias, NEG_INF), jnp.where(ok, bucket, -1).astype(jnp.int32)


def _b_geometry(t, dil, g, n_groups):
    hg = B_HEADS_PER_GROUP
    length = t // dil
    bq = _tile(length, B_BQ)
    nblk = length // bq
    gw = hg * HEAD_DIM
    per_tok = 3 * n_groups
    return hg, length, bq, nblk, gw, per_tok


def mixer_b_group_fwd(qkv, bias, dil, g, n_groups):
    t = qkv.shape[0]
    hg, length, bq, nblk, gw, per_tok = _b_geometry(t, dil, g, n_groups)
    view = qkv.reshape(length, dil * qkv.shape[1])
    col = lambda c, which: c * per_tok + 3 * g + which
    kblk = lambda i, s: jnp.clip(i - 1 + s, 0, nblk - 1)
    spec = lambda which, streamed: pl.BlockSpec(
        (bq, gw), (lambda c, i, s: (kblk(i, s), col(c, which))) if streamed else (lambda c, i, s: (i, col(c, which))))
    valid = lambda i, s: (i - 1 + s >= 0) & (i - 1 + s < nblk)
    o, lz = attn_fwd(f"b_attn_fwd_d{dil}", view, view, view, bias, grid=(dil, nblk, 3),
                     q_spec=spec(0, False), k_spec=spec(1, True), v_spec=spec(2, True),
                     b_spec=pl.BlockSpec((hg, None, bq, bq), lambda c, i, s: (0, s, 0, 0)),
                     o_spec=pl.BlockSpec((bq, gw), lambda c, i, s: (i, c)), valid=valid, nh=hg,
                     shared_kv=False, bq=bq, bk=bq, o_shape=(length, dil * gw), o_dtype=F32)
    return o.reshape(t, gw), lz.reshape(t, gw)


def mixer_b_group_bwd(qkv, bias, do_g, lz_g, dlt_g, dil, g, n_groups):
    t = qkv.shape[0]
    hg, length, bq, nblk, gw, per_tok = _b_geometry(t, dil, g, n_groups)
    view = qkv.reshape(length, dil * qkv.shape[1])
    dov, lzv, dlv = (x.reshape(length, dil * gw) for x in (do_g, lz_g, dlt_g))
    col = lambda c, which: c * per_tok + 3 * g + which
    nbr = lambda i, s: jnp.clip(i - 1 + s, 0, nblk - 1)
    valid = lambda i, s: (i - 1 + s >= 0) & (i - 1 + s < nblk)
    q_spec = pl.BlockSpec((bq, gw), lambda c, i, s: (i, col(c, 0)))
    k_spec = pl.BlockSpec((bq, gw), lambda c, i, s: (nbr(i, s), col(c, 1)))
    v_spec = pl.BlockSpec((bq, gw), lambda c, i, s: (nbr(i, s), col(c, 2)))
    stat = pl.BlockSpec((bq, gw), lambda c, i, s: (i, c))
    dq, dbias = _band_bwd_dq(f"b_attn_dq_d{dil}", view, dov, lzv, dlv, bias, grid=(dil, nblk, 3),
                             q_spec=q_spec, k_spec=k_spec, v_spec=v_spec, stat_spec=stat,
                             b_spec=pl.BlockSpec((hg, None, bq, bq), lambda c, i, s: (0, s, 0, 0)),
                             valid=valid, nh=hg, bq=bq, o_shape=(length, dil * gw))
    q_spec = pl.BlockSpec((bq, gw), lambda c, i, s: (nbr(i, s), col(c, 0)))
    k_spec = pl.BlockSpec((bq, gw), lambda c, i, s: (i, col(c, 1)))
    v_spec = pl.BlockSpec((bq, gw), lambda c, i, s: (i, col(c, 2)))
    stat = pl.BlockSpec((bq, gw), lambda c, i, s: (nbr(i, s), c))
    dk, dv = _band_bwd_dkv(f"b_attn_dkv_d{dil}", view, dov, lzv, dlv, bias, grid=(dil, nblk, 3),
                           q_spec=q_spec, k_spec=k_spec, v_spec=v_spec, stat_spec=stat,
                           b_spec=pl.BlockSpec((hg, None, bq, bq), lambda c, i, s: (0, 2 - s, 0, 0)),
                           o_spec=pl.BlockSpec((bq, gw), lambda c, i, s: (i, c)),
                           valid=valid, nh=hg, bq=bq, o_shape=(length, dil * gw))
    return dq.reshape(t, gw), dk.reshape(t, gw), dv.reshape(t, gw), dbias


def _band_bwd_dq(name, view, do, lse, dlt, bias, *, grid, q_spec, k_spec, v_spec, stat_spec, b_spec, valid,
                 nh, bq, o_shape):
    scale = HEAD_DIM ** -0.5
    bias_shape = (nh, 3, bq, bq)

    def body(q_ref, k_ref, v_ref, do_ref, lse_ref, dlt_ref, b_ref, dq_ref, db_ref, acc_s):
        step = pl.program_id(2)

        @pl.when((pl.program_id(0) == 0) & (pl.program_id(1) == 0) & (step == 0))
        def _():
            db_ref[...] = jnp.zeros_like(db_ref)

        @pl.when(step == 0)
        def _():
            acc_s[...] = jnp.zeros_like(acc_s)

        @pl.when(valid(pl.program_id(1), step))
        def _():
            for hd in range(nh):
                _, ds = _probs(q_ref, k_ref, v_ref, do_ref, lse_ref, dlt_ref, b_ref[hd], hd, _hs(hd), bq, scale)
                db_ref[hd, step] += ds
                acc_s[hd] += _dot((ds * scale).astype(BF16), k_ref[:, _hs(hd)], NN)

        @pl.when(step == 2)
        def _():
            for hd in range(nh):
                dq_ref[:, _hs(hd)] = acc_s[hd].astype(BF16)

    return pl.pallas_call(
        body, name=name, grid=grid,
        in_specs=[q_spec, k_spec, v_spec, stat_spec, stat_spec, stat_spec, b_spec],
        out_specs=[stat_spec, pl.BlockSpec(bias_shape, lambda c, i, s: (0, 0, 0, 0))],
        out_shape=[jax.ShapeDtypeStruct(o_shape, BF16), jax.ShapeDtypeStruct(bias_shape, F32)],
        scratch_shapes=[pltpu.VMEM((nh, bq, LANES), F32)], compiler_params=_params(("arbitrary",) * 3),
    )(view, view, view, do, lse, dlt, bias)


def _band_bwd_dkv(name, view, do, lse, dlt, bias, *, grid, q_spec, k_spec, v_spec, stat_spec, b_spec, o_spec,
                  valid, nh, bq, o_shape):
    scale = HEAD_DIM ** -0.5

    def body(q_ref, k_ref, v_ref, do_ref, lse_ref, dlt_ref, b_ref, dk_ref, dv_ref, dk_s, dv_s):
        step = pl.program_id(2)

        @pl.when(step == 0)
        def _():
            dk_s[...] = jnp.zeros_like(dk_s)
            dv_s[...] = jnp.zeros_like(dv_s)

        @pl.when(valid(pl.program_id(1), step))
        def _():
            for hd in range(nh):
                p, ds = _probs(q_ref, k_ref, v_ref, do_ref, lse_ref, dlt_ref, b_ref[hd], hd, _hs(hd), bq, scale)
                dv_s[hd] += _dot(p.astype(BF16), do_ref[:, _hs(hd)], TN)
                dk_s[hd] += _dot((ds * scale).astype(BF16), q_ref[:, _hs(hd)], TN)

        @pl.when(step == 2)
        def _():
            for hd in range(nh):
                dk_ref[:, _hs(hd)] = dk_s[hd].astype(BF16)
                dv_ref[:, _hs(hd)] = dv_s[hd].astype(BF16)

    acc = pltpu.VMEM((nh, bq, LANES), F32)
    return pl.pallas_call(
        body, name=name, grid=grid,
        in_specs=[q_spec, k_spec, v_spec, stat_spec, stat_spec, stat_spec, b_spec],
        out_specs=[o_spec, o_spec], out_shape=[jax.ShapeDtypeStruct(o_shape, BF16)] * 2,
        scratch_shapes=[acc, acc], compiler_params=_params(("parallel", "parallel", "arbitrary")),
    )(view, view, view, do, lse, dlt, bias)


def bias_bucket_sums(name, dbias, bucket):
    nh, _, bq, _ = dbias.shape
    db2 = dbias.reshape(nh, 3 * bq, bq)
    bk2 = bucket.reshape(3 * bq, bq)

    def body(db_ref, bk_ref, o_ref):
        row = lax.broadcasted_iota(jnp.int32, (nh, LANES), 0)
        lane = lax.broadcasted_iota(jnp.int32, (nh, LANES), 1)
        out = jnp.zeros((nh, LANES), F32)
        bkt = bk_ref[...]
        for hd in range(nh):
            x = db_ref[hd]
            for r in range(REL_BUCKETS):
                part = jnp.sum(jnp.where(bkt == r, x, 0.0), axis=1, keepdims=True)
                tot = jnp.sum(part, axis=0, keepdims=True)
                out = out + jnp.where((row == hd) & (lane == r), tot, 0.0)
        o_ref[...] = out

    return pl.pallas_call(
        body, name=name, out_shape=jax.ShapeDtypeStruct((nh, LANES), F32),
        compiler_params=pltpu.CompilerParams(vmem_limit_bytes=VMEM_LIMIT),
    )(db2, bk2)


def combine_fwd(name, outs, lzs):
    n_g = len(outs)
    t, gw = outs[0].shape
    tm = _tile(t, ROW_TILE)

    def body(*refs):
        o_refs, lz_refs, y_ref = refs[:n_g], refs[n_g:2 * n_g], refs[2 * n_g]
        lz = [r[...] for r in lz_refs]
        mx = functools.reduce(jnp.maximum, lz)
        e = [jnp.exp(x - mx) for x in lz]
        den = functools.reduce(lambda a, b: a + b, e)
        for g in range(n_g):
            y_ref[:, g * gw:(g + 1) * gw] = (e[g] / den * o_refs[g][...]).astype(BF16)

    return pl.pallas_call(
        body, name=name, grid=(t // tm,), in_specs=[_rows(gw, tm)] * (2 * n_g), out_specs=_rows(n_g * gw, tm),
        out_shape=jax.ShapeDtypeStruct((t, n_g * gw), BF16), compiler_params=_params(("parallel",)),
    )(*outs, *lzs)


def combine_bwd(name, dy, outs, lzs):
    n_g = len(outs)
    t, gw = outs[0].shape
    tm = _tile(t, ROW_TILE)
    nh = gw // HEAD_DIM

    def body(*refs):
        dy_ref = refs[0]
        o_refs, lz_refs = refs[1:1 + n_g], refs[1 + n_g:1 + 2 * n_g]
        do_refs, dl_refs = refs[1 + 2 * n_g:1 + 3 * n_g], refs[1 + 3 * n_g:]
        lz = [r[...] for r in lz_refs]
        mx = functools.reduce(jnp.maximum, lz)
        e = [jnp.exp(x - mx) for x in lz]
        den = functools.reduce(lambda a, b: a + b, e)
        wts = [x / den for x in e]
        for g in range(n_g):
            do_refs[g][...] = (wts[g] * dy_ref[:, g * gw:(g + 1) * gw]).astype(BF16)
        for hd in range(nh):
            mix = jnp.zeros((tm, HEAD_DIM), F32)
            for g in range(n_g):
                prod = dy_ref[:, g * gw + hd * HEAD_DIM:g * gw + (hd + 1) * HEAD_DIM] * o_refs[g][:, _hs(hd)]
                dw = jnp.broadcast_to(jnp.sum(prod, axis=-1, keepdims=True), (tm, HEAD_DIM))
                mix = mix + wts[g][:, _hs(hd)] * dw
            for g in range(n_g):
                dl_refs[g][:, _hs(hd)] = wts[g][:, _hs(hd)] * mix

    return pl.pallas_call(
        body, name=name, grid=(t // tm,),
        in_specs=[_rows(n_g * gw, tm)] + [_rows(gw, tm)] * (2 * n_g),
        out_specs=[_rows(gw, tm)] * (2 * n_g),
        out_shape=[jax.ShapeDtypeStruct((t, gw), BF16)] * n_g + [jax.ShapeDtypeStruct((t, gw), F32)] * n_g,
        compiler_params=_params(("parallel",)),
    )(dy, *outs, *lzs)


def _conv3(u, w_ref, b):
    t = u.shape[0]
    row = lax.broadcasted_iota(jnp.int32, u.shape, 0)
    prev = jnp.where(row == 0, 0.0, pltpu.roll(u, 1, 0))
    nxt = jnp.where(row == t - 1, 0.0, pltpu.roll(u, t - 1, 0))
    out = w_ref[0:1, :] * prev + w_ref[1:2, :] * u + w_ref[2:3, :] * nxt
    return out if b is None else out + b


def _conv3_t(d, w_ref):
    t = d.shape[0]
    row = lax.broadcasted_iota(jnp.int32, d.shape, 0)
    prev = jnp.where(row == 0, 0.0, pltpu.roll(d, 1, 0))
    nxt = jnp.where(row == t - 1, 0.0, pltpu.roll(d, t - 1, 0))
    return w_ref[0:1, :] * nxt + w_ref[1:2, :] * d + w_ref[2:3, :] * prev


def conv_act_fwd(name, u2, cw2, cb2):
    _, t, dff = u2.shape
    tn = LANES

    def body(u_ref, w_ref, b_ref, o_ref):
        cg = _conv3(u_ref[0], w_ref.at[0], b_ref[0])
        cv = _conv3(u_ref[1], w_ref.at[1], b_ref[1])
        o_ref[...] = (cg * jax.nn.sigmoid(cg) * cv).astype(BF16)

    return pl.pallas_call(
        body, name=name, grid=(dff // tn,),
        in_specs=[pl.BlockSpec((2, t, tn), lambda j: (0, 0, j)), pl.BlockSpec((2, 3, tn), lambda j: (0, 0, j)),
                  pl.BlockSpec((2, 1, tn), lambda j: (0, 0, j))],
        out_specs=pl.BlockSpec((t, tn), lambda j: (0, j)), out_shape=jax.ShapeDtypeStruct((t, dff), BF16),
        compiler_params=_params(("parallel",)),
    )(u2, cw2, cb2)


def conv_act_bwd(name, u2, cw2, cb2, dact):
    _, t, dff = u2.shape
    tn = LANES

    def body(u_ref, w_ref, b_ref, d_ref, du_ref, dw_ref):
        d = d_ref[...]
        ug, uv = u_ref[0], u_ref[1]
        cg = _conv3(ug, w_ref.at[0], b_ref[0])
        cv = _conv3(uv, w_ref.at[1], b_ref[1])
        sg = jax.nn.sigmoid(cg)
        dcv = d * (cg * sg)
        dcg = d * cv * (sg * (1.0 + cg * (1.0 - sg)))
        du_ref[0] = _conv3_t(dcg, w_ref.at[0]).astype(BF16)
        du_ref[1] = _conv3_t(dcv, w_ref.at[1]).astype(BF16)
        row = lax.broadcasted_iota(jnp.int32, ug.shape, 0)
        for half, (dc, u) in enumerate(((dcg, ug), (dcv, uv))):
            prev = jnp.where(row == 0, 0.0, pltpu.roll(u, 1, 0))
            nxt = jnp.where(row == t - 1, 0.0, pltpu.roll(u, t - 1, 0))
            for tap, x in enumerate((prev, u, nxt)):
                dw_ref[half, tap:tap + 1, :] = jnp.sum(dc * x, axis=0, keepdims=True)
            dw_ref[half, 3:4, :] = jnp.sum(dc, axis=0, keepdims=True)
            dw_ref[half, 4:8, :] = jnp.zeros((4, tn), F32)

    return pl.pallas_call(
        body, name=name, grid=(dff // tn,),
        in_specs=[pl.BlockSpec((2, t, tn), lambda j: (0, 0, j)), pl.BlockSpec((2, 3, tn), lambda j: (0, 0, j)),
                  pl.BlockSpec((2, 1, tn), lambda j: (0, 0, j)), pl.BlockSpec((t, tn), lambda j: (0, j))],
        out_specs=[pl.BlockSpec((2, t, tn), lambda j: (0, 0, j)), pl.BlockSpec((2, 8, tn), lambda j: (0, 0, j))],
        out_shape=[jax.ShapeDtypeStruct((2, t, dff), BF16), jax.ShapeDtypeStruct((2, 8, dff), F32)],
        compiler_params=_params(("parallel",)),
    )(u2, cw2, cb2, dact)


def _place():
    x, y, c = lax.axis_index("x"), lax.axis_index("y"), lax.axis_index("c")
    chips = [(1 - x, y), (x, 1 - y), (1 - x, 1 - y)]
    return x, y, c, chips


ANY = pl.BlockSpec(memory_space=pl.ANY)


def all_gather(name, shards, axes):
    n = len(shards)

    def body(*refs):
        src, out = refs[:n], refs[n:2 * n]
        send, recv, loc = refs[2 * n:]
        x, y, c, chips = _place()
        sibling = (x, y, 1 - c)

        def slot(a, px, py, pc):
            idx = 4 * px + 2 * py + pc
            return out[a].at[idx] if axes[a] == 0 else out[a].at[:, idx]

        def copy(a, k, block, to, from_src=False):
            return pltpu.make_async_remote_copy(
                src_ref=src[a] if from_src else slot(a, *block), dst_ref=slot(a, *block),
                send_sem=send.at[a, k], recv_sem=recv.at[a, k], device_id=to, device_id_type=MESH)

        mine = [pltpu.make_async_copy(src[a], slot(a, x, y, c), loc.at[a]) for a in range(n)]
        for cp in mine:
            cp.start()
        first = []
        for a in range(n):
            first.append(copy(a, 0, (x, y, c), sibling, True))
            first += [copy(a, 1 + j, (x, y, c), (*chip, c), True) for j, chip in enumerate(chips)]
        for cp in first:
            cp.start()
        passed = []
        for j, chip in enumerate(chips):
            for a in range(n):
                copy(a, 1 + j, (*chip, c), (x, y, c)).wait_recv()
                cp = copy(a, 4 + j, (*chip, c), sibling)
                cp.start()
                passed.append(cp)
        for a in range(n):
            copy(a, 0, sibling, (x, y, c)).wait_recv()
            for j, chip in enumerate(chips):
                copy(a, 4 + j, (*chip, 1 - c), (x, y, c)).wait_recv()
        for cp in first + passed:
            cp.wait_send()
        for cp in mine:
            cp.wait()

    def gshape(s, ax):
        return s.shape[:ax] + (N_DEV,) + s.shape[ax:]

    return pl.pallas_call(
        body, name=name, in_specs=[ANY] * n, out_specs=[ANY] * n,
        out_shape=[jax.ShapeDtypeStruct(gshape(s, ax), s.dtype) for s, ax in zip(shards, axes)],
        scratch_shapes=[pltpu.SemaphoreType.DMA((n, 7)), pltpu.SemaphoreType.DMA((n, 7)),
                        pltpu.SemaphoreType.DMA((n,))],
    )(*shards)


def scatter_to_sibling(name, grads):
    n = len(grads)

    def body(*refs):
        src, got = refs[:n], refs[n:2 * n]
        send, recv = refs[2 * n:]
        x, y, c, _ = _place()
        sibling = (x, y, 1 - c)
        remote = []
        for a in range(n):
            for q in range(4):
                remote.append(pltpu.make_async_remote_copy(
                    src_ref=src[a].at[:, 2 * q + 1 - c], dst_ref=got[a].at[:, q], send_sem=send.at[a, q],
                    recv_sem=recv.at[a, q], device_id=sibling, device_id_type=MESH))
        for cp in remote:
            cp.start()
        for cp in remote:
            cp.wait()

    return pl.pallas_call(
        body, name=name, in_specs=[ANY] * n, out_specs=[ANY] * n,
        out_shape=[jax.ShapeDtypeStruct(g.shape[:1] + (4,) + g.shape[2:], g.dtype) for g in grads],
        scratch_shapes=[pltpu.SemaphoreType.DMA((n, 4)), pltpu.SemaphoreType.DMA((n, 4))],
    )(*grads)


def scatter_to_chips(name, parts):
    n = len(parts)

    def body(*refs):
        src, got = refs[:n], refs[n:2 * n]
        send, recv = refs[2 * n:]
        x, y, c, chips = _place()
        remote = []
        for a in range(n):
            for j, (px, py) in enumerate(chips):
                remote.append(pltpu.make_async_remote_copy(
                    src_ref=src[a].at[:, 2 * px + py], dst_ref=got[a].at[:, j], send_sem=send.at[a, j],
                    recv_sem=recv.at[a, j], device_id=(px, py, c), device_id_type=MESH))
        for cp in remote:
            cp.start()
        for cp in remote:
            cp.wait()

    return pl.pallas_call(
        body, name=name, in_specs=[ANY] * n, out_specs=[ANY] * n,
        out_shape=[jax.ShapeDtypeStruct(p.shape[:1] + (3,) + p.shape[2:], p.dtype) for p in parts],
        scratch_shapes=[pltpu.SemaphoreType.DMA((n, 3)), pltpu.SemaphoreType.DMA((n, 3))],
    )(*parts)


def all_reduce_small(name, vec):
    rows, m = vec.shape

    def body(x_ref, o_ref, buf, send, recv):
        x, y, c, chips = _place()
        sibling = (x, y, 1 - c)

        def blk(px, py, pc):
            return buf.at[pl.ds(pl.multiple_of((4 * px + 2 * py + pc) * rows, rows), rows), :]

        def copy(k, block, to):
            return pltpu.make_async_remote_copy(src_ref=blk(*block), dst_ref=blk(*block), send_sem=send.at[k],
                                                recv_sem=recv.at[k], device_id=to, device_id_type=MESH)

        blk(x, y, c)[...] = x_ref[...]
        first = [copy(0, (x, y, c), sibling)] + [copy(1 + j, (x, y, c), (*chip, c)) for j, chip in enumerate(chips)]
        for cp in first:
            cp.start()
        passed = [copy(4 + j, (*chip, c), sibling) for j, chip in enumerate(chips)]
        for j, chip in enumerate(chips):
            copy(1 + j, (*chip, c), (x, y, c)).wait_recv()
            passed[j].start()
        copy(0, sibling, (x, y, c)).wait_recv()
        for j, chip in enumerate(chips):
            copy(4 + j, (*chip, 1 - c), (x, y, c)).wait_recv()
        for cp in first + passed:
            cp.wait_send()
        tot = buf[0:rows, :]
        for dev in range(1, N_DEV):
            tot = tot + buf[dev * rows:(dev + 1) * rows, :]
        o_ref[...] = tot

    return pl.pallas_call(
        body, name=name, in_specs=[pl.BlockSpec(memory_space=pltpu.VMEM)],
        out_specs=pl.BlockSpec(memory_space=pltpu.VMEM), out_shape=jax.ShapeDtypeStruct((rows, m), F32),
        scratch_shapes=[pltpu.VMEM((N_DEV * rows, m), F32), pltpu.SemaphoreType.DMA((7,)),
                        pltpu.SemaphoreType.DMA((7,))],
        compiler_params=pltpu.CompilerParams(vmem_limit_bytes=VMEM_LIMIT),
    )(vec)


def _as2d(a):
    return a.reshape(-1, a.shape[-1])


def _ew_tiles(rows, cols):
    tr = rows
    for cand in (512, 256, 128, 64, 32, 16):
        if rows % cand == 0 and cand * cols * 4 <= (1 << 20):
            tr = cand
            break
    return tr


def chip_sum(name, full, got, core):
    lyr, _, kdim, ncol = full.shape
    tr = _ew_tiles(kdim, ncol)
    blk = (None, None, tr, ncol)
    by_chip = pl.BlockSpec(blk, lambda l, q, i, c: (l, q, i, 0))

    def body(c_ref, a_ref, b_ref, o_ref):
        o_ref[...] = (a_ref[...].astype(F32) + b_ref[...].astype(F32)).astype(BF16)

    return pl.pallas_call(
        body, name=name,
        grid_spec=pltpu.PrefetchScalarGridSpec(
            num_scalar_prefetch=1, grid=(lyr, 4, kdim // tr),
            in_specs=[pl.BlockSpec(blk, lambda l, q, i, c: (l, 2 * q + c[0], i, 0)), by_chip],
            out_specs=by_chip),
        out_shape=jax.ShapeDtypeStruct((lyr, 4, kdim, ncol), BF16),
        compiler_params=_params(("parallel", "parallel", "parallel")),
    )(core, full, got)


def _adamw_math(w, g, m, v):
    m = ADAM_B1 * m + (1.0 - ADAM_B1) * g
    v = ADAM_B2 * v + (1.0 - ADAM_B2) * (g * g)
    m_hat = m / (1.0 - ADAM_B1 ** ADAM_STEP)
    v_hat = v / (1.0 - ADAM_B2 ** ADAM_STEP)
    delta = -ADAM_LR * (m_hat / (jnp.sqrt(v_hat) + ADAM_EPS) + ADAM_WD * w)
    return delta, m, v


def adamw_sharded(name, sums, got, w, m, v, chip):
    lyr, _, kdim, ncol = sums.shape
    tr = _ew_tiles(kdim, ncol)
    spec = pl.BlockSpec((None, tr, ncol), lambda l, i, q: (l, i, 0))
    ospec = pl.BlockSpec((None, None, tr, ncol), lambda l, i, q: (l, q[0], i, 0))
    gspec = pl.BlockSpec((None, 3, tr, ncol), lambda l, i, q: (l, 0, i, 0))

    def body(q_ref, o_ref, g_ref, w_ref, m_ref, v_ref, go_ref, d_ref, mo_ref, vo_ref):
        g = o_ref[...].astype(F32)
        for j in range(3):
            g = g + g_ref[j].astype(F32)
        d, mn, vn = _adamw_math(w_ref[...], g, m_ref[...], v_ref[...])
        go_ref[...] = g
        d_ref[...] = d
        mo_ref[...] = mn
        vo_ref[...] = vn

    return pl.pallas_call(
        body, name=name,
        grid_spec=pltpu.PrefetchScalarGridSpec(
            num_scalar_prefetch=1, grid=(lyr, kdim // tr),
            in_specs=[ospec, gspec, spec, spec, spec], out_specs=[spec] * 4),
        out_shape=[jax.ShapeDtypeStruct((lyr, kdim, ncol), F32)] * 4,
        compiler_params=_params(("parallel", "parallel")),
    )(chip, sums, got, w, m, v)


def adamw_small(name, g, w, m, v):
    def body(g_ref, w_ref, m_ref, v_ref, d_ref, mo_ref, vo_ref):
        d, mn, vn = _adamw_math(w_ref[...], g_ref[...], m_ref[...], v_ref[...])
        d_ref[...] = d
        mo_ref[...] = mn
        vo_ref[...] = vn

    vm = pl.BlockSpec(memory_space=pltpu.VMEM)
    return pl.pallas_call(
        body, name=name, in_specs=[vm] * 4, out_specs=[vm] * 3,
        out_shape=[jax.ShapeDtypeStruct(g.shape, F32)] * 3,
        compiler_params=pltpu.CompilerParams(vmem_limit_bytes=VMEM_LIMIT),
    )(g, w, m, v)


def _pack(parts, width):
    flat = jnp.concatenate([p.reshape(-1).astype(F32) for p in parts])
    pad = (-flat.shape[0]) % width
    return jnp.pad(flat, (0, pad)).reshape(-1, width) if pad else flat.reshape(-1, width)


def _unpack(packed, shapes):
    flat = packed.reshape(-1)
    out, off = [], 0
    for s in shapes:
        size = math.prod(s)
        out.append(flat[off:off + size].reshape(s))
        off += size
    return out


def _local_step(h, target, gathered, params):
    ga_qkv, ga_o, gb_qkv, gb_o, g_up, g_down, cw_full = gathered
    a_q_gain, a_k_gain, rel_bias, mix_norm, ffn_norm, conv_b, final_norm = params
    t, d = h.shape
    depth = mix_norm.shape[0]
    n_groups = len(B_GROUPS)
    hg = B_HEADS_PER_GROUP
    n_kv = A_KV_HEADS
    w_a, w_b, w_u = ga_qkv.shape[3], gb_qkv.shape[3], g_up.shape[3]
    n_q = w_a * N_DEV // HEAD_DIM - 2 * n_kv
    dff = g_down.shape[1]
    n_a, n_b = ga_qkv.shape[1], gb_qkv.shape[1]
    cb_full = conv_b.reshape(depth, 2, 1, dff)

    cos, sin = rope_tables(t)
    tables = [band_tables(rel_bias[:, g * hg:(g + 1) * hg], win // (2 * dil), dil, _tile(t // dil, B_BQ))
              for g, (win, dil) in enumerate(B_GROUPS)]

    saved = []
    for i in range(depth):
        j = i // 2
        s = {"h_in": h}
        hn = rms_fwd("mix_norm_fwd", h, mix_norm[i])
        s["hn"] = hn
        if i % 2 == 0:
            qkv = mm_col_fwd("a_qkv_fwd", hn, ga_qkv, j, F32)
            qkv_r = qk_prep_fwd("a_qk_prep_fwd", qkv, a_q_gain[j], a_k_gain[j], cos, sin, n_q, n_kv)
            o, lse = mixer_a_fwd(qkv_r, n_q, n_kv)
            s.update(qkv=qkv, qkv_r=qkv_r, o=o, lse=lse)
            h = mm_row_fwd("a_out_fwd", o, ga_o, j, h)
        else:
            qkv = mm_col_fwd("b_qkv_fwd", hn, gb_qkv, j, BF16)
            outs, lzs = [], []
            for g, (win, dil) in enumerate(B_GROUPS):
                o_g, lz_g = mixer_b_group_fwd(qkv, tables[g][0], dil, g, n_groups)
                outs.append(o_g)
                lzs.append(lz_g)
            y = combine_fwd("b_combine_fwd", outs, lzs)
            s.update(qkv=qkv, outs=outs, lzs=lzs, y=y)
            h = mm_row_fwd("b_out_fwd", y, gb_o, j, h)
        s["h_mid"] = h
        hn2 = rms_fwd("ffn_norm_fwd", h, ffn_norm[i])
        u2 = mm_col_fwd("ffn_up_fwd", hn2, g_up, i, F32, split=2)
        act = conv_act_fwd("ffn_conv_act_fwd", u2, cw_full[i], cb_full[i])
        s.update(hn2=hn2, u2=u2, act=act)
        h = mm_row_fwd("ffn_down_fwd", act, g_down, i, h)
        saved.append(s)

    dh, d_final, loss_part = loss_head("loss_head", h, final_norm, target)

    d_mix, d_ffn, d_cw, d_cb = [None] * depth, [None] * depth, [None] * depth, [None] * depth
    d_qg, d_kg = [None] * n_a, [None] * n_a
    d_rel = jnp.zeros((n_groups * hg, LANES), F32)
    gw_up, gw_down = [None] * depth, [None] * depth
    ga_qkv_d, ga_o_d, gb_qkv_d, gb_o_d = {}, {}, {}, {}
    for i in reversed(range(depth)):
        j = i // 2
        s = saved[i]
        dact = mm_row_dx("ffn_down_dx", dh, g_down, i)
        gw_down[i] = mm_row_dw("ffn_down_dw", s["act"], dh)
        du2, dcw = conv_act_bwd("ffn_conv_act_bwd", s["u2"], cw_full[i], cb_full[i], dact)
        d_cw[i] = dcw[:, 0:3, :].transpose(1, 0, 2).reshape(3, 2 * dff)
        d_cb[i] = dcw[:, 3, :].reshape(2 * dff)
        gw_up[i] = mm_col_dw("ffn_up_dw", s["hn2"], du2, w_u, split=2)
        dhn2 = mm_col_dx("ffn_up_dx", du2, g_up, i, split=2)
        dh, d_ffn[i] = rms_bwd("ffn_norm_bwd", s["h_mid"], ffn_norm[i], dhn2, dh)
        if i % 2 == 0:
            do = mm_row_dx("a_out_dx", dh, ga_o, j)
            ga_o_d[j] = mm_row_dw("a_out_dw", s["o"], dh)
            dlt, do_b = row_delta("a_delta", do, s["o"], n_q)
            dq, dk, dv = mixer_a_bwd(s["qkv_r"], do_b, s["lse"], dlt, n_q, n_kv)
            dqkv, dgain = qk_prep_bwd("a_qk_prep_bwd", s["qkv"], dq, dk, dv, a_q_gain[j], a_k_gain[j], cos, sin,
                                      n_q, n_kv)
            d_qg[j], d_kg[j] = dgain[0], dgain[1]
            ga_qkv_d[j] = mm_col_dw("a_qkv_dw", s["hn"], dqkv, w_a)
            dhn = mm_col_dx("a_qkv_dx", dqkv, ga_qkv, j)
        else:
            dy = mm_row_dx("b_out_dx", dh, gb_o, j)
            gb_o_d[j] = mm_row_dw("b_out_dw", s["y"], dh)
            res = combine_bwd("b_combine_bwd", dy, s["outs"], s["lzs"])
            dos, dlts = res[:n_groups], res[n_groups:]
            pieces, rel_rows = [], []
            for g, (win, dil) in enumerate(B_GROUPS):
                dq, dk, dv, dbias = mixer_b_group_bwd(s["qkv"], tables[g][0], dos[g], s["lzs"][g], dlts[g], dil, g,
                                                      n_groups)
                pieces += [dq, dk, dv]
                rel_rows.append(bias_bucket_sums(f"b_bias_sums_d{dil}", dbias, tables[g][1]))
            d_rel = d_rel + jnp.concatenate(rel_rows, axis=0)
            dqkv = jnp.concatenate(pieces, axis=1)
            gb_qkv_d[j] = mm_col_dw("b_qkv_dw", s["hn"], dqkv, w_b)
            dhn = mm_col_dx("b_qkv_dx", dqkv, gb_qkv, j)
        dh, d_mix[i] = rms_bwd("mix_norm_bwd", s["h_in"], mix_norm[i], dhn, dh)

    full = [
        jnp.stack([ga_qkv_d[l] for l in range(n_a)]),
        jnp.stack([ga_o_d[l].reshape(N_DEV, -1, d) for l in range(n_a)]),
        jnp.stack([gb_qkv_d[l] for l in range(n_b)]),
        jnp.stack([gb_o_d[l].reshape(N_DEV, -1, d) for l in range(n_b)]),
        jnp.stack(gw_up),
        jnp.stack([g.reshape(N_DEV, -1, d) for g in gw_down]),
    ]
    d_rel_bias = d_rel[:, :REL_BUCKETS].T
    small_g = [jnp.stack(d_qg), jnp.stack(d_kg), d_rel_bias, jnp.concatenate(d_mix, 0), jnp.concatenate(d_ffn, 0),
               jnp.stack(d_cb), d_final.reshape(-1), jnp.stack(d_cw), loss_part]
    return dh, full, small_g


def kernel(x, a_w_qkv, a_w_o, a_q_gain, a_k_gain, b_w_qkv, b_w_o, rel_bias, mix_norm, ffn_norm, w_up, conv_w, conv_b, w_down, final_norm, loss_target, m_a_w_qkv, m_a_w_o, m_a_q_gain, m_a_k_gain, m_b_w_qkv, m_b_w_o, m_rel_bias, m_mix_norm, m_ffn_norm, m_w_up, m_conv_w, m_conv_b, m_w_down, m_final_norm, v_a_w_qkv, v_a_w_o, v_a_q_gain, v_a_k_gain, v_b_w_qkv, v_b_w_o, v_rel_bias, v_mix_norm, v_ffn_norm, v_w_up, v_conv_w, v_conv_b, v_w_down, v_final_norm):
    d = x.shape[2]
    depth = mix_norm.shape[0]
    dff = w_down.shape[1] * N_DEV
    w_u = w_up.shape[2]

    big = [a_w_qkv, a_w_o, b_w_qkv, b_w_o, w_up, w_down]
    col_sharded = [True, False, True, False, True, False]
    shards = [w.astype(BF16) for w in big] + [conv_w]
    axes = [0 if c else 1 for c in col_sharded] + [0]
    ga_qkv, ga_o, gb_qkv, gb_o, g_up, g_down, g_cw = all_gather("gather_weights", shards, axes)
    ga_o = ga_o.reshape(ga_o.shape[0], -1, d)
    gb_o = gb_o.reshape(gb_o.shape[0], -1, d)
    g_down = g_down.reshape(depth, dff, d)
    cw_full = g_cw.transpose(1, 2, 0, 3).reshape(depth, 3, 2, dff).transpose(0, 2, 1, 3)

    dh, full, small_g = _local_step(x[0], loss_target[0], (ga_qkv, ga_o, gb_qkv, gb_o, g_up, g_down, cw_full),
                                    (a_q_gain, a_k_gain, rel_bias, mix_norm, ffn_norm, conv_b, final_norm))
    grad_x = dh[None]

    core = lax.axis_index("c").astype(jnp.int32).reshape(1)
    chip = (2 * lax.axis_index("x") + lax.axis_index("y")).astype(jnp.int32).reshape(1)
    got1 = scatter_to_sibling("grads_to_sibling", full)
    sums = [chip_sum(f"chip_sum_{a}", full[a], got1[a], core) for a in range(len(full))]
    got2 = scatter_to_chips("grads_to_chips", sums)
    moments = [(m_a_w_qkv, v_a_w_qkv), (m_a_w_o, v_a_w_o), (m_b_w_qkv, v_b_w_qkv), (m_b_w_o, v_b_w_o),
               (m_w_up, v_w_up), (m_w_down, v_w_down)]
    big_out = [adamw_sharded(f"adamw_{a}", sums[a], got2[a], big[a], *moments[a], chip) for a in range(len(full))]

    width = 2048
    packed = _pack(small_g, N_DEV * width).reshape(-1, N_DEV, width)
    n_rows = packed.shape[0]
    packed = packed.transpose(1, 0, 2).reshape(N_DEV, n_rows * width)
    red = all_reduce_small("small_all_reduce", packed)
    red = red.reshape(N_DEV, n_rows, width).transpose(1, 0, 2)
    (g_qg, g_kg, g_rel, g_mix, g_ffn, g_cb, g_fin, g_cw_all, loss) = _unpack(red, [p.shape for p in small_g])
    idx = 4 * lax.axis_index("x") + 2 * lax.axis_index("y") + lax.axis_index("c")
    g_cw_mine = lax.dynamic_slice_in_dim(g_cw_all, idx * w_u, w_u, axis=2)

    small_w = [a_q_gain, a_k_gain, rel_bias, mix_norm, ffn_norm, conv_b, final_norm, conv_w]
    small_m = [m_a_q_gain, m_a_k_gain, m_rel_bias, m_mix_norm, m_ffn_norm, m_conv_b, m_final_norm, m_conv_w]
    small_v = [v_a_q_gain, v_a_k_gain, v_rel_bias, v_mix_norm, v_ffn_norm, v_conv_b, v_final_norm, v_conv_w]
    small_grads = [g_qg, g_kg, g_rel, g_mix, g_ffn, g_cb, g_fin, g_cw_mine]
    shapes = [w.shape for w in small_w]
    pad_rows = (-_pack(small_w, width).shape[0]) % 8

    def pk8(parts):
        p = _pack(parts, width)
        return jnp.pad(p, ((0, pad_rows), (0, 0))) if pad_rows else p

    sd, sm, sv = adamw_small("adamw_small", pk8(small_grads), pk8(small_w), pk8(small_m), pk8(small_v))
    sd, sm, sv = _unpack(sd, shapes), _unpack(sm, shapes), _unpack(sv, shapes)

    names = ["a_w_qkv", "a_w_o", "a_q_gain", "a_k_gain", "b_w_qkv", "b_w_o", "rel_bias", "mix_norm", "ffn_norm",
             "w_up", "conv_w", "conv_b", "w_down", "final_norm"]
    big_names = ["a_w_qkv", "a_w_o", "b_w_qkv", "b_w_o", "w_up", "w_down"]
    small_names = ["a_q_gain", "a_k_gain", "rel_bias", "mix_norm", "ffn_norm", "conv_b", "final_norm", "conv_w"]
    grads, deltas, new_m, new_v = {}, {}, {}, {}
    for a, nm in enumerate(big_names):
        grads[nm], deltas[nm], new_m[nm], new_v[nm] = big_out[a]
    for a, nm in enumerate(small_names):
        grads[nm] = small_grads[a].reshape(shapes[a])
        deltas[nm], new_m[nm], new_v[nm] = sd[a], sm[a], sv[a]
    return (loss.reshape(()), grad_x, *[grads[n] for n in names], *[deltas[n] for n in names],
            *[new_m[n] for n in names], *[new_v[n] for n in names])
```

```python
import functools
import math

import jax
import jax.numpy as jnp
from jax import lax
from jax.experimental import pallas as pl
from jax.experimental.pallas import tpu as pltpu
from jax.experimental.pallas import tpu_sc as plsc

F32 = jnp.float32
BF16 = jnp.bfloat16
MESH = pl.DeviceIdType.MESH

N_DEV = 8
LANES = 128
HEAD_DIM = 128
VMEM_LIMIT = 56 * 1024 * 1024
GRID_W = 64
ROPE_THETA = 10000.0
A_KV_HEADS = 4
B_GROUPS = ((128, 1), (512, 4), (2048, 16))
B_HEADS_PER_GROUP = 8
REL_BUCKETS = 32
REL_MAX_DISTANCE = 1024
EPS = 1e-6
NEG_INF = -1e30
ADAM_LR = 0.001
ADAM_B1 = 0.9
ADAM_B2 = 0.999
ADAM_EPS = 1e-08
ADAM_WD = 0.01
ADAM_STEP = 10

ROW_TILE = 256
MM_TM = 1024
MM_TK = 2048
A_BQ = 512
A_BK = 512
B_BQ = 256

NN = (((1,), (0,)), ((), ()))
NT = (((1,), (1,)), ((), ()))
TN = (((0,), (0,)), ((), ()))


def _tile(n, pref):
    return pref if n % pref == 0 else n


def _div_tile(n, pref):
    for cand in range(pref - pref % LANES, 0, -LANES):
        if n % cand == 0:
            return cand
    return n


def _params(sem):
    return pltpu.CompilerParams(dimension_semantics=sem, vmem_limit_bytes=VMEM_LIMIT)


def _dot(a, b, dims):
    return lax.dot_general(a, b, dims, preferred_element_type=F32)


def _mm(name, a, b, *, grid, a_blk, a_map, b_blk, b_map, o_blk, o_map, out_shape, out_dtype, dims,
        res=None):
    nk = grid[2]
    acc_shape = tuple(d for d in o_blk if d is not None)

    def body(*refs):
        if res is None:
            a_ref, b_ref, o_ref, acc = refs
            r_ref = None
        else:
            a_ref, b_ref, r_ref, o_ref, acc = refs
        k = pl.program_id(2)

        @pl.when(k == 0)
        def _():
            acc[...] = jnp.zeros_like(acc)

        acc[...] += _dot(a_ref[...].astype(BF16), b_ref[...].astype(BF16), dims)

        @pl.when(k == nk - 1)
        def _():
            r = acc[...]
            if r_ref is not None:
                r = r + r_ref[...]
            o_ref[...] = r.astype(out_dtype)

    in_specs = [pl.BlockSpec(a_blk, a_map), pl.BlockSpec(b_blk, b_map)]
    args = [a, b]
    if res is not None:
        in_specs.append(pl.BlockSpec(o_blk, o_map))
        args.append(res)
    return pl.pallas_call(
        body, name=name, grid=grid, in_specs=in_specs, out_specs=pl.BlockSpec(o_blk, o_map),
        out_shape=jax.ShapeDtypeStruct(out_shape, out_dtype),
        scratch_shapes=[pltpu.VMEM(acc_shape, F32)],
        compiler_params=_params(("parallel", "parallel", "arbitrary")),
    )(*args)


def mm_col_fwd(name, a, wg, out_dtype, split=1):
    m, kdim = a.shape
    n_dev, _, w = wg.shape
    tm, tk = _tile(m, MM_TM), _div_tile(kdim, MM_TK)
    per = n_dev // split
    if split == 1:
        o_blk, o_map, o_shape = (tm, w), (lambda i, j, k: (i, j)), (m, n_dev * w)
    else:
        o_blk, o_map, o_shape = (None, tm, w), (lambda i, j, k: (j // per, i, j % per)), (split, m, per * w)
    return _mm(name, a, wg, grid=(m // tm, n_dev, kdim // tk),
               a_blk=(tm, tk), a_map=lambda i, j, k: (i, k),
               b_blk=(None, tk, w), b_map=lambda i, j, k: (j, k, 0),
               o_blk=o_blk, o_map=o_map, out_shape=o_shape, out_dtype=out_dtype, dims=NN)


def mm_col_dx(name, dy, wg, split=1):
    n_dev, kdim, w = wg.shape
    m = dy.shape[-2]
    tm, tk = _tile(m, MM_TM), _div_tile(kdim, MM_TK)
    per = n_dev // split
    if split == 1:
        a_blk, a_map = (tm, w), (lambda i, j, k: (i, k))
    else:
        a_blk, a_map = (None, tm, w), (lambda i, j, k: (k // per, i, k % per))
    return _mm(name, dy, wg, grid=(m // tm, kdim // tk, n_dev),
               a_blk=a_blk, a_map=a_map,
               b_blk=(None, tk, w), b_map=lambda i, j, k: (k, j, 0),
               o_blk=(tm, tk), o_map=lambda i, j, k: (i, j), out_shape=(m, kdim), out_dtype=F32, dims=NT)


def mm_col_dw(name, x, dy, w, split=1):
    m, kdim = x.shape
    tm, tk = _tile(m, MM_TM), _div_tile(kdim, MM_TK)
    per = N_DEV // split
    if split == 1:
        b_blk, b_map = (tm, w), (lambda i, j, k: (k, j))
    else:
        b_blk, b_map = (None, tm, w), (lambda i, j, k: (j // per, k, j % per))
    return _mm(name, x, dy, grid=(kdim // tk, N_DEV, m // tm),
               a_blk=(tm, tk), a_map=lambda i, j, k: (k, i),
               b_blk=b_blk, b_map=b_map,
               o_blk=(None, tk, w), o_map=lambda i, j, k: (j, i, 0),
               out_shape=(N_DEV, kdim, w), out_dtype=BF16, dims=TN)


def mm_row_fwd(name, a, wg, res):
    m, kdim = a.shape
    n = wg.shape[1]
    tm, tk, tn = _tile(m, MM_TM), _div_tile(kdim, MM_TK), _tile(n, 1024)
    return _mm(name, a, wg, grid=(m // tm, n // tn, kdim // tk),
               a_blk=(tm, tk), a_map=lambda i, j, k: (i, k),
               b_blk=(tk, tn), b_map=lambda i, j, k: (k, j),
               o_blk=(tm, tn), o_map=lambda i, j, k: (i, j), out_shape=(m, n), out_dtype=F32, dims=NN,
               res=res)


def mm_row_dx(name, dy, wg):
    m, n = dy.shape
    kdim = wg.shape[0]
    tm, tk, tn = _tile(m, MM_TM), _div_tile(kdim, MM_TK), _tile(n, 1024)
    return _mm(name, dy, wg, grid=(m // tm, kdim // tk, n // tn),
               a_blk=(tm, tn), a_map=lambda i, j, k: (i, k),
               b_blk=(tk, tn), b_map=lambda i, j, k: (j, k),
               o_blk=(tm, tk), o_map=lambda i, j, k: (i, j), out_shape=(m, kdim), out_dtype=F32, dims=NT)


def mm_row_dw(name, x, dy):
    m, kdim = x.shape
    n = dy.shape[1]
    tm, tk, tn = _tile(m, MM_TM), _div_tile(kdim, MM_TK), _tile(n, 1024)
    return _mm(name, x, dy, grid=(kdim // tk, n // tn, m // tm),
               a_blk=(tm, tk), a_map=lambda i, j, k: (k, i),
               b_blk=(tm, tn), b_map=lambda i, j, k: (k, j),
               o_blk=(tk, tn), o_map=lambda i, j, k: (i, j), out_shape=(kdim, n), out_dtype=BF16, dims=TN)


def _rows(d, tm):
    return pl.BlockSpec((tm, d), lambda i: (i, 0))


def _vec(d):
    return pl.BlockSpec((1, d), lambda i: (0, 0))


def rms_fwd(name, h, gain):
    t, d = h.shape
    tm = _tile(t, ROW_TILE)

    def body(h_ref, g_ref, o_ref):
        x = h_ref[...]
        rstd = lax.rsqrt(jnp.mean(x * x, axis=-1, keepdims=True) + EPS)
        o_ref[...] = (x * rstd * g_ref[...]).astype(BF16)

    return pl.pallas_call(
        body, name=name, grid=(t // tm,), in_specs=[_rows(d, tm), _vec(d)], out_specs=_rows(d, tm),
        out_shape=jax.ShapeDtypeStruct((t, d), BF16), compiler_params=_params(("parallel",)),
    )(h, gain.reshape(1, d))


def rms_bwd(name, h, gain, dy, dres):
    t, d = h.shape
    tm = _tile(t, ROW_TILE)

    def body(h_ref, g_ref, dy_ref, r_ref, dh_ref, dg_ref):
        @pl.when(pl.program_id(0) == 0)
        def _():
            dg_ref[...] = jnp.zeros_like(dg_ref)

        x = h_ref[...]
        rstd = lax.rsqrt(jnp.mean(x * x, axis=-1, keepdims=True) + EPS)
        xhat = x * rstd
        dyv = dy_ref[...]
        dxhat = dyv * g_ref[...]
        dh_ref[...] = r_ref[...] + rstd * (dxhat - xhat * jnp.mean(dxhat * xhat, axis=-1, keepdims=True))
        dg_ref[...] += jnp.sum(dyv * xhat, axis=0, keepdims=True)

    return pl.pallas_call(
        body, name=name, grid=(t // tm,),
        in_specs=[_rows(d, tm), _vec(d), _rows(d, tm), _rows(d, tm)],
        out_specs=[_rows(d, tm), _vec(d)],
        out_shape=[jax.ShapeDtypeStruct((t, d), F32), jax.ShapeDtypeStruct((1, d), F32)],
        compiler_params=_params(("arbitrary",)),
    )(h, gain.reshape(1, d), dy, dres)


def loss_head(name, h, gain, target):
    t, d = h.shape
    tm = _tile(t, ROW_TILE)

    def body(h_ref, g_ref, t_ref, dh_ref, dg_ref, loss_ref):
        @pl.when(pl.program_id(0) == 0)
        def _():
            dg_ref[...] = jnp.zeros_like(dg_ref)
            loss_ref[...] = jnp.zeros_like(loss_ref)

        x = h_ref[...]
        rstd = lax.rsqrt(jnp.mean(x * x, axis=-1, keepdims=True) + EPS)
        xhat = x * rstd
        err = xhat * g_ref[...] - t_ref[...]
        row = jnp.mean(err * err, axis=-1, keepdims=True)
        loss_ref[...] += 0.5 * jnp.sum(row, axis=0, keepdims=True)
        dyv = err * (1.0 / d)
        dxhat = dyv * g_ref[...]
        dh_ref[...] = rstd * (dxhat - xhat * jnp.mean(dxhat * xhat, axis=-1, keepdims=True))
        dg_ref[...] += jnp.sum(dyv * xhat, axis=0, keepdims=True)

    return pl.pallas_call(
        body, name=name, grid=(t // tm,),
        in_specs=[_rows(d, tm), _vec(d), _rows(d, tm)],
        out_specs=[_rows(d, tm), _vec(d), pl.BlockSpec((1, 1), lambda i: (0, 0))],
        out_shape=[jax.ShapeDtypeStruct((t, d), F32), jax.ShapeDtypeStruct((1, d), F32),
                   jax.ShapeDtypeStruct((1, 1), F32)],
        compiler_params=_params(("arbitrary",)),
    )(h, gain.reshape(1, d), target)


def rope_tables(seq):
    pos = jnp.arange(seq, dtype=jnp.int32)
    row_ids = (pos // GRID_W).astype(F32)
    col_ids = (pos % GRID_W).astype(F32)
    quarter = HEAD_DIM // 4
    inv_freq = ROPE_THETA ** (-jnp.arange(quarter, dtype=F32) / quarter)
    ar = row_ids[:, None] * inv_freq[None, :]
    ac = col_ids[:, None] * inv_freq[None, :]
    cos = jnp.concatenate([jnp.cos(ar), jnp.cos(ar), jnp.cos(ac), jnp.cos(ac)], axis=-1)
    sin = jnp.concatenate([-jnp.sin(ar), jnp.sin(ar), -jnp.sin(ac), jnp.sin(ac)], axis=-1)
    return cos, sin


def _swap_quarters(x):
    lane = lax.broadcasted_iota(jnp.int32, x.shape, 1)
    q = HEAD_DIM // 4
    return jnp.where((lane % (2 * q)) < q, pltpu.roll(x, HEAD_DIM - q, 1), pltpu.roll(x, q, 1))


def qk_prep_fwd(name, qkv, q_gain, k_gain, cos, sin, n_q, n_kv):
    t, width = qkv.shape
    tm = _tile(t, ROW_TILE)

    def body(x_ref, qg_ref, kg_ref, c_ref, s_ref, o_ref):
        c, s = c_ref[...], s_ref[...]
        for hd in range(n_q + n_kv):
            sl = slice(hd * HEAD_DIM, (hd + 1) * HEAD_DIM)
            x = x_ref[:, sl]
            g = qg_ref[...] if hd < n_q else kg_ref[...]
            xn = x * lax.rsqrt(jnp.mean(x * x, axis=-1, keepdims=True) + EPS) * g
            o_ref[:, sl] = (xn * c + _swap_quarters(xn) * s).astype(BF16)
        vs = slice((n_q + n_kv) * HEAD_DIM, width)
        o_ref[:, vs] = x_ref[:, vs].astype(BF16)

    return pl.pallas_call(
        body, name=name, grid=(t // tm,),
        in_specs=[_rows(width, tm), _vec(HEAD_DIM), _vec(HEAD_DIM), _rows(HEAD_DIM, tm), _rows(HEAD_DIM, tm)],
        out_specs=_rows(width, tm), out_shape=jax.ShapeDtypeStruct((t, width), BF16),
        compiler_params=_params(("parallel",)),
    )(qkv, q_gain.reshape(1, HEAD_DIM), k_gain.reshape(1, HEAD_DIM), cos, sin)


def qk_prep_bwd(name, qkv, dq, dk, dv, q_gain, k_gain, cos, sin, n_q, n_kv):
    t, width = qkv.shape
    tm = _tile(t, ROW_TILE)

    def body(x_ref, dq_ref, dk_ref, dv_ref, qg_ref, kg_ref, c_ref, s_ref, o_ref, dg_ref):
        @pl.when(pl.program_id(0) == 0)
        def _():
            dg_ref[...] = jnp.zeros_like(dg_ref)

        c, s = c_ref[...], s_ref[...]
        dgq = jnp.zeros((1, HEAD_DIM), F32)
        dgk = jnp.zeros((1, HEAD_DIM), F32)
        for hd in range(n_q + n_kv):
            sl = slice(hd * HEAD_DIM, (hd + 1) * HEAD_DIM)
            x = x_ref[:, sl]
            if hd < n_q:
                g, dout = qg_ref[...], dq_ref[:, sl]
            else:
                ks = slice((hd - n_q) * HEAD_DIM, (hd - n_q + 1) * HEAD_DIM)
                g, dout = kg_ref[...], dk_ref[:, ks]
            rstd = lax.rsqrt(jnp.mean(x * x, axis=-1, keepdims=True) + EPS)
            xhat = x * rstd
            dxn = dout * c + _swap_quarters(dout * s)
            part = jnp.sum(dxn * xhat, axis=0, keepdims=True)
            if hd < n_q:
                dgq = dgq + part
            else:
                dgk = dgk + part
            dxhat = dxn * g
            o_ref[:, sl] = (rstd * (dxhat - xhat * jnp.mean(dxhat * xhat, axis=-1, keepdims=True))).astype(BF16)
        o_ref[:, slice((n_q + n_kv) * HEAD_DIM, width)] = dv_ref[...].astype(BF16)
        dg_ref[0:1, :] += dgq
        dg_ref[1:2, :] += dgk

    kvw = n_kv * HEAD_DIM
    return pl.pallas_call(
        body, name=name, grid=(t // tm,),
        in_specs=[_rows(width, tm), _rows(n_q * HEAD_DIM, tm), _rows(kvw, tm), _rows(kvw, tm),
                  _vec(HEAD_DIM), _vec(HEAD_DIM), _rows(HEAD_DIM, tm), _rows(HEAD_DIM, tm)],
        out_specs=[_rows(width, tm), pl.BlockSpec((2, HEAD_DIM), lambda i: (0, 0))],
        out_shape=[jax.ShapeDtypeStruct((t, width), BF16), jax.ShapeDtypeStruct((2, HEAD_DIM), F32)],
        compiler_params=_params(("arbitrary",)),
    )(qkv, dq, dk, dv, q_gain.reshape(1, HEAD_DIM), k_gain.reshape(1, HEAD_DIM), cos, sin)


def _lanes(x, width):
    return jnp.tile(x, (1, width // LANES))


def _hs(hd):
    return slice(hd * HEAD_DIM, (hd + 1) * HEAD_DIM)


def attn_fwd(name, q, k, v, bias, *, grid, q_spec, k_spec, v_spec, b_spec, o_spec, valid, nh, shared_kv,
             bq, bk, o_shape, o_dtype):
    ns = grid[2]
    scale = HEAD_DIM ** -0.5

    def body(*refs):
        if bias is None:
            q_ref, k_ref, v_ref, o_ref, lse_ref, m_s, l_s, acc_s = refs
            b_ref = None
        else:
            q_ref, k_ref, v_ref, b_ref, o_ref, lse_ref, m_s, l_s, acc_s = refs
        step = pl.program_id(2)

        @pl.when(step == 0)
        def _():
            m_s[...] = jnp.full_like(m_s, -jnp.inf)
            l_s[...] = jnp.zeros_like(l_s)
            acc_s[...] = jnp.zeros_like(acc_s)

        @pl.when(valid(pl.program_id(1), step))
        def _():
            for hd in range(nh):
                kh = _hs(0 if shared_kv else hd)
                s = _dot(q_ref[:, _hs(hd)], k_ref[:, kh], NT) * scale
                if b_ref is not None:
                    s = s + b_ref[hd]
                m_prev = m_s[hd]
                m_new = jnp.maximum(m_prev, jnp.max(s, axis=-1, keepdims=True))
                alpha = jnp.exp(m_prev - m_new)
                p = jnp.exp(s - _lanes(m_new, bk))
                l_s[hd] = alpha * l_s[hd] + jnp.sum(p, axis=-1, keepdims=True)
                acc_s[hd] = alpha * acc_s[hd] + _dot(p.astype(BF16), v_ref[:, kh], NN)
                m_s[hd] = m_new

        @pl.when(step == ns - 1)
        def _():
            for hd in range(nh):
                o_ref[:, _hs(hd)] = (acc_s[hd] / l_s[hd]).astype(o_dtype)
                lse_ref[:, _hs(hd)] = m_s[hd] + jnp.log(l_s[hd])

    in_specs = [q_spec, k_spec, v_spec] + ([] if bias is None else [b_spec])
    args = [q, k, v] + ([] if bias is None else [bias])
    stat = pltpu.VMEM((nh, bq, LANES), F32)
    return pl.pallas_call(
        body, name=name, grid=grid, in_specs=in_specs, out_specs=[o_spec, o_spec],
        out_shape=[jax.ShapeDtypeStruct(o_shape, o_dtype), jax.ShapeDtypeStruct(o_shape, F32)],
        scratch_shapes=[stat, stat, stat],
        compiler_params=_params(("parallel", "parallel", "arbitrary")),
    )(*args)


def _probs(q_ref, k_ref, v_ref, do_ref, lse_ref, dlt_ref, b, hd, kh, bk, scale):
    s = _dot(q_ref[:, _hs(hd)], k_ref[:, kh], NT) * scale
    if b is not None:
        s = s + b
    p = jnp.exp(s - _lanes(lse_ref[:, _hs(hd)], bk))
    dp = _dot(do_ref[:, _hs(hd)], v_ref[:, kh], NT)
    ds = p * (dp - _lanes(dlt_ref[:, _hs(hd)], bk))
    return p, ds


def attn_bwd_dq(name, q, k, v, do, lse, dlt, *, grid, q_spec, k_spec, v_spec, nh, bq, bk, o_shape):
    ns = grid[2]
    scale = HEAD_DIM ** -0.5

    def body(q_ref, k_ref, v_ref, do_ref, lse_ref, dlt_ref, dq_ref, acc_s):
        step = pl.program_id(2)

        @pl.when(step == 0)
        def _():
            acc_s[...] = jnp.zeros_like(acc_s)

        for hd in range(nh):
            _, ds = _probs(q_ref, k_ref, v_ref, do_ref, lse_ref, dlt_ref, None, hd, _hs(0), bk, scale)
            acc_s[hd] += _dot((ds * scale).astype(BF16), k_ref[:, _hs(0)], NN)

        @pl.when(step == ns - 1)
        def _():
            for hd in range(nh):
                dq_ref[:, _hs(hd)] = acc_s[hd]

    return pl.pallas_call(
        body, name=name, grid=grid, in_specs=[q_spec, k_spec, v_spec, q_spec, q_spec, q_spec],
        out_specs=q_spec, out_shape=jax.ShapeDtypeStruct(o_shape, F32),
        scratch_shapes=[pltpu.VMEM((nh, bq, LANES), F32)],
        compiler_params=_params(("parallel", "parallel", "arbitrary")),
    )(q, k, v, do, lse, dlt)


def _always(i, s):
    return s >= 0


def row_delta(name, do, o, n_heads):
    t, width = do.shape
    tm = _tile(t, ROW_TILE)

    def body(do_ref, o_ref, dl_ref, dob_ref):
        for hd in range(n_heads):
            d = do_ref[:, _hs(hd)]
            s = jnp.sum(d * o_ref[:, _hs(hd)].astype(F32), axis=-1, keepdims=True)
            dl_ref[:, _hs(hd)] = jnp.broadcast_to(s, (tm, HEAD_DIM))
            dob_ref[:, _hs(hd)] = d.astype(BF16)

    return pl.pallas_call(
        body, name=name, grid=(t // tm,), in_specs=[_rows(width, tm), _rows(width, tm)],
        out_specs=[_rows(width, tm), _rows(width, tm)],
        out_shape=[jax.ShapeDtypeStruct((t, width), F32), jax.ShapeDtypeStruct((t, width), BF16)],
        compiler_params=_params(("parallel",)),
    )(do, o)


def _a_specs(n_q, n_kv, bq, bk, q_major):
    grp = n_q // n_kv
    if q_major:
        qm, km = (lambda b, i, s: (i, b)), (lambda b, i, s: (s, n_q + b))
        vm = lambda b, i, s: (s, n_q + n_kv + b)
    else:
        qm, km = (lambda b, i, s: (s, b)), (lambda b, i, s: (i, n_q + b))
        vm = lambda b, i, s: (i, n_q + n_kv + b)
    return (pl.BlockSpec((bq, grp * HEAD_DIM), qm), pl.BlockSpec((bk, HEAD_DIM), km),
            pl.BlockSpec((bk, HEAD_DIM), vm))


def mixer_a_fwd(qkv_r, n_q, n_kv):
    t = qkv_r.shape[0]
    bq, bk = _tile(t, A_BQ), _tile(t, A_BK)
    q_spec, k_spec, v_spec = _a_specs(n_q, n_kv, bq, bk, True)
    return attn_fwd("a_attn_fwd", qkv_r, qkv_r, qkv_r, None, grid=(n_kv, t // bq, t // bk),
                    q_spec=q_spec, k_spec=k_spec, v_spec=v_spec, b_spec=None, o_spec=q_spec, valid=_always,
                    nh=n_q // n_kv, shared_kv=True, bq=bq, bk=bk, o_shape=(t, n_q * HEAD_DIM), o_dtype=BF16)


def mixer_a_bwd(qkv_r, do_b, lse, dlt, n_q, n_kv):
    t = qkv_r.shape[0]
    bq, bk = _tile(t, A_BQ), _tile(t, A_BK)
    grp = n_q // n_kv
    q_spec, k_spec, v_spec = _a_specs(n_q, n_kv, bq, bk, True)
    dq = attn_bwd_dq("a_attn_dq", qkv_r, qkv_r, qkv_r, do_b, lse, dlt, grid=(n_kv, t // bq, t // bk),
                     q_spec=q_spec, k_spec=k_spec, v_spec=v_spec, nh=grp, bq=bq, bk=bk,
                     o_shape=(t, n_q * HEAD_DIM))
    q_spec, k_spec, v_spec = _a_specs(n_q, n_kv, bq, bk, False)
    o_spec = pl.BlockSpec((bk, HEAD_DIM), lambda b, i, s: (i, b))
    dk, dv = _attn_bwd_dkv_out(qkv_r, do_b, lse, dlt, grid=(n_kv, t // bk, t // bq), q_spec=q_spec,
                               k_spec=k_spec, v_spec=v_spec, o_spec=o_spec, grp=grp, bq=bq, bk=bk,
                               o_shape=(t, n_kv * HEAD_DIM))
    return dq, dk, dv


def _attn_bwd_dkv_out(qkv_r, do_b, lse, dlt, *, grid, q_spec, k_spec, v_spec, o_spec, grp, bq, bk, o_shape):
    ns = grid[2]
    scale = HEAD_DIM ** -0.5

    def body(q_ref, k_ref, v_ref, do_ref, lse_ref, dlt_ref, dk_ref, dv_ref, dk_s, dv_s):
        step = pl.program_id(2)

        @pl.when(step == 0)
        def _():
            dk_s[...] = jnp.zeros_like(dk_s)
            dv_s[...] = jnp.zeros_like(dv_s)

        for hd in range(grp):
            p, ds = _probs(q_ref, k_ref, v_ref, do_ref, lse_ref, dlt_ref, None, hd, _hs(0), bk, scale)
            dv_s[...] += _dot(p.astype(BF16), do_ref[:, _hs(hd)], TN)
            dk_s[...] += _dot((ds * scale).astype(BF16), q_ref[:, _hs(hd)], TN)

        @pl.when(step == ns - 1)
        def _():
            dk_ref[...] = dk_s[...]
            dv_ref[...] = dv_s[...]

    acc = pltpu.VMEM((bk, HEAD_DIM), F32)
    return pl.pallas_call(
        body, name="a_attn_dkv", grid=grid, in_specs=[q_spec, k_spec, v_spec, q_spec, q_spec, q_spec],
        out_specs=[o_spec, o_spec], out_shape=[jax.ShapeDtypeStruct(o_shape, F32)] * 2,
        scratch_shapes=[acc, acc], compiler_params=_params(("parallel", "parallel", "arbitrary")),
    )(qkv_r, qkv_r, qkv_r, do_b, lse, dlt)


def t5_bucket(rel):
    nb = REL_BUCKETS // 2
    max_exact = nb // 2
    base = jnp.where(rel > 0, nb, 0)
    n = jnp.abs(rel)
    nf = jnp.maximum(n, 1).astype(F32)
    large = max_exact + (jnp.log(nf / max_exact) / math.log(REL_MAX_DISTANCE / max_exact)
                         * (nb - max_exact)).astype(jnp.int32)
    large = jnp.minimum(large, nb - 1)
    return base + jnp.where(n < max_exact, n, large)


def band_tables(rel_bias_g, half_span, dil, bq):
    a = jnp.arange(bq)[:, None]
    b = jnp.arange(bq)[None, :]
    rel = jnp.stack([(s - 1) * bq + b - a for s in range(3)])
    ok = jnp.abs(rel) <= half_span
    bucket = t5_bucket(rel * dil)
    bias = jnp.zeros((rel_bias_g.shape[1],) + rel.shape, F32)
    for r in range(REL_BUCKETS):
        bias = bias + jnp.where(bucket[None] == r, rel_bias_g[r][:, None, None, None], 0.0)
    return jnp.where(ok[None], bias, NEG_INF), jnp.where(ok, bucket, -1).astype(jnp.int32)


def _b_geometry(t, dil, g, n_groups):
    hg = B_HEADS_PER_GROUP
    length = t // dil
    bq = _tile(length, B_BQ)
    nblk = length // bq
    gw = hg * HEAD_DIM
    per_tok = 3 * n_groups
    return hg, length, bq, nblk, gw, per_tok


def mixer_b_group_fwd(qkv, bias, dil, g, n_groups):
    t = qkv.shape[0]
    hg, length, bq, nblk, gw, per_tok = _b_geometry(t, dil, g, n_groups)
    view = qkv.reshape(length, dil * qkv.shape[1])
    col = lambda c, which: c * per_tok + 3 * g + which
    kblk = lambda i, s: jnp.clip(i - 1 + s, 0, nblk - 1)
    spec = lambda which, streamed: pl.BlockSpec(
        (bq, gw), (lambda c, i, s: (kblk(i, s), col(c, which))) if streamed else (lambda c, i, s: (i, col(c, which))))
    valid = lambda i, s: (i - 1 + s >= 0) & (i - 1 + s < nblk)
    o, lz = attn_fwd(f"b_attn_fwd_d{dil}", view, view, view, bias, grid=(dil, nblk, 3),
                     q_spec=spec(0, False), k_spec=spec(1, True), v_spec=spec(2, True),
                     b_spec=pl.BlockSpec((hg, None, bq, bq), lambda c, i, s: (0, s, 0, 0)),
                     o_spec=pl.BlockSpec((bq, gw), lambda c, i, s: (i, c)), valid=valid, nh=hg,
                     shared_kv=False, bq=bq, bk=bq, o_shape=(length, dil * gw), o_dtype=F32)
    return o.reshape(t, gw), lz.reshape(t, gw)


def mixer_b_group_bwd(qkv, bias, do_g, lz_g, dlt_g, dil, g, n_groups):
    t = qkv.shape[0]
    hg, length, bq, nblk, gw, per_tok = _b_geometry(t, dil, g, n_groups)
    view = qkv.reshape(length, dil * qkv.shape[1])
    dov, lzv, dlv = (x.reshape(length, dil * gw) for x in (do_g, lz_g, dlt_g))
    col = lambda c, which: c * per_tok + 3 * g + which
    nbr = lambda i, s: jnp.clip(i - 1 + s, 0, nblk - 1)
    valid = lambda i, s: (i - 1 + s >= 0) & (i - 1 + s < nblk)
    q_spec = pl.BlockSpec((bq, gw), lambda c, i, s: (i, col(c, 0)))
    k_spec = pl.BlockSpec((bq, gw), lambda c, i, s: (nbr(i, s), col(c, 1)))
    v_spec = pl.BlockSpec((bq, gw), lambda c, i, s: (nbr(i, s), col(c, 2)))
    stat = pl.BlockSpec((bq, gw), lambda c, i, s: (i, c))
    dq, dbias = _band_bwd_dq(f"b_attn_dq_d{dil}", view, dov, lzv, dlv, bias, grid=(dil, nblk, 3),
                             q_spec=q_spec, k_spec=k_spec, v_spec=v_spec, stat_spec=stat,
                             b_spec=pl.BlockSpec((hg, None, bq, bq), lambda c, i, s: (0, s, 0, 0)),
                             valid=valid, nh=hg, bq=bq, o_shape=(length, dil * gw))
    q_spec = pl.BlockSpec((bq, gw), lambda c, i, s: (nbr(i, s), col(c, 0)))
    k_spec = pl.BlockSpec((bq, gw), lambda c, i, s: (i, col(c, 1)))
    v_spec = pl.BlockSpec((bq, gw), lambda c, i, s: (i, col(c, 2)))
    stat = pl.BlockSpec((bq, gw), lambda c, i, s: (nbr(i, s), c))
    dk, dv = _band_bwd_dkv(f"b_attn_dkv_d{dil}", view, dov, lzv, dlv, bias, grid=(dil, nblk, 3),
                           q_spec=q_spec, k_spec=k_spec, v_spec=v_spec, stat_spec=stat,
                           b_spec=pl.BlockSpec((hg, None, bq, bq), lambda c, i, s: (0, 2 - s, 0, 0)),
                           o_spec=pl.BlockSpec((bq, gw), lambda c, i, s: (i, c)),
                           valid=valid, nh=hg, bq=bq, o_shape=(length, dil * gw))
    return dq.reshape(t, gw), dk.reshape(t, gw), dv.reshape(t, gw), dbias


def _band_bwd_dq(name, view, do, lse, dlt, bias, *, grid, q_spec, k_spec, v_spec, stat_spec, b_spec, valid,
                 nh, bq, o_shape):
    scale = HEAD_DIM ** -0.5
    bias_shape = (nh, 3, bq, bq)

    def body(q_ref, k_ref, v_ref, do_ref, lse_ref, dlt_ref, b_ref, dq_ref, db_ref, acc_s):
        step = pl.program_id(2)

        @pl.when((pl.program_id(0) == 0) & (pl.program_id(1) == 0) & (step == 0))
        def _():
            db_ref[...] = jnp.zeros_like(db_ref)

        @pl.when(step == 0)
        def _():
            acc_s[...] = jnp.zeros_like(acc_s)

        @pl.when(valid(pl.program_id(1), step))
        def _():
            for hd in range(nh):
                _, ds = _probs(q_ref, k_ref, v_ref, do_ref, lse_ref, dlt_ref, b_ref[hd], hd, _hs(hd), bq, scale)
                db_ref[hd, step] += ds
                acc_s[hd] += _dot((ds * scale).astype(BF16), k_ref[:, _hs(hd)], NN)

        @pl.when(step == 2)
        def _():
            for hd in range(nh):
                dq_ref[:, _hs(hd)] = acc_s[hd].astype(BF16)

    return pl.pallas_call(
        body, name=name, grid=grid,
        in_specs=[q_spec, k_spec, v_spec, stat_spec, stat_spec, stat_spec, b_spec],
        out_specs=[stat_spec, pl.BlockSpec(bias_shape, lambda c, i, s: (0, 0, 0, 0))],
        out_shape=[jax.ShapeDtypeStruct(o_shape, BF16), jax.ShapeDtypeStruct(bias_shape, F32)],
        scratch_shapes=[pltpu.VMEM((nh, bq, LANES), F32)], compiler_params=_params(("arbitrary",) * 3),
    )(view, view, view, do, lse, dlt, bias)


def _band_bwd_dkv(name, view, do, lse, dlt, bias, *, grid, q_spec, k_spec, v_spec, stat_spec, b_spec, o_spec,
                  valid, nh, bq, o_shape):
    scale = HEAD_DIM ** -0.5

    def body(q_ref, k_ref, v_ref, do_ref, lse_ref, dlt_ref, b_ref, dk_ref, dv_ref, dk_s, dv_s):
        step = pl.program_id(2)

        @pl.when(step == 0)
        def _():
            dk_s[...] = jnp.zeros_like(dk_s)
            dv_s[...] = jnp.zeros_like(dv_s)

        @pl.when(valid(pl.program_id(1), step))
        def _():
            for hd in range(nh):
                p, ds = _probs(q_ref, k_ref, v_ref, do_ref, lse_ref, dlt_ref, b_ref[hd], hd, _hs(hd), bq, scale)
                dv_s[hd] += _dot(p.astype(BF16), do_ref[:, _hs(hd)], TN)
                dk_s[hd] += _dot((ds * scale).astype(BF16), q_ref[:, _hs(hd)], TN)

        @pl.when(step == 2)
        def _():
            for hd in range(nh):
                dk_ref[:, _hs(hd)] = dk_s[hd].astype(BF16)
                dv_ref[:, _hs(hd)] = dv_s[hd].astype(BF16)

    acc = pltpu.VMEM((nh, bq, LANES), F32)
    return pl.pallas_call(
        body, name=name, grid=grid,
        in_specs=[q_spec, k_spec, v_spec, stat_spec, stat_spec, stat_spec, b_spec],
        out_specs=[o_spec, o_spec], out_shape=[jax.ShapeDtypeStruct(o_shape, BF16)] * 2,
        scratch_shapes=[acc, acc], compiler_params=_params(("parallel", "parallel", "arbitrary")),
    )(view, view, view, do, lse, dlt, bias)


def bias_bucket_sums(name, dbias, bucket):
    nh, _, bq, _ = dbias.shape
    db2 = dbias.reshape(nh, 3 * bq, bq)
    bk2 = bucket.reshape(3 * bq, bq)

    def body(db_ref, bk_ref, o_ref):
        row = lax.broadcasted_iota(jnp.int32, (nh, LANES), 0)
        lane = lax.broadcasted_iota(jnp.int32, (nh, LANES), 1)
        out = jnp.zeros((nh, LANES), F32)
        bkt = bk_ref[...]
        for hd in range(nh):
            x = db_ref[hd]
            for r in range(REL_BUCKETS):
                part = jnp.sum(jnp.where(bkt == r, x, 0.0), axis=1, keepdims=True)
                tot = jnp.sum(part, axis=0, keepdims=True)
                out = out + jnp.where((row == hd) & (lane == r), tot, 0.0)
        o_ref[...] = out

    return pl.pallas_call(
        body, name=name, out_shape=jax.ShapeDtypeStruct((nh, LANES), F32),
        compiler_params=pltpu.CompilerParams(vmem_limit_bytes=VMEM_LIMIT),
    )(db2, bk2)


def combine_fwd(name, outs, lzs):
    n_g = len(outs)
    t, gw = outs[0].shape
    tm = _tile(t, ROW_TILE)

    def body(*refs):
        o_refs, lz_refs, y_ref = refs[:n_g], refs[n_g:2 * n_g], refs[2 * n_g]
        lz = [r[...] for r in lz_refs]
        mx = functools.reduce(jnp.maximum, lz)
        e = [jnp.exp(x - mx) for x in lz]
        den = functools.reduce(lambda a, b: a + b, e)
        for g in range(n_g):
            y_ref[:, g * gw:(g + 1) * gw] = (e[g] / den * o_refs[g][...]).astype(BF16)

    return pl.pallas_call(
        body, name=name, grid=(t // tm,), in_specs=[_rows(gw, tm)] * (2 * n_g), out_specs=_rows(n_g * gw, tm),
        out_shape=jax.ShapeDtypeStruct((t, n_g * gw), BF16), compiler_params=_params(("parallel",)),
    )(*outs, *lzs)


def combine_bwd(name, dy, outs, lzs):
    n_g = len(outs)
    t, gw = outs[0].shape
    tm = _tile(t, ROW_TILE)
    nh = gw // HEAD_DIM

    def body(*refs):
        dy_ref = refs[0]
        o_refs, lz_refs = refs[1:1 + n_g], refs[1 + n_g:1 + 2 * n_g]
        do_refs, dl_refs = refs[1 + 2 * n_g:1 + 3 * n_g], refs[1 + 3 * n_g:]
        lz = [r[...] for r in lz_refs]
        mx = functools.reduce(jnp.maximum, lz)
        e = [jnp.exp(x - mx) for x in lz]
        den = functools.reduce(lambda a, b: a + b, e)
        wts = [x / den for x in e]
        for g in range(n_g):
            do_refs[g][...] = (wts[g] * dy_ref[:, g * gw:(g + 1) * gw]).astype(BF16)
        for hd in range(nh):
            mix = jnp.zeros((tm, HEAD_DIM), F32)
            for g in range(n_g):
                prod = dy_ref[:, g * gw + hd * HEAD_DIM:g * gw + (hd + 1) * HEAD_DIM] * o_refs[g][:, _hs(hd)]
                dw = jnp.broadcast_to(jnp.sum(prod, axis=-1, keepdims=True), (tm, HEAD_DIM))
                mix = mix + wts[g][:, _hs(hd)] * dw
            for g in range(n_g):
                dl_refs[g][:, _hs(hd)] = wts[g][:, _hs(hd)] * mix

    return pl.pallas_call(
        body, name=name, grid=(t // tm,),
        in_specs=[_rows(n_g * gw, tm)] + [_rows(gw, tm)] * (2 * n_g),
        out_specs=[_rows(gw, tm)] * (2 * n_g),
        out_shape=[jax.ShapeDtypeStruct((t, gw), BF16)] * n_g + [jax.ShapeDtypeStruct((t, gw), F32)] * n_g,
        compiler_params=_params(("parallel",)),
    )(dy, *outs, *lzs)


def _conv3(u, w_ref, b):
    t = u.shape[0]
    row = lax.broadcasted_iota(jnp.int32, u.shape, 0)
    prev = jnp.where(row == 0, 0.0, pltpu.roll(u, 1, 0))
    nxt = jnp.where(row == t - 1, 0.0, pltpu.roll(u, t - 1, 0))
    out = w_ref[0:1, :] * prev + w_ref[1:2, :] * u + w_ref[2:3, :] * nxt
    return out if b is None else out + b


def _conv3_t(d, w_ref):
    t = d.shape[0]
    row = lax.broadcasted_iota(jnp.int32, d.shape, 0)
    prev = jnp.where(row == 0, 0.0, pltpu.roll(d, 1, 0))
    nxt = jnp.where(row == t - 1, 0.0, pltpu.roll(d, t - 1, 0))
    return w_ref[0:1, :] * nxt + w_ref[1:2, :] * d + w_ref[2:3, :] * prev


def conv_act_fwd(name, u2, cw2, cb2):
    _, t, dff = u2.shape
    tn = LANES

    def body(u_ref, w_ref, b_ref, o_ref):
        cg = _conv3(u_ref[0], w_ref.at[0], b_ref[0])
        cv = _conv3(u_ref[1], w_ref.at[1], b_ref[1])
        o_ref[...] = (cg * jax.nn.sigmoid(cg) * cv).astype(BF16)

    return pl.pallas_call(
        body, name=name, grid=(dff // tn,),
        in_specs=[pl.BlockSpec((2, t, tn), lambda j: (0, 0, j)), pl.BlockSpec((2, 3, tn), lambda j: (0, 0, j)),
                  pl.BlockSpec((2, 1, tn), lambda j: (0, 0, j))],
        out_specs=pl.BlockSpec((t, tn), lambda j: (0, j)), out_shape=jax.ShapeDtypeStruct((t, dff), BF16),
        compiler_params=_params(("parallel",)),
    )(u2, cw2, cb2)


def conv_act_bwd(name, u2, cw2, cb2, dact):
    _, t, dff = u2.shape
    tn = LANES

    def body(u_ref, w_ref, b_ref, d_ref, du_ref, dw_ref):
        d = d_ref[...]
        ug, uv = u_ref[0], u_ref[1]
        cg = _conv3(ug, w_ref.at[0], b_ref[0])
        cv = _conv3(uv, w_ref.at[1], b_ref[1])
        sg = jax.nn.sigmoid(cg)
        dcv = d * (cg * sg)
        dcg = d * cv * (sg * (1.0 + cg * (1.0 - sg)))
        du_ref[0] = _conv3_t(dcg, w_ref.at[0]).astype(BF16)
        du_ref[1] = _conv3_t(dcv, w_ref.at[1]).astype(BF16)
        row = lax.broadcasted_iota(jnp.int32, ug.shape, 0)
        for half, (dc, u) in enumerate(((dcg, ug), (dcv, uv))):
            prev = jnp.where(row == 0, 0.0, pltpu.roll(u, 1, 0))
            nxt = jnp.where(row == t - 1, 0.0, pltpu.roll(u, t - 1, 0))
            for tap, x in enumerate((prev, u, nxt)):
                dw_ref[half, tap:tap + 1, :] = jnp.sum(dc * x, axis=0, keepdims=True)
            dw_ref[half, 3:4, :] = jnp.sum(dc, axis=0, keepdims=True)
            dw_ref[half, 4:8, :] = jnp.zeros((4, tn), F32)

    return pl.pallas_call(
        body, name=name, grid=(dff // tn,),
        in_specs=[pl.BlockSpec((2, t, tn), lambda j: (0, 0, j)), pl.BlockSpec((2, 3, tn), lambda j: (0, 0, j)),
                  pl.BlockSpec((2, 1, tn), lambda j: (0, 0, j)), pl.BlockSpec((t, tn), lambda j: (0, j))],
        out_specs=[pl.BlockSpec((2, t, tn), lambda j: (0, 0, j)), pl.BlockSpec((2, 8, tn), lambda j: (0, 0, j))],
        out_shape=[jax.ShapeDtypeStruct((2, t, dff), BF16), jax.ShapeDtypeStruct((2, 8, dff), F32)],
        compiler_params=_params(("parallel",)),
    )(u2, cw2, cb2, dact)


GATHER_ID, SIBLING_ID, CHIPS_ID = 0, 1, 2


def _place():
    x, y, c = lax.axis_index("x"), lax.axis_index("y"), lax.axis_index("c")
    chips = [(1 - x, y), (x, 1 - y), (1 - x, 1 - y)]
    return x, y, c, chips


def _handshake(peers):
    barrier = pltpu.get_barrier_semaphore()
    for peer in peers:
        pl.semaphore_signal(barrier, inc=1, device_id=peer, device_id_type=MESH)
    pl.semaphore_wait(barrier, len(peers))


def _sequencer(name, body, out_type, scratch_types, collective_id):
    return pl.kernel(body, out_type=out_type, mesh=plsc.ScalarSubcoreMesh(axis_name="seq", num_cores=1),
                     scratch_types=scratch_types, name=name,
                     compiler_params=pltpu.CompilerParams(collective_id=collective_id))


def _gather_body(n):
    def body(*refs):
        src, out = refs[:n], refs[n:2 * n]
        send, recv, loc = refs[2 * n:]
        x, y, c, chips = _place()
        sibling = (x, y, 1 - c)
        _handshake([sibling] + [(*chip, c) for chip in chips])

        def slot(a, px, py, pc):
            return out[a].at[4 * px + 2 * py + pc]

        def copy(a, k, block, to, from_src=False):
            return pltpu.make_async_remote_copy(
                src_ref=src[a] if from_src else slot(a, *block), dst_ref=slot(a, *block),
                send_sem=send.at[a, k], recv_sem=recv.at[a, k], device_id=to, device_id_type=MESH)

        mine = [pltpu.make_async_copy(src[a], slot(a, x, y, c), loc.at[a]) for a in range(n)]
        for cp in mine:
            cp.start()
        first = []
        for a in range(n):
            first.append(copy(a, 0, (x, y, c), sibling, True))
            first += [copy(a, 1 + j, (x, y, c), (*chip, c), True) for j, chip in enumerate(chips)]
        for cp in first:
            cp.start()
        passed = []
        for j, chip in enumerate(chips):
            for a in range(n):
                copy(a, 1 + j, (*chip, c), (x, y, c)).wait_recv()
                cp = copy(a, 4 + j, (*chip, c), sibling)
                cp.start()
                passed.append(cp)
        for a in range(n):
            copy(a, 0, sibling, (x, y, c)).wait_recv()
            for j, chip in enumerate(chips):
                copy(a, 4 + j, (*chip, 1 - c), (x, y, c)).wait_recv()
        for cp in first + passed:
            cp.wait_send()
        for cp in mine:
            cp.wait()

    return body


def gather_layer(name, shards):
    n = len(shards)
    out_type = [jax.ShapeDtypeStruct((N_DEV,) + s.shape, s.dtype) for s in shards]
    scratch = [pltpu.SemaphoreType.DMA((n, 7)), pltpu.SemaphoreType.DMA((n, 7)), pltpu.SemaphoreType.DMA((n,))]
    return _sequencer(name, _gather_body(n), out_type, scratch, GATHER_ID)(*shards)


def _to_sibling_body(n):
    def body(*refs):
        src, got = refs[:n], refs[n:2 * n]
        send, recv = refs[2 * n:]
        x, y, c, _ = _place()
        sibling = (x, y, 1 - c)
        _handshake([sibling])
        remote = []
        for a in range(n):
            for q in range(4):
                remote.append(pltpu.make_async_remote_copy(
                    src_ref=src[a].at[2 * q + 1 - c], dst_ref=got[a].at[q], send_sem=send.at[a, q],
                    recv_sem=recv.at[a, q], device_id=sibling, device_id_type=MESH))
        for cp in remote:
            cp.start()
        for cp in remote:
            cp.wait()

    return body


def grads_to_sibling(name, grads):
    n = len(grads)
    out_type = [jax.ShapeDtypeStruct((4,) + g.shape[1:], g.dtype) for g in grads]
    scratch = [pltpu.SemaphoreType.DMA((n, 4)), pltpu.SemaphoreType.DMA((n, 4))]
    return _sequencer(name, _to_sibling_body(n), out_type, scratch, SIBLING_ID)(*grads)


def _to_chips_body(n):
    def body(*refs):
        src, got = refs[:n], refs[n:2 * n]
        send, recv = refs[2 * n:]
        x, y, c, chips = _place()
        _handshake([(*chip, c) for chip in chips])
        remote = []
        for a in range(n):
            for j, (px, py) in enumerate(chips):
                remote.append(pltpu.make_async_remote_copy(
                    src_ref=src[a].at[2 * px + py], dst_ref=got[a].at[j], send_sem=send.at[a, j],
                    recv_sem=recv.at[a, j], device_id=(px, py, c), device_id_type=MESH))
        for cp in remote:
            cp.start()
        for cp in remote:
            cp.wait()

    return body


def grads_to_chips(name, parts):
    n = len(parts)
    out_type = [jax.ShapeDtypeStruct((3,) + p.shape[1:], p.dtype) for p in parts]
    scratch = [pltpu.SemaphoreType.DMA((n, 3)), pltpu.SemaphoreType.DMA((n, 3))]
    return _sequencer(name, _to_chips_body(n), out_type, scratch, CHIPS_ID)(*parts)


def all_reduce_small(name, vec):
    rows, m = vec.shape

    def body(x_ref, o_ref, buf, send, recv):
        x, y, c, chips = _place()
        sibling = (x, y, 1 - c)

        def blk(px, py, pc):
            return buf.at[pl.ds(pl.multiple_of((4 * px + 2 * py + pc) * rows, rows), rows), :]

        def copy(k, block, to):
            return pltpu.make_async_remote_copy(src_ref=blk(*block), dst_ref=blk(*block), send_sem=send.at[k],
                                                recv_sem=recv.at[k], device_id=to, device_id_type=MESH)

        blk(x, y, c)[...] = x_ref[...]
        first = [copy(0, (x, y, c), sibling)] + [copy(1 + j, (x, y, c), (*chip, c)) for j, chip in enumerate(chips)]
        for cp in first:
            cp.start()
        passed = [copy(4 + j, (*chip, c), sibling) for j, chip in enumerate(chips)]
        for j, chip in enumerate(chips):
            copy(1 + j, (*chip, c), (x, y, c)).wait_recv()
            passed[j].start()
        copy(0, sibling, (x, y, c)).wait_recv()
        for j, chip in enumerate(chips):
            copy(4 + j, (*chip, 1 - c), (x, y, c)).wait_recv()
        for cp in first + passed:
            cp.wait_send()
        tot = buf[0:rows, :]
        for dev in range(1, N_DEV):
            tot = tot + buf[dev * rows:(dev + 1) * rows, :]
        o_ref[...] = tot

    return pl.pallas_call(
        body, name=name, in_specs=[pl.BlockSpec(memory_space=pltpu.VMEM)],
        out_specs=pl.BlockSpec(memory_space=pltpu.VMEM), out_shape=jax.ShapeDtypeStruct((rows, m), F32),
        scratch_shapes=[pltpu.VMEM((N_DEV * rows, m), F32), pltpu.SemaphoreType.DMA((7,)),
                        pltpu.SemaphoreType.DMA((7,))],
        compiler_params=pltpu.CompilerParams(vmem_limit_bytes=VMEM_LIMIT),
    )(vec)


def _ew_tiles(rows, cols):
    tr = rows
    for cand in (512, 256, 128, 64, 32, 16):
        if rows % cand == 0 and cand * cols * 4 <= (1 << 20):
            tr = cand
            break
    return tr


def chip_sum(name, full, got, core):
    _, kdim, ncol = full.shape
    tr = _ew_tiles(kdim, ncol)
    blk = (None, tr, ncol)
    by_chip = pl.BlockSpec(blk, lambda q, i, c: (q, i, 0))

    def body(c_ref, a_ref, b_ref, o_ref):
        o_ref[...] = (a_ref[...].astype(F32) + b_ref[...].astype(F32)).astype(BF16)

    return pl.pallas_call(
        body, name=name,
        grid_spec=pltpu.PrefetchScalarGridSpec(
            num_scalar_prefetch=1, grid=(4, kdim // tr),
            in_specs=[pl.BlockSpec(blk, lambda q, i, c: (2 * q + c[0], i, 0)), by_chip], out_specs=by_chip),
        out_shape=jax.ShapeDtypeStruct((4, kdim, ncol), BF16),
        compiler_params=_params(("parallel", "parallel")),
    )(core, full, got)


def _adamw_math(w, g, m, v):
    m = ADAM_B1 * m + (1.0 - ADAM_B1) * g
    v = ADAM_B2 * v + (1.0 - ADAM_B2) * (g * g)
    m_hat = m / (1.0 - ADAM_B1 ** ADAM_STEP)
    v_hat = v / (1.0 - ADAM_B2 ** ADAM_STEP)
    delta = -ADAM_LR * (m_hat / (jnp.sqrt(v_hat) + ADAM_EPS) + ADAM_WD * w)
    return delta, m, v


def adamw_layer(name, sums, got, w, m, v, layer, chip):
    _, kdim, ncol = sums.shape
    tr = _ew_tiles(kdim, ncol)
    out = pl.BlockSpec((tr, ncol), lambda i, q: (i, 0))
    mine = pl.BlockSpec((None, tr, ncol), lambda i, q: (q[0], i, 0))
    others = pl.BlockSpec((3, tr, ncol), lambda i, q: (0, i, 0))
    param = pl.BlockSpec((None, tr, ncol), lambda i, q: (layer, i, 0))

    def body(q_ref, o_ref, g_ref, w_ref, m_ref, v_ref, go_ref, d_ref, mo_ref, vo_ref):
        g = o_ref[...].astype(F32)
        for j in range(3):
            g = g + g_ref[j].astype(F32)
        d, mn, vn = _adamw_math(w_ref[...], g, m_ref[...], v_ref[...])
        go_ref[...] = g
        d_ref[...] = d
        mo_ref[...] = mn
        vo_ref[...] = vn

    return pl.pallas_call(
        body, name=name,
        grid_spec=pltpu.PrefetchScalarGridSpec(
            num_scalar_prefetch=1, grid=(kdim // tr,),
            in_specs=[mine, others, param, param, param], out_specs=[out] * 4),
        out_shape=[jax.ShapeDtypeStruct((kdim, ncol), F32)] * 4,
        compiler_params=_params(("parallel",)),
    )(chip, sums, got, w, m, v)


def adamw_small(name, g, w, m, v):
    def body(g_ref, w_ref, m_ref, v_ref, d_ref, mo_ref, vo_ref):
        d, mn, vn = _adamw_math(w_ref[...], g_ref[...], m_ref[...], v_ref[...])
        d_ref[...] = d
        mo_ref[...] = mn
        vo_ref[...] = vn

    vm = pl.BlockSpec(memory_space=pltpu.VMEM)
    return pl.pallas_call(
        body, name=name, in_specs=[vm] * 4, out_specs=[vm] * 3,
        out_shape=[jax.ShapeDtypeStruct(g.shape, F32)] * 3,
        compiler_params=pltpu.CompilerParams(vmem_limit_bytes=VMEM_LIMIT),
    )(g, w, m, v)


def _pack(parts, width):
    flat = jnp.concatenate([p.reshape(-1).astype(F32) for p in parts])
    pad = (-flat.shape[0]) % width
    return jnp.pad(flat, (0, pad)).reshape(-1, width) if pad else flat.reshape(-1, width)


def _unpack(packed, shapes):
    flat = packed.reshape(-1)
    out, off = [], 0
    for s in shapes:
        size = math.prod(s)
        out.append(flat[off:off + size].reshape(s))
        off += size
    return out


def _local_step(h, target, layers, params, on_grads=None):
    a_q_gain, a_k_gain, rel_bias, mix_norm, ffn_norm, conv_b, final_norm = params
    t, d = h.shape
    depth = len(layers)
    n_groups = len(B_GROUPS)
    hg = B_HEADS_PER_GROUP
    n_kv = A_KV_HEADS
    w_a, w_b, w_u = layers[0][0].shape[2], layers[1][0].shape[2], layers[0][2].shape[2]
    n_q = w_a * N_DEV // HEAD_DIM - 2 * n_kv
    dff = layers[0][3].shape[0]
    n_a = (depth + 1) // 2
    cb_full = conv_b.reshape(depth, 2, 1, dff)

    cos, sin = rope_tables(t)
    tables = [band_tables(rel_bias[:, g * hg:(g + 1) * hg], win // (2 * dil), dil, _tile(t // dil, B_BQ))
              for g, (win, dil) in enumerate(B_GROUPS)]

    saved = []
    for i in range(depth):
        j = i // 2
        w_qkv, w_o, w_up_i, w_down_i, cw = layers[i]
        s = {"h_in": h}
        hn = rms_fwd("mix_norm_fwd", h, mix_norm[i])
        s["hn"] = hn
        if i % 2 == 0:
            qkv = mm_col_fwd("a_qkv_fwd", hn, w_qkv, F32)
            qkv_r = qk_prep_fwd("a_qk_prep_fwd", qkv, a_q_gain[j], a_k_gain[j], cos, sin, n_q, n_kv)
            o, lse = mixer_a_fwd(qkv_r, n_q, n_kv)
            s.update(qkv=qkv, qkv_r=qkv_r, o=o, lse=lse)
            h = mm_row_fwd("a_out_fwd", o, w_o, h)
        else:
            qkv = mm_col_fwd("b_qkv_fwd", hn, w_qkv, BF16)
            outs, lzs = [], []
            for g, (win, dil) in enumerate(B_GROUPS):
                o_g, lz_g = mixer_b_group_fwd(qkv, tables[g][0], dil, g, n_groups)
                outs.append(o_g)
                lzs.append(lz_g)
            y = combine_fwd("b_combine_fwd", outs, lzs)
            s.update(qkv=qkv, outs=outs, lzs=lzs, y=y)
            h = mm_row_fwd("b_out_fwd", y, w_o, h)
        s["h_mid"] = h
        hn2 = rms_fwd("ffn_norm_fwd", h, ffn_norm[i])
        u2 = mm_col_fwd("ffn_up_fwd", hn2, w_up_i, F32, split=2)
        act = conv_act_fwd("ffn_conv_act_fwd", u2, cw, cb_full[i])
        s.update(hn2=hn2, u2=u2, act=act)
        h = mm_row_fwd("ffn_down_fwd", act, w_down_i, h)
        saved.append(s)

    dh, d_final, loss_part = loss_head("loss_head", h, final_norm, target)

    d_mix, d_ffn, d_cw, d_cb = [None] * depth, [None] * depth, [None] * depth, [None] * depth
    d_qg, d_kg = [None] * n_a, [None] * n_a
    d_rel = jnp.zeros((n_groups * hg, LANES), F32)
    layer_grads = [None] * depth
    for i in reversed(range(depth)):
        j = i // 2
        w_qkv, w_o, w_up_i, w_down_i, cw = layers[i]
        s = saved[i]
        dact = mm_row_dx("ffn_down_dx", dh, w_down_i)
        g_down = mm_row_dw("ffn_down_dw", s["act"], dh)
        du2, dcw = conv_act_bwd("ffn_conv_act_bwd", s["u2"], cw, cb_full[i], dact)
        d_cw[i] = dcw[:, 0:3, :].transpose(1, 0, 2).reshape(3, 2 * dff)
        d_cb[i] = dcw[:, 3, :].reshape(2 * dff)
        g_up = mm_col_dw("ffn_up_dw", s["hn2"], du2, w_u, split=2)
        dhn2 = mm_col_dx("ffn_up_dx", du2, w_up_i, split=2)
        dh, d_ffn[i] = rms_bwd("ffn_norm_bwd", s["h_mid"], ffn_norm[i], dhn2, dh)
        if i % 2 == 0:
            do = mm_row_dx("a_out_dx", dh, w_o)
            g_o = mm_row_dw("a_out_dw", s["o"], dh)
            dlt, do_b = row_delta("a_delta", do, s["o"], n_q)
            dq, dk, dv = mixer_a_bwd(s["qkv_r"], do_b, s["lse"], dlt, n_q, n_kv)
            dqkv, dgain = qk_prep_bwd("a_qk_prep_bwd", s["qkv"], dq, dk, dv, a_q_gain[j], a_k_gain[j], cos, sin,
                                      n_q, n_kv)
            d_qg[j], d_kg[j] = dgain[0], dgain[1]
            g_qkv = mm_col_dw("a_qkv_dw", s["hn"], dqkv, w_a)
            dhn = mm_col_dx("a_qkv_dx", dqkv, w_qkv)
        else:
            dy = mm_row_dx("b_out_dx", dh, w_o)
            g_o = mm_row_dw("b_out_dw", s["y"], dh)
            res = combine_bwd("b_combine_bwd", dy, s["outs"], s["lzs"])
            dos, dlts = res[:n_groups], res[n_groups:]
            pieces, rel_rows = [], []
            for g, (win, dil) in enumerate(B_GROUPS):
                dq, dk, dv, dbias = mixer_b_group_bwd(s["qkv"], tables[g][0], dos[g], s["lzs"][g], dlts[g], dil, g,
                                                      n_groups)
                pieces += [dq, dk, dv]
                rel_rows.append(bias_bucket_sums(f"b_bias_sums_d{dil}", dbias, tables[g][1]))
            d_rel = d_rel + jnp.concatenate(rel_rows, axis=0)
            dqkv = jnp.concatenate(pieces, axis=1)
            g_qkv = mm_col_dw("b_qkv_dw", s["hn"], dqkv, w_b)
            dhn = mm_col_dx("b_qkv_dx", dqkv, w_qkv)
        grads = [g_qkv, g_o.reshape(N_DEV, -1, d), g_up, g_down.reshape(N_DEV, -1, d)]
        layer_grads[i] = grads if on_grads is None else on_grads(i, grads)
        dh, d_mix[i] = rms_bwd("mix_norm_bwd", s["h_in"], mix_norm[i], dhn, dh)

    d_rel_bias = d_rel[:, :REL_BUCKETS].T
    small_g = [jnp.stack(d_qg), jnp.stack(d_kg), d_rel_bias, jnp.concatenate(d_mix, 0), jnp.concatenate(d_ffn, 0),
               jnp.stack(d_cb), d_final.reshape(-1), jnp.stack(d_cw), loss_part]
    return dh, layer_grads, small_g


def kernel(x, a_w_qkv, a_w_o, a_q_gain, a_k_gain, b_w_qkv, b_w_o, rel_bias, mix_norm, ffn_norm, w_up, conv_w, conv_b, w_down, final_norm, loss_target, m_a_w_qkv, m_a_w_o, m_a_q_gain, m_a_k_gain, m_b_w_qkv, m_b_w_o, m_rel_bias, m_mix_norm, m_ffn_norm, m_w_up, m_conv_w, m_conv_b, m_w_down, m_final_norm, v_a_w_qkv, v_a_w_o, v_a_q_gain, v_a_k_gain, v_b_w_qkv, v_b_w_o, v_rel_bias, v_mix_norm, v_ffn_norm, v_w_up, v_conv_w, v_conv_b, v_w_down, v_final_norm):
    d = x.shape[2]
    depth = mix_norm.shape[0]
    dff = w_down.shape[1] * N_DEV
    w_u = w_up.shape[2]
    mixers = [(a_w_qkv, a_w_o, m_a_w_qkv, m_a_w_o, v_a_w_qkv, v_a_w_o),
              (b_w_qkv, b_w_o, m_b_w_qkv, m_b_w_o, v_b_w_qkv, v_b_w_o)]

    layers = []
    for i in range(depth):
        w_qkv, w_o = mixers[i % 2][0][i // 2], mixers[i % 2][1][i // 2]
        shards = [w_qkv.astype(BF16), w_o.astype(BF16), w_up[i].astype(BF16), w_down[i].astype(BF16), conv_w[i]]
        g_qkv, g_o, g_up, g_down, g_cw = gather_layer(f"gather_l{i}", shards)
        cw = g_cw.transpose(1, 0, 2).reshape(3, 2, dff).transpose(1, 0, 2)
        layers.append((g_qkv, g_o.reshape(-1, d), g_up, g_down.reshape(dff, d), cw))

    core = lax.axis_index("c").astype(jnp.int32).reshape(1)
    chip = (2 * lax.axis_index("x") + lax.axis_index("y")).astype(jnp.int32).reshape(1)

    def reduce_and_update(i, grads):
        w_qkv, w_o, m_qkv, m_o, v_qkv, v_o = mixers[i % 2]
        state = [(w_qkv, m_qkv, v_qkv, i // 2), (w_o, m_o, v_o, i // 2), (w_up, m_w_up, v_w_up, i),
                 (w_down, m_w_down, v_w_down, i)]
        got1 = grads_to_sibling(f"to_sibling_l{i}", grads)
        sums = [chip_sum(f"chip_sum_l{i}_{a}", grads[a], got1[a], core) for a in range(4)]
        got2 = grads_to_chips(f"to_chips_l{i}", sums)
        return [adamw_layer(f"adamw_l{i}_{a}", sums[a], got2[a], *state[a], chip) for a in range(4)]

    dh, updates, small_g = _local_step(x[0], loss_target[0], layers,
                                       (a_q_gain, a_k_gain, rel_bias, mix_norm, ffn_norm, conv_b, final_norm),
                                       reduce_and_update)
    grad_x = dh[None]
    big_out = {}
    for a, nm in enumerate(["qkv", "o", "w_up", "w_down"]):
        for par in range(2 if a < 2 else 1):
            ids = range(par, depth, 2) if a < 2 else range(depth)
            key = ("a_w_" if par == 0 else "b_w_") + nm if a < 2 else nm
            big_out[key] = [jnp.stack([updates[i][a][k] for i in ids]) for k in range(4)]

    width = 2048
    packed = _pack(small_g, N_DEV * width).reshape(-1, N_DEV, width)
    n_rows = packed.shape[0]
    packed = packed.transpose(1, 0, 2).reshape(N_DEV, n_rows * width)
    red = all_reduce_small("small_all_reduce", packed)
    red = red.reshape(N_DEV, n_rows, width).transpose(1, 0, 2)
    (g_qg, g_kg, g_rel, g_mix, g_ffn, g_cb, g_fin, g_cw_all, loss) = _unpack(red, [p.shape for p in small_g])
    idx = 4 * lax.axis_index("x") + 2 * lax.axis_index("y") + lax.axis_index("c")
    g_cw_mine = lax.dynamic_slice_in_dim(g_cw_all, idx * w_u, w_u, axis=2)

    small_w = [a_q_gain, a_k_gain, rel_bias, mix_norm, ffn_norm, conv_b, final_norm, conv_w]
    small_m = [m_a_q_gain, m_a_k_gain, m_rel_bias, m_mix_norm, m_ffn_norm, m_conv_b, m_final_norm, m_conv_w]
    small_v = [v_a_q_gain, v_a_k_gain, v_rel_bias, v_mix_norm, v_ffn_norm, v_conv_b, v_final_norm, v_conv_w]
    small_grads = [g_qg, g_kg, g_rel, g_mix, g_ffn, g_cb, g_fin, g_cw_mine]
    shapes = [w.shape for w in small_w]
    pad_rows = (-_pack(small_w, width).shape[0]) % 8

    def pk8(parts):
        p = _pack(parts, width)
        return jnp.pad(p, ((0, pad_rows), (0, 0))) if pad_rows else p

    sd, sm, sv = adamw_small("adamw_small", pk8(small_grads), pk8(small_w), pk8(small_m), pk8(small_v))
    sd, sm, sv = _unpack(sd, shapes), _unpack(sm, shapes), _unpack(sv, shapes)

    names = ["a_w_qkv", "a_w_o", "a_q_gain", "a_k_gain", "b_w_qkv", "b_w_o", "rel_bias", "mix_norm", "ffn_norm",
             "w_up", "conv_w", "conv_b", "w_down", "final_norm"]
    small_names = ["a_q_gain", "a_k_gain", "rel_bias", "mix_norm", "ffn_norm", "conv_b", "final_norm", "conv_w"]
    grads, deltas, new_m, new_v = {}, {}, {}, {}
    for nm, outs in big_out.items():
        grads[nm], deltas[nm], new_m[nm], new_v[nm] = outs
    for a, nm in enumerate(small_names):
        grads[nm] = small_grads[a].reshape(shapes[a])
        deltas[nm], new_m[nm], new_v[nm] = sd[a], sm[a], sv[a]
    return (loss.reshape(()), grad_x, *[grads[n] for n in names], *[deltas[n] for n in names],
            *[new_m[n] for n in names], *[new_v[n] for n in names])
```

```python
import functools
import math

import jax
import jax.numpy as jnp
from jax import lax
from jax.experimental import pallas as pl
from jax.experimental.pallas import tpu as pltpu
from jax.experimental.pallas import tpu_sc as plsc

F32 = jnp.float32
BF16 = jnp.bfloat16
MESH = pl.DeviceIdType.MESH

N_DEV = 8
LANES = 128
HEAD_DIM = 128
VMEM_LIMIT = 56 * 1024 * 1024
GRID_W = 64
ROPE_THETA = 10000.0
A_KV_HEADS = 4
B_GROUPS = ((128, 1), (512, 4), (2048, 16))
B_HEADS_PER_GROUP = 8
REL_BUCKETS = 32
REL_MAX_DISTANCE = 1024
EPS = 1e-6
NEG_INF = -1e30
ADAM_LR = 0.001
ADAM_B1 = 0.9
ADAM_B2 = 0.999
ADAM_EPS = 1e-08
ADAM_WD = 0.01
ADAM_STEP = 10

ROW_TILE = 256
MM_TM = 1024
MM_TK = 2048
A_BQ = 512
A_BK = 512
B_BQ = 256
ATTN_ROWS = 16
ATTN_SCALE = HEAD_DIM ** -0.5

NN = (((1,), (0,)), ((), ()))
NT = (((1,), (1,)), ((), ()))
TN = (((0,), (0,)), ((), ()))


def _tile(n, pref):
    return pref if n % pref == 0 else n


def _div_tile(n, pref):
    for cand in range(pref - pref % LANES, 0, -LANES):
        if n % cand == 0:
            return cand
    return n


def _params(sem):
    return pltpu.CompilerParams(dimension_semantics=sem, vmem_limit_bytes=VMEM_LIMIT)


def _dot(a, b, dims):
    return lax.dot_general(a, b, dims, preferred_element_type=F32)


def _mm(name, a, b, *, grid, a_blk, a_map, b_blk, b_map, o_blk, o_map, out_shape, out_dtype, dims,
        res=None):
    nk = grid[2]
    acc_shape = tuple(d for d in o_blk if d is not None)

    def body(*refs):
        if res is None:
            a_ref, b_ref, o_ref, acc = refs
            r_ref = None
        else:
            a_ref, b_ref, r_ref, o_ref, acc = refs
        k = pl.program_id(2)

        @pl.when(k == 0)
        def _():
            acc[...] = jnp.zeros_like(acc)

        acc[...] += _dot(a_ref[...].astype(BF16), b_ref[...].astype(BF16), dims)

        @pl.when(k == nk - 1)
        def _():
            r = acc[...]
            if r_ref is not None:
                r = r + r_ref[...]
            o_ref[...] = r.astype(out_dtype)

    in_specs = [pl.BlockSpec(a_blk, a_map), pl.BlockSpec(b_blk, b_map)]
    args = [a, b]
    if res is not None:
        in_specs.append(pl.BlockSpec(o_blk, o_map))
        args.append(res)
    return pl.pallas_call(
        body, name=name, grid=grid, in_specs=in_specs, out_specs=pl.BlockSpec(o_blk, o_map),
        out_shape=jax.ShapeDtypeStruct(out_shape, out_dtype),
        scratch_shapes=[pltpu.VMEM(acc_shape, F32)],
        compiler_params=_params(("parallel", "parallel", "arbitrary")),
    )(*args)


def mm_col_fwd(name, a, wg, out_dtype, split=1):
    m, kdim = a.shape
    n_dev, _, w = wg.shape
    tm, tk = _tile(m, MM_TM), _div_tile(kdim, MM_TK)
    per = n_dev // split
    if split == 1:
        o_blk, o_map, o_shape = (tm, w), (lambda i, j, k: (i, j)), (m, n_dev * w)
    else:
        o_blk, o_map, o_shape = (None, tm, w), (lambda i, j, k: (j // per, i, j % per)), (split, m, per * w)
    return _mm(name, a, wg, grid=(m // tm, n_dev, kdim // tk),
               a_blk=(tm, tk), a_map=lambda i, j, k: (i, k),
               b_blk=(None, tk, w), b_map=lambda i, j, k: (j, k, 0),
               o_blk=o_blk, o_map=o_map, out_shape=o_shape, out_dtype=out_dtype, dims=NN)


def mm_col_dx(name, dy, wg, split=1):
    n_dev, kdim, w = wg.shape
    m = dy.shape[-2]
    tm, tk = _tile(m, MM_TM), _div_tile(kdim, MM_TK)
    per = n_dev // split
    if split == 1:
        a_blk, a_map = (tm, w), (lambda i, j, k: (i, k))
    else:
        a_blk, a_map = (None, tm, w), (lambda i, j, k: (k // per, i, k % per))
    return _mm(name, dy, wg, grid=(m // tm, kdim // tk, n_dev),
               a_blk=a_blk, a_map=a_map,
               b_blk=(None, tk, w), b_map=lambda i, j, k: (k, j, 0),
               o_blk=(tm, tk), o_map=lambda i, j, k: (i, j), out_shape=(m, kdim), out_dtype=F32, dims=NT)


def mm_col_dw(name, x, dy, w, split=1):
    m, kdim = x.shape
    tm, tk = _tile(m, MM_TM), _div_tile(kdim, MM_TK)
    per = N_DEV // split
    if split == 1:
        b_blk, b_map = (tm, w), (lambda i, j, k: (k, j))
    else:
        b_blk, b_map = (None, tm, w), (lambda i, j, k: (j // per, k, j % per))
    return _mm(name, x, dy, grid=(kdim // tk, N_DEV, m // tm),
               a_blk=(tm, tk), a_map=lambda i, j, k: (k, i),
               b_blk=b_blk, b_map=b_map,
               o_blk=(None, tk, w), o_map=lambda i, j, k: (j, i, 0),
               out_shape=(N_DEV, kdim, w), out_dtype=BF16, dims=TN)


def mm_row_fwd(name, a, wg, res):
    m, kdim = a.shape
    n = wg.shape[1]
    tm, tk, tn = _tile(m, MM_TM), _div_tile(kdim, MM_TK), _tile(n, 1024)
    return _mm(name, a, wg, grid=(m // tm, n // tn, kdim // tk),
               a_blk=(tm, tk), a_map=lambda i, j, k: (i, k),
               b_blk=(tk, tn), b_map=lambda i, j, k: (k, j),
               o_blk=(tm, tn), o_map=lambda i, j, k: (i, j), out_shape=(m, n), out_dtype=F32, dims=NN,
               res=res)


def mm_row_dx(name, dy, wg):
    m, n = dy.shape
    kdim = wg.shape[0]
    tm, tk, tn = _tile(m, MM_TM), _div_tile(kdim, MM_TK), _tile(n, 1024)
    return _mm(name, dy, wg, grid=(m // tm, kdim // tk, n // tn),
               a_blk=(tm, tn), a_map=lambda i, j, k: (i, k),
               b_blk=(tk, tn), b_map=lambda i, j, k: (j, k),
               o_blk=(tm, tk), o_map=lambda i, j, k: (i, j), out_shape=(m, kdim), out_dtype=F32, dims=NT)


def mm_row_dw(name, x, dy):
    m, kdim = x.shape
    n = dy.shape[1]
    tm, tk, tn = _tile(m, MM_TM), _div_tile(kdim, MM_TK), _tile(n, 1024)
    return _mm(name, x, dy, grid=(kdim // tk, n // tn, m // tm),
               a_blk=(tm, tk), a_map=lambda i, j, k: (k, i),
               b_blk=(tm, tn), b_map=lambda i, j, k: (k, j),
               o_blk=(tk, tn), o_map=lambda i, j, k: (i, j), out_shape=(kdim, n), out_dtype=BF16, dims=TN)


def _rows(d, tm):
    return pl.BlockSpec((tm, d), lambda i: (i, 0))


def _vec(d):
    return pl.BlockSpec((1, d), lambda i: (0, 0))


def rms_fwd(name, h, gain):
    t, d = h.shape
    tm = _tile(t, ROW_TILE)

    def body(h_ref, g_ref, o_ref):
        x = h_ref[...]
        rstd = lax.rsqrt(jnp.mean(x * x, axis=-1, keepdims=True) + EPS)
        o_ref[...] = (x * rstd * g_ref[...]).astype(BF16)

    return pl.pallas_call(
        body, name=name, grid=(t // tm,), in_specs=[_rows(d, tm), _vec(d)], out_specs=_rows(d, tm),
        out_shape=jax.ShapeDtypeStruct((t, d), BF16), compiler_params=_params(("parallel",)),
    )(h, gain.reshape(1, d))


def rms_bwd(name, h, gain, dy, dres):
    t, d = h.shape
    tm = _tile(t, ROW_TILE)

    def body(h_ref, g_ref, dy_ref, r_ref, dh_ref, dg_ref):
        @pl.when(pl.program_id(0) == 0)
        def _():
            dg_ref[...] = jnp.zeros_like(dg_ref)

        x = h_ref[...]
        rstd = lax.rsqrt(jnp.mean(x * x, axis=-1, keepdims=True) + EPS)
        xhat = x * rstd
        dyv = dy_ref[...]
        dxhat = dyv * g_ref[...]
        dh_ref[...] = r_ref[...] + rstd * (dxhat - xhat * jnp.mean(dxhat * xhat, axis=-1, keepdims=True))
        dg_ref[...] += jnp.sum(dyv * xhat, axis=0, keepdims=True)

    return pl.pallas_call(
        body, name=name, grid=(t // tm,),
        in_specs=[_rows(d, tm), _vec(d), _rows(d, tm), _rows(d, tm)],
        out_specs=[_rows(d, tm), _vec(d)],
        out_shape=[jax.ShapeDtypeStruct((t, d), F32), jax.ShapeDtypeStruct((1, d), F32)],
        compiler_params=_params(("arbitrary",)),
    )(h, gain.reshape(1, d), dy, dres)


def loss_head(name, h, gain, target):
    t, d = h.shape
    tm = _tile(t, ROW_TILE)

    def body(h_ref, g_ref, t_ref, dh_ref, dg_ref, loss_ref):
        @pl.when(pl.program_id(0) == 0)
        def _():
            dg_ref[...] = jnp.zeros_like(dg_ref)
            loss_ref[...] = jnp.zeros_like(loss_ref)

        x = h_ref[...]
        rstd = lax.rsqrt(jnp.mean(x * x, axis=-1, keepdims=True) + EPS)
        xhat = x * rstd
        err = xhat * g_ref[...] - t_ref[...]
        row = jnp.mean(err * err, axis=-1, keepdims=True)
        loss_ref[...] += 0.5 * jnp.sum(row, axis=0, keepdims=True)
        dyv = err * (1.0 / d)
        dxhat = dyv * g_ref[...]
        dh_ref[...] = rstd * (dxhat - xhat * jnp.mean(dxhat * xhat, axis=-1, keepdims=True))
        dg_ref[...] += jnp.sum(dyv * xhat, axis=0, keepdims=True)

    return pl.pallas_call(
        body, name=name, grid=(t // tm,),
        in_specs=[_rows(d, tm), _vec(d), _rows(d, tm)],
        out_specs=[_rows(d, tm), _vec(d), pl.BlockSpec((1, 1), lambda i: (0, 0))],
        out_shape=[jax.ShapeDtypeStruct((t, d), F32), jax.ShapeDtypeStruct((1, d), F32),
                   jax.ShapeDtypeStruct((1, 1), F32)],
        compiler_params=_params(("arbitrary",)),
    )(h, gain.reshape(1, d), target)


def rope_tables(seq):
    pos = jnp.arange(seq, dtype=jnp.int32)
    row_ids = (pos // GRID_W).astype(F32)
    col_ids = (pos % GRID_W).astype(F32)
    quarter = HEAD_DIM // 4
    inv_freq = ROPE_THETA ** (-jnp.arange(quarter, dtype=F32) / quarter)
    ar = row_ids[:, None] * inv_freq[None, :]
    ac = col_ids[:, None] * inv_freq[None, :]
    cos = jnp.concatenate([jnp.cos(ar), jnp.cos(ar), jnp.cos(ac), jnp.cos(ac)], axis=-1)
    sin = jnp.concatenate([-jnp.sin(ar), jnp.sin(ar), -jnp.sin(ac), jnp.sin(ac)], axis=-1)
    return cos, sin


def _swap_quarters(x):
    lane = lax.broadcasted_iota(jnp.int32, x.shape, 1)
    q = HEAD_DIM // 4
    return jnp.where((lane % (2 * q)) < q, pltpu.roll(x, HEAD_DIM - q, 1), pltpu.roll(x, q, 1))


def qk_prep_fwd(name, qkv, q_gain, k_gain, cos, sin, n_q, n_kv):
    t, width = qkv.shape
    tm = _tile(t, ROW_TILE)

    def body(x_ref, qg_ref, kg_ref, c_ref, s_ref, o_ref):
        c, s = c_ref[...], s_ref[...]
        for hd in range(n_q + n_kv):
            sl = slice(hd * HEAD_DIM, (hd + 1) * HEAD_DIM)
            x = x_ref[:, sl]
            g = qg_ref[...] if hd < n_q else kg_ref[...]
            xn = x * lax.rsqrt(jnp.mean(x * x, axis=-1, keepdims=True) + EPS) * g
            o_ref[:, sl] = (xn * c + _swap_quarters(xn) * s).astype(BF16)
        vs = slice((n_q + n_kv) * HEAD_DIM, width)
        o_ref[:, vs] = x_ref[:, vs].astype(BF16)

    return pl.pallas_call(
        body, name=name, grid=(t // tm,),
        in_specs=[_rows(width, tm), _vec(HEAD_DIM), _vec(HEAD_DIM), _rows(HEAD_DIM, tm), _rows(HEAD_DIM, tm)],
        out_specs=_rows(width, tm), out_shape=jax.ShapeDtypeStruct((t, width), BF16),
        compiler_params=_params(("parallel",)),
    )(qkv, q_gain.reshape(1, HEAD_DIM), k_gain.reshape(1, HEAD_DIM), cos, sin)


def qk_prep_bwd(name, qkv, dq, dk, dv, q_gain, k_gain, cos, sin, n_q, n_kv):
    t, width = qkv.shape
    tm = _tile(t, ROW_TILE)

    def body(x_ref, dq_ref, dk_ref, dv_ref, qg_ref, kg_ref, c_ref, s_ref, o_ref, dg_ref):
        @pl.when(pl.program_id(0) == 0)
        def _():
            dg_ref[...] = jnp.zeros_like(dg_ref)

        c, s = c_ref[...], s_ref[...]
        dgq = jnp.zeros((1, HEAD_DIM), F32)
        dgk = jnp.zeros((1, HEAD_DIM), F32)
        for hd in range(n_q + n_kv):
            sl = slice(hd * HEAD_DIM, (hd + 1) * HEAD_DIM)
            x = x_ref[:, sl]
            if hd < n_q:
                g, dout = qg_ref[...], dq_ref[:, sl]
            else:
                ks = slice((hd - n_q) * HEAD_DIM, (hd - n_q + 1) * HEAD_DIM)
                g, dout = kg_ref[...], dk_ref[:, ks]
            rstd = lax.rsqrt(jnp.mean(x * x, axis=-1, keepdims=True) + EPS)
            xhat = x * rstd
            dxn = dout * c + _swap_quarters(dout * s)
            part = jnp.sum(dxn * xhat, axis=0, keepdims=True)
            if hd < n_q:
                dgq = dgq + part
            else:
                dgk = dgk + part
            dxhat = dxn * g
            o_ref[:, sl] = (rstd * (dxhat - xhat * jnp.mean(dxhat * xhat, axis=-1, keepdims=True))).astype(BF16)
        o_ref[:, slice((n_q + n_kv) * HEAD_DIM, width)] = dv_ref[...].astype(BF16)
        dg_ref[0:1, :] += dgq
        dg_ref[1:2, :] += dgk

    kvw = n_kv * HEAD_DIM
    return pl.pallas_call(
        body, name=name, grid=(t // tm,),
        in_specs=[_rows(width, tm), _rows(n_q * HEAD_DIM, tm), _rows(kvw, tm), _rows(kvw, tm),
                  _vec(HEAD_DIM), _vec(HEAD_DIM), _rows(HEAD_DIM, tm), _rows(HEAD_DIM, tm)],
        out_specs=[_rows(width, tm), pl.BlockSpec((2, HEAD_DIM), lambda i: (0, 0))],
        out_shape=[jax.ShapeDtypeStruct((t, width), BF16), jax.ShapeDtypeStruct((2, HEAD_DIM), F32)],
        compiler_params=_params(("arbitrary",)),
    )(qkv, dq, dk, dv, q_gain.reshape(1, HEAD_DIM), k_gain.reshape(1, HEAD_DIM), cos, sin)


def _lanes(x, width):
    return jnp.tile(x, (1, width // LANES))


def _hs(hd):
    return slice(hd * HEAD_DIM, (hd + 1) * HEAD_DIM)


def attn_fwd(name, q, k, v, bias, *, grid, q_spec, k_spec, v_spec, b_spec, o_spec, valid, nh, shared_kv,
             bq, bk, o_shape, o_dtype):
    ns = grid[2]

    def body(*refs):
        if bias is None:
            q_ref, k_ref, v_ref, o_ref, lse_ref, m_s, l_s, acc_s = refs
            b_ref = None
        else:
            q_ref, k_ref, v_ref, b_ref, o_ref, lse_ref, m_s, l_s, acc_s = refs
        step = pl.program_id(2)

        @pl.when(step == 0)
        def _():
            m_s[...] = jnp.full_like(m_s, -jnp.inf)
            l_s[...] = jnp.zeros_like(l_s)
            acc_s[...] = jnp.zeros_like(acc_s)

        @pl.when(valid(pl.program_id(1), step))
        def _():
            for hd in range(nh):
                kh = _hs(0 if shared_kv else hd)
                s = _dot(q_ref[:, _hs(hd)], k_ref[:, kh], NT)
                p_rows, a_rows = [], []
                for r0 in range(0, bq, ATTN_ROWS):
                    rows = slice(r0, r0 + ATTN_ROWS)
                    z = s[rows] * ATTN_SCALE
                    if b_ref is not None:
                        z = z + b_ref[hd, rows, :]
                    m_prev = m_s[hd, rows, :]
                    m_new = jnp.maximum(m_prev, jnp.max(z, axis=-1, keepdims=True))
                    alpha = jnp.exp(m_prev - m_new)
                    p = jnp.exp(z - _lanes(m_new, bk))
                    l_s[hd, rows, :] = alpha * l_s[hd, rows, :] + jnp.sum(p, axis=-1, keepdims=True)
                    m_s[hd, rows, :] = m_new
                    p_rows.append(p.astype(BF16))
                    a_rows.append(alpha)
                pv = _dot(jnp.concatenate(p_rows, axis=0), v_ref[:, kh], NN)
                acc_s[hd] = jnp.concatenate(a_rows, axis=0) * acc_s[hd] + pv

        @pl.when(step == ns - 1)
        def _():
            for hd in range(nh):
                o_ref[:, _hs(hd)] = (acc_s[hd] / l_s[hd]).astype(o_dtype)
                lse_ref[:, _hs(hd)] = m_s[hd] + jnp.log(l_s[hd])

    in_specs = [q_spec, k_spec, v_spec] + ([] if bias is None else [b_spec])
    args = [q, k, v] + ([] if bias is None else [bias])
    stat = pltpu.VMEM((nh, bq, LANES), F32)
    return pl.pallas_call(
        body, name=name, grid=grid, in_specs=in_specs, out_specs=[o_spec, o_spec],
        out_shape=[jax.ShapeDtypeStruct(o_shape, o_dtype), jax.ShapeDtypeStruct(o_shape, F32)],
        scratch_shapes=[stat, stat, stat],
        compiler_params=_params(("parallel", "parallel", "arbitrary")),
    )(*args)


def _probs(q_ref, k_ref, v_ref, do_ref, lse_ref, dlt_ref, b_ref, hd, kh, bq, bk, want_p=True, on_ds=None):
    s = _dot(q_ref[:, _hs(hd)], k_ref[:, kh], NT)
    dp = _dot(do_ref[:, _hs(hd)], v_ref[:, kh], NT)
    p_rows, ds_rows = [], []
    for r0 in range(0, bq, ATTN_ROWS):
        rows = slice(r0, r0 + ATTN_ROWS)
        z = s[rows] * ATTN_SCALE
        if b_ref is not None:
            z = z + b_ref[hd, rows, :]
        p = jnp.exp(z - _lanes(lse_ref[rows, _hs(hd)], bk))
        ds = p * (dp[rows] - _lanes(dlt_ref[rows, _hs(hd)], bk))
        if on_ds is not None:
            on_ds(rows, ds)
        if want_p:
            p_rows.append(p.astype(BF16))
        ds_rows.append(ds.astype(BF16))
    return (jnp.concatenate(p_rows, axis=0) if want_p else None), jnp.concatenate(ds_rows, axis=0)


def attn_bwd_dq(name, q, k, v, do, lse, dlt, *, grid, q_spec, k_spec, v_spec, nh, bq, bk, o_shape):
    ns = grid[2]
    scale = HEAD_DIM ** -0.5

    def body(q_ref, k_ref, v_ref, do_ref, lse_ref, dlt_ref, dq_ref, acc_s):
        step = pl.program_id(2)

        @pl.when(step == 0)
        def _():
            acc_s[...] = jnp.zeros_like(acc_s)

        for hd in range(nh):
            _, ds = _probs(q_ref, k_ref, v_ref, do_ref, lse_ref, dlt_ref, None, hd, _hs(0), bq, bk, want_p=False)
            acc_s[hd] += _dot(ds, k_ref[:, _hs(0)], NN)

        @pl.when(step == ns - 1)
        def _():
            for hd in range(nh):
                dq_ref[:, _hs(hd)] = acc_s[hd] * scale

    return pl.pallas_call(
        body, name=name, grid=grid, in_specs=[q_spec, k_spec, v_spec, q_spec, q_spec, q_spec],
        out_specs=q_spec, out_shape=jax.ShapeDtypeStruct(o_shape, F32),
        scratch_shapes=[pltpu.VMEM((nh, bq, LANES), F32)],
        compiler_params=_params(("parallel", "parallel", "arbitrary")),
    )(q, k, v, do, lse, dlt)


def _always(i, s):
    return s >= 0


def row_delta(name, do, o, n_heads):
    t, width = do.shape
    tm = _tile(t, ROW_TILE)

    def body(do_ref, o_ref, dl_ref, dob_ref):
        for hd in range(n_heads):
            d = do_ref[:, _hs(hd)]
            s = jnp.sum(d * o_ref[:, _hs(hd)].astype(F32), axis=-1, keepdims=True)
            dl_ref[:, _hs(hd)] = jnp.broadcast_to(s, (tm, HEAD_DIM))
            dob_ref[:, _hs(hd)] = d.astype(BF16)

    return pl.pallas_call(
        body, name=name, grid=(t // tm,), in_specs=[_rows(width, tm), _rows(width, tm)],
        out_specs=[_rows(width, tm), _rows(width, tm)],
        out_shape=[jax.ShapeDtypeStruct((t, width), F32), jax.ShapeDtypeStruct((t, width), BF16)],
        compiler_params=_params(("parallel",)),
    )(do, o)


def _a_specs(n_q, n_kv, bq, bk, q_major):
    grp = n_q // n_kv
    if q_major:
        qm, km = (lambda b, i, s: (i, b)), (lambda b, i, s: (s, n_q + b))
        vm = lambda b, i, s: (s, n_q + n_kv + b)
    else:
        qm, km = (lambda b, i, s: (s, b)), (lambda b, i, s: (i, n_q + b))
        vm = lambda b, i, s: (i, n_q + n_kv + b)
    return (pl.BlockSpec((bq, grp * HEAD_DIM), qm), pl.BlockSpec((bk, HEAD_DIM), km),
            pl.BlockSpec((bk, HEAD_DIM), vm))


def mixer_a_fwd(qkv_r, n_q, n_kv):
    t = qkv_r.shape[0]
    bq, bk = _tile(t, A_BQ), _tile(t, A_BK)
    q_spec, k_spec, v_spec = _a_specs(n_q, n_kv, bq, bk, True)
    return attn_fwd("a_attn_fwd", qkv_r, qkv_r, qkv_r, None, grid=(n_kv, t // bq, t // bk),
                    q_spec=q_spec, k_spec=k_spec, v_spec=v_spec, b_spec=None, o_spec=q_spec, valid=_always,
                    nh=n_q // n_kv, shared_kv=True, bq=bq, bk=bk, o_shape=(t, n_q * HEAD_DIM), o_dtype=BF16)


def mixer_a_bwd(qkv_r, do_b, lse, dlt, n_q, n_kv):
    t = qkv_r.shape[0]
    bq, bk = _tile(t, A_BQ), _tile(t, A_BK)
    grp = n_q // n_kv
    q_spec, k_spec, v_spec = _a_specs(n_q, n_kv, bq, bk, True)
    dq = attn_bwd_dq("a_attn_dq", qkv_r, qkv_r, qkv_r, do_b, lse, dlt, grid=(n_kv, t // bq, t // bk),
                     q_spec=q_spec, k_spec=k_spec, v_spec=v_spec, nh=grp, bq=bq, bk=bk,
                     o_shape=(t, n_q * HEAD_DIM))
    q_spec, k_spec, v_spec = _a_specs(n_q, n_kv, bq, bk, False)
    o_spec = pl.BlockSpec((bk, HEAD_DIM), lambda b, i, s: (i, b))
    dk, dv = _attn_bwd_dkv_out(qkv_r, do_b, lse, dlt, grid=(n_kv, t // bk, t // bq), q_spec=q_spec,
                               k_spec=k_spec, v_spec=v_spec, o_spec=o_spec, grp=grp, bq=bq, bk=bk,
                               o_shape=(t, n_kv * HEAD_DIM))
    return dq, dk, dv


def _attn_bwd_dkv_out(qkv_r, do_b, lse, dlt, *, grid, q_spec, k_spec, v_spec, o_spec, grp, bq, bk, o_shape):
    ns = grid[2]
    scale = HEAD_DIM ** -0.5

    def body(q_ref, k_ref, v_ref, do_ref, lse_ref, dlt_ref, dk_ref, dv_ref, dk_s, dv_s):
        step = pl.program_id(2)

        @pl.when(step == 0)
        def _():
            dk_s[...] = jnp.zeros_like(dk_s)
            dv_s[...] = jnp.zeros_like(dv_s)

        for hd in range(grp):
            p, ds = _probs(q_ref, k_ref, v_ref, do_ref, lse_ref, dlt_ref, None, hd, _hs(0), bq, bk)
            dv_s[...] += _dot(p, do_ref[:, _hs(hd)], TN)
            dk_s[...] += _dot(ds, q_ref[:, _hs(hd)], TN)

        @pl.when(step == ns - 1)
        def _():
            dk_ref[...] = dk_s[...] * scale
            dv_ref[...] = dv_s[...]

    acc = pltpu.VMEM((bk, HEAD_DIM), F32)
    return pl.pallas_call(
        body, name="a_attn_dkv", grid=grid, in_specs=[q_spec, k_spec, v_spec, q_spec, q_spec, q_spec],
        out_specs=[o_spec, o_spec], out_shape=[jax.ShapeDtypeStruct(o_shape, F32)] * 2,
        scratch_shapes=[acc, acc], compiler_params=_params(("parallel", "parallel", "arbitrary")),
    )(qkv_r, qkv_r, qkv_r, do_b, lse, dlt)


def t5_bucket(rel):
    nb = REL_BUCKETS // 2
    max_exact = nb // 2
    base = jnp.where(rel > 0, nb, 0)
    n = jnp.abs(rel)
    nf = jnp.maximum(n, 1).astype(F32)
    large = max_exact + (jnp.log(nf / max_exact) / math.log(REL_MAX_DISTANCE / max_exact)
                         * (nb - max_exact)).astype(jnp.int32)
    large = jnp.minimum(large, nb - 1)
    return base + jnp.where(n < max_exact, n, large)


def band_tables(rel_bias_g, half_span, dil, bq):
    a = jnp.arange(bq)[:, None]
    b = jnp.arange(bq)[None, :]
    rel = jnp.stack([(s - 1) * bq + b - a for s in range(3)])
    ok = jnp.abs(rel) <= half_span
    bucket = t5_bucket(rel * dil)
    bias = jnp.zeros((rel_bias_g.shape[1],) + rel.shape, F32)
    for r in range(REL_BUCKETS):
        bias = bias + jnp.where(bucket[None] == r, rel_bias_g[r][:, None, None, None], 0.0)
    return jnp.where(ok[None], bias, NEG_INF), jnp.where(ok, bucket, -1).astype(jnp.int32)


def _b_geometry(t, dil, g, n_groups):
    hg = B_HEADS_PER_GROUP
    length = t // dil
    bq = _tile(length, B_BQ)
    nblk = length // bq
    gw = hg * HEAD_DIM
    per_tok = 3 * n_groups
    return hg, length, bq, nblk, gw, per_tok


def mixer_b_group_fwd(qkv, bias, dil, g, n_groups):
    t = qkv.shape[0]
    hg, length, bq, nblk, gw, per_tok = _b_geometry(t, dil, g, n_groups)
    view = qkv.reshape(length, dil * qkv.shape[1])
    col = lambda c, which: c * per_tok + 3 * g + which
    kblk = lambda i, s: jnp.clip(i - 1 + s, 0, nblk - 1)
    spec = lambda which, streamed: pl.BlockSpec(
        (bq, gw), (lambda c, i, s: (kblk(i, s), col(c, which))) if streamed else (lambda c, i, s: (i, col(c, which))))
    valid = lambda i, s: (i - 1 + s >= 0) & (i - 1 + s < nblk)
    o, lz = attn_fwd(f"b_attn_fwd_d{dil}", view, view, view, bias, grid=(dil, nblk, 3),
                     q_spec=spec(0, False), k_spec=spec(1, True), v_spec=spec(2, True),
                     b_spec=pl.BlockSpec((hg, None, bq, bq), lambda c, i, s: (0, s, 0, 0)),
                     o_spec=pl.BlockSpec((bq, gw), lambda c, i, s: (i, c)), valid=valid, nh=hg,
                     shared_kv=False, bq=bq, bk=bq, o_shape=(length, dil * gw), o_dtype=F32)
    return o.reshape(t, gw), lz.reshape(t, gw)


def mixer_b_group_bwd(qkv, bias, do_g, lz_g, dlt_g, dil, g, n_groups):
    t = qkv.shape[0]
    hg, length, bq, nblk, gw, per_tok = _b_geometry(t, dil, g, n_groups)
    view = qkv.reshape(length, dil * qkv.shape[1])
    dov, lzv, dlv = (x.reshape(length, dil * gw) for x in (do_g, lz_g, dlt_g))
    col = lambda c, which: c * per_tok + 3 * g + which
    nbr = lambda i, s: jnp.clip(i - 1 + s, 0, nblk - 1)
    valid = lambda i, s: (i - 1 + s >= 0) & (i - 1 + s < nblk)
    q_spec = pl.BlockSpec((bq, gw), lambda c, i, s: (i, col(c, 0)))
    k_spec = pl.BlockSpec((bq, gw), lambda c, i, s: (nbr(i, s), col(c, 1)))
    v_spec = pl.BlockSpec((bq, gw), lambda c, i, s: (nbr(i, s), col(c, 2)))
    stat = pl.BlockSpec((bq, gw), lambda c, i, s: (i, c))
    dq, dbias = _band_bwd_dq(f"b_attn_dq_d{dil}", view, dov, lzv, dlv, bias, grid=(dil, nblk, 3),
                             q_spec=q_spec, k_spec=k_spec, v_spec=v_spec, stat_spec=stat,
                             b_spec=pl.BlockSpec((hg, None, bq, bq), lambda c, i, s: (0, s, 0, 0)),
                             valid=valid, nh=hg, bq=bq, o_shape=(length, dil * gw))
    q_spec = pl.BlockSpec((bq, gw), lambda c, i, s: (nbr(i, s), col(c, 0)))
    k_spec = pl.BlockSpec((bq, gw), lambda c, i, s: (i, col(c, 1)))
    v_spec = pl.BlockSpec((bq, gw), lambda c, i, s: (i, col(c, 2)))
    stat = pl.BlockSpec((bq, gw), lambda c, i, s: (nbr(i, s), c))
    dk, dv = _band_bwd_dkv(f"b_attn_dkv_d{dil}", view, dov, lzv, dlv, bias, grid=(dil, nblk, 3),
                           q_spec=q_spec, k_spec=k_spec, v_spec=v_spec, stat_spec=stat,
                           b_spec=pl.BlockSpec((hg, None, bq, bq), lambda c, i, s: (0, 2 - s, 0, 0)),
                           o_spec=pl.BlockSpec((bq, gw), lambda c, i, s: (i, c)),
                           valid=valid, nh=hg, bq=bq, o_shape=(length, dil * gw))
    return dq.reshape(t, gw), dk.reshape(t, gw), dv.reshape(t, gw), dbias


def _band_bwd_dq(name, view, do, lse, dlt, bias, *, grid, q_spec, k_spec, v_spec, stat_spec, b_spec, valid,
                 nh, bq, o_shape):
    scale = HEAD_DIM ** -0.5
    bias_shape = (nh, 3, bq, bq)

    def body(q_ref, k_ref, v_ref, do_ref, lse_ref, dlt_ref, b_ref, dq_ref, db_ref, acc_s):
        step = pl.program_id(2)

        @pl.when((pl.program_id(0) == 0) & (pl.program_id(1) == 0) & (step == 0))
        def _():
            db_ref[...] = jnp.zeros_like(db_ref)

        @pl.when(step == 0)
        def _():
            acc_s[...] = jnp.zeros_like(acc_s)

        @pl.when(valid(pl.program_id(1), step))
        def _():
            for hd in range(nh):
                def add_bias_grad(rows, ds, hd=hd):
                    db_ref[hd, step, rows, :] += ds

                _, ds = _probs(q_ref, k_ref, v_ref, do_ref, lse_ref, dlt_ref, b_ref, hd, _hs(hd), bq, bq,
                               want_p=False, on_ds=add_bias_grad)
                acc_s[hd] += _dot(ds, k_ref[:, _hs(hd)], NN)

        @pl.when(step == 2)
        def _():
            for hd in range(nh):
                dq_ref[:, _hs(hd)] = (acc_s[hd] * scale).astype(BF16)

    return pl.pallas_call(
        body, name=name, grid=grid,
        in_specs=[q_spec, k_spec, v_spec, stat_spec, stat_spec, stat_spec, b_spec],
        out_specs=[stat_spec, pl.BlockSpec(bias_shape, lambda c, i, s: (0, 0, 0, 0))],
        out_shape=[jax.ShapeDtypeStruct(o_shape, BF16), jax.ShapeDtypeStruct(bias_shape, F32)],
        scratch_shapes=[pltpu.VMEM((nh, bq, LANES), F32)], compiler_params=_params(("arbitrary",) * 3),
    )(view, view, view, do, lse, dlt, bias)


def _band_bwd_dkv(name, view, do, lse, dlt, bias, *, grid, q_spec, k_spec, v_spec, stat_spec, b_spec, o_spec,
                  valid, nh, bq, o_shape):
    scale = HEAD_DIM ** -0.5

    def body(q_ref, k_ref, v_ref, do_ref, lse_ref, dlt_ref, b_ref, dk_ref, dv_ref, dk_s, dv_s):
        step = pl.program_id(2)

        @pl.when(step == 0)
        def _():
            dk_s[...] = jnp.zeros_like(dk_s)
            dv_s[...] = jnp.zeros_like(dv_s)

        @pl.when(valid(pl.program_id(1), step))
        def _():
            for hd in range(nh):
                p, ds = _probs(q_ref, k_ref, v_ref, do_ref, lse_ref, dlt_ref, b_ref, hd, _hs(hd), bq, bq)
                dv_s[hd] += _dot(p, do_ref[:, _hs(hd)], TN)
                dk_s[hd] += _dot(ds, q_ref[:, _hs(hd)], TN)

        @pl.when(step == 2)
        def _():
            for hd in range(nh):
                dk_ref[:, _hs(hd)] = (dk_s[hd] * scale).astype(BF16)
                dv_ref[:, _hs(hd)] = dv_s[hd].astype(BF16)

    acc = pltpu.VMEM((nh, bq, LANES), F32)
    return pl.pallas_call(
        body, name=name, grid=grid,
        in_specs=[q_spec, k_spec, v_spec, stat_spec, stat_spec, stat_spec, b_spec],
        out_specs=[o_spec, o_spec], out_shape=[jax.ShapeDtypeStruct(o_shape, BF16)] * 2,
        scratch_shapes=[acc, acc], compiler_params=_params(("parallel", "parallel", "arbitrary")),
    )(view, view, view, do, lse, dlt, bias)


def bias_bucket_sums(name, dbias, bucket):
    nh, _, bq, _ = dbias.shape
    db2 = dbias.reshape(nh, 3 * bq, bq)
    bk2 = bucket.reshape(3 * bq, bq)

    def body(db_ref, bk_ref, o_ref):
        row = lax.broadcasted_iota(jnp.int32, (nh, LANES), 0)
        lane = lax.broadcasted_iota(jnp.int32, (nh, LANES), 1)
        out = jnp.zeros((nh, LANES), F32)
        bkt = bk_ref[...]
        for hd in range(nh):
            x = db_ref[hd]
            for r in range(REL_BUCKETS):
                part = jnp.sum(jnp.where(bkt == r, x, 0.0), axis=1, keepdims=True)
                tot = jnp.sum(part, axis=0, keepdims=True)
                out = out + jnp.where((row == hd) & (lane == r), tot, 0.0)
        o_ref[...] = out

    return pl.pallas_call(
        body, name=name, out_shape=jax.ShapeDtypeStruct((nh, LANES), F32),
        compiler_params=pltpu.CompilerParams(vmem_limit_bytes=VMEM_LIMIT),
    )(db2, bk2)


def combine_fwd(name, outs, lzs):
    n_g = len(outs)
    t, gw = outs[0].shape
    tm = _tile(t, ROW_TILE)

    def body(*refs):
        o_refs, lz_refs, y_ref = refs[:n_g], refs[n_g:2 * n_g], refs[2 * n_g]
        lz = [r[...] for r in lz_refs]
        mx = functools.reduce(jnp.maximum, lz)
        e = [jnp.exp(x - mx) for x in lz]
        den = functools.reduce(lambda a, b: a + b, e)
        for g in range(n_g):
            y_ref[:, g * gw:(g + 1) * gw] = (e[g] / den * o_refs[g][...]).astype(BF16)

    return pl.pallas_call(
        body, name=name, grid=(t // tm,), in_specs=[_rows(gw, tm)] * (2 * n_g), out_specs=_rows(n_g * gw, tm),
        out_shape=jax.ShapeDtypeStruct((t, n_g * gw), BF16), compiler_params=_params(("parallel",)),
    )(*outs, *lzs)


def combine_bwd(name, dy, outs, lzs):
    n_g = len(outs)
    t, gw = outs[0].shape
    tm = _tile(t, ROW_TILE)
    nh = gw // HEAD_DIM

    def body(*refs):
        dy_ref = refs[0]
        o_refs, lz_refs = refs[1:1 + n_g], refs[1 + n_g:1 + 2 * n_g]
        do_refs, dl_refs = refs[1 + 2 * n_g:1 + 3 * n_g], refs[1 + 3 * n_g:]
        lz = [r[...] for r in lz_refs]
        mx = functools.reduce(jnp.maximum, lz)
        e = [jnp.exp(x - mx) for x in lz]
        den = functools.reduce(lambda a, b: a + b, e)
        wts = [x / den for x in e]
        for g in range(n_g):
            do_refs[g][...] = (wts[g] * dy_ref[:, g * gw:(g + 1) * gw]).astype(BF16)
        for hd in range(nh):
            mix = jnp.zeros((tm, HEAD_DIM), F32)
            for g in range(n_g):
                prod = dy_ref[:, g * gw + hd * HEAD_DIM:g * gw + (hd + 1) * HEAD_DIM] * o_refs[g][:, _hs(hd)]
                dw = jnp.broadcast_to(jnp.sum(prod, axis=-1, keepdims=True), (tm, HEAD_DIM))
                mix = mix + wts[g][:, _hs(hd)] * dw
            for g in range(n_g):
                dl_refs[g][:, _hs(hd)] = wts[g][:, _hs(hd)] * mix

    return pl.pallas_call(
        body, name=name, grid=(t // tm,),
        in_specs=[_rows(n_g * gw, tm)] + [_rows(gw, tm)] * (2 * n_g),
        out_specs=[_rows(gw, tm)] * (2 * n_g),
        out_shape=[jax.ShapeDtypeStruct((t, gw), BF16)] * n_g + [jax.ShapeDtypeStruct((t, gw), F32)] * n_g,
        compiler_params=_params(("parallel",)),
    )(dy, *outs, *lzs)


def _conv3(u, w_ref, b):
    t = u.shape[0]
    row = lax.broadcasted_iota(jnp.int32, u.shape, 0)
    prev = jnp.where(row == 0, 0.0, pltpu.roll(u, 1, 0))
    nxt = jnp.where(row == t - 1, 0.0, pltpu.roll(u, t - 1, 0))
    out = w_ref[0:1, :] * prev + w_ref[1:2, :] * u + w_ref[2:3, :] * nxt
    return out if b is None else out + b


def _conv3_t(d, w_ref):
    t = d.shape[0]
    row = lax.broadcasted_iota(jnp.int32, d.shape, 0)
    prev = jnp.where(row == 0, 0.0, pltpu.roll(d, 1, 0))
    nxt = jnp.where(row == t - 1, 0.0, pltpu.roll(d, t - 1, 0))
    return w_ref[0:1, :] * nxt + w_ref[1:2, :] * d + w_ref[2:3, :] * prev


def conv_act_fwd(name, u2, cw2, cb2):
    _, t, dff = u2.shape
    tn = LANES

    def body(u_ref, w_ref, b_ref, o_ref):
        cg = _conv3(u_ref[0], w_ref.at[0], b_ref[0])
        cv = _conv3(u_ref[1], w_ref.at[1], b_ref[1])
        o_ref[...] = (cg * jax.nn.sigmoid(cg) * cv).astype(BF16)

    return pl.pallas_call(
        body, name=name, grid=(dff // tn,),
        in_specs=[pl.BlockSpec((2, t, tn), lambda j: (0, 0, j)), pl.BlockSpec((2, 3, tn), lambda j: (0, 0, j)),
                  pl.BlockSpec((2, 1, tn), lambda j: (0, 0, j))],
        out_specs=pl.BlockSpec((t, tn), lambda j: (0, j)), out_shape=jax.ShapeDtypeStruct((t, dff), BF16),
        compiler_params=_params(("parallel",)),
    )(u2, cw2, cb2)


def conv_act_bwd(name, u2, cw2, cb2, dact):
    _, t, dff = u2.shape
    tn = LANES

    def body(u_ref, w_ref, b_ref, d_ref, du_ref, dw_ref):
        d = d_ref[...]
        ug, uv = u_ref[0], u_ref[1]
        cg = _conv3(ug, w_ref.at[0], b_ref[0])
        cv = _conv3(uv, w_ref.at[1], b_ref[1])
        sg = jax.nn.sigmoid(cg)
        dcv = d * (cg * sg)
        dcg = d * cv * (sg * (1.0 + cg * (1.0 - sg)))
        du_ref[0] = _conv3_t(dcg, w_ref.at[0]).astype(BF16)
        du_ref[1] = _conv3_t(dcv, w_ref.at[1]).astype(BF16)
        row = lax.broadcasted_iota(jnp.int32, ug.shape, 0)
        for half, (dc, u) in enumerate(((dcg, ug), (dcv, uv))):
            prev = jnp.where(row == 0, 0.0, pltpu.roll(u, 1, 0))
            nxt = jnp.where(row == t - 1, 0.0, pltpu.roll(u, t - 1, 0))
            for tap, x in enumerate((prev, u, nxt)):
                dw_ref[half, tap:tap + 1, :] = jnp.sum(dc * x, axis=0, keepdims=True)
            dw_ref[half, 3:4, :] = jnp.sum(dc, axis=0, keepdims=True)
            dw_ref[half, 4:8, :] = jnp.zeros((4, tn), F32)

    return pl.pallas_call(
        body, name=name, grid=(dff // tn,),
        in_specs=[pl.BlockSpec((2, t, tn), lambda j: (0, 0, j)), pl.BlockSpec((2, 3, tn), lambda j: (0, 0, j)),
                  pl.BlockSpec((2, 1, tn), lambda j: (0, 0, j)), pl.BlockSpec((t, tn), lambda j: (0, j))],
        out_specs=[pl.BlockSpec((2, t, tn), lambda j: (0, 0, j)), pl.BlockSpec((2, 8, tn), lambda j: (0, 0, j))],
        out_shape=[jax.ShapeDtypeStruct((2, t, dff), BF16), jax.ShapeDtypeStruct((2, 8, dff), F32)],
        compiler_params=_params(("parallel",)),
    )(u2, cw2, cb2, dact)


GATHER_ID, SIBLING_ID, CHIPS_ID = 0, 1, 2


def _place():
    x, y, c = lax.axis_index("x"), lax.axis_index("y"), lax.axis_index("c")
    chips = [(1 - x, y), (x, 1 - y), (1 - x, 1 - y)]
    return x, y, c, chips


def _handshake(peers):
    barrier = pltpu.get_barrier_semaphore()
    for peer in peers:
        pl.semaphore_signal(barrier, inc=1, device_id=peer, device_id_type=MESH)
    pl.semaphore_wait(barrier, len(peers))


def _sequencer(name, body, out_type, scratch_types, collective_id):
    return pl.kernel(body, out_type=out_type, mesh=plsc.ScalarSubcoreMesh(axis_name="seq", num_cores=1),
                     scratch_types=scratch_types, name=name,
                     compiler_params=pltpu.CompilerParams(collective_id=collective_id))


def _gather_body(n):
    def body(*refs):
        src, out = refs[:n], refs[n:2 * n]
        send, recv, loc = refs[2 * n:]
        x, y, c, chips = _place()
        sibling = (x, y, 1 - c)
        _handshake([sibling] + [(*chip, c) for chip in chips])

        def slot(a, px, py, pc):
            return out[a].at[4 * px + 2 * py + pc]

        def copy(a, k, block, to, from_src=False):
            return pltpu.make_async_remote_copy(
                src_ref=src[a] if from_src else slot(a, *block), dst_ref=slot(a, *block),
                send_sem=send.at[a, k], recv_sem=recv.at[a, k], device_id=to, device_id_type=MESH)

        mine = [pltpu.make_async_copy(src[a], slot(a, x, y, c), loc.at[a]) for a in range(n)]
        for cp in mine:
            cp.start()
        first = []
        for a in range(n):
            first.append(copy(a, 0, (x, y, c), sibling, True))
            first += [copy(a, 1 + j, (x, y, c), (*chip, c), True) for j, chip in enumerate(chips)]
        for cp in first:
            cp.start()
        passed = []
        for j, chip in enumerate(chips):
            for a in range(n):
                copy(a, 1 + j, (*chip, c), (x, y, c)).wait_recv()
                cp = copy(a, 4 + j, (*chip, c), sibling)
                cp.start()
                passed.append(cp)
        for a in range(n):
            copy(a, 0, sibling, (x, y, c)).wait_recv()
            for j, chip in enumerate(chips):
                copy(a, 4 + j, (*chip, 1 - c), (x, y, c)).wait_recv()
        for cp in first + passed:
            cp.wait_send()
        for cp in mine:
            cp.wait()

    return body


def gather_layer(name, shards):
    n = len(shards)
    out_type = [jax.ShapeDtypeStruct((N_DEV,) + s.shape, s.dtype) for s in shards]
    scratch = [pltpu.SemaphoreType.DMA((n, 7)), pltpu.SemaphoreType.DMA((n, 7)), pltpu.SemaphoreType.DMA((n,))]
    return _sequencer(name, _gather_body(n), out_type, scratch, GATHER_ID)(*shards)


def _to_sibling_body(n):
    def body(*refs):
        src, got = refs[:n], refs[n:2 * n]
        send, recv = refs[2 * n:]
        x, y, c, _ = _place()
        sibling = (x, y, 1 - c)
        _handshake([sibling])
        remote = []
        for a in range(n):
            for q in range(4):
                remote.append(pltpu.make_async_remote_copy(
                    src_ref=src[a].at[2 * q + 1 - c], dst_ref=got[a].at[q], send_sem=send.at[a, q],
                    recv_sem=recv.at[a, q], device_id=sibling, device_id_type=MESH))
        for cp in remote:
            cp.start()
        for cp in remote:
            cp.wait()

    return body


def grads_to_sibling(name, grads):
    n = len(grads)
    out_type = [jax.ShapeDtypeStruct((4,) + g.shape[1:], g.dtype) for g in grads]
    scratch = [pltpu.SemaphoreType.DMA((n, 4)), pltpu.SemaphoreType.DMA((n, 4))]
    return _sequencer(name, _to_sibling_body(n), out_type, scratch, SIBLING_ID)(*grads)


def _to_chips_body(n):
    def body(*refs):
        src, got = refs[:n], refs[n:2 * n]
        send, recv = refs[2 * n:]
        x, y, c, chips = _place()
        _handshake([(*chip, c) for chip in chips])
        remote = []
        for a in range(n):
            for j, (px, py) in enumerate(chips):
                remote.append(pltpu.make_async_remote_copy(
                    src_ref=src[a].at[2 * px + py], dst_ref=got[a].at[j], send_sem=send.at[a, j],
                    recv_sem=recv.at[a, j], device_id=(px, py, c), device_id_type=MESH))
        for cp in remote:
            cp.start()
        for cp in remote:
            cp.wait()

    return body


def grads_to_chips(name, parts):
    n = len(parts)
    out_type = [jax.ShapeDtypeStruct((3,) + p.shape[1:], p.dtype) for p in parts]
    scratch = [pltpu.SemaphoreType.DMA((n, 3)), pltpu.SemaphoreType.DMA((n, 3))]
    return _sequencer(name, _to_chips_body(n), out_type, scratch, CHIPS_ID)(*parts)


def all_reduce_small(name, vec):
    rows, m = vec.shape

    def body(x_ref, o_ref, buf, send, recv):
        x, y, c, chips = _place()
        sibling = (x, y, 1 - c)

        def blk(px, py, pc):
            return buf.at[pl.ds(pl.multiple_of((4 * px + 2 * py + pc) * rows, rows), rows), :]

        def copy(k, block, to):
            return pltpu.make_async_remote_copy(src_ref=blk(*block), dst_ref=blk(*block), send_sem=send.at[k],
                                                recv_sem=recv.at[k], device_id=to, device_id_type=MESH)

        blk(x, y, c)[...] = x_ref[...]
        first = [copy(0, (x, y, c), sibling)] + [copy(1 + j, (x, y, c), (*chip, c)) for j, chip in enumerate(chips)]
        for cp in first:
            cp.start()
        passed = [copy(4 + j, (*chip, c), sibling) for j, chip in enumerate(chips)]
        for j, chip in enumerate(chips):
            copy(1 + j, (*chip, c), (x, y, c)).wait_recv()
            passed[j].start()
        copy(0, sibling, (x, y, c)).wait_recv()
        for j, chip in enumerate(chips):
            copy(4 + j, (*chip, 1 - c), (x, y, c)).wait_recv()
        for cp in first + passed:
            cp.wait_send()
        tot = buf[0:rows, :]
        for dev in range(1, N_DEV):
            tot = tot + buf[dev * rows:(dev + 1) * rows, :]
        o_ref[...] = tot

    return pl.pallas_call(
        body, name=name, in_specs=[pl.BlockSpec(memory_space=pltpu.VMEM)],
        out_specs=pl.BlockSpec(memory_space=pltpu.VMEM), out_shape=jax.ShapeDtypeStruct((rows, m), F32),
        scratch_shapes=[pltpu.VMEM((N_DEV * rows, m), F32), pltpu.SemaphoreType.DMA((7,)),
                        pltpu.SemaphoreType.DMA((7,))],
        compiler_params=pltpu.CompilerParams(vmem_limit_bytes=VMEM_LIMIT),
    )(vec)


def _ew_tiles(rows, cols, max_elems=1 << 18):
    tr = rows
    for cand in (1024, 512, 256, 128, 64, 32, 16):
        if rows % cand == 0 and cand * cols <= max_elems:
            tr = cand
            break
    return tr


def chip_sum(name, full, got, core):
    _, kdim, ncol = full.shape
    tr = _ew_tiles(kdim, ncol, max_elems=1 << 20)
    blk = (None, tr, ncol)
    by_chip = pl.BlockSpec(blk, lambda q, i, c: (q, i, 0))

    def body(c_ref, a_ref, b_ref, o_ref):
        o_ref[...] = (a_ref[...].astype(F32) + b_ref[...].astype(F32)).astype(BF16)

    return pl.pallas_call(
        body, name=name,
        grid_spec=pltpu.PrefetchScalarGridSpec(
            num_scalar_prefetch=1, grid=(4, kdim // tr),
            in_specs=[pl.BlockSpec(blk, lambda q, i, c: (2 * q + c[0], i, 0)), by_chip], out_specs=by_chip),
        out_shape=jax.ShapeDtypeStruct((4, kdim, ncol), BF16),
        compiler_params=_params(("parallel", "parallel")),
    )(core, full, got)


def _adamw_math(w, g, m, v):
    m = ADAM_B1 * m + (1.0 - ADAM_B1) * g
    v = ADAM_B2 * v + (1.0 - ADAM_B2) * (g * g)
    m_hat = m / (1.0 - ADAM_B1 ** ADAM_STEP)
    v_hat = v / (1.0 - ADAM_B2 ** ADAM_STEP)
    delta = -ADAM_LR * (m_hat / (jnp.sqrt(v_hat) + ADAM_EPS) + ADAM_WD * w)
    return delta, m, v


def adamw_layer(name, sums, got, w, m, v, layer, chip, after):
    _, kdim, ncol = sums.shape
    tr = _ew_tiles(kdim, ncol)
    out = pl.BlockSpec((tr, ncol), lambda i, q: (i, 0))
    mine = pl.BlockSpec((None, tr, ncol), lambda i, q: (q[0], i, 0))
    others = pl.BlockSpec((3, tr, ncol), lambda i, q: (0, i, 0))
    param = pl.BlockSpec((None, tr, ncol), lambda i, q: (layer, i, 0))

    def body(q_ref, o_ref, g_ref, w_ref, m_ref, v_ref, after_ref, go_ref, d_ref, mo_ref, vo_ref):
        g = o_ref[...].astype(F32)
        for j in range(3):
            g = g + g_ref[j].astype(F32)
        d, mn, vn = _adamw_math(w_ref[...], g, m_ref[...], v_ref[...])
        go_ref[...] = g
        d_ref[...] = d
        mo_ref[...] = mn
        vo_ref[...] = vn

    return pl.pallas_call(
        body, name=name,
        grid_spec=pltpu.PrefetchScalarGridSpec(
            num_scalar_prefetch=1, grid=(kdim // tr,),
            in_specs=[mine, others, param, param, param, pl.BlockSpec(memory_space=pl.ANY)], out_specs=[out] * 4),
        out_shape=[jax.ShapeDtypeStruct((kdim, ncol), F32)] * 4,
        compiler_params=_params(("parallel",)),
    )(chip, sums, got, w, m, v, after)


def adamw_small(name, g, w, m, v):
    def body(g_ref, w_ref, m_ref, v_ref, d_ref, mo_ref, vo_ref):
        d, mn, vn = _adamw_math(w_ref[...], g_ref[...], m_ref[...], v_ref[...])
        d_ref[...] = d
        mo_ref[...] = mn
        vo_ref[...] = vn

    vm = pl.BlockSpec(memory_space=pltpu.VMEM)
    return pl.pallas_call(
        body, name=name, in_specs=[vm] * 4, out_specs=[vm] * 3,
        out_shape=[jax.ShapeDtypeStruct(g.shape, F32)] * 3,
        compiler_params=pltpu.CompilerParams(vmem_limit_bytes=VMEM_LIMIT),
    )(g, w, m, v)


def _pack(parts, width):
    flat = jnp.concatenate([p.reshape(-1).astype(F32) for p in parts])
    pad = (-flat.shape[0]) % width
    return jnp.pad(flat, (0, pad)).reshape(-1, width) if pad else flat.reshape(-1, width)


def _unpack(packed, shapes):
    flat = packed.reshape(-1)
    out, off = [], 0
    for s in shapes:
        size = math.prod(s)
        out.append(flat[off:off + size].reshape(s))
        off += size
    return out


def _local_step(h, target, layers, params, on_grads=None):
    a_q_gain, a_k_gain, rel_bias, mix_norm, ffn_norm, conv_b, final_norm = params
    t, d = h.shape
    depth = len(layers)
    n_groups = len(B_GROUPS)
    hg = B_HEADS_PER_GROUP
    n_kv = A_KV_HEADS
    w_a, w_b, w_u = layers[0][0].shape[2], layers[1][0].shape[2], layers[0][2].shape[2]
    n_q = w_a * N_DEV // HEAD_DIM - 2 * n_kv
    dff = layers[0][3].shape[0]
    n_a = (depth + 1) // 2
    cb_full = conv_b.reshape(depth, 2, 1, dff)

    cos, sin = rope_tables(t)
    tables = [band_tables(rel_bias[:, g * hg:(g + 1) * hg], win // (2 * dil), dil, _tile(t // dil, B_BQ))
              for g, (win, dil) in enumerate(B_GROUPS)]

    saved = []
    for i in range(depth):
        j = i // 2
        w_qkv, w_o, w_up_i, w_down_i, cw = layers[i]
        s = {"h_in": h}
        hn = rms_fwd("mix_norm_fwd", h, mix_norm[i])
        s["hn"] = hn
        if i % 2 == 0:
            qkv = mm_col_fwd("a_qkv_fwd", hn, w_qkv, F32)
            qkv_r = qk_prep_fwd("a_qk_prep_fwd", qkv, a_q_gain[j], a_k_gain[j], cos, sin, n_q, n_kv)
            o, lse = mixer_a_fwd(qkv_r, n_q, n_kv)
            s.update(qkv=qkv, qkv_r=qkv_r, o=o, lse=lse)
            h = mm_row_fwd("a_out_fwd", o, w_o, h)
        else:
            qkv = mm_col_fwd("b_qkv_fwd", hn, w_qkv, BF16)
            outs, lzs = [], []
            for g, (win, dil) in enumerate(B_GROUPS):
                o_g, lz_g = mixer_b_group_fwd(qkv, tables[g][0], dil, g, n_groups)
                outs.append(o_g)
                lzs.append(lz_g)
            y = combine_fwd("b_combine_fwd", outs, lzs)
            s.update(qkv=qkv, outs=outs, lzs=lzs, y=y)
            h = mm_row_fwd("b_out_fwd", y, w_o, h)
        s["h_mid"] = h
        hn2 = rms_fwd("ffn_norm_fwd", h, ffn_norm[i])
        u2 = mm_col_fwd("ffn_up_fwd", hn2, w_up_i, F32, split=2)
        act = conv_act_fwd("ffn_conv_act_fwd", u2, cw, cb_full[i])
        s.update(hn2=hn2, u2=u2, act=act)
        h = mm_row_fwd("ffn_down_fwd", act, w_down_i, h)
        saved.append(s)

    dh, d_final, loss_part = loss_head("loss_head", h, final_norm, target)

    d_mix, d_ffn, d_cw, d_cb = [None] * depth, [None] * depth, [None] * depth, [None] * depth
    d_qg, d_kg = [None] * n_a, [None] * n_a
    d_rel = jnp.zeros((n_groups * hg, LANES), F32)
    layer_grads = [{} for _ in range(depth)]
    pending = []

    def settle(token):
        while pending:
            i_p, part_p, finish = pending.pop()
            layer_grads[i_p][part_p] = finish(token)

    def register(i_p, part_p, grads):
        if on_grads is None:
            layer_grads[i_p][part_p] = grads
        else:
            pending.append((i_p, part_p, on_grads(i_p, part_p, grads)))

    for i in reversed(range(depth)):
        j = i // 2
        w_qkv, w_o, w_up_i, w_down_i, cw = layers[i]
        s = saved[i]
        dact = mm_row_dx("ffn_down_dx", dh, w_down_i)
        g_down = mm_row_dw("ffn_down_dw", s["act"], dh)
        du2, dcw = conv_act_bwd("ffn_conv_act_bwd", s["u2"], cw, cb_full[i], dact)
        d_cw[i] = dcw[:, 0:3, :].transpose(1, 0, 2).reshape(3, 2 * dff)
        d_cb[i] = dcw[:, 3, :].reshape(2 * dff)
        g_up = mm_col_dw("ffn_up_dw", s["hn2"], du2, w_u, split=2)
        dhn2 = mm_col_dx("ffn_up_dx", du2, w_up_i, split=2)
        dh, d_ffn[i] = rms_bwd("ffn_norm_bwd", s["h_mid"], ffn_norm[i], dhn2, dh)
        settle(d_ffn[i])
        register(i, "ffn", [g_up, g_down.reshape(N_DEV, -1, d)])
        if i % 2 == 0:
            do = mm_row_dx("a_out_dx", dh, w_o)
            g_o = mm_row_dw("a_out_dw", s["o"], dh)
            dlt, do_b = row_delta("a_delta", do, s["o"], n_q)
            dq, dk, dv = mixer_a_bwd(s["qkv_r"], do_b, s["lse"], dlt, n_q, n_kv)
            dqkv, dgain = qk_prep_bwd("a_qk_prep_bwd", s["qkv"], dq, dk, dv, a_q_gain[j], a_k_gain[j], cos, sin,
                                      n_q, n_kv)
            d_qg[j], d_kg[j] = dgain[0], dgain[1]
            g_qkv = mm_col_dw("a_qkv_dw", s["hn"], dqkv, w_a)
            dhn = mm_col_dx("a_qkv_dx", dqkv, w_qkv)
        else:
            dy = mm_row_dx("b_out_dx", dh, w_o)
            g_o = mm_row_dw("b_out_dw", s["y"], dh)
            res = combine_bwd("b_combine_bwd", dy, s["outs"], s["lzs"])
            dos, dlts = res[:n_groups], res[n_groups:]
            pieces, rel_rows = [], []
            for g, (win, dil) in enumerate(B_GROUPS):
                dq, dk, dv, dbias = mixer_b_group_bwd(s["qkv"], tables[g][0], dos[g], s["lzs"][g], dlts[g], dil, g,
                                                      n_groups)
                pieces += [dq, dk, dv]
                rel_rows.append(bias_bucket_sums(f"b_bias_sums_d{dil}", dbias, tables[g][1]))
            d_rel = d_rel + jnp.concatenate(rel_rows, axis=0)
            dqkv = jnp.concatenate(pieces, axis=1)
            g_qkv = mm_col_dw("b_qkv_dw", s["hn"], dqkv, w_b)
            dhn = mm_col_dx("b_qkv_dx", dqkv, w_qkv)
        dh, d_mix[i] = rms_bwd("mix_norm_bwd", s["h_in"], mix_norm[i], dhn, dh)
        settle(d_mix[i])
        register(i, "mix", [g_qkv, g_o.reshape(N_DEV, -1, d)])
    last = pending.pop()[2] if pending else None

    d_rel_bias = d_rel[:, :REL_BUCKETS].T
    small_g = [jnp.stack(d_qg), jnp.stack(d_kg), d_rel_bias, jnp.concatenate(d_mix, 0), jnp.concatenate(d_ffn, 0),
               jnp.stack(d_cb), d_final.reshape(-1), jnp.stack(d_cw), loss_part]
    return dh, layer_grads, small_g, last


def kernel(x, a_w_qkv, a_w_o, a_q_gain, a_k_gain, b_w_qkv, b_w_o, rel_bias, mix_norm, ffn_norm, w_up, conv_w, conv_b, w_down, final_norm, loss_target, m_a_w_qkv, m_a_w_o, m_a_q_gain, m_a_k_gain, m_b_w_qkv, m_b_w_o, m_rel_bias, m_mix_norm, m_ffn_norm, m_w_up, m_conv_w, m_conv_b, m_w_down, m_final_norm, v_a_w_qkv, v_a_w_o, v_a_q_gain, v_a_k_gain, v_b_w_qkv, v_b_w_o, v_rel_bias, v_mix_norm, v_ffn_norm, v_w_up, v_conv_w, v_conv_b, v_w_down, v_final_norm):
    d = x.shape[2]
    depth = mix_norm.shape[0]
    dff = w_down.shape[1] * N_DEV
    w_u = w_up.shape[2]
    mixers = [(a_w_qkv, a_w_o, m_a_w_qkv, m_a_w_o, v_a_w_qkv, v_a_w_o),
              (b_w_qkv, b_w_o, m_b_w_qkv, m_b_w_o, v_b_w_qkv, v_b_w_o)]

    layers = []
    for i in range(depth):
        w_qkv, w_o = mixers[i % 2][0][i // 2], mixers[i % 2][1][i // 2]
        shards = [w_qkv.astype(BF16), w_o.astype(BF16), w_up[i].astype(BF16), w_down[i].astype(BF16), conv_w[i]]
        if i == 0:
            (g_qkv,) = gather_layer("gather_l0_qkv", shards[:1])
            g_o, g_up, g_down, g_cw = gather_layer("gather_l0", shards[1:])
        else:
            g_qkv, g_o, g_up, g_down, g_cw = gather_layer(f"gather_l{i}", shards)
        cw = g_cw.transpose(1, 0, 2).reshape(3, 2, dff).transpose(1, 0, 2)
        layers.append((g_qkv, g_o.reshape(-1, d), g_up, g_down.reshape(dff, d), cw))

    core = lax.axis_index("c").astype(jnp.int32).reshape(1)
    chip = (2 * lax.axis_index("x") + lax.axis_index("y")).astype(jnp.int32).reshape(1)

    def reduce_and_update(i, part, grads):
        w_qkv, w_o, m_qkv, m_o, v_qkv, v_o = mixers[i % 2]
        state = {"mix": [(w_qkv, m_qkv, v_qkv, i // 2), (w_o, m_o, v_o, i // 2)],
                 "ffn": [(w_up, m_w_up, v_w_up, i), (w_down, m_w_down, v_w_down, i)]}[part]
        got1 = grads_to_sibling(f"to_sibling_l{i}_{part}", grads)
        sums = [chip_sum(f"chip_sum_l{i}_{part}{a}", grads[a], got1[a], core) for a in range(2)]
        got2 = grads_to_chips(f"to_chips_l{i}_{part}", sums)

        def finish(token):
            return [adamw_layer(f"adamw_l{i}_{part}{a}", sums[a], got2[a], *state[a], chip, token) for a in range(2)]

        return finish

    dh, updates, small_g, last = _local_step(x[0], loss_target[0], layers,
                                             (a_q_gain, a_k_gain, rel_bias, mix_norm, ffn_norm, conv_b, final_norm),
                                             reduce_and_update)
    grad_x = dh[None]

    width = 2048
    packed = _pack(small_g, N_DEV * width).reshape(-1, N_DEV, width)
    n_rows = packed.shape[0]
    packed = packed.transpose(1, 0, 2).reshape(N_DEV, n_rows * width)
    red = all_reduce_small("small_all_reduce", packed)
    updates[0]["mix"] = last(red)
    big_out = {}
    for nm, part, a in (("qkv", "mix", 0), ("o", "mix", 1)):
        for par, prefix in enumerate(("a_w_", "b_w_")):
            big_out[prefix + nm] = [jnp.stack([updates[i][part][a][k] for i in range(par, depth, 2)]) for k in range(4)]
    for nm, a in (("w_up", 0), ("w_down", 1)):
        big_out[nm] = [jnp.stack([updates[i]["ffn"][a][k] for i in range(depth)]) for k in range(4)]
    red = red.reshape(N_DEV, n_rows, width).transpose(1, 0, 2)
    (g_qg, g_kg, g_rel, g_mix, g_ffn, g_cb, g_fin, g_cw_all, loss) = _unpack(red, [p.shape for p in small_g])
    idx = 4 * lax.axis_index("x") + 2 * lax.axis_index("y") + lax.axis_index("c")
    g_cw_mine = lax.dynamic_slice_in_dim(g_cw_all, idx * w_u, w_u, axis=2)

    small_w = [a_q_gain, a_k_gain, rel_bias, mix_norm, ffn_norm, conv_b, final_norm, conv_w]
    small_m = [m_a_q_gain, m_a_k_gain, m_rel_bias, m_mix_norm, m_ffn_norm, m_conv_b, m_final_norm, m_conv_w]
    small_v = [v_a_q_gain, v_a_k_gain, v_rel_bias, v_mix_norm, v_ffn_norm, v_conv_b, v_final_norm, v_conv_w]
    small_grads = [g_qg, g_kg, g_rel, g_mix, g_ffn, g_cb, g_fin, g_cw_mine]
    shapes = [w.shape for w in small_w]
    pad_rows = (-_pack(small_w, width).shape[0]) % 8

    def pk8(parts):
        p = _pack(parts, width)
        return jnp.pad(p, ((0, pad_rows), (0, 0))) if pad_rows else p

    sd, sm, sv = adamw_small("adamw_small", pk8(small_grads), pk8(small_w), pk8(small_m), pk8(small_v))
    sd, sm, sv = _unpack(sd, shapes), _unpack(sm, shapes), _unpack(sv, shapes)

    names = ["a_w_qkv", "a_w_o", "a_q_gain", "a_k_gain", "b_w_qkv", "b_w_o", "rel_bias", "mix_norm", "ffn_norm",
             "w_up", "conv_w", "conv_b", "w_down", "final_norm"]
    small_names = ["a_q_gain", "a_k_gain", "rel_bias", "mix_norm", "ffn_norm", "conv_b", "final_norm", "conv_w"]
    grads, deltas, new_m, new_v = {}, {}, {}, {}
    for nm, outs in big_out.items():
        grads[nm], deltas[nm], new_m[nm], new_v[nm] = outs
    for a, nm in enumerate(small_names):
        grads[nm] = small_grads[a].reshape(shapes[a])
        deltas[nm], new_m[nm], new_v[nm] = sd[a], sm[a], sv[a]
    return (loss.reshape(()), grad_x, *[grads[n] for n in names], *[deltas[n] for n in names],
            *[new_m[n] for n in names], *[new_v[n] for n in names])
```

```python
import functools
import math

import jax
import jax.numpy as jnp
from jax import lax
from jax.experimental import pallas as pl
from jax.experimental.pallas import tpu as pltpu
from jax.experimental.pallas import tpu_sc as plsc

F32 = jnp.float32
BF16 = jnp.bfloat16
MESH = pl.DeviceIdType.MESH

N_DEV = 8
LANES = 128
HEAD_DIM = 128
VMEM_LIMIT = 56 * 1024 * 1024
GRID_W = 64
ROPE_THETA = 10000.0
A_KV_HEADS = 4
B_GROUPS = ((128, 1), (512, 4), (2048, 16))
B_HEADS_PER_GROUP = 8
REL_BUCKETS = 32
REL_MAX_DISTANCE = 1024
EPS = 1e-6
NEG_INF = -1e30
ADAM_LR = 0.001
ADAM_B1 = 0.9
ADAM_B2 = 0.999
ADAM_EPS = 1e-08
ADAM_WD = 0.01
ADAM_STEP = 10

ROW_TILE = 256
MM_TM = 1024
MM_TK = 2048
A_BQ = 512
A_BK = 1024
B_BQ = 256
ATTN_ROWS = 16
ATTN_SCALE = HEAD_DIM ** -0.5

NN = (((1,), (0,)), ((), ()))
NT = (((1,), (1,)), ((), ()))
TN = (((0,), (0,)), ((), ()))


def _tile(n, pref):
    return pref if n % pref == 0 else n


def _div_tile(n, pref):
    for cand in range(pref - pref % LANES, 0, -LANES):
        if n % cand == 0:
            return cand
    return n


def _params(sem):
    return pltpu.CompilerParams(dimension_semantics=sem, vmem_limit_bytes=VMEM_LIMIT)


def _dot(a, b, dims):
    return lax.dot_general(a, b, dims, preferred_element_type=F32)


def _mm(name, a, b, *, grid, a_blk, a_map, b_blk, b_map, o_blk, o_map, out_shape, out_dtype, dims,
        res=None, after=()):
    nk = grid[2]
    acc_shape = tuple(d for d in o_blk if d is not None)

    def body(*refs):
        a_ref, b_ref = refs[:2]
        r_ref = None if res is None else refs[2]
        o_ref, acc = refs[-2:]
        k = pl.program_id(2)

        @pl.when(k == 0)
        def _():
            acc[...] = jnp.zeros_like(acc)

        acc[...] += _dot(a_ref[...].astype(BF16), b_ref[...].astype(BF16), dims)

        @pl.when(k == nk - 1)
        def _():
            r = acc[...]
            if r_ref is not None:
                r = r + r_ref[...]
            o_ref[...] = r.astype(out_dtype)

    in_specs = [pl.BlockSpec(a_blk, a_map), pl.BlockSpec(b_blk, b_map)]
    args = [a, b]
    if res is not None:
        in_specs.append(pl.BlockSpec(o_blk, o_map))
        args.append(res)
    in_specs += [pl.BlockSpec(memory_space=pl.ANY)] * len(after)
    args += list(after)
    return pl.pallas_call(
        body, name=name, grid=grid, in_specs=in_specs, out_specs=pl.BlockSpec(o_blk, o_map),
        out_shape=jax.ShapeDtypeStruct(out_shape, out_dtype),
        scratch_shapes=[pltpu.VMEM(acc_shape, F32)],
        compiler_params=_params(("parallel", "parallel", "arbitrary")),
    )(*args)


def mm_col_fwd(name, a, wg, out_dtype, split=1):
    m, kdim = a.shape
    n_dev, _, w = wg.shape
    tm, tk = _tile(m, MM_TM), _div_tile(kdim, MM_TK)
    per = n_dev // split
    if split == 1:
        o_blk, o_map, o_shape = (tm, w), (lambda i, j, k: (i, j)), (m, n_dev * w)
    else:
        o_blk, o_map, o_shape = (None, tm, w), (lambda i, j, k: (j // per, i, j % per)), (split, m, per * w)
    return _mm(name, a, wg, grid=(m // tm, n_dev, kdim // tk),
               a_blk=(tm, tk), a_map=lambda i, j, k: (i, k),
               b_blk=(None, tk, w), b_map=lambda i, j, k: (j, k, 0),
               o_blk=o_blk, o_map=o_map, out_shape=o_shape, out_dtype=out_dtype, dims=NN)


def mm_col_dx(name, dy, wg, split=1):
    n_dev, kdim, w = wg.shape
    m = dy.shape[-2]
    tm, tk = _tile(m, MM_TM), _div_tile(kdim, MM_TK)
    per = n_dev // split
    if split == 1:
        a_blk, a_map = (tm, w), (lambda i, j, k: (i, k))
    else:
        a_blk, a_map = (None, tm, w), (lambda i, j, k: (k // per, i, k % per))
    return _mm(name, dy, wg, grid=(m // tm, kdim // tk, n_dev),
               a_blk=a_blk, a_map=a_map,
               b_blk=(None, tk, w), b_map=lambda i, j, k: (k, j, 0),
               o_blk=(tm, tk), o_map=lambda i, j, k: (i, j), out_shape=(m, kdim), out_dtype=F32, dims=NT)


def mm_col_dw(name, x, dy, w, split=1):
    m, kdim = x.shape
    tm, tk = _tile(m, MM_TM), _div_tile(kdim, MM_TK)
    per = N_DEV // split
    if split == 1:
        b_blk, b_map = (tm, w), (lambda i, j, k: (k, j))
    else:
        b_blk, b_map = (None, tm, w), (lambda i, j, k: (j // per, k, j % per))
    return _mm(name, x, dy, grid=(kdim // tk, N_DEV, m // tm),
               a_blk=(tm, tk), a_map=lambda i, j, k: (k, i),
               b_blk=b_blk, b_map=b_map,
               o_blk=(None, tk, w), o_map=lambda i, j, k: (j, i, 0),
               out_shape=(N_DEV, kdim, w), out_dtype=BF16, dims=TN)


def mm_row_fwd(name, a, wg, res):
    m, kdim = a.shape
    n = wg.shape[1]
    tm, tk, tn = _tile(m, MM_TM), _div_tile(kdim, MM_TK), _tile(n, 1024)
    return _mm(name, a, wg, grid=(m // tm, n // tn, kdim // tk),
               a_blk=(tm, tk), a_map=lambda i, j, k: (i, k),
               b_blk=(tk, tn), b_map=lambda i, j, k: (k, j),
               o_blk=(tm, tn), o_map=lambda i, j, k: (i, j), out_shape=(m, n), out_dtype=F32, dims=NN,
               res=res)


def mm_row_dx(name, dy, wg):
    m, n = dy.shape
    kdim = wg.shape[0]
    tm, tk, tn = _tile(m, MM_TM), _div_tile(kdim, MM_TK), _tile(n, 1024)
    return _mm(name, dy, wg, grid=(m // tm, kdim // tk, n // tn),
               a_blk=(tm, tn), a_map=lambda i, j, k: (i, k),
               b_blk=(tk, tn), b_map=lambda i, j, k: (j, k),
               o_blk=(tm, tk), o_map=lambda i, j, k: (i, j), out_shape=(m, kdim), out_dtype=F32, dims=NT)


def mm_row_dw(name, x, dy, after=()):
    m, kdim = x.shape
    n = dy.shape[1]
    tm, tk, tn = _tile(m, MM_TM), _div_tile(kdim, MM_TK), _tile(n, 1024)
    return _mm(name, x, dy, grid=(kdim // tk, n // tn, m // tm),
               a_blk=(tm, tk), a_map=lambda i, j, k: (k, i),
               b_blk=(tm, tn), b_map=lambda i, j, k: (k, j),
               o_blk=(tk, tn), o_map=lambda i, j, k: (i, j), out_shape=(kdim, n), out_dtype=BF16, dims=TN,
               after=after)


def _rows(d, tm):
    return pl.BlockSpec((tm, d), lambda i: (i, 0))


def _vec(d):
    return pl.BlockSpec((1, d), lambda i: (0, 0))


def rms_fwd(name, h, gain):
    t, d = h.shape
    tm = _tile(t, ROW_TILE)

    def body(h_ref, g_ref, o_ref):
        x = h_ref[...]
        rstd = lax.rsqrt(jnp.mean(x * x, axis=-1, keepdims=True) + EPS)
        o_ref[...] = (x * rstd * g_ref[...]).astype(BF16)

    return pl.pallas_call(
        body, name=name, grid=(t // tm,), in_specs=[_rows(d, tm), _vec(d)], out_specs=_rows(d, tm),
        out_shape=jax.ShapeDtypeStruct((t, d), BF16), compiler_params=_params(("parallel",)),
    )(h, gain.reshape(1, d))


def rms_bwd(name, h, gain, dy, dres):
    t, d = h.shape
    tm = _tile(t, ROW_TILE)

    def body(h_ref, g_ref, dy_ref, r_ref, dh_ref, dg_ref):
        @pl.when(pl.program_id(0) == 0)
        def _():
            dg_ref[...] = jnp.zeros_like(dg_ref)

        x = h_ref[...]
        rstd = lax.rsqrt(jnp.mean(x * x, axis=-1, keepdims=True) + EPS)
        xhat = x * rstd
        dyv = dy_ref[...]
        dxhat = dyv * g_ref[...]
        dh_ref[...] = r_ref[...] + rstd * (dxhat - xhat * jnp.mean(dxhat * xhat, axis=-1, keepdims=True))
        dg_ref[...] += jnp.sum(dyv * xhat, axis=0, keepdims=True)

    return pl.pallas_call(
        body, name=name, grid=(t // tm,),
        in_specs=[_rows(d, tm), _vec(d), _rows(d, tm), _rows(d, tm)],
        out_specs=[_rows(d, tm), _vec(d)],
        out_shape=[jax.ShapeDtypeStruct((t, d), F32), jax.ShapeDtypeStruct((1, d), F32)],
        compiler_params=_params(("arbitrary",)),
    )(h, gain.reshape(1, d), dy, dres)


def loss_head(name, h, gain, target):
    t, d = h.shape
    tm = _tile(t, ROW_TILE)

    def body(h_ref, g_ref, t_ref, dh_ref, dg_ref, loss_ref):
        @pl.when(pl.program_id(0) == 0)
        def _():
            dg_ref[...] = jnp.zeros_like(dg_ref)
            loss_ref[...] = jnp.zeros_like(loss_ref)

        x = h_ref[...]
        rstd = lax.rsqrt(jnp.mean(x * x, axis=-1, keepdims=True) + EPS)
        xhat = x * rstd
        err = xhat * g_ref[...] - t_ref[...]
        row = jnp.mean(err * err, axis=-1, keepdims=True)
        loss_ref[...] += 0.5 * jnp.sum(row, axis=0, keepdims=True)
        dyv = err * (1.0 / d)
        dxhat = dyv * g_ref[...]
        dh_ref[...] = rstd * (dxhat - xhat * jnp.mean(dxhat * xhat, axis=-1, keepdims=True))
        dg_ref[...] += jnp.sum(dyv * xhat, axis=0, keepdims=True)

    return pl.pallas_call(
        body, name=name, grid=(t // tm,),
        in_specs=[_rows(d, tm), _vec(d), _rows(d, tm)],
        out_specs=[_rows(d, tm), _vec(d), pl.BlockSpec((1, 1), lambda i: (0, 0))],
        out_shape=[jax.ShapeDtypeStruct((t, d), F32), jax.ShapeDtypeStruct((1, d), F32),
                   jax.ShapeDtypeStruct((1, 1), F32)],
        compiler_params=_params(("arbitrary",)),
    )(h, gain.reshape(1, d), target)


def rope_tables(seq):
    pos = jnp.arange(seq, dtype=jnp.int32)
    row_ids = (pos // GRID_W).astype(F32)
    col_ids = (pos % GRID_W).astype(F32)
    quarter = HEAD_DIM // 4
    inv_freq = ROPE_THETA ** (-jnp.arange(quarter, dtype=F32) / quarter)
    ar = row_ids[:, None] * inv_freq[None, :]
    ac = col_ids[:, None] * inv_freq[None, :]
    cos = jnp.concatenate([jnp.cos(ar), jnp.cos(ar), jnp.cos(ac), jnp.cos(ac)], axis=-1)
    sin = jnp.concatenate([-jnp.sin(ar), jnp.sin(ar), -jnp.sin(ac), jnp.sin(ac)], axis=-1)
    return cos, sin


def _swap_quarters(x):
    lane = lax.broadcasted_iota(jnp.int32, x.shape, 1)
    q = HEAD_DIM // 4
    return jnp.where((lane % (2 * q)) < q, pltpu.roll(x, HEAD_DIM - q, 1), pltpu.roll(x, q, 1))


def qk_prep_fwd(name, qkv, q_gain, k_gain, cos, sin, n_q, n_kv):
    t, width = qkv.shape
    tm = _tile(t, ROW_TILE)

    def body(x_ref, qg_ref, kg_ref, c_ref, s_ref, o_ref):
        c, s = c_ref[...], s_ref[...]
        for hd in range(n_q + n_kv):
            sl = slice(hd * HEAD_DIM, (hd + 1) * HEAD_DIM)
            x = x_ref[:, sl]
            g = qg_ref[...] if hd < n_q else kg_ref[...]
            xn = x * lax.rsqrt(jnp.mean(x * x, axis=-1, keepdims=True) + EPS) * g
            o_ref[:, sl] = (xn * c + _swap_quarters(xn) * s).astype(BF16)
        vs = slice((n_q + n_kv) * HEAD_DIM, width)
        o_ref[:, vs] = x_ref[:, vs].astype(BF16)

    return pl.pallas_call(
        body, name=name, grid=(t // tm,),
        in_specs=[_rows(width, tm), _vec(HEAD_DIM), _vec(HEAD_DIM), _rows(HEAD_DIM, tm), _rows(HEAD_DIM, tm)],
        out_specs=_rows(width, tm), out_shape=jax.ShapeDtypeStruct((t, width), BF16),
        compiler_params=_params(("parallel",)),
    )(qkv, q_gain.reshape(1, HEAD_DIM), k_gain.reshape(1, HEAD_DIM), cos, sin)


def qk_prep_bwd(name, qkv, dq, dk, dv, q_gain, k_gain, cos, sin, n_q, n_kv):
    t, width = qkv.shape
    tm = _tile(t, ROW_TILE)

    def body(x_ref, dq_ref, dk_ref, dv_ref, qg_ref, kg_ref, c_ref, s_ref, o_ref, dg_ref):
        @pl.when(pl.program_id(0) == 0)
        def _():
            dg_ref[...] = jnp.zeros_like(dg_ref)

        c, s = c_ref[...], s_ref[...]
        dgq = jnp.zeros((1, HEAD_DIM), F32)
        dgk = jnp.zeros((1, HEAD_DIM), F32)
        for hd in range(n_q + n_kv):
            sl = slice(hd * HEAD_DIM, (hd + 1) * HEAD_DIM)
            x = x_ref[:, sl]
            if hd < n_q:
                g, dout = qg_ref[...], dq_ref[:, sl]
            else:
                ks = slice((hd - n_q) * HEAD_DIM, (hd - n_q + 1) * HEAD_DIM)
                g, dout = kg_ref[...], dk_ref[:, ks]
            rstd = lax.rsqrt(jnp.mean(x * x, axis=-1, keepdims=True) + EPS)
            xhat = x * rstd
            dxn = dout * c + _swap_quarters(dout * s)
            part = jnp.sum(dxn * xhat, axis=0, keepdims=True)
            if hd < n_q:
                dgq = dgq + part
            else:
                dgk = dgk + part
            dxhat = dxn * g
            o_ref[:, sl] = (rstd * (dxhat - xhat * jnp.mean(dxhat * xhat, axis=-1, keepdims=True))).astype(BF16)
        o_ref[:, slice((n_q + n_kv) * HEAD_DIM, width)] = dv_ref[...].astype(BF16)
        dg_ref[0:1, :] += dgq
        dg_ref[1:2, :] += dgk

    kvw = n_kv * HEAD_DIM
    return pl.pallas_call(
        body, name=name, grid=(t // tm,),
        in_specs=[_rows(width, tm), _rows(n_q * HEAD_DIM, tm), _rows(kvw, tm), _rows(kvw, tm),
                  _vec(HEAD_DIM), _vec(HEAD_DIM), _rows(HEAD_DIM, tm), _rows(HEAD_DIM, tm)],
        out_specs=[_rows(width, tm), pl.BlockSpec((2, HEAD_DIM), lambda i: (0, 0))],
        out_shape=[jax.ShapeDtypeStruct((t, width), BF16), jax.ShapeDtypeStruct((2, HEAD_DIM), F32)],
        compiler_params=_params(("arbitrary",)),
    )(qkv, dq, dk, dv, q_gain.reshape(1, HEAD_DIM), k_gain.reshape(1, HEAD_DIM), cos, sin)


def _lanes(x, width):
    return jnp.tile(x, (1, width // LANES))


def _hs(hd):
    return slice(hd * HEAD_DIM, (hd + 1) * HEAD_DIM)


def attn_fwd(name, q, k, v, bias, *, grid, q_spec, k_spec, v_spec, b_spec, o_spec, valid, nh, shared_kv,
             bq, bk, o_shape, o_dtype):
    ns = grid[2]

    def body(*refs):
        if bias is None:
            q_ref, k_ref, v_ref, o_ref, lse_ref, m_s, l_s, acc_s = refs
            b_ref = None
        else:
            q_ref, k_ref, v_ref, b_ref, o_ref, lse_ref, m_s, l_s, acc_s = refs
        step = pl.program_id(2)

        @pl.when(step == 0)
        def _():
            m_s[...] = jnp.full_like(m_s, -jnp.inf)
            l_s[...] = jnp.zeros_like(l_s)
            acc_s[...] = jnp.zeros_like(acc_s)

        @pl.when(valid(pl.program_id(1), step))
        def _():
            for hd in range(nh):
                kh = _hs(0 if shared_kv else hd)
                s = _dot(q_ref[:, _hs(hd)], k_ref[:, kh], NT)
                p_rows, a_rows = [], []
                for r0 in range(0, bq, ATTN_ROWS):
                    rows = slice(r0, r0 + ATTN_ROWS)
                    z = s[rows] * ATTN_SCALE
                    if b_ref is not None:
                        z = z + b_ref[hd, rows, :]
                    m_prev = m_s[hd, rows, :]
                    m_new = jnp.maximum(m_prev, jnp.max(z, axis=-1, keepdims=True))
                    alpha = jnp.exp(m_prev - m_new)
                    p = jnp.exp(z - _lanes(m_new, bk))
                    l_s[hd, rows, :] = alpha * l_s[hd, rows, :] + jnp.sum(p, axis=-1, keepdims=True)
                    m_s[hd, rows, :] = m_new
                    p_rows.append(p.astype(BF16))
                    a_rows.append(alpha)
                pv = _dot(jnp.concatenate(p_rows, axis=0), v_ref[:, kh], NN)
                acc_s[hd] = jnp.concatenate(a_rows, axis=0) * acc_s[hd] + pv

        @pl.when(step == ns - 1)
        def _():
            for hd in range(nh):
                o_ref[:, _hs(hd)] = (acc_s[hd] / l_s[hd]).astype(o_dtype)
                lse_ref[:, _hs(hd)] = m_s[hd] + jnp.log(l_s[hd])

    in_specs = [q_spec, k_spec, v_spec] + ([] if bias is None else [b_spec])
    args = [q, k, v] + ([] if bias is None else [bias])
    stat = pltpu.VMEM((nh, bq, LANES), F32)
    return pl.pallas_call(
        body, name=name, grid=grid, in_specs=in_specs, out_specs=[o_spec, o_spec],
        out_shape=[jax.ShapeDtypeStruct(o_shape, o_dtype), jax.ShapeDtypeStruct(o_shape, F32)],
        scratch_shapes=[stat, stat, stat],
        compiler_params=_params(("parallel", "parallel", "arbitrary")),
    )(*args)


def _probs(q_ref, k_ref, v_ref, do_ref, lse_ref, dlt_ref, b_ref, hd, kh, bq, bk, want_p=True, on_ds=None):
    s = _dot(q_ref[:, _hs(hd)], k_ref[:, kh], NT)
    dp = _dot(do_ref[:, _hs(hd)], v_ref[:, kh], NT)
    p_rows, ds_rows = [], []
    for r0 in range(0, bq, ATTN_ROWS):
        rows = slice(r0, r0 + ATTN_ROWS)
        z = s[rows] * ATTN_SCALE
        if b_ref is not None:
            z = z + b_ref[hd, rows, :]
        p = jnp.exp(z - _lanes(lse_ref[rows, _hs(hd)], bk))
        ds = p * (dp[rows] - _lanes(dlt_ref[rows, _hs(hd)], bk))
        if on_ds is not None:
            on_ds(rows, ds)
        if want_p:
            p_rows.append(p.astype(BF16))
        ds_rows.append(ds.astype(BF16))
    return (jnp.concatenate(p_rows, axis=0) if want_p else None), jnp.concatenate(ds_rows, axis=0)


def attn_bwd_dq(name, q, k, v, do, lse, dlt, *, grid, q_spec, k_spec, v_spec, nh, bq, bk, o_shape):
    ns = grid[2]
    scale = HEAD_DIM ** -0.5

    def body(q_ref, k_ref, v_ref, do_ref, lse_ref, dlt_ref, dq_ref, acc_s):
        step = pl.program_id(2)

        @pl.when(step == 0)
        def _():
            acc_s[...] = jnp.zeros_like(acc_s)

        for hd in range(nh):
            _, ds = _probs(q_ref, k_ref, v_ref, do_ref, lse_ref, dlt_ref, None, hd, _hs(0), bq, bk, want_p=False)
            acc_s[hd] += _dot(ds, k_ref[:, _hs(0)], NN)

        @pl.when(step == ns - 1)
        def _():
            for hd in range(nh):
                dq_ref[:, _hs(hd)] = acc_s[hd] * scale

    return pl.pallas_call(
        body, name=name, grid=grid, in_specs=[q_spec, k_spec, v_spec, q_spec, q_spec, q_spec],
        out_specs=q_spec, out_shape=jax.ShapeDtypeStruct(o_shape, F32),
        scratch_shapes=[pltpu.VMEM((nh, bq, LANES), F32)],
        compiler_params=_params(("parallel", "parallel", "arbitrary")),
    )(q, k, v, do, lse, dlt)


def _always(i, s):
    return s >= 0


def row_delta(name, do, o, n_heads):
    t, width = do.shape
    tm = _tile(t, ROW_TILE)

    def body(do_ref, o_ref, dl_ref, dob_ref):
        for hd in range(n_heads):
            d = do_ref[:, _hs(hd)]
            s = jnp.sum(d * o_ref[:, _hs(hd)].astype(F32), axis=-1, keepdims=True)
            dl_ref[:, _hs(hd)] = jnp.broadcast_to(s, (tm, HEAD_DIM))
            dob_ref[:, _hs(hd)] = d.astype(BF16)

    return pl.pallas_call(
        body, name=name, grid=(t // tm,), in_specs=[_rows(width, tm), _rows(width, tm)],
        out_specs=[_rows(width, tm), _rows(width, tm)],
        out_shape=[jax.ShapeDtypeStruct((t, width), F32), jax.ShapeDtypeStruct((t, width), BF16)],
        compiler_params=_params(("parallel",)),
    )(do, o)


def _a_specs(n_q, n_kv, bq, bk, q_major):
    grp = n_q // n_kv
    if q_major:
        qm, km = (lambda b, i, s: (i, b)), (lambda b, i, s: (s, n_q + b))
        vm = lambda b, i, s: (s, n_q + n_kv + b)
    else:
        qm, km = (lambda b, i, s: (s, b)), (lambda b, i, s: (i, n_q + b))
        vm = lambda b, i, s: (i, n_q + n_kv + b)
    return (pl.BlockSpec((bq, grp * HEAD_DIM), qm), pl.BlockSpec((bk, HEAD_DIM), km),
            pl.BlockSpec((bk, HEAD_DIM), vm))


def mixer_a_fwd(qkv_r, n_q, n_kv):
    t = qkv_r.shape[0]
    bq, bk = _tile(t, A_BQ), _tile(t, A_BK)
    q_spec, k_spec, v_spec = _a_specs(n_q, n_kv, bq, bk, True)
    return attn_fwd("a_attn_fwd", qkv_r, qkv_r, qkv_r, None, grid=(n_kv, t // bq, t // bk),
                    q_spec=q_spec, k_spec=k_spec, v_spec=v_spec, b_spec=None, o_spec=q_spec, valid=_always,
                    nh=n_q // n_kv, shared_kv=True, bq=bq, bk=bk, o_shape=(t, n_q * HEAD_DIM), o_dtype=BF16)


def mixer_a_bwd(qkv_r, do_b, lse, dlt, n_q, n_kv):
    t = qkv_r.shape[0]
    bq, bk = _tile(t, A_BQ), _tile(t, A_BK)
    grp = n_q // n_kv
    q_spec, k_spec, v_spec = _a_specs(n_q, n_kv, bq, bk, True)
    dq = attn_bwd_dq("a_attn_dq", qkv_r, qkv_r, qkv_r, do_b, lse, dlt, grid=(n_kv, t // bq, t // bk),
                     q_spec=q_spec, k_spec=k_spec, v_spec=v_spec, nh=grp, bq=bq, bk=bk,
                     o_shape=(t, n_q * HEAD_DIM))
    q_spec, k_spec, v_spec = _a_specs(n_q, n_kv, bq, bk, False)
    o_spec = pl.BlockSpec((bk, HEAD_DIM), lambda b, i, s: (i, b))
    dk, dv = _attn_bwd_dkv_out(qkv_r, do_b, lse, dlt, grid=(n_kv, t // bk, t // bq), q_spec=q_spec,
                               k_spec=k_spec, v_spec=v_spec, o_spec=o_spec, grp=grp, bq=bq, bk=bk,
                               o_shape=(t, n_kv * HEAD_DIM))
    return dq, dk, dv


def _attn_bwd_dkv_out(qkv_r, do_b, lse, dlt, *, grid, q_spec, k_spec, v_spec, o_spec, grp, bq, bk, o_shape):
    ns = grid[2]
    scale = HEAD_DIM ** -0.5

    def body(q_ref, k_ref, v_ref, do_ref, lse_ref, dlt_ref, dk_ref, dv_ref, dk_s, dv_s):
        step = pl.program_id(2)

        @pl.when(step == 0)
        def _():
            dk_s[...] = jnp.zeros_like(dk_s)
            dv_s[...] = jnp.zeros_like(dv_s)

        for hd in range(grp):
            p, ds = _probs(q_ref, k_ref, v_ref, do_ref, lse_ref, dlt_ref, None, hd, _hs(0), bq, bk)
            dv_s[...] += _dot(p, do_ref[:, _hs(hd)], TN)
            dk_s[...] += _dot(ds, q_ref[:, _hs(hd)], TN)

        @pl.when(step == ns - 1)
        def _():
            dk_ref[...] = dk_s[...] * scale
            dv_ref[...] = dv_s[...]

    acc = pltpu.VMEM((bk, HEAD_DIM), F32)
    return pl.pallas_call(
        body, name="a_attn_dkv", grid=grid, in_specs=[q_spec, k_spec, v_spec, q_spec, q_spec, q_spec],
        out_specs=[o_spec, o_spec], out_shape=[jax.ShapeDtypeStruct(o_shape, F32)] * 2,
        scratch_shapes=[acc, acc], compiler_params=_params(("parallel", "parallel", "arbitrary")),
    )(qkv_r, qkv_r, qkv_r, do_b, lse, dlt)


def t5_bucket(rel):
    nb = REL_BUCKETS // 2
    max_exact = nb // 2
    base = jnp.where(rel > 0, nb, 0)
    n = jnp.abs(rel)
    nf = jnp.maximum(n, 1).astype(F32)
    large = max_exact + (jnp.log(nf / max_exact) / math.log(REL_MAX_DISTANCE / max_exact)
                         * (nb - max_exact)).astype(jnp.int32)
    large = jnp.minimum(large, nb - 1)
    return base + jnp.where(n < max_exact, n, large)


def band_tables(rel_bias_g, half_span, dil, bq):
    a = jnp.arange(bq)[:, None]
    b = jnp.arange(bq)[None, :]
    rel = jnp.stack([(s - 1) * bq + b - a for s in range(3)])
    ok = jnp.abs(rel) <= half_span
    bucket = t5_bucket(rel * dil)
    bias = jnp.zeros((rel_bias_g.shape[1],) + rel.shape, F32)
    for r in range(REL_BUCKETS):
        bias = bias + jnp.where(bucket[None] == r, rel_bias_g[r][:, None, None, None], 0.0)
    return jnp.where(ok[None], bias, NEG_INF), jnp.where(ok, bucket, -1).astype(jnp.int32)


def _b_geometry(t, dil, g, n_groups):
    hg = B_HEADS_PER_GROUP
    length = t // dil
    bq = _tile(length, B_BQ)
    nblk = length // bq
    gw = hg * HEAD_DIM
    per_tok = 3 * n_groups
    return hg, length, bq, nblk, gw, per_tok


def mixer_b_group_fwd(qkv, bias, dil, g, n_groups):
    t = qkv.shape[0]
    hg, length, bq, nblk, gw, per_tok = _b_geometry(t, dil, g, n_groups)
    view = qkv.reshape(length, dil * qkv.shape[1])
    col = lambda c, which: c * per_tok + 3 * g + which
    kblk = lambda i, s: jnp.clip(i - 1 + s, 0, nblk - 1)
    spec = lambda which, streamed: pl.BlockSpec(
        (bq, gw), (lambda c, i, s: (kblk(i, s), col(c, which))) if streamed else (lambda c, i, s: (i, col(c, which))))
    valid = lambda i, s: (i - 1 + s >= 0) & (i - 1 + s < nblk)
    o, lz = attn_fwd(f"b_attn_fwd_d{dil}", view, view, view, bias, grid=(dil, nblk, 3),
                     q_spec=spec(0, False), k_spec=spec(1, True), v_spec=spec(2, True),
                     b_spec=pl.BlockSpec((hg, None, bq, bq), lambda c, i, s: (0, s, 0, 0)),
                     o_spec=pl.BlockSpec((bq, gw), lambda c, i, s: (i, c)), valid=valid, nh=hg,
                     shared_kv=False, bq=bq, bk=bq, o_shape=(length, dil * gw), o_dtype=F32)
    return o.reshape(t, gw), lz.reshape(t, gw)


def mixer_b_group_bwd(qkv, bias, do_g, lz_g, dlt_g, dil, g, n_groups):
    t = qkv.shape[0]
    hg, length, bq, nblk, gw, per_tok = _b_geometry(t, dil, g, n_groups)
    view = qkv.reshape(length, dil * qkv.shape[1])
    dov, lzv, dlv = (x.reshape(length, dil * gw) for x in (do_g, lz_g, dlt_g))
    col = lambda c, which: c * per_tok + 3 * g + which
    nbr = lambda i, s: jnp.clip(i - 1 + s, 0, nblk - 1)
    valid = lambda i, s: (i - 1 + s >= 0) & (i - 1 + s < nblk)
    q_spec = pl.BlockSpec((bq, gw), lambda c, i, s: (i, col(c, 0)))
    k_spec = pl.BlockSpec((bq, gw), lambda c, i, s: (nbr(i, s), col(c, 1)))
    v_spec = pl.BlockSpec((bq, gw), lambda c, i, s: (nbr(i, s), col(c, 2)))
    stat = pl.BlockSpec((bq, gw), lambda c, i, s: (i, c))
    dq, dbias = _band_bwd_dq(f"b_attn_dq_d{dil}", view, dov, lzv, dlv, bias, grid=(dil, nblk, 3),
                             q_spec=q_spec, k_spec=k_spec, v_spec=v_spec, stat_spec=stat,
                             b_spec=pl.BlockSpec((hg, None, bq, bq), lambda c, i, s: (0, s, 0, 0)),
                             valid=valid, nh=hg, bq=bq, o_shape=(length, dil * gw))
    q_spec = pl.BlockSpec((bq, gw), lambda c, i, s: (nbr(i, s), col(c, 0)))
    k_spec = pl.BlockSpec((bq, gw), lambda c, i, s: (i, col(c, 1)))
    v_spec = pl.BlockSpec((bq, gw), lambda c, i, s: (i, col(c, 2)))
    stat = pl.BlockSpec((bq, gw), lambda c, i, s: (nbr(i, s), c))
    dk, dv = _band_bwd_dkv(f"b_attn_dkv_d{dil}", view, dov, lzv, dlv, bias, grid=(dil, nblk, 3),
                           q_spec=q_spec, k_spec=k_spec, v_spec=v_spec, stat_spec=stat,
                           b_spec=pl.BlockSpec((hg, None, bq, bq), lambda c, i, s: (0, 2 - s, 0, 0)),
                           o_spec=pl.BlockSpec((bq, gw), lambda c, i, s: (i, c)),
                           valid=valid, nh=hg, bq=bq, o_shape=(length, dil * gw))
    return dq.reshape(t, gw), dk.reshape(t, gw), dv.reshape(t, gw), dbias


def _band_bwd_dq(name, view, do, lse, dlt, bias, *, grid, q_spec, k_spec, v_spec, stat_spec, b_spec, valid,
                 nh, bq, o_shape):
    scale = HEAD_DIM ** -0.5
    bias_shape = (nh, 3, bq, bq)

    def body(q_ref, k_ref, v_ref, do_ref, lse_ref, dlt_ref, b_ref, dq_ref, db_ref, acc_s):
        step = pl.program_id(2)

        @pl.when((pl.program_id(0) == 0) & (pl.program_id(1) == 0) & (step == 0))
        def _():
            db_ref[...] = jnp.zeros_like(db_ref)

        @pl.when(step == 0)
        def _():
            acc_s[...] = jnp.zeros_like(acc_s)

        @pl.when(valid(pl.program_id(1), step))
        def _():
            for hd in range(nh):
                def add_bias_grad(rows, ds, hd=hd):
                    db_ref[hd, step, rows, :] += ds

                _, ds = _probs(q_ref, k_ref, v_ref, do_ref, lse_ref, dlt_ref, b_ref, hd, _hs(hd), bq, bq,
                               want_p=False, on_ds=add_bias_grad)
                acc_s[hd] += _dot(ds, k_ref[:, _hs(hd)], NN)

        @pl.when(step == 2)
        def _():
            for hd in range(nh):
                dq_ref[:, _hs(hd)] = (acc_s[hd] * scale).astype(BF16)

    return pl.pallas_call(
        body, name=name, grid=grid,
        in_specs=[q_spec, k_spec, v_spec, stat_spec, stat_spec, stat_spec, b_spec],
        out_specs=[stat_spec, pl.BlockSpec(bias_shape, lambda c, i, s: (0, 0, 0, 0))],
        out_shape=[jax.ShapeDtypeStruct(o_shape, BF16), jax.ShapeDtypeStruct(bias_shape, F32)],
        scratch_shapes=[pltpu.VMEM((nh, bq, LANES), F32)], compiler_params=_params(("arbitrary",) * 3),
    )(view, view, view, do, lse, dlt, bias)


def _band_bwd_dkv(name, view, do, lse, dlt, bias, *, grid, q_spec, k_spec, v_spec, stat_spec, b_spec, o_spec,
                  valid, nh, bq, o_shape):
    scale = HEAD_DIM ** -0.5

    def body(q_ref, k_ref, v_ref, do_ref, lse_ref, dlt_ref, b_ref, dk_ref, dv_ref, dk_s, dv_s):
        step = pl.program_id(2)

        @pl.when(step == 0)
        def _():
            dk_s[...] = jnp.zeros_like(dk_s)
            dv_s[...] = jnp.zeros_like(dv_s)

        @pl.when(valid(pl.program_id(1), step))
        def _():
            for hd in range(nh):
                p, ds = _probs(q_ref, k_ref, v_ref, do_ref, lse_ref, dlt_ref, b_ref, hd, _hs(hd), bq, bq)
                dv_s[hd] += _dot(p, do_ref[:, _hs(hd)], TN)
                dk_s[hd] += _dot(ds, q_ref[:, _hs(hd)], TN)

        @pl.when(step == 2)
        def _():
            for hd in range(nh):
                dk_ref[:, _hs(hd)] = (dk_s[hd] * scale).astype(BF16)
                dv_ref[:, _hs(hd)] = dv_s[hd].astype(BF16)

    acc = pltpu.VMEM((nh, bq, LANES), F32)
    return pl.pallas_call(
        body, name=name, grid=grid,
        in_specs=[q_spec, k_spec, v_spec, stat_spec, stat_spec, stat_spec, b_spec],
        out_specs=[o_spec, o_spec], out_shape=[jax.ShapeDtypeStruct(o_shape, BF16)] * 2,
        scratch_shapes=[acc, acc], compiler_params=_params(("parallel", "parallel", "arbitrary")),
    )(view, view, view, do, lse, dlt, bias)


def bias_bucket_sums(name, dbias, bucket):
    nh, _, bq, _ = dbias.shape
    db2 = dbias.reshape(nh, 3 * bq, bq)
    bk2 = bucket.reshape(3 * bq, bq)

    def body(db_ref, bk_ref, o_ref):
        row = lax.broadcasted_iota(jnp.int32, (nh, LANES), 0)
        lane = lax.broadcasted_iota(jnp.int32, (nh, LANES), 1)
        out = jnp.zeros((nh, LANES), F32)
        bkt = bk_ref[...]
        for hd in range(nh):
            x = db_ref[hd]
            for r in range(REL_BUCKETS):
                part = jnp.sum(jnp.where(bkt == r, x, 0.0), axis=1, keepdims=True)
                tot = jnp.sum(part, axis=0, keepdims=True)
                out = out + jnp.where((row == hd) & (lane == r), tot, 0.0)
        o_ref[...] = out

    return pl.pallas_call(
        body, name=name, out_shape=jax.ShapeDtypeStruct((nh, LANES), F32),
        compiler_params=pltpu.CompilerParams(vmem_limit_bytes=VMEM_LIMIT),
    )(db2, bk2)


def combine_fwd(name, outs, lzs):
    n_g = len(outs)
    t, gw = outs[0].shape
    tm = _tile(t, ROW_TILE)

    def body(*refs):
        o_refs, lz_refs, y_ref = refs[:n_g], refs[n_g:2 * n_g], refs[2 * n_g]
        lz = [r[...] for r in lz_refs]
        mx = functools.reduce(jnp.maximum, lz)
        e = [jnp.exp(x - mx) for x in lz]
        den = functools.reduce(lambda a, b: a + b, e)
        for g in range(n_g):
            y_ref[:, g * gw:(g + 1) * gw] = (e[g] / den * o_refs[g][...]).astype(BF16)

    return pl.pallas_call(
        body, name=name, grid=(t // tm,), in_specs=[_rows(gw, tm)] * (2 * n_g), out_specs=_rows(n_g * gw, tm),
        out_shape=jax.ShapeDtypeStruct((t, n_g * gw), BF16), compiler_params=_params(("parallel",)),
    )(*outs, *lzs)


def combine_bwd(name, dy, outs, lzs):
    n_g = len(outs)
    t, gw = outs[0].shape
    tm = _tile(t, ROW_TILE)
    nh = gw // HEAD_DIM

    def body(*refs):
        dy_ref = refs[0]
        o_refs, lz_refs = refs[1:1 + n_g], refs[1 + n_g:1 + 2 * n_g]
        do_refs, dl_refs = refs[1 + 2 * n_g:1 + 3 * n_g], refs[1 + 3 * n_g:]
        lz = [r[...] for r in lz_refs]
        mx = functools.reduce(jnp.maximum, lz)
        e = [jnp.exp(x - mx) for x in lz]
        den = functools.reduce(lambda a, b: a + b, e)
        wts = [x / den for x in e]
        for g in range(n_g):
            do_refs[g][...] = (wts[g] * dy_ref[:, g * gw:(g + 1) * gw]).astype(BF16)
        for hd in range(nh):
            mix = jnp.zeros((tm, HEAD_DIM), F32)
            for g in range(n_g):
                prod = dy_ref[:, g * gw + hd * HEAD_DIM:g * gw + (hd + 1) * HEAD_DIM] * o_refs[g][:, _hs(hd)]
                dw = jnp.broadcast_to(jnp.sum(prod, axis=-1, keepdims=True), (tm, HEAD_DIM))
                mix = mix + wts[g][:, _hs(hd)] * dw
            for g in range(n_g):
                dl_refs[g][:, _hs(hd)] = wts[g][:, _hs(hd)] * mix

    return pl.pallas_call(
        body, name=name, grid=(t // tm,),
        in_specs=[_rows(n_g * gw, tm)] + [_rows(gw, tm)] * (2 * n_g),
        out_specs=[_rows(gw, tm)] * (2 * n_g),
        out_shape=[jax.ShapeDtypeStruct((t, gw), BF16)] * n_g + [jax.ShapeDtypeStruct((t, gw), F32)] * n_g,
        compiler_params=_params(("parallel",)),
    )(dy, *outs, *lzs)


def _conv3(u, w_ref, b):
    t = u.shape[0]
    row = lax.broadcasted_iota(jnp.int32, u.shape, 0)
    prev = jnp.where(row == 0, 0.0, pltpu.roll(u, 1, 0))
    nxt = jnp.where(row == t - 1, 0.0, pltpu.roll(u, t - 1, 0))
    out = w_ref[0:1, :] * prev + w_ref[1:2, :] * u + w_ref[2:3, :] * nxt
    return out if b is None else out + b


def _conv3_t(d, w_ref):
    t = d.shape[0]
    row = lax.broadcasted_iota(jnp.int32, d.shape, 0)
    prev = jnp.where(row == 0, 0.0, pltpu.roll(d, 1, 0))
    nxt = jnp.where(row == t - 1, 0.0, pltpu.roll(d, t - 1, 0))
    return w_ref[0:1, :] * nxt + w_ref[1:2, :] * d + w_ref[2:3, :] * prev


def conv_act_fwd(name, u2, cw2, cb2):
    _, t, dff = u2.shape
    tn = LANES

    def body(u_ref, w_ref, b_ref, o_ref):
        cg = _conv3(u_ref[0], w_ref.at[0], b_ref[0])
        cv = _conv3(u_ref[1], w_ref.at[1], b_ref[1])
        o_ref[...] = (cg * jax.nn.sigmoid(cg) * cv).astype(BF16)

    return pl.pallas_call(
        body, name=name, grid=(dff // tn,),
        in_specs=[pl.BlockSpec((2, t, tn), lambda j: (0, 0, j)), pl.BlockSpec((2, 3, tn), lambda j: (0, 0, j)),
                  pl.BlockSpec((2, 1, tn), lambda j: (0, 0, j))],
        out_specs=pl.BlockSpec((t, tn), lambda j: (0, j)), out_shape=jax.ShapeDtypeStruct((t, dff), BF16),
        compiler_params=_params(("parallel",)),
    )(u2, cw2, cb2)


def conv_act_bwd(name, u2, cw2, cb2, dact):
    _, t, dff = u2.shape
    tn = LANES

    def body(u_ref, w_ref, b_ref, d_ref, du_ref, dw_ref):
        d = d_ref[...]
        ug, uv = u_ref[0], u_ref[1]
        cg = _conv3(ug, w_ref.at[0], b_ref[0])
        cv = _conv3(uv, w_ref.at[1], b_ref[1])
        sg = jax.nn.sigmoid(cg)
        dcv = d * (cg * sg)
        dcg = d * cv * (sg * (1.0 + cg * (1.0 - sg)))
        du_ref[0] = _conv3_t(dcg, w_ref.at[0]).astype(BF16)
        du_ref[1] = _conv3_t(dcv, w_ref.at[1]).astype(BF16)
        row = lax.broadcasted_iota(jnp.int32, ug.shape, 0)
        for half, (dc, u) in enumerate(((dcg, ug), (dcv, uv))):
            prev = jnp.where(row == 0, 0.0, pltpu.roll(u, 1, 0))
            nxt = jnp.where(row == t - 1, 0.0, pltpu.roll(u, t - 1, 0))
            for tap, x in enumerate((prev, u, nxt)):
                dw_ref[half, tap:tap + 1, :] = jnp.sum(dc * x, axis=0, keepdims=True)
            dw_ref[half, 3:4, :] = jnp.sum(dc, axis=0, keepdims=True)
            dw_ref[half, 4:8, :] = jnp.zeros((4, tn), F32)

    return pl.pallas_call(
        body, name=name, grid=(dff // tn,),
        in_specs=[pl.BlockSpec((2, t, tn), lambda j: (0, 0, j)), pl.BlockSpec((2, 3, tn), lambda j: (0, 0, j)),
                  pl.BlockSpec((2, 1, tn), lambda j: (0, 0, j)), pl.BlockSpec((t, tn), lambda j: (0, j))],
        out_specs=[pl.BlockSpec((2, t, tn), lambda j: (0, 0, j)), pl.BlockSpec((2, 8, tn), lambda j: (0, 0, j))],
        out_shape=[jax.ShapeDtypeStruct((2, t, dff), BF16), jax.ShapeDtypeStruct((2, 8, dff), F32)],
        compiler_params=_params(("parallel",)),
    )(u2, cw2, cb2, dact)


GATHER_ID, SIBLING_ID, CHIPS_ID = 0, 1, 2


def _place():
    x, y, c = lax.axis_index("x"), lax.axis_index("y"), lax.axis_index("c")
    chips = [(1 - x, y), (x, 1 - y), (1 - x, 1 - y)]
    return x, y, c, chips


def _handshake(peers):
    barrier = pltpu.get_barrier_semaphore()
    for peer in peers:
        pl.semaphore_signal(barrier, inc=1, device_id=peer, device_id_type=MESH)
    pl.semaphore_wait(barrier, len(peers))


def _sequencer(name, body, out_type, scratch_types, collective_id):
    return pl.kernel(body, out_type=out_type, mesh=plsc.ScalarSubcoreMesh(axis_name="seq", num_cores=1),
                     scratch_types=scratch_types, name=name,
                     compiler_params=pltpu.CompilerParams(collective_id=collective_id))


def _gather_body(n):
    def body(*refs):
        src, out = refs[:n], refs[n:2 * n]
        send, recv, loc = refs[2 * n:]
        x, y, c, chips = _place()
        sibling = (x, y, 1 - c)
        _handshake([sibling] + [(*chip, c) for chip in chips])

        def slot(a, px, py, pc):
            return out[a].at[4 * px + 2 * py + pc]

        def copy(a, k, block, to, from_src=False):
            return pltpu.make_async_remote_copy(
                src_ref=src[a] if from_src else slot(a, *block), dst_ref=slot(a, *block),
                send_sem=send.at[a, k], recv_sem=recv.at[a, k], device_id=to, device_id_type=MESH)

        mine = [pltpu.make_async_copy(src[a], slot(a, x, y, c), loc.at[a]) for a in range(n)]
        for cp in mine:
            cp.start()
        first = []
        for a in range(n):
            first.append(copy(a, 0, (x, y, c), sibling, True))
            first += [copy(a, 1 + j, (x, y, c), (*chip, c), True) for j, chip in enumerate(chips)]
        for cp in first:
            cp.start()
        passed = []
        for j, chip in enumerate(chips):
            for a in range(n):
                copy(a, 1 + j, (*chip, c), (x, y, c)).wait_recv()
                cp = copy(a, 4 + j, (*chip, c), sibling)
                cp.start()
                passed.append(cp)
        for a in range(n):
            copy(a, 0, sibling, (x, y, c)).wait_recv()
            for j, chip in enumerate(chips):
                copy(a, 4 + j, (*chip, 1 - c), (x, y, c)).wait_recv()
        for cp in first + passed:
            cp.wait_send()
        for cp in mine:
            cp.wait()

    return body


def gather_layer(name, shards):
    n = len(shards)
    out_type = [jax.ShapeDtypeStruct((N_DEV,) + s.shape, s.dtype) for s in shards]
    scratch = [pltpu.SemaphoreType.DMA((n, 7)), pltpu.SemaphoreType.DMA((n, 7)), pltpu.SemaphoreType.DMA((n,))]
    return _sequencer(name, _gather_body(n), out_type, scratch, GATHER_ID)(*shards)


def _to_sibling_body(n):
    def body(*refs):
        src, got = refs[:n], refs[n:2 * n]
        send, recv = refs[2 * n:]
        x, y, c, _ = _place()
        sibling = (x, y, 1 - c)
        _handshake([sibling])
        remote = []
        for a in range(n):
            for q in range(4):
                remote.append(pltpu.make_async_remote_copy(
                    src_ref=src[a].at[2 * q + 1 - c], dst_ref=got[a].at[q], send_sem=send.at[a, q],
                    recv_sem=recv.at[a, q], device_id=sibling, device_id_type=MESH))
        for cp in remote:
            cp.start()
        for cp in remote:
            cp.wait()

    return body


def grads_to_sibling(name, grads):
    n = len(grads)
    out_type = [jax.ShapeDtypeStruct((4,) + g.shape[1:], g.dtype) for g in grads]
    scratch = [pltpu.SemaphoreType.DMA((n, 4)), pltpu.SemaphoreType.DMA((n, 4))]
    return _sequencer(name, _to_sibling_body(n), out_type, scratch, SIBLING_ID)(*grads)


def _to_chips_body(n):
    def body(*refs):
        src, got = refs[:n], refs[n:2 * n]
        send, recv = refs[2 * n:]
        x, y, c, chips = _place()
        _handshake([(*chip, c) for chip in chips])
        remote = []
        for a in range(n):
            for j, (px, py) in enumerate(chips):
                remote.append(pltpu.make_async_remote_copy(
                    src_ref=src[a].at[2 * px + py], dst_ref=got[a].at[j], send_sem=send.at[a, j],
                    recv_sem=recv.at[a, j], device_id=(px, py, c), device_id_type=MESH))
        for cp in remote:
            cp.start()
        for cp in remote:
            cp.wait()

    return body


def grads_to_chips(name, parts):
    n = len(parts)
    out_type = [jax.ShapeDtypeStruct((3,) + p.shape[1:], p.dtype) for p in parts]
    scratch = [pltpu.SemaphoreType.DMA((n, 3)), pltpu.SemaphoreType.DMA((n, 3))]
    return _sequencer(name, _to_chips_body(n), out_type, scratch, CHIPS_ID)(*parts)


def all_reduce_small(name, vec):
    rows, m = vec.shape

    def body(x_ref, o_ref, buf, send, recv):
        x, y, c, chips = _place()
        sibling = (x, y, 1 - c)

        def blk(px, py, pc):
            return buf.at[pl.ds(pl.multiple_of((4 * px + 2 * py + pc) * rows, rows), rows), :]

        def copy(k, block, to):
            return pltpu.make_async_remote_copy(src_ref=blk(*block), dst_ref=blk(*block), send_sem=send.at[k],
                                                recv_sem=recv.at[k], device_id=to, device_id_type=MESH)

        blk(x, y, c)[...] = x_ref[...]
        first = [copy(0, (x, y, c), sibling)] + [copy(1 + j, (x, y, c), (*chip, c)) for j, chip in enumerate(chips)]
        for cp in first:
            cp.start()
        passed = [copy(4 + j, (*chip, c), sibling) for j, chip in enumerate(chips)]
        for j, chip in enumerate(chips):
            copy(1 + j, (*chip, c), (x, y, c)).wait_recv()
            passed[j].start()
        copy(0, sibling, (x, y, c)).wait_recv()
        for j, chip in enumerate(chips):
            copy(4 + j, (*chip, 1 - c), (x, y, c)).wait_recv()
        for cp in first + passed:
            cp.wait_send()
        tot = buf[0:rows, :]
        for dev in range(1, N_DEV):
            tot = tot + buf[dev * rows:(dev + 1) * rows, :]
        o_ref[...] = tot

    return pl.pallas_call(
        body, name=name, in_specs=[pl.BlockSpec(memory_space=pltpu.VMEM)],
        out_specs=pl.BlockSpec(memory_space=pltpu.VMEM), out_shape=jax.ShapeDtypeStruct((rows, m), F32),
        scratch_shapes=[pltpu.VMEM((N_DEV * rows, m), F32), pltpu.SemaphoreType.DMA((7,)),
                        pltpu.SemaphoreType.DMA((7,))],
        compiler_params=pltpu.CompilerParams(vmem_limit_bytes=VMEM_LIMIT),
    )(vec)


def _ew_tiles(rows, cols, max_elems=1 << 18):
    tr = rows
    for cand in (1024, 512, 256, 128, 64, 32, 16):
        if rows % cand == 0 and cand * cols <= max_elems:
            tr = cand
            break
    return tr


def chip_sum(name, full, got, core):
    _, kdim, ncol = full.shape
    tr = _ew_tiles(kdim, ncol, max_elems=1 << 20)
    blk = (None, tr, ncol)
    by_chip = pl.BlockSpec(blk, lambda q, i, c: (q, i, 0))

    def body(c_ref, a_ref, b_ref, o_ref):
        o_ref[...] = (a_ref[...].astype(F32) + b_ref[...].astype(F32)).astype(BF16)

    return pl.pallas_call(
        body, name=name,
        grid_spec=pltpu.PrefetchScalarGridSpec(
            num_scalar_prefetch=1, grid=(4, kdim // tr),
            in_specs=[pl.BlockSpec(blk, lambda q, i, c: (2 * q + c[0], i, 0)), by_chip], out_specs=by_chip),
        out_shape=jax.ShapeDtypeStruct((4, kdim, ncol), BF16),
        compiler_params=_params(("parallel", "parallel")),
    )(core, full, got)


def _adamw_math(w, g, m, v):
    m = ADAM_B1 * m + (1.0 - ADAM_B1) * g
    v = ADAM_B2 * v + (1.0 - ADAM_B2) * (g * g)
    m_hat = m / (1.0 - ADAM_B1 ** ADAM_STEP)
    v_hat = v / (1.0 - ADAM_B2 ** ADAM_STEP)
    delta = -ADAM_LR * (m_hat / (jnp.sqrt(v_hat) + ADAM_EPS) + ADAM_WD * w)
    return delta, m, v


def adamw_layer(name, sums, got, w, m, v, layer, chip, after):
    _, kdim, ncol = sums.shape
    tr = _ew_tiles(kdim, ncol)
    out = pl.BlockSpec((tr, ncol), lambda i, q: (i, 0))
    mine = pl.BlockSpec((None, tr, ncol), lambda i, q: (q[0], i, 0))
    others = pl.BlockSpec((3, tr, ncol), lambda i, q: (0, i, 0))
    param = pl.BlockSpec((None, tr, ncol), lambda i, q: (layer, i, 0))

    def body(q_ref, o_ref, g_ref, w_ref, m_ref, v_ref, after_ref, go_ref, d_ref, mo_ref, vo_ref):
        g = o_ref[...].astype(F32)
        for j in range(3):
            g = g + g_ref[j].astype(F32)
        d, mn, vn = _adamw_math(w_ref[...], g, m_ref[...], v_ref[...])
        go_ref[...] = g
        d_ref[...] = d
        mo_ref[...] = mn
        vo_ref[...] = vn

    return pl.pallas_call(
        body, name=name,
        grid_spec=pltpu.PrefetchScalarGridSpec(
            num_scalar_prefetch=1, grid=(kdim // tr,),
            in_specs=[mine, others, param, param, param, pl.BlockSpec(memory_space=pl.ANY)], out_specs=[out] * 4),
        out_shape=[jax.ShapeDtypeStruct((kdim, ncol), F32)] * 4,
        compiler_params=_params(("parallel",)),
    )(chip, sums, got, w, m, v, after)


def adamw_small(name, g, w, m, v):
    def body(g_ref, w_ref, m_ref, v_ref, d_ref, mo_ref, vo_ref):
        d, mn, vn = _adamw_math(w_ref[...], g_ref[...], m_ref[...], v_ref[...])
        d_ref[...] = d
        mo_ref[...] = mn
        vo_ref[...] = vn

    vm = pl.BlockSpec(memory_space=pltpu.VMEM)
    return pl.pallas_call(
        body, name=name, in_specs=[vm] * 4, out_specs=[vm] * 3,
        out_shape=[jax.ShapeDtypeStruct(g.shape, F32)] * 3,
        compiler_params=pltpu.CompilerParams(vmem_limit_bytes=VMEM_LIMIT),
    )(g, w, m, v)


def _pack(parts, width):
    flat = jnp.concatenate([p.reshape(-1).astype(F32) for p in parts])
    pad = (-flat.shape[0]) % width
    return jnp.pad(flat, (0, pad)).reshape(-1, width) if pad else flat.reshape(-1, width)


def _unpack(packed, shapes):
    flat = packed.reshape(-1)
    out, off = [], 0
    for s in shapes:
        size = math.prod(s)
        out.append(flat[off:off + size].reshape(s))
        off += size
    return out


def _local_step(h, target, layers, params, on_grads=None):
    a_q_gain, a_k_gain, rel_bias, mix_norm, ffn_norm, conv_b, final_norm = params
    t, d = h.shape
    depth = len(layers)
    n_groups = len(B_GROUPS)
    hg = B_HEADS_PER_GROUP
    n_kv = A_KV_HEADS
    w_a, w_b, w_u = layers[0][0].shape[2], layers[1][0].shape[2], layers[0][2].shape[2]
    n_q = w_a * N_DEV // HEAD_DIM - 2 * n_kv
    dff = layers[0][3].shape[0]
    n_a = (depth + 1) // 2
    cb_full = conv_b.reshape(depth, 2, 1, dff)

    cos, sin = rope_tables(t)
    tables = [band_tables(rel_bias[:, g * hg:(g + 1) * hg], win // (2 * dil), dil, _tile(t // dil, B_BQ))
              for g, (win, dil) in enumerate(B_GROUPS)]

    saved = []
    for i in range(depth):
        j = i // 2
        w_qkv, w_o, w_up_i, w_down_i, cw = layers[i]
        s = {"h_in": h}
        hn = rms_fwd("mix_norm_fwd", h, mix_norm[i])
        s["hn"] = hn
        if i % 2 == 0:
            qkv = mm_col_fwd("a_qkv_fwd", hn, w_qkv, F32)
            qkv_r = qk_prep_fwd("a_qk_prep_fwd", qkv, a_q_gain[j], a_k_gain[j], cos, sin, n_q, n_kv)
            o, lse = mixer_a_fwd(qkv_r, n_q, n_kv)
            s.update(qkv=qkv, qkv_r=qkv_r, o=o, lse=lse)
            h = mm_row_fwd("a_out_fwd", o, w_o, h)
        else:
            qkv = mm_col_fwd("b_qkv_fwd", hn, w_qkv, BF16)
            outs, lzs = [], []
            for g, (win, dil) in enumerate(B_GROUPS):
                o_g, lz_g = mixer_b_group_fwd(qkv, tables[g][0], dil, g, n_groups)
                outs.append(o_g)
                lzs.append(lz_g)
            y = combine_fwd("b_combine_fwd", outs, lzs)
            s.update(qkv=qkv, outs=outs, lzs=lzs, y=y)
            h = mm_row_fwd("b_out_fwd", y, w_o, h)
        s["h_mid"] = h
        hn2 = rms_fwd("ffn_norm_fwd", h, ffn_norm[i])
        u2 = mm_col_fwd("ffn_up_fwd", hn2, w_up_i, F32, split=2)
        act = conv_act_fwd("ffn_conv_act_fwd", u2, cw, cb_full[i])
        s.update(hn2=hn2, u2=u2, act=act)
        h = mm_row_fwd("ffn_down_fwd", act, w_down_i, h)
        saved.append(s)

    dh, d_final, loss_part = loss_head("loss_head", h, final_norm, target)

    d_mix, d_ffn, d_cw, d_cb = [None] * depth, [None] * depth, [None] * depth, [None] * depth
    d_qg, d_kg = [None] * n_a, [None] * n_a
    d_rel = jnp.zeros((n_groups * hg, LANES), F32)
    layer_grads = [{} for _ in range(depth)]
    pending = []

    def settle(token):
        while pending:
            i_p, part_p, finish = pending.pop()
            layer_grads[i_p][part_p] = finish(token)

    early = []

    def register(i_p, part_p, grads):
        if on_grads is None:
            layer_grads[i_p][part_p] = grads
        else:
            first, finish = on_grads(i_p, part_p, grads)
            early.extend(first)
            pending.append((i_p, part_p, finish))

    def take_early():
        first = tuple(early)
        early.clear()
        return first

    for i in reversed(range(depth)):
        j = i // 2
        w_qkv, w_o, w_up_i, w_down_i, cw = layers[i]
        s = saved[i]
        dact = mm_row_dx("ffn_down_dx", dh, w_down_i)
        g_down = mm_row_dw("ffn_down_dw", s["act"], dh, take_early())
        du2, dcw = conv_act_bwd("ffn_conv_act_bwd", s["u2"], cw, cb_full[i], dact)
        d_cw[i] = dcw[:, 0:3, :].transpose(1, 0, 2).reshape(3, 2 * dff)
        d_cb[i] = dcw[:, 3, :].reshape(2 * dff)
        g_up = mm_col_dw("ffn_up_dw", s["hn2"], du2, w_u, split=2)
        dhn2 = mm_col_dx("ffn_up_dx", du2, w_up_i, split=2)
        dh, d_ffn[i] = rms_bwd("ffn_norm_bwd", s["h_mid"], ffn_norm[i], dhn2, dh)
        settle(d_ffn[i])
        register(i, "ffn", [g_up, g_down.reshape(N_DEV, -1, d)])
        if i % 2 == 0:
            do = mm_row_dx("a_out_dx", dh, w_o)
            g_o = mm_row_dw("a_out_dw", s["o"], dh, take_early())
            dlt, do_b = row_delta("a_delta", do, s["o"], n_q)
            dq, dk, dv = mixer_a_bwd(s["qkv_r"], do_b, s["lse"], dlt, n_q, n_kv)
            dqkv, dgain = qk_prep_bwd("a_qk_prep_bwd", s["qkv"], dq, dk, dv, a_q_gain[j], a_k_gain[j], cos, sin,
                                      n_q, n_kv)
            d_qg[j], d_kg[j] = dgain[0], dgain[1]
            g_qkv = mm_col_dw("a_qkv_dw", s["hn"], dqkv, w_a)
            dhn = mm_col_dx("a_qkv_dx", dqkv, w_qkv)
        else:
            dy = mm_row_dx("b_out_dx", dh, w_o)
            g_o = mm_row_dw("b_out_dw", s["y"], dh, take_early())
            res = combine_bwd("b_combine_bwd", dy, s["outs"], s["lzs"])
            dos, dlts = res[:n_groups], res[n_groups:]
            pieces, rel_rows = [], []
            for g, (win, dil) in enumerate(B_GROUPS):
                dq, dk, dv, dbias = mixer_b_group_bwd(s["qkv"], tables[g][0], dos[g], s["lzs"][g], dlts[g], dil, g,
                                                      n_groups)
                pieces += [dq, dk, dv]
                rel_rows.append(bias_bucket_sums(f"b_bias_sums_d{dil}", dbias, tables[g][1]))
            d_rel = d_rel + jnp.concatenate(rel_rows, axis=0)
            dqkv = jnp.concatenate(pieces, axis=1)
            g_qkv = mm_col_dw("b_qkv_dw", s["hn"], dqkv, w_b)
            dhn = mm_col_dx("b_qkv_dx", dqkv, w_qkv)
        dh, d_mix[i] = rms_bwd("mix_norm_bwd", s["h_in"], mix_norm[i], dhn, dh)
        settle(d_mix[i])
        register(i, "mix", [g_qkv, g_o.reshape(N_DEV, -1, d)])
    last = pending.pop()[2] if pending else None

    d_rel_bias = d_rel[:, :REL_BUCKETS].T
    small_g = [jnp.stack(d_qg), jnp.stack(d_kg), d_rel_bias, jnp.concatenate(d_mix, 0), jnp.concatenate(d_ffn, 0),
               jnp.stack(d_cb), d_final.reshape(-1), jnp.stack(d_cw), loss_part]
    return dh, layer_grads, small_g, last


def kernel(x, a_w_qkv, a_w_o, a_q_gain, a_k_gain, b_w_qkv, b_w_o, rel_bias, mix_norm, ffn_norm, w_up, conv_w, conv_b, w_down, final_norm, loss_target, m_a_w_qkv, m_a_w_o, m_a_q_gain, m_a_k_gain, m_b_w_qkv, m_b_w_o, m_rel_bias, m_mix_norm, m_ffn_norm, m_w_up, m_conv_w, m_conv_b, m_w_down, m_final_norm, v_a_w_qkv, v_a_w_o, v_a_q_gain, v_a_k_gain, v_b_w_qkv, v_b_w_o, v_rel_bias, v_mix_norm, v_ffn_norm, v_w_up, v_conv_w, v_conv_b, v_w_down, v_final_norm):
    d = x.shape[2]
    depth = mix_norm.shape[0]
    dff = w_down.shape[1] * N_DEV
    w_u = w_up.shape[2]
    mixers = [(a_w_qkv, a_w_o, m_a_w_qkv, m_a_w_o, v_a_w_qkv, v_a_w_o),
              (b_w_qkv, b_w_o, m_b_w_qkv, m_b_w_o, v_b_w_qkv, v_b_w_o)]

    layers = []
    for i in range(depth):
        w_qkv, w_o = mixers[i % 2][0][i // 2], mixers[i % 2][1][i // 2]
        shards = [w_qkv.astype(BF16), w_o.astype(BF16), w_up[i].astype(BF16), w_down[i].astype(BF16), conv_w[i]]
        if i == 0:
            (g_qkv,) = gather_layer("gather_l0_qkv", shards[:1])
            g_o, g_up, g_down, g_cw = gather_layer("gather_l0", shards[1:])
        else:
            g_qkv, g_o, g_up, g_down, g_cw = gather_layer(f"gather_l{i}", shards)
        cw = g_cw.transpose(1, 0, 2).reshape(3, 2, dff).transpose(1, 0, 2)
        layers.append((g_qkv, g_o.reshape(-1, d), g_up, g_down.reshape(dff, d), cw))

    core = lax.axis_index("c").astype(jnp.int32).reshape(1)
    chip = (2 * lax.axis_index("x") + lax.axis_index("y")).astype(jnp.int32).reshape(1)

    def reduce_and_update(i, part, grads):
        w_qkv, w_o, m_qkv, m_o, v_qkv, v_o = mixers[i % 2]
        state = {"mix": [(w_qkv, m_qkv, v_qkv, i // 2), (w_o, m_o, v_o, i // 2)],
                 "ffn": [(w_up, m_w_up, v_w_up, i), (w_down, m_w_down, v_w_down, i)]}[part]
        got1 = grads_to_sibling(f"to_sibling_l{i}_{part}", grads)
        sums = [chip_sum(f"chip_sum_l{i}_{part}{a}", grads[a], got1[a], core) for a in range(2)]
        got2 = grads_to_chips(f"to_chips_l{i}_{part}", sums)

        def finish(token):
            return [adamw_layer(f"adamw_l{i}_{part}{a}", sums[a], got2[a], *state[a], chip, token) for a in range(2)]

        return sums, finish

    dh, updates, small_g, last = _local_step(x[0], loss_target[0], layers,
                                             (a_q_gain, a_k_gain, rel_bias, mix_norm, ffn_norm, conv_b, final_norm),
                                             reduce_and_update)
    grad_x = dh[None]

    width = 2048
    packed = _pack(small_g, N_DEV * width).reshape(-1, N_DEV, width)
    n_rows = packed.shape[0]
    packed = packed.transpose(1, 0, 2).reshape(N_DEV, n_rows * width)
    red = all_reduce_small("small_all_reduce", packed)
    updates[0]["mix"] = last(red)
    big_out = {}
    for nm, part, a in (("qkv", "mix", 0), ("o", "mix", 1)):
        for par, prefix in enumerate(("a_w_", "b_w_")):
            big_out[prefix + nm] = [jnp.stack([updates[i][part][a][k] for i in range(par, depth, 2)]) for k in range(4)]
    for nm, a in (("w_up", 0), ("w_down", 1)):
        big_out[nm] = [jnp.stack([updates[i]["ffn"][a][k] for i in range(depth)]) for k in range(4)]
    red = red.reshape(N_DEV, n_rows, width).transpose(1, 0, 2)
    (g_qg, g_kg, g_rel, g_mix, g_ffn, g_cb, g_fin, g_cw_all, loss) = _unpack(red, [p.shape for p in small_g])
    idx = 4 * lax.axis_index("x") + 2 * lax.axis_index("y") + lax.axis_index("c")
    g_cw_mine = lax.dynamic_slice_in_dim(g_cw_all, idx * w_u, w_u, axis=2)

    small_w = [a_q_gain, a_k_gain, rel_bias, mix_norm, ffn_norm, conv_b, final_norm, conv_w]
    small_m = [m_a_q_gain, m_a_k_gain, m_rel_bias, m_mix_norm, m_ffn_norm, m_conv_b, m_final_norm, m_conv_w]
    small_v = [v_a_q_gain, v_a_k_gain, v_rel_bias, v_mix_norm, v_ffn_norm, v_conv_b, v_final_norm, v_conv_w]
    small_grads = [g_qg, g_kg, g_rel, g_mix, g_ffn, g_cb, g_fin, g_cw_mine]
    shapes = [w.shape for w in small_w]
    pad_rows = (-_pack(small_w, width).shape[0]) % 8

    def pk8(parts):
        p = _pack(parts, width)
        return jnp.pad(p, ((0, pad_rows), (0, 0))) if pad_rows else p

    sd, sm, sv = adamw_small("adamw_small", pk8(small_grads), pk8(small_w), pk8(small_m), pk8(small_v))
    sd, sm, sv = _unpack(sd, shapes), _unpack(sm, shapes), _unpack(sv, shapes)

    names = ["a_w_qkv", "a_w_o", "a_q_gain", "a_k_gain", "b_w_qkv", "b_w_o", "rel_bias", "mix_norm", "ffn_norm",
             "w_up", "conv_w", "conv_b", "w_down", "final_norm"]
    small_names = ["a_q_gain", "a_k_gain", "rel_bias", "mix_norm", "ffn_norm", "conv_b", "final_norm", "conv_w"]
    grads, deltas, new_m, new_v = {}, {}, {}, {}
    for nm, outs in big_out.items():
        grads[nm], deltas[nm], new_m[nm], new_v[nm] = outs
    for a, nm in enumerate(small_names):
        grads[nm] = small_grads[a].reshape(shapes[a])
        deltas[nm], new_m[nm], new_v[nm] = sd[a], sm[a], sv[a]
    return (loss.reshape(()), grad_x, *[grads[n] for n in names], *[deltas[n] for n in names],
            *[new_m[n] for n in names], *[new_v[n] for n in names])
```

```python
import functools
import math

import jax
import jax.numpy as jnp
from jax import lax
from jax.experimental import pallas as pl
from jax.experimental.pallas import tpu as pltpu
from jax.experimental.pallas import tpu_sc as plsc

F32 = jnp.float32
BF16 = jnp.bfloat16
MESH = pl.DeviceIdType.MESH

N_DEV = 8
LANES = 128
HEAD_DIM = 128
VMEM_LIMIT = 56 * 1024 * 1024
GRID_W = 64
ROPE_THETA = 10000.0
A_KV_HEADS = 4
B_GROUPS = ((128, 1), (512, 4), (2048, 16))
B_HEADS_PER_GROUP = 8
REL_BUCKETS = 32
REL_MAX_DISTANCE = 1024
EPS = 1e-6
NEG_INF = -1e30
ADAM_LR = 0.001
ADAM_B1 = 0.9
ADAM_B2 = 0.999
ADAM_EPS = 1e-08
ADAM_WD = 0.01
ADAM_STEP = 10

ROW_TILE = 256
MM_TM = 1024
MM_TK = 2048
A_BQ = 512
A_BK = 1024
B_BQ = 256
ATTN_ROWS = 16
ATTN_SCALE = HEAD_DIM ** -0.5

NN = (((1,), (0,)), ((), ()))
NT = (((1,), (1,)), ((), ()))
TN = (((0,), (0,)), ((), ()))


def _tile(n, pref):
    return pref if n % pref == 0 else n


def _div_tile(n, pref):
    for cand in range(pref - pref % LANES, 0, -LANES):
        if n % cand == 0:
            return cand
    return n


def _params(sem):
    return pltpu.CompilerParams(dimension_semantics=sem, vmem_limit_bytes=VMEM_LIMIT)


def _dot(a, b, dims):
    return lax.dot_general(a, b, dims, preferred_element_type=F32)


def _mm(name, a, b, *, grid, a_blk, a_map, b_blk, b_map, o_blk, o_map, out_shape, out_dtype, dims,
        res=None, after=()):
    nk = grid[2]
    acc_shape = tuple(d for d in o_blk if d is not None)

    def body(*refs):
        a_ref, b_ref = refs[:2]
        r_ref = None if res is None else refs[2]
        o_ref, acc = refs[-2:]
        k = pl.program_id(2)

        @pl.when(k == 0)
        def _():
            acc[...] = jnp.zeros_like(acc)

        acc[...] += _dot(a_ref[...].astype(BF16), b_ref[...].astype(BF16), dims)

        @pl.when(k == nk - 1)
        def _():
            r = acc[...]
            if r_ref is not None:
                r = r + r_ref[...]
            o_ref[...] = r.astype(out_dtype)

    in_specs = [pl.BlockSpec(a_blk, a_map), pl.BlockSpec(b_blk, b_map)]
    args = [a, b]
    if res is not None:
        in_specs.append(pl.BlockSpec(o_blk, o_map))
        args.append(res)
    in_specs += [pl.BlockSpec(memory_space=pl.ANY)] * len(after)
    args += list(after)
    return pl.pallas_call(
        body, name=name, grid=grid, in_specs=in_specs, out_specs=pl.BlockSpec(o_blk, o_map),
        out_shape=jax.ShapeDtypeStruct(out_shape, out_dtype),
        scratch_shapes=[pltpu.VMEM(acc_shape, F32)],
        compiler_params=_params(("parallel", "parallel", "arbitrary")),
    )(*args)


def mm_col_fwd(name, a, wg, out_dtype, split=1):
    m, kdim = a.shape
    n_dev, _, w = wg.shape
    tm, tk = _tile(m, MM_TM), _div_tile(kdim, MM_TK)
    per = n_dev // split
    if split == 1:
        o_blk, o_map, o_shape = (tm, w), (lambda i, j, k: (i, j)), (m, n_dev * w)
    else:
        o_blk, o_map, o_shape = (None, tm, w), (lambda i, j, k: (j // per, i, j % per)), (split, m, per * w)
    return _mm(name, a, wg, grid=(m // tm, n_dev, kdim // tk),
               a_blk=(tm, tk), a_map=lambda i, j, k: (i, k),
               b_blk=(None, tk, w), b_map=lambda i, j, k: (j, k, 0),
               o_blk=o_blk, o_map=o_map, out_shape=o_shape, out_dtype=out_dtype, dims=NN)


def mm_col_dx(name, dy, wg, split=1):
    n_dev, kdim, w = wg.shape
    m = dy.shape[-2]
    tm, tk = _tile(m, MM_TM), _div_tile(kdim, MM_TK)
    per = n_dev // split
    if split == 1:
        a_blk, a_map = (tm, w), (lambda i, j, k: (i, k))
    else:
        a_blk, a_map = (None, tm, w), (lambda i, j, k: (k // per, i, k % per))
    return _mm(name, dy, wg, grid=(m // tm, kdim // tk, n_dev),
               a_blk=a_blk, a_map=a_map,
               b_blk=(None, tk, w), b_map=lambda i, j, k: (k, j, 0),
               o_blk=(tm, tk), o_map=lambda i, j, k: (i, j), out_shape=(m, kdim), out_dtype=F32, dims=NT)


def mm_col_dw(name, x, dy, w, split=1):
    m, kdim = x.shape
    tm, tk = _tile(m, MM_TM), _div_tile(kdim, MM_TK)
    per = N_DEV // split
    if split == 1:
        b_blk, b_map = (tm, w), (lambda i, j, k: (k, j))
    else:
        b_blk, b_map = (None, tm, w), (lambda i, j, k: (j // per, k, j % per))
    return _mm(name, x, dy, grid=(kdim // tk, N_DEV, m // tm),
               a_blk=(tm, tk), a_map=lambda i, j, k: (k, i),
               b_blk=b_blk, b_map=b_map,
               o_blk=(None, tk, w), o_map=lambda i, j, k: (j, i, 0),
               out_shape=(N_DEV, kdim, w), out_dtype=BF16, dims=TN)


def mm_row_fwd(name, a, wg, res):
    m, kdim = a.shape
    n = wg.shape[1]
    tm, tk, tn = _tile(m, MM_TM), _div_tile(kdim, MM_TK), _tile(n, 1024)
    return _mm(name, a, wg, grid=(m // tm, n // tn, kdim // tk),
               a_blk=(tm, tk), a_map=lambda i, j, k: (i, k),
               b_blk=(tk, tn), b_map=lambda i, j, k: (k, j),
               o_blk=(tm, tn), o_map=lambda i, j, k: (i, j), out_shape=(m, n), out_dtype=F32, dims=NN,
               res=res)


def mm_row_dx(name, dy, wg):
    m, n = dy.shape
    kdim = wg.shape[0]
    tm, tk, tn = _tile(m, MM_TM), _div_tile(kdim, MM_TK), _tile(n, 1024)
    return _mm(name, dy, wg, grid=(m // tm, kdim // tk, n // tn),
               a_blk=(tm, tn), a_map=lambda i, j, k: (i, k),
               b_blk=(tk, tn), b_map=lambda i, j, k: (j, k),
               o_blk=(tm, tk), o_map=lambda i, j, k: (i, j), out_shape=(m, kdim), out_dtype=F32, dims=NT)


def mm_row_dw(name, x, dy, after=()):
    m, kdim = x.shape
    n = dy.shape[1]
    tm, tk, tn = _tile(m, MM_TM), _div_tile(kdim, MM_TK), _tile(n, 1024)
    return _mm(name, x, dy, grid=(kdim // tk, n // tn, m // tm),
               a_blk=(tm, tk), a_map=lambda i, j, k: (k, i),
               b_blk=(tm, tn), b_map=lambda i, j, k: (k, j),
               o_blk=(tk, tn), o_map=lambda i, j, k: (i, j), out_shape=(kdim, n), out_dtype=BF16, dims=TN,
               after=after)


def _rows(d, tm):
    return pl.BlockSpec((tm, d), lambda i: (i, 0))


def _vec(d):
    return pl.BlockSpec((1, d), lambda i: (0, 0))


def rms_fwd(name, h, gain):
    t, d = h.shape
    tm = _tile(t, ROW_TILE)

    def body(h_ref, g_ref, o_ref):
        x = h_ref[...]
        rstd = lax.rsqrt(jnp.mean(x * x, axis=-1, keepdims=True) + EPS)
        o_ref[...] = (x * rstd * g_ref[...]).astype(BF16)

    return pl.pallas_call(
        body, name=name, grid=(t // tm,), in_specs=[_rows(d, tm), _vec(d)], out_specs=_rows(d, tm),
        out_shape=jax.ShapeDtypeStruct((t, d), BF16), compiler_params=_params(("parallel",)),
    )(h, gain.reshape(1, d))


def rms_bwd(name, h, gain, dy, dres, after=()):
    t, d = h.shape
    tm = _tile(t, ROW_TILE)

    def body(h_ref, g_ref, dy_ref, r_ref, *rest):
        dh_ref, dg_ref = rest[-2:]

        @pl.when(pl.program_id(0) == 0)
        def _():
            dg_ref[...] = jnp.zeros_like(dg_ref)

        x = h_ref[...]
        rstd = lax.rsqrt(jnp.mean(x * x, axis=-1, keepdims=True) + EPS)
        xhat = x * rstd
        dyv = dy_ref[...]
        dxhat = dyv * g_ref[...]
        dh_ref[...] = r_ref[...] + rstd * (dxhat - xhat * jnp.mean(dxhat * xhat, axis=-1, keepdims=True))
        dg_ref[...] += jnp.sum(dyv * xhat, axis=0, keepdims=True)

    return pl.pallas_call(
        body, name=name, grid=(t // tm,),
        in_specs=[_rows(d, tm), _vec(d), _rows(d, tm), _rows(d, tm)]
        + [pl.BlockSpec(memory_space=pl.ANY)] * len(after),
        out_specs=[_rows(d, tm), _vec(d)],
        out_shape=[jax.ShapeDtypeStruct((t, d), F32), jax.ShapeDtypeStruct((1, d), F32)],
        compiler_params=_params(("arbitrary",)),
    )(h, gain.reshape(1, d), dy, dres, *after)


def loss_head(name, h, gain, target):
    t, d = h.shape
    tm = _tile(t, ROW_TILE)

    def body(h_ref, g_ref, t_ref, dh_ref, dg_ref, loss_ref):
        @pl.when(pl.program_id(0) == 0)
        def _():
            dg_ref[...] = jnp.zeros_like(dg_ref)
            loss_ref[...] = jnp.zeros_like(loss_ref)

        x = h_ref[...]
        rstd = lax.rsqrt(jnp.mean(x * x, axis=-1, keepdims=True) + EPS)
        xhat = x * rstd
        err = xhat * g_ref[...] - t_ref[...]
        row = jnp.mean(err * err, axis=-1, keepdims=True)
        loss_ref[...] += 0.5 * jnp.sum(row, axis=0, keepdims=True)
        dyv = err * (1.0 / d)
        dxhat = dyv * g_ref[...]
        dh_ref[...] = rstd * (dxhat - xhat * jnp.mean(dxhat * xhat, axis=-1, keepdims=True))
        dg_ref[...] += jnp.sum(dyv * xhat, axis=0, keepdims=True)

    return pl.pallas_call(
        body, name=name, grid=(t // tm,),
        in_specs=[_rows(d, tm), _vec(d), _rows(d, tm)],
        out_specs=[_rows(d, tm), _vec(d), pl.BlockSpec((1, 1), lambda i: (0, 0))],
        out_shape=[jax.ShapeDtypeStruct((t, d), F32), jax.ShapeDtypeStruct((1, d), F32),
                   jax.ShapeDtypeStruct((1, 1), F32)],
        compiler_params=_params(("arbitrary",)),
    )(h, gain.reshape(1, d), target)


def rope_tables(seq):
    pos = jnp.arange(seq, dtype=jnp.int32)
    row_ids = (pos // GRID_W).astype(F32)
    col_ids = (pos % GRID_W).astype(F32)
    quarter = HEAD_DIM // 4
    inv_freq = ROPE_THETA ** (-jnp.arange(quarter, dtype=F32) / quarter)
    ar = row_ids[:, None] * inv_freq[None, :]
    ac = col_ids[:, None] * inv_freq[None, :]
    cos = jnp.concatenate([jnp.cos(ar), jnp.cos(ar), jnp.cos(ac), jnp.cos(ac)], axis=-1)
    sin = jnp.concatenate([-jnp.sin(ar), jnp.sin(ar), -jnp.sin(ac), jnp.sin(ac)], axis=-1)
    return cos, sin


def _swap_quarters(x):
    lane = lax.broadcasted_iota(jnp.int32, x.shape, 1)
    q = HEAD_DIM // 4
    return jnp.where((lane % (2 * q)) < q, pltpu.roll(x, HEAD_DIM - q, 1), pltpu.roll(x, q, 1))


def qk_prep_fwd(name, qkv, q_gain, k_gain, cos, sin, n_q, n_kv):
    t, width = qkv.shape
    tm = _tile(t, ROW_TILE)

    def body(x_ref, qg_ref, kg_ref, c_ref, s_ref, o_ref):
        c, s = c_ref[...], s_ref[...]
        for hd in range(n_q + n_kv):
            sl = slice(hd * HEAD_DIM, (hd + 1) * HEAD_DIM)
            x = x_ref[:, sl]
            g = qg_ref[...] if hd < n_q else kg_ref[...]
            xn = x * lax.rsqrt(jnp.mean(x * x, axis=-1, keepdims=True) + EPS) * g
            o_ref[:, sl] = (xn * c + _swap_quarters(xn) * s).astype(BF16)
        vs = slice((n_q + n_kv) * HEAD_DIM, width)
        o_ref[:, vs] = x_ref[:, vs].astype(BF16)

    return pl.pallas_call(
        body, name=name, grid=(t // tm,),
        in_specs=[_rows(width, tm), _vec(HEAD_DIM), _vec(HEAD_DIM), _rows(HEAD_DIM, tm), _rows(HEAD_DIM, tm)],
        out_specs=_rows(width, tm), out_shape=jax.ShapeDtypeStruct((t, width), BF16),
        compiler_params=_params(("parallel",)),
    )(qkv, q_gain.reshape(1, HEAD_DIM), k_gain.reshape(1, HEAD_DIM), cos, sin)


def qk_prep_bwd(name, qkv, dq, dk, dv, q_gain, k_gain, cos, sin, n_q, n_kv):
    t, width = qkv.shape
    tm = _tile(t, ROW_TILE)

    def body(x_ref, dq_ref, dk_ref, dv_ref, qg_ref, kg_ref, c_ref, s_ref, o_ref, dg_ref):
        @pl.when(pl.program_id(0) == 0)
        def _():
            dg_ref[...] = jnp.zeros_like(dg_ref)

        c, s = c_ref[...], s_ref[...]
        dgq = jnp.zeros((1, HEAD_DIM), F32)
        dgk = jnp.zeros((1, HEAD_DIM), F32)
        for hd in range(n_q + n_kv):
            sl = slice(hd * HEAD_DIM, (hd + 1) * HEAD_DIM)
            x = x_ref[:, sl]
            if hd < n_q:
                g, dout = qg_ref[...], dq_ref[:, sl]
            else:
                ks = slice((hd - n_q) * HEAD_DIM, (hd - n_q + 1) * HEAD_DIM)
                g, dout = kg_ref[...], dk_ref[:, ks]
            rstd = lax.rsqrt(jnp.mean(x * x, axis=-1, keepdims=True) + EPS)
            xhat = x * rstd
            dxn = dout * c + _swap_quarters(dout * s)
            part = jnp.sum(dxn * xhat, axis=0, keepdims=True)
            if hd < n_q:
                dgq = dgq + part
            else:
                dgk = dgk + part
            dxhat = dxn * g
            o_ref[:, sl] = (rstd * (dxhat - xhat * jnp.mean(dxhat * xhat, axis=-1, keepdims=True))).astype(BF16)
        o_ref[:, slice((n_q + n_kv) * HEAD_DIM, width)] = dv_ref[...].astype(BF16)
        dg_ref[0:1, :] += dgq
        dg_ref[1:2, :] += dgk

    kvw = n_kv * HEAD_DIM
    return pl.pallas_call(
        body, name=name, grid=(t // tm,),
        in_specs=[_rows(width, tm), _rows(n_q * HEAD_DIM, tm), _rows(kvw, tm), _rows(kvw, tm),
                  _vec(HEAD_DIM), _vec(HEAD_DIM), _rows(HEAD_DIM, tm), _rows(HEAD_DIM, tm)],
        out_specs=[_rows(width, tm), pl.BlockSpec((2, HEAD_DIM), lambda i: (0, 0))],
        out_shape=[jax.ShapeDtypeStruct((t, width), BF16), jax.ShapeDtypeStruct((2, HEAD_DIM), F32)],
        compiler_params=_params(("arbitrary",)),
    )(qkv, dq, dk, dv, q_gain.reshape(1, HEAD_DIM), k_gain.reshape(1, HEAD_DIM), cos, sin)


def _lanes(x, width):
    return jnp.tile(x, (1, width // LANES))


def _hs(hd):
    return slice(hd * HEAD_DIM, (hd + 1) * HEAD_DIM)


def attn_fwd(name, q, k, v, bias, *, grid, q_spec, k_spec, v_spec, b_spec, o_spec, valid, nh, shared_kv,
             bq, bk, o_shape, o_dtype):
    ns = grid[2]

    def body(*refs):
        if bias is None:
            q_ref, k_ref, v_ref, o_ref, lse_ref, m_s, l_s, acc_s = refs
            b_ref = None
        else:
            q_ref, k_ref, v_ref, b_ref, o_ref, lse_ref, m_s, l_s, acc_s = refs
        step = pl.program_id(2)

        @pl.when(step == 0)
        def _():
            m_s[...] = jnp.full_like(m_s, -jnp.inf)
            l_s[...] = jnp.zeros_like(l_s)
            acc_s[...] = jnp.zeros_like(acc_s)

        @pl.when(valid(pl.program_id(1), step))
        def _():
            for hd in range(nh):
                kh = _hs(0 if shared_kv else hd)
                s = _dot(q_ref[:, _hs(hd)], k_ref[:, kh], NT)
                p_rows, a_rows = [], []
                for r0 in range(0, bq, ATTN_ROWS):
                    rows = slice(r0, r0 + ATTN_ROWS)
                    z = s[rows] * ATTN_SCALE
                    if b_ref is not None:
                        z = z + b_ref[hd, rows, :]
                    m_prev = m_s[hd, rows, :]
                    m_new = jnp.maximum(m_prev, jnp.max(z, axis=-1, keepdims=True))
                    alpha = jnp.exp(m_prev - m_new)
                    p = jnp.exp(z - _lanes(m_new, bk))
                    l_s[hd, rows, :] = alpha * l_s[hd, rows, :] + jnp.sum(p, axis=-1, keepdims=True)
                    m_s[hd, rows, :] = m_new
                    p_rows.append(p.astype(BF16))
                    a_rows.append(alpha)
                pv = _dot(jnp.concatenate(p_rows, axis=0), v_ref[:, kh], NN)
                acc_s[hd] = jnp.concatenate(a_rows, axis=0) * acc_s[hd] + pv

        @pl.when(step == ns - 1)
        def _():
            for hd in range(nh):
                o_ref[:, _hs(hd)] = (acc_s[hd] / l_s[hd]).astype(o_dtype)
                lse_ref[:, _hs(hd)] = m_s[hd] + jnp.log(l_s[hd])

    in_specs = [q_spec, k_spec, v_spec] + ([] if bias is None else [b_spec])
    args = [q, k, v] + ([] if bias is None else [bias])
    stat = pltpu.VMEM((nh, bq, LANES), F32)
    return pl.pallas_call(
        body, name=name, grid=grid, in_specs=in_specs, out_specs=[o_spec, o_spec],
        out_shape=[jax.ShapeDtypeStruct(o_shape, o_dtype), jax.ShapeDtypeStruct(o_shape, F32)],
        scratch_shapes=[stat, stat, stat],
        compiler_params=_params(("parallel", "parallel", "arbitrary")),
    )(*args)


def _probs(q_ref, k_ref, v_ref, do_ref, lse_ref, dlt_ref, b_ref, hd, kh, bq, bk, want_p=True, on_ds=None):
    s = _dot(q_ref[:, _hs(hd)], k_ref[:, kh], NT)
    dp = _dot(do_ref[:, _hs(hd)], v_ref[:, kh], NT)
    p_rows, ds_rows = [], []
    for r0 in range(0, bq, ATTN_ROWS):
        rows = slice(r0, r0 + ATTN_ROWS)
        z = s[rows] * ATTN_SCALE
        if b_ref is not None:
            z = z + b_ref[hd, rows, :]
        p = jnp.exp(z - _lanes(lse_ref[rows, _hs(hd)], bk))
        ds = p * (dp[rows] - _lanes(dlt_ref[rows, _hs(hd)], bk))
        if on_ds is not None:
            on_ds(rows, ds)
        if want_p:
            p_rows.append(p.astype(BF16))
        ds_rows.append(ds.astype(BF16))
    return (jnp.concatenate(p_rows, axis=0) if want_p else None), jnp.concatenate(ds_rows, axis=0)


def attn_bwd_dq(name, q, k, v, do, lse, dlt, *, grid, q_spec, k_spec, v_spec, nh, bq, bk, o_shape):
    ns = grid[2]
    scale = HEAD_DIM ** -0.5

    def body(q_ref, k_ref, v_ref, do_ref, lse_ref, dlt_ref, dq_ref, acc_s):
        step = pl.program_id(2)

        @pl.when(step == 0)
        def _():
            acc_s[...] = jnp.zeros_like(acc_s)

        for hd in range(nh):
            _, ds = _probs(q_ref, k_ref, v_ref, do_ref, lse_ref, dlt_ref, None, hd, _hs(0), bq, bk, want_p=False)
            acc_s[hd] += _dot(ds, k_ref[:, _hs(0)], NN)

        @pl.when(step == ns - 1)
        def _():
            for hd in range(nh):
                dq_ref[:, _hs(hd)] = acc_s[hd] * scale

    return pl.pallas_call(
        body, name=name, grid=grid, in_specs=[q_spec, k_spec, v_spec, q_spec, q_spec, q_spec],
        out_specs=q_spec, out_shape=jax.ShapeDtypeStruct(o_shape, F32),
        scratch_shapes=[pltpu.VMEM((nh, bq, LANES), F32)],
        compiler_params=_params(("parallel", "parallel", "arbitrary")),
    )(q, k, v, do, lse, dlt)


def _always(i, s):
    return s >= 0


def row_delta(name, do, o, n_heads):
    t, width = do.shape
    tm = _tile(t, ROW_TILE)

    def body(do_ref, o_ref, dl_ref, dob_ref):
        for hd in range(n_heads):
            d = do_ref[:, _hs(hd)]
            s = jnp.sum(d * o_ref[:, _hs(hd)].astype(F32), axis=-1, keepdims=True)
            dl_ref[:, _hs(hd)] = jnp.broadcast_to(s, (tm, HEAD_DIM))
            dob_ref[:, _hs(hd)] = d.astype(BF16)

    return pl.pallas_call(
        body, name=name, grid=(t // tm,), in_specs=[_rows(width, tm), _rows(width, tm)],
        out_specs=[_rows(width, tm), _rows(width, tm)],
        out_shape=[jax.ShapeDtypeStruct((t, width), F32), jax.ShapeDtypeStruct((t, width), BF16)],
        compiler_params=_params(("parallel",)),
    )(do, o)


def _a_specs(n_q, n_kv, bq, bk, q_major):
    grp = n_q // n_kv
    if q_major:
        qm, km = (lambda b, i, s: (i, b)), (lambda b, i, s: (s, n_q + b))
        vm = lambda b, i, s: (s, n_q + n_kv + b)
    else:
        qm, km = (lambda b, i, s: (s, b)), (lambda b, i, s: (i, n_q + b))
        vm = lambda b, i, s: (i, n_q + n_kv + b)
    return (pl.BlockSpec((bq, grp * HEAD_DIM), qm), pl.BlockSpec((bk, HEAD_DIM), km),
            pl.BlockSpec((bk, HEAD_DIM), vm))


def mixer_a_fwd(qkv_r, n_q, n_kv):
    t = qkv_r.shape[0]
    bq, bk = _tile(t, A_BQ), _tile(t, A_BK)
    q_spec, k_spec, v_spec = _a_specs(n_q, n_kv, bq, bk, True)
    return attn_fwd("a_attn_fwd", qkv_r, qkv_r, qkv_r, None, grid=(n_kv, t // bq, t // bk),
                    q_spec=q_spec, k_spec=k_spec, v_spec=v_spec, b_spec=None, o_spec=q_spec, valid=_always,
                    nh=n_q // n_kv, shared_kv=True, bq=bq, bk=bk, o_shape=(t, n_q * HEAD_DIM), o_dtype=BF16)


def mixer_a_bwd(qkv_r, do_b, lse, dlt, n_q, n_kv):
    t = qkv_r.shape[0]
    bq, bk = _tile(t, A_BQ), _tile(t, A_BK)
    grp = n_q // n_kv
    q_spec, k_spec, v_spec = _a_specs(n_q, n_kv, bq, bk, True)
    dq = attn_bwd_dq("a_attn_dq", qkv_r, qkv_r, qkv_r, do_b, lse, dlt, grid=(n_kv, t // bq, t // bk),
                     q_spec=q_spec, k_spec=k_spec, v_spec=v_spec, nh=grp, bq=bq, bk=bk,
                     o_shape=(t, n_q * HEAD_DIM))
    q_spec, k_spec, v_spec = _a_specs(n_q, n_kv, bq, bk, False)
    o_spec = pl.BlockSpec((bk, HEAD_DIM), lambda b, i, s: (i, b))
    dk, dv = _attn_bwd_dkv_out(qkv_r, do_b, lse, dlt, grid=(n_kv, t // bk, t // bq), q_spec=q_spec,
                               k_spec=k_spec, v_spec=v_spec, o_spec=o_spec, grp=grp, bq=bq, bk=bk,
                               o_shape=(t, n_kv * HEAD_DIM))
    return dq, dk, dv


def _attn_bwd_dkv_out(qkv_r, do_b, lse, dlt, *, grid, q_spec, k_spec, v_spec, o_spec, grp, bq, bk, o_shape):
    ns = grid[2]
    scale = HEAD_DIM ** -0.5

    def body(q_ref, k_ref, v_ref, do_ref, lse_ref, dlt_ref, dk_ref, dv_ref, dk_s, dv_s):
        step = pl.program_id(2)

        @pl.when(step == 0)
        def _():
            dk_s[...] = jnp.zeros_like(dk_s)
            dv_s[...] = jnp.zeros_like(dv_s)

        for hd in range(grp):
            p, ds = _probs(q_ref, k_ref, v_ref, do_ref, lse_ref, dlt_ref, None, hd, _hs(0), bq, bk)
            dv_s[...] += _dot(p, do_ref[:, _hs(hd)], TN)
            dk_s[...] += _dot(ds, q_ref[:, _hs(hd)], TN)

        @pl.when(step == ns - 1)
        def _():
            dk_ref[...] = dk_s[...] * scale
            dv_ref[...] = dv_s[...]

    acc = pltpu.VMEM((bk, HEAD_DIM), F32)
    return pl.pallas_call(
        body, name="a_attn_dkv", grid=grid, in_specs=[q_spec, k_spec, v_spec, q_spec, q_spec, q_spec],
        out_specs=[o_spec, o_spec], out_shape=[jax.ShapeDtypeStruct(o_shape, F32)] * 2,
        scratch_shapes=[acc, acc], compiler_params=_params(("parallel", "parallel", "arbitrary")),
    )(qkv_r, qkv_r, qkv_r, do_b, lse, dlt)


def t5_bucket(rel):
    nb = REL_BUCKETS // 2
    max_exact = nb // 2
    base = jnp.where(rel > 0, nb, 0)
    n = jnp.abs(rel)
    nf = jnp.maximum(n, 1).astype(F32)
    large = max_exact + (jnp.log(nf / max_exact) / math.log(REL_MAX_DISTANCE / max_exact)
                         * (nb - max_exact)).astype(jnp.int32)
    large = jnp.minimum(large, nb - 1)
    return base + jnp.where(n < max_exact, n, large)


def band_stride(t, win, dil):
    return 1 if t % B_BQ == 0 and win // 2 <= B_BQ else dil


def band_tables(rel_bias_g, win, dil, stride, bq):
    a = jnp.arange(bq)[:, None]
    b = jnp.arange(bq)[None, :]
    rel = jnp.stack([(s - 1) * bq + b - a for s in range(3)]) * stride
    ok = (jnp.abs(rel) <= win // 2) & (rel % dil == 0)
    bucket = t5_bucket(rel)
    bias = jnp.zeros((rel_bias_g.shape[1],) + rel.shape, F32)
    for r in range(REL_BUCKETS):
        bias = bias + jnp.where(bucket[None] == r, rel_bias_g[r][:, None, None, None], 0.0)
    return jnp.where(ok[None], bias, NEG_INF), jnp.where(ok, bucket, -1).astype(jnp.int32)


def _b_geometry(t, dil, g, n_groups):
    hg = B_HEADS_PER_GROUP
    length = t // dil
    bq = _tile(length, B_BQ)
    nblk = length // bq
    gw = hg * HEAD_DIM
    per_tok = 3 * n_groups
    return hg, length, bq, nblk, gw, per_tok


def mixer_b_group_fwd(qkv, bias, dil, g, n_groups, tag):
    t = qkv.shape[0]
    hg, length, bq, nblk, gw, per_tok = _b_geometry(t, dil, g, n_groups)
    view = qkv.reshape(length, dil * qkv.shape[1])
    col = lambda c, which: c * per_tok + 3 * g + which
    kblk = lambda i, s: jnp.clip(i - 1 + s, 0, nblk - 1)
    spec = lambda which, streamed: pl.BlockSpec(
        (bq, gw), (lambda c, i, s: (kblk(i, s), col(c, which))) if streamed else (lambda c, i, s: (i, col(c, which))))
    valid = lambda i, s: (i - 1 + s >= 0) & (i - 1 + s < nblk)
    o, lz = attn_fwd(f"b_attn_fwd_d{tag}", view, view, view, bias, grid=(dil, nblk, 3),
                     q_spec=spec(0, False), k_spec=spec(1, True), v_spec=spec(2, True),
                     b_spec=pl.BlockSpec((hg, None, bq, bq), lambda c, i, s: (0, s, 0, 0)),
                     o_spec=pl.BlockSpec((bq, gw), lambda c, i, s: (i, c)), valid=valid, nh=hg,
                     shared_kv=False, bq=bq, bk=bq, o_shape=(length, dil * gw), o_dtype=F32)
    return o.reshape(t, gw), lz.reshape(t, gw)


def mixer_b_group_bwd(qkv, bias, do_g, lz_g, dlt_g, dil, g, n_groups, tag):
    t = qkv.shape[0]
    hg, length, bq, nblk, gw, per_tok = _b_geometry(t, dil, g, n_groups)
    view = qkv.reshape(length, dil * qkv.shape[1])
    dov, lzv, dlv = (x.reshape(length, dil * gw) for x in (do_g, lz_g, dlt_g))
    col = lambda c, which: c * per_tok + 3 * g + which
    nbr = lambda i, s: jnp.clip(i - 1 + s, 0, nblk - 1)
    valid = lambda i, s: (i - 1 + s >= 0) & (i - 1 + s < nblk)
    q_spec = pl.BlockSpec((bq, gw), lambda c, i, s: (i, col(c, 0)))
    k_spec = pl.BlockSpec((bq, gw), lambda c, i, s: (nbr(i, s), col(c, 1)))
    v_spec = pl.BlockSpec((bq, gw), lambda c, i, s: (nbr(i, s), col(c, 2)))
    stat = pl.BlockSpec((bq, gw), lambda c, i, s: (i, c))
    dq, dbias = _band_bwd_dq(f"b_attn_dq_d{tag}", view, dov, lzv, dlv, bias, grid=(dil, nblk, 3),
                             q_spec=q_spec, k_spec=k_spec, v_spec=v_spec, stat_spec=stat,
                             b_spec=pl.BlockSpec((hg, None, bq, bq), lambda c, i, s: (0, s, 0, 0)),
                             valid=valid, nh=hg, bq=bq, o_shape=(length, dil * gw))
    q_spec = pl.BlockSpec((bq, gw), lambda c, i, s: (nbr(i, s), col(c, 0)))
    k_spec = pl.BlockSpec((bq, gw), lambda c, i, s: (i, col(c, 1)))
    v_spec = pl.BlockSpec((bq, gw), lambda c, i, s: (i, col(c, 2)))
    stat = pl.BlockSpec((bq, gw), lambda c, i, s: (nbr(i, s), c))
    dk, dv = _band_bwd_dkv(f"b_attn_dkv_d{tag}", view, dov, lzv, dlv, bias, grid=(dil, nblk, 3),
                           q_spec=q_spec, k_spec=k_spec, v_spec=v_spec, stat_spec=stat,
                           b_spec=pl.BlockSpec((hg, None, bq, bq), lambda c, i, s: (0, 2 - s, 0, 0)),
                           o_spec=pl.BlockSpec((bq, gw), lambda c, i, s: (i, c)),
                           valid=valid, nh=hg, bq=bq, o_shape=(length, dil * gw))
    return dq.reshape(t, gw), dk.reshape(t, gw), dv.reshape(t, gw), dbias


def _band_bwd_dq(name, view, do, lse, dlt, bias, *, grid, q_spec, k_spec, v_spec, stat_spec, b_spec, valid,
                 nh, bq, o_shape):
    scale = HEAD_DIM ** -0.5
    bias_shape = (nh, 3, bq, bq)

    def body(q_ref, k_ref, v_ref, do_ref, lse_ref, dlt_ref, b_ref, dq_ref, db_ref, acc_s):
        step = pl.program_id(2)

        @pl.when((pl.program_id(0) == 0) & (pl.program_id(1) == 0) & (step == 0))
        def _():
            db_ref[...] = jnp.zeros_like(db_ref)

        @pl.when(step == 0)
        def _():
            acc_s[...] = jnp.zeros_like(acc_s)

        @pl.when(valid(pl.program_id(1), step))
        def _():
            for hd in range(nh):
                def add_bias_grad(rows, ds, hd=hd):
                    db_ref[hd, step, rows, :] += ds

                _, ds = _probs(q_ref, k_ref, v_ref, do_ref, lse_ref, dlt_ref, b_ref, hd, _hs(hd), bq, bq,
                               want_p=False, on_ds=add_bias_grad)
                acc_s[hd] += _dot(ds, k_ref[:, _hs(hd)], NN)

        @pl.when(step == 2)
        def _():
            for hd in range(nh):
                dq_ref[:, _hs(hd)] = (acc_s[hd] * scale).astype(BF16)

    return pl.pallas_call(
        body, name=name, grid=grid,
        in_specs=[q_spec, k_spec, v_spec, stat_spec, stat_spec, stat_spec, b_spec],
        out_specs=[stat_spec, pl.BlockSpec(bias_shape, lambda c, i, s: (0, 0, 0, 0))],
        out_shape=[jax.ShapeDtypeStruct(o_shape, BF16), jax.ShapeDtypeStruct(bias_shape, F32)],
        scratch_shapes=[pltpu.VMEM((nh, bq, LANES), F32)], compiler_params=_params(("arbitrary",) * 3),
    )(view, view, view, do, lse, dlt, bias)


def _band_bwd_dkv(name, view, do, lse, dlt, bias, *, grid, q_spec, k_spec, v_spec, stat_spec, b_spec, o_spec,
                  valid, nh, bq, o_shape):
    scale = HEAD_DIM ** -0.5

    def body(q_ref, k_ref, v_ref, do_ref, lse_ref, dlt_ref, b_ref, dk_ref, dv_ref, dk_s, dv_s):
        step = pl.program_id(2)

        @pl.when(step == 0)
        def _():
            dk_s[...] = jnp.zeros_like(dk_s)
            dv_s[...] = jnp.zeros_like(dv_s)

        @pl.when(valid(pl.program_id(1), step))
        def _():
            for hd in range(nh):
                p, ds = _probs(q_ref, k_ref, v_ref, do_ref, lse_ref, dlt_ref, b_ref, hd, _hs(hd), bq, bq)
                dv_s[hd] += _dot(p, do_ref[:, _hs(hd)], TN)
                dk_s[hd] += _dot(ds, q_ref[:, _hs(hd)], TN)

        @pl.when(step == 2)
        def _():
            for hd in range(nh):
                dk_ref[:, _hs(hd)] = (dk_s[hd] * scale).astype(BF16)
                dv_ref[:, _hs(hd)] = dv_s[hd].astype(BF16)

    acc = pltpu.VMEM((nh, bq, LANES), F32)
    return pl.pallas_call(
        body, name=name, grid=grid,
        in_specs=[q_spec, k_spec, v_spec, stat_spec, stat_spec, stat_spec, b_spec],
        out_specs=[o_spec, o_spec], out_shape=[jax.ShapeDtypeStruct(o_shape, BF16)] * 2,
        scratch_shapes=[acc, acc], compiler_params=_params(("parallel", "parallel", "arbitrary")),
    )(view, view, view, do, lse, dlt, bias)


def bias_bucket_sums(name, dbias, bucket):
    nh, _, bq, _ = dbias.shape
    db2 = dbias.reshape(nh, 3 * bq, bq)
    bk2 = bucket.reshape(3 * bq, bq)

    def body(db_ref, bk_ref, o_ref):
        row = lax.broadcasted_iota(jnp.int32, (nh, LANES), 0)
        lane = lax.broadcasted_iota(jnp.int32, (nh, LANES), 1)
        out = jnp.zeros((nh, LANES), F32)
        bkt = bk_ref[...]
        for hd in range(nh):
            x = db_ref[hd]
            for r in range(REL_BUCKETS):
                part = jnp.sum(jnp.where(bkt == r, x, 0.0), axis=1, keepdims=True)
                tot = jnp.sum(part, axis=0, keepdims=True)
                out = out + jnp.where((row == hd) & (lane == r), tot, 0.0)
        o_ref[...] = out

    return pl.pallas_call(
        body, name=name, out_shape=jax.ShapeDtypeStruct((nh, LANES), F32),
        compiler_params=pltpu.CompilerParams(vmem_limit_bytes=VMEM_LIMIT),
    )(db2, bk2)


def combine_fwd(name, outs, lzs):
    n_g = len(outs)
    t, gw = outs[0].shape
    tm = _tile(t, ROW_TILE)

    def body(*refs):
        o_refs, lz_refs, y_ref = refs[:n_g], refs[n_g:2 * n_g], refs[2 * n_g]
        lz = [r[...] for r in lz_refs]
        mx = functools.reduce(jnp.maximum, lz)
        e = [jnp.exp(x - mx) for x in lz]
        den = functools.reduce(lambda a, b: a + b, e)
        for g in range(n_g):
            y_ref[:, g * gw:(g + 1) * gw] = (e[g] / den * o_refs[g][...]).astype(BF16)

    return pl.pallas_call(
        body, name=name, grid=(t // tm,), in_specs=[_rows(gw, tm)] * (2 * n_g), out_specs=_rows(n_g * gw, tm),
        out_shape=jax.ShapeDtypeStruct((t, n_g * gw), BF16), compiler_params=_params(("parallel",)),
    )(*outs, *lzs)


def combine_bwd(name, dy, outs, lzs):
    n_g = len(outs)
    t, gw = outs[0].shape
    tm = _tile(t, ROW_TILE)
    nh = gw // HEAD_DIM

    def body(*refs):
        dy_ref = refs[0]
        o_refs, lz_refs = refs[1:1 + n_g], refs[1 + n_g:1 + 2 * n_g]
        do_refs, dl_refs = refs[1 + 2 * n_g:1 + 3 * n_g], refs[1 + 3 * n_g:]
        lz = [r[...] for r in lz_refs]
        mx = functools.reduce(jnp.maximum, lz)
        e = [jnp.exp(x - mx) for x in lz]
        den = functools.reduce(lambda a, b: a + b, e)
        wts = [x / den for x in e]
        for g in range(n_g):
            do_refs[g][...] = (wts[g] * dy_ref[:, g * gw:(g + 1) * gw]).astype(BF16)
        for hd in range(nh):
            mix = jnp.zeros((tm, HEAD_DIM), F32)
            for g in range(n_g):
                prod = dy_ref[:, g * gw + hd * HEAD_DIM:g * gw + (hd + 1) * HEAD_DIM] * o_refs[g][:, _hs(hd)]
                dw = jnp.broadcast_to(jnp.sum(prod, axis=-1, keepdims=True), (tm, HEAD_DIM))
                mix = mix + wts[g][:, _hs(hd)] * dw
            for g in range(n_g):
                dl_refs[g][:, _hs(hd)] = wts[g][:, _hs(hd)] * mix

    return pl.pallas_call(
        body, name=name, grid=(t // tm,),
        in_specs=[_rows(n_g * gw, tm)] + [_rows(gw, tm)] * (2 * n_g),
        out_specs=[_rows(gw, tm)] * (2 * n_g),
        out_shape=[jax.ShapeDtypeStruct((t, gw), BF16)] * n_g + [jax.ShapeDtypeStruct((t, gw), F32)] * n_g,
        compiler_params=_params(("parallel",)),
    )(dy, *outs, *lzs)


def _conv3(u, w_ref, b):
    t = u.shape[0]
    row = lax.broadcasted_iota(jnp.int32, u.shape, 0)
    prev = jnp.where(row == 0, 0.0, pltpu.roll(u, 1, 0))
    nxt = jnp.where(row == t - 1, 0.0, pltpu.roll(u, t - 1, 0))
    out = w_ref[0:1, :] * prev + w_ref[1:2, :] * u + w_ref[2:3, :] * nxt
    return out if b is None else out + b


def _conv3_t(d, w_ref):
    t = d.shape[0]
    row = lax.broadcasted_iota(jnp.int32, d.shape, 0)
    prev = jnp.where(row == 0, 0.0, pltpu.roll(d, 1, 0))
    nxt = jnp.where(row == t - 1, 0.0, pltpu.roll(d, t - 1, 0))
    return w_ref[0:1, :] * nxt + w_ref[1:2, :] * d + w_ref[2:3, :] * prev


def conv_act_fwd(name, u2, cw2, cb2):
    _, t, dff = u2.shape
    tn = LANES

    def body(u_ref, w_ref, b_ref, o_ref):
        cg = _conv3(u_ref[0], w_ref.at[0], b_ref[0])
        cv = _conv3(u_ref[1], w_ref.at[1], b_ref[1])
        o_ref[...] = (cg * jax.nn.sigmoid(cg) * cv).astype(BF16)

    return pl.pallas_call(
        body, name=name, grid=(dff // tn,),
        in_specs=[pl.BlockSpec((2, t, tn), lambda j: (0, 0, j)), pl.BlockSpec((2, 3, tn), lambda j: (0, 0, j)),
                  pl.BlockSpec((2, 1, tn), lambda j: (0, 0, j))],
        out_specs=pl.BlockSpec((t, tn), lambda j: (0, j)), out_shape=jax.ShapeDtypeStruct((t, dff), BF16),
        compiler_params=_params(("parallel",)),
    )(u2, cw2, cb2)


def conv_act_bwd(name, u2, cw2, cb2, dact):
    _, t, dff = u2.shape
    tn = LANES

    def body(u_ref, w_ref, b_ref, d_ref, du_ref, dw_ref):
        d = d_ref[...]
        ug, uv = u_ref[0], u_ref[1]
        cg = _conv3(ug, w_ref.at[0], b_ref[0])
        cv = _conv3(uv, w_ref.at[1], b_ref[1])
        sg = jax.nn.sigmoid(cg)
        dcv = d * (cg * sg)
        dcg = d * cv * (sg * (1.0 + cg * (1.0 - sg)))
        du_ref[0] = _conv3_t(dcg, w_ref.at[0]).astype(BF16)
        du_ref[1] = _conv3_t(dcv, w_ref.at[1]).astype(BF16)
        row = lax.broadcasted_iota(jnp.int32, ug.shape, 0)
        for half, (dc, u) in enumerate(((dcg, ug), (dcv, uv))):
            prev = jnp.where(row == 0, 0.0, pltpu.roll(u, 1, 0))
            nxt = jnp.where(row == t - 1, 0.0, pltpu.roll(u, t - 1, 0))
            for tap, x in enumerate((prev, u, nxt)):
                dw_ref[half, tap:tap + 1, :] = jnp.sum(dc * x, axis=0, keepdims=True)
            dw_ref[half, 3:4, :] = jnp.sum(dc, axis=0, keepdims=True)
            dw_ref[half, 4:8, :] = jnp.zeros((4, tn), F32)

    return pl.pallas_call(
        body, name=name, grid=(dff // tn,),
        in_specs=[pl.BlockSpec((2, t, tn), lambda j: (0, 0, j)), pl.BlockSpec((2, 3, tn), lambda j: (0, 0, j)),
                  pl.BlockSpec((2, 1, tn), lambda j: (0, 0, j)), pl.BlockSpec((t, tn), lambda j: (0, j))],
        out_specs=[pl.BlockSpec((2, t, tn), lambda j: (0, 0, j)), pl.BlockSpec((2, 8, tn), lambda j: (0, 0, j))],
        out_shape=[jax.ShapeDtypeStruct((2, t, dff), BF16), jax.ShapeDtypeStruct((2, 8, dff), F32)],
        compiler_params=_params(("parallel",)),
    )(u2, cw2, cb2, dact)


GATHER_ID, SIBLING_ID, CHIPS_ID = 0, 1, 2


def _place():
    x, y, c = lax.axis_index("x"), lax.axis_index("y"), lax.axis_index("c")
    chips = [(1 - x, y), (x, 1 - y), (1 - x, 1 - y)]
    return x, y, c, chips


def _handshake(peers):
    barrier = pltpu.get_barrier_semaphore()
    for peer in peers:
        pl.semaphore_signal(barrier, inc=1, device_id=peer, device_id_type=MESH)
    pl.semaphore_wait(barrier, len(peers))


def _sequencer(name, body, out_type, scratch_types, collective_id):
    return pl.kernel(body, out_type=out_type, mesh=plsc.ScalarSubcoreMesh(axis_name="seq", num_cores=1),
                     scratch_types=scratch_types, name=name,
                     compiler_params=pltpu.CompilerParams(collective_id=collective_id))


def _gather_body(n):
    def body(*refs):
        src, out = refs[:n], refs[n:2 * n]
        send, recv, loc = refs[2 * n:]
        x, y, c, chips = _place()
        sibling = (x, y, 1 - c)
        _handshake([sibling] + [(*chip, c) for chip in chips])

        def slot(a, px, py, pc):
            return out[a].at[4 * px + 2 * py + pc]

        def copy(a, k, block, to, from_src=False):
            return pltpu.make_async_remote_copy(
                src_ref=src[a] if from_src else slot(a, *block), dst_ref=slot(a, *block),
                send_sem=send.at[a, k], recv_sem=recv.at[a, k], device_id=to, device_id_type=MESH)

        mine = [pltpu.make_async_copy(src[a], slot(a, x, y, c), loc.at[a]) for a in range(n)]
        for cp in mine:
            cp.start()
        first = []
        for a in range(n):
            first.append(copy(a, 0, (x, y, c), sibling, True))
            first += [copy(a, 1 + j, (x, y, c), (*chip, c), True) for j, chip in enumerate(chips)]
        for cp in first:
            cp.start()
        passed = []
        for j, chip in enumerate(chips):
            for a in range(n):
                copy(a, 1 + j, (*chip, c), (x, y, c)).wait_recv()
                cp = copy(a, 4 + j, (*chip, c), sibling)
                cp.start()
                passed.append(cp)
        for a in range(n):
            copy(a, 0, sibling, (x, y, c)).wait_recv()
            for j, chip in enumerate(chips):
                copy(a, 4 + j, (*chip, 1 - c), (x, y, c)).wait_recv()
        for cp in first + passed:
            cp.wait_send()
        for cp in mine:
            cp.wait()

    return body


def gather_layer(name, shards):
    n = len(shards)
    out_type = [jax.ShapeDtypeStruct((N_DEV,) + s.shape, s.dtype) for s in shards]
    scratch = [pltpu.SemaphoreType.DMA((n, 7)), pltpu.SemaphoreType.DMA((n, 7)), pltpu.SemaphoreType.DMA((n,))]
    return _sequencer(name, _gather_body(n), out_type, scratch, GATHER_ID)(*shards)


def _to_sibling_body(n):
    def body(*refs):
        src, got = refs[:n], refs[n:2 * n]
        send, recv = refs[2 * n:]
        x, y, c, _ = _place()
        sibling = (x, y, 1 - c)
        _handshake([sibling])
        remote = []
        for a in range(n):
            for q in range(4):
                remote.append(pltpu.make_async_remote_copy(
                    src_ref=src[a].at[2 * q + 1 - c], dst_ref=got[a].at[q], send_sem=send.at[a, q],
                    recv_sem=recv.at[a, q], device_id=sibling, device_id_type=MESH))
        for cp in remote:
            cp.start()
        for cp in remote:
            cp.wait()

    return body


def grads_to_sibling(name, grads):
    n = len(grads)
    out_type = [jax.ShapeDtypeStruct((4,) + g.shape[1:], g.dtype) for g in grads]
    scratch = [pltpu.SemaphoreType.DMA((n, 4)), pltpu.SemaphoreType.DMA((n, 4))]
    return _sequencer(name, _to_sibling_body(n), out_type, scratch, SIBLING_ID)(*grads)


def _to_chips_body(n):
    def body(*refs):
        src, got = refs[:n], refs[n:2 * n]
        send, recv = refs[2 * n:]
        x, y, c, chips = _place()
        _handshake([(*chip, c) for chip in chips])
        remote = []
        for a in range(n):
            for j, (px, py) in enumerate(chips):
                remote.append(pltpu.make_async_remote_copy(
                    src_ref=src[a].at[2 * px + py], dst_ref=got[a].at[j], send_sem=send.at[a, j],
                    recv_sem=recv.at[a, j], device_id=(px, py, c), device_id_type=MESH))
        for cp in remote:
            cp.start()
        for cp in remote:
            cp.wait()

    return body


def grads_to_chips(name, parts):
    n = len(parts)
    out_type = [jax.ShapeDtypeStruct((3,) + p.shape[1:], p.dtype) for p in parts]
    scratch = [pltpu.SemaphoreType.DMA((n, 3)), pltpu.SemaphoreType.DMA((n, 3))]
    return _sequencer(name, _to_chips_body(n), out_type, scratch, CHIPS_ID)(*parts)


def all_reduce_small(name, vec):
    rows, m = vec.shape

    def body(x_ref, o_ref, buf, send, recv):
        x, y, c, chips = _place()
        sibling = (x, y, 1 - c)

        def blk(px, py, pc):
            return buf.at[pl.ds(pl.multiple_of((4 * px + 2 * py + pc) * rows, rows), rows), :]

        def copy(k, block, to):
            return pltpu.make_async_remote_copy(src_ref=blk(*block), dst_ref=blk(*block), send_sem=send.at[k],
                                                recv_sem=recv.at[k], device_id=to, device_id_type=MESH)

        blk(x, y, c)[...] = x_ref[...]
        first = [copy(0, (x, y, c), sibling)] + [copy(1 + j, (x, y, c), (*chip, c)) for j, chip in enumerate(chips)]
        for cp in first:
            cp.start()
        passed = [copy(4 + j, (*chip, c), sibling) for j, chip in enumerate(chips)]
        for j, chip in enumerate(chips):
            copy(1 + j, (*chip, c), (x, y, c)).wait_recv()
            passed[j].start()
        copy(0, sibling, (x, y, c)).wait_recv()
        for j, chip in enumerate(chips):
            copy(4 + j, (*chip, 1 - c), (x, y, c)).wait_recv()
        for cp in first + passed:
            cp.wait_send()
        tot = buf[0:rows, :]
        for dev in range(1, N_DEV):
            tot = tot + buf[dev * rows:(dev + 1) * rows, :]
        o_ref[...] = tot

    return pl.pallas_call(
        body, name=name, in_specs=[pl.BlockSpec(memory_space=pltpu.VMEM)],
        out_specs=pl.BlockSpec(memory_space=pltpu.VMEM), out_shape=jax.ShapeDtypeStruct((rows, m), F32),
        scratch_shapes=[pltpu.VMEM((N_DEV * rows, m), F32), pltpu.SemaphoreType.DMA((7,)),
                        pltpu.SemaphoreType.DMA((7,))],
        compiler_params=pltpu.CompilerParams(vmem_limit_bytes=VMEM_LIMIT),
    )(vec)


def _ew_tiles(rows, cols, max_elems=1 << 18):
    tr = rows
    for cand in (1024, 512, 256, 128, 64, 32, 16):
        if rows % cand == 0 and cand * cols <= max_elems:
            tr = cand
            break
    return tr


def chip_sum(name, full, got, core):
    _, kdim, ncol = full.shape
    tr = _ew_tiles(kdim, ncol, max_elems=1 << 20)
    blk = (None, tr, ncol)
    by_chip = pl.BlockSpec(blk, lambda q, i, c: (q, i, 0))

    def body(c_ref, a_ref, b_ref, o_ref):
        o_ref[...] = (a_ref[...].astype(F32) + b_ref[...].astype(F32)).astype(BF16)

    return pl.pallas_call(
        body, name=name,
        grid_spec=pltpu.PrefetchScalarGridSpec(
            num_scalar_prefetch=1, grid=(4, kdim // tr),
            in_specs=[pl.BlockSpec(blk, lambda q, i, c: (2 * q + c[0], i, 0)), by_chip], out_specs=by_chip),
        out_shape=jax.ShapeDtypeStruct((4, kdim, ncol), BF16),
        compiler_params=_params(("parallel", "parallel")),
    )(core, full, got)


def _adamw_math(w, g, m, v):
    m = ADAM_B1 * m + (1.0 - ADAM_B1) * g
    v = ADAM_B2 * v + (1.0 - ADAM_B2) * (g * g)
    m_hat = m / (1.0 - ADAM_B1 ** ADAM_STEP)
    v_hat = v / (1.0 - ADAM_B2 ** ADAM_STEP)
    delta = -ADAM_LR * (m_hat / (jnp.sqrt(v_hat) + ADAM_EPS) + ADAM_WD * w)
    return delta, m, v


def adamw_layer(name, sums, got, w, m, v, layer, chip):
    _, kdim, ncol = sums.shape
    tr = _ew_tiles(kdim, ncol)
    out = pl.BlockSpec((tr, ncol), lambda i, q: (i, 0))
    mine = pl.BlockSpec((None, tr, ncol), lambda i, q: (q[0], i, 0))
    others = pl.BlockSpec((3, tr, ncol), lambda i, q: (0, i, 0))
    param = pl.BlockSpec((None, tr, ncol), lambda i, q: (layer, i, 0))

    def body(q_ref, o_ref, g_ref, w_ref, m_ref, v_ref, go_ref, d_ref, mo_ref, vo_ref):
        g = o_ref[...].astype(F32)
        for j in range(3):
            g = g + g_ref[j].astype(F32)
        d, mn, vn = _adamw_math(w_ref[...], g, m_ref[...], v_ref[...])
        go_ref[...] = g
        d_ref[...] = d
        mo_ref[...] = mn
        vo_ref[...] = vn

    return pl.pallas_call(
        body, name=name,
        grid_spec=pltpu.PrefetchScalarGridSpec(
            num_scalar_prefetch=1, grid=(kdim // tr,),
            in_specs=[mine, others, param, param, param], out_specs=[out] * 4),
        out_shape=[jax.ShapeDtypeStruct((kdim, ncol), F32)] * 4,
        compiler_params=_params(("parallel",)),
    )(chip, sums, got, w, m, v)


def adamw_small(name, g, w, m, v):
    def body(g_ref, w_ref, m_ref, v_ref, d_ref, mo_ref, vo_ref):
        d, mn, vn = _adamw_math(w_ref[...], g_ref[...], m_ref[...], v_ref[...])
        d_ref[...] = d
        mo_ref[...] = mn
        vo_ref[...] = vn

    vm = pl.BlockSpec(memory_space=pltpu.VMEM)
    return pl.pallas_call(
        body, name=name, in_specs=[vm] * 4, out_specs=[vm] * 3,
        out_shape=[jax.ShapeDtypeStruct(g.shape, F32)] * 3,
        compiler_params=pltpu.CompilerParams(vmem_limit_bytes=VMEM_LIMIT),
    )(g, w, m, v)


def _pack(parts, width):
    flat = jnp.concatenate([p.reshape(-1).astype(F32) for p in parts])
    pad = (-flat.shape[0]) % width
    return jnp.pad(flat, (0, pad)).reshape(-1, width) if pad else flat.reshape(-1, width)


def _unpack(packed, shapes):
    flat = packed.reshape(-1)
    out, off = [], 0
    for s in shapes:
        size = math.prod(s)
        out.append(flat[off:off + size].reshape(s))
        off += size
    return out


def _local_step(h, target, layers, params, on_grads=None):
    a_q_gain, a_k_gain, rel_bias, mix_norm, ffn_norm, conv_b, final_norm = params
    t, d = h.shape
    depth = len(layers)
    n_groups = len(B_GROUPS)
    hg = B_HEADS_PER_GROUP
    n_kv = A_KV_HEADS
    w_a, w_b, w_u = layers[0][0].shape[2], layers[1][0].shape[2], layers[0][2].shape[2]
    n_q = w_a * N_DEV // HEAD_DIM - 2 * n_kv
    dff = layers[0][3].shape[0]
    n_a = (depth + 1) // 2
    cb_full = conv_b.reshape(depth, 2, 1, dff)

    cos, sin = rope_tables(t)
    strides = [band_stride(t, win, dil) for win, dil in B_GROUPS]
    tables = [band_tables(rel_bias[:, g * hg:(g + 1) * hg], win, dil, strides[g], _tile(t // strides[g], B_BQ))
              for g, (win, dil) in enumerate(B_GROUPS)]

    saved = []
    for i in range(depth):
        j = i // 2
        w_qkv, w_o, w_up_i, w_down_i, cw = layers[i]
        s = {"h_in": h}
        hn = rms_fwd("mix_norm_fwd", h, mix_norm[i])
        s["hn"] = hn
        if i % 2 == 0:
            qkv = mm_col_fwd("a_qkv_fwd", hn, w_qkv, F32)
            qkv_r = qk_prep_fwd("a_qk_prep_fwd", qkv, a_q_gain[j], a_k_gain[j], cos, sin, n_q, n_kv)
            o, lse = mixer_a_fwd(qkv_r, n_q, n_kv)
            s.update(qkv=qkv, qkv_r=qkv_r, o=o, lse=lse)
            h = mm_row_fwd("a_out_fwd", o, w_o, h)
        else:
            qkv = mm_col_fwd("b_qkv_fwd", hn, w_qkv, BF16)
            outs, lzs = [], []
            for g, (win, dil) in enumerate(B_GROUPS):
                o_g, lz_g = mixer_b_group_fwd(qkv, tables[g][0], strides[g], g, n_groups, dil)
                outs.append(o_g)
                lzs.append(lz_g)
            y = combine_fwd("b_combine_fwd", outs, lzs)
            s.update(qkv=qkv, outs=outs, lzs=lzs, y=y)
            h = mm_row_fwd("b_out_fwd", y, w_o, h)
        s["h_mid"] = h
        hn2 = rms_fwd("ffn_norm_fwd", h, ffn_norm[i])
        u2 = mm_col_fwd("ffn_up_fwd", hn2, w_up_i, F32, split=2)
        act = conv_act_fwd("ffn_conv_act_fwd", u2, cw, cb_full[i])
        s.update(hn2=hn2, u2=u2, act=act)
        h = mm_row_fwd("ffn_down_fwd", act, w_down_i, h)
        saved.append(s)

    dh, d_final, loss_part = loss_head("loss_head", h, final_norm, target)

    d_mix, d_ffn, d_cw, d_cb = [None] * depth, [None] * depth, [None] * depth, [None] * depth
    d_qg, d_kg = [None] * n_a, [None] * n_a
    d_rel = jnp.zeros((n_groups * hg, LANES), F32)
    layer_grads = [{} for _ in range(depth)]
    pending = []

    def settle():
        done = []
        while pending:
            i_p, part_p, finish = pending.pop()
            layer_grads[i_p][part_p] = finish()
            done += [upd[0] for upd in layer_grads[i_p][part_p]]
        return done

    early = []

    def register(i_p, part_p, grads):
        if on_grads is None:
            layer_grads[i_p][part_p] = grads
        else:
            first, finish = on_grads(i_p, part_p, grads)
            early.extend(first)
            pending.append((i_p, part_p, finish))

    def take_early():
        first = tuple(early)
        early.clear()
        return first

    for i in reversed(range(depth)):
        j = i // 2
        w_qkv, w_o, w_up_i, w_down_i, cw = layers[i]
        s = saved[i]
        dact = mm_row_dx("ffn_down_dx", dh, w_down_i)
        g_down = mm_row_dw("ffn_down_dw", s["act"], dh, take_early())
        du2, dcw = conv_act_bwd("ffn_conv_act_bwd", s["u2"], cw, cb_full[i], dact)
        d_cw[i] = dcw[:, 0:3, :].transpose(1, 0, 2).reshape(3, 2 * dff)
        d_cb[i] = dcw[:, 3, :].reshape(2 * dff)
        g_up = mm_col_dw("ffn_up_dw", s["hn2"], du2, w_u, split=2)
        dhn2 = mm_col_dx("ffn_up_dx", du2, w_up_i, split=2)
        dh, d_ffn[i] = rms_bwd("ffn_norm_bwd", s["h_mid"], ffn_norm[i], dhn2, dh, settle())
        register(i, "ffn", [g_up, g_down.reshape(N_DEV, -1, d)])
        if i % 2 == 0:
            do = mm_row_dx("a_out_dx", dh, w_o)
            g_o = mm_row_dw("a_out_dw", s["o"], dh, take_early())
            dlt, do_b = row_delta("a_delta", do, s["o"], n_q)
            dq, dk, dv = mixer_a_bwd(s["qkv_r"], do_b, s["lse"], dlt, n_q, n_kv)
            dqkv, dgain = qk_prep_bwd("a_qk_prep_bwd", s["qkv"], dq, dk, dv, a_q_gain[j], a_k_gain[j], cos, sin,
                                      n_q, n_kv)
            d_qg[j], d_kg[j] = dgain[0], dgain[1]
            g_qkv = mm_col_dw("a_qkv_dw", s["hn"], dqkv, w_a)
            dhn = mm_col_dx("a_qkv_dx", dqkv, w_qkv)
        else:
            dy = mm_row_dx("b_out_dx", dh, w_o)
            g_o = mm_row_dw("b_out_dw", s["y"], dh, take_early())
            res = combine_bwd("b_combine_bwd", dy, s["outs"], s["lzs"])
            dos, dlts = res[:n_groups], res[n_groups:]
            pieces, rel_rows = [], []
            for g, (win, dil) in enumerate(B_GROUPS):
                dq, dk, dv, dbias = mixer_b_group_bwd(s["qkv"], tables[g][0], dos[g], s["lzs"][g], dlts[g],
                                                      strides[g], g, n_groups, dil)
                pieces += [dq, dk, dv]
                rel_rows.append(bias_bucket_sums(f"b_bias_sums_d{dil}", dbias, tables[g][1]))
            d_rel = d_rel + jnp.concatenate(rel_rows, axis=0)
            dqkv = jnp.concatenate(pieces, axis=1)
            g_qkv = mm_col_dw("b_qkv_dw", s["hn"], dqkv, w_b)
            dhn = mm_col_dx("b_qkv_dx", dqkv, w_qkv)
        dh, d_mix[i] = rms_bwd("mix_norm_bwd", s["h_in"], mix_norm[i], dhn, dh, settle())
        register(i, "mix", [g_qkv, g_o.reshape(N_DEV, -1, d)])
    last = pending.pop()[2] if pending else None

    d_rel_bias = d_rel[:, :REL_BUCKETS].T
    small_g = [jnp.stack(d_qg), jnp.stack(d_kg), d_rel_bias, jnp.concatenate(d_mix, 0), jnp.concatenate(d_ffn, 0),
               jnp.stack(d_cb), d_final.reshape(-1), jnp.stack(d_cw), loss_part]
    return dh, layer_grads, small_g, last


def kernel(x, a_w_qkv, a_w_o, a_q_gain, a_k_gain, b_w_qkv, b_w_o, rel_bias, mix_norm, ffn_norm, w_up, conv_w, conv_b, w_down, final_norm, loss_target, m_a_w_qkv, m_a_w_o, m_a_q_gain, m_a_k_gain, m_b_w_qkv, m_b_w_o, m_rel_bias, m_mix_norm, m_ffn_norm, m_w_up, m_conv_w, m_conv_b, m_w_down, m_final_norm, v_a_w_qkv, v_a_w_o, v_a_q_gain, v_a_k_gain, v_b_w_qkv, v_b_w_o, v_rel_bias, v_mix_norm, v_ffn_norm, v_w_up, v_conv_w, v_conv_b, v_w_down, v_final_norm):
    d = x.shape[2]
    depth = mix_norm.shape[0]
    dff = w_down.shape[1] * N_DEV
    w_u = w_up.shape[2]
    mixers = [(a_w_qkv, a_w_o, m_a_w_qkv, m_a_w_o, v_a_w_qkv, v_a_w_o),
              (b_w_qkv, b_w_o, m_b_w_qkv, m_b_w_o, v_b_w_qkv, v_b_w_o)]

    layers = []
    for i in range(depth):
        w_qkv, w_o = mixers[i % 2][0][i // 2], mixers[i % 2][1][i // 2]
        shards = [w_qkv.astype(BF16), w_o.astype(BF16), w_up[i].astype(BF16), w_down[i].astype(BF16), conv_w[i]]
        if i == 0:
            (g_qkv,) = gather_layer("gather_l0_qkv", shards[:1])
            g_o, g_up, g_down, g_cw = gather_layer("gather_l0", shards[1:])
        else:
            g_qkv, g_o, g_up, g_down, g_cw = gather_layer(f"gather_l{i}", shards)
        cw = g_cw.transpose(1, 0, 2).reshape(3, 2, dff).transpose(1, 0, 2)
        layers.append((g_qkv, g_o.reshape(-1, d), g_up, g_down.reshape(dff, d), cw))

    core = lax.axis_index("c").astype(jnp.int32).reshape(1)
    chip = (2 * lax.axis_index("x") + lax.axis_index("y")).astype(jnp.int32).reshape(1)

    def reduce_and_update(i, part, grads):
        w_qkv, w_o, m_qkv, m_o, v_qkv, v_o = mixers[i % 2]
        state = {"mix": [(w_qkv, m_qkv, v_qkv, i // 2), (w_o, m_o, v_o, i // 2)],
                 "ffn": [(w_up, m_w_up, v_w_up, i), (w_down, m_w_down, v_w_down, i)]}[part]
        got1 = grads_to_sibling(f"to_sibling_l{i}_{part}", grads)
        sums = [chip_sum(f"chip_sum_l{i}_{part}{a}", grads[a], got1[a], core) for a in range(2)]
        got2 = grads_to_chips(f"to_chips_l{i}_{part}", sums)

        def finish():
            return [adamw_layer(f"adamw_l{i}_{part}{a}", sums[a], got2[a], *state[a], chip) for a in range(2)]

        return sums, finish

    dh, updates, small_g, last = _local_step(x[0], loss_target[0], layers,
                                             (a_q_gain, a_k_gain, rel_bias, mix_norm, ffn_norm, conv_b, final_norm),
                                             reduce_and_update)
    grad_x = dh[None]

    width = 2048
    packed = _pack(small_g, N_DEV * width).reshape(-1, N_DEV, width)
    n_rows = packed.shape[0]
    packed = packed.transpose(1, 0, 2).reshape(N_DEV, n_rows * width)
    red = all_reduce_small("small_all_reduce", packed)
    updates[0]["mix"] = last()
    big_out = {}
    for nm, part, a in (("qkv", "mix", 0), ("o", "mix", 1)):
        for par, prefix in enumerate(("a_w_", "b_w_")):
            big_out[prefix + nm] = [jnp.stack([updates[i][part][a][k] for i in range(par, depth, 2)]) for k in range(4)]
    for nm, a in (("w_up", 0), ("w_down", 1)):
        big_out[nm] = [jnp.stack([updates[i]["ffn"][a][k] for i in range(depth)]) for k in range(4)]
    red = red.reshape(N_DEV, n_rows, width).transpose(1, 0, 2)
    (g_qg, g_kg, g_rel, g_mix, g_ffn, g_cb, g_fin, g_cw_all, loss) = _unpack(red, [p.shape for p in small_g])
    idx = 4 * lax.axis_index("x") + 2 * lax.axis_index("y") + lax.axis_index("c")
    g_cw_mine = lax.dynamic_slice_in_dim(g_cw_all, idx * w_u, w_u, axis=2)

    small_w = [a_q_gain, a_k_gain, rel_bias, mix_norm, ffn_norm, conv_b, final_norm, conv_w]
    small_m = [m_a_q_gain, m_a_k_gain, m_rel_bias, m_mix_norm, m_ffn_norm, m_conv_b, m_final_norm, m_conv_w]
    small_v = [v_a_q_gain, v_a_k_gain, v_rel_bias, v_mix_norm, v_ffn_norm, v_conv_b, v_final_norm, v_conv_w]
    small_grads = [g_qg, g_kg, g_rel, g_mix, g_ffn, g_cb, g_fin, g_cw_mine]
    shapes = [w.shape for w in small_w]
    pad_rows = (-_pack(small_w, width).shape[0]) % 8

    def pk8(parts):
        p = _pack(parts, width)
        return jnp.pad(p, ((0, pad_rows), (0, 0))) if pad_rows else p

    sd, sm, sv = adamw_small("adamw_small", pk8(small_grads), pk8(small_w), pk8(small_m), pk8(small_v))
    sd, sm, sv = _unpack(sd, shapes), _unpack(sm, shapes), _unpack(sv, shapes)

    names = ["a_w_qkv", "a_w_o", "a_q_gain", "a_k_gain", "b_w_qkv", "b_w_o", "rel_bias", "mix_norm", "ffn_norm",
             "w_up", "conv_w", "conv_b", "w_down", "final_norm"]
    small_names = ["a_q_gain", "a_k_gain", "rel_bias", "mix_norm", "ffn_norm", "conv_b", "final_norm", "conv_w"]
    grads, deltas, new_m, new_v = {}, {}, {}, {}
    for nm, outs in big_out.items():
        grads[nm], deltas[nm], new_m[nm], new_v[nm] = outs
    for a, nm in enumerate(small_names):
        grads[nm] = small_grads[a].reshape(shapes[a])
        deltas[nm], new_m[nm], new_v[nm] = sd[a], sm[a], sv[a]
    return (loss.reshape(()), grad_x, *[grads[n] for n in names], *[deltas[n] for n in names],
            *[new_m[n] for n in names], *[new_v[n] for n in names])
```

```python
import functools
import math

import jax
import jax.numpy as jnp
from jax import lax
from jax.experimental import pallas as pl
from jax.experimental.pallas import tpu as pltpu
from jax.experimental.pallas import tpu_sc as plsc

F32 = jnp.float32
BF16 = jnp.bfloat16
MESH = pl.DeviceIdType.MESH

N_DEV = 8
LANES = 128
HEAD_DIM = 128
VMEM_LIMIT = 56 * 1024 * 1024
GRID_W = 64
ROPE_THETA = 10000.0
A_KV_HEADS = 4
B_GROUPS = ((128, 1), (512, 4), (2048, 16))
B_HEADS_PER_GROUP = 8
REL_BUCKETS = 32
REL_MAX_DISTANCE = 1024
EPS = 1e-6
NEG_INF = -1e30
ADAM_LR = 0.001
ADAM_B1 = 0.9
ADAM_B2 = 0.999
ADAM_EPS = 1e-08
ADAM_WD = 0.01
ADAM_STEP = 10

ROW_TILE = 256
MM_TM = 1024
MM_TK = 2048
A_BQ = 1024
A_BK = 1024
B_BQ = 256
ATTN_ROWS = 16
ATTN_SCALE = HEAD_DIM ** -0.5

NN = (((1,), (0,)), ((), ()))
NT = (((1,), (1,)), ((), ()))
TN = (((0,), (0,)), ((), ()))


def _tile(n, pref):
    return pref if n % pref == 0 else n


def _div_tile(n, pref):
    for cand in range(pref - pref % LANES, 0, -LANES):
        if n % cand == 0:
            return cand
    return n


def _params(sem):
    return pltpu.CompilerParams(dimension_semantics=sem, vmem_limit_bytes=VMEM_LIMIT)


def _dot(a, b, dims):
    return lax.dot_general(a, b, dims, preferred_element_type=F32)


def _mm(name, a, b, *, grid, a_blk, a_map, b_blk, b_map, o_blk, o_map, out_shape, out_dtype, dims,
        res=None, after=()):
    nk = grid[2]
    acc_shape = tuple(d for d in o_blk if d is not None)

    def body(*refs):
        a_ref, b_ref = refs[:2]
        r_ref = None if res is None else refs[2]
        o_ref, acc = refs[-2:]
        k = pl.program_id(2)

        @pl.when(k == 0)
        def _():
            acc[...] = jnp.zeros_like(acc)

        acc[...] += _dot(a_ref[...].astype(BF16), b_ref[...].astype(BF16), dims)

        @pl.when(k == nk - 1)
        def _():
            r = acc[...]
            if r_ref is not None:
                r = r + r_ref[...]
            o_ref[...] = r.astype(out_dtype)

    in_specs = [pl.BlockSpec(a_blk, a_map), pl.BlockSpec(b_blk, b_map)]
    args = [a, b]
    if res is not None:
        in_specs.append(pl.BlockSpec(o_blk, o_map))
        args.append(res)
    in_specs += [pl.BlockSpec(memory_space=pl.ANY)] * len(after)
    args += list(after)
    return pl.pallas_call(
        body, name=name, grid=grid, in_specs=in_specs, out_specs=pl.BlockSpec(o_blk, o_map),
        out_shape=jax.ShapeDtypeStruct(out_shape, out_dtype),
        scratch_shapes=[pltpu.VMEM(acc_shape, F32)],
        compiler_params=_params(("parallel", "parallel", "arbitrary")),
    )(*args)


def mm_col_fwd(name, a, wg, out_dtype, split=1):
    m, kdim = a.shape
    n_dev, _, w = wg.shape
    tm, tk = _tile(m, MM_TM), _div_tile(kdim, MM_TK)
    per = n_dev // split
    if split == 1:
        o_blk, o_map, o_shape = (tm, w), (lambda i, j, k: (i, j)), (m, n_dev * w)
    else:
        o_blk, o_map, o_shape = (None, tm, w), (lambda i, j, k: (j // per, i, j % per)), (split, m, per * w)
    return _mm(name, a, wg, grid=(m // tm, n_dev, kdim // tk),
               a_blk=(tm, tk), a_map=lambda i, j, k: (i, k),
               b_blk=(None, tk, w), b_map=lambda i, j, k: (j, k, 0),
               o_blk=o_blk, o_map=o_map, out_shape=o_shape, out_dtype=out_dtype, dims=NN)


def mm_col_dx(name, dy, wg, split=1):
    n_dev, kdim, w = wg.shape
    m = dy.shape[-2]
    tm, tk = _tile(m, MM_TM), _div_tile(kdim, MM_TK)
    per = n_dev // split
    if split == 1:
        a_blk, a_map = (tm, w), (lambda i, j, k: (i, k))
    else:
        a_blk, a_map = (None, tm, w), (lambda i, j, k: (k // per, i, k % per))
    return _mm(name, dy, wg, grid=(m // tm, kdim // tk, n_dev),
               a_blk=a_blk, a_map=a_map,
               b_blk=(None, tk, w), b_map=lambda i, j, k: (k, j, 0),
               o_blk=(tm, tk), o_map=lambda i, j, k: (i, j), out_shape=(m, kdim), out_dtype=F32, dims=NT)


def mm_col_dw(name, x, dy, w, split=1):
    m, kdim = x.shape
    tm, tk = _tile(m, MM_TM), _div_tile(kdim, MM_TK)
    per = N_DEV // split
    if split == 1:
        b_blk, b_map = (tm, w), (lambda i, j, k: (k, j))
    else:
        b_blk, b_map = (None, tm, w), (lambda i, j, k: (j // per, k, j % per))
    return _mm(name, x, dy, grid=(kdim // tk, N_DEV, m // tm),
               a_blk=(tm, tk), a_map=lambda i, j, k: (k, i),
               b_blk=b_blk, b_map=b_map,
               o_blk=(None, tk, w), o_map=lambda i, j, k: (j, i, 0),
               out_shape=(N_DEV, kdim, w), out_dtype=BF16, dims=TN)


def mm_row_fwd(name, a, wg, res):
    m, kdim = a.shape
    n = wg.shape[1]
    tm, tk, tn = _tile(m, MM_TM), _div_tile(kdim, MM_TK), _tile(n, 1024)
    return _mm(name, a, wg, grid=(m // tm, n // tn, kdim // tk),
               a_blk=(tm, tk), a_map=lambda i, j, k: (i, k),
               b_blk=(tk, tn), b_map=lambda i, j, k: (k, j),
               o_blk=(tm, tn), o_map=lambda i, j, k: (i, j), out_shape=(m, n), out_dtype=F32, dims=NN,
               res=res)


def mm_row_dx(name, dy, wg, after=()):
    m, n = dy.shape
    kdim = wg.shape[0]
    tm, tk, tn = _tile(m, MM_TM), _div_tile(kdim, MM_TK), _tile(n, 1024)
    return _mm(name, dy, wg, grid=(m // tm, kdim // tk, n // tn),
               a_blk=(tm, tn), a_map=lambda i, j, k: (i, k),
               b_blk=(tk, tn), b_map=lambda i, j, k: (j, k),
               o_blk=(tm, tk), o_map=lambda i, j, k: (i, j), out_shape=(m, kdim), out_dtype=F32, dims=NT,
               after=after)


def mm_row_dw(name, x, dy):
    m, kdim = x.shape
    n = dy.shape[1]
    tm, tk, tn = _tile(m, MM_TM), _div_tile(kdim, MM_TK), _tile(n, 1024)
    return _mm(name, x, dy, grid=(kdim // tk, n // tn, m // tm),
               a_blk=(tm, tk), a_map=lambda i, j, k: (k, i),
               b_blk=(tm, tn), b_map=lambda i, j, k: (k, j),
               o_blk=(tk, tn), o_map=lambda i, j, k: (i, j), out_shape=(kdim, n), out_dtype=BF16, dims=TN)


def _rows(d, tm):
    return pl.BlockSpec((tm, d), lambda i: (i, 0))


def _vec(d):
    return pl.BlockSpec((1, d), lambda i: (0, 0))


def rms_fwd(name, h, gain):
    t, d = h.shape
    tm = _tile(t, ROW_TILE)

    def body(h_ref, g_ref, o_ref):
        x = h_ref[...]
        rstd = lax.rsqrt(jnp.mean(x * x, axis=-1, keepdims=True) + EPS)
        o_ref[...] = (x * rstd * g_ref[...]).astype(BF16)

    return pl.pallas_call(
        body, name=name, grid=(t // tm,), in_specs=[_rows(d, tm), _vec(d)], out_specs=_rows(d, tm),
        out_shape=jax.ShapeDtypeStruct((t, d), BF16), compiler_params=_params(("parallel",)),
    )(h, gain.reshape(1, d))


def rms_bwd(name, h, gain, dy, dres, after=()):
    t, d = h.shape
    tm = _tile(t, ROW_TILE)

    def body(h_ref, g_ref, dy_ref, r_ref, *rest):
        dh_ref, dhb_ref, dg_ref = rest[-3:]

        @pl.when(pl.program_id(0) == 0)
        def _():
            dg_ref[...] = jnp.zeros_like(dg_ref)

        x = h_ref[...]
        rstd = lax.rsqrt(jnp.mean(x * x, axis=-1, keepdims=True) + EPS)
        xhat = x * rstd
        dyv = dy_ref[...]
        dxhat = dyv * g_ref[...]
        dh = r_ref[...] + rstd * (dxhat - xhat * jnp.mean(dxhat * xhat, axis=-1, keepdims=True))
        dh_ref[...] = dh
        dhb_ref[...] = dh.astype(BF16)
        dg_ref[...] += jnp.sum(dyv * xhat, axis=0, keepdims=True)

    return pl.pallas_call(
        body, name=name, grid=(t // tm,),
        in_specs=[_rows(d, tm), _vec(d), _rows(d, tm), _rows(d, tm)]
        + [pl.BlockSpec(memory_space=pl.ANY)] * len(after),
        out_specs=[_rows(d, tm), _rows(d, tm), _vec(d)],
        out_shape=[jax.ShapeDtypeStruct((t, d), F32), jax.ShapeDtypeStruct((t, d), BF16),
                   jax.ShapeDtypeStruct((1, d), F32)],
        compiler_params=_params(("arbitrary",)),
    )(h, gain.reshape(1, d), dy, dres, *after)


def loss_head(name, h, gain, target):
    t, d = h.shape
    tm = _tile(t, ROW_TILE)

    def body(h_ref, g_ref, t_ref, dh_ref, dhb_ref, dg_ref, loss_ref):
        @pl.when(pl.program_id(0) == 0)
        def _():
            dg_ref[...] = jnp.zeros_like(dg_ref)
            loss_ref[...] = jnp.zeros_like(loss_ref)

        x = h_ref[...]
        rstd = lax.rsqrt(jnp.mean(x * x, axis=-1, keepdims=True) + EPS)
        xhat = x * rstd
        err = xhat * g_ref[...] - t_ref[...]
        row = jnp.mean(err * err, axis=-1, keepdims=True)
        loss_ref[...] += 0.5 * jnp.sum(row, axis=0, keepdims=True)
        dyv = err * (1.0 / d)
        dxhat = dyv * g_ref[...]
        dh = rstd * (dxhat - xhat * jnp.mean(dxhat * xhat, axis=-1, keepdims=True))
        dh_ref[...] = dh
        dhb_ref[...] = dh.astype(BF16)
        dg_ref[...] += jnp.sum(dyv * xhat, axis=0, keepdims=True)

    return pl.pallas_call(
        body, name=name, grid=(t // tm,),
        in_specs=[_rows(d, tm), _vec(d), _rows(d, tm)],
        out_specs=[_rows(d, tm), _rows(d, tm), _vec(d), pl.BlockSpec((1, 1), lambda i: (0, 0))],
        out_shape=[jax.ShapeDtypeStruct((t, d), F32), jax.ShapeDtypeStruct((t, d), BF16),
                   jax.ShapeDtypeStruct((1, d), F32), jax.ShapeDtypeStruct((1, 1), F32)],
        compiler_params=_params(("arbitrary",)),
    )(h, gain.reshape(1, d), target)


def rope_tables(seq):
    pos = jnp.arange(seq, dtype=jnp.int32)
    row_ids = (pos // GRID_W).astype(F32)
    col_ids = (pos % GRID_W).astype(F32)
    quarter = HEAD_DIM // 4
    inv_freq = ROPE_THETA ** (-jnp.arange(quarter, dtype=F32) / quarter)
    ar = row_ids[:, None] * inv_freq[None, :]
    ac = col_ids[:, None] * inv_freq[None, :]
    cos = jnp.concatenate([jnp.cos(ar), jnp.cos(ar), jnp.cos(ac), jnp.cos(ac)], axis=-1)
    sin = jnp.concatenate([-jnp.sin(ar), jnp.sin(ar), -jnp.sin(ac), jnp.sin(ac)], axis=-1)
    return cos, sin


def _swap_quarters(x):
    lane = lax.broadcasted_iota(jnp.int32, x.shape, 1)
    q = HEAD_DIM // 4
    return jnp.where((lane % (2 * q)) < q, pltpu.roll(x, HEAD_DIM - q, 1), pltpu.roll(x, q, 1))


def qk_prep_fwd(name, qkv, q_gain, k_gain, cos, sin, n_q, n_kv):
    t, width = qkv.shape
    tm = _tile(t, ROW_TILE)

    def body(x_ref, qg_ref, kg_ref, c_ref, s_ref, o_ref):
        c, s = c_ref[...], s_ref[...]
        for hd in range(n_q + n_kv):
            sl = slice(hd * HEAD_DIM, (hd + 1) * HEAD_DIM)
            x = x_ref[:, sl]
            g = qg_ref[...] if hd < n_q else kg_ref[...]
            xn = x * lax.rsqrt(jnp.mean(x * x, axis=-1, keepdims=True) + EPS) * g
            o_ref[:, sl] = (xn * c + _swap_quarters(xn) * s).astype(BF16)
        vs = slice((n_q + n_kv) * HEAD_DIM, width)
        o_ref[:, vs] = x_ref[:, vs].astype(BF16)

    return pl.pallas_call(
        body, name=name, grid=(t // tm,),
        in_specs=[_rows(width, tm), _vec(HEAD_DIM), _vec(HEAD_DIM), _rows(HEAD_DIM, tm), _rows(HEAD_DIM, tm)],
        out_specs=_rows(width, tm), out_shape=jax.ShapeDtypeStruct((t, width), BF16),
        compiler_params=_params(("parallel",)),
    )(qkv, q_gain.reshape(1, HEAD_DIM), k_gain.reshape(1, HEAD_DIM), cos, sin)


def qk_prep_bwd(name, qkv, dq, dk, dv, q_gain, k_gain, cos, sin, n_q, n_kv):
    t, width = qkv.shape
    tm = _tile(t, ROW_TILE)

    def body(x_ref, dq_ref, dk_ref, dv_ref, qg_ref, kg_ref, c_ref, s_ref, o_ref, dg_ref):
        @pl.when(pl.program_id(0) == 0)
        def _():
            dg_ref[...] = jnp.zeros_like(dg_ref)

        c, s = c_ref[...], s_ref[...]
        dgq = jnp.zeros((1, HEAD_DIM), F32)
        dgk = jnp.zeros((1, HEAD_DIM), F32)
        for hd in range(n_q + n_kv):
            sl = slice(hd * HEAD_DIM, (hd + 1) * HEAD_DIM)
            x = x_ref[:, sl]
            if hd < n_q:
                g, dout = qg_ref[...], dq_ref[:, sl]
            else:
                ks = slice((hd - n_q) * HEAD_DIM, (hd - n_q + 1) * HEAD_DIM)
                g, dout = kg_ref[...], dk_ref[:, ks]
            rstd = lax.rsqrt(jnp.mean(x * x, axis=-1, keepdims=True) + EPS)
            xhat = x * rstd
            dxn = dout * c + _swap_quarters(dout * s)
            part = jnp.sum(dxn * xhat, axis=0, keepdims=True)
            if hd < n_q:
                dgq = dgq + part
            else:
                dgk = dgk + part
            dxhat = dxn * g
            o_ref[:, sl] = (rstd * (dxhat - xhat * jnp.mean(dxhat * xhat, axis=-1, keepdims=True))).astype(BF16)
        o_ref[:, slice((n_q + n_kv) * HEAD_DIM, width)] = dv_ref[...].astype(BF16)
        dg_ref[0:1, :] += dgq
        dg_ref[1:2, :] += dgk

    kvw = n_kv * HEAD_DIM
    return pl.pallas_call(
        body, name=name, grid=(t // tm,),
        in_specs=[_rows(width, tm), _rows(n_q * HEAD_DIM, tm), _rows(kvw, tm), _rows(kvw, tm),
                  _vec(HEAD_DIM), _vec(HEAD_DIM), _rows(HEAD_DIM, tm), _rows(HEAD_DIM, tm)],
        out_specs=[_rows(width, tm), pl.BlockSpec((2, HEAD_DIM), lambda i: (0, 0))],
        out_shape=[jax.ShapeDtypeStruct((t, width), BF16), jax.ShapeDtypeStruct((2, HEAD_DIM), F32)],
        compiler_params=_params(("arbitrary",)),
    )(qkv, dq, dk, dv, q_gain.reshape(1, HEAD_DIM), k_gain.reshape(1, HEAD_DIM), cos, sin)


def _lanes(x, width):
    return jnp.tile(x, (1, width // LANES))


def _hs(hd):
    return slice(hd * HEAD_DIM, (hd + 1) * HEAD_DIM)


def attn_fwd(name, q, k, v, bias, *, grid, q_spec, k_spec, v_spec, b_spec, o_spec, valid, nh, shared_kv,
             bq, bk, o_shape, o_dtype):
    ns = grid[2]

    def body(*refs):
        if bias is None:
            q_ref, k_ref, v_ref, o_ref, lse_ref, m_s, l_s, acc_s = refs
            b_ref = None
        else:
            q_ref, k_ref, v_ref, b_ref, o_ref, lse_ref, m_s, l_s, acc_s = refs
        step = pl.program_id(2)

        @pl.when(step == 0)
        def _():
            m_s[...] = jnp.full_like(m_s, -jnp.inf)
            l_s[...] = jnp.zeros_like(l_s)
            acc_s[...] = jnp.zeros_like(acc_s)

        @pl.when(valid(pl.program_id(1), step))
        def _():
            for hd in range(nh):
                kh = _hs(0 if shared_kv else hd)
                s = _dot(q_ref[:, _hs(hd)], k_ref[:, kh], NT)
                p_rows, a_rows = [], []
                for r0 in range(0, bq, ATTN_ROWS):
                    rows = slice(r0, r0 + ATTN_ROWS)
                    z = s[rows] * ATTN_SCALE
                    if b_ref is not None:
                        z = z + b_ref[hd, rows, :]
                    m_prev = m_s[hd, rows, :]
                    m_new = jnp.maximum(m_prev, jnp.max(z, axis=-1, keepdims=True))
                    alpha = jnp.exp(m_prev - m_new)
                    p = jnp.exp(z - _lanes(m_new, bk))
                    l_s[hd, rows, :] = alpha * l_s[hd, rows, :] + jnp.sum(p, axis=-1, keepdims=True)
                    m_s[hd, rows, :] = m_new
                    p_rows.append(p.astype(BF16))
                    a_rows.append(alpha)
                pv = _dot(jnp.concatenate(p_rows, axis=0), v_ref[:, kh], NN)
                acc_s[hd] = jnp.concatenate(a_rows, axis=0) * acc_s[hd] + pv

        @pl.when(step == ns - 1)
        def _():
            for hd in range(nh):
                o_ref[:, _hs(hd)] = (acc_s[hd] / l_s[hd]).astype(o_dtype)
                lse_ref[:, _hs(hd)] = m_s[hd] + jnp.log(l_s[hd])

    in_specs = [q_spec, k_spec, v_spec] + ([] if bias is None else [b_spec])
    args = [q, k, v] + ([] if bias is None else [bias])
    stat = pltpu.VMEM((nh, bq, LANES), F32)
    return pl.pallas_call(
        body, name=name, grid=grid, in_specs=in_specs, out_specs=[o_spec, o_spec],
        out_shape=[jax.ShapeDtypeStruct(o_shape, o_dtype), jax.ShapeDtypeStruct(o_shape, F32)],
        scratch_shapes=[stat, stat, stat],
        compiler_params=_params(("parallel", "parallel", "arbitrary")),
    )(*args)


def _probs(q_ref, k_ref, v_ref, do_ref, lse_ref, dlt_ref, b_ref, hd, kh, bq, bk, want_p=True, on_ds=None):
    s = _dot(q_ref[:, _hs(hd)], k_ref[:, kh], NT)
    dp = _dot(do_ref[:, _hs(hd)], v_ref[:, kh], NT)
    p_rows, ds_rows = [], []
    for r0 in range(0, bq, ATTN_ROWS):
        rows = slice(r0, r0 + ATTN_ROWS)
        z = s[rows] * ATTN_SCALE
        if b_ref is not None:
            z = z + b_ref[hd, rows, :]
        p = jnp.exp(z - _lanes(lse_ref[rows, _hs(hd)], bk))
        ds = p * (dp[rows] - _lanes(dlt_ref[rows, _hs(hd)], bk))
        if on_ds is not None:
            on_ds(rows, ds)
        if want_p:
            p_rows.append(p.astype(BF16))
        ds_rows.append(ds.astype(BF16))
    return (jnp.concatenate(p_rows, axis=0) if want_p else None), jnp.concatenate(ds_rows, axis=0)


def attn_bwd_dq(name, q, k, v, do, lse, dlt, *, grid, q_spec, k_spec, v_spec, nh, bq, bk, o_shape):
    ns = grid[2]
    scale = HEAD_DIM ** -0.5

    def body(q_ref, k_ref, v_ref, do_ref, lse_ref, dlt_ref, dq_ref, acc_s):
        step = pl.program_id(2)

        @pl.when(step == 0)
        def _():
            acc_s[...] = jnp.zeros_like(acc_s)

        for hd in range(nh):
            _, ds = _probs(q_ref, k_ref, v_ref, do_ref, lse_ref, dlt_ref, None, hd, _hs(0), bq, bk, want_p=False)
            acc_s[hd] += _dot(ds, k_ref[:, _hs(0)], NN)

        @pl.when(step == ns - 1)
        def _():
            for hd in range(nh):
                dq_ref[:, _hs(hd)] = acc_s[hd] * scale

    return pl.pallas_call(
        body, name=name, grid=grid, in_specs=[q_spec, k_spec, v_spec, q_spec, q_spec, q_spec],
        out_specs=q_spec, out_shape=jax.ShapeDtypeStruct(o_shape, F32),
        scratch_shapes=[pltpu.VMEM((nh, bq, LANES), F32)],
        compiler_params=_params(("parallel", "parallel", "arbitrary")),
    )(q, k, v, do, lse, dlt)


def _always(i, s):
    return s >= 0


def row_delta(name, do, o, n_heads):
    t, width = do.shape
    tm = _tile(t, ROW_TILE)

    def body(do_ref, o_ref, dl_ref, dob_ref):
        for hd in range(n_heads):
            d = do_ref[:, _hs(hd)]
            s = jnp.sum(d * o_ref[:, _hs(hd)].astype(F32), axis=-1, keepdims=True)
            dl_ref[:, _hs(hd)] = jnp.broadcast_to(s, (tm, HEAD_DIM))
            dob_ref[:, _hs(hd)] = d.astype(BF16)

    return pl.pallas_call(
        body, name=name, grid=(t // tm,), in_specs=[_rows(width, tm), _rows(width, tm)],
        out_specs=[_rows(width, tm), _rows(width, tm)],
        out_shape=[jax.ShapeDtypeStruct((t, width), F32), jax.ShapeDtypeStruct((t, width), BF16)],
        compiler_params=_params(("parallel",)),
    )(do, o)


def _a_specs(n_q, n_kv, bq, bk, q_major):
    grp = n_q // n_kv
    if q_major:
        qm, km = (lambda b, i, s: (i, b)), (lambda b, i, s: (s, n_q + b))
        vm = lambda b, i, s: (s, n_q + n_kv + b)
    else:
        qm, km = (lambda b, i, s: (s, b)), (lambda b, i, s: (i, n_q + b))
        vm = lambda b, i, s: (i, n_q + n_kv + b)
    return (pl.BlockSpec((bq, grp * HEAD_DIM), qm), pl.BlockSpec((bk, HEAD_DIM), km),
            pl.BlockSpec((bk, HEAD_DIM), vm))


def mixer_a_fwd(qkv_r, n_q, n_kv):
    t = qkv_r.shape[0]
    bq, bk = _tile(t, A_BQ), _tile(t, A_BK)
    q_spec, k_spec, v_spec = _a_specs(n_q, n_kv, bq, bk, True)
    return attn_fwd("a_attn_fwd", qkv_r, qkv_r, qkv_r, None, grid=(n_kv, t // bq, t // bk),
                    q_spec=q_spec, k_spec=k_spec, v_spec=v_spec, b_spec=None, o_spec=q_spec, valid=_always,
                    nh=n_q // n_kv, shared_kv=True, bq=bq, bk=bk, o_shape=(t, n_q * HEAD_DIM), o_dtype=BF16)


def mixer_a_bwd(qkv_r, do_b, lse, dlt, n_q, n_kv):
    t = qkv_r.shape[0]
    bq, bk = _tile(t, A_BQ), _tile(t, A_BK)
    grp = n_q // n_kv
    q_spec, k_spec, v_spec = _a_specs(n_q, n_kv, bq, bk, True)
    dq = attn_bwd_dq("a_attn_dq", qkv_r, qkv_r, qkv_r, do_b, lse, dlt, grid=(n_kv, t // bq, t // bk),
                     q_spec=q_spec, k_spec=k_spec, v_spec=v_spec, nh=grp, bq=bq, bk=bk,
                     o_shape=(t, n_q * HEAD_DIM))
    q_spec, k_spec, v_spec = _a_specs(n_q, n_kv, bq, bk, False)
    o_spec = pl.BlockSpec((bk, HEAD_DIM), lambda b, i, s: (i, b))
    dk, dv = _attn_bwd_dkv_out(qkv_r, do_b, lse, dlt, grid=(n_kv, t // bk, t // bq), q_spec=q_spec,
                               k_spec=k_spec, v_spec=v_spec, o_spec=o_spec, grp=grp, bq=bq, bk=bk,
                               o_shape=(t, n_kv * HEAD_DIM))
    return dq, dk, dv


def _attn_bwd_dkv_out(qkv_r, do_b, lse, dlt, *, grid, q_spec, k_spec, v_spec, o_spec, grp, bq, bk, o_shape):
    ns = grid[2]
    scale = HEAD_DIM ** -0.5

    def body(q_ref, k_ref, v_ref, do_ref, lse_ref, dlt_ref, dk_ref, dv_ref, dk_s, dv_s):
        step = pl.program_id(2)

        @pl.when(step == 0)
        def _():
            dk_s[...] = jnp.zeros_like(dk_s)
            dv_s[...] = jnp.zeros_like(dv_s)

        for hd in range(grp):
            p, ds = _probs(q_ref, k_ref, v_ref, do_ref, lse_ref, dlt_ref, None, hd, _hs(0), bq, bk)
            dv_s[...] += _dot(p, do_ref[:, _hs(hd)], TN)
            dk_s[...] += _dot(ds, q_ref[:, _hs(hd)], TN)

        @pl.when(step == ns - 1)
        def _():
            dk_ref[...] = dk_s[...] * scale
            dv_ref[...] = dv_s[...]

    acc = pltpu.VMEM((bk, HEAD_DIM), F32)
    return pl.pallas_call(
        body, name="a_attn_dkv", grid=grid, in_specs=[q_spec, k_spec, v_spec, q_spec, q_spec, q_spec],
        out_specs=[o_spec, o_spec], out_shape=[jax.ShapeDtypeStruct(o_shape, F32)] * 2,
        scratch_shapes=[acc, acc], compiler_params=_params(("parallel", "parallel", "arbitrary")),
    )(qkv_r, qkv_r, qkv_r, do_b, lse, dlt)


def t5_bucket(rel):
    nb = REL_BUCKETS // 2
    max_exact = nb // 2
    base = jnp.where(rel > 0, nb, 0)
    n = jnp.abs(rel)
    nf = jnp.maximum(n, 1).astype(F32)
    large = max_exact + (jnp.log(nf / max_exact) / math.log(REL_MAX_DISTANCE / max_exact)
                         * (nb - max_exact)).astype(jnp.int32)
    large = jnp.minimum(large, nb - 1)
    return base + jnp.where(n < max_exact, n, large)


def band_stride(t, win, dil):
    return 1 if t % B_BQ == 0 and win // 2 <= B_BQ else dil


def band_tables(rel_bias_g, win, dil, stride, bq):
    a = jnp.arange(bq)[:, None]
    b = jnp.arange(bq)[None, :]
    rel = jnp.stack([(s - 1) * bq + b - a for s in range(3)]) * stride
    ok = (jnp.abs(rel) <= win // 2) & (rel % dil == 0)
    bucket = t5_bucket(rel)
    bias = jnp.zeros((rel_bias_g.shape[1],) + rel.shape, F32)
    for r in range(REL_BUCKETS):
        bias = bias + jnp.where(bucket[None] == r, rel_bias_g[r][:, None, None, None], 0.0)
    return jnp.where(ok[None], bias, NEG_INF), jnp.where(ok, bucket, -1).astype(jnp.int32)


def band_block(t, win, stride):
    half_rows = win // 2 // stride
    return _tile(t // stride, min(B_BQ, max(LANES, -(-half_rows // LANES) * LANES)))


def _b_geometry(t, dil, g, n_groups, bq):
    hg = B_HEADS_PER_GROUP
    length = t // dil
    nblk = length // bq
    gw = hg * HEAD_DIM
    per_tok = 3 * n_groups
    return hg, length, bq, nblk, gw, per_tok


def mixer_b_group_fwd(qkv, bias, dil, g, n_groups, tag):
    t = qkv.shape[0]
    hg, length, bq, nblk, gw, per_tok = _b_geometry(t, dil, g, n_groups, bias.shape[2])
    view = qkv.reshape(length, dil * qkv.shape[1])
    col = lambda c, which: c * per_tok + 3 * g + which
    kblk = lambda i, s: jnp.clip(i - 1 + s, 0, nblk - 1)
    spec = lambda which, streamed: pl.BlockSpec(
        (bq, gw), (lambda c, i, s: (kblk(i, s), col(c, which))) if streamed else (lambda c, i, s: (i, col(c, which))))
    valid = lambda i, s: (i - 1 + s >= 0) & (i - 1 + s < nblk)
    o, lz = attn_fwd(f"b_attn_fwd_d{tag}", view, view, view, bias, grid=(dil, nblk, 3),
                     q_spec=spec(0, False), k_spec=spec(1, True), v_spec=spec(2, True),
                     b_spec=pl.BlockSpec((hg, None, bq, bq), lambda c, i, s: (0, s, 0, 0)),
                     o_spec=pl.BlockSpec((bq, gw), lambda c, i, s: (i, c)), valid=valid, nh=hg,
                     shared_kv=False, bq=bq, bk=bq, o_shape=(length, dil * gw), o_dtype=F32)
    return o.reshape(t, gw), lz.reshape(t, gw)


def mixer_b_group_bwd(qkv, bias, do_g, lz_g, dlt_g, dil, g, n_groups, tag):
    t = qkv.shape[0]
    hg, length, bq, nblk, gw, per_tok = _b_geometry(t, dil, g, n_groups, bias.shape[2])
    view = qkv.reshape(length, dil * qkv.shape[1])
    dov, lzv, dlv = (x.reshape(length, dil * gw) for x in (do_g, lz_g, dlt_g))
    col = lambda c, which: c * per_tok + 3 * g + which
    nbr = lambda i, s: jnp.clip(i - 1 + s, 0, nblk - 1)
    valid = lambda i, s: (i - 1 + s >= 0) & (i - 1 + s < nblk)
    q_spec = pl.BlockSpec((bq, gw), lambda c, i, s: (i, col(c, 0)))
    k_spec = pl.BlockSpec((bq, gw), lambda c, i, s: (nbr(i, s), col(c, 1)))
    v_spec = pl.BlockSpec((bq, gw), lambda c, i, s: (nbr(i, s), col(c, 2)))
    stat = pl.BlockSpec((bq, gw), lambda c, i, s: (i, c))
    dq, dbias = _band_bwd_dq(f"b_attn_dq_d{tag}", view, dov, lzv, dlv, bias, grid=(dil, nblk, 3),
                             q_spec=q_spec, k_spec=k_spec, v_spec=v_spec, stat_spec=stat,
                             b_spec=pl.BlockSpec((hg, None, bq, bq), lambda c, i, s: (0, s, 0, 0)),
                             valid=valid, nh=hg, bq=bq, o_shape=(length, dil * gw))
    q_spec = pl.BlockSpec((bq, gw), lambda c, i, s: (nbr(i, s), col(c, 0)))
    k_spec = pl.BlockSpec((bq, gw), lambda c, i, s: (i, col(c, 1)))
    v_spec = pl.BlockSpec((bq, gw), lambda c, i, s: (i, col(c, 2)))
    stat = pl.BlockSpec((bq, gw), lambda c, i, s: (nbr(i, s), c))
    dk, dv = _band_bwd_dkv(f"b_attn_dkv_d{tag}", view, dov, lzv, dlv, bias, grid=(dil, nblk, 3),
                           q_spec=q_spec, k_spec=k_spec, v_spec=v_spec, stat_spec=stat,
                           b_spec=pl.BlockSpec((hg, None, bq, bq), lambda c, i, s: (0, 2 - s, 0, 0)),
                           o_spec=pl.BlockSpec((bq, gw), lambda c, i, s: (i, c)),
                           valid=valid, nh=hg, bq=bq, o_shape=(length, dil * gw))
    return dq.reshape(t, gw), dk.reshape(t, gw), dv.reshape(t, gw), dbias


def _band_bwd_dq(name, view, do, lse, dlt, bias, *, grid, q_spec, k_spec, v_spec, stat_spec, b_spec, valid,
                 nh, bq, o_shape):
    scale = HEAD_DIM ** -0.5
    bias_shape = (nh, 3, bq, bq)

    def body(q_ref, k_ref, v_ref, do_ref, lse_ref, dlt_ref, b_ref, dq_ref, db_ref, acc_s):
        step = pl.program_id(2)

        @pl.when((pl.program_id(0) == 0) & (pl.program_id(1) == 0) & (step == 0))
        def _():
            db_ref[...] = jnp.zeros_like(db_ref)

        @pl.when(step == 0)
        def _():
            acc_s[...] = jnp.zeros_like(acc_s)

        @pl.when(valid(pl.program_id(1), step))
        def _():
            for hd in range(nh):
                def add_bias_grad(rows, ds, hd=hd):
                    db_ref[hd, step, rows, :] += ds

                _, ds = _probs(q_ref, k_ref, v_ref, do_ref, lse_ref, dlt_ref, b_ref, hd, _hs(hd), bq, bq,
                               want_p=False, on_ds=add_bias_grad)
                acc_s[hd] += _dot(ds, k_ref[:, _hs(hd)], NN)

        @pl.when(step == 2)
        def _():
            for hd in range(nh):
                dq_ref[:, _hs(hd)] = (acc_s[hd] * scale).astype(BF16)

    return pl.pallas_call(
        body, name=name, grid=grid,
        in_specs=[q_spec, k_spec, v_spec, stat_spec, stat_spec, stat_spec, b_spec],
        out_specs=[stat_spec, pl.BlockSpec(bias_shape, lambda c, i, s: (0, 0, 0, 0))],
        out_shape=[jax.ShapeDtypeStruct(o_shape, BF16), jax.ShapeDtypeStruct(bias_shape, F32)],
        scratch_shapes=[pltpu.VMEM((nh, bq, LANES), F32)], compiler_params=_params(("arbitrary",) * 3),
    )(view, view, view, do, lse, dlt, bias)


def _band_bwd_dkv(name, view, do, lse, dlt, bias, *, grid, q_spec, k_spec, v_spec, stat_spec, b_spec, o_spec,
                  valid, nh, bq, o_shape):
    scale = HEAD_DIM ** -0.5

    def body(q_ref, k_ref, v_ref, do_ref, lse_ref, dlt_ref, b_ref, dk_ref, dv_ref, dk_s, dv_s):
        step = pl.program_id(2)

        @pl.when(step == 0)
        def _():
            dk_s[...] = jnp.zeros_like(dk_s)
            dv_s[...] = jnp.zeros_like(dv_s)

        @pl.when(valid(pl.program_id(1), step))
        def _():
            for hd in range(nh):
                p, ds = _probs(q_ref, k_ref, v_ref, do_ref, lse_ref, dlt_ref, b_ref, hd, _hs(hd), bq, bq)
                dv_s[hd] += _dot(p, do_ref[:, _hs(hd)], TN)
                dk_s[hd] += _dot(ds, q_ref[:, _hs(hd)], TN)

        @pl.when(step == 2)
        def _():
            for hd in range(nh):
                dk_ref[:, _hs(hd)] = (dk_s[hd] * scale).astype(BF16)
                dv_ref[:, _hs(hd)] = dv_s[hd].astype(BF16)

    acc = pltpu.VMEM((nh, bq, LANES), F32)
    return pl.pallas_call(
        body, name=name, grid=grid,
        in_specs=[q_spec, k_spec, v_spec, stat_spec, stat_spec, stat_spec, b_spec],
        out_specs=[o_spec, o_spec], out_shape=[jax.ShapeDtypeStruct(o_shape, BF16)] * 2,
        scratch_shapes=[acc, acc], compiler_params=_params(("parallel", "parallel", "arbitrary")),
    )(view, view, view, do, lse, dlt, bias)


def bias_bucket_sums(name, dbias, bucket):
    nh, _, bq, _ = dbias.shape
    db2 = dbias.reshape(nh, 3 * bq, bq)
    bk2 = bucket.reshape(3 * bq, bq)

    def body(db_ref, bk_ref, o_ref):
        row = lax.broadcasted_iota(jnp.int32, (nh, LANES), 0)
        lane = lax.broadcasted_iota(jnp.int32, (nh, LANES), 1)
        out = jnp.zeros((nh, LANES), F32)
        bkt = bk_ref[...]
        for hd in range(nh):
            x = db_ref[hd]
            for r in range(REL_BUCKETS):
                part = jnp.sum(jnp.where(bkt == r, x, 0.0), axis=1, keepdims=True)
                tot = jnp.sum(part, axis=0, keepdims=True)
                out = out + jnp.where((row == hd) & (lane == r), tot, 0.0)
        o_ref[...] = out

    return pl.pallas_call(
        body, name=name, out_shape=jax.ShapeDtypeStruct((nh, LANES), F32),
        compiler_params=pltpu.CompilerParams(vmem_limit_bytes=VMEM_LIMIT),
    )(db2, bk2)


def combine_fwd(name, outs, lzs):
    n_g = len(outs)
    t, gw = outs[0].shape
    tm = _tile(t, ROW_TILE)

    def body(*refs):
        o_refs, lz_refs, y_ref = refs[:n_g], refs[n_g:2 * n_g], refs[2 * n_g]
        lz = [r[...] for r in lz_refs]
        mx = functools.reduce(jnp.maximum, lz)
        e = [jnp.exp(x - mx) for x in lz]
        den = functools.reduce(lambda a, b: a + b, e)
        for g in range(n_g):
            y_ref[:, g * gw:(g + 1) * gw] = (e[g] / den * o_refs[g][...]).astype(BF16)

    return pl.pallas_call(
        body, name=name, grid=(t // tm,), in_specs=[_rows(gw, tm)] * (2 * n_g), out_specs=_rows(n_g * gw, tm),
        out_shape=jax.ShapeDtypeStruct((t, n_g * gw), BF16), compiler_params=_params(("parallel",)),
    )(*outs, *lzs)


def combine_bwd(name, dy, outs, lzs):
    n_g = len(outs)
    t, gw = outs[0].shape
    tm = _tile(t, ROW_TILE)
    nh = gw // HEAD_DIM

    def body(*refs):
        dy_ref = refs[0]
        o_refs, lz_refs = refs[1:1 + n_g], refs[1 + n_g:1 + 2 * n_g]
        do_refs, dl_refs = refs[1 + 2 * n_g:1 + 3 * n_g], refs[1 + 3 * n_g:]
        lz = [r[...] for r in lz_refs]
        mx = functools.reduce(jnp.maximum, lz)
        e = [jnp.exp(x - mx) for x in lz]
        den = functools.reduce(lambda a, b: a + b, e)
        wts = [x / den for x in e]
        for g in range(n_g):
            do_refs[g][...] = (wts[g] * dy_ref[:, g * gw:(g + 1) * gw]).astype(BF16)
        for hd in range(nh):
            mix = jnp.zeros((tm, HEAD_DIM), F32)
            for g in range(n_g):
                prod = dy_ref[:, g * gw + hd * HEAD_DIM:g * gw + (hd + 1) * HEAD_DIM] * o_refs[g][:, _hs(hd)]
                dw = jnp.broadcast_to(jnp.sum(prod, axis=-1, keepdims=True), (tm, HEAD_DIM))
                mix = mix + wts[g][:, _hs(hd)] * dw
            for g in range(n_g):
                dl_refs[g][:, _hs(hd)] = wts[g][:, _hs(hd)] * mix

    return pl.pallas_call(
        body, name=name, grid=(t // tm,),
        in_specs=[_rows(n_g * gw, tm)] + [_rows(gw, tm)] * (2 * n_g),
        out_specs=[_rows(gw, tm)] * (2 * n_g),
        out_shape=[jax.ShapeDtypeStruct((t, gw), BF16)] * n_g + [jax.ShapeDtypeStruct((t, gw), F32)] * n_g,
        compiler_params=_params(("parallel",)),
    )(dy, *outs, *lzs)


def _shifted(u):
    t = u.shape[0]
    row = lax.broadcasted_iota(jnp.int32, u.shape, 0)
    prev = jnp.where(row == 0, 0.0, pltpu.roll(u, 1, 0))
    nxt = jnp.where(row == t - 1, 0.0, pltpu.roll(u, t - 1, 0))
    return prev, nxt


def _conv3(u, prev, nxt, w_ref, b):
    return w_ref[0:1, :] * prev + w_ref[1:2, :] * u + w_ref[2:3, :] * nxt + b


def _conv3_t(d, w_ref):
    prev, nxt = _shifted(d)
    return w_ref[0:1, :] * nxt + w_ref[1:2, :] * d + w_ref[2:3, :] * prev


def conv_act_fwd(name, u2, cw2, cb2):
    _, t, dff = u2.shape
    tn = LANES

    def body(u_ref, w_ref, b_ref, o_ref):
        ug, uv = u_ref[0], u_ref[1]
        cg = _conv3(ug, *_shifted(ug), w_ref.at[0], b_ref[0])
        cv = _conv3(uv, *_shifted(uv), w_ref.at[1], b_ref[1])
        o_ref[...] = (cg * jax.nn.sigmoid(cg) * cv).astype(BF16)

    return pl.pallas_call(
        body, name=name, grid=(dff // tn,),
        in_specs=[pl.BlockSpec((2, t, tn), lambda j: (0, 0, j)), pl.BlockSpec((2, 3, tn), lambda j: (0, 0, j)),
                  pl.BlockSpec((2, 1, tn), lambda j: (0, 0, j))],
        out_specs=pl.BlockSpec((t, tn), lambda j: (0, j)), out_shape=jax.ShapeDtypeStruct((t, dff), BF16),
        compiler_params=_params(("parallel",)),
    )(u2, cw2, cb2)


def conv_act_bwd(name, u2, cw2, cb2, dact, after=()):
    _, t, dff = u2.shape
    tn = LANES

    def body(u_ref, w_ref, b_ref, d_ref, *rest):
        du_ref, dw_ref = rest[-2:]
        d = d_ref[...]
        ug, uv = u_ref[0], u_ref[1]
        shifted = (_shifted(ug), _shifted(uv))
        cg = _conv3(ug, *shifted[0], w_ref.at[0], b_ref[0])
        cv = _conv3(uv, *shifted[1], w_ref.at[1], b_ref[1])
        sg = jax.nn.sigmoid(cg)
        dcv = d * (cg * sg)
        dcg = d * cv * (sg * (1.0 + cg * (1.0 - sg)))
        du_ref[0] = _conv3_t(dcg, w_ref.at[0]).astype(BF16)
        du_ref[1] = _conv3_t(dcv, w_ref.at[1]).astype(BF16)
        for half, (dc, u) in enumerate(((dcg, ug), (dcv, uv))):
            prev, nxt = shifted[half]
            for tap, x in enumerate((prev, u, nxt)):
                dw_ref[half, tap:tap + 1, :] = jnp.sum(dc * x, axis=0, keepdims=True)
            dw_ref[half, 3:4, :] = jnp.sum(dc, axis=0, keepdims=True)
            dw_ref[half, 4:8, :] = jnp.zeros((4, tn), F32)

    return pl.pallas_call(
        body, name=name, grid=(dff // tn,),
        in_specs=[pl.BlockSpec((2, t, tn), lambda j: (0, 0, j)), pl.BlockSpec((2, 3, tn), lambda j: (0, 0, j)),
                  pl.BlockSpec((2, 1, tn), lambda j: (0, 0, j)), pl.BlockSpec((t, tn), lambda j: (0, j))]
        + [pl.BlockSpec(memory_space=pl.ANY)] * len(after),
        out_specs=[pl.BlockSpec((2, t, tn), lambda j: (0, 0, j)), pl.BlockSpec((2, 8, tn), lambda j: (0, 0, j))],
        out_shape=[jax.ShapeDtypeStruct((2, t, dff), BF16), jax.ShapeDtypeStruct((2, 8, dff), F32)],
        compiler_params=_params(("parallel",)),
    )(u2, cw2, cb2, dact, *after)


GATHER_ID, SIBLING_ID, CHIPS_ID = 0, 1, 2


def _place():
    x, y, c = lax.axis_index("x"), lax.axis_index("y"), lax.axis_index("c")
    chips = [(1 - x, y), (x, 1 - y), (1 - x, 1 - y)]
    return x, y, c, chips


def _handshake(peers):
    barrier = pltpu.get_barrier_semaphore()
    for peer in peers:
        pl.semaphore_signal(barrier, inc=1, device_id=peer, device_id_type=MESH)
    pl.semaphore_wait(barrier, len(peers))


def _sequencer(name, body, out_type, scratch_types, collective_id):
    return pl.kernel(body, out_type=out_type, mesh=plsc.ScalarSubcoreMesh(axis_name="seq", num_cores=1),
                     scratch_types=scratch_types, name=name,
                     compiler_params=pltpu.CompilerParams(collective_id=collective_id))


def _gather_body(n):
    def body(*refs):
        src, out = refs[:n], refs[n:2 * n]
        send, recv, loc = refs[2 * n:]
        x, y, c, chips = _place()
        sibling = (x, y, 1 - c)
        _handshake([sibling] + [(*chip, c) for chip in chips])

        def slot(a, px, py, pc):
            return out[a].at[4 * px + 2 * py + pc]

        def copy(a, k, block, to, from_src=False):
            return pltpu.make_async_remote_copy(
                src_ref=src[a] if from_src else slot(a, *block), dst_ref=slot(a, *block),
                send_sem=send.at[a, k], recv_sem=recv.at[a, k], device_id=to, device_id_type=MESH)

        mine = [pltpu.make_async_copy(src[a], slot(a, x, y, c), loc.at[a]) for a in range(n)]
        for cp in mine:
            cp.start()
        first = []
        for a in range(n):
            first.append(copy(a, 0, (x, y, c), sibling, True))
            first += [copy(a, 1 + j, (x, y, c), (*chip, c), True) for j, chip in enumerate(chips)]
        for cp in first:
            cp.start()
        passed = []
        for j, chip in enumerate(chips):
            for a in range(n):
                copy(a, 1 + j, (*chip, c), (x, y, c)).wait_recv()
                cp = copy(a, 4 + j, (*chip, c), sibling)
                cp.start()
                passed.append(cp)
        for a in range(n):
            copy(a, 0, sibling, (x, y, c)).wait_recv()
            for j, chip in enumerate(chips):
                copy(a, 4 + j, (*chip, 1 - c), (x, y, c)).wait_recv()
        for cp in first + passed:
            cp.wait_send()
        for cp in mine:
            cp.wait()

    return body


def gather_layer(name, shards):
    n = len(shards)
    out_type = [jax.ShapeDtypeStruct((N_DEV,) + s.shape, s.dtype) for s in shards]
    scratch = [pltpu.SemaphoreType.DMA((n, 7)), pltpu.SemaphoreType.DMA((n, 7)), pltpu.SemaphoreType.DMA((n,))]
    return _sequencer(name, _gather_body(n), out_type, scratch, GATHER_ID)(*shards)


def _to_sibling_body(n):
    def body(*refs):
        src, got = refs[:n], refs[n:2 * n]
        send, recv = refs[2 * n:]
        x, y, c, _ = _place()
        sibling = (x, y, 1 - c)
        _handshake([sibling])
        remote = []
        for a in range(n):
            for q in range(4):
                remote.append(pltpu.make_async_remote_copy(
                    src_ref=src[a].at[2 * q + 1 - c], dst_ref=got[a].at[q], send_sem=send.at[a, q],
                    recv_sem=recv.at[a, q], device_id=sibling, device_id_type=MESH))
        for cp in remote:
            cp.start()
        for cp in remote:
            cp.wait()

    return body


def grads_to_sibling(name, grads):
    n = len(grads)
    out_type = [jax.ShapeDtypeStruct((4,) + g.shape[1:], g.dtype) for g in grads]
    scratch = [pltpu.SemaphoreType.DMA((n, 4)), pltpu.SemaphoreType.DMA((n, 4))]
    return _sequencer(name, _to_sibling_body(n), out_type, scratch, SIBLING_ID)(*grads)


def _to_chips_body(n):
    def body(*refs):
        src, got = refs[:n], refs[n:2 * n]
        send, recv = refs[2 * n:]
        x, y, c, chips = _place()
        _handshake([(*chip, c) for chip in chips])
        remote = []
        for a in range(n):
            for j, (px, py) in enumerate(chips):
                remote.append(pltpu.make_async_remote_copy(
                    src_ref=src[a].at[2 * px + py], dst_ref=got[a].at[j], send_sem=send.at[a, j],
                    recv_sem=recv.at[a, j], device_id=(px, py, c), device_id_type=MESH))
        for cp in remote:
            cp.start()
        for cp in remote:
            cp.wait()

    return body


def grads_to_chips(name, parts):
    n = len(parts)
    out_type = [jax.ShapeDtypeStruct((3,) + p.shape[1:], p.dtype) for p in parts]
    scratch = [pltpu.SemaphoreType.DMA((n, 3)), pltpu.SemaphoreType.DMA((n, 3))]
    return _sequencer(name, _to_chips_body(n), out_type, scratch, CHIPS_ID)(*parts)


def all_reduce_small(name, vec):
    rows, m = vec.shape

    def body(x_ref, o_ref, buf, send, recv):
        x, y, c, chips = _place()
        sibling = (x, y, 1 - c)

        def blk(px, py, pc):
            return buf.at[pl.ds(pl.multiple_of((4 * px + 2 * py + pc) * rows, rows), rows), :]

        def copy(k, block, to):
            return pltpu.make_async_remote_copy(src_ref=blk(*block), dst_ref=blk(*block), send_sem=send.at[k],
                                                recv_sem=recv.at[k], device_id=to, device_id_type=MESH)

        blk(x, y, c)[...] = x_ref[...]
        first = [copy(0, (x, y, c), sibling)] + [copy(1 + j, (x, y, c), (*chip, c)) for j, chip in enumerate(chips)]
        for cp in first:
            cp.start()
        passed = [copy(4 + j, (*chip, c), sibling) for j, chip in enumerate(chips)]
        for j, chip in enumerate(chips):
            copy(1 + j, (*chip, c), (x, y, c)).wait_recv()
            passed[j].start()
        copy(0, sibling, (x, y, c)).wait_recv()
        for j, chip in enumerate(chips):
            copy(4 + j, (*chip, 1 - c), (x, y, c)).wait_recv()
        for cp in first + passed:
            cp.wait_send()
        tot = buf[0:rows, :]
        for dev in range(1, N_DEV):
            tot = tot + buf[dev * rows:(dev + 1) * rows, :]
        o_ref[...] = tot

    return pl.pallas_call(
        body, name=name, in_specs=[pl.BlockSpec(memory_space=pltpu.VMEM)],
        out_specs=pl.BlockSpec(memory_space=pltpu.VMEM), out_shape=jax.ShapeDtypeStruct((rows, m), F32),
        scratch_shapes=[pltpu.VMEM((N_DEV * rows, m), F32), pltpu.SemaphoreType.DMA((7,)),
                        pltpu.SemaphoreType.DMA((7,))],
        compiler_params=pltpu.CompilerParams(vmem_limit_bytes=VMEM_LIMIT),
    )(vec)


def _ew_tiles(rows, cols, max_elems=1 << 18):
    tr = rows
    for cand in (1024, 512, 256, 128, 64, 32, 16):
        if rows % cand == 0 and cand * cols <= max_elems:
            tr = cand
            break
    return tr


def chip_sum(name, full, got, core):
    _, kdim, ncol = full.shape
    tr = _ew_tiles(kdim, ncol, max_elems=1 << 20)
    blk = (None, tr, ncol)
    by_chip = pl.BlockSpec(blk, lambda q, i, c: (q, i, 0))

    def body(c_ref, a_ref, b_ref, o_ref):
        o_ref[...] = (a_ref[...].astype(F32) + b_ref[...].astype(F32)).astype(BF16)

    return pl.pallas_call(
        body, name=name,
        grid_spec=pltpu.PrefetchScalarGridSpec(
            num_scalar_prefetch=1, grid=(4, kdim // tr),
            in_specs=[pl.BlockSpec(blk, lambda q, i, c: (2 * q + c[0], i, 0)), by_chip], out_specs=by_chip),
        out_shape=jax.ShapeDtypeStruct((4, kdim, ncol), BF16),
        compiler_params=_params(("parallel", "parallel")),
    )(core, full, got)


def _adamw_math(w, g, m, v):
    m = ADAM_B1 * m + (1.0 - ADAM_B1) * g
    v = ADAM_B2 * v + (1.0 - ADAM_B2) * (g * g)
    m_hat = m / (1.0 - ADAM_B1 ** ADAM_STEP)
    v_hat = v / (1.0 - ADAM_B2 ** ADAM_STEP)
    delta = -ADAM_LR * (m_hat / (jnp.sqrt(v_hat) + ADAM_EPS) + ADAM_WD * w)
    return delta, m, v


def adamw_layer(name, sums, got, w, m, v, layer, chip):
    _, kdim, ncol = sums.shape
    tr = _ew_tiles(kdim, ncol)
    out = pl.BlockSpec((tr, ncol), lambda i, q: (i, 0))
    mine = pl.BlockSpec((None, tr, ncol), lambda i, q: (q[0], i, 0))
    others = pl.BlockSpec((3, tr, ncol), lambda i, q: (0, i, 0))
    param = pl.BlockSpec((None, tr, ncol), lambda i, q: (layer, i, 0))

    def body(q_ref, o_ref, g_ref, w_ref, m_ref, v_ref, go_ref, d_ref, mo_ref, vo_ref):
        g = o_ref[...].astype(F32)
        for j in range(3):
            g = g + g_ref[j].astype(F32)
        d, mn, vn = _adamw_math(w_ref[...], g, m_ref[...], v_ref[...])
        go_ref[...] = g
        d_ref[...] = d
        mo_ref[...] = mn
        vo_ref[...] = vn

    return pl.pallas_call(
        body, name=name,
        grid_spec=pltpu.PrefetchScalarGridSpec(
            num_scalar_prefetch=1, grid=(kdim // tr,),
            in_specs=[mine, others, param, param, param], out_specs=[out] * 4),
        out_shape=[jax.ShapeDtypeStruct((kdim, ncol), F32)] * 4,
        compiler_params=_params(("parallel",)),
    )(chip, sums, got, w, m, v)


def adamw_small(name, g, w, m, v):
    def body(g_ref, w_ref, m_ref, v_ref, d_ref, mo_ref, vo_ref):
        d, mn, vn = _adamw_math(w_ref[...], g_ref[...], m_ref[...], v_ref[...])
        d_ref[...] = d
        mo_ref[...] = mn
        vo_ref[...] = vn

    vm = pl.BlockSpec(memory_space=pltpu.VMEM)
    return pl.pallas_call(
        body, name=name, in_specs=[vm] * 4, out_specs=[vm] * 3,
        out_shape=[jax.ShapeDtypeStruct(g.shape, F32)] * 3,
        compiler_params=pltpu.CompilerParams(vmem_limit_bytes=VMEM_LIMIT),
    )(g, w, m, v)


def _pack(parts, width):
    flat = jnp.concatenate([p.reshape(-1).astype(F32) for p in parts])
    pad = (-flat.shape[0]) % width
    return jnp.pad(flat, (0, pad)).reshape(-1, width) if pad else flat.reshape(-1, width)


def _unpack(packed, shapes):
    flat = packed.reshape(-1)
    out, off = [], 0
    for s in shapes:
        size = math.prod(s)
        out.append(flat[off:off + size].reshape(s))
        off += size
    return out


def _local_step(h, target, layers, params, on_grads=None):
    a_q_gain, a_k_gain, rel_bias, mix_norm, ffn_norm, conv_b, final_norm = params
    t, d = h.shape
    depth = len(layers)
    n_groups = len(B_GROUPS)
    hg = B_HEADS_PER_GROUP
    n_kv = A_KV_HEADS
    w_a, w_b, w_u = layers[0][0].shape[2], layers[1][0].shape[2], layers[0][2].shape[2]
    n_q = w_a * N_DEV // HEAD_DIM - 2 * n_kv
    dff = layers[0][3].shape[0]
    n_a = (depth + 1) // 2
    cb_full = conv_b.reshape(depth, 2, 1, dff)

    cos, sin = rope_tables(t)
    strides = [band_stride(t, win, dil) for win, dil in B_GROUPS]
    tables = [band_tables(rel_bias[:, g * hg:(g + 1) * hg], win, dil, strides[g], band_block(t, win, strides[g]))
              for g, (win, dil) in enumerate(B_GROUPS)]

    saved = []
    for i in range(depth):
        j = i // 2
        w_qkv, w_o, w_up_i, w_down_i, cw = layers[i]
        s = {"h_in": h}
        hn = rms_fwd("mix_norm_fwd", h, mix_norm[i])
        s["hn"] = hn
        if i % 2 == 0:
            qkv = mm_col_fwd("a_qkv_fwd", hn, w_qkv, F32)
            qkv_r = qk_prep_fwd("a_qk_prep_fwd", qkv, a_q_gain[j], a_k_gain[j], cos, sin, n_q, n_kv)
            o, lse = mixer_a_fwd(qkv_r, n_q, n_kv)
            s.update(qkv=qkv, qkv_r=qkv_r, o=o, lse=lse)
            h = mm_row_fwd("a_out_fwd", o, w_o, h)
        else:
            qkv = mm_col_fwd("b_qkv_fwd", hn, w_qkv, BF16)
            outs, lzs = [], []
            for g, (win, dil) in enumerate(B_GROUPS):
                o_g, lz_g = mixer_b_group_fwd(qkv, tables[g][0], strides[g], g, n_groups, dil)
                outs.append(o_g)
                lzs.append(lz_g)
            y = combine_fwd("b_combine_fwd", outs, lzs)
            s.update(qkv=qkv, outs=outs, lzs=lzs, y=y)
            h = mm_row_fwd("b_out_fwd", y, w_o, h)
        s["h_mid"] = h
        hn2 = rms_fwd("ffn_norm_fwd", h, ffn_norm[i])
        u2 = mm_col_fwd("ffn_up_fwd", hn2, w_up_i, F32, split=2)
        act = conv_act_fwd("ffn_conv_act_fwd", u2, cw, cb_full[i])
        s.update(hn2=hn2, u2=u2, act=act)
        h = mm_row_fwd("ffn_down_fwd", act, w_down_i, h)
        saved.append(s)

    dh, dh_b, d_final, loss_part = loss_head("loss_head", h, final_norm, target)

    d_mix, d_ffn, d_cw, d_cb = [None] * depth, [None] * depth, [None] * depth, [None] * depth
    d_qg, d_kg = [None] * n_a, [None] * n_a
    d_rel = jnp.zeros((n_groups * hg, LANES), F32)
    layer_grads = [{} for _ in range(depth)]
    pending = []

    def settle():
        done = []
        while pending:
            i_p, part_p, finish = pending.pop()
            layer_grads[i_p][part_p] = finish()
            done += [upd[0] for upd in layer_grads[i_p][part_p]]
        return done

    early = []

    def register(i_p, part_p, grads):
        if on_grads is None:
            layer_grads[i_p][part_p] = grads
        else:
            first, finish = on_grads(i_p, part_p, grads)
            early.extend(first)
            pending.append((i_p, part_p, finish))

    def take_early():
        first = tuple(early)
        early.clear()
        return first

    for i in reversed(range(depth)):
        j = i // 2
        w_qkv, w_o, w_up_i, w_down_i, cw = layers[i]
        s = saved[i]
        dact = mm_row_dx("ffn_down_dx", dh_b, w_down_i)
        g_down = mm_row_dw("ffn_down_dw", s["act"], dh_b)
        du2, dcw = conv_act_bwd("ffn_conv_act_bwd", s["u2"], cw, cb_full[i], dact, take_early())
        d_cw[i] = dcw[:, 0:3, :].transpose(1, 0, 2).reshape(3, 2 * dff)
        d_cb[i] = dcw[:, 3, :].reshape(2 * dff)
        g_up = mm_col_dw("ffn_up_dw", s["hn2"], du2, w_u, split=2)
        dhn2 = mm_col_dx("ffn_up_dx", du2, w_up_i, split=2)
        dh, dh_b, d_ffn[i] = rms_bwd("ffn_norm_bwd", s["h_mid"], ffn_norm[i], dhn2, dh, settle())
        register(i, "ffn", [g_up, g_down.reshape(N_DEV, -1, d)])
        if i % 2 == 0:
            do = mm_row_dx("a_out_dx", dh_b, w_o, take_early())
            g_o = mm_row_dw("a_out_dw", s["o"], dh_b)
            dlt, do_b = row_delta("a_delta", do, s["o"], n_q)
            dq, dk, dv = mixer_a_bwd(s["qkv_r"], do_b, s["lse"], dlt, n_q, n_kv)
            dqkv, dgain = qk_prep_bwd("a_qk_prep_bwd", s["qkv"], dq, dk, dv, a_q_gain[j], a_k_gain[j], cos, sin,
                                      n_q, n_kv)
            d_qg[j], d_kg[j] = dgain[0], dgain[1]
            g_qkv = mm_col_dw("a_qkv_dw", s["hn"], dqkv, w_a)
            dhn = mm_col_dx("a_qkv_dx", dqkv, w_qkv)
        else:
            dy = mm_row_dx("b_out_dx", dh_b, w_o, take_early())
            g_o = mm_row_dw("b_out_dw", s["y"], dh_b)
            res = combine_bwd("b_combine_bwd", dy, s["outs"], s["lzs"])
            dos, dlts = res[:n_groups], res[n_groups:]
            pieces, rel_rows = [], []
            for g, (win, dil) in enumerate(B_GROUPS):
                dq, dk, dv, dbias = mixer_b_group_bwd(s["qkv"], tables[g][0], dos[g], s["lzs"][g], dlts[g],
                                                      strides[g], g, n_groups, dil)
                pieces += [dq, dk, dv]
                rel_rows.append(bias_bucket_sums(f"b_bias_sums_d{dil}", dbias, tables[g][1]))
            d_rel = d_rel + jnp.concatenate(rel_rows, axis=0)
            dqkv = jnp.concatenate(pieces, axis=1)
            g_qkv = mm_col_dw("b_qkv_dw", s["hn"], dqkv, w_b)
            dhn = mm_col_dx("b_qkv_dx", dqkv, w_qkv)
        dh, dh_b, d_mix[i] = rms_bwd("mix_norm_bwd", s["h_in"], mix_norm[i], dhn, dh, settle())
        register(i, "mix", [g_qkv, g_o.reshape(N_DEV, -1, d)])
    last = pending.pop()[2] if pending else None

    d_rel_bias = d_rel[:, :REL_BUCKETS].T
    small_g = [jnp.stack(d_qg), jnp.stack(d_kg), d_rel_bias, jnp.concatenate(d_mix, 0), jnp.concatenate(d_ffn, 0),
               jnp.stack(d_cb), d_final.reshape(-1), jnp.stack(d_cw), loss_part]
    return dh, layer_grads, small_g, last


def kernel(x, a_w_qkv, a_w_o, a_q_gain, a_k_gain, b_w_qkv, b_w_o, rel_bias, mix_norm, ffn_norm, w_up, conv_w, conv_b, w_down, final_norm, loss_target, m_a_w_qkv, m_a_w_o, m_a_q_gain, m_a_k_gain, m_b_w_qkv, m_b_w_o, m_rel_bias, m_mix_norm, m_ffn_norm, m_w_up, m_conv_w, m_conv_b, m_w_down, m_final_norm, v_a_w_qkv, v_a_w_o, v_a_q_gain, v_a_k_gain, v_b_w_qkv, v_b_w_o, v_rel_bias, v_mix_norm, v_ffn_norm, v_w_up, v_conv_w, v_conv_b, v_w_down, v_final_norm):
    d = x.shape[2]
    depth = mix_norm.shape[0]
    dff = w_down.shape[1] * N_DEV
    w_u = w_up.shape[2]
    mixers = [(a_w_qkv, a_w_o, m_a_w_qkv, m_a_w_o, v_a_w_qkv, v_a_w_o),
              (b_w_qkv, b_w_o, m_b_w_qkv, m_b_w_o, v_b_w_qkv, v_b_w_o)]

    layers = []
    for i in range(depth):
        w_qkv, w_o = mixers[i % 2][0][i // 2], mixers[i % 2][1][i // 2]
        shards = [w_qkv.astype(BF16), w_o.astype(BF16), w_up[i].astype(BF16), w_down[i].astype(BF16), conv_w[i]]
        if i == 0:
            (g_qkv,) = gather_layer("gather_l0_qkv", shards[:1])
            g_o, g_up, g_down, g_cw = gather_layer("gather_l0", shards[1:])
        else:
            g_qkv, g_o, g_up, g_down, g_cw = gather_layer(f"gather_l{i}", shards)
        cw = g_cw.transpose(1, 0, 2).reshape(3, 2, dff).transpose(1, 0, 2)
        layers.append((g_qkv, g_o.reshape(-1, d), g_up, g_down.reshape(dff, d), cw))

    core = lax.axis_index("c").astype(jnp.int32).reshape(1)
    chip = (2 * lax.axis_index("x") + lax.axis_index("y")).astype(jnp.int32).reshape(1)

    def reduce_and_update(i, part, grads):
        w_qkv, w_o, m_qkv, m_o, v_qkv, v_o = mixers[i % 2]
        state = {"mix": [(w_qkv, m_qkv, v_qkv, i // 2), (w_o, m_o, v_o, i // 2)],
                 "ffn": [(w_up, m_w_up, v_w_up, i), (w_down, m_w_down, v_w_down, i)]}[part]
        got1 = grads_to_sibling(f"to_sibling_l{i}_{part}", grads)
        sums = [chip_sum(f"chip_sum_l{i}_{part}{a}", grads[a], got1[a], core) for a in range(2)]
        got2 = grads_to_chips(f"to_chips_l{i}_{part}", sums)

        def finish():
            return [adamw_layer(f"adamw_l{i}_{part}{a}", sums[a], got2[a], *state[a], chip) for a in range(2)]

        return sums, finish

    dh, updates, small_g, last = _local_step(x[0], loss_target[0], layers,
                                             (a_q_gain, a_k_gain, rel_bias, mix_norm, ffn_norm, conv_b, final_norm),
                                             reduce_and_update)
    grad_x = dh[None]

    width = 2048
    packed = _pack(small_g, N_DEV * width).reshape(-1, N_DEV, width)
    n_rows = packed.shape[0]
    packed = packed.transpose(1, 0, 2).reshape(N_DEV, n_rows * width)
    red = all_reduce_small("small_all_reduce", packed)
    updates[0]["mix"] = last()
    big_out = {}
    for nm, part, a in (("qkv", "mix", 0), ("o", "mix", 1)):
        for par, prefix in enumerate(("a_w_", "b_w_")):
            big_out[prefix + nm] = [jnp.stack([updates[i][part][a][k] for i in range(par, depth, 2)]) for k in range(4)]
    for nm, a in (("w_up", 0), ("w_down", 1)):
        big_out[nm] = [jnp.stack([updates[i]["ffn"][a][k] for i in range(depth)]) for k in range(4)]
    red = red.reshape(N_DEV, n_rows, width).transpose(1, 0, 2)
    (g_qg, g_kg, g_rel, g_mix, g_ffn, g_cb, g_fin, g_cw_all, loss) = _unpack(red, [p.shape for p in small_g])
    idx = 4 * lax.axis_index("x") + 2 * lax.axis_index("y") + lax.axis_index("c")
    g_cw_mine = lax.dynamic_slice_in_dim(g_cw_all, idx * w_u, w_u, axis=2)

    small_w = [a_q_gain, a_k_gain, rel_bias, mix_norm, ffn_norm, conv_b, final_norm, conv_w]
    small_m = [m_a_q_gain, m_a_k_gain, m_rel_bias, m_mix_norm, m_ffn_norm, m_conv_b, m_final_norm, m_conv_w]
    small_v = [v_a_q_gain, v_a_k_gain, v_rel_bias, v_mix_norm, v_ffn_norm, v_conv_b, v_final_norm, v_conv_w]
    small_grads = [g_qg, g_kg, g_rel, g_mix, g_ffn, g_cb, g_fin, g_cw_mine]
    shapes = [w.shape for w in small_w]
    pad_rows = (-_pack(small_w, width).shape[0]) % 8

    def pk8(parts):
        p = _pack(parts, width)
        return jnp.pad(p, ((0, pad_rows), (0, 0))) if pad_rows else p

    sd, sm, sv = adamw_small("adamw_small", pk8(small_grads), pk8(small_w), pk8(small_m), pk8(small_v))
    sd, sm, sv = _unpack(sd, shapes), _unpack(sm, shapes), _unpack(sv, shapes)

    names = ["a_w_qkv", "a_w_o", "a_q_gain", "a_k_gain", "b_w_qkv", "b_w_o", "rel_bias", "mix_norm", "ffn_norm",
             "w_up", "conv_w", "conv_b", "w_down", "final_norm"]
    small_names = ["a_q_gain", "a_k_gain", "rel_bias", "mix_norm", "ffn_norm", "conv_b", "final_norm", "conv_w"]
    grads, deltas, new_m, new_v = {}, {}, {}, {}
    for nm, outs in big_out.items():
        grads[nm], deltas[nm], new_m[nm], new_v[nm] = outs
    for a, nm in enumerate(small_names):
        grads[nm] = small_grads[a].reshape(shapes[a])
        deltas[nm], new_m[nm], new_v[nm] = sd[a], sm[a], sv[a]
    return (loss.reshape(()), grad_x, *[grads[n] for n in names], *[deltas[n] for n in names],
            *[new_m[n] for n in names], *[new_v[n] for n in names])
```

```python
import functools
import math

import jax
import jax.numpy as jnp
from jax import lax
from jax.experimental import pallas as pl
from jax.experimental.pallas import tpu as pltpu
from jax.experimental.pallas import tpu_sc as plsc

F32 = jnp.float32
BF16 = jnp.bfloat16
MESH = pl.DeviceIdType.MESH

N_DEV = 8
LANES = 128
HEAD_DIM = 128
VMEM_LIMIT = 56 * 1024 * 1024
GRID_W = 64
ROPE_THETA = 10000.0
A_KV_HEADS = 4
B_GROUPS = ((128, 1), (512, 4), (2048, 16))
B_HEADS_PER_GROUP = 8
REL_BUCKETS = 32
REL_MAX_DISTANCE = 1024
EPS = 1e-6
NEG_INF = -1e30
ADAM_LR = 0.001
ADAM_B1 = 0.9
ADAM_B2 = 0.999
ADAM_EPS = 1e-08
ADAM_WD = 0.01
ADAM_STEP = 10

ROW_TILE = 256
MM_TM = 1024
MM_TK = 2048
A_BQ = 1024
A_BK = 1024
B_BQ = 256
ATTN_ROWS = 16
ATTN_SCALE = HEAD_DIM ** -0.5

NN = (((1,), (0,)), ((), ()))
NT = (((1,), (1,)), ((), ()))
TN = (((0,), (0,)), ((), ()))


def _tile(n, pref):
    return pref if n % pref == 0 else n


def _div_tile(n, pref):
    for cand in range(pref - pref % LANES, 0, -LANES):
        if n % cand == 0:
            return cand
    return n


def _params(sem):
    return pltpu.CompilerParams(dimension_semantics=sem, vmem_limit_bytes=VMEM_LIMIT)


def _dot(a, b, dims):
    return lax.dot_general(a, b, dims, preferred_element_type=F32)


def _mm(name, a, b, *, grid, a_blk, a_map, b_blk, b_map, o_blk, o_map, out_shape, out_dtype, dims,
        res=None, after=()):
    nk = grid[2]
    acc_shape = tuple(d for d in o_blk if d is not None)

    def body(*refs):
        a_ref, b_ref = refs[:2]
        r_ref = None if res is None else refs[2]
        o_ref, acc = refs[-2:]
        k = pl.program_id(2)

        @pl.when(k == 0)
        def _():
            acc[...] = jnp.zeros_like(acc)

        acc[...] += _dot(a_ref[...].astype(BF16), b_ref[...].astype(BF16), dims)

        @pl.when(k == nk - 1)
        def _():
            r = acc[...]
            if r_ref is not None:
                r = r + r_ref[...]
            o_ref[...] = r.astype(out_dtype)

    in_specs = [pl.BlockSpec(a_blk, a_map), pl.BlockSpec(b_blk, b_map)]
    args = [a, b]
    if res is not None:
        in_specs.append(pl.BlockSpec(o_blk, o_map))
        args.append(res)
    in_specs += [pl.BlockSpec(memory_space=pl.ANY)] * len(after)
    args += list(after)
    return pl.pallas_call(
        body, name=name, grid=grid, in_specs=in_specs, out_specs=pl.BlockSpec(o_blk, o_map),
        out_shape=jax.ShapeDtypeStruct(out_shape, out_dtype),
        scratch_shapes=[pltpu.VMEM(acc_shape, F32)],
        compiler_params=_params(("parallel", "parallel", "arbitrary")),
    )(*args)


def mm_col_fwd(name, a, wg, out_dtype, split=1):
    m, kdim = a.shape
    n_dev, _, w = wg.shape
    tm, tk = _tile(m, MM_TM), _div_tile(kdim, MM_TK)
    per = n_dev // split
    if split == 1:
        o_blk, o_map, o_shape = (tm, w), (lambda i, j, k: (i, j)), (m, n_dev * w)
    else:
        o_blk, o_map, o_shape = (None, tm, w), (lambda i, j, k: (j // per, i, j % per)), (split, m, per * w)
    return _mm(name, a, wg, grid=(m // tm, n_dev, kdim // tk),
               a_blk=(tm, tk), a_map=lambda i, j, k: (i, k),
               b_blk=(None, tk, w), b_map=lambda i, j, k: (j, k, 0),
               o_blk=o_blk, o_map=o_map, out_shape=o_shape, out_dtype=out_dtype, dims=NN)


def mm_col_dx(name, dy, wg, split=1):
    n_dev, kdim, w = wg.shape
    m = dy.shape[-2]
    tm, tk = _tile(m, MM_TM), _div_tile(kdim, MM_TK)
    per = n_dev // split
    if split == 1:
        a_blk, a_map = (tm, w), (lambda i, j, k: (i, k))
    else:
        a_blk, a_map = (None, tm, w), (lambda i, j, k: (k // per, i, k % per))
    return _mm(name, dy, wg, grid=(m // tm, kdim // tk, n_dev),
               a_blk=a_blk, a_map=a_map,
               b_blk=(None, tk, w), b_map=lambda i, j, k: (k, j, 0),
               o_blk=(tm, tk), o_map=lambda i, j, k: (i, j), out_shape=(m, kdim), out_dtype=F32, dims=NT)


def mm_col_dw(name, x, dy, w, split=1):
    m, kdim = x.shape
    tm, tk = _tile(m, MM_TM), _div_tile(kdim, MM_TK)
    per = N_DEV // split
    if split == 1:
        b_blk, b_map = (tm, w), (lambda i, j, k: (k, j))
    else:
        b_blk, b_map = (None, tm, w), (lambda i, j, k: (j // per, k, j % per))
    return _mm(name, x, dy, grid=(kdim // tk, N_DEV, m // tm),
               a_blk=(tm, tk), a_map=lambda i, j, k: (k, i),
               b_blk=b_blk, b_map=b_map,
               o_blk=(None, tk, w), o_map=lambda i, j, k: (j, i, 0),
               out_shape=(N_DEV, kdim, w), out_dtype=BF16, dims=TN)


def mm_row_fwd(name, a, wg, res):
    m, kdim = a.shape
    n = wg.shape[1]
    tm, tk, tn = _tile(m, MM_TM), _div_tile(kdim, MM_TK), _tile(n, 1024)
    return _mm(name, a, wg, grid=(m // tm, n // tn, kdim // tk),
               a_blk=(tm, tk), a_map=lambda i, j, k: (i, k),
               b_blk=(tk, tn), b_map=lambda i, j, k: (k, j),
               o_blk=(tm, tn), o_map=lambda i, j, k: (i, j), out_shape=(m, n), out_dtype=F32, dims=NN,
               res=res)


def mm_row_dx(name, dy, wg, after=()):
    m, n = dy.shape
    kdim = wg.shape[0]
    tm, tk, tn = _tile(m, MM_TM), _div_tile(kdim, MM_TK), _tile(n, 1024)
    return _mm(name, dy, wg, grid=(m // tm, kdim // tk, n // tn),
               a_blk=(tm, tn), a_map=lambda i, j, k: (i, k),
               b_blk=(tk, tn), b_map=lambda i, j, k: (j, k),
               o_blk=(tm, tk), o_map=lambda i, j, k: (i, j), out_shape=(m, kdim), out_dtype=F32, dims=NT,
               after=after)


def mm_row_dw(name, x, dy):
    m, kdim = x.shape
    n = dy.shape[1]
    tm, tk, tn = _tile(m, MM_TM), _div_tile(kdim, MM_TK), _tile(n, 1024)
    return _mm(name, x, dy, grid=(kdim // tk, n // tn, m // tm),
               a_blk=(tm, tk), a_map=lambda i, j, k: (k, i),
               b_blk=(tm, tn), b_map=lambda i, j, k: (k, j),
               o_blk=(tk, tn), o_map=lambda i, j, k: (i, j), out_shape=(kdim, n), out_dtype=BF16, dims=TN)


def _rows(d, tm):
    return pl.BlockSpec((tm, d), lambda i: (i, 0))


def _vec(d):
    return pl.BlockSpec((1, d), lambda i: (0, 0))


def rms_fwd(name, h, gain):
    t, d = h.shape
    tm = _tile(t, ROW_TILE)

    def body(h_ref, g_ref, o_ref):
        x = h_ref[...]
        rstd = lax.rsqrt(jnp.mean(x * x, axis=-1, keepdims=True) + EPS)
        o_ref[...] = (x * rstd * g_ref[...]).astype(BF16)

    return pl.pallas_call(
        body, name=name, grid=(t // tm,), in_specs=[_rows(d, tm), _vec(d)], out_specs=_rows(d, tm),
        out_shape=jax.ShapeDtypeStruct((t, d), BF16), compiler_params=_params(("parallel",)),
    )(h, gain.reshape(1, d))


def rms_bwd(name, h, gain, dy, dres, after=()):
    t, d = h.shape
    tm = _tile(t, ROW_TILE)

    def body(h_ref, g_ref, dy_ref, r_ref, *rest):
        dh_ref, dhb_ref, dg_ref = rest[-3:]

        @pl.when(pl.program_id(0) == 0)
        def _():
            dg_ref[...] = jnp.zeros_like(dg_ref)

        x = h_ref[...]
        rstd = lax.rsqrt(jnp.mean(x * x, axis=-1, keepdims=True) + EPS)
        xhat = x * rstd
        dyv = dy_ref[...]
        dxhat = dyv * g_ref[...]
        dh = r_ref[...] + rstd * (dxhat - xhat * jnp.mean(dxhat * xhat, axis=-1, keepdims=True))
        dh_ref[...] = dh
        dhb_ref[...] = dh.astype(BF16)
        dg_ref[...] += jnp.sum(dyv * xhat, axis=0, keepdims=True)

    return pl.pallas_call(
        body, name=name, grid=(t // tm,),
        in_specs=[_rows(d, tm), _vec(d), _rows(d, tm), _rows(d, tm)]
        + [pl.BlockSpec(memory_space=pl.ANY)] * len(after),
        out_specs=[_rows(d, tm), _rows(d, tm), _vec(d)],
        out_shape=[jax.ShapeDtypeStruct((t, d), F32), jax.ShapeDtypeStruct((t, d), BF16),
                   jax.ShapeDtypeStruct((1, d), F32)],
        compiler_params=_params(("arbitrary",)),
    )(h, gain.reshape(1, d), dy, dres, *after)


def loss_head(name, h, gain, target):
    t, d = h.shape
    tm = _tile(t, ROW_TILE)

    def body(h_ref, g_ref, t_ref, dh_ref, dhb_ref, dg_ref, loss_ref):
        @pl.when(pl.program_id(0) == 0)
        def _():
            dg_ref[...] = jnp.zeros_like(dg_ref)
            loss_ref[...] = jnp.zeros_like(loss_ref)

        x = h_ref[...]
        rstd = lax.rsqrt(jnp.mean(x * x, axis=-1, keepdims=True) + EPS)
        xhat = x * rstd
        err = xhat * g_ref[...] - t_ref[...]
        row = jnp.mean(err * err, axis=-1, keepdims=True)
        loss_ref[...] += 0.5 * jnp.sum(row, axis=0, keepdims=True)
        dyv = err * (1.0 / d)
        dxhat = dyv * g_ref[...]
        dh = rstd * (dxhat - xhat * jnp.mean(dxhat * xhat, axis=-1, keepdims=True))
        dh_ref[...] = dh
        dhb_ref[...] = dh.astype(BF16)
        dg_ref[...] += jnp.sum(dyv * xhat, axis=0, keepdims=True)

    return pl.pallas_call(
        body, name=name, grid=(t // tm,),
        in_specs=[_rows(d, tm), _vec(d), _rows(d, tm)],
        out_specs=[_rows(d, tm), _rows(d, tm), _vec(d), pl.BlockSpec((1, 1), lambda i: (0, 0))],
        out_shape=[jax.ShapeDtypeStruct((t, d), F32), jax.ShapeDtypeStruct((t, d), BF16),
                   jax.ShapeDtypeStruct((1, d), F32), jax.ShapeDtypeStruct((1, 1), F32)],
        compiler_params=_params(("arbitrary",)),
    )(h, gain.reshape(1, d), target)


def rope_tables(seq):
    pos = jnp.arange(seq, dtype=jnp.int32)
    row_ids = (pos // GRID_W).astype(F32)
    col_ids = (pos % GRID_W).astype(F32)
    quarter = HEAD_DIM // 4
    inv_freq = ROPE_THETA ** (-jnp.arange(quarter, dtype=F32) / quarter)
    ar = row_ids[:, None] * inv_freq[None, :]
    ac = col_ids[:, None] * inv_freq[None, :]
    cos = jnp.concatenate([jnp.cos(ar), jnp.cos(ar), jnp.cos(ac), jnp.cos(ac)], axis=-1)
    sin = jnp.concatenate([-jnp.sin(ar), jnp.sin(ar), -jnp.sin(ac), jnp.sin(ac)], axis=-1)
    return cos, sin


def _swap_quarters(x):
    lane = lax.broadcasted_iota(jnp.int32, x.shape, 1)
    q = HEAD_DIM // 4
    return jnp.where((lane % (2 * q)) < q, pltpu.roll(x, HEAD_DIM - q, 1), pltpu.roll(x, q, 1))


def qk_prep_fwd(name, qkv, q_gain, k_gain, cos, sin, n_q, n_kv):
    t, width = qkv.shape
    tm = _tile(t, ROW_TILE)

    def body(x_ref, qg_ref, kg_ref, c_ref, s_ref, o_ref):
        c, s = c_ref[...], s_ref[...]
        for hd in range(n_q + n_kv):
            sl = slice(hd * HEAD_DIM, (hd + 1) * HEAD_DIM)
            x = x_ref[:, sl]
            g = qg_ref[...] if hd < n_q else kg_ref[...]
            xn = x * lax.rsqrt(jnp.mean(x * x, axis=-1, keepdims=True) + EPS) * g
            o_ref[:, sl] = (xn * c + _swap_quarters(xn) * s).astype(BF16)
        vs = slice((n_q + n_kv) * HEAD_DIM, width)
        o_ref[:, vs] = x_ref[:, vs].astype(BF16)

    return pl.pallas_call(
        body, name=name, grid=(t // tm,),
        in_specs=[_rows(width, tm), _vec(HEAD_DIM), _vec(HEAD_DIM), _rows(HEAD_DIM, tm), _rows(HEAD_DIM, tm)],
        out_specs=_rows(width, tm), out_shape=jax.ShapeDtypeStruct((t, width), BF16),
        compiler_params=_params(("parallel",)),
    )(qkv, q_gain.reshape(1, HEAD_DIM), k_gain.reshape(1, HEAD_DIM), cos, sin)


def qk_prep_bwd(name, qkv, dq, dk, dv, q_gain, k_gain, cos, sin, n_q, n_kv):
    t, width = qkv.shape
    tm = _tile(t, ROW_TILE)

    def body(x_ref, dq_ref, dk_ref, dv_ref, qg_ref, kg_ref, c_ref, s_ref, o_ref, dg_ref):
        @pl.when(pl.program_id(0) == 0)
        def _():
            dg_ref[...] = jnp.zeros_like(dg_ref)

        c, s = c_ref[...], s_ref[...]
        dgq = jnp.zeros((1, HEAD_DIM), F32)
        dgk = jnp.zeros((1, HEAD_DIM), F32)
        for hd in range(n_q + n_kv):
            sl = slice(hd * HEAD_DIM, (hd + 1) * HEAD_DIM)
            x = x_ref[:, sl]
            if hd < n_q:
                g, dout = qg_ref[...], dq_ref[:, sl]
            else:
                ks = slice((hd - n_q) * HEAD_DIM, (hd - n_q + 1) * HEAD_DIM)
                g, dout = kg_ref[...], dk_ref[:, ks]
            rstd = lax.rsqrt(jnp.mean(x * x, axis=-1, keepdims=True) + EPS)
            xhat = x * rstd
            dxn = dout * c + _swap_quarters(dout * s)
            part = jnp.sum(dxn * xhat, axis=0, keepdims=True)
            if hd < n_q:
                dgq = dgq + part
            else:
                dgk = dgk + part
            dxhat = dxn * g
            o_ref[:, sl] = (rstd * (dxhat - xhat * jnp.mean(dxhat * xhat, axis=-1, keepdims=True))).astype(BF16)
        o_ref[:, slice((n_q + n_kv) * HEAD_DIM, width)] = dv_ref[...].astype(BF16)
        dg_ref[0:1, :] += dgq
        dg_ref[1:2, :] += dgk

    kvw = n_kv * HEAD_DIM
    return pl.pallas_call(
        body, name=name, grid=(t // tm,),
        in_specs=[_rows(width, tm), _rows(n_q * HEAD_DIM, tm), _rows(kvw, tm), _rows(kvw, tm),
                  _vec(HEAD_DIM), _vec(HEAD_DIM), _rows(HEAD_DIM, tm), _rows(HEAD_DIM, tm)],
        out_specs=[_rows(width, tm), pl.BlockSpec((2, HEAD_DIM), lambda i: (0, 0))],
        out_shape=[jax.ShapeDtypeStruct((t, width), BF16), jax.ShapeDtypeStruct((2, HEAD_DIM), F32)],
        compiler_params=_params(("arbitrary",)),
    )(qkv, dq, dk, dv, q_gain.reshape(1, HEAD_DIM), k_gain.reshape(1, HEAD_DIM), cos, sin)


def _lanes(x, width):
    return jnp.tile(x, (1, width // LANES))


def _hs(hd):
    return slice(hd * HEAD_DIM, (hd + 1) * HEAD_DIM)


def attn_fwd(name, q, k, v, bias, *, grid, q_spec, k_spec, v_spec, b_spec, o_spec, valid, nh, shared_kv,
             bq, bk, o_shape, o_dtype):
    ns = grid[2]

    def body(*refs):
        if bias is None:
            q_ref, k_ref, v_ref, o_ref, lse_ref, m_s, l_s, acc_s = refs
            b_ref = None
        else:
            q_ref, k_ref, v_ref, b_ref, o_ref, lse_ref, m_s, l_s, acc_s = refs
        step = pl.program_id(2)

        @pl.when(step == 0)
        def _():
            m_s[...] = jnp.full_like(m_s, -jnp.inf)
            l_s[...] = jnp.zeros_like(l_s)
            acc_s[...] = jnp.zeros_like(acc_s)

        @pl.when(valid(pl.program_id(1), step))
        def _():
            for hd in range(nh):
                kh = _hs(0 if shared_kv else hd)
                s = _dot(q_ref[:, _hs(hd)], k_ref[:, kh], NT)
                p_rows, a_rows = [], []
                for r0 in range(0, bq, ATTN_ROWS):
                    rows = slice(r0, r0 + ATTN_ROWS)
                    z = s[rows] * ATTN_SCALE
                    if b_ref is not None:
                        z = z + b_ref[hd, rows, :]
                    m_prev = m_s[hd, rows, :]
                    m_new = jnp.maximum(m_prev, jnp.max(z, axis=-1, keepdims=True))
                    alpha = jnp.exp(m_prev - m_new)
                    p = jnp.exp(z - _lanes(m_new, bk))
                    l_s[hd, rows, :] = alpha * l_s[hd, rows, :] + jnp.sum(p, axis=-1, keepdims=True)
                    m_s[hd, rows, :] = m_new
                    p_rows.append(p.astype(BF16))
                    a_rows.append(alpha)
                pv = _dot(jnp.concatenate(p_rows, axis=0), v_ref[:, kh], NN)
                acc_s[hd] = jnp.concatenate(a_rows, axis=0) * acc_s[hd] + pv

        @pl.when(step == ns - 1)
        def _():
            for hd in range(nh):
                o_ref[:, _hs(hd)] = (acc_s[hd] / l_s[hd]).astype(o_dtype)
                lse_ref[:, _hs(hd)] = m_s[hd] + jnp.log(l_s[hd])

    in_specs = [q_spec, k_spec, v_spec] + ([] if bias is None else [b_spec])
    args = [q, k, v] + ([] if bias is None else [bias])
    stat = pltpu.VMEM((nh, bq, LANES), F32)
    return pl.pallas_call(
        body, name=name, grid=grid, in_specs=in_specs, out_specs=[o_spec, o_spec],
        out_shape=[jax.ShapeDtypeStruct(o_shape, o_dtype), jax.ShapeDtypeStruct(o_shape, F32)],
        scratch_shapes=[stat, stat, stat],
        compiler_params=_params(("parallel", "parallel", "arbitrary")),
    )(*args)


def _probs(q_ref, k_ref, v_ref, do_ref, lse_ref, dlt_ref, b_ref, hd, kh, bq, bk, want_p=True, on_ds=None):
    s = _dot(q_ref[:, _hs(hd)], k_ref[:, kh], NT)
    dp = _dot(do_ref[:, _hs(hd)], v_ref[:, kh], NT)
    p_rows, ds_rows = [], []
    for r0 in range(0, bq, ATTN_ROWS):
        rows = slice(r0, r0 + ATTN_ROWS)
        z = s[rows] * ATTN_SCALE
        if b_ref is not None:
            z = z + b_ref[hd, rows, :]
        p = jnp.exp(z - _lanes(lse_ref[rows, _hs(hd)], bk))
        ds = p * (dp[rows] - _lanes(dlt_ref[rows, _hs(hd)], bk))
        if on_ds is not None:
            on_ds(rows, ds)
        if want_p:
            p_rows.append(p.astype(BF16))
        ds_rows.append(ds.astype(BF16))
    return (jnp.concatenate(p_rows, axis=0) if want_p else None), jnp.concatenate(ds_rows, axis=0)


def attn_bwd_dq(name, q, k, v, do, lse, dlt, *, grid, q_spec, k_spec, v_spec, nh, bq, bk, o_shape):
    ns = grid[2]
    scale = HEAD_DIM ** -0.5

    def body(q_ref, k_ref, v_ref, do_ref, lse_ref, dlt_ref, dq_ref, acc_s):
        step = pl.program_id(2)

        @pl.when(step == 0)
        def _():
            acc_s[...] = jnp.zeros_like(acc_s)

        for hd in range(nh):
            _, ds = _probs(q_ref, k_ref, v_ref, do_ref, lse_ref, dlt_ref, None, hd, _hs(0), bq, bk, want_p=False)
            acc_s[hd] += _dot(ds, k_ref[:, _hs(0)], NN)

        @pl.when(step == ns - 1)
        def _():
            for hd in range(nh):
                dq_ref[:, _hs(hd)] = acc_s[hd] * scale

    return pl.pallas_call(
        body, name=name, grid=grid, in_specs=[q_spec, k_spec, v_spec, q_spec, q_spec, q_spec],
        out_specs=q_spec, out_shape=jax.ShapeDtypeStruct(o_shape, F32),
        scratch_shapes=[pltpu.VMEM((nh, bq, LANES), F32)],
        compiler_params=_params(("parallel", "parallel", "arbitrary")),
    )(q, k, v, do, lse, dlt)


def _always(i, s):
    return s >= 0


def row_delta(name, do, o, n_heads):
    t, width = do.shape
    tm = _tile(t, ROW_TILE)

    def body(do_ref, o_ref, dl_ref, dob_ref):
        for hd in range(n_heads):
            d = do_ref[:, _hs(hd)]
            s = jnp.sum(d * o_ref[:, _hs(hd)].astype(F32), axis=-1, keepdims=True)
            dl_ref[:, _hs(hd)] = jnp.broadcast_to(s, (tm, HEAD_DIM))
            dob_ref[:, _hs(hd)] = d.astype(BF16)

    return pl.pallas_call(
        body, name=name, grid=(t // tm,), in_specs=[_rows(width, tm), _rows(width, tm)],
        out_specs=[_rows(width, tm), _rows(width, tm)],
        out_shape=[jax.ShapeDtypeStruct((t, width), F32), jax.ShapeDtypeStruct((t, width), BF16)],
        compiler_params=_params(("parallel",)),
    )(do, o)


def _a_specs(n_q, n_kv, bq, bk, q_major):
    grp = n_q // n_kv
    if q_major:
        qm, km = (lambda b, i, s: (i, b)), (lambda b, i, s: (s, n_q + b))
        vm = lambda b, i, s: (s, n_q + n_kv + b)
    else:
        qm, km = (lambda b, i, s: (s, b)), (lambda b, i, s: (i, n_q + b))
        vm = lambda b, i, s: (i, n_q + n_kv + b)
    return (pl.BlockSpec((bq, grp * HEAD_DIM), qm), pl.BlockSpec((bk, HEAD_DIM), km),
            pl.BlockSpec((bk, HEAD_DIM), vm))


def mixer_a_fwd(qkv_r, n_q, n_kv):
    t = qkv_r.shape[0]
    bq, bk = _tile(t, A_BQ), _tile(t, A_BK)
    q_spec, k_spec, v_spec = _a_specs(n_q, n_kv, bq, bk, True)
    return attn_fwd("a_attn_fwd", qkv_r, qkv_r, qkv_r, None, grid=(n_kv, t // bq, t // bk),
                    q_spec=q_spec, k_spec=k_spec, v_spec=v_spec, b_spec=None, o_spec=q_spec, valid=_always,
                    nh=n_q // n_kv, shared_kv=True, bq=bq, bk=bk, o_shape=(t, n_q * HEAD_DIM), o_dtype=BF16)


def mixer_a_bwd(qkv_r, do_b, lse, dlt, n_q, n_kv):
    t = qkv_r.shape[0]
    bq, bk = _tile(t, A_BQ), _tile(t, A_BK)
    grp = n_q // n_kv
    q_spec, k_spec, v_spec = _a_specs(n_q, n_kv, bq, bk, True)
    dq = attn_bwd_dq("a_attn_dq", qkv_r, qkv_r, qkv_r, do_b, lse, dlt, grid=(n_kv, t // bq, t // bk),
                     q_spec=q_spec, k_spec=k_spec, v_spec=v_spec, nh=grp, bq=bq, bk=bk,
                     o_shape=(t, n_q * HEAD_DIM))
    q_spec, k_spec, v_spec = _a_specs(n_q, n_kv, bq, bk, False)
    o_spec = pl.BlockSpec((bk, HEAD_DIM), lambda b, i, s: (i, b))
    dk, dv = _attn_bwd_dkv_out(qkv_r, do_b, lse, dlt, grid=(n_kv, t // bk, t // bq), q_spec=q_spec,
                               k_spec=k_spec, v_spec=v_spec, o_spec=o_spec, grp=grp, bq=bq, bk=bk,
                               o_shape=(t, n_kv * HEAD_DIM))
    return dq, dk, dv


def _attn_bwd_dkv_out(qkv_r, do_b, lse, dlt, *, grid, q_spec, k_spec, v_spec, o_spec, grp, bq, bk, o_shape):
    ns = grid[2]
    scale = HEAD_DIM ** -0.5

    def body(q_ref, k_ref, v_ref, do_ref, lse_ref, dlt_ref, dk_ref, dv_ref, dk_s, dv_s):
        step = pl.program_id(2)

        @pl.when(step == 0)
        def _():
            dk_s[...] = jnp.zeros_like(dk_s)
            dv_s[...] = jnp.zeros_like(dv_s)

        for hd in range(grp):
            p, ds = _probs(q_ref, k_ref, v_ref, do_ref, lse_ref, dlt_ref, None, hd, _hs(0), bq, bk)
            dv_s[...] += _dot(p, do_ref[:, _hs(hd)], TN)
            dk_s[...] += _dot(ds, q_ref[:, _hs(hd)], TN)

        @pl.when(step == ns - 1)
        def _():
            dk_ref[...] = dk_s[...] * scale
            dv_ref[...] = dv_s[...]

    acc = pltpu.VMEM((bk, HEAD_DIM), F32)
    return pl.pallas_call(
        body, name="a_attn_dkv", grid=grid, in_specs=[q_spec, k_spec, v_spec, q_spec, q_spec, q_spec],
        out_specs=[o_spec, o_spec], out_shape=[jax.ShapeDtypeStruct(o_shape, F32)] * 2,
        scratch_shapes=[acc, acc], compiler_params=_params(("parallel", "parallel", "arbitrary")),
    )(qkv_r, qkv_r, qkv_r, do_b, lse, dlt)


def t5_bucket(rel):
    nb = REL_BUCKETS // 2
    max_exact = nb // 2
    base = jnp.where(rel > 0, nb, 0)
    n = jnp.abs(rel)
    nf = jnp.maximum(n, 1).astype(F32)
    large = max_exact + (jnp.log(nf / max_exact) / math.log(REL_MAX_DISTANCE / max_exact)
                         * (nb - max_exact)).astype(jnp.int32)
    large = jnp.minimum(large, nb - 1)
    return base + jnp.where(n < max_exact, n, large)


def band_stride(t, win, dil):
    return 1 if t % B_BQ == 0 and win // 2 <= B_BQ else dil


def band_tables(rel_bias_g, win, dil, stride, bq):
    a = jnp.arange(bq)[:, None]
    b = jnp.arange(bq)[None, :]
    rel = jnp.stack([(s - 1) * bq + b - a for s in range(3)]) * stride
    ok = (jnp.abs(rel) <= win // 2) & (rel % dil == 0)
    bucket = t5_bucket(rel)
    bias = jnp.zeros((rel_bias_g.shape[1],) + rel.shape, F32)
    for r in range(REL_BUCKETS):
        bias = bias + jnp.where(bucket[None] == r, rel_bias_g[r][:, None, None, None], 0.0)
    return jnp.where(ok[None], bias, NEG_INF), jnp.where(ok, bucket, -1).astype(jnp.int32)


def band_block(t, stride):
    return _tile(t // stride, B_BQ)


def _b_geometry(t, dil, g, n_groups, bq):
    hg = B_HEADS_PER_GROUP
    length = t // dil
    nblk = length // bq
    gw = hg * HEAD_DIM
    per_tok = 3 * n_groups
    return hg, length, bq, nblk, gw, per_tok


def mixer_b_group_fwd(qkv, bias, dil, g, n_groups, tag):
    t = qkv.shape[0]
    hg, length, bq, nblk, gw, per_tok = _b_geometry(t, dil, g, n_groups, bias.shape[2])
    view = qkv.reshape(length, dil * qkv.shape[1])
    col = lambda c, which: c * per_tok + 3 * g + which
    kblk = lambda i, s: jnp.clip(i - 1 + s, 0, nblk - 1)
    spec = lambda which, streamed: pl.BlockSpec(
        (bq, gw), (lambda c, i, s: (kblk(i, s), col(c, which))) if streamed else (lambda c, i, s: (i, col(c, which))))
    valid = lambda i, s: (i - 1 + s >= 0) & (i - 1 + s < nblk)
    o, lz = attn_fwd(f"b_attn_fwd_d{tag}", view, view, view, bias, grid=(dil, nblk, 3),
                     q_spec=spec(0, False), k_spec=spec(1, True), v_spec=spec(2, True),
                     b_spec=pl.BlockSpec((hg, None, bq, bq), lambda c, i, s: (0, s, 0, 0)),
                     o_spec=pl.BlockSpec((bq, gw), lambda c, i, s: (i, c)), valid=valid, nh=hg,
                     shared_kv=False, bq=bq, bk=bq, o_shape=(length, dil * gw), o_dtype=F32)
    return o.reshape(t, gw), lz.reshape(t, gw)


def mixer_b_group_bwd(qkv, bias, do_g, lz_g, dlt_g, dil, g, n_groups, tag):
    t = qkv.shape[0]
    hg, length, bq, nblk, gw, per_tok = _b_geometry(t, dil, g, n_groups, bias.shape[2])
    view = qkv.reshape(length, dil * qkv.shape[1])
    dov, lzv, dlv = (x.reshape(length, dil * gw) for x in (do_g, lz_g, dlt_g))
    col = lambda c, which: c * per_tok + 3 * g + which
    nbr = lambda i, s: jnp.clip(i - 1 + s, 0, nblk - 1)
    valid = lambda i, s: (i - 1 + s >= 0) & (i - 1 + s < nblk)
    q_spec = pl.BlockSpec((bq, gw), lambda c, i, s: (i, col(c, 0)))
    k_spec = pl.BlockSpec((bq, gw), lambda c, i, s: (nbr(i, s), col(c, 1)))
    v_spec = pl.BlockSpec((bq, gw), lambda c, i, s: (nbr(i, s), col(c, 2)))
    stat = pl.BlockSpec((bq, gw), lambda c, i, s: (i, c))
    dq, dbias = _band_bwd_dq(f"b_attn_dq_d{tag}", view, dov, lzv, dlv, bias, grid=(dil, nblk, 3),
                             q_spec=q_spec, k_spec=k_spec, v_spec=v_spec, stat_spec=stat,
                             b_spec=pl.BlockSpec((hg, None, bq, bq), lambda c, i, s: (0, s, 0, 0)),
                             valid=valid, nh=hg, bq=bq, o_shape=(length, dil * gw))
    q_spec = pl.BlockSpec((bq, gw), lambda c, i, s: (nbr(i, s), col(c, 0)))
    k_spec = pl.BlockSpec((bq, gw), lambda c, i, s: (i, col(c, 1)))
    v_spec = pl.BlockSpec((bq, gw), lambda c, i, s: (i, col(c, 2)))
    stat = pl.BlockSpec((bq, gw), lambda c, i, s: (nbr(i, s), c))
    dk, dv = _band_bwd_dkv(f"b_attn_dkv_d{tag}", view, dov, lzv, dlv, bias, grid=(dil, nblk, 3),
                           q_spec=q_spec, k_spec=k_spec, v_spec=v_spec, stat_spec=stat,
                           b_spec=pl.BlockSpec((hg, None, bq, bq), lambda c, i, s: (0, 2 - s, 0, 0)),
                           o_spec=pl.BlockSpec((bq, gw), lambda c, i, s: (i, c)),
                           valid=valid, nh=hg, bq=bq, o_shape=(length, dil * gw))
    return dq.reshape(t, gw), dk.reshape(t, gw), dv.reshape(t, gw), dbias


def _band_bwd_dq(name, view, do, lse, dlt, bias, *, grid, q_spec, k_spec, v_spec, stat_spec, b_spec, valid,
                 nh, bq, o_shape):
    scale = HEAD_DIM ** -0.5
    bias_shape = (nh, 3, bq, bq)

    def body(q_ref, k_ref, v_ref, do_ref, lse_ref, dlt_ref, b_ref, dq_ref, db_ref, acc_s):
        step = pl.program_id(2)

        @pl.when((pl.program_id(0) == 0) & (pl.program_id(1) == 0) & (step == 0))
        def _():
            db_ref[...] = jnp.zeros_like(db_ref)

        @pl.when(step == 0)
        def _():
            acc_s[...] = jnp.zeros_like(acc_s)

        @pl.when(valid(pl.program_id(1), step))
        def _():
            for hd in range(nh):
                def add_bias_grad(rows, ds, hd=hd):
                    db_ref[hd, step, rows, :] += ds

                _, ds = _probs(q_ref, k_ref, v_ref, do_ref, lse_ref, dlt_ref, b_ref, hd, _hs(hd), bq, bq,
                               want_p=False, on_ds=add_bias_grad)
                acc_s[hd] += _dot(ds, k_ref[:, _hs(hd)], NN)

        @pl.when(step == 2)
        def _():
            for hd in range(nh):
                dq_ref[:, _hs(hd)] = (acc_s[hd] * scale).astype(BF16)

    return pl.pallas_call(
        body, name=name, grid=grid,
        in_specs=[q_spec, k_spec, v_spec, stat_spec, stat_spec, stat_spec, b_spec],
        out_specs=[stat_spec, pl.BlockSpec(bias_shape, lambda c, i, s: (0, 0, 0, 0))],
        out_shape=[jax.ShapeDtypeStruct(o_shape, BF16), jax.ShapeDtypeStruct(bias_shape, F32)],
        scratch_shapes=[pltpu.VMEM((nh, bq, LANES), F32)], compiler_params=_params(("arbitrary",) * 3),
    )(view, view, view, do, lse, dlt, bias)


def _band_bwd_dkv(name, view, do, lse, dlt, bias, *, grid, q_spec, k_spec, v_spec, stat_spec, b_spec, o_spec,
                  valid, nh, bq, o_shape):
    scale = HEAD_DIM ** -0.5

    def body(q_ref, k_ref, v_ref, do_ref, lse_ref, dlt_ref, b_ref, dk_ref, dv_ref, dk_s, dv_s):
        step = pl.program_id(2)

        @pl.when(step == 0)
        def _():
            dk_s[...] = jnp.zeros_like(dk_s)
            dv_s[...] = jnp.zeros_like(dv_s)

        @pl.when(valid(pl.program_id(1), step))
        def _():
            for hd in range(nh):
                p, ds = _probs(q_ref, k_ref, v_ref, do_ref, lse_ref, dlt_ref, b_ref, hd, _hs(hd), bq, bq)
                dv_s[hd] += _dot(p, do_ref[:, _hs(hd)], TN)
                dk_s[hd] += _dot(ds, q_ref[:, _hs(hd)], TN)

        @pl.when(step == 2)
        def _():
            for hd in range(nh):
                dk_ref[:, _hs(hd)] = (dk_s[hd] * scale).astype(BF16)
                dv_ref[:, _hs(hd)] = dv_s[hd].astype(BF16)

    acc = pltpu.VMEM((nh, bq, LANES), F32)
    return pl.pallas_call(
        body, name=name, grid=grid,
        in_specs=[q_spec, k_spec, v_spec, stat_spec, stat_spec, stat_spec, b_spec],
        out_specs=[o_spec, o_spec], out_shape=[jax.ShapeDtypeStruct(o_shape, BF16)] * 2,
        scratch_shapes=[acc, acc], compiler_params=_params(("parallel", "parallel", "arbitrary")),
    )(view, view, view, do, lse, dlt, bias)


def bias_bucket_sums(name, dbias, bucket):
    nh, _, bq, _ = dbias.shape
    db2 = dbias.reshape(nh, 3 * bq, bq)
    bk2 = bucket.reshape(3 * bq, bq)

    def body(db_ref, bk_ref, o_ref):
        row = lax.broadcasted_iota(jnp.int32, (nh, LANES), 0)
        lane = lax.broadcasted_iota(jnp.int32, (nh, LANES), 1)
        out = jnp.zeros((nh, LANES), F32)
        bkt = bk_ref[...]
        for hd in range(nh):
            x = db_ref[hd]
            for r in range(REL_BUCKETS):
                part = jnp.sum(jnp.where(bkt == r, x, 0.0), axis=1, keepdims=True)
                tot = jnp.sum(part, axis=0, keepdims=True)
                out = out + jnp.where((row == hd) & (lane == r), tot, 0.0)
        o_ref[...] = out

    return pl.pallas_call(
        body, name=name, out_shape=jax.ShapeDtypeStruct((nh, LANES), F32),
        compiler_params=pltpu.CompilerParams(vmem_limit_bytes=VMEM_LIMIT),
    )(db2, bk2)


def combine_fwd(name, outs, lzs):
    n_g = len(outs)
    t, gw = outs[0].shape
    tm = _tile(t, ROW_TILE)

    def body(*refs):
        o_refs, lz_refs, y_ref = refs[:n_g], refs[n_g:2 * n_g], refs[2 * n_g]
        lz = [r[...] for r in lz_refs]
        mx = functools.reduce(jnp.maximum, lz)
        e = [jnp.exp(x - mx) for x in lz]
        den = functools.reduce(lambda a, b: a + b, e)
        for g in range(n_g):
            y_ref[:, g * gw:(g + 1) * gw] = (e[g] / den * o_refs[g][...]).astype(BF16)

    return pl.pallas_call(
        body, name=name, grid=(t // tm,), in_specs=[_rows(gw, tm)] * (2 * n_g), out_specs=_rows(n_g * gw, tm),
        out_shape=jax.ShapeDtypeStruct((t, n_g * gw), BF16), compiler_params=_params(("parallel",)),
    )(*outs, *lzs)


def combine_bwd(name, dy, outs, lzs):
    n_g = len(outs)
    t, gw = outs[0].shape
    tm = _tile(t, ROW_TILE)
    nh = gw // HEAD_DIM

    def body(*refs):
        dy_ref = refs[0]
        o_refs, lz_refs = refs[1:1 + n_g], refs[1 + n_g:1 + 2 * n_g]
        do_refs, dl_refs = refs[1 + 2 * n_g:1 + 3 * n_g], refs[1 + 3 * n_g:]
        lz = [r[...] for r in lz_refs]
        mx = functools.reduce(jnp.maximum, lz)
        e = [jnp.exp(x - mx) for x in lz]
        den = functools.reduce(lambda a, b: a + b, e)
        wts = [x / den for x in e]
        for g in range(n_g):
            do_refs[g][...] = (wts[g] * dy_ref[:, g * gw:(g + 1) * gw]).astype(BF16)
        for hd in range(nh):
            mix = jnp.zeros((tm, HEAD_DIM), F32)
            for g in range(n_g):
                prod = dy_ref[:, g * gw + hd * HEAD_DIM:g * gw + (hd + 1) * HEAD_DIM] * o_refs[g][:, _hs(hd)]
                dw = jnp.broadcast_to(jnp.sum(prod, axis=-1, keepdims=True), (tm, HEAD_DIM))
                mix = mix + wts[g][:, _hs(hd)] * dw
            for g in range(n_g):
                dl_refs[g][:, _hs(hd)] = wts[g][:, _hs(hd)] * mix

    return pl.pallas_call(
        body, name=name, grid=(t // tm,),
        in_specs=[_rows(n_g * gw, tm)] + [_rows(gw, tm)] * (2 * n_g),
        out_specs=[_rows(gw, tm)] * (2 * n_g),
        out_shape=[jax.ShapeDtypeStruct((t, gw), BF16)] * n_g + [jax.ShapeDtypeStruct((t, gw), F32)] * n_g,
        compiler_params=_params(("parallel",)),
    )(dy, *outs, *lzs)


def _shifted(u):
    t = u.shape[0]
    row = lax.broadcasted_iota(jnp.int32, u.shape, 0)
    prev = jnp.where(row == 0, 0.0, pltpu.roll(u, 1, 0))
    nxt = jnp.where(row == t - 1, 0.0, pltpu.roll(u, t - 1, 0))
    return prev, nxt


def _conv3(u, prev, nxt, w_ref, b):
    return w_ref[0:1, :] * prev + w_ref[1:2, :] * u + w_ref[2:3, :] * nxt + b


def _conv3_t(d, w_ref):
    prev, nxt = _shifted(d)
    return w_ref[0:1, :] * nxt + w_ref[1:2, :] * d + w_ref[2:3, :] * prev


def conv_act_fwd(name, u2, cw2, cb2):
    _, t, dff = u2.shape
    tn = LANES

    def body(u_ref, w_ref, b_ref, o_ref):
        ug, uv = u_ref[0], u_ref[1]
        cg = _conv3(ug, *_shifted(ug), w_ref.at[0], b_ref[0])
        cv = _conv3(uv, *_shifted(uv), w_ref.at[1], b_ref[1])
        o_ref[...] = (cg * jax.nn.sigmoid(cg) * cv).astype(BF16)

    return pl.pallas_call(
        body, name=name, grid=(dff // tn,),
        in_specs=[pl.BlockSpec((2, t, tn), lambda j: (0, 0, j)), pl.BlockSpec((2, 3, tn), lambda j: (0, 0, j)),
                  pl.BlockSpec((2, 1, tn), lambda j: (0, 0, j))],
        out_specs=pl.BlockSpec((t, tn), lambda j: (0, j)), out_shape=jax.ShapeDtypeStruct((t, dff), BF16),
        compiler_params=_params(("parallel",)),
    )(u2, cw2, cb2)


def conv_act_bwd(name, u2, cw2, cb2, dact, after=()):
    _, t, dff = u2.shape
    tn = LANES

    def body(u_ref, w_ref, b_ref, d_ref, *rest):
        du_ref, dw_ref = rest[-2:]
        d = d_ref[...]
        ug, uv = u_ref[0], u_ref[1]
        shifted = (_shifted(ug), _shifted(uv))
        cg = _conv3(ug, *shifted[0], w_ref.at[0], b_ref[0])
        cv = _conv3(uv, *shifted[1], w_ref.at[1], b_ref[1])
        sg = jax.nn.sigmoid(cg)
        dcv = d * (cg * sg)
        dcg = d * cv * (sg * (1.0 + cg * (1.0 - sg)))
        du_ref[0] = _conv3_t(dcg, w_ref.at[0]).astype(BF16)
        du_ref[1] = _conv3_t(dcv, w_ref.at[1]).astype(BF16)
        for half, (dc, u) in enumerate(((dcg, ug), (dcv, uv))):
            prev, nxt = shifted[half]
            for tap, x in enumerate((prev, u, nxt)):
                dw_ref[half, tap:tap + 1, :] = jnp.sum(dc * x, axis=0, keepdims=True)
            dw_ref[half, 3:4, :] = jnp.sum(dc, axis=0, keepdims=True)
            dw_ref[half, 4:8, :] = jnp.zeros((4, tn), F32)

    return pl.pallas_call(
        body, name=name, grid=(dff // tn,),
        in_specs=[pl.BlockSpec((2, t, tn), lambda j: (0, 0, j)), pl.BlockSpec((2, 3, tn), lambda j: (0, 0, j)),
                  pl.BlockSpec((2, 1, tn), lambda j: (0, 0, j)), pl.BlockSpec((t, tn), lambda j: (0, j))]
        + [pl.BlockSpec(memory_space=pl.ANY)] * len(after),
        out_specs=[pl.BlockSpec((2, t, tn), lambda j: (0, 0, j)), pl.BlockSpec((2, 8, tn), lambda j: (0, 0, j))],
        out_shape=[jax.ShapeDtypeStruct((2, t, dff), BF16), jax.ShapeDtypeStruct((2, 8, dff), F32)],
        compiler_params=_params(("parallel",)),
    )(u2, cw2, cb2, dact, *after)


GATHER_ID, SIBLING_ID, CHIPS_ID = 0, 1, 2


def _place():
    x, y, c = lax.axis_index("x"), lax.axis_index("y"), lax.axis_index("c")
    chips = [(1 - x, y), (x, 1 - y), (1 - x, 1 - y)]
    return x, y, c, chips


def _handshake(peers):
    barrier = pltpu.get_barrier_semaphore()
    for peer in peers:
        pl.semaphore_signal(barrier, inc=1, device_id=peer, device_id_type=MESH)
    pl.semaphore_wait(barrier, len(peers))


def _sequencer(name, body, out_type, scratch_types, collective_id):
    return pl.kernel(body, out_type=out_type, mesh=plsc.ScalarSubcoreMesh(axis_name="seq", num_cores=1),
                     scratch_types=scratch_types, name=name,
                     compiler_params=pltpu.CompilerParams(collective_id=collective_id))


def _gather_body(n):
    def body(*refs):
        src, out = refs[:n], refs[n:2 * n]
        send, recv, loc = refs[2 * n:]
        x, y, c, chips = _place()
        sibling = (x, y, 1 - c)
        _handshake([sibling] + [(*chip, c) for chip in chips])

        def slot(a, px, py, pc):
            return out[a].at[4 * px + 2 * py + pc]

        def copy(a, k, block, to, from_src=False):
            return pltpu.make_async_remote_copy(
                src_ref=src[a] if from_src else slot(a, *block), dst_ref=slot(a, *block),
                send_sem=send.at[a, k], recv_sem=recv.at[a, k], device_id=to, device_id_type=MESH)

        mine = [pltpu.make_async_copy(src[a], slot(a, x, y, c), loc.at[a]) for a in range(n)]
        for cp in mine:
            cp.start()
        first = []
        for a in range(n):
            first.append(copy(a, 0, (x, y, c), sibling, True))
            first += [copy(a, 1 + j, (x, y, c), (*chip, c), True) for j, chip in enumerate(chips)]
        for cp in first:
            cp.start()
        passed = []
        for j, chip in enumerate(chips):
            for a in range(n):
                copy(a, 1 + j, (*chip, c), (x, y, c)).wait_recv()
                cp = copy(a, 4 + j, (*chip, c), sibling)
                cp.start()
                passed.append(cp)
        for a in range(n):
            copy(a, 0, sibling, (x, y, c)).wait_recv()
            for j, chip in enumerate(chips):
                copy(a, 4 + j, (*chip, 1 - c), (x, y, c)).wait_recv()
        for cp in first + passed:
            cp.wait_send()
        for cp in mine:
            cp.wait()

    return body


def gather_layer(name, shards):
    n = len(shards)
    out_type = [jax.ShapeDtypeStruct((N_DEV,) + s.shape, s.dtype) for s in shards]
    scratch = [pltpu.SemaphoreType.DMA((n, 7)), pltpu.SemaphoreType.DMA((n, 7)), pltpu.SemaphoreType.DMA((n,))]
    return _sequencer(name, _gather_body(n), out_type, scratch, GATHER_ID)(*shards)


def _to_sibling_body(n):
    def body(*refs):
        src, got = refs[:n], refs[n:2 * n]
        send, recv = refs[2 * n:]
        x, y, c, _ = _place()
        sibling = (x, y, 1 - c)
        _handshake([sibling])
        remote = []
        for a in range(n):
            for q in range(4):
                remote.append(pltpu.make_async_remote_copy(
                    src_ref=src[a].at[2 * q + 1 - c], dst_ref=got[a].at[q], send_sem=send.at[a, q],
                    recv_sem=recv.at[a, q], device_id=sibling, device_id_type=MESH))
        for cp in remote:
            cp.start()
        for cp in remote:
            cp.wait()

    return body


def grads_to_sibling(name, grads):
    n = len(grads)
    out_type = [jax.ShapeDtypeStruct((4,) + g.shape[1:], g.dtype) for g in grads]
    scratch = [pltpu.SemaphoreType.DMA((n, 4)), pltpu.SemaphoreType.DMA((n, 4))]
    return _sequencer(name, _to_sibling_body(n), out_type, scratch, SIBLING_ID)(*grads)


def _to_chips_body(n):
    def body(*refs):
        src, got = refs[:n], refs[n:2 * n]
        send, recv = refs[2 * n:]
        x, y, c, chips = _place()
        _handshake([(*chip, c) for chip in chips])
        remote = []
        for a in range(n):
            for j, (px, py) in enumerate(chips):
                remote.append(pltpu.make_async_remote_copy(
                    src_ref=src[a].at[2 * px + py], dst_ref=got[a].at[j], send_sem=send.at[a, j],
                    recv_sem=recv.at[a, j], device_id=(px, py, c), device_id_type=MESH))
        for cp in remote:
            cp.start()
        for cp in remote:
            cp.wait()

    return body


def grads_to_chips(name, parts):
    n = len(parts)
    out_type = [jax.ShapeDtypeStruct((3,) + p.shape[1:], p.dtype) for p in parts]
    scratch = [pltpu.SemaphoreType.DMA((n, 3)), pltpu.SemaphoreType.DMA((n, 3))]
    return _sequencer(name, _to_chips_body(n), out_type, scratch, CHIPS_ID)(*parts)


def all_reduce_small(name, vec):
    rows, m = vec.shape

    def body(x_ref, o_ref, buf, send, recv):
        x, y, c, chips = _place()
        sibling = (x, y, 1 - c)

        def blk(px, py, pc):
            return buf.at[pl.ds(pl.multiple_of((4 * px + 2 * py + pc) * rows, rows), rows), :]

        def copy(k, block, to):
            return pltpu.make_async_remote_copy(src_ref=blk(*block), dst_ref=blk(*block), send_sem=send.at[k],
                                                recv_sem=recv.at[k], device_id=to, device_id_type=MESH)

        blk(x, y, c)[...] = x_ref[...]
        first = [copy(0, (x, y, c), sibling)] + [copy(1 + j, (x, y, c), (*chip, c)) for j, chip in enumerate(chips)]
        for cp in first:
            cp.start()
        passed = [copy(4 + j, (*chip, c), sibling) for j, chip in enumerate(chips)]
        for j, chip in enumerate(chips):
            copy(1 + j, (*chip, c), (x, y, c)).wait_recv()
            passed[j].start()
        copy(0, sibling, (x, y, c)).wait_recv()
        for j, chip in enumerate(chips):
            copy(4 + j, (*chip, 1 - c), (x, y, c)).wait_recv()
        for cp in first + passed:
            cp.wait_send()
        tot = buf[0:rows, :]
        for dev in range(1, N_DEV):
            tot = tot + buf[dev * rows:(dev + 1) * rows, :]
        o_ref[...] = tot

    return pl.pallas_call(
        body, name=name, in_specs=[pl.BlockSpec(memory_space=pltpu.VMEM)],
        out_specs=pl.BlockSpec(memory_space=pltpu.VMEM), out_shape=jax.ShapeDtypeStruct((rows, m), F32),
        scratch_shapes=[pltpu.VMEM((N_DEV * rows, m), F32), pltpu.SemaphoreType.DMA((7,)),
                        pltpu.SemaphoreType.DMA((7,))],
        compiler_params=pltpu.CompilerParams(vmem_limit_bytes=VMEM_LIMIT),
    )(vec)


def _ew_tiles(rows, cols, max_elems=1 << 18):
    tr = rows
    for cand in (1024, 512, 256, 128, 64, 32, 16):
        if rows % cand == 0 and cand * cols <= max_elems:
            tr = cand
            break
    return tr


def chip_sum(name, full, got, core):
    _, kdim, ncol = full.shape
    tr = _ew_tiles(kdim, ncol, max_elems=1 << 20)
    blk = (None, tr, ncol)
    by_chip = pl.BlockSpec(blk, lambda q, i, c: (q, i, 0))

    def body(c_ref, a_ref, b_ref, o_ref):
        o_ref[...] = (a_ref[...].astype(F32) + b_ref[...].astype(F32)).astype(BF16)

    return pl.pallas_call(
        body, name=name,
        grid_spec=pltpu.PrefetchScalarGridSpec(
            num_scalar_prefetch=1, grid=(4, kdim // tr),
            in_specs=[pl.BlockSpec(blk, lambda q, i, c: (2 * q + c[0], i, 0)), by_chip], out_specs=by_chip),
        out_shape=jax.ShapeDtypeStruct((4, kdim, ncol), BF16),
        compiler_params=_params(("parallel", "parallel")),
    )(core, full, got)


def _adamw_math(w, g, m, v):
    m = ADAM_B1 * m + (1.0 - ADAM_B1) * g
    v = ADAM_B2 * v + (1.0 - ADAM_B2) * (g * g)
    m_hat = m / (1.0 - ADAM_B1 ** ADAM_STEP)
    v_hat = v / (1.0 - ADAM_B2 ** ADAM_STEP)
    delta = -ADAM_LR * (m_hat / (jnp.sqrt(v_hat) + ADAM_EPS) + ADAM_WD * w)
    return delta, m, v


def adamw_layer(name, sums, got, w, m, v, outs, layer, chip):
    _, kdim, ncol = sums.shape
    tr = _ew_tiles(kdim, ncol)
    mine = pl.BlockSpec((None, tr, ncol), lambda i, q: (q[0], i, 0))
    others = pl.BlockSpec((3, tr, ncol), lambda i, q: (0, i, 0))
    param = pl.BlockSpec((None, tr, ncol), lambda i, q: (layer, i, 0))
    whole = pl.BlockSpec(memory_space=pl.ANY)

    def body(q_ref, o_ref, g_ref, w_ref, m_ref, v_ref, *rest):
        go_ref, d_ref, mo_ref, vo_ref = rest[-4:]
        g = o_ref[...].astype(F32)
        for j in range(3):
            g = g + g_ref[j].astype(F32)
        d, mn, vn = _adamw_math(w_ref[...], g, m_ref[...], v_ref[...])
        go_ref[...] = g
        d_ref[...] = d
        mo_ref[...] = mn
        vo_ref[...] = vn

    n_in = 6
    return pl.pallas_call(
        body, name=name,
        grid_spec=pltpu.PrefetchScalarGridSpec(
            num_scalar_prefetch=1, grid=(kdim // tr,),
            in_specs=[mine, others, param, param, param] + [whole] * 4, out_specs=[param] * 4),
        out_shape=[jax.ShapeDtypeStruct(w.shape, F32)] * 4,
        input_output_aliases={n_in + k: k for k in range(4)},
        compiler_params=_params(("parallel",)),
    )(chip, sums, got, w, m, v, *outs)


def adamw_small(name, g, w, m, v):
    def body(g_ref, w_ref, m_ref, v_ref, d_ref, mo_ref, vo_ref):
        d, mn, vn = _adamw_math(w_ref[...], g_ref[...], m_ref[...], v_ref[...])
        d_ref[...] = d
        mo_ref[...] = mn
        vo_ref[...] = vn

    vm = pl.BlockSpec(memory_space=pltpu.VMEM)
    return pl.pallas_call(
        body, name=name, in_specs=[vm] * 4, out_specs=[vm] * 3,
        out_shape=[jax.ShapeDtypeStruct(g.shape, F32)] * 3,
        compiler_params=pltpu.CompilerParams(vmem_limit_bytes=VMEM_LIMIT),
    )(g, w, m, v)


def _pack(parts, width):
    flat = jnp.concatenate([p.reshape(-1).astype(F32) for p in parts])
    pad = (-flat.shape[0]) % width
    return jnp.pad(flat, (0, pad)).reshape(-1, width) if pad else flat.reshape(-1, width)


def _unpack(packed, shapes):
    flat = packed.reshape(-1)
    out, off = [], 0
    for s in shapes:
        size = math.prod(s)
        out.append(flat[off:off + size].reshape(s))
        off += size
    return out


def _local_step(h, target, layers, params, on_grads=None):
    a_q_gain, a_k_gain, rel_bias, mix_norm, ffn_norm, conv_b, final_norm = params
    t, d = h.shape
    depth = len(layers)
    n_groups = len(B_GROUPS)
    hg = B_HEADS_PER_GROUP
    n_kv = A_KV_HEADS
    w_a, w_b, w_u = layers[0][0].shape[2], layers[1][0].shape[2], layers[0][2].shape[2]
    n_q = w_a * N_DEV // HEAD_DIM - 2 * n_kv
    dff = layers[0][3].shape[0]
    n_a = (depth + 1) // 2
    cb_full = conv_b.reshape(depth, 2, 1, dff)

    cos, sin = rope_tables(t)
    strides = [band_stride(t, win, dil) for win, dil in B_GROUPS]
    tables = [band_tables(rel_bias[:, g * hg:(g + 1) * hg], win, dil, strides[g], band_block(t, strides[g]))
              for g, (win, dil) in enumerate(B_GROUPS)]

    saved = []
    for i in range(depth):
        j = i // 2
        w_qkv, w_o, w_up_i, w_down_i, cw = layers[i]
        s = {"h_in": h}
        hn = rms_fwd("mix_norm_fwd", h, mix_norm[i])
        s["hn"] = hn
        if i % 2 == 0:
            qkv = mm_col_fwd("a_qkv_fwd", hn, w_qkv, F32)
            qkv_r = qk_prep_fwd("a_qk_prep_fwd", qkv, a_q_gain[j], a_k_gain[j], cos, sin, n_q, n_kv)
            o, lse = mixer_a_fwd(qkv_r, n_q, n_kv)
            s.update(qkv=qkv, qkv_r=qkv_r, o=o, lse=lse)
            h = mm_row_fwd("a_out_fwd", o, w_o, h)
        else:
            qkv = mm_col_fwd("b_qkv_fwd", hn, w_qkv, BF16)
            outs, lzs = [], []
            for g, (win, dil) in enumerate(B_GROUPS):
                o_g, lz_g = mixer_b_group_fwd(qkv, tables[g][0], strides[g], g, n_groups, dil)
                outs.append(o_g)
                lzs.append(lz_g)
            y = combine_fwd("b_combine_fwd", outs, lzs)
            s.update(qkv=qkv, outs=outs, lzs=lzs, y=y)
            h = mm_row_fwd("b_out_fwd", y, w_o, h)
        s["h_mid"] = h
        hn2 = rms_fwd("ffn_norm_fwd", h, ffn_norm[i])
        u2 = mm_col_fwd("ffn_up_fwd", hn2, w_up_i, F32, split=2)
        act = conv_act_fwd("ffn_conv_act_fwd", u2, cw, cb_full[i])
        s.update(hn2=hn2, u2=u2, act=act)
        h = mm_row_fwd("ffn_down_fwd", act, w_down_i, h)
        saved.append(s)

    dh, dh_b, d_final, loss_part = loss_head("loss_head", h, final_norm, target)

    d_mix, d_ffn, d_cw, d_cb = [None] * depth, [None] * depth, [None] * depth, [None] * depth
    d_qg, d_kg = [None] * n_a, [None] * n_a
    d_rel = jnp.zeros((n_groups * hg, LANES), F32)
    layer_grads = [{} for _ in range(depth)]
    pending = []

    def settle():
        done = []
        while pending:
            i_p, part_p, finish = pending.pop()
            layer_grads[i_p][part_p] = finish()
            done += [upd[0] for upd in layer_grads[i_p][part_p]]
        return done

    early = []

    def register(i_p, part_p, grads):
        if on_grads is None:
            layer_grads[i_p][part_p] = grads
        else:
            first, finish = on_grads(i_p, part_p, grads)
            early.extend(first)
            pending.append((i_p, part_p, finish))

    def take_early():
        first = tuple(early)
        early.clear()
        return first

    for i in reversed(range(depth)):
        j = i // 2
        w_qkv, w_o, w_up_i, w_down_i, cw = layers[i]
        s = saved[i]
        dact = mm_row_dx("ffn_down_dx", dh_b, w_down_i)
        g_down = mm_row_dw("ffn_down_dw", s["act"], dh_b)
        du2, dcw = conv_act_bwd("ffn_conv_act_bwd", s["u2"], cw, cb_full[i], dact, take_early())
        d_cw[i] = dcw[:, 0:3, :].transpose(1, 0, 2).reshape(3, 2 * dff)
        d_cb[i] = dcw[:, 3, :].reshape(2 * dff)
        g_up = mm_col_dw("ffn_up_dw", s["hn2"], du2, w_u, split=2)
        dhn2 = mm_col_dx("ffn_up_dx", du2, w_up_i, split=2)
        dh, dh_b, d_ffn[i] = rms_bwd("ffn_norm_bwd", s["h_mid"], ffn_norm[i], dhn2, dh, settle())
        register(i, "ffn", [g_up, g_down.reshape(N_DEV, -1, d)])
        if i % 2 == 0:
            do = mm_row_dx("a_out_dx", dh_b, w_o, take_early())
            g_o = mm_row_dw("a_out_dw", s["o"], dh_b)
            dlt, do_b = row_delta("a_delta", do, s["o"], n_q)
            dq, dk, dv = mixer_a_bwd(s["qkv_r"], do_b, s["lse"], dlt, n_q, n_kv)
            dqkv, dgain = qk_prep_bwd("a_qk_prep_bwd", s["qkv"], dq, dk, dv, a_q_gain[j], a_k_gain[j], cos, sin,
                                      n_q, n_kv)
            d_qg[j], d_kg[j] = dgain[0], dgain[1]
            g_qkv = mm_col_dw("a_qkv_dw", s["hn"], dqkv, w_a)
            dhn = mm_col_dx("a_qkv_dx", dqkv, w_qkv)
        else:
            dy = mm_row_dx("b_out_dx", dh_b, w_o, take_early())
            g_o = mm_row_dw("b_out_dw", s["y"], dh_b)
            res = combine_bwd("b_combine_bwd", dy, s["outs"], s["lzs"])
            dos, dlts = res[:n_groups], res[n_groups:]
            pieces, rel_rows = [], []
            for g, (win, dil) in enumerate(B_GROUPS):
                dq, dk, dv, dbias = mixer_b_group_bwd(s["qkv"], tables[g][0], dos[g], s["lzs"][g], dlts[g],
                                                      strides[g], g, n_groups, dil)
                pieces += [dq, dk, dv]
                rel_rows.append(bias_bucket_sums(f"b_bias_sums_d{dil}", dbias, tables[g][1]))
            d_rel = d_rel + jnp.concatenate(rel_rows, axis=0)
            dqkv = jnp.concatenate(pieces, axis=1)
            g_qkv = mm_col_dw("b_qkv_dw", s["hn"], dqkv, w_b)
            dhn = mm_col_dx("b_qkv_dx", dqkv, w_qkv)
        dh, dh_b, d_mix[i] = rms_bwd("mix_norm_bwd", s["h_in"], mix_norm[i], dhn, dh, settle())
        register(i, "mix", [g_qkv, g_o.reshape(N_DEV, -1, d)])
    last = pending.pop()[2] if pending else None

    d_rel_bias = d_rel[:, :REL_BUCKETS].T
    small_g = [jnp.stack(d_qg), jnp.stack(d_kg), d_rel_bias, jnp.concatenate(d_mix, 0), jnp.concatenate(d_ffn, 0),
               jnp.stack(d_cb), d_final.reshape(-1), jnp.stack(d_cw), loss_part]
    return dh, layer_grads, small_g, last


def kernel(x, a_w_qkv, a_w_o, a_q_gain, a_k_gain, b_w_qkv, b_w_o, rel_bias, mix_norm, ffn_norm, w_up, conv_w, conv_b, w_down, final_norm, loss_target, m_a_w_qkv, m_a_w_o, m_a_q_gain, m_a_k_gain, m_b_w_qkv, m_b_w_o, m_rel_bias, m_mix_norm, m_ffn_norm, m_w_up, m_conv_w, m_conv_b, m_w_down, m_final_norm, v_a_w_qkv, v_a_w_o, v_a_q_gain, v_a_k_gain, v_b_w_qkv, v_b_w_o, v_rel_bias, v_mix_norm, v_ffn_norm, v_w_up, v_conv_w, v_conv_b, v_w_down, v_final_norm):
    d = x.shape[2]
    depth = mix_norm.shape[0]
    dff = w_down.shape[1] * N_DEV
    w_u = w_up.shape[2]
    mixers = [(a_w_qkv, a_w_o, m_a_w_qkv, m_a_w_o, v_a_w_qkv, v_a_w_o),
              (b_w_qkv, b_w_o, m_b_w_qkv, m_b_w_o, v_b_w_qkv, v_b_w_o)]

    layers = []
    for i in range(depth):
        w_qkv, w_o = mixers[i % 2][0][i // 2], mixers[i % 2][1][i // 2]
        shards = [w_qkv.astype(BF16), w_o.astype(BF16), w_up[i].astype(BF16), w_down[i].astype(BF16), conv_w[i]]
        if i == 0:
            (g_qkv,) = gather_layer("gather_l0_qkv", shards[:1])
            g_o, g_up, g_down, g_cw = gather_layer("gather_l0", shards[1:])
        else:
            g_qkv, g_o, g_up, g_down, g_cw = gather_layer(f"gather_l{i}", shards)
        cw = g_cw.transpose(1, 0, 2).reshape(3, 2, dff).transpose(1, 0, 2)
        layers.append((g_qkv, g_o.reshape(-1, d), g_up, g_down.reshape(dff, d), cw))

    core = lax.axis_index("c").astype(jnp.int32).reshape(1)
    chip = (2 * lax.axis_index("x") + lax.axis_index("y")).astype(jnp.int32).reshape(1)

    def reduce_and_update(i, part, grads):
        w_qkv, w_o, m_qkv, m_o, v_qkv, v_o = mixers[i % 2]
        prefix = ("a_w_", "b_w_")[i % 2]
        state = {"mix": [(prefix + "qkv", w_qkv, m_qkv, v_qkv, i // 2), (prefix + "o", w_o, m_o, v_o, i // 2)],
                 "ffn": [("w_up", w_up, m_w_up, v_w_up, i), ("w_down", w_down, m_w_down, v_w_down, i)]}[part]
        got1 = grads_to_sibling(f"to_sibling_l{i}_{part}", grads)
        sums = [chip_sum(f"chip_sum_l{i}_{part}{a}", grads[a], got1[a], core) for a in range(2)]
        got2 = grads_to_chips(f"to_chips_l{i}_{part}", sums)

        def finish():
            for a, (key, w, m, v, layer) in enumerate(state):
                outs = big_out.get(key) or [lax.empty(w.shape, F32) for _ in range(4)]
                big_out[key] = adamw_layer(f"adamw_l{i}_{part}{a}", sums[a], got2[a], w, m, v, outs, layer, chip)
            return [big_out[key] for key, *_ in state]

        return sums, finish

    big_out = {}
    dh, _, small_g, last = _local_step(x[0], loss_target[0], layers,
                                       (a_q_gain, a_k_gain, rel_bias, mix_norm, ffn_norm, conv_b, final_norm),
                                       reduce_and_update)
    grad_x = dh[None]

    width = 2048
    packed = _pack(small_g, N_DEV * width).reshape(-1, N_DEV, width)
    n_rows = packed.shape[0]
    packed = packed.transpose(1, 0, 2).reshape(N_DEV, n_rows * width)
    red = all_reduce_small("small_all_reduce", packed)
    last()
    red = red.reshape(N_DEV, n_rows, width).transpose(1, 0, 2)
    (g_qg, g_kg, g_rel, g_mix, g_ffn, g_cb, g_fin, g_cw_all, loss) = _unpack(red, [p.shape for p in small_g])
    idx = 4 * lax.axis_index("x") + 2 * lax.axis_index("y") + lax.axis_index("c")
    g_cw_mine = lax.dynamic_slice_in_dim(g_cw_all, idx * w_u, w_u, axis=2)

    small_w = [a_q_gain, a_k_gain, rel_bias, mix_norm, ffn_norm, conv_b, final_norm, conv_w]
    small_m = [m_a_q_gain, m_a_k_gain, m_rel_bias, m_mix_norm, m_ffn_norm, m_conv_b, m_final_norm, m_conv_w]
    small_v = [v_a_q_gain, v_a_k_gain, v_rel_bias, v_mix_norm, v_ffn_norm, v_conv_b, v_final_norm, v_conv_w]
    small_grads = [g_qg, g_kg, g_rel, g_mix, g_ffn, g_cb, g_fin, g_cw_mine]
    shapes = [w.shape for w in small_w]
    pad_rows = (-_pack(small_w, width).shape[0]) % 8

    def pk8(parts):
        p = _pack(parts, width)
        return jnp.pad(p, ((0, pad_rows), (0, 0))) if pad_rows else p

    sd, sm, sv = adamw_small("adamw_small", pk8(small_grads), pk8(small_w), pk8(small_m), pk8(small_v))
    sd, sm, sv = _unpack(sd, shapes), _unpack(sm, shapes), _unpack(sv, shapes)

    names = ["a_w_qkv", "a_w_o", "a_q_gain", "a_k_gain", "b_w_qkv", "b_w_o", "rel_bias", "mix_norm", "ffn_norm",
             "w_up", "conv_w", "conv_b", "w_down", "final_norm"]
    small_names = ["a_q_gain", "a_k_gain", "rel_bias", "mix_norm", "ffn_norm", "conv_b", "final_norm", "conv_w"]
    grads, deltas, new_m, new_v = {}, {}, {}, {}
    for nm, outs in big_out.items():
        grads[nm], deltas[nm], new_m[nm], new_v[nm] = outs
    for a, nm in enumerate(small_names):
        grads[nm] = small_grads[a].reshape(shapes[a])
        deltas[nm], new_m[nm], new_v[nm] = sd[a], sm[a], sv[a]
    return (loss.reshape(()), grad_x, *[grads[n] for n in names], *[deltas[n] for n in names],
            *[new_m[n] for n in names], *[new_v[n] for n in names])
```

```python
import functools
import math

import jax
import jax.numpy as jnp
from jax import lax
from jax.experimental import pallas as pl
from jax.experimental.pallas import tpu as pltpu
from jax.experimental.pallas import tpu_sc as plsc

F32 = jnp.float32
BF16 = jnp.bfloat16
MESH = pl.DeviceIdType.MESH

N_DEV = 8
LANES = 128
HEAD_DIM = 128
VMEM_LIMIT = 56 * 1024 * 1024
GRID_W = 64
ROPE_THETA = 10000.0
A_KV_HEADS = 4
B_GROUPS = ((128, 1), (512, 4), (2048, 16))
B_HEADS_PER_GROUP = 8
REL_BUCKETS = 32
REL_MAX_DISTANCE = 1024
EPS = 1e-6
NEG_INF = -1e30
ADAM_LR = 0.001
ADAM_B1 = 0.9
ADAM_B2 = 0.999
ADAM_EPS = 1e-08
ADAM_WD = 0.01
ADAM_STEP = 10

ROW_TILE = 256
MM_TM = 1024
MM_TK = 2048
A_BQ = 1024
A_BK = 1024
B_BQ = 256
ATTN_ROWS = 16
ATTN_SCALE = HEAD_DIM ** -0.5

NN = (((1,), (0,)), ((), ()))
NT = (((1,), (1,)), ((), ()))
TN = (((0,), (0,)), ((), ()))


def _tile(n, pref):
    return pref if n % pref == 0 else n


def _div_tile(n, pref):
    for cand in range(pref - pref % LANES, 0, -LANES):
        if n % cand == 0:
            return cand
    return n


def _params(sem):
    return pltpu.CompilerParams(dimension_semantics=sem, vmem_limit_bytes=VMEM_LIMIT)


def _dot(a, b, dims):
    return lax.dot_general(a, b, dims, preferred_element_type=F32)


def _mm(name, a, b, *, grid, a_blk, a_map, b_blk, b_map, o_blk, o_map, out_shape, out_dtype, dims,
        res=None, after=()):
    nk = grid[2]
    acc_shape = tuple(d for d in o_blk if d is not None)

    def body(*refs):
        a_ref, b_ref = refs[:2]
        r_ref = None if res is None else refs[2]
        part = _dot(a_ref[...].astype(BF16), b_ref[...].astype(BF16), dims)
        if nk == 1:
            o_ref = refs[-1]
            o_ref[...] = (part if r_ref is None else part + r_ref[...]).astype(out_dtype)
            return
        o_ref, acc = refs[-2:]
        k = pl.program_id(2)

        @pl.when(k == 0)
        def _():
            acc[...] = part

        @pl.when(k > 0)
        def _():
            acc[...] += part

        @pl.when(k == nk - 1)
        def _():
            r = acc[...]
            if r_ref is not None:
                r = r + r_ref[...]
            o_ref[...] = r.astype(out_dtype)

    in_specs = [pl.BlockSpec(a_blk, a_map), pl.BlockSpec(b_blk, b_map)]
    args = [a, b]
    if res is not None:
        in_specs.append(pl.BlockSpec(o_blk, o_map))
        args.append(res)
    in_specs += [pl.BlockSpec(memory_space=pl.ANY)] * len(after)
    args += list(after)
    return pl.pallas_call(
        body, name=name, grid=grid, in_specs=in_specs, out_specs=pl.BlockSpec(o_blk, o_map),
        out_shape=jax.ShapeDtypeStruct(out_shape, out_dtype),
        scratch_shapes=[] if nk == 1 else [pltpu.VMEM(acc_shape, F32)],
        compiler_params=_params(("parallel", "parallel", "arbitrary")),
    )(*args)


def mm_col_fwd(name, a, wg, out_dtype, split=1):
    m, kdim = a.shape
    n_dev, _, w = wg.shape
    tm, tk = _tile(m, MM_TM), _div_tile(kdim, MM_TK)
    per = n_dev // split
    if split == 1:
        o_blk, o_map, o_shape = (tm, w), (lambda i, j, k: (i, j)), (m, n_dev * w)
    else:
        o_blk, o_map, o_shape = (None, tm, w), (lambda i, j, k: (j // per, i, j % per)), (split, m, per * w)
    return _mm(name, a, wg, grid=(m // tm, n_dev, kdim // tk),
               a_blk=(tm, tk), a_map=lambda i, j, k: (i, k),
               b_blk=(None, tk, w), b_map=lambda i, j, k: (j, k, 0),
               o_blk=o_blk, o_map=o_map, out_shape=o_shape, out_dtype=out_dtype, dims=NN)


def mm_col_dx(name, dy, wg, split=1):
    n_dev, kdim, w = wg.shape
    m = dy.shape[-2]
    tm, tk = _tile(m, MM_TM), _div_tile(kdim, MM_TK)
    per = n_dev // split
    if split == 1:
        a_blk, a_map = (tm, w), (lambda i, j, k: (i, k))
    else:
        a_blk, a_map = (None, tm, w), (lambda i, j, k: (k // per, i, k % per))
    return _mm(name, dy, wg, grid=(m // tm, kdim // tk, n_dev),
               a_blk=a_blk, a_map=a_map,
               b_blk=(None, tk, w), b_map=lambda i, j, k: (k, j, 0),
               o_blk=(tm, tk), o_map=lambda i, j, k: (i, j), out_shape=(m, kdim), out_dtype=F32, dims=NT)


def mm_col_dw(name, x, dy, w, split=1):
    m, kdim = x.shape
    tm, tk = _tile(m, MM_TM), _div_tile(kdim, MM_TK)
    per = N_DEV // split
    if split == 1:
        b_blk, b_map = (tm, w), (lambda i, j, k: (k, j))
    else:
        b_blk, b_map = (None, tm, w), (lambda i, j, k: (j // per, k, j % per))
    return _mm(name, x, dy, grid=(kdim // tk, N_DEV, m // tm),
               a_blk=(tm, tk), a_map=lambda i, j, k: (k, i),
               b_blk=b_blk, b_map=b_map,
               o_blk=(None, tk, w), o_map=lambda i, j, k: (j, i, 0),
               out_shape=(N_DEV, kdim, w), out_dtype=BF16, dims=TN)


def mm_row_fwd(name, a, wg, res):
    m, kdim = a.shape
    n = wg.shape[1]
    tm, tk, tn = _tile(m, MM_TM), _div_tile(kdim, MM_TK), _tile(n, 1024)
    return _mm(name, a, wg, grid=(m // tm, n // tn, kdim // tk),
               a_blk=(tm, tk), a_map=lambda i, j, k: (i, k),
               b_blk=(tk, tn), b_map=lambda i, j, k: (k, j),
               o_blk=(tm, tn), o_map=lambda i, j, k: (i, j), out_shape=(m, n), out_dtype=F32, dims=NN,
               res=res)


def mm_row_dx(name, dy, wg, after=()):
    m, n = dy.shape
    kdim = wg.shape[0]
    tm, tk, tn = _tile(m, MM_TM), _div_tile(kdim, MM_TK), _tile(n, MM_TK)
    return _mm(name, dy, wg, grid=(m // tm, kdim // tk, n // tn),
               a_blk=(tm, tn), a_map=lambda i, j, k: (i, k),
               b_blk=(tk, tn), b_map=lambda i, j, k: (j, k),
               o_blk=(tm, tk), o_map=lambda i, j, k: (i, j), out_shape=(m, kdim), out_dtype=F32, dims=NT,
               after=after)


def mm_row_dw(name, x, dy):
    m, kdim = x.shape
    n = dy.shape[1]
    tm, tk, tn = _tile(m, MM_TM), _div_tile(kdim, MM_TK), _tile(n, 1024)
    return _mm(name, x, dy, grid=(kdim // tk, n // tn, m // tm),
               a_blk=(tm, tk), a_map=lambda i, j, k: (k, i),
               b_blk=(tm, tn), b_map=lambda i, j, k: (k, j),
               o_blk=(tk, tn), o_map=lambda i, j, k: (i, j), out_shape=(kdim, n), out_dtype=BF16, dims=TN)


def _rows(d, tm):
    return pl.BlockSpec((tm, d), lambda i: (i, 0))


def _vec(d):
    return pl.BlockSpec((1, d), lambda i: (0, 0))


def rms_fwd(name, h, gain):
    t, d = h.shape
    tm = _tile(t, ROW_TILE)

    def body(h_ref, g_ref, o_ref):
        x = h_ref[...]
        rstd = lax.rsqrt(jnp.mean(x * x, axis=-1, keepdims=True) + EPS)
        o_ref[...] = (x * rstd * g_ref[...]).astype(BF16)

    return pl.pallas_call(
        body, name=name, grid=(t // tm,), in_specs=[_rows(d, tm), _vec(d)], out_specs=_rows(d, tm),
        out_shape=jax.ShapeDtypeStruct((t, d), BF16), compiler_params=_params(("parallel",)),
    )(h, gain.reshape(1, d))


def rms_bwd(name, h, gain, dy, dres, after=()):
    t, d = h.shape
    tm = _tile(t, ROW_TILE)

    def body(h_ref, g_ref, dy_ref, r_ref, *rest):
        dh_ref, dhb_ref, dg_ref = rest[-3:]

        @pl.when(pl.program_id(0) == 0)
        def _():
            dg_ref[...] = jnp.zeros_like(dg_ref)

        x = h_ref[...]
        rstd = lax.rsqrt(jnp.mean(x * x, axis=-1, keepdims=True) + EPS)
        xhat = x * rstd
        dyv = dy_ref[...]
        dxhat = dyv * g_ref[...]
        dh = r_ref[...] + rstd * (dxhat - xhat * jnp.mean(dxhat * xhat, axis=-1, keepdims=True))
        dh_ref[...] = dh
        dhb_ref[...] = dh.astype(BF16)
        dg_ref[...] += jnp.sum(dyv * xhat, axis=0, keepdims=True)

    return pl.pallas_call(
        body, name=name, grid=(t // tm,),
        in_specs=[_rows(d, tm), _vec(d), _rows(d, tm), _rows(d, tm)]
        + [pl.BlockSpec(memory_space=pl.ANY)] * len(after),
        out_specs=[_rows(d, tm), _rows(d, tm), _vec(d)],
        out_shape=[jax.ShapeDtypeStruct((t, d), F32), jax.ShapeDtypeStruct((t, d), BF16),
                   jax.ShapeDtypeStruct((1, d), F32)],
        compiler_params=_params(("arbitrary",)),
    )(h, gain.reshape(1, d), dy, dres, *after)


def loss_head(name, h, gain, target):
    t, d = h.shape
    tm = _tile(t, ROW_TILE)

    def body(h_ref, g_ref, t_ref, dh_ref, dhb_ref, dg_ref, loss_ref):
        @pl.when(pl.program_id(0) == 0)
        def _():
            dg_ref[...] = jnp.zeros_like(dg_ref)
            loss_ref[...] = jnp.zeros_like(loss_ref)

        x = h_ref[...]
        rstd = lax.rsqrt(jnp.mean(x * x, axis=-1, keepdims=True) + EPS)
        xhat = x * rstd
        err = xhat * g_ref[...] - t_ref[...]
        row = jnp.mean(err * err, axis=-1, keepdims=True)
        loss_ref[...] += 0.5 * jnp.sum(row, axis=0, keepdims=True)
        dyv = err * (1.0 / d)
        dxhat = dyv * g_ref[...]
        dh = rstd * (dxhat - xhat * jnp.mean(dxhat * xhat, axis=-1, keepdims=True))
        dh_ref[...] = dh
        dhb_ref[...] = dh.astype(BF16)
        dg_ref[...] += jnp.sum(dyv * xhat, axis=0, keepdims=True)

    return pl.pallas_call(
        body, name=name, grid=(t // tm,),
        in_specs=[_rows(d, tm), _vec(d), _rows(d, tm)],
        out_specs=[_rows(d, tm), _rows(d, tm), _vec(d), pl.BlockSpec((1, 1), lambda i: (0, 0))],
        out_shape=[jax.ShapeDtypeStruct((t, d), F32), jax.ShapeDtypeStruct((t, d), BF16),
                   jax.ShapeDtypeStruct((1, d), F32), jax.ShapeDtypeStruct((1, 1), F32)],
        compiler_params=_params(("arbitrary",)),
    )(h, gain.reshape(1, d), target)


def rope_tables(seq):
    pos = jnp.arange(seq, dtype=jnp.int32)
    row_ids = (pos // GRID_W).astype(F32)
    col_ids = (pos % GRID_W).astype(F32)
    quarter = HEAD_DIM // 4
    inv_freq = ROPE_THETA ** (-jnp.arange(quarter, dtype=F32) / quarter)
    ar = row_ids[:, None] * inv_freq[None, :]
    ac = col_ids[:, None] * inv_freq[None, :]
    cos = jnp.concatenate([jnp.cos(ar), jnp.cos(ar), jnp.cos(ac), jnp.cos(ac)], axis=-1)
    sin = jnp.concatenate([-jnp.sin(ar), jnp.sin(ar), -jnp.sin(ac), jnp.sin(ac)], axis=-1)
    return cos, sin


def _swap_quarters(x):
    lane = lax.broadcasted_iota(jnp.int32, x.shape, 1)
    q = HEAD_DIM // 4
    return jnp.where((lane % (2 * q)) < q, pltpu.roll(x, HEAD_DIM - q, 1), pltpu.roll(x, q, 1))


def qk_prep_fwd(name, qkv, q_gain, k_gain, cos, sin, n_q, n_kv):
    t, width = qkv.shape
    tm = _tile(t, ROW_TILE)

    def body(x_ref, qg_ref, kg_ref, c_ref, s_ref, o_ref):
        c, s = c_ref[...], s_ref[...]
        for hd in range(n_q + n_kv):
            sl = slice(hd * HEAD_DIM, (hd + 1) * HEAD_DIM)
            x = x_ref[:, sl]
            g = qg_ref[...] if hd < n_q else kg_ref[...]
            xn = x * lax.rsqrt(jnp.mean(x * x, axis=-1, keepdims=True) + EPS) * g
            o_ref[:, sl] = (xn * c + _swap_quarters(xn) * s).astype(BF16)
        vs = slice((n_q + n_kv) * HEAD_DIM, width)
        o_ref[:, vs] = x_ref[:, vs].astype(BF16)

    return pl.pallas_call(
        body, name=name, grid=(t // tm,),
        in_specs=[_rows(width, tm), _vec(HEAD_DIM), _vec(HEAD_DIM), _rows(HEAD_DIM, tm), _rows(HEAD_DIM, tm)],
        out_specs=_rows(width, tm), out_shape=jax.ShapeDtypeStruct((t, width), BF16),
        compiler_params=_params(("parallel",)),
    )(qkv, q_gain.reshape(1, HEAD_DIM), k_gain.reshape(1, HEAD_DIM), cos, sin)


def qk_prep_bwd(name, qkv, dq, dk, dv, q_gain, k_gain, cos, sin, n_q, n_kv):
    t, width = qkv.shape
    tm = _tile(t, ROW_TILE)

    def body(x_ref, dq_ref, dk_ref, dv_ref, qg_ref, kg_ref, c_ref, s_ref, o_ref, dg_ref):
        @pl.when(pl.program_id(0) == 0)
        def _():
            dg_ref[...] = jnp.zeros_like(dg_ref)

        c, s = c_ref[...], s_ref[...]
        dgq = jnp.zeros((1, HEAD_DIM), F32)
        dgk = jnp.zeros((1, HEAD_DIM), F32)
        for hd in range(n_q + n_kv):
            sl = slice(hd * HEAD_DIM, (hd + 1) * HEAD_DIM)
            x = x_ref[:, sl]
            if hd < n_q:
                g, dout = qg_ref[...], dq_ref[:, sl]
            else:
                ks = slice((hd - n_q) * HEAD_DIM, (hd - n_q + 1) * HEAD_DIM)
                g, dout = kg_ref[...], dk_ref[:, ks]
            rstd = lax.rsqrt(jnp.mean(x * x, axis=-1, keepdims=True) + EPS)
            xhat = x * rstd
            dxn = dout * c + _swap_quarters(dout * s)
            part = jnp.sum(dxn * xhat, axis=0, keepdims=True)
            if hd < n_q:
                dgq = dgq + part
            else:
                dgk = dgk + part
            dxhat = dxn * g
            o_ref[:, sl] = (rstd * (dxhat - xhat * jnp.mean(dxhat * xhat, axis=-1, keepdims=True))).astype(BF16)
        o_ref[:, slice((n_q + n_kv) * HEAD_DIM, width)] = dv_ref[...].astype(BF16)
        dg_ref[0:1, :] += dgq
        dg_ref[1:2, :] += dgk

    kvw = n_kv * HEAD_DIM
    return pl.pallas_call(
        body, name=name, grid=(t // tm,),
        in_specs=[_rows(width, tm), _rows(n_q * HEAD_DIM, tm), _rows(kvw, tm), _rows(kvw, tm),
                  _vec(HEAD_DIM), _vec(HEAD_DIM), _rows(HEAD_DIM, tm), _rows(HEAD_DIM, tm)],
        out_specs=[_rows(width, tm), pl.BlockSpec((2, HEAD_DIM), lambda i: (0, 0))],
        out_shape=[jax.ShapeDtypeStruct((t, width), BF16), jax.ShapeDtypeStruct((2, HEAD_DIM), F32)],
        compiler_params=_params(("arbitrary",)),
    )(qkv, dq, dk, dv, q_gain.reshape(1, HEAD_DIM), k_gain.reshape(1, HEAD_DIM), cos, sin)


def _lanes(x, width):
    return jnp.tile(x, (1, width // LANES))


def _hs(hd):
    return slice(hd * HEAD_DIM, (hd + 1) * HEAD_DIM)


def attn_fwd(name, q, k, v, bias, *, grid, q_spec, k_spec, v_spec, b_spec, o_spec, valid, nh, shared_kv,
             bq, bk, o_shape, o_dtype):
    ns = grid[2]

    def body(*refs):
        if bias is None:
            q_ref, k_ref, v_ref, o_ref, lse_ref, m_s, l_s, acc_s = refs
            b_ref = None
        else:
            q_ref, k_ref, v_ref, b_ref, o_ref, lse_ref, m_s, l_s, acc_s = refs
        step = pl.program_id(2)

        @pl.when(step == 0)
        def _():
            m_s[...] = jnp.full_like(m_s, -jnp.inf)
            l_s[...] = jnp.zeros_like(l_s)
            acc_s[...] = jnp.zeros_like(acc_s)

        @pl.when(valid(pl.program_id(1), step))
        def _():
            for hd in range(nh):
                kh = _hs(0 if shared_kv else hd)
                s = _dot(q_ref[:, _hs(hd)], k_ref[:, kh], NT)
                p_rows, a_rows = [], []
                for r0 in range(0, bq, ATTN_ROWS):
                    rows = slice(r0, r0 + ATTN_ROWS)
                    z = s[rows] * ATTN_SCALE
                    if b_ref is not None:
                        z = z + b_ref[hd, rows, :]
                    m_prev = m_s[hd, rows, :]
                    m_new = jnp.maximum(m_prev, jnp.max(z, axis=-1, keepdims=True))
                    alpha = jnp.exp(m_prev - m_new)
                    p = jnp.exp(z - _lanes(m_new, bk))
                    l_s[hd, rows, :] = alpha * l_s[hd, rows, :] + jnp.sum(p, axis=-1, keepdims=True)
                    m_s[hd, rows, :] = m_new
                    p_rows.append(p.astype(BF16))
                    a_rows.append(alpha)
                pv = _dot(jnp.concatenate(p_rows, axis=0), v_ref[:, kh], NN)
                acc_s[hd] = jnp.concatenate(a_rows, axis=0) * acc_s[hd] + pv

        @pl.when(step == ns - 1)
        def _():
            for hd in range(nh):
                o_ref[:, _hs(hd)] = (acc_s[hd] / l_s[hd]).astype(o_dtype)
                lse_ref[:, _hs(hd)] = m_s[hd] + jnp.log(l_s[hd])

    in_specs = [q_spec, k_spec, v_spec] + ([] if bias is None else [b_spec])
    args = [q, k, v] + ([] if bias is None else [bias])
    stat = pltpu.VMEM((nh, bq, LANES), F32)
    return pl.pallas_call(
        body, name=name, grid=grid, in_specs=in_specs, out_specs=[o_spec, o_spec],
        out_shape=[jax.ShapeDtypeStruct(o_shape, o_dtype), jax.ShapeDtypeStruct(o_shape, F32)],
        scratch_shapes=[stat, stat, stat],
        compiler_params=_params(("parallel", "parallel", "arbitrary")),
    )(*args)


def _probs(q_ref, k_ref, v_ref, do_ref, lse_ref, dlt_ref, b_ref, hd, kh, bq, bk, want_p=True, on_ds=None):
    s = _dot(q_ref[:, _hs(hd)], k_ref[:, kh], NT)
    dp = _dot(do_ref[:, _hs(hd)], v_ref[:, kh], NT)
    p_rows, ds_rows = [], []
    for r0 in range(0, bq, ATTN_ROWS):
        rows = slice(r0, r0 + ATTN_ROWS)
        z = s[rows] * ATTN_SCALE
        if b_ref is not None:
            z = z + b_ref[hd, rows, :]
        p = jnp.exp(z - _lanes(lse_ref[rows, _hs(hd)], bk))
        ds = p * (dp[rows] - _lanes(dlt_ref[rows, _hs(hd)], bk))
        if on_ds is not None:
            on_ds(rows, ds)
        if want_p:
            p_rows.append(p.astype(BF16))
        ds_rows.append(ds.astype(BF16))
    return (jnp.concatenate(p_rows, axis=0) if want_p else None), jnp.concatenate(ds_rows, axis=0)


def attn_bwd_dq(name, q, k, v, do, lse, dlt, *, grid, q_spec, k_spec, v_spec, nh, bq, bk, o_shape):
    ns = grid[2]
    scale = HEAD_DIM ** -0.5

    def body(q_ref, k_ref, v_ref, do_ref, lse_ref, dlt_ref, dq_ref, acc_s):
        step = pl.program_id(2)

        @pl.when(step == 0)
        def _():
            acc_s[...] = jnp.zeros_like(acc_s)

        for hd in range(nh):
            _, ds = _probs(q_ref, k_ref, v_ref, do_ref, lse_ref, dlt_ref, None, hd, _hs(0), bq, bk, want_p=False)
            acc_s[hd] += _dot(ds, k_ref[:, _hs(0)], NN)

        @pl.when(step == ns - 1)
        def _():
            for hd in range(nh):
                dq_ref[:, _hs(hd)] = acc_s[hd] * scale

    return pl.pallas_call(
        body, name=name, grid=grid, in_specs=[q_spec, k_spec, v_spec, q_spec, q_spec, q_spec],
        out_specs=q_spec, out_shape=jax.ShapeDtypeStruct(o_shape, F32),
        scratch_shapes=[pltpu.VMEM((nh, bq, LANES), F32)],
        compiler_params=_params(("parallel", "parallel", "arbitrary")),
    )(q, k, v, do, lse, dlt)


def _always(i, s):
    return s >= 0


def row_delta(name, do, o, n_heads):
    t, width = do.shape
    tm = _tile(t, ROW_TILE)

    def body(do_ref, o_ref, dl_ref, dob_ref):
        for hd in range(n_heads):
            d = do_ref[:, _hs(hd)]
            s = jnp.sum(d * o_ref[:, _hs(hd)].astype(F32), axis=-1, keepdims=True)
            dl_ref[:, _hs(hd)] = jnp.broadcast_to(s, (tm, HEAD_DIM))
            dob_ref[:, _hs(hd)] = d.astype(BF16)

    return pl.pallas_call(
        body, name=name, grid=(t // tm,), in_specs=[_rows(width, tm), _rows(width, tm)],
        out_specs=[_rows(width, tm), _rows(width, tm)],
        out_shape=[jax.ShapeDtypeStruct((t, width), F32), jax.ShapeDtypeStruct((t, width), BF16)],
        compiler_params=_params(("parallel",)),
    )(do, o)


def _a_specs(n_q, n_kv, bq, bk, q_major):
    grp = n_q // n_kv
    if q_major:
        qm, km = (lambda b, i, s: (i, b)), (lambda b, i, s: (s, n_q + b))
        vm = lambda b, i, s: (s, n_q + n_kv + b)
    else:
        qm, km = (lambda b, i, s: (s, b)), (lambda b, i, s: (i, n_q + b))
        vm = lambda b, i, s: (i, n_q + n_kv + b)
    return (pl.BlockSpec((bq, grp * HEAD_DIM), qm), pl.BlockSpec((bk, HEAD_DIM), km),
            pl.BlockSpec((bk, HEAD_DIM), vm))


def mixer_a_fwd(qkv_r, n_q, n_kv):
    t = qkv_r.shape[0]
    bq, bk = _tile(t, A_BQ), _tile(t, A_BK)
    q_spec, k_spec, v_spec = _a_specs(n_q, n_kv, bq, bk, True)
    return attn_fwd("a_attn_fwd", qkv_r, qkv_r, qkv_r, None, grid=(n_kv, t // bq, t // bk),
                    q_spec=q_spec, k_spec=k_spec, v_spec=v_spec, b_spec=None, o_spec=q_spec, valid=_always,
                    nh=n_q // n_kv, shared_kv=True, bq=bq, bk=bk, o_shape=(t, n_q * HEAD_DIM), o_dtype=BF16)


def mixer_a_bwd(qkv_r, do_b, lse, dlt, n_q, n_kv):
    t = qkv_r.shape[0]
    bq, bk = _tile(t, A_BQ), _tile(t, A_BK)
    grp = n_q // n_kv
    q_spec, k_spec, v_spec = _a_specs(n_q, n_kv, bq, bk, True)
    dq = attn_bwd_dq("a_attn_dq", qkv_r, qkv_r, qkv_r, do_b, lse, dlt, grid=(n_kv, t // bq, t // bk),
                     q_spec=q_spec, k_spec=k_spec, v_spec=v_spec, nh=grp, bq=bq, bk=bk,
                     o_shape=(t, n_q * HEAD_DIM))
    q_spec, k_spec, v_spec = _a_specs(n_q, n_kv, bq, bk, False)
    o_spec = pl.BlockSpec((bk, HEAD_DIM), lambda b, i, s: (i, b))
    dk, dv = _attn_bwd_dkv_out(qkv_r, do_b, lse, dlt, grid=(n_kv, t // bk, t // bq), q_spec=q_spec,
                               k_spec=k_spec, v_spec=v_spec, o_spec=o_spec, grp=grp, bq=bq, bk=bk,
                               o_shape=(t, n_kv * HEAD_DIM))
    return dq, dk, dv


def _attn_bwd_dkv_out(qkv_r, do_b, lse, dlt, *, grid, q_spec, k_spec, v_spec, o_spec, grp, bq, bk, o_shape):
    ns = grid[2]
    scale = HEAD_DIM ** -0.5

    def body(q_ref, k_ref, v_ref, do_ref, lse_ref, dlt_ref, dk_ref, dv_ref, dk_s, dv_s):
        step = pl.program_id(2)

        @pl.when(step == 0)
        def _():
            dk_s[...] = jnp.zeros_like(dk_s)
            dv_s[...] = jnp.zeros_like(dv_s)

        for hd in range(grp):
            p, ds = _probs(q_ref, k_ref, v_ref, do_ref, lse_ref, dlt_ref, None, hd, _hs(0), bq, bk)
            dv_s[...] += _dot(p, do_ref[:, _hs(hd)], TN)
            dk_s[...] += _dot(ds, q_ref[:, _hs(hd)], TN)

        @pl.when(step == ns - 1)
        def _():
            dk_ref[...] = dk_s[...] * scale
            dv_ref[...] = dv_s[...]

    acc = pltpu.VMEM((bk, HEAD_DIM), F32)
    return pl.pallas_call(
        body, name="a_attn_dkv", grid=grid, in_specs=[q_spec, k_spec, v_spec, q_spec, q_spec, q_spec],
        out_specs=[o_spec, o_spec], out_shape=[jax.ShapeDtypeStruct(o_shape, F32)] * 2,
        scratch_shapes=[acc, acc], compiler_params=_params(("parallel", "parallel", "arbitrary")),
    )(qkv_r, qkv_r, qkv_r, do_b, lse, dlt)


def t5_bucket(rel):
    nb = REL_BUCKETS // 2
    max_exact = nb // 2
    base = jnp.where(rel > 0, nb, 0)
    n = jnp.abs(rel)
    nf = jnp.maximum(n, 1).astype(F32)
    large = max_exact + (jnp.log(nf / max_exact) / math.log(REL_MAX_DISTANCE / max_exact)
                         * (nb - max_exact)).astype(jnp.int32)
    large = jnp.minimum(large, nb - 1)
    return base + jnp.where(n < max_exact, n, large)


def band_stride(t, win, dil):
    return 1 if t % B_BQ == 0 and win // 2 <= B_BQ else dil


def band_tables(rel_bias_g, win, dil, stride, bq):
    a = jnp.arange(bq)[:, None]
    b = jnp.arange(bq)[None, :]
    rel = jnp.stack([(s - 1) * bq + b - a for s in range(3)]) * stride
    ok = (jnp.abs(rel) <= win // 2) & (rel % dil == 0)
    bucket = t5_bucket(rel)
    bias = jnp.zeros((rel_bias_g.shape[1],) + rel.shape, F32)
    for r in range(REL_BUCKETS):
        bias = bias + jnp.where(bucket[None] == r, rel_bias_g[r][:, None, None, None], 0.0)
    return jnp.where(ok[None], bias, NEG_INF), jnp.where(ok, bucket, -1).astype(jnp.int32)


def band_block(t, stride):
    return _tile(t // stride, B_BQ)


def _b_geometry(t, dil, g, n_groups, bq):
    hg = B_HEADS_PER_GROUP
    length = t // dil
    nblk = length // bq
    gw = hg * HEAD_DIM
    per_tok = 3 * n_groups
    return hg, length, bq, nblk, gw, per_tok


def mixer_b_group_fwd(qkv, bias, dil, g, n_groups, tag):
    t = qkv.shape[0]
    hg, length, bq, nblk, gw, per_tok = _b_geometry(t, dil, g, n_groups, bias.shape[2])
    if dil > 1:
        qkv, g, per_tok = qkv[:, 3 * g * gw:3 * (g + 1) * gw], 0, 3
    view = qkv.reshape(length, dil * qkv.shape[1])
    col = lambda c, which: c * per_tok + 3 * g + which
    kblk = lambda i, s: jnp.clip(i - 1 + s, 0, nblk - 1)
    spec = lambda which, streamed: pl.BlockSpec(
        (bq, gw), (lambda c, i, s: (kblk(i, s), col(c, which))) if streamed else (lambda c, i, s: (i, col(c, which))))
    valid = lambda i, s: (i - 1 + s >= 0) & (i - 1 + s < nblk)
    o, lz = attn_fwd(f"b_attn_fwd_d{tag}", view, view, view, bias, grid=(dil, nblk, 3),
                     q_spec=spec(0, False), k_spec=spec(1, True), v_spec=spec(2, True),
                     b_spec=pl.BlockSpec((hg, None, bq, bq), lambda c, i, s: (0, s, 0, 0)),
                     o_spec=pl.BlockSpec((bq, gw), lambda c, i, s: (i, c)), valid=valid, nh=hg,
                     shared_kv=False, bq=bq, bk=bq, o_shape=(length, dil * gw), o_dtype=F32)
    return o.reshape(t, gw), lz.reshape(t, gw)


def mixer_b_group_bwd(qkv, bias, do_g, lz_g, dlt_g, dil, g, n_groups, tag):
    t = qkv.shape[0]
    hg, length, bq, nblk, gw, per_tok = _b_geometry(t, dil, g, n_groups, bias.shape[2])
    if dil > 1:
        qkv, g, per_tok = qkv[:, 3 * g * gw:3 * (g + 1) * gw], 0, 3
    view = qkv.reshape(length, dil * qkv.shape[1])
    dov, lzv, dlv = (x.reshape(length, dil * gw) for x in (do_g, lz_g, dlt_g))
    col = lambda c, which: c * per_tok + 3 * g + which
    nbr = lambda i, s: jnp.clip(i - 1 + s, 0, nblk - 1)
    valid = lambda i, s: (i - 1 + s >= 0) & (i - 1 + s < nblk)
    q_spec = pl.BlockSpec((bq, gw), lambda c, i, s: (i, col(c, 0)))
    k_spec = pl.BlockSpec((bq, gw), lambda c, i, s: (nbr(i, s), col(c, 1)))
    v_spec = pl.BlockSpec((bq, gw), lambda c, i, s: (nbr(i, s), col(c, 2)))
    stat = pl.BlockSpec((bq, gw), lambda c, i, s: (i, c))
    dq, dbias = _band_bwd_dq(f"b_attn_dq_d{tag}", view, dov, lzv, dlv, bias, grid=(dil, nblk, 3),
                             q_spec=q_spec, k_spec=k_spec, v_spec=v_spec, stat_spec=stat,
                             b_spec=pl.BlockSpec((hg, None, bq, bq), lambda c, i, s: (0, s, 0, 0)),
                             valid=valid, nh=hg, bq=bq, o_shape=(length, dil * gw))
    q_spec = pl.BlockSpec((bq, gw), lambda c, i, s: (nbr(i, s), col(c, 0)))
    k_spec = pl.BlockSpec((bq, gw), lambda c, i, s: (i, col(c, 1)))
    v_spec = pl.BlockSpec((bq, gw), lambda c, i, s: (i, col(c, 2)))
    stat = pl.BlockSpec((bq, gw), lambda c, i, s: (nbr(i, s), c))
    dk, dv = _band_bwd_dkv(f"b_attn_dkv_d{tag}", view, dov, lzv, dlv, bias, grid=(dil, nblk, 3),
                           q_spec=q_spec, k_spec=k_spec, v_spec=v_spec, stat_spec=stat,
                           b_spec=pl.BlockSpec((hg, None, bq, bq), lambda c, i, s: (0, 2 - s, 0, 0)),
                           o_spec=pl.BlockSpec((bq, gw), lambda c, i, s: (i, c)),
                           valid=valid, nh=hg, bq=bq, o_shape=(length, dil * gw))
    return dq.reshape(t, gw), dk.reshape(t, gw), dv.reshape(t, gw), dbias


def _band_bwd_dq(name, view, do, lse, dlt, bias, *, grid, q_spec, k_spec, v_spec, stat_spec, b_spec, valid,
                 nh, bq, o_shape):
    scale = HEAD_DIM ** -0.5
    bias_shape = (nh, 3, bq, bq)

    def body(q_ref, k_ref, v_ref, do_ref, lse_ref, dlt_ref, b_ref, dq_ref, db_ref, acc_s):
        step = pl.program_id(2)

        @pl.when((pl.program_id(0) == 0) & (pl.program_id(1) == 0) & (step == 0))
        def _():
            db_ref[...] = jnp.zeros_like(db_ref)

        @pl.when(step == 0)
        def _():
            acc_s[...] = jnp.zeros_like(acc_s)

        @pl.when(valid(pl.program_id(1), step))
        def _():
            for hd in range(nh):
                def add_bias_grad(rows, ds, hd=hd):
                    db_ref[hd, step, rows, :] += ds

                _, ds = _probs(q_ref, k_ref, v_ref, do_ref, lse_ref, dlt_ref, b_ref, hd, _hs(hd), bq, bq,
                               want_p=False, on_ds=add_bias_grad)
                acc_s[hd] += _dot(ds, k_ref[:, _hs(hd)], NN)

        @pl.when(step == 2)
        def _():
            for hd in range(nh):
                dq_ref[:, _hs(hd)] = (acc_s[hd] * scale).astype(BF16)

    return pl.pallas_call(
        body, name=name, grid=grid,
        in_specs=[q_spec, k_spec, v_spec, stat_spec, stat_spec, stat_spec, b_spec],
        out_specs=[stat_spec, pl.BlockSpec(bias_shape, lambda c, i, s: (0, 0, 0, 0))],
        out_shape=[jax.ShapeDtypeStruct(o_shape, BF16), jax.ShapeDtypeStruct(bias_shape, F32)],
        scratch_shapes=[pltpu.VMEM((nh, bq, LANES), F32)], compiler_params=_params(("arbitrary",) * 3),
    )(view, view, view, do, lse, dlt, bias)


def _band_bwd_dkv(name, view, do, lse, dlt, bias, *, grid, q_spec, k_spec, v_spec, stat_spec, b_spec, o_spec,
                  valid, nh, bq, o_shape):
    scale = HEAD_DIM ** -0.5

    def body(q_ref, k_ref, v_ref, do_ref, lse_ref, dlt_ref, b_ref, dk_ref, dv_ref, dk_s, dv_s):
        step = pl.program_id(2)

        @pl.when(step == 0)
        def _():
            dk_s[...] = jnp.zeros_like(dk_s)
            dv_s[...] = jnp.zeros_like(dv_s)

        @pl.when(valid(pl.program_id(1), step))
        def _():
            for hd in range(nh):
                p, ds = _probs(q_ref, k_ref, v_ref, do_ref, lse_ref, dlt_ref, b_ref, hd, _hs(hd), bq, bq)
                dv_s[hd] += _dot(p, do_ref[:, _hs(hd)], TN)
                dk_s[hd] += _dot(ds, q_ref[:, _hs(hd)], TN)

        @pl.when(step == 2)
        def _():
            for hd in range(nh):
                dk_ref[:, _hs(hd)] = (dk_s[hd] * scale).astype(BF16)
                dv_ref[:, _hs(hd)] = dv_s[hd].astype(BF16)

    acc = pltpu.VMEM((nh, bq, LANES), F32)
    return pl.pallas_call(
        body, name=name, grid=grid,
        in_specs=[q_spec, k_spec, v_spec, stat_spec, stat_spec, stat_spec, b_spec],
        out_specs=[o_spec, o_spec], out_shape=[jax.ShapeDtypeStruct(o_shape, BF16)] * 2,
        scratch_shapes=[acc, acc], compiler_params=_params(("parallel", "parallel", "arbitrary")),
    )(view, view, view, do, lse, dlt, bias)


def bias_bucket_sums(name, dbias, bucket):
    nh, _, bq, _ = dbias.shape
    db2 = dbias.reshape(nh, 3 * bq, bq)
    bk2 = bucket.reshape(3 * bq, bq)

    def body(db_ref, bk_ref, o_ref):
        row = lax.broadcasted_iota(jnp.int32, (nh, LANES), 0)
        lane = lax.broadcasted_iota(jnp.int32, (nh, LANES), 1)
        out = jnp.zeros((nh, LANES), F32)
        bkt = bk_ref[...]
        for hd in range(nh):
            x = db_ref[hd]
            for r in range(REL_BUCKETS):
                part = jnp.sum(jnp.where(bkt == r, x, 0.0), axis=1, keepdims=True)
                tot = jnp.sum(part, axis=0, keepdims=True)
                out = out + jnp.where((row == hd) & (lane == r), tot, 0.0)
        o_ref[...] = out

    return pl.pallas_call(
        body, name=name, out_shape=jax.ShapeDtypeStruct((nh, LANES), F32),
        compiler_params=pltpu.CompilerParams(vmem_limit_bytes=VMEM_LIMIT),
    )(db2, bk2)


def combine_fwd(name, outs, lzs):
    n_g = len(outs)
    t, gw = outs[0].shape
    tm = _tile(t, ROW_TILE)

    def body(*refs):
        o_refs, lz_refs, y_ref = refs[:n_g], refs[n_g:2 * n_g], refs[2 * n_g]
        lz = [r[...] for r in lz_refs]
        mx = functools.reduce(jnp.maximum, lz)
        e = [jnp.exp(x - mx) for x in lz]
        den = functools.reduce(lambda a, b: a + b, e)
        for g in range(n_g):
            y_ref[:, g * gw:(g + 1) * gw] = (e[g] / den * o_refs[g][...]).astype(BF16)

    return pl.pallas_call(
        body, name=name, grid=(t // tm,), in_specs=[_rows(gw, tm)] * (2 * n_g), out_specs=_rows(n_g * gw, tm),
        out_shape=jax.ShapeDtypeStruct((t, n_g * gw), BF16), compiler_params=_params(("parallel",)),
    )(*outs, *lzs)


def combine_bwd(name, dy, outs, lzs):
    n_g = len(outs)
    t, gw = outs[0].shape
    tm = _tile(t, ROW_TILE)
    nh = gw // HEAD_DIM

    def body(*refs):
        dy_ref = refs[0]
        o_refs, lz_refs = refs[1:1 + n_g], refs[1 + n_g:1 + 2 * n_g]
        do_refs, dl_refs = refs[1 + 2 * n_g:1 + 3 * n_g], refs[1 + 3 * n_g:]
        lz = [r[...] for r in lz_refs]
        mx = functools.reduce(jnp.maximum, lz)
        e = [jnp.exp(x - mx) for x in lz]
        den = functools.reduce(lambda a, b: a + b, e)
        wts = [x / den for x in e]
        for g in range(n_g):
            do_refs[g][...] = (wts[g] * dy_ref[:, g * gw:(g + 1) * gw]).astype(BF16)
        for hd in range(nh):
            mix = jnp.zeros((tm, HEAD_DIM), F32)
            for g in range(n_g):
                prod = dy_ref[:, g * gw + hd * HEAD_DIM:g * gw + (hd + 1) * HEAD_DIM] * o_refs[g][:, _hs(hd)]
                dw = jnp.broadcast_to(jnp.sum(prod, axis=-1, keepdims=True), (tm, HEAD_DIM))
                mix = mix + wts[g][:, _hs(hd)] * dw
            for g in range(n_g):
                dl_refs[g][:, _hs(hd)] = wts[g][:, _hs(hd)] * mix

    return pl.pallas_call(
        body, name=name, grid=(t // tm,),
        in_specs=[_rows(n_g * gw, tm)] + [_rows(gw, tm)] * (2 * n_g),
        out_specs=[_rows(gw, tm)] * (2 * n_g),
        out_shape=[jax.ShapeDtypeStruct((t, gw), BF16)] * n_g + [jax.ShapeDtypeStruct((t, gw), F32)] * n_g,
        compiler_params=_params(("parallel",)),
    )(dy, *outs, *lzs)


def _shifted(u):
    t = u.shape[0]
    row = lax.broadcasted_iota(jnp.int32, u.shape, 0)
    prev = jnp.where(row == 0, 0.0, pltpu.roll(u, 1, 0))
    nxt = jnp.where(row == t - 1, 0.0, pltpu.roll(u, t - 1, 0))
    return prev, nxt


def _conv3(u, prev, nxt, w_ref, b):
    return w_ref[0:1, :] * prev + w_ref[1:2, :] * u + w_ref[2:3, :] * nxt + b


def _conv3_t(d, w_ref):
    prev, nxt = _shifted(d)
    return w_ref[0:1, :] * nxt + w_ref[1:2, :] * d + w_ref[2:3, :] * prev


def conv_act_fwd(name, u2, cw2, cb2):
    _, t, dff = u2.shape
    tn = LANES

    def body(u_ref, w_ref, b_ref, o_ref):
        ug, uv = u_ref[0], u_ref[1]
        cg = _conv3(ug, *_shifted(ug), w_ref.at[0], b_ref[0])
        cv = _conv3(uv, *_shifted(uv), w_ref.at[1], b_ref[1])
        o_ref[...] = (cg * jax.nn.sigmoid(cg) * cv).astype(BF16)

    return pl.pallas_call(
        body, name=name, grid=(dff // tn,),
        in_specs=[pl.BlockSpec((2, t, tn), lambda j: (0, 0, j)), pl.BlockSpec((2, 3, tn), lambda j: (0, 0, j)),
                  pl.BlockSpec((2, 1, tn), lambda j: (0, 0, j))],
        out_specs=pl.BlockSpec((t, tn), lambda j: (0, j)), out_shape=jax.ShapeDtypeStruct((t, dff), BF16),
        compiler_params=_params(("parallel",)),
    )(u2, cw2, cb2)


def conv_act_bwd(name, u2, cw2, cb2, dact, after=()):
    _, t, dff = u2.shape
    tn = LANES

    def body(u_ref, w_ref, b_ref, d_ref, *rest):
        du_ref, dw_ref = rest[-2:]
        d = d_ref[...]
        ug, uv = u_ref[0], u_ref[1]
        shifted = (_shifted(ug), _shifted(uv))
        cg = _conv3(ug, *shifted[0], w_ref.at[0], b_ref[0])
        cv = _conv3(uv, *shifted[1], w_ref.at[1], b_ref[1])
        sg = jax.nn.sigmoid(cg)
        dcv = d * (cg * sg)
        dcg = d * cv * (sg * (1.0 + cg * (1.0 - sg)))
        du_ref[0] = _conv3_t(dcg, w_ref.at[0]).astype(BF16)
        du_ref[1] = _conv3_t(dcv, w_ref.at[1]).astype(BF16)
        for half, (dc, u) in enumerate(((dcg, ug), (dcv, uv))):
            prev, nxt = shifted[half]
            for tap, x in enumerate((prev, u, nxt)):
                dw_ref[half, tap:tap + 1, :] = jnp.sum(dc * x, axis=0, keepdims=True)
            dw_ref[half, 3:4, :] = jnp.sum(dc, axis=0, keepdims=True)
            dw_ref[half, 4:8, :] = jnp.zeros((4, tn), F32)

    return pl.pallas_call(
        body, name=name, grid=(dff // tn,),
        in_specs=[pl.BlockSpec((2, t, tn), lambda j: (0, 0, j)), pl.BlockSpec((2, 3, tn), lambda j: (0, 0, j)),
                  pl.BlockSpec((2, 1, tn), lambda j: (0, 0, j)), pl.BlockSpec((t, tn), lambda j: (0, j))]
        + [pl.BlockSpec(memory_space=pl.ANY)] * len(after),
        out_specs=[pl.BlockSpec((2, t, tn), lambda j: (0, 0, j)), pl.BlockSpec((2, 8, tn), lambda j: (0, 0, j))],
        out_shape=[jax.ShapeDtypeStruct((2, t, dff), BF16), jax.ShapeDtypeStruct((2, 8, dff), F32)],
        compiler_params=_params(("parallel",)),
    )(u2, cw2, cb2, dact, *after)


GATHER_ID, SIBLING_ID, CHIPS_ID = 0, 1, 2


def _place():
    x, y, c = lax.axis_index("x"), lax.axis_index("y"), lax.axis_index("c")
    chips = [(1 - x, y), (x, 1 - y), (1 - x, 1 - y)]
    return x, y, c, chips


def _handshake(peers):
    barrier = pltpu.get_barrier_semaphore()
    for peer in peers:
        pl.semaphore_signal(barrier, inc=1, device_id=peer, device_id_type=MESH)
    pl.semaphore_wait(barrier, len(peers))


def _sequencer(name, body, out_type, scratch_types, collective_id):
    return pl.kernel(body, out_type=out_type, mesh=plsc.ScalarSubcoreMesh(axis_name="seq", num_cores=1),
                     scratch_types=scratch_types, name=name,
                     compiler_params=pltpu.CompilerParams(collective_id=collective_id))


def _gather_body(n):
    def body(*refs):
        src, out = refs[:n], refs[n:2 * n]
        send, recv, loc = refs[2 * n:]
        x, y, c, chips = _place()
        sibling = (x, y, 1 - c)
        _handshake([sibling] + [(*chip, c) for chip in chips])

        def slot(a, px, py, pc):
            return out[a].at[4 * px + 2 * py + pc]

        def copy(a, k, block, to, from_src=False):
            return pltpu.make_async_remote_copy(
                src_ref=src[a] if from_src else slot(a, *block), dst_ref=slot(a, *block),
                send_sem=send.at[a, k], recv_sem=recv.at[a, k], device_id=to, device_id_type=MESH)

        mine = [pltpu.make_async_copy(src[a], slot(a, x, y, c), loc.at[a]) for a in range(n)]
        for cp in mine:
            cp.start()
        first = []
        for a in range(n):
            first.append(copy(a, 0, (x, y, c), sibling, True))
            first += [copy(a, 1 + j, (x, y, c), (*chip, c), True) for j, chip in enumerate(chips)]
        for cp in first:
            cp.start()
        passed = []
        for j, chip in enumerate(chips):
            for a in range(n):
                copy(a, 1 + j, (*chip, c), (x, y, c)).wait_recv()
                cp = copy(a, 4 + j, (*chip, c), sibling)
                cp.start()
                passed.append(cp)
        for a in range(n):
            copy(a, 0, sibling, (x, y, c)).wait_recv()
            for j, chip in enumerate(chips):
                copy(a, 4 + j, (*chip, 1 - c), (x, y, c)).wait_recv()
        for cp in first + passed:
            cp.wait_send()
        for cp in mine:
            cp.wait()

    return body


def gather_layer(name, shards):
    n = len(shards)
    out_type = [jax.ShapeDtypeStruct((N_DEV,) + s.shape, s.dtype) for s in shards]
    scratch = [pltpu.SemaphoreType.DMA((n, 7)), pltpu.SemaphoreType.DMA((n, 7)), pltpu.SemaphoreType.DMA((n,))]
    return _sequencer(name, _gather_body(n), out_type, scratch, GATHER_ID)(*shards)


def _to_sibling_body(n):
    def body(*refs):
        src, got = refs[:n], refs[n:2 * n]
        send, recv = refs[2 * n:]
        x, y, c, _ = _place()
        sibling = (x, y, 1 - c)
        _handshake([sibling])
        remote = []
        for a in range(n):
            for q in range(4):
                remote.append(pltpu.make_async_remote_copy(
                    src_ref=src[a].at[2 * q + 1 - c], dst_ref=got[a].at[q], send_sem=send.at[a, q],
                    recv_sem=recv.at[a, q], device_id=sibling, device_id_type=MESH))
        for cp in remote:
            cp.start()
        for cp in remote:
            cp.wait()

    return body


def grads_to_sibling(name, grads):
    n = len(grads)
    out_type = [jax.ShapeDtypeStruct((4,) + g.shape[1:], g.dtype) for g in grads]
    scratch = [pltpu.SemaphoreType.DMA((n, 4)), pltpu.SemaphoreType.DMA((n, 4))]
    return _sequencer(name, _to_sibling_body(n), out_type, scratch, SIBLING_ID)(*grads)


def _to_chips_body(n):
    def body(*refs):
        src, got = refs[:n], refs[n:2 * n]
        send, recv = refs[2 * n:]
        x, y, c, chips = _place()
        _handshake([(*chip, c) for chip in chips])
        remote = []
        for a in range(n):
            for j, (px, py) in enumerate(chips):
                remote.append(pltpu.make_async_remote_copy(
                    src_ref=src[a].at[2 * px + py], dst_ref=got[a].at[j], send_sem=send.at[a, j],
                    recv_sem=recv.at[a, j], device_id=(px, py, c), device_id_type=MESH))
        for cp in remote:
            cp.start()
        for cp in remote:
            cp.wait()

    return body


def grads_to_chips(name, parts):
    n = len(parts)
    out_type = [jax.ShapeDtypeStruct((3,) + p.shape[1:], p.dtype) for p in parts]
    scratch = [pltpu.SemaphoreType.DMA((n, 3)), pltpu.SemaphoreType.DMA((n, 3))]
    return _sequencer(name, _to_chips_body(n), out_type, scratch, CHIPS_ID)(*parts)


def all_reduce_small(name, vec):
    rows, m = vec.shape

    def body(x_ref, o_ref, buf, send, recv):
        x, y, c, chips = _place()
        sibling = (x, y, 1 - c)

        def blk(px, py, pc):
            return buf.at[pl.ds(pl.multiple_of((4 * px + 2 * py + pc) * rows, rows), rows), :]

        def copy(k, block, to):
            return pltpu.make_async_remote_copy(src_ref=blk(*block), dst_ref=blk(*block), send_sem=send.at[k],
                                                recv_sem=recv.at[k], device_id=to, device_id_type=MESH)

        blk(x, y, c)[...] = x_ref[...]
        first = [copy(0, (x, y, c), sibling)] + [copy(1 + j, (x, y, c), (*chip, c)) for j, chip in enumerate(chips)]
        for cp in first:
            cp.start()
        passed = [copy(4 + j, (*chip, c), sibling) for j, chip in enumerate(chips)]
        for j, chip in enumerate(chips):
            copy(1 + j, (*chip, c), (x, y, c)).wait_recv()
            passed[j].start()
        copy(0, sibling, (x, y, c)).wait_recv()
        for j, chip in enumerate(chips):
            copy(4 + j, (*chip, 1 - c), (x, y, c)).wait_recv()
        for cp in first + passed:
            cp.wait_send()
        tot = buf[0:rows, :]
        for dev in range(1, N_DEV):
            tot = tot + buf[dev * rows:(dev + 1) * rows, :]
        o_ref[...] = tot

    return pl.pallas_call(
        body, name=name, in_specs=[pl.BlockSpec(memory_space=pltpu.VMEM)],
        out_specs=pl.BlockSpec(memory_space=pltpu.VMEM), out_shape=jax.ShapeDtypeStruct((rows, m), F32),
        scratch_shapes=[pltpu.VMEM((N_DEV * rows, m), F32), pltpu.SemaphoreType.DMA((7,)),
                        pltpu.SemaphoreType.DMA((7,))],
        compiler_params=pltpu.CompilerParams(vmem_limit_bytes=VMEM_LIMIT),
    )(vec)


def _ew_tiles(rows, cols, max_elems=1 << 18):
    tr = rows
    for cand in (1024, 512, 256, 128, 64, 32, 16):
        if rows % cand == 0 and cand * cols <= max_elems:
            tr = cand
            break
    return tr


def chip_sum(name, full, got, core):
    _, kdim, ncol = full.shape
    tr = _ew_tiles(kdim, ncol, max_elems=1 << 20)
    blk = (None, tr, ncol)
    by_chip = pl.BlockSpec(blk, lambda q, i, c: (q, i, 0))

    def body(c_ref, a_ref, b_ref, o_ref):
        o_ref[...] = (a_ref[...].astype(F32) + b_ref[...].astype(F32)).astype(BF16)

    return pl.pallas_call(
        body, name=name,
        grid_spec=pltpu.PrefetchScalarGridSpec(
            num_scalar_prefetch=1, grid=(4, kdim // tr),
            in_specs=[pl.BlockSpec(blk, lambda q, i, c: (2 * q + c[0], i, 0)), by_chip], out_specs=by_chip),
        out_shape=jax.ShapeDtypeStruct((4, kdim, ncol), BF16),
        compiler_params=_params(("parallel", "parallel")),
    )(core, full, got)


def _adamw_math(w, g, m, v):
    m = ADAM_B1 * m + (1.0 - ADAM_B1) * g
    v = ADAM_B2 * v + (1.0 - ADAM_B2) * (g * g)
    m_hat = m / (1.0 - ADAM_B1 ** ADAM_STEP)
    v_hat = v / (1.0 - ADAM_B2 ** ADAM_STEP)
    delta = -ADAM_LR * (m_hat / (jnp.sqrt(v_hat) + ADAM_EPS) + ADAM_WD * w)
    return delta, m, v


def adamw_layer(name, sums, got, w, m, v, outs, layer, chip):
    _, kdim, ncol = sums.shape
    tr = _ew_tiles(kdim, ncol)
    mine = pl.BlockSpec((None, tr, ncol), lambda i, q: (q[0], i, 0))
    others = pl.BlockSpec((3, tr, ncol), lambda i, q: (0, i, 0))
    param = pl.BlockSpec((None, tr, ncol), lambda i, q: (layer, i, 0))
    whole = pl.BlockSpec(memory_space=pl.ANY)

    def body(q_ref, o_ref, g_ref, w_ref, m_ref, v_ref, *rest):
        go_ref, d_ref, mo_ref, vo_ref = rest[-4:]
        g = o_ref[...].astype(F32)
        for j in range(3):
            g = g + g_ref[j].astype(F32)
        d, mn, vn = _adamw_math(w_ref[...], g, m_ref[...], v_ref[...])
        go_ref[...] = g
        d_ref[...] = d
        mo_ref[...] = mn
        vo_ref[...] = vn

    n_in = 6
    return pl.pallas_call(
        body, name=name,
        grid_spec=pltpu.PrefetchScalarGridSpec(
            num_scalar_prefetch=1, grid=(kdim // tr,),
            in_specs=[mine, others, param, param, param] + [whole] * 4, out_specs=[param] * 4),
        out_shape=[jax.ShapeDtypeStruct(w.shape, F32)] * 4,
        input_output_aliases={n_in + k: k for k in range(4)},
        compiler_params=_params(("parallel",)),
    )(chip, sums, got, w, m, v, *outs)


def adamw_small(name, g, w, m, v):
    def body(g_ref, w_ref, m_ref, v_ref, d_ref, mo_ref, vo_ref):
        d, mn, vn = _adamw_math(w_ref[...], g_ref[...], m_ref[...], v_ref[...])
        d_ref[...] = d
        mo_ref[...] = mn
        vo_ref[...] = vn

    vm = pl.BlockSpec(memory_space=pltpu.VMEM)
    return pl.pallas_call(
        body, name=name, in_specs=[vm] * 4, out_specs=[vm] * 3,
        out_shape=[jax.ShapeDtypeStruct(g.shape, F32)] * 3,
        compiler_params=pltpu.CompilerParams(vmem_limit_bytes=VMEM_LIMIT),
    )(g, w, m, v)


def _pack(parts, width):
    flat = jnp.concatenate([p.reshape(-1).astype(F32) for p in parts])
    pad = (-flat.shape[0]) % width
    return jnp.pad(flat, (0, pad)).reshape(-1, width) if pad else flat.reshape(-1, width)


def _unpack(packed, shapes):
    flat = packed.reshape(-1)
    out, off = [], 0
    for s in shapes:
        size = math.prod(s)
        out.append(flat[off:off + size].reshape(s))
        off += size
    return out


def _local_step(h, target, layers, params, on_grads=None):
    a_q_gain, a_k_gain, rel_bias, mix_norm, ffn_norm, conv_b, final_norm = params
    t, d = h.shape
    depth = len(layers)
    n_groups = len(B_GROUPS)
    hg = B_HEADS_PER_GROUP
    n_kv = A_KV_HEADS
    w_a, w_b, w_u = layers[0][0].shape[2], layers[1][0].shape[2], layers[0][2].shape[2]
    n_q = w_a * N_DEV // HEAD_DIM - 2 * n_kv
    dff = layers[0][3].shape[0]
    n_a = (depth + 1) // 2
    cb_full = conv_b.reshape(depth, 2, 1, dff)

    cos, sin = rope_tables(t)
    strides = [band_stride(t, win, dil) for win, dil in B_GROUPS]
    tables = [band_tables(rel_bias[:, g * hg:(g + 1) * hg], win, dil, strides[g], band_block(t, strides[g]))
              for g, (win, dil) in enumerate(B_GROUPS)]

    saved = []
    for i in range(depth):
        j = i // 2
        w_qkv, w_o, w_up_i, w_down_i, cw = layers[i]
        s = {"h_in": h}
        hn = rms_fwd("mix_norm_fwd", h, mix_norm[i])
        s["hn"] = hn
        if i % 2 == 0:
            qkv = mm_col_fwd("a_qkv_fwd", hn, w_qkv, F32)
            qkv_r = qk_prep_fwd("a_qk_prep_fwd", qkv, a_q_gain[j], a_k_gain[j], cos, sin, n_q, n_kv)
            o, lse = mixer_a_fwd(qkv_r, n_q, n_kv)
            s.update(qkv=qkv, qkv_r=qkv_r, o=o, lse=lse)
            h = mm_row_fwd("a_out_fwd", o, w_o, h)
        else:
            qkv = mm_col_fwd("b_qkv_fwd", hn, w_qkv, BF16)
            outs, lzs = [], []
            for g, (win, dil) in enumerate(B_GROUPS):
                o_g, lz_g = mixer_b_group_fwd(qkv, tables[g][0], strides[g], g, n_groups, dil)
                outs.append(o_g)
                lzs.append(lz_g)
            y = combine_fwd("b_combine_fwd", outs, lzs)
            s.update(qkv=qkv, outs=outs, lzs=lzs, y=y)
            h = mm_row_fwd("b_out_fwd", y, w_o, h)
        s["h_mid"] = h
        hn2 = rms_fwd("ffn_norm_fwd", h, ffn_norm[i])
        u2 = mm_col_fwd("ffn_up_fwd", hn2, w_up_i, F32, split=2)
        act = conv_act_fwd("ffn_conv_act_fwd", u2, cw, cb_full[i])
        s.update(hn2=hn2, u2=u2, act=act)
        h = mm_row_fwd("ffn_down_fwd", act, w_down_i, h)
        saved.append(s)

    dh, dh_b, d_final, loss_part = loss_head("loss_head", h, final_norm, target)

    d_mix, d_ffn, d_cw, d_cb = [None] * depth, [None] * depth, [None] * depth, [None] * depth
    d_qg, d_kg = [None] * n_a, [None] * n_a
    d_rel = jnp.zeros((n_groups * hg, LANES), F32)
    layer_grads = [{} for _ in range(depth)]
    pending = []

    def settle():
        done = []
        while pending:
            i_p, part_p, finish = pending.pop()
            layer_grads[i_p][part_p] = finish()
            done += [upd[0] for upd in layer_grads[i_p][part_p]]
        return done

    early = []

    def register(i_p, part_p, grads):
        if on_grads is None:
            layer_grads[i_p][part_p] = grads
        else:
            first, finish = on_grads(i_p, part_p, grads)
            early.extend(first)
            pending.append((i_p, part_p, finish))

    def take_early():
        first = tuple(early)
        early.clear()
        return first

    for i in reversed(range(depth)):
        j = i // 2
        w_qkv, w_o, w_up_i, w_down_i, cw = layers[i]
        s = saved[i]
        dact = mm_row_dx("ffn_down_dx", dh_b, w_down_i)
        g_down = mm_row_dw("ffn_down_dw", s["act"], dh_b)
        du2, dcw = conv_act_bwd("ffn_conv_act_bwd", s["u2"], cw, cb_full[i], dact, take_early())
        d_cw[i] = dcw[:, 0:3, :].transpose(1, 0, 2).reshape(3, 2 * dff)
        d_cb[i] = dcw[:, 3, :].reshape(2 * dff)
        g_up = mm_col_dw("ffn_up_dw", s["hn2"], du2, w_u, split=2)
        dhn2 = mm_col_dx("ffn_up_dx", du2, w_up_i, split=2)
        dh, dh_b, d_ffn[i] = rms_bwd("ffn_norm_bwd", s["h_mid"], ffn_norm[i], dhn2, dh, settle())
        register(i, "ffn", [g_up, g_down.reshape(N_DEV, -1, d)])
        if i % 2 == 0:
            do = mm_row_dx("a_out_dx", dh_b, w_o, take_early())
            g_o = mm_row_dw("a_out_dw", s["o"], dh_b)
            dlt, do_b = row_delta("a_delta", do, s["o"], n_q)
            dq, dk, dv = mixer_a_bwd(s["qkv_r"], do_b, s["lse"], dlt, n_q, n_kv)
            dqkv, dgain = qk_prep_bwd("a_qk_prep_bwd", s["qkv"], dq, dk, dv, a_q_gain[j], a_k_gain[j], cos, sin,
                                      n_q, n_kv)
            d_qg[j], d_kg[j] = dgain[0], dgain[1]
            g_qkv = mm_col_dw("a_qkv_dw", s["hn"], dqkv, w_a)
            dhn = mm_col_dx("a_qkv_dx", dqkv, w_qkv)
        else:
            dy = mm_row_dx("b_out_dx", dh_b, w_o, take_early())
            g_o = mm_row_dw("b_out_dw", s["y"], dh_b)
            res = combine_bwd("b_combine_bwd", dy, s["outs"], s["lzs"])
            dos, dlts = res[:n_groups], res[n_groups:]
            pieces, rel_rows = [], []
            for g, (win, dil) in enumerate(B_GROUPS):
                dq, dk, dv, dbias = mixer_b_group_bwd(s["qkv"], tables[g][0], dos[g], s["lzs"][g], dlts[g],
                                                      strides[g], g, n_groups, dil)
                pieces += [dq, dk, dv]
                rel_rows.append(bias_bucket_sums(f"b_bias_sums_d{dil}", dbias, tables[g][1]))
            d_rel = d_rel + jnp.concatenate(rel_rows, axis=0)
            dqkv = jnp.concatenate(pieces, axis=1)
            g_qkv = mm_col_dw("b_qkv_dw", s["hn"], dqkv, w_b)
            dhn = mm_col_dx("b_qkv_dx", dqkv, w_qkv)
        dh, dh_b, d_mix[i] = rms_bwd("mix_norm_bwd", s["h_in"], mix_norm[i], dhn, dh, settle())
        register(i, "mix", [g_qkv, g_o.reshape(N_DEV, -1, d)])
    last = pending.pop()[2] if pending else None

    d_rel_bias = d_rel[:, :REL_BUCKETS].T
    small_g = [jnp.stack(d_qg), jnp.stack(d_kg), d_rel_bias, jnp.concatenate(d_mix, 0), jnp.concatenate(d_ffn, 0),
               jnp.stack(d_cb), d_final.reshape(-1), jnp.stack(d_cw), loss_part]
    return dh, layer_grads, small_g, last


def kernel(x, a_w_qkv, a_w_o, a_q_gain, a_k_gain, b_w_qkv, b_w_o, rel_bias, mix_norm, ffn_norm, w_up, conv_w, conv_b, w_down, final_norm, loss_target, m_a_w_qkv, m_a_w_o, m_a_q_gain, m_a_k_gain, m_b_w_qkv, m_b_w_o, m_rel_bias, m_mix_norm, m_ffn_norm, m_w_up, m_conv_w, m_conv_b, m_w_down, m_final_norm, v_a_w_qkv, v_a_w_o, v_a_q_gain, v_a_k_gain, v_b_w_qkv, v_b_w_o, v_rel_bias, v_mix_norm, v_ffn_norm, v_w_up, v_conv_w, v_conv_b, v_w_down, v_final_norm):
    d = x.shape[2]
    depth = mix_norm.shape[0]
    dff = w_down.shape[1] * N_DEV
    w_u = w_up.shape[2]
    mixers = [(a_w_qkv, a_w_o, m_a_w_qkv, m_a_w_o, v_a_w_qkv, v_a_w_o),
              (b_w_qkv, b_w_o, m_b_w_qkv, m_b_w_o, v_b_w_qkv, v_b_w_o)]

    layers = []
    for i in range(depth):
        w_qkv, w_o = mixers[i % 2][0][i // 2], mixers[i % 2][1][i // 2]
        shards = [w_qkv.astype(BF16), w_o.astype(BF16), w_up[i].astype(BF16), w_down[i].astype(BF16), conv_w[i]]
        if i == 0:
            (g_qkv,) = gather_layer("gather_l0_qkv", shards[:1])
            g_o, g_up, g_down, g_cw = gather_layer("gather_l0", shards[1:])
        else:
            g_qkv, g_o, g_up, g_down, g_cw = gather_layer(f"gather_l{i}", shards)
        cw = g_cw.transpose(1, 0, 2).reshape(3, 2, dff).transpose(1, 0, 2)
        layers.append((g_qkv, g_o.reshape(-1, d), g_up, g_down.reshape(dff, d), cw))

    core = lax.axis_index("c").astype(jnp.int32).reshape(1)
    chip = (2 * lax.axis_index("x") + lax.axis_index("y")).astype(jnp.int32).reshape(1)

    def reduce_and_update(i, part, grads):
        w_qkv, w_o, m_qkv, m_o, v_qkv, v_o = mixers[i % 2]
        prefix = ("a_w_", "b_w_")[i % 2]
        state = {"mix": [(prefix + "qkv", w_qkv, m_qkv, v_qkv, i // 2), (prefix + "o", w_o, m_o, v_o, i // 2)],
                 "ffn": [("w_up", w_up, m_w_up, v_w_up, i), ("w_down", w_down, m_w_down, v_w_down, i)]}[part]
        got1 = grads_to_sibling(f"to_sibling_l{i}_{part}", grads)
        sums = [chip_sum(f"chip_sum_l{i}_{part}{a}", grads[a], got1[a], core) for a in range(2)]
        got2 = grads_to_chips(f"to_chips_l{i}_{part}", sums)

        def finish():
            for a, (key, w, m, v, layer) in enumerate(state):
                outs = big_out.get(key) or [lax.empty(w.shape, F32) for _ in range(4)]
                big_out[key] = adamw_layer(f"adamw_l{i}_{part}{a}", sums[a], got2[a], w, m, v, outs, layer, chip)
            return [big_out[key] for key, *_ in state]

        return sums, finish

    big_out = {}
    dh, _, small_g, last = _local_step(x[0], loss_target[0], layers,
                                       (a_q_gain, a_k_gain, rel_bias, mix_norm, ffn_norm, conv_b, final_norm),
                                       reduce_and_update)
    grad_x = dh[None]

    width = 2048
    packed = _pack(small_g, N_DEV * width).reshape(-1, N_DEV, width)
    n_rows = packed.shape[0]
    packed = packed.transpose(1, 0, 2).reshape(N_DEV, n_rows * width)
    red = all_reduce_small("small_all_reduce", packed)
    last()
    red = red.reshape(N_DEV, n_rows, width).transpose(1, 0, 2)
    (g_qg, g_kg, g_rel, g_mix, g_ffn, g_cb, g_fin, g_cw_all, loss) = _unpack(red, [p.shape for p in small_g])
    idx = 4 * lax.axis_index("x") + 2 * lax.axis_index("y") + lax.axis_index("c")
    g_cw_mine = lax.dynamic_slice_in_dim(g_cw_all, idx * w_u, w_u, axis=2)

    small_w = [a_q_gain, a_k_gain, rel_bias, mix_norm, ffn_norm, conv_b, final_norm, conv_w]
    small_m = [m_a_q_gain, m_a_k_gain, m_rel_bias, m_mix_norm, m_ffn_norm, m_conv_b, m_final_norm, m_conv_w]
    small_v = [v_a_q_gain, v_a_k_gain, v_rel_bias, v_mix_norm, v_ffn_norm, v_conv_b, v_final_norm, v_conv_w]
    small_grads = [g_qg, g_kg, g_rel, g_mix, g_ffn, g_cb, g_fin, g_cw_mine]
    shapes = [w.shape for w in small_w]
    pad_rows = (-_pack(small_w, width).shape[0]) % 8

    def pk8(parts):
        p = _pack(parts, width)
        return jnp.pad(p, ((0, pad_rows), (0, 0))) if pad_rows else p

    sd, sm, sv = adamw_small("adamw_small", pk8(small_grads), pk8(small_w), pk8(small_m), pk8(small_v))
    sd, sm, sv = _unpack(sd, shapes), _unpack(sm, shapes), _unpack(sv, shapes)

    names = ["a_w_qkv", "a_w_o", "a_q_gain", "a_k_gain", "b_w_qkv", "b_w_o", "rel_bias", "mix_norm", "ffn_norm",
             "w_up", "conv_w", "conv_b", "w_down", "final_norm"]
    small_names = ["a_q_gain", "a_k_gain", "rel_bias", "mix_norm", "ffn_norm", "conv_b", "final_norm", "conv_w"]
    grads, deltas, new_m, new_v = {}, {}, {}, {}
    for nm, outs in big_out.items():
        grads[nm], deltas[nm], new_m[nm], new_v[nm] = outs
    for a, nm in enumerate(small_names):
        grads[nm] = small_grads[a].reshape(shapes[a])
        deltas[nm], new_m[nm], new_v[nm] = sd[a], sm[a], sv[a]
    return (loss.reshape(()), grad_x, *[grads[n] for n in names], *[deltas[n] for n in names],
            *[new_m[n] for n in names], *[new_v[n] for n in names])
```

```python
import functools
import math

import jax
import jax.numpy as jnp
from jax import lax
from jax.experimental import pallas as pl
from jax.experimental.pallas import tpu as pltpu
from jax.experimental.pallas import tpu_sc as plsc

F32 = jnp.float32
BF16 = jnp.bfloat16
MESH = pl.DeviceIdType.MESH

N_DEV = 8
LANES = 128
HEAD_DIM = 128
VMEM_LIMIT = 56 * 1024 * 1024
GRID_W = 64
ROPE_THETA = 10000.0
A_KV_HEADS = 4
B_GROUPS = ((128, 1), (512, 4), (2048, 16))
B_HEADS_PER_GROUP = 8
REL_BUCKETS = 32
REL_MAX_DISTANCE = 1024
EPS = 1e-6
NEG_INF = -1e30
ADAM_LR = 0.001
ADAM_B1 = 0.9
ADAM_B2 = 0.999
ADAM_EPS = 1e-08
ADAM_WD = 0.01
ADAM_STEP = 10

ROW_TILE = 256
MM_TM = 1024
MM_TK = 2048
A_BQ = 1024
A_BK = 1024
B_BQ = 256
ATTN_ROWS = 16
ATTN_SCALE = HEAD_DIM ** -0.5

NN = (((1,), (0,)), ((), ()))
NT = (((1,), (1,)), ((), ()))
TN = (((0,), (0,)), ((), ()))


def _tile(n, pref):
    return pref if n % pref == 0 else n


def _div_tile(n, pref):
    for cand in range(pref - pref % LANES, 0, -LANES):
        if n % cand == 0:
            return cand
    return n


def _params(sem):
    return pltpu.CompilerParams(dimension_semantics=sem, vmem_limit_bytes=VMEM_LIMIT)


def _dot(a, b, dims):
    return lax.dot_general(a, b, dims, preferred_element_type=F32)


def _mm(name, a, b, *, grid, a_blk, a_map, b_blk, b_map, o_blk, o_map, out_shape, out_dtype, dims,
        res=None, after=()):
    nk = grid[2]
    acc_shape = tuple(d for d in o_blk if d is not None)

    def body(*refs):
        a_ref, b_ref = refs[:2]
        r_ref = None if res is None else refs[2]
        if nk == 1:
            o_ref = refs[-1]
            part = _dot(a_ref[...].astype(BF16), b_ref[...].astype(BF16), dims)
            o_ref[...] = (part if r_ref is None else part + r_ref[...]).astype(out_dtype)
            return
        o_ref, acc = refs[-2:]
        k = pl.program_id(2)

        @pl.when(k == 0)
        def _():
            acc[...] = jnp.zeros_like(acc)

        acc[...] += _dot(a_ref[...].astype(BF16), b_ref[...].astype(BF16), dims)

        @pl.when(k == nk - 1)
        def _():
            r = acc[...]
            if r_ref is not None:
                r = r + r_ref[...]
            o_ref[...] = r.astype(out_dtype)

    in_specs = [pl.BlockSpec(a_blk, a_map), pl.BlockSpec(b_blk, b_map)]
    args = [a, b]
    if res is not None:
        in_specs.append(pl.BlockSpec(o_blk, o_map))
        args.append(res)
    in_specs += [pl.BlockSpec(memory_space=pl.ANY)] * len(after)
    args += list(after)
    return pl.pallas_call(
        body, name=name, grid=grid, in_specs=in_specs, out_specs=pl.BlockSpec(o_blk, o_map),
        out_shape=jax.ShapeDtypeStruct(out_shape, out_dtype),
        scratch_shapes=[] if nk == 1 else [pltpu.VMEM(acc_shape, F32)],
        compiler_params=_params(("parallel", "parallel", "arbitrary")),
    )(*args)


def mm_col_fwd(name, a, wg, out_dtype, split=1):
    m, kdim = a.shape
    n_dev, _, w = wg.shape
    tm, tk = _tile(m, MM_TM), _div_tile(kdim, MM_TK)
    per = n_dev // split
    if split == 1:
        o_blk, o_map, o_shape = (tm, w), (lambda i, j, k: (i, j)), (m, n_dev * w)
    else:
        o_blk, o_map, o_shape = (None, tm, w), (lambda i, j, k: (j // per, i, j % per)), (split, m, per * w)
    return _mm(name, a, wg, grid=(m // tm, n_dev, kdim // tk),
               a_blk=(tm, tk), a_map=lambda i, j, k: (i, k),
               b_blk=(None, tk, w), b_map=lambda i, j, k: (j, k, 0),
               o_blk=o_blk, o_map=o_map, out_shape=o_shape, out_dtype=out_dtype, dims=NN)


def mm_col_dx(name, dy, wg, split=1):
    n_dev, kdim, w = wg.shape
    m = dy.shape[-2]
    tm, tk = _tile(m, MM_TM), _div_tile(kdim, MM_TK)
    per = n_dev // split
    if split == 1:
        a_blk, a_map = (tm, w), (lambda i, j, k: (i, k))
    else:
        a_blk, a_map = (None, tm, w), (lambda i, j, k: (k // per, i, k % per))
    return _mm(name, dy, wg, grid=(m // tm, kdim // tk, n_dev),
               a_blk=a_blk, a_map=a_map,
               b_blk=(None, tk, w), b_map=lambda i, j, k: (k, j, 0),
               o_blk=(tm, tk), o_map=lambda i, j, k: (i, j), out_shape=(m, kdim), out_dtype=F32, dims=NT)


def mm_col_dw(name, x, dy, w, split=1):
    m, kdim = x.shape
    tm, tk = _tile(m, MM_TM), _div_tile(kdim, MM_TK)
    per = N_DEV // split
    if split == 1:
        b_blk, b_map = (tm, w), (lambda i, j, k: (k, j))
    else:
        b_blk, b_map = (None, tm, w), (lambda i, j, k: (j // per, k, j % per))
    return _mm(name, x, dy, grid=(kdim // tk, N_DEV, m // tm),
               a_blk=(tm, tk), a_map=lambda i, j, k: (k, i),
               b_blk=b_blk, b_map=b_map,
               o_blk=(None, tk, w), o_map=lambda i, j, k: (j, i, 0),
               out_shape=(N_DEV, kdim, w), out_dtype=BF16, dims=TN)


def mm_row_fwd(name, a, wg, res):
    m, kdim = a.shape
    n = wg.shape[1]
    tm, tk, tn = _tile(m, MM_TM), _div_tile(kdim, MM_TK), _tile(n, 1024)
    return _mm(name, a, wg, grid=(m // tm, n // tn, kdim // tk),
               a_blk=(tm, tk), a_map=lambda i, j, k: (i, k),
               b_blk=(tk, tn), b_map=lambda i, j, k: (k, j),
               o_blk=(tm, tn), o_map=lambda i, j, k: (i, j), out_shape=(m, n), out_dtype=F32, dims=NN,
               res=res)


def mm_row_dx(name, dy, wg, after=()):
    m, n = dy.shape
    kdim = wg.shape[0]
    tm, tk, tn = _tile(m, MM_TM), _div_tile(kdim, MM_TK), _tile(n, MM_TK)
    return _mm(name, dy, wg, grid=(m // tm, kdim // tk, n // tn),
               a_blk=(tm, tn), a_map=lambda i, j, k: (i, k),
               b_blk=(tk, tn), b_map=lambda i, j, k: (j, k),
               o_blk=(tm, tk), o_map=lambda i, j, k: (i, j), out_shape=(m, kdim), out_dtype=F32, dims=NT,
               after=after)


def mm_row_dw(name, x, dy):
    m, kdim = x.shape
    n = dy.shape[1]
    tm, tk, tn = _tile(m, MM_TM), _div_tile(kdim, MM_TK), _tile(n, 1024)
    return _mm(name, x, dy, grid=(kdim // tk, n // tn, m // tm),
               a_blk=(tm, tk), a_map=lambda i, j, k: (k, i),
               b_blk=(tm, tn), b_map=lambda i, j, k: (k, j),
               o_blk=(tk, tn), o_map=lambda i, j, k: (i, j), out_shape=(kdim, n), out_dtype=BF16, dims=TN)


def _rows(d, tm):
    return pl.BlockSpec((tm, d), lambda i: (i, 0))


def _vec(d):
    return pl.BlockSpec((1, d), lambda i: (0, 0))


def rms_fwd(name, h, gain):
    t, d = h.shape
    tm = _tile(t, ROW_TILE)

    def body(h_ref, g_ref, o_ref):
        x = h_ref[...]
        rstd = lax.rsqrt(jnp.mean(x * x, axis=-1, keepdims=True) + EPS)
        o_ref[...] = (x * rstd * g_ref[...]).astype(BF16)

    return pl.pallas_call(
        body, name=name, grid=(t // tm,), in_specs=[_rows(d, tm), _vec(d)], out_specs=_rows(d, tm),
        out_shape=jax.ShapeDtypeStruct((t, d), BF16), compiler_params=_params(("parallel",)),
    )(h, gain.reshape(1, d))


def rms_bwd(name, h, gain, dy, dres, after=()):
    t, d = h.shape
    tm = _tile(t, ROW_TILE)

    def body(h_ref, g_ref, dy_ref, r_ref, *rest):
        dh_ref, dhb_ref, dg_ref = rest[-3:]

        @pl.when(pl.program_id(0) == 0)
        def _():
            dg_ref[...] = jnp.zeros_like(dg_ref)

        x = h_ref[...]
        rstd = lax.rsqrt(jnp.mean(x * x, axis=-1, keepdims=True) + EPS)
        xhat = x * rstd
        dyv = dy_ref[...]
        dxhat = dyv * g_ref[...]
        dh = r_ref[...] + rstd * (dxhat - xhat * jnp.mean(dxhat * xhat, axis=-1, keepdims=True))
        dh_ref[...] = dh
        dhb_ref[...] = dh.astype(BF16)
        dg_ref[...] += jnp.sum(dyv * xhat, axis=0, keepdims=True)

    return pl.pallas_call(
        body, name=name, grid=(t // tm,),
        in_specs=[_rows(d, tm), _vec(d), _rows(d, tm), _rows(d, tm)]
        + [pl.BlockSpec(memory_space=pl.ANY)] * len(after),
        out_specs=[_rows(d, tm), _rows(d, tm), _vec(d)],
        out_shape=[jax.ShapeDtypeStruct((t, d), F32), jax.ShapeDtypeStruct((t, d), BF16),
                   jax.ShapeDtypeStruct((1, d), F32)],
        compiler_params=_params(("arbitrary",)),
    )(h, gain.reshape(1, d), dy, dres, *after)


def loss_head(name, h, gain, target):
    t, d = h.shape
    tm = _tile(t, ROW_TILE)

    def body(h_ref, g_ref, t_ref, dh_ref, dhb_ref, dg_ref, loss_ref):
        @pl.when(pl.program_id(0) == 0)
        def _():
            dg_ref[...] = jnp.zeros_like(dg_ref)
            loss_ref[...] = jnp.zeros_like(loss_ref)

        x = h_ref[...]
        rstd = lax.rsqrt(jnp.mean(x * x, axis=-1, keepdims=True) + EPS)
        xhat = x * rstd
        err = xhat * g_ref[...] - t_ref[...]
        row = jnp.mean(err * err, axis=-1, keepdims=True)
        loss_ref[...] += 0.5 * jnp.sum(row, axis=0, keepdims=True)
        dyv = err * (1.0 / d)
        dxhat = dyv * g_ref[...]
        dh = rstd * (dxhat - xhat * jnp.mean(dxhat * xhat, axis=-1, keepdims=True))
        dh_ref[...] = dh
        dhb_ref[...] = dh.astype(BF16)
        dg_ref[...] += jnp.sum(dyv * xhat, axis=0, keepdims=True)

    return pl.pallas_call(
        body, name=name, grid=(t // tm,),
        in_specs=[_rows(d, tm), _vec(d), _rows(d, tm)],
        out_specs=[_rows(d, tm), _rows(d, tm), _vec(d), pl.BlockSpec((1, 1), lambda i: (0, 0))],
        out_shape=[jax.ShapeDtypeStruct((t, d), F32), jax.ShapeDtypeStruct((t, d), BF16),
                   jax.ShapeDtypeStruct((1, d), F32), jax.ShapeDtypeStruct((1, 1), F32)],
        compiler_params=_params(("arbitrary",)),
    )(h, gain.reshape(1, d), target)


def rope_tables(seq):
    pos = jnp.arange(seq, dtype=jnp.int32)
    row_ids = (pos // GRID_W).astype(F32)
    col_ids = (pos % GRID_W).astype(F32)
    quarter = HEAD_DIM // 4
    inv_freq = ROPE_THETA ** (-jnp.arange(quarter, dtype=F32) / quarter)
    ar = row_ids[:, None] * inv_freq[None, :]
    ac = col_ids[:, None] * inv_freq[None, :]
    cos = jnp.concatenate([jnp.cos(ar), jnp.cos(ar), jnp.cos(ac), jnp.cos(ac)], axis=-1)
    sin = jnp.concatenate([-jnp.sin(ar), jnp.sin(ar), -jnp.sin(ac), jnp.sin(ac)], axis=-1)
    return cos, sin


def _swap_quarters(x):
    lane = lax.broadcasted_iota(jnp.int32, x.shape, 1)
    q = HEAD_DIM // 4
    return jnp.where((lane % (2 * q)) < q, pltpu.roll(x, HEAD_DIM - q, 1), pltpu.roll(x, q, 1))


def qk_prep_fwd(name, qkv, q_gain, k_gain, cos, sin, n_q, n_kv):
    t, width = qkv.shape
    tm = _tile(t, ROW_TILE)

    def body(x_ref, qg_ref, kg_ref, c_ref, s_ref, o_ref):
        c, s = c_ref[...], s_ref[...]
        for hd in range(n_q + n_kv):
            sl = slice(hd * HEAD_DIM, (hd + 1) * HEAD_DIM)
            x = x_ref[:, sl]
            g = qg_ref[...] if hd < n_q else kg_ref[...]
            xn = x * lax.rsqrt(jnp.mean(x * x, axis=-1, keepdims=True) + EPS) * g
            o_ref[:, sl] = (xn * c + _swap_quarters(xn) * s).astype(BF16)
        vs = slice((n_q + n_kv) * HEAD_DIM, width)
        o_ref[:, vs] = x_ref[:, vs].astype(BF16)

    return pl.pallas_call(
        body, name=name, grid=(t // tm,),
        in_specs=[_rows(width, tm), _vec(HEAD_DIM), _vec(HEAD_DIM), _rows(HEAD_DIM, tm), _rows(HEAD_DIM, tm)],
        out_specs=_rows(width, tm), out_shape=jax.ShapeDtypeStruct((t, width), BF16),
        compiler_params=_params(("parallel",)),
    )(qkv, q_gain.reshape(1, HEAD_DIM), k_gain.reshape(1, HEAD_DIM), cos, sin)


def qk_prep_bwd(name, qkv, dq, dk, dv, q_gain, k_gain, cos, sin, n_q, n_kv):
    t, width = qkv.shape
    tm = _tile(t, ROW_TILE)

    def body(x_ref, dq_ref, dk_ref, dv_ref, qg_ref, kg_ref, c_ref, s_ref, o_ref, dg_ref):
        @pl.when(pl.program_id(0) == 0)
        def _():
            dg_ref[...] = jnp.zeros_like(dg_ref)

        c, s = c_ref[...], s_ref[...]
        dgq = jnp.zeros((1, HEAD_DIM), F32)
        dgk = jnp.zeros((1, HEAD_DIM), F32)
        for hd in range(n_q + n_kv):
            sl = slice(hd * HEAD_DIM, (hd + 1) * HEAD_DIM)
            x = x_ref[:, sl]
            if hd < n_q:
                g, dout = qg_ref[...], dq_ref[:, sl]
            else:
                ks = slice((hd - n_q) * HEAD_DIM, (hd - n_q + 1) * HEAD_DIM)
                g, dout = kg_ref[...], dk_ref[:, ks]
            rstd = lax.rsqrt(jnp.mean(x * x, axis=-1, keepdims=True) + EPS)
            xhat = x * rstd
            dxn = dout * c + _swap_quarters(dout * s)
            part = jnp.sum(dxn * xhat, axis=0, keepdims=True)
            if hd < n_q:
                dgq = dgq + part
            else:
                dgk = dgk + part
            dxhat = dxn * g
            o_ref[:, sl] = (rstd * (dxhat - xhat * jnp.mean(dxhat * xhat, axis=-1, keepdims=True))).astype(BF16)
        o_ref[:, slice((n_q + n_kv) * HEAD_DIM, width)] = dv_ref[...].astype(BF16)
        dg_ref[0:1, :] += dgq
        dg_ref[1:2, :] += dgk

    kvw = n_kv * HEAD_DIM
    return pl.pallas_call(
        body, name=name, grid=(t // tm,),
        in_specs=[_rows(width, tm), _rows(n_q * HEAD_DIM, tm), _rows(kvw, tm), _rows(kvw, tm),
                  _vec(HEAD_DIM), _vec(HEAD_DIM), _rows(HEAD_DIM, tm), _rows(HEAD_DIM, tm)],
        out_specs=[_rows(width, tm), pl.BlockSpec((2, HEAD_DIM), lambda i: (0, 0))],
        out_shape=[jax.ShapeDtypeStruct((t, width), BF16), jax.ShapeDtypeStruct((2, HEAD_DIM), F32)],
        compiler_params=_params(("arbitrary",)),
    )(qkv, dq, dk, dv, q_gain.reshape(1, HEAD_DIM), k_gain.reshape(1, HEAD_DIM), cos, sin)


def _lanes(x, width):
    return jnp.tile(x, (1, width // LANES))


def _hs(hd):
    return slice(hd * HEAD_DIM, (hd + 1) * HEAD_DIM)


def attn_fwd(name, q, k, v, bias, *, grid, q_spec, k_spec, v_spec, b_spec, o_spec, valid, nh, shared_kv,
             bq, bk, o_shape, o_dtype):
    ns = grid[2]

    def body(*refs):
        if bias is None:
            q_ref, k_ref, v_ref, o_ref, lse_ref, m_s, l_s, acc_s = refs
            b_ref = None
        else:
            q_ref, k_ref, v_ref, b_ref, o_ref, lse_ref, m_s, l_s, acc_s = refs
        step = pl.program_id(2)

        @pl.when(step == 0)
        def _():
            m_s[...] = jnp.full_like(m_s, -jnp.inf)
            l_s[...] = jnp.zeros_like(l_s)
            acc_s[...] = jnp.zeros_like(acc_s)

        @pl.when(valid(pl.program_id(1), step))
        def _():
            for hd in range(nh):
                kh = _hs(0 if shared_kv else hd)
                s = _dot(q_ref[:, _hs(hd)], k_ref[:, kh], NT)
                p_rows, a_rows = [], []
                for r0 in range(0, bq, ATTN_ROWS):
                    rows = slice(r0, r0 + ATTN_ROWS)
                    z = s[rows] * ATTN_SCALE
                    if b_ref is not None:
                        z = z + b_ref[hd, rows, :]
                    m_prev = m_s[hd, rows, :]
                    m_new = jnp.maximum(m_prev, jnp.max(z, axis=-1, keepdims=True))
                    alpha = jnp.exp(m_prev - m_new)
                    p = jnp.exp(z - _lanes(m_new, bk))
                    l_s[hd, rows, :] = alpha * l_s[hd, rows, :] + jnp.sum(p, axis=-1, keepdims=True)
                    m_s[hd, rows, :] = m_new
                    p_rows.append(p.astype(BF16))
                    a_rows.append(alpha)
                pv = _dot(jnp.concatenate(p_rows, axis=0), v_ref[:, kh], NN)
                acc_s[hd] = jnp.concatenate(a_rows, axis=0) * acc_s[hd] + pv

        @pl.when(step == ns - 1)
        def _():
            for hd in range(nh):
                o_ref[:, _hs(hd)] = (acc_s[hd] / l_s[hd]).astype(o_dtype)
                lse_ref[:, _hs(hd)] = m_s[hd] + jnp.log(l_s[hd])

    in_specs = [q_spec, k_spec, v_spec] + ([] if bias is None else [b_spec])
    args = [q, k, v] + ([] if bias is None else [bias])
    stat = pltpu.VMEM((nh, bq, LANES), F32)
    return pl.pallas_call(
        body, name=name, grid=grid, in_specs=in_specs, out_specs=[o_spec, o_spec],
        out_shape=[jax.ShapeDtypeStruct(o_shape, o_dtype), jax.ShapeDtypeStruct(o_shape, F32)],
        scratch_shapes=[stat, stat, stat],
        compiler_params=_params(("parallel", "parallel", "arbitrary")),
    )(*args)


def _probs(q_ref, k_ref, v_ref, do_ref, lse_ref, dlt_ref, b_ref, hd, kh, bq, bk, want_p=True, on_ds=None):
    s = _dot(q_ref[:, _hs(hd)], k_ref[:, kh], NT)
    dp = _dot(do_ref[:, _hs(hd)], v_ref[:, kh], NT)
    p_rows, ds_rows = [], []
    for r0 in range(0, bq, ATTN_ROWS):
        rows = slice(r0, r0 + ATTN_ROWS)
        z = s[rows] * ATTN_SCALE
        if b_ref is not None:
            z = z + b_ref[hd, rows, :]
        p = jnp.exp(z - _lanes(lse_ref[rows, _hs(hd)], bk))
        ds = p * (dp[rows] - _lanes(dlt_ref[rows, _hs(hd)], bk))
        if on_ds is not None:
            on_ds(rows, ds)
        if want_p:
            p_rows.append(p.astype(BF16))
        ds_rows.append(ds.astype(BF16))
    return (jnp.concatenate(p_rows, axis=0) if want_p else None), jnp.concatenate(ds_rows, axis=0)


def attn_bwd_dq(name, q, k, v, do, lse, dlt, *, grid, q_spec, k_spec, v_spec, nh, bq, bk, o_shape):
    ns = grid[2]
    scale = HEAD_DIM ** -0.5

    def body(q_ref, k_ref, v_ref, do_ref, lse_ref, dlt_ref, dq_ref, acc_s):
        step = pl.program_id(2)

        @pl.when(step == 0)
        def _():
            acc_s[...] = jnp.zeros_like(acc_s)

        for hd in range(nh):
            _, ds = _probs(q_ref, k_ref, v_ref, do_ref, lse_ref, dlt_ref, None, hd, _hs(0), bq, bk, want_p=False)
            acc_s[hd] += _dot(ds, k_ref[:, _hs(0)], NN)

        @pl.when(step == ns - 1)
        def _():
            for hd in range(nh):
                dq_ref[:, _hs(hd)] = acc_s[hd] * scale

    return pl.pallas_call(
        body, name=name, grid=grid, in_specs=[q_spec, k_spec, v_spec, q_spec, q_spec, q_spec],
        out_specs=q_spec, out_shape=jax.ShapeDtypeStruct(o_shape, F32),
        scratch_shapes=[pltpu.VMEM((nh, bq, LANES), F32)],
        compiler_params=_params(("parallel", "parallel", "arbitrary")),
    )(q, k, v, do, lse, dlt)


def _always(i, s):
    return s >= 0


def row_delta(name, do, o, n_heads):
    t, width = do.shape
    tm = _tile(t, ROW_TILE)

    def body(do_ref, o_ref, dl_ref, dob_ref):
        for hd in range(n_heads):
            d = do_ref[:, _hs(hd)]
            s = jnp.sum(d * o_ref[:, _hs(hd)].astype(F32), axis=-1, keepdims=True)
            dl_ref[:, _hs(hd)] = jnp.broadcast_to(s, (tm, HEAD_DIM))
            dob_ref[:, _hs(hd)] = d.astype(BF16)

    return pl.pallas_call(
        body, name=name, grid=(t // tm,), in_specs=[_rows(width, tm), _rows(width, tm)],
        out_specs=[_rows(width, tm), _rows(width, tm)],
        out_shape=[jax.ShapeDtypeStruct((t, width), F32), jax.ShapeDtypeStruct((t, width), BF16)],
        compiler_params=_params(("parallel",)),
    )(do, o)


def _a_specs(n_q, n_kv, bq, bk, q_major):
    grp = n_q // n_kv
    if q_major:
        qm, km = (lambda b, i, s: (i, b)), (lambda b, i, s: (s, n_q + b))
        vm = lambda b, i, s: (s, n_q + n_kv + b)
    else:
        qm, km = (lambda b, i, s: (s, b)), (lambda b, i, s: (i, n_q + b))
        vm = lambda b, i, s: (i, n_q + n_kv + b)
    return (pl.BlockSpec((bq, grp * HEAD_DIM), qm), pl.BlockSpec((bk, HEAD_DIM), km),
            pl.BlockSpec((bk, HEAD_DIM), vm))


def mixer_a_fwd(qkv_r, n_q, n_kv):
    t = qkv_r.shape[0]
    bq, bk = _tile(t, A_BQ), _tile(t, A_BK)
    q_spec, k_spec, v_spec = _a_specs(n_q, n_kv, bq, bk, True)
    return attn_fwd("a_attn_fwd", qkv_r, qkv_r, qkv_r, None, grid=(n_kv, t // bq, t // bk),
                    q_spec=q_spec, k_spec=k_spec, v_spec=v_spec, b_spec=None, o_spec=q_spec, valid=_always,
                    nh=n_q // n_kv, shared_kv=True, bq=bq, bk=bk, o_shape=(t, n_q * HEAD_DIM), o_dtype=BF16)


def mixer_a_bwd(qkv_r, do_b, lse, dlt, n_q, n_kv):
    t = qkv_r.shape[0]
    bq, bk = _tile(t, A_BQ), _tile(t, A_BK)
    grp = n_q // n_kv
    q_spec, k_spec, v_spec = _a_specs(n_q, n_kv, bq, bk, True)
    dq = attn_bwd_dq("a_attn_dq", qkv_r, qkv_r, qkv_r, do_b, lse, dlt, grid=(n_kv, t // bq, t // bk),
                     q_spec=q_spec, k_spec=k_spec, v_spec=v_spec, nh=grp, bq=bq, bk=bk,
                     o_shape=(t, n_q * HEAD_DIM))
    q_spec, k_spec, v_spec = _a_specs(n_q, n_kv, bq, bk, False)
    o_spec = pl.BlockSpec((bk, HEAD_DIM), lambda b, i, s: (i, b))
    dk, dv = _attn_bwd_dkv_out(qkv_r, do_b, lse, dlt, grid=(n_kv, t // bk, t // bq), q_spec=q_spec,
                               k_spec=k_spec, v_spec=v_spec, o_spec=o_spec, grp=grp, bq=bq, bk=bk,
                               o_shape=(t, n_kv * HEAD_DIM))
    return dq, dk, dv


def _attn_bwd_dkv_out(qkv_r, do_b, lse, dlt, *, grid, q_spec, k_spec, v_spec, o_spec, grp, bq, bk, o_shape):
    ns = grid[2]
    scale = HEAD_DIM ** -0.5

    def body(q_ref, k_ref, v_ref, do_ref, lse_ref, dlt_ref, dk_ref, dv_ref, dk_s, dv_s):
        step = pl.program_id(2)

        @pl.when(step == 0)
        def _():
            dk_s[...] = jnp.zeros_like(dk_s)
            dv_s[...] = jnp.zeros_like(dv_s)

        for hd in range(grp):
            p, ds = _probs(q_ref, k_ref, v_ref, do_ref, lse_ref, dlt_ref, None, hd, _hs(0), bq, bk)
            dv_s[...] += _dot(p, do_ref[:, _hs(hd)], TN)
            dk_s[...] += _dot(ds, q_ref[:, _hs(hd)], TN)

        @pl.when(step == ns - 1)
        def _():
            dk_ref[...] = dk_s[...] * scale
            dv_ref[...] = dv_s[...]

    acc = pltpu.VMEM((bk, HEAD_DIM), F32)
    return pl.pallas_call(
        body, name="a_attn_dkv", grid=grid, in_specs=[q_spec, k_spec, v_spec, q_spec, q_spec, q_spec],
        out_specs=[o_spec, o_spec], out_shape=[jax.ShapeDtypeStruct(o_shape, F32)] * 2,
        scratch_shapes=[acc, acc], compiler_params=_params(("parallel", "parallel", "arbitrary")),
    )(qkv_r, qkv_r, qkv_r, do_b, lse, dlt)


def t5_bucket(rel):
    nb = REL_BUCKETS // 2
    max_exact = nb // 2
    base = jnp.where(rel > 0, nb, 0)
    n = jnp.abs(rel)
    nf = jnp.maximum(n, 1).astype(F32)
    large = max_exact + (jnp.log(nf / max_exact) / math.log(REL_MAX_DISTANCE / max_exact)
                         * (nb - max_exact)).astype(jnp.int32)
    large = jnp.minimum(large, nb - 1)
    return base + jnp.where(n < max_exact, n, large)


def band_stride(t, win, dil):
    return 1 if t % B_BQ == 0 and win // 2 <= B_BQ else dil


def band_tables(rel_bias_g, win, dil, stride, bq):
    a = jnp.arange(bq)[:, None]
    b = jnp.arange(bq)[None, :]
    rel = jnp.stack([(s - 1) * bq + b - a for s in range(3)]) * stride
    ok = (jnp.abs(rel) <= win // 2) & (rel % dil == 0)
    bucket = t5_bucket(rel)
    bias = jnp.zeros((rel_bias_g.shape[1],) + rel.shape, F32)
    for r in range(REL_BUCKETS):
        bias = bias + jnp.where(bucket[None] == r, rel_bias_g[r][:, None, None, None], 0.0)
    return jnp.where(ok[None], bias, NEG_INF), jnp.where(ok, bucket, -1).astype(jnp.int32)


def band_block(t, stride):
    return _tile(t // stride, B_BQ)


def _b_geometry(t, dil, g, n_groups, bq):
    hg = B_HEADS_PER_GROUP
    length = t // dil
    nblk = length // bq
    gw = hg * HEAD_DIM
    per_tok = 3 * n_groups
    return hg, length, bq, nblk, gw, per_tok


def mixer_b_group_fwd(qkv, bias, dil, g, n_groups, tag):
    t = qkv.shape[0]
    hg, length, bq, nblk, gw, per_tok = _b_geometry(t, dil, g, n_groups, bias.shape[2])
    if dil > 1:
        qkv, g, per_tok = qkv[:, 3 * g * gw:3 * (g + 1) * gw], 0, 3
    view = qkv.reshape(length, dil * qkv.shape[1])
    col = lambda c, which: c * per_tok + 3 * g + which
    kblk = lambda i, s: jnp.clip(i - 1 + s, 0, nblk - 1)
    spec = lambda which, streamed: pl.BlockSpec(
        (bq, gw), (lambda c, i, s: (kblk(i, s), col(c, which))) if streamed else (lambda c, i, s: (i, col(c, which))))
    valid = lambda i, s: (i - 1 + s >= 0) & (i - 1 + s < nblk)
    o, lz = attn_fwd(f"b_attn_fwd_d{tag}", view, view, view, bias, grid=(dil, nblk, 3),
                     q_spec=spec(0, False), k_spec=spec(1, True), v_spec=spec(2, True),
                     b_spec=pl.BlockSpec((hg, None, bq, bq), lambda c, i, s: (0, s, 0, 0)),
                     o_spec=pl.BlockSpec((bq, gw), lambda c, i, s: (i, c)), valid=valid, nh=hg,
                     shared_kv=False, bq=bq, bk=bq, o_shape=(length, dil * gw), o_dtype=F32)
    return o.reshape(t, gw), lz.reshape(t, gw)


def mixer_b_group_bwd(qkv, bias, do_g, lz_g, dlt_g, dil, g, n_groups, tag):
    t = qkv.shape[0]
    hg, length, bq, nblk, gw, per_tok = _b_geometry(t, dil, g, n_groups, bias.shape[2])
    if dil > 1:
        qkv, g, per_tok = qkv[:, 3 * g * gw:3 * (g + 1) * gw], 0, 3
    view = qkv.reshape(length, dil * qkv.shape[1])
    dov, lzv, dlv = (x.reshape(length, dil * gw) for x in (do_g, lz_g, dlt_g))
    col = lambda c, which: c * per_tok + 3 * g + which
    nbr = lambda i, s: jnp.clip(i - 1 + s, 0, nblk - 1)
    valid = lambda i, s: (i - 1 + s >= 0) & (i - 1 + s < nblk)
    q_spec = pl.BlockSpec((bq, gw), lambda c, i, s: (i, col(c, 0)))
    k_spec = pl.BlockSpec((bq, gw), lambda c, i, s: (nbr(i, s), col(c, 1)))
    v_spec = pl.BlockSpec((bq, gw), lambda c, i, s: (nbr(i, s), col(c, 2)))
    stat = pl.BlockSpec((bq, gw), lambda c, i, s: (i, c))
    dq, dbias = _band_bwd_dq(f"b_attn_dq_d{tag}", view, dov, lzv, dlv, bias, grid=(dil, nblk, 3),
                             q_spec=q_spec, k_spec=k_spec, v_spec=v_spec, stat_spec=stat,
                             b_spec=pl.BlockSpec((hg, None, bq, bq), lambda c, i, s: (0, s, 0, 0)),
                             valid=valid, nh=hg, bq=bq, o_shape=(length, dil * gw))
    q_spec = pl.BlockSpec((bq, gw), lambda c, i, s: (nbr(i, s), col(c, 0)))
    k_spec = pl.BlockSpec((bq, gw), lambda c, i, s: (i, col(c, 1)))
    v_spec = pl.BlockSpec((bq, gw), lambda c, i, s: (i, col(c, 2)))
    stat = pl.BlockSpec((bq, gw), lambda c, i, s: (nbr(i, s), c))
    dk, dv = _band_bwd_dkv(f"b_attn_dkv_d{tag}", view, dov, lzv, dlv, bias, grid=(dil, nblk, 3),
                           q_spec=q_spec, k_spec=k_spec, v_spec=v_spec, stat_spec=stat,
                           b_spec=pl.BlockSpec((hg, None, bq, bq), lambda c, i, s: (0, 2 - s, 0, 0)),
                           o_spec=pl.BlockSpec((bq, gw), lambda c, i, s: (i, c)),
                           valid=valid, nh=hg, bq=bq, o_shape=(length, dil * gw))
    return dq.reshape(t, gw), dk.reshape(t, gw), dv.reshape(t, gw), dbias


def _band_bwd_dq(name, view, do, lse, dlt, bias, *, grid, q_spec, k_spec, v_spec, stat_spec, b_spec, valid,
                 nh, bq, o_shape):
    scale = HEAD_DIM ** -0.5
    bias_shape = (nh, 3, bq, bq)

    def body(q_ref, k_ref, v_ref, do_ref, lse_ref, dlt_ref, b_ref, dq_ref, db_ref, acc_s):
        step = pl.program_id(2)

        @pl.when((pl.program_id(0) == 0) & (pl.program_id(1) == 0) & (step == 0))
        def _():
            db_ref[...] = jnp.zeros_like(db_ref)

        @pl.when(step == 0)
        def _():
            acc_s[...] = jnp.zeros_like(acc_s)

        @pl.when(valid(pl.program_id(1), step))
        def _():
            for hd in range(nh):
                def add_bias_grad(rows, ds, hd=hd):
                    db_ref[hd, step, rows, :] += ds

                _, ds = _probs(q_ref, k_ref, v_ref, do_ref, lse_ref, dlt_ref, b_ref, hd, _hs(hd), bq, bq,
                               want_p=False, on_ds=add_bias_grad)
                acc_s[hd] += _dot(ds, k_ref[:, _hs(hd)], NN)

        @pl.when(step == 2)
        def _():
            for hd in range(nh):
                dq_ref[:, _hs(hd)] = (acc_s[hd] * scale).astype(BF16)

    return pl.pallas_call(
        body, name=name, grid=grid,
        in_specs=[q_spec, k_spec, v_spec, stat_spec, stat_spec, stat_spec, b_spec],
        out_specs=[stat_spec, pl.BlockSpec(bias_shape, lambda c, i, s: (0, 0, 0, 0))],
        out_shape=[jax.ShapeDtypeStruct(o_shape, BF16), jax.ShapeDtypeStruct(bias_shape, F32)],
        scratch_shapes=[pltpu.VMEM((nh, bq, LANES), F32)], compiler_params=_params(("arbitrary",) * 3),
    )(view, view, view, do, lse, dlt, bias)


def _band_bwd_dkv(name, view, do, lse, dlt, bias, *, grid, q_spec, k_spec, v_spec, stat_spec, b_spec, o_spec,
                  valid, nh, bq, o_shape):
    scale = HEAD_DIM ** -0.5

    def body(q_ref, k_ref, v_ref, do_ref, lse_ref, dlt_ref, b_ref, dk_ref, dv_ref, dk_s, dv_s):
        step = pl.program_id(2)

        @pl.when(step == 0)
        def _():
            dk_s[...] = jnp.zeros_like(dk_s)
            dv_s[...] = jnp.zeros_like(dv_s)

        @pl.when(valid(pl.program_id(1), step))
        def _():
            for hd in range(nh):
                p, ds = _probs(q_ref, k_ref, v_ref, do_ref, lse_ref, dlt_ref, b_ref, hd, _hs(hd), bq, bq)
                dv_s[hd] += _dot(p, do_ref[:, _hs(hd)], TN)
                dk_s[hd] += _dot(ds, q_ref[:, _hs(hd)], TN)

        @pl.when(step == 2)
        def _():
            for hd in range(nh):
                dk_ref[:, _hs(hd)] = (dk_s[hd] * scale).astype(BF16)
                dv_ref[:, _hs(hd)] = dv_s[hd].astype(BF16)

    acc = pltpu.VMEM((nh, bq, LANES), F32)
    return pl.pallas_call(
        body, name=name, grid=grid,
        in_specs=[q_spec, k_spec, v_spec, stat_spec, stat_spec, stat_spec, b_spec],
        out_specs=[o_spec, o_spec], out_shape=[jax.ShapeDtypeStruct(o_shape, BF16)] * 2,
        scratch_shapes=[acc, acc], compiler_params=_params(("parallel", "parallel", "arbitrary")),
    )(view, view, view, do, lse, dlt, bias)


def bias_bucket_sums(name, dbias, bucket):
    nh, _, bq, _ = dbias.shape
    db2 = dbias.reshape(nh, 3 * bq, bq)
    bk2 = bucket.reshape(3 * bq, bq)

    def body(db_ref, bk_ref, o_ref):
        row = lax.broadcasted_iota(jnp.int32, (nh, LANES), 0)
        lane = lax.broadcasted_iota(jnp.int32, (nh, LANES), 1)
        out = jnp.zeros((nh, LANES), F32)
        bkt = bk_ref[...]
        for hd in range(nh):
            x = db_ref[hd]
            for r in range(REL_BUCKETS):
                part = jnp.sum(jnp.where(bkt == r, x, 0.0), axis=1, keepdims=True)
                tot = jnp.sum(part, axis=0, keepdims=True)
                out = out + jnp.where((row == hd) & (lane == r), tot, 0.0)
        o_ref[...] = out

    return pl.pallas_call(
        body, name=name, out_shape=jax.ShapeDtypeStruct((nh, LANES), F32),
        compiler_params=pltpu.CompilerParams(vmem_limit_bytes=VMEM_LIMIT),
    )(db2, bk2)


def combine_fwd(name, outs, lzs):
    n_g = len(outs)
    t, gw = outs[0].shape
    tm = _tile(t, ROW_TILE)

    def body(*refs):
        o_refs, lz_refs, y_ref = refs[:n_g], refs[n_g:2 * n_g], refs[2 * n_g]
        lz = [r[...] for r in lz_refs]
        mx = functools.reduce(jnp.maximum, lz)
        e = [jnp.exp(x - mx) for x in lz]
        den = functools.reduce(lambda a, b: a + b, e)
        for g in range(n_g):
            y_ref[:, g * gw:(g + 1) * gw] = (e[g] / den * o_refs[g][...]).astype(BF16)

    return pl.pallas_call(
        body, name=name, grid=(t // tm,), in_specs=[_rows(gw, tm)] * (2 * n_g), out_specs=_rows(n_g * gw, tm),
        out_shape=jax.ShapeDtypeStruct((t, n_g * gw), BF16), compiler_params=_params(("parallel",)),
    )(*outs, *lzs)


def combine_bwd(name, dy, outs, lzs):
    n_g = len(outs)
    t, gw = outs[0].shape
    tm = _tile(t, ROW_TILE)
    nh = gw // HEAD_DIM

    def body(*refs):
        dy_ref = refs[0]
        o_refs, lz_refs = refs[1:1 + n_g], refs[1 + n_g:1 + 2 * n_g]
        do_refs, dl_refs = refs[1 + 2 * n_g:1 + 3 * n_g], refs[1 + 3 * n_g:]
        lz = [r[...] for r in lz_refs]
        mx = functools.reduce(jnp.maximum, lz)
        e = [jnp.exp(x - mx) for x in lz]
        den = functools.reduce(lambda a, b: a + b, e)
        wts = [x / den for x in e]
        for g in range(n_g):
            do_refs[g][...] = (wts[g] * dy_ref[:, g * gw:(g + 1) * gw]).astype(BF16)
        for hd in range(nh):
            mix = jnp.zeros((tm, HEAD_DIM), F32)
            for g in range(n_g):
                prod = dy_ref[:, g * gw + hd * HEAD_DIM:g * gw + (hd + 1) * HEAD_DIM] * o_refs[g][:, _hs(hd)]
                dw = jnp.broadcast_to(jnp.sum(prod, axis=-1, keepdims=True), (tm, HEAD_DIM))
                mix = mix + wts[g][:, _hs(hd)] * dw
            for g in range(n_g):
                dl_refs[g][:, _hs(hd)] = wts[g][:, _hs(hd)] * mix

    return pl.pallas_call(
        body, name=name, grid=(t // tm,),
        in_specs=[_rows(n_g * gw, tm)] + [_rows(gw, tm)] * (2 * n_g),
        out_specs=[_rows(gw, tm)] * (2 * n_g),
        out_shape=[jax.ShapeDtypeStruct((t, gw), BF16)] * n_g + [jax.ShapeDtypeStruct((t, gw), F32)] * n_g,
        compiler_params=_params(("parallel",)),
    )(dy, *outs, *lzs)


def _shifted(u):
    t = u.shape[0]
    row = lax.broadcasted_iota(jnp.int32, u.shape, 0)
    prev = jnp.where(row == 0, 0.0, pltpu.roll(u, 1, 0))
    nxt = jnp.where(row == t - 1, 0.0, pltpu.roll(u, t - 1, 0))
    return prev, nxt


def _conv3(u, prev, nxt, w_ref, b):
    return w_ref[0:1, :] * prev + w_ref[1:2, :] * u + w_ref[2:3, :] * nxt + b


def _conv3_t(d, w_ref):
    prev, nxt = _shifted(d)
    return w_ref[0:1, :] * nxt + w_ref[1:2, :] * d + w_ref[2:3, :] * prev


def conv_act_fwd(name, u2, cw2, cb2):
    _, t, dff = u2.shape
    tn = LANES

    def body(u_ref, w_ref, b_ref, o_ref):
        ug, uv = u_ref[0], u_ref[1]
        cg = _conv3(ug, *_shifted(ug), w_ref.at[0], b_ref[0])
        cv = _conv3(uv, *_shifted(uv), w_ref.at[1], b_ref[1])
        o_ref[...] = (cg * jax.nn.sigmoid(cg) * cv).astype(BF16)

    return pl.pallas_call(
        body, name=name, grid=(dff // tn,),
        in_specs=[pl.BlockSpec((2, t, tn), lambda j: (0, 0, j)), pl.BlockSpec((2, 3, tn), lambda j: (0, 0, j)),
                  pl.BlockSpec((2, 1, tn), lambda j: (0, 0, j))],
        out_specs=pl.BlockSpec((t, tn), lambda j: (0, j)), out_shape=jax.ShapeDtypeStruct((t, dff), BF16),
        compiler_params=_params(("parallel",)),
    )(u2, cw2, cb2)


def conv_act_bwd(name, u2, cw2, cb2, dact, after=()):
    _, t, dff = u2.shape
    tn = LANES

    def body(u_ref, w_ref, b_ref, d_ref, *rest):
        du_ref, dw_ref = rest[-2:]
        d = d_ref[...]
        ug, uv = u_ref[0], u_ref[1]
        shifted = (_shifted(ug), _shifted(uv))
        cg = _conv3(ug, *shifted[0], w_ref.at[0], b_ref[0])
        cv = _conv3(uv, *shifted[1], w_ref.at[1], b_ref[1])
        sg = jax.nn.sigmoid(cg)
        dcv = d * (cg * sg)
        dcg = d * cv * (sg * (1.0 + cg * (1.0 - sg)))
        du_ref[0] = _conv3_t(dcg, w_ref.at[0]).astype(BF16)
        du_ref[1] = _conv3_t(dcv, w_ref.at[1]).astype(BF16)
        for half, (dc, u) in enumerate(((dcg, ug), (dcv, uv))):
            prev, nxt = shifted[half]
            for tap, x in enumerate((prev, u, nxt)):
                dw_ref[half, tap:tap + 1, :] = jnp.sum(dc * x, axis=0, keepdims=True)
            dw_ref[half, 3:4, :] = jnp.sum(dc, axis=0, keepdims=True)
            dw_ref[half, 4:8, :] = jnp.zeros((4, tn), F32)

    return pl.pallas_call(
        body, name=name, grid=(dff // tn,),
        in_specs=[pl.BlockSpec((2, t, tn), lambda j: (0, 0, j)), pl.BlockSpec((2, 3, tn), lambda j: (0, 0, j)),
                  pl.BlockSpec((2, 1, tn), lambda j: (0, 0, j)), pl.BlockSpec((t, tn), lambda j: (0, j))]
        + [pl.BlockSpec(memory_space=pl.ANY)] * len(after),
        out_specs=[pl.BlockSpec((2, t, tn), lambda j: (0, 0, j)), pl.BlockSpec((2, 8, tn), lambda j: (0, 0, j))],
        out_shape=[jax.ShapeDtypeStruct((2, t, dff), BF16), jax.ShapeDtypeStruct((2, 8, dff), F32)],
        compiler_params=_params(("parallel",)),
    )(u2, cw2, cb2, dact, *after)


GATHER_ID, SIBLING_ID, CHIPS_ID = 0, 1, 2


def _place():
    x, y, c = lax.axis_index("x"), lax.axis_index("y"), lax.axis_index("c")
    chips = [(1 - x, y), (x, 1 - y), (1 - x, 1 - y)]
    return x, y, c, chips


def _handshake(peers):
    barrier = pltpu.get_barrier_semaphore()
    for peer in peers:
        pl.semaphore_signal(barrier, inc=1, device_id=peer, device_id_type=MESH)
    pl.semaphore_wait(barrier, len(peers))


def _sequencer(name, body, out_type, scratch_types, collective_id):
    return pl.kernel(body, out_type=out_type, mesh=plsc.ScalarSubcoreMesh(axis_name="seq", num_cores=1),
                     scratch_types=scratch_types, name=name,
                     compiler_params=pltpu.CompilerParams(collective_id=collective_id))


def _gather_body(n):
    def body(*refs):
        src, out = refs[:n], refs[n:2 * n]
        send, recv, loc = refs[2 * n:]
        x, y, c, chips = _place()
        sibling = (x, y, 1 - c)
        _handshake([sibling] + [(*chip, c) for chip in chips])

        def slot(a, px, py, pc):
            return out[a].at[4 * px + 2 * py + pc]

        def copy(a, k, block, to, from_src=False):
            return pltpu.make_async_remote_copy(
                src_ref=src[a] if from_src else slot(a, *block), dst_ref=slot(a, *block),
                send_sem=send.at[a, k], recv_sem=recv.at[a, k], device_id=to, device_id_type=MESH)

        mine = [pltpu.make_async_copy(src[a], slot(a, x, y, c), loc.at[a]) for a in range(n)]
        for cp in mine:
            cp.start()
        first = []
        for a in range(n):
            first.append(copy(a, 0, (x, y, c), sibling, True))
            first += [copy(a, 1 + j, (x, y, c), (*chip, c), True) for j, chip in enumerate(chips)]
        for cp in first:
            cp.start()
        passed = []
        for j, chip in enumerate(chips):
            for a in range(n):
                copy(a, 1 + j, (*chip, c), (x, y, c)).wait_recv()
                cp = copy(a, 4 + j, (*chip, c), sibling)
                cp.start()
                passed.append(cp)
        for a in range(n):
            copy(a, 0, sibling, (x, y, c)).wait_recv()
            for j, chip in enumerate(chips):
                copy(a, 4 + j, (*chip, 1 - c), (x, y, c)).wait_recv()
        for cp in first + passed:
            cp.wait_send()
        for cp in mine:
            cp.wait()

    return body


def gather_layer(name, shards):
    n = len(shards)
    out_type = [jax.ShapeDtypeStruct((N_DEV,) + s.shape, s.dtype) for s in shards]
    scratch = [pltpu.SemaphoreType.DMA((n, 7)), pltpu.SemaphoreType.DMA((n, 7)), pltpu.SemaphoreType.DMA((n,))]
    return _sequencer(name, _gather_body(n), out_type, scratch, GATHER_ID)(*shards)


def _to_sibling_body(n):
    def body(*refs):
        src, got = refs[:n], refs[n:2 * n]
        send, recv = refs[2 * n:]
        x, y, c, _ = _place()
        sibling = (x, y, 1 - c)
        _handshake([sibling])
        remote = []
        for a in range(n):
            for q in range(4):
                remote.append(pltpu.make_async_remote_copy(
                    src_ref=src[a].at[2 * q + 1 - c], dst_ref=got[a].at[q], send_sem=send.at[a, q],
                    recv_sem=recv.at[a, q], device_id=sibling, device_id_type=MESH))
        for cp in remote:
            cp.start()
        for cp in remote:
            cp.wait()

    return body


def grads_to_sibling(name, grads):
    n = len(grads)
    out_type = [jax.ShapeDtypeStruct((4,) + g.shape[1:], g.dtype) for g in grads]
    scratch = [pltpu.SemaphoreType.DMA((n, 4)), pltpu.SemaphoreType.DMA((n, 4))]
    return _sequencer(name, _to_sibling_body(n), out_type, scratch, SIBLING_ID)(*grads)


def _to_chips_body(n):
    def body(*refs):
        src, got = refs[:n], refs[n:2 * n]
        send, recv = refs[2 * n:]
        x, y, c, chips = _place()
        _handshake([(*chip, c) for chip in chips])
        remote = []
        for a in range(n):
            for j, (px, py) in enumerate(chips):
                remote.append(pltpu.make_async_remote_copy(
                    src_ref=src[a].at[2 * px + py], dst_ref=got[a].at[j], send_sem=send.at[a, j],
                    recv_sem=recv.at[a, j], device_id=(px, py, c), device_id_type=MESH))
        for cp in remote:
            cp.start()
        for cp in remote:
            cp.wait()

    return body


def grads_to_chips(name, parts):
    n = len(parts)
    out_type = [jax.ShapeDtypeStruct((3,) + p.shape[1:], p.dtype) for p in parts]
    scratch = [pltpu.SemaphoreType.DMA((n, 3)), pltpu.SemaphoreType.DMA((n, 3))]
    return _sequencer(name, _to_chips_body(n), out_type, scratch, CHIPS_ID)(*parts)


def all_reduce_small(name, vec):
    rows, m = vec.shape

    def body(x_ref, o_ref, buf, send, recv):
        x, y, c, chips = _place()
        sibling = (x, y, 1 - c)

        def blk(px, py, pc):
            return buf.at[pl.ds(pl.multiple_of((4 * px + 2 * py + pc) * rows, rows), rows), :]

        def copy(k, block, to):
            return pltpu.make_async_remote_copy(src_ref=blk(*block), dst_ref=blk(*block), send_sem=send.at[k],
                                                recv_sem=recv.at[k], device_id=to, device_id_type=MESH)

        blk(x, y, c)[...] = x_ref[...]
        first = [copy(0, (x, y, c), sibling)] + [copy(1 + j, (x, y, c), (*chip, c)) for j, chip in enumerate(chips)]
        for cp in first:
            cp.start()
        passed = [copy(4 + j, (*chip, c), sibling) for j, chip in enumerate(chips)]
        for j, chip in enumerate(chips):
            copy(1 + j, (*chip, c), (x, y, c)).wait_recv()
            passed[j].start()
        copy(0, sibling, (x, y, c)).wait_recv()
        for j, chip in enumerate(chips):
            copy(4 + j, (*chip, 1 - c), (x, y, c)).wait_recv()
        for cp in first + passed:
            cp.wait_send()
        tot = buf[0:rows, :]
        for dev in range(1, N_DEV):
            tot = tot + buf[dev * rows:(dev + 1) * rows, :]
        o_ref[...] = tot

    return pl.pallas_call(
        body, name=name, in_specs=[pl.BlockSpec(memory_space=pltpu.VMEM)],
        out_specs=pl.BlockSpec(memory_space=pltpu.VMEM), out_shape=jax.ShapeDtypeStruct((rows, m), F32),
        scratch_shapes=[pltpu.VMEM((N_DEV * rows, m), F32), pltpu.SemaphoreType.DMA((7,)),
                        pltpu.SemaphoreType.DMA((7,))],
        compiler_params=pltpu.CompilerParams(vmem_limit_bytes=VMEM_LIMIT),
    )(vec)


def _ew_tiles(rows, cols, max_elems=1 << 18):
    tr = rows
    for cand in (1024, 512, 256, 128, 64, 32, 16):
        if rows % cand == 0 and cand * cols <= max_elems:
            tr = cand
            break
    return tr


def chip_sum(name, full, got, core):
    _, kdim, ncol = full.shape
    tr = _ew_tiles(kdim, ncol, max_elems=1 << 20)
    blk = (None, tr, ncol)
    by_chip = pl.BlockSpec(blk, lambda q, i, c: (q, i, 0))

    def body(c_ref, a_ref, b_ref, o_ref):
        o_ref[...] = (a_ref[...].astype(F32) + b_ref[...].astype(F32)).astype(BF16)

    return pl.pallas_call(
        body, name=name,
        grid_spec=pltpu.PrefetchScalarGridSpec(
            num_scalar_prefetch=1, grid=(4, kdim // tr),
            in_specs=[pl.BlockSpec(blk, lambda q, i, c: (2 * q + c[0], i, 0)), by_chip], out_specs=by_chip),
        out_shape=jax.ShapeDtypeStruct((4, kdim, ncol), BF16),
        compiler_params=_params(("parallel", "parallel")),
    )(core, full, got)


def _adamw_math(w, g, m, v):
    m = ADAM_B1 * m + (1.0 - ADAM_B1) * g
    v = ADAM_B2 * v + (1.0 - ADAM_B2) * (g * g)
    m_hat = m / (1.0 - ADAM_B1 ** ADAM_STEP)
    v_hat = v / (1.0 - ADAM_B2 ** ADAM_STEP)
    delta = -ADAM_LR * (m_hat / (jnp.sqrt(v_hat) + ADAM_EPS) + ADAM_WD * w)
    return delta, m, v


def adamw_layer(name, sums, got, w, m, v, outs, layer, chip):
    _, kdim, ncol = sums.shape
    tr = _ew_tiles(kdim, ncol)
    mine = pl.BlockSpec((None, tr, ncol), lambda i, q: (q[0], i, 0))
    others = pl.BlockSpec((3, tr, ncol), lambda i, q: (0, i, 0))
    param = pl.BlockSpec((None, tr, ncol), lambda i, q: (layer, i, 0))
    whole = pl.BlockSpec(memory_space=pl.ANY)

    def body(q_ref, o_ref, g_ref, w_ref, m_ref, v_ref, *rest):
        go_ref, d_ref, mo_ref, vo_ref = rest[-4:]
        g = o_ref[...].astype(F32)
        for j in range(3):
            g = g + g_ref[j].astype(F32)
        d, mn, vn = _adamw_math(w_ref[...], g, m_ref[...], v_ref[...])
        go_ref[...] = g
        d_ref[...] = d
        mo_ref[...] = mn
        vo_ref[...] = vn

    n_in = 6
    return pl.pallas_call(
        body, name=name,
        grid_spec=pltpu.PrefetchScalarGridSpec(
            num_scalar_prefetch=1, grid=(kdim // tr,),
            in_specs=[mine, others, param, param, param] + [whole] * 4, out_specs=[param] * 4),
        out_shape=[jax.ShapeDtypeStruct(w.shape, F32)] * 4,
        input_output_aliases={n_in + k: k for k in range(4)},
        compiler_params=_params(("parallel",)),
    )(chip, sums, got, w, m, v, *outs)


def adamw_small(name, g, w, m, v):
    def body(g_ref, w_ref, m_ref, v_ref, d_ref, mo_ref, vo_ref):
        d, mn, vn = _adamw_math(w_ref[...], g_ref[...], m_ref[...], v_ref[...])
        d_ref[...] = d
        mo_ref[...] = mn
        vo_ref[...] = vn

    vm = pl.BlockSpec(memory_space=pltpu.VMEM)
    return pl.pallas_call(
        body, name=name, in_specs=[vm] * 4, out_specs=[vm] * 3,
        out_shape=[jax.ShapeDtypeStruct(g.shape, F32)] * 3,
        compiler_params=pltpu.CompilerParams(vmem_limit_bytes=VMEM_LIMIT),
    )(g, w, m, v)


def _pack(parts, width):
    flat = jnp.concatenate([p.reshape(-1).astype(F32) for p in parts])
    pad = (-flat.shape[0]) % width
    return jnp.pad(flat, (0, pad)).reshape(-1, width) if pad else flat.reshape(-1, width)


def _unpack(packed, shapes):
    flat = packed.reshape(-1)
    out, off = [], 0
    for s in shapes:
        size = math.prod(s)
        out.append(flat[off:off + size].reshape(s))
        off += size
    return out


def _local_step(h, target, layers, params, on_grads=None):
    a_q_gain, a_k_gain, rel_bias, mix_norm, ffn_norm, conv_b, final_norm = params
    t, d = h.shape
    depth = len(layers)
    n_groups = len(B_GROUPS)
    hg = B_HEADS_PER_GROUP
    n_kv = A_KV_HEADS
    w_a, w_b, w_u = layers[0][0].shape[2], layers[1][0].shape[2], layers[0][2].shape[2]
    n_q = w_a * N_DEV // HEAD_DIM - 2 * n_kv
    dff = layers[0][3].shape[0]
    n_a = (depth + 1) // 2
    cb_full = conv_b.reshape(depth, 2, 1, dff)

    cos, sin = rope_tables(t)
    strides = [band_stride(t, win, dil) for win, dil in B_GROUPS]
    tables = [band_tables(rel_bias[:, g * hg:(g + 1) * hg], win, dil, strides[g], band_block(t, strides[g]))
              for g, (win, dil) in enumerate(B_GROUPS)]

    saved = []
    for i in range(depth):
        j = i // 2
        w_qkv, w_o, w_up_i, w_down_i, cw = layers[i]
        s = {"h_in": h}
        hn = rms_fwd("mix_norm_fwd", h, mix_norm[i])
        s["hn"] = hn
        if i % 2 == 0:
            qkv = mm_col_fwd("a_qkv_fwd", hn, w_qkv, F32)
            qkv_r = qk_prep_fwd("a_qk_prep_fwd", qkv, a_q_gain[j], a_k_gain[j], cos, sin, n_q, n_kv)
            o, lse = mixer_a_fwd(qkv_r, n_q, n_kv)
            s.update(qkv=qkv, qkv_r=qkv_r, o=o, lse=lse)
            h = mm_row_fwd("a_out_fwd", o, w_o, h)
        else:
            qkv = mm_col_fwd("b_qkv_fwd", hn, w_qkv, BF16)
            outs, lzs = [], []
            for g, (win, dil) in enumerate(B_GROUPS):
                o_g, lz_g = mixer_b_group_fwd(qkv, tables[g][0], strides[g], g, n_groups, dil)
                outs.append(o_g)
                lzs.append(lz_g)
            y = combine_fwd("b_combine_fwd", outs, lzs)
            s.update(qkv=qkv, outs=outs, lzs=lzs, y=y)
            h = mm_row_fwd("b_out_fwd", y, w_o, h)
        s["h_mid"] = h
        hn2 = rms_fwd("ffn_norm_fwd", h, ffn_norm[i])
        u2 = mm_col_fwd("ffn_up_fwd", hn2, w_up_i, F32, split=2)
        act = conv_act_fwd("ffn_conv_act_fwd", u2, cw, cb_full[i])
        s.update(hn2=hn2, u2=u2, act=act)
        h = mm_row_fwd("ffn_down_fwd", act, w_down_i, h)
        saved.append(s)

    dh, dh_b, d_final, loss_part = loss_head("loss_head", h, final_norm, target)

    d_mix, d_ffn, d_cw, d_cb = [None] * depth, [None] * depth, [None] * depth, [None] * depth
    d_qg, d_kg = [None] * n_a, [None] * n_a
    d_rel = jnp.zeros((n_groups * hg, LANES), F32)
    layer_grads = [{} for _ in range(depth)]
    pending = []

    def settle():
        done = []
        while pending:
            i_p, part_p, finish = pending.pop()
            layer_grads[i_p][part_p] = finish()
            done += [upd[0] for upd in layer_grads[i_p][part_p]]
        return done

    early = []

    def register(i_p, part_p, grads):
        if on_grads is None:
            layer_grads[i_p][part_p] = grads
        else:
            first, finish = on_grads(i_p, part_p, grads)
            early.extend(first)
            pending.append((i_p, part_p, finish))

    def take_early():
        first = tuple(early)
        early.clear()
        return first

    for i in reversed(range(depth)):
        j = i // 2
        w_qkv, w_o, w_up_i, w_down_i, cw = layers[i]
        s = saved[i]
        dact = mm_row_dx("ffn_down_dx", dh_b, w_down_i)
        g_down = mm_row_dw("ffn_down_dw", s["act"], dh_b)
        du2, dcw = conv_act_bwd("ffn_conv_act_bwd", s["u2"], cw, cb_full[i], dact, take_early())
        d_cw[i] = dcw[:, 0:3, :].transpose(1, 0, 2).reshape(3, 2 * dff)
        d_cb[i] = dcw[:, 3, :].reshape(2 * dff)
        g_up = mm_col_dw("ffn_up_dw", s["hn2"], du2, w_u, split=2)
        dhn2 = mm_col_dx("ffn_up_dx", du2, w_up_i, split=2)
        dh, dh_b, d_ffn[i] = rms_bwd("ffn_norm_bwd", s["h_mid"], ffn_norm[i], dhn2, dh, settle())
        register(i, "ffn", [g_up, g_down.reshape(N_DEV, -1, d)])
        if i % 2 == 0:
            do = mm_row_dx("a_out_dx", dh_b, w_o, take_early())
            g_o = mm_row_dw("a_out_dw", s["o"], dh_b)
            dlt, do_b = row_delta("a_delta", do, s["o"], n_q)
            dq, dk, dv = mixer_a_bwd(s["qkv_r"], do_b, s["lse"], dlt, n_q, n_kv)
            dqkv, dgain = qk_prep_bwd("a_qk_prep_bwd", s["qkv"], dq, dk, dv, a_q_gain[j], a_k_gain[j], cos, sin,
                                      n_q, n_kv)
            d_qg[j], d_kg[j] = dgain[0], dgain[1]
            g_qkv = mm_col_dw("a_qkv_dw", s["hn"], dqkv, w_a)
            dhn = mm_col_dx("a_qkv_dx", dqkv, w_qkv)
        else:
            dy = mm_row_dx("b_out_dx", dh_b, w_o, take_early())
            g_o = mm_row_dw("b_out_dw", s["y"], dh_b)
            res = combine_bwd("b_combine_bwd", dy, s["outs"], s["lzs"])
            dos, dlts = res[:n_groups], res[n_groups:]
            pieces, rel_rows = [], []
            for g, (win, dil) in enumerate(B_GROUPS):
                dq, dk, dv, dbias = mixer_b_group_bwd(s["qkv"], tables[g][0], dos[g], s["lzs"][g], dlts[g],
                                                      strides[g], g, n_groups, dil)
                pieces += [dq, dk, dv]
                rel_rows.append(bias_bucket_sums(f"b_bias_sums_d{dil}", dbias, tables[g][1]))
            d_rel = d_rel + jnp.concatenate(rel_rows, axis=0)
            dqkv = jnp.concatenate(pieces, axis=1)
            g_qkv = mm_col_dw("b_qkv_dw", s["hn"], dqkv, w_b)
            dhn = mm_col_dx("b_qkv_dx", dqkv, w_qkv)
        dh, dh_b, d_mix[i] = rms_bwd("mix_norm_bwd", s["h_in"], mix_norm[i], dhn, dh, settle())
        register(i, "mix", [g_qkv, g_o.reshape(N_DEV, -1, d)])
    last = pending.pop()[2] if pending else None

    d_rel_bias = d_rel[:, :REL_BUCKETS].T
    small_g = [jnp.stack(d_qg), jnp.stack(d_kg), d_rel_bias, jnp.concatenate(d_mix, 0), jnp.concatenate(d_ffn, 0),
               jnp.stack(d_cb), d_final.reshape(-1), jnp.stack(d_cw), loss_part]
    return dh, layer_grads, small_g, last


def kernel(x, a_w_qkv, a_w_o, a_q_gain, a_k_gain, b_w_qkv, b_w_o, rel_bias, mix_norm, ffn_norm, w_up, conv_w, conv_b, w_down, final_norm, loss_target, m_a_w_qkv, m_a_w_o, m_a_q_gain, m_a_k_gain, m_b_w_qkv, m_b_w_o, m_rel_bias, m_mix_norm, m_ffn_norm, m_w_up, m_conv_w, m_conv_b, m_w_down, m_final_norm, v_a_w_qkv, v_a_w_o, v_a_q_gain, v_a_k_gain, v_b_w_qkv, v_b_w_o, v_rel_bias, v_mix_norm, v_ffn_norm, v_w_up, v_conv_w, v_conv_b, v_w_down, v_final_norm):
    d = x.shape[2]
    depth = mix_norm.shape[0]
    dff = w_down.shape[1] * N_DEV
    w_u = w_up.shape[2]
    mixers = [(a_w_qkv, a_w_o, m_a_w_qkv, m_a_w_o, v_a_w_qkv, v_a_w_o),
              (b_w_qkv, b_w_o, m_b_w_qkv, m_b_w_o, v_b_w_qkv, v_b_w_o)]

    layers = []
    for i in range(depth):
        w_qkv, w_o = mixers[i % 2][0][i // 2], mixers[i % 2][1][i // 2]
        shards = [w_qkv.astype(BF16), w_o.astype(BF16), w_up[i].astype(BF16), w_down[i].astype(BF16), conv_w[i]]
        if i == 0:
            (g_qkv,) = gather_layer("gather_l0_qkv", shards[:1])
            g_o, g_up, g_down, g_cw = gather_layer("gather_l0", shards[1:])
        else:
            g_qkv, g_o, g_up, g_down, g_cw = gather_layer(f"gather_l{i}", shards)
        cw = g_cw.transpose(1, 0, 2).reshape(3, 2, dff).transpose(1, 0, 2)
        layers.append((g_qkv, g_o.reshape(-1, d), g_up, g_down.reshape(dff, d), cw))

    core = lax.axis_index("c").astype(jnp.int32).reshape(1)
    chip = (2 * lax.axis_index("x") + lax.axis_index("y")).astype(jnp.int32).reshape(1)

    def reduce_and_update(i, part, grads):
        w_qkv, w_o, m_qkv, m_o, v_qkv, v_o = mixers[i % 2]
        prefix = ("a_w_", "b_w_")[i % 2]
        state = {"mix": [(prefix + "qkv", w_qkv, m_qkv, v_qkv, i // 2), (prefix + "o", w_o, m_o, v_o, i // 2)],
                 "ffn": [("w_up", w_up, m_w_up, v_w_up, i), ("w_down", w_down, m_w_down, v_w_down, i)]}[part]
        got1 = grads_to_sibling(f"to_sibling_l{i}_{part}", grads)
        sums = [chip_sum(f"chip_sum_l{i}_{part}{a}", grads[a], got1[a], core) for a in range(2)]
        got2 = grads_to_chips(f"to_chips_l{i}_{part}", sums)

        def finish():
            for a, (key, w, m, v, layer) in enumerate(state):
                outs = big_out.get(key) or [lax.empty(w.shape, F32) for _ in range(4)]
                big_out[key] = adamw_layer(f"adamw_l{i}_{part}{a}", sums[a], got2[a], w, m, v, outs, layer, chip)
            return [big_out[key] for key, *_ in state]

        return sums, finish

    big_out = {}
    dh, _, small_g, last = _local_step(x[0], loss_target[0], layers,
                                       (a_q_gain, a_k_gain, rel_bias, mix_norm, ffn_norm, conv_b, final_norm),
                                       reduce_and_update)
    grad_x = dh[None]

    width = 2048
    packed = _pack(small_g, N_DEV * width).reshape(-1, N_DEV, width)
    n_rows = packed.shape[0]
    packed = packed.transpose(1, 0, 2).reshape(N_DEV, n_rows * width)
    red = all_reduce_small("small_all_reduce", packed)
    last()
    red = red.reshape(N_DEV, n_rows, width).transpose(1, 0, 2)
    (g_qg, g_kg, g_rel, g_mix, g_ffn, g_cb, g_fin, g_cw_all, loss) = _unpack(red, [p.shape for p in small_g])
    idx = 4 * lax.axis_index("x") + 2 * lax.axis_index("y") + lax.axis_index("c")
    g_cw_mine = lax.dynamic_slice_in_dim(g_cw_all, idx * w_u, w_u, axis=2)

    small_w = [a_q_gain, a_k_gain, rel_bias, mix_norm, ffn_norm, conv_b, final_norm, conv_w]
    small_m = [m_a_q_gain, m_a_k_gain, m_rel_bias, m_mix_norm, m_ffn_norm, m_conv_b, m_final_norm, m_conv_w]
    small_v = [v_a_q_gain, v_a_k_gain, v_rel_bias, v_mix_norm, v_ffn_norm, v_conv_b, v_final_norm, v_conv_w]
    small_grads = [g_qg, g_kg, g_rel, g_mix, g_ffn, g_cb, g_fin, g_cw_mine]
    shapes = [w.shape for w in small_w]
    pad_rows = (-_pack(small_w, width).shape[0]) % 8

    def pk8(parts):
        p = _pack(parts, width)
        return jnp.pad(p, ((0, pad_rows), (0, 0))) if pad_rows else p

    sd, sm, sv = adamw_small("adamw_small", pk8(small_grads), pk8(small_w), pk8(small_m), pk8(small_v))
    sd, sm, sv = _unpack(sd, shapes), _unpack(sm, shapes), _unpack(sv, shapes)

    names = ["a_w_qkv", "a_w_o", "a_q_gain", "a_k_gain", "b_w_qkv", "b_w_o", "rel_bias", "mix_norm", "ffn_norm",
             "w_up", "conv_w", "conv_b", "w_down", "final_norm"]
    small_names = ["a_q_gain", "a_k_gain", "rel_bias", "mix_norm", "ffn_norm", "conv_b", "final_norm", "conv_w"]
    grads, deltas, new_m, new_v = {}, {}, {}, {}
    for nm, outs in big_out.items():
        grads[nm], deltas[nm], new_m[nm], new_v[nm] = outs
    for a, nm in enumerate(small_names):
        grads[nm] = small_grads[a].reshape(shapes[a])
        deltas[nm], new_m[nm], new_v[nm] = sd[a], sm[a], sv[a]
    return (loss.reshape(()), grad_x, *[grads[n] for n in names], *[deltas[n] for n in names],
            *[new_m[n] for n in names], *[new_v[n] for n in names])
```

```python
import functools
import math

import jax
import jax.numpy as jnp
from jax import lax
from jax.experimental import pallas as pl
from jax.experimental.pallas import tpu as pltpu
from jax.experimental.pallas import tpu_sc as plsc

F32 = jnp.float32
BF16 = jnp.bfloat16
MESH = pl.DeviceIdType.MESH

N_DEV = 8
LANES = 128
HEAD_DIM = 128
VMEM_LIMIT = 56 * 1024 * 1024
GRID_W = 64
ROPE_THETA = 10000.0
A_KV_HEADS = 4
B_GROUPS = ((128, 1), (512, 4), (2048, 16))
B_HEADS_PER_GROUP = 8
REL_BUCKETS = 32
REL_MAX_DISTANCE = 1024
EPS = 1e-6
NEG_INF = -1e30
ADAM_LR = 0.001
ADAM_B1 = 0.9
ADAM_B2 = 0.999
ADAM_EPS = 1e-08
ADAM_WD = 0.01
ADAM_STEP = 10

ROW_TILE = 256
MM_TM = 1024
MM_TK = 2048
A_BQ = 1024
A_BK = 1024
B_BQ = 256
ATTN_ROWS = 32
ATTN_SCALE = HEAD_DIM ** -0.5

NN = (((1,), (0,)), ((), ()))
NT = (((1,), (1,)), ((), ()))
TN = (((0,), (0,)), ((), ()))


def _tile(n, pref):
    return pref if n % pref == 0 else n


def _div_tile(n, pref):
    for cand in range(pref - pref % LANES, 0, -LANES):
        if n % cand == 0:
            return cand
    return n


def _params(sem):
    return pltpu.CompilerParams(dimension_semantics=sem, vmem_limit_bytes=VMEM_LIMIT)


def _dot(a, b, dims):
    return lax.dot_general(a, b, dims, preferred_element_type=F32)


def _mm(name, a, b, *, grid, a_blk, a_map, b_blk, b_map, o_blk, o_map, out_shape, out_dtype, dims,
        res=None):
    nk = grid[2]
    acc_shape = tuple(d for d in o_blk if d is not None)

    def body(*refs):
        a_ref, b_ref = refs[:2]
        r_ref = None if res is None else refs[2]
        if nk == 1:
            o_ref = refs[-1]
            part = _dot(a_ref[...].astype(BF16), b_ref[...].astype(BF16), dims)
            o_ref[...] = (part if r_ref is None else part + r_ref[...]).astype(out_dtype)
            return
        o_ref, acc = refs[-2:]
        k = pl.program_id(2)

        @pl.when(k == 0)
        def _():
            acc[...] = jnp.zeros_like(acc)

        acc[...] += _dot(a_ref[...].astype(BF16), b_ref[...].astype(BF16), dims)

        @pl.when(k == nk - 1)
        def _():
            r = acc[...]
            if r_ref is not None:
                r = r + r_ref[...]
            o_ref[...] = r.astype(out_dtype)

    in_specs = [pl.BlockSpec(a_blk, a_map), pl.BlockSpec(b_blk, b_map)]
    args = [a, b]
    if res is not None:
        in_specs.append(pl.BlockSpec(o_blk, o_map))
        args.append(res)
    return pl.pallas_call(
        body, name=name, grid=grid, in_specs=in_specs, out_specs=pl.BlockSpec(o_blk, o_map),
        out_shape=jax.ShapeDtypeStruct(out_shape, out_dtype),
        scratch_shapes=[] if nk == 1 else [pltpu.VMEM(acc_shape, F32)],
        compiler_params=_params(("parallel", "parallel", "arbitrary")),
    )(*args)


def mm_col_fwd(name, a, wg, out_dtype, split=1):
    m, kdim = a.shape
    n_dev, _, w = wg.shape
    tm, tk = _tile(m, MM_TM), _div_tile(kdim, MM_TK)
    per = n_dev // split
    if split == 1:
        o_blk, o_map, o_shape = (tm, w), (lambda i, j, k: (i, j)), (m, n_dev * w)
    else:
        o_blk, o_map, o_shape = (None, tm, w), (lambda i, j, k: (j // per, i, j % per)), (split, m, per * w)
    return _mm(name, a, wg, grid=(m // tm, n_dev, kdim // tk),
               a_blk=(tm, tk), a_map=lambda i, j, k: (i, k),
               b_blk=(None, tk, w), b_map=lambda i, j, k: (j, k, 0),
               o_blk=o_blk, o_map=o_map, out_shape=o_shape, out_dtype=out_dtype, dims=NN)


def mm_col_dx(name, dy, wg, split=1):
    n_dev, kdim, w = wg.shape
    m = dy.shape[-2]
    tm, tk = _tile(m, MM_TM), _div_tile(kdim, MM_TK)
    per = n_dev // split
    if split == 1:
        a_blk, a_map = (tm, w), (lambda i, j, k: (i, k))
    else:
        a_blk, a_map = (None, tm, w), (lambda i, j, k: (k // per, i, k % per))
    return _mm(name, dy, wg, grid=(m // tm, kdim // tk, n_dev),
               a_blk=a_blk, a_map=a_map,
               b_blk=(None, tk, w), b_map=lambda i, j, k: (k, j, 0),
               o_blk=(tm, tk), o_map=lambda i, j, k: (i, j), out_shape=(m, kdim), out_dtype=F32, dims=NT)


def mm_col_dw(name, x, dy, w, split=1):
    m, kdim = x.shape
    tm, tk = _tile(m, MM_TM), _div_tile(kdim, MM_TK)
    per = N_DEV // split
    if split == 1:
        b_blk, b_map = (tm, w), (lambda i, j, k: (k, j))
    else:
        b_blk, b_map = (None, tm, w), (lambda i, j, k: (j // per, k, j % per))
    return _mm(name, x, dy, grid=(kdim // tk, N_DEV, m // tm),
               a_blk=(tm, tk), a_map=lambda i, j, k: (k, i),
               b_blk=b_blk, b_map=b_map,
               o_blk=(None, tk, w), o_map=lambda i, j, k: (j, i, 0),
               out_shape=(N_DEV, kdim, w), out_dtype=BF16, dims=TN)


def mm_row_fwd(name, a, wg, res):
    m, kdim = a.shape
    n = wg.shape[1]
    tm, tk, tn = _tile(m, MM_TM), _div_tile(kdim, MM_TK), _tile(n, 1024)
    return _mm(name, a, wg, grid=(m // tm, n // tn, kdim // tk),
               a_blk=(tm, tk), a_map=lambda i, j, k: (i, k),
               b_blk=(tk, tn), b_map=lambda i, j, k: (k, j),
               o_blk=(tm, tn), o_map=lambda i, j, k: (i, j), out_shape=(m, n), out_dtype=F32, dims=NN,
               res=res)


def mm_row_dx(name, dy, wg):
    m, n = dy.shape
    kdim = wg.shape[0]
    tm, tk, tn = _tile(m, MM_TM), _div_tile(kdim, MM_TK), _tile(n, MM_TK)
    return _mm(name, dy, wg, grid=(m // tm, kdim // tk, n // tn),
               a_blk=(tm, tn), a_map=lambda i, j, k: (i, k),
               b_blk=(tk, tn), b_map=lambda i, j, k: (j, k),
               o_blk=(tm, tk), o_map=lambda i, j, k: (i, j), out_shape=(m, kdim), out_dtype=F32, dims=NT)


def mm_row_dw(name, x, dy):
    m, kdim = x.shape
    n = dy.shape[1]
    tm, tk, tn = _tile(m, MM_TM), _div_tile(kdim, MM_TK), _tile(n, 1024)
    return _mm(name, x, dy, grid=(kdim // tk, n // tn, m // tm),
               a_blk=(tm, tk), a_map=lambda i, j, k: (k, i),
               b_blk=(tm, tn), b_map=lambda i, j, k: (k, j),
               o_blk=(tk, tn), o_map=lambda i, j, k: (i, j), out_shape=(kdim, n), out_dtype=BF16, dims=TN)


def _rows(d, tm):
    return pl.BlockSpec((tm, d), lambda i: (i, 0))


def _vec(d):
    return pl.BlockSpec((1, d), lambda i: (0, 0))


def rms_fwd(name, h, gain):
    t, d = h.shape
    tm = _tile(t, ROW_TILE)

    def body(h_ref, g_ref, o_ref):
        x = h_ref[...]
        rstd = lax.rsqrt(jnp.mean(x * x, axis=-1, keepdims=True) + EPS)
        o_ref[...] = (x * rstd * g_ref[...]).astype(BF16)

    return pl.pallas_call(
        body, name=name, grid=(t // tm,), in_specs=[_rows(d, tm), _vec(d)], out_specs=_rows(d, tm),
        out_shape=jax.ShapeDtypeStruct((t, d), BF16), compiler_params=_params(("parallel",)),
    )(h, gain.reshape(1, d))


def rms_bwd(name, h, gain, dy, dres, after=()):
    t, d = h.shape
    tm = _tile(t, ROW_TILE)

    def body(h_ref, g_ref, dy_ref, r_ref, *rest):
        dh_ref, dhb_ref, dg_ref = rest[-3:]

        @pl.when(pl.program_id(0) == 0)
        def _():
            dg_ref[...] = jnp.zeros_like(dg_ref)

        x = h_ref[...]
        rstd = lax.rsqrt(jnp.mean(x * x, axis=-1, keepdims=True) + EPS)
        xhat = x * rstd
        dyv = dy_ref[...]
        dxhat = dyv * g_ref[...]
        dh = r_ref[...] + rstd * (dxhat - xhat * jnp.mean(dxhat * xhat, axis=-1, keepdims=True))
        dh_ref[...] = dh
        dhb_ref[...] = dh.astype(BF16)
        dg_ref[...] += jnp.sum(dyv * xhat, axis=0, keepdims=True)

    return pl.pallas_call(
        body, name=name, grid=(t // tm,),
        in_specs=[_rows(d, tm), _vec(d), _rows(d, tm), _rows(d, tm)]
        + [pl.BlockSpec(memory_space=pl.ANY)] * len(after),
        out_specs=[_rows(d, tm), _rows(d, tm), _vec(d)],
        out_shape=[jax.ShapeDtypeStruct((t, d), F32), jax.ShapeDtypeStruct((t, d), BF16),
                   jax.ShapeDtypeStruct((1, d), F32)],
        compiler_params=_params(("arbitrary",)),
    )(h, gain.reshape(1, d), dy, dres, *after)


def loss_head(name, h, gain, target):
    t, d = h.shape
    tm = _tile(t, ROW_TILE)

    def body(h_ref, g_ref, t_ref, dh_ref, dhb_ref, dg_ref, loss_ref):
        @pl.when(pl.program_id(0) == 0)
        def _():
            dg_ref[...] = jnp.zeros_like(dg_ref)
            loss_ref[...] = jnp.zeros_like(loss_ref)

        x = h_ref[...]
        rstd = lax.rsqrt(jnp.mean(x * x, axis=-1, keepdims=True) + EPS)
        xhat = x * rstd
        err = xhat * g_ref[...] - t_ref[...]
        row = jnp.mean(err * err, axis=-1, keepdims=True)
        loss_ref[...] += 0.5 * jnp.sum(row, axis=0, keepdims=True)
        dyv = err * (1.0 / d)
        dxhat = dyv * g_ref[...]
        dh = rstd * (dxhat - xhat * jnp.mean(dxhat * xhat, axis=-1, keepdims=True))
        dh_ref[...] = dh
        dhb_ref[...] = dh.astype(BF16)
        dg_ref[...] += jnp.sum(dyv * xhat, axis=0, keepdims=True)

    return pl.pallas_call(
        body, name=name, grid=(t // tm,),
        in_specs=[_rows(d, tm), _vec(d), _rows(d, tm)],
        out_specs=[_rows(d, tm), _rows(d, tm), _vec(d), pl.BlockSpec((1, 1), lambda i: (0, 0))],
        out_shape=[jax.ShapeDtypeStruct((t, d), F32), jax.ShapeDtypeStruct((t, d), BF16),
                   jax.ShapeDtypeStruct((1, d), F32), jax.ShapeDtypeStruct((1, 1), F32)],
        compiler_params=_params(("arbitrary",)),
    )(h, gain.reshape(1, d), target)


def rope_tables(seq):
    pos = jnp.arange(seq, dtype=jnp.int32)
    row_ids = (pos // GRID_W).astype(F32)
    col_ids = (pos % GRID_W).astype(F32)
    quarter = HEAD_DIM // 4
    inv_freq = ROPE_THETA ** (-jnp.arange(quarter, dtype=F32) / quarter)
    ar = row_ids[:, None] * inv_freq[None, :]
    ac = col_ids[:, None] * inv_freq[None, :]
    cos = jnp.concatenate([jnp.cos(ar), jnp.cos(ar), jnp.cos(ac), jnp.cos(ac)], axis=-1)
    sin = jnp.concatenate([-jnp.sin(ar), jnp.sin(ar), -jnp.sin(ac), jnp.sin(ac)], axis=-1)
    return cos, sin


def _swap_quarters(x):
    lane = lax.broadcasted_iota(jnp.int32, x.shape, 1)
    q = HEAD_DIM // 4
    return jnp.where((lane % (2 * q)) < q, pltpu.roll(x, HEAD_DIM - q, 1), pltpu.roll(x, q, 1))


def qk_prep_fwd(name, qkv, q_gain, k_gain, cos, sin, n_q, n_kv):
    t, width = qkv.shape
    tm = _tile(t, ROW_TILE)

    def body(x_ref, qg_ref, kg_ref, c_ref, s_ref, o_ref):
        c, s = c_ref[...], s_ref[...]
        for hd in range(n_q + n_kv):
            sl = slice(hd * HEAD_DIM, (hd + 1) * HEAD_DIM)
            x = x_ref[:, sl]
            g = qg_ref[...] if hd < n_q else kg_ref[...]
            xn = x * lax.rsqrt(jnp.mean(x * x, axis=-1, keepdims=True) + EPS) * g
            o_ref[:, sl] = (xn * c + _swap_quarters(xn) * s).astype(BF16)
        vs = slice((n_q + n_kv) * HEAD_DIM, width)
        o_ref[:, vs] = x_ref[:, vs].astype(BF16)

    return pl.pallas_call(
        body, name=name, grid=(t // tm,),
        in_specs=[_rows(width, tm), _vec(HEAD_DIM), _vec(HEAD_DIM), _rows(HEAD_DIM, tm), _rows(HEAD_DIM, tm)],
        out_specs=_rows(width, tm), out_shape=jax.ShapeDtypeStruct((t, width), BF16),
        compiler_params=_params(("parallel",)),
    )(qkv, q_gain.reshape(1, HEAD_DIM), k_gain.reshape(1, HEAD_DIM), cos, sin)


def qk_prep_bwd(name, qkv, dq, dk, dv, q_gain, k_gain, cos, sin, n_q, n_kv):
    t, width = qkv.shape
    tm = _tile(t, ROW_TILE)

    def body(x_ref, dq_ref, dk_ref, dv_ref, qg_ref, kg_ref, c_ref, s_ref, o_ref, dg_ref):
        @pl.when(pl.program_id(0) == 0)
        def _():
            dg_ref[...] = jnp.zeros_like(dg_ref)

        c, s = c_ref[...], s_ref[...]
        dgq = jnp.zeros((1, HEAD_DIM), F32)
        dgk = jnp.zeros((1, HEAD_DIM), F32)
        for hd in range(n_q + n_kv):
            sl = slice(hd * HEAD_DIM, (hd + 1) * HEAD_DIM)
            x = x_ref[:, sl]
            if hd < n_q:
                g, dout = qg_ref[...], dq_ref[:, sl]
            else:
                ks = slice((hd - n_q) * HEAD_DIM, (hd - n_q + 1) * HEAD_DIM)
                g, dout = kg_ref[...], dk_ref[:, ks]
            rstd = lax.rsqrt(jnp.mean(x * x, axis=-1, keepdims=True) + EPS)
            xhat = x * rstd
            dxn = dout * c + _swap_quarters(dout * s)
            part = jnp.sum(dxn * xhat, axis=0, keepdims=True)
            if hd < n_q:
                dgq = dgq + part
            else:
                dgk = dgk + part
            dxhat = dxn * g
            o_ref[:, sl] = (rstd * (dxhat - xhat * jnp.mean(dxhat * xhat, axis=-1, keepdims=True))).astype(BF16)
        o_ref[:, slice((n_q + n_kv) * HEAD_DIM, width)] = dv_ref[...].astype(BF16)
        dg_ref[0:1, :] += dgq
        dg_ref[1:2, :] += dgk

    kvw = n_kv * HEAD_DIM
    return pl.pallas_call(
        body, name=name, grid=(t // tm,),
        in_specs=[_rows(width, tm), _rows(n_q * HEAD_DIM, tm), _rows(kvw, tm), _rows(kvw, tm),
                  _vec(HEAD_DIM), _vec(HEAD_DIM), _rows(HEAD_DIM, tm), _rows(HEAD_DIM, tm)],
        out_specs=[_rows(width, tm), pl.BlockSpec((2, HEAD_DIM), lambda i: (0, 0))],
        out_shape=[jax.ShapeDtypeStruct((t, width), BF16), jax.ShapeDtypeStruct((2, HEAD_DIM), F32)],
        compiler_params=_params(("arbitrary",)),
    )(qkv, dq, dk, dv, q_gain.reshape(1, HEAD_DIM), k_gain.reshape(1, HEAD_DIM), cos, sin)


def _lanes(x, width):
    return jnp.tile(x, (1, width // LANES))


def _hs(hd):
    return slice(hd * HEAD_DIM, (hd + 1) * HEAD_DIM)


def attn_fwd(name, q, k, v, bias, *, grid, q_spec, k_spec, v_spec, b_spec, o_spec, valid, nh, shared_kv,
             bq, bk, o_shape, o_dtype):
    ns = grid[2]

    def body(*refs):
        if bias is None:
            q_ref, k_ref, v_ref, o_ref, lse_ref, m_s, l_s, acc_s = refs
            b_ref = None
        else:
            q_ref, k_ref, v_ref, b_ref, o_ref, lse_ref, m_s, l_s, acc_s = refs
        step = pl.program_id(2)

        @pl.when(step == 0)
        def _():
            m_s[...] = jnp.full_like(m_s, -jnp.inf)
            l_s[...] = jnp.zeros_like(l_s)
            acc_s[...] = jnp.zeros_like(acc_s)

        @pl.when(valid(pl.program_id(1), step))
        def _():
            for hd in range(nh):
                kh = _hs(0 if shared_kv else hd)
                s = _dot(q_ref[:, _hs(hd)], k_ref[:, kh], NT)
                p_rows, a_rows = [], []
                for r0 in range(0, bq, ATTN_ROWS):
                    rows = slice(r0, r0 + ATTN_ROWS)
                    z = s[rows] * ATTN_SCALE
                    if b_ref is not None:
                        z = z + b_ref[hd, rows, :]
                    m_prev = m_s[hd, rows, :]
                    m_new = jnp.maximum(m_prev, jnp.max(z, axis=-1, keepdims=True))
                    alpha = jnp.exp(m_prev - m_new)
                    p = jnp.exp(z - _lanes(m_new, bk))
                    l_s[hd, rows, :] = alpha * l_s[hd, rows, :] + jnp.sum(p, axis=-1, keepdims=True)
                    m_s[hd, rows, :] = m_new
                    p_rows.append(p.astype(BF16))
                    a_rows.append(alpha)
                pv = _dot(jnp.concatenate(p_rows, axis=0), v_ref[:, kh], NN)
                acc_s[hd] = jnp.concatenate(a_rows, axis=0) * acc_s[hd] + pv

        @pl.when(step == ns - 1)
        def _():
            for hd in range(nh):
                o_ref[:, _hs(hd)] = (acc_s[hd] / l_s[hd]).astype(o_dtype)
                lse_ref[:, _hs(hd)] = m_s[hd] + jnp.log(l_s[hd])

    in_specs = [q_spec, k_spec, v_spec] + ([] if bias is None else [b_spec])
    args = [q, k, v] + ([] if bias is None else [bias])
    stat = pltpu.VMEM((nh, bq, LANES), F32)
    return pl.pallas_call(
        body, name=name, grid=grid, in_specs=in_specs, out_specs=[o_spec, o_spec],
        out_shape=[jax.ShapeDtypeStruct(o_shape, o_dtype), jax.ShapeDtypeStruct(o_shape, F32)],
        scratch_shapes=[stat, stat, stat],
        compiler_params=_params(("parallel", "parallel", "arbitrary")),
    )(*args)


def _probs(q_ref, k_ref, v_ref, do_ref, lse_ref, dlt_ref, b_ref, hd, kh, bq, bk, want_p=True, on_ds=None):
    s = _dot(q_ref[:, _hs(hd)], k_ref[:, kh], NT)
    dp = _dot(do_ref[:, _hs(hd)], v_ref[:, kh], NT)
    p_rows, ds_rows = [], []
    for r0 in range(0, bq, ATTN_ROWS):
        rows = slice(r0, r0 + ATTN_ROWS)
        z = s[rows] * ATTN_SCALE
        if b_ref is not None:
            z = z + b_ref[hd, rows, :]
        p = jnp.exp(z - _lanes(lse_ref[rows, _hs(hd)], bk))
        ds = p * (dp[rows] - _lanes(dlt_ref[rows, _hs(hd)], bk))
        if on_ds is not None:
            on_ds(rows, ds)
        if want_p:
            p_rows.append(p.astype(BF16))
        ds_rows.append(ds.astype(BF16))
    return (jnp.concatenate(p_rows, axis=0) if want_p else None), jnp.concatenate(ds_rows, axis=0)


def attn_bwd_dq(name, q, k, v, do, lse, dlt, *, grid, q_spec, k_spec, v_spec, nh, bq, bk, o_shape):
    ns = grid[2]
    scale = HEAD_DIM ** -0.5

    def body(q_ref, k_ref, v_ref, do_ref, lse_ref, dlt_ref, dq_ref, acc_s):
        step = pl.program_id(2)

        @pl.when(step == 0)
        def _():
            acc_s[...] = jnp.zeros_like(acc_s)

        for hd in range(nh):
            _, ds = _probs(q_ref, k_ref, v_ref, do_ref, lse_ref, dlt_ref, None, hd, _hs(0), bq, bk, want_p=False)
            acc_s[hd] += _dot(ds, k_ref[:, _hs(0)], NN)

        @pl.when(step == ns - 1)
        def _():
            for hd in range(nh):
                dq_ref[:, _hs(hd)] = acc_s[hd] * scale

    return pl.pallas_call(
        body, name=name, grid=grid, in_specs=[q_spec, k_spec, v_spec, q_spec, q_spec, q_spec],
        out_specs=q_spec, out_shape=jax.ShapeDtypeStruct(o_shape, F32),
        scratch_shapes=[pltpu.VMEM((nh, bq, LANES), F32)],
        compiler_params=_params(("parallel", "parallel", "arbitrary")),
    )(q, k, v, do, lse, dlt)


def _always(i, s):
    return s >= 0


def row_delta(name, do, o, n_heads, after=()):
    t, width = do.shape
    tm = _tile(t, ROW_TILE)

    def body(do_ref, o_ref, *rest):
        dl_ref, dob_ref = rest[-2:]
        for hd in range(n_heads):
            d = do_ref[:, _hs(hd)]
            s = jnp.sum(d * o_ref[:, _hs(hd)].astype(F32), axis=-1, keepdims=True)
            dl_ref[:, _hs(hd)] = jnp.broadcast_to(s, (tm, HEAD_DIM))
            dob_ref[:, _hs(hd)] = d.astype(BF16)

    return pl.pallas_call(
        body, name=name, grid=(t // tm,),
        in_specs=[_rows(width, tm), _rows(width, tm)] + [pl.BlockSpec(memory_space=pl.ANY)] * len(after),
        out_specs=[_rows(width, tm), _rows(width, tm)],
        out_shape=[jax.ShapeDtypeStruct((t, width), F32), jax.ShapeDtypeStruct((t, width), BF16)],
        compiler_params=_params(("parallel",)),
    )(do, o, *after)


def _a_specs(n_q, n_kv, bq, bk, q_major):
    grp = n_q // n_kv
    if q_major:
        qm, km = (lambda b, i, s: (i, b)), (lambda b, i, s: (s, n_q + b))
        vm = lambda b, i, s: (s, n_q + n_kv + b)
    else:
        qm, km = (lambda b, i, s: (s, b)), (lambda b, i, s: (i, n_q + b))
        vm = lambda b, i, s: (i, n_q + n_kv + b)
    return (pl.BlockSpec((bq, grp * HEAD_DIM), qm), pl.BlockSpec((bk, HEAD_DIM), km),
            pl.BlockSpec((bk, HEAD_DIM), vm))


def mixer_a_fwd(qkv_r, n_q, n_kv):
    t = qkv_r.shape[0]
    bq, bk = _tile(t, A_BQ), _tile(t, A_BK)
    q_spec, k_spec, v_spec = _a_specs(n_q, n_kv, bq, bk, True)
    return attn_fwd("a_attn_fwd", qkv_r, qkv_r, qkv_r, None, grid=(n_kv, t // bq, t // bk),
                    q_spec=q_spec, k_spec=k_spec, v_spec=v_spec, b_spec=None, o_spec=q_spec, valid=_always,
                    nh=n_q // n_kv, shared_kv=True, bq=bq, bk=bk, o_shape=(t, n_q * HEAD_DIM), o_dtype=BF16)


def mixer_a_bwd(qkv_r, do_b, lse, dlt, n_q, n_kv):
    t = qkv_r.shape[0]
    bq, bk = _tile(t, A_BQ), _tile(t, A_BK)
    grp = n_q // n_kv
    q_spec, k_spec, v_spec = _a_specs(n_q, n_kv, bq, bk, True)
    dq = attn_bwd_dq("a_attn_dq", qkv_r, qkv_r, qkv_r, do_b, lse, dlt, grid=(n_kv, t // bq, t // bk),
                     q_spec=q_spec, k_spec=k_spec, v_spec=v_spec, nh=grp, bq=bq, bk=bk,
                     o_shape=(t, n_q * HEAD_DIM))
    q_spec, k_spec, v_spec = _a_specs(n_q, n_kv, bq, bk, False)
    o_spec = pl.BlockSpec((bk, HEAD_DIM), lambda b, i, s: (i, b))
    dk, dv = _attn_bwd_dkv_out(qkv_r, do_b, lse, dlt, grid=(n_kv, t // bk, t // bq), q_spec=q_spec,
                               k_spec=k_spec, v_spec=v_spec, o_spec=o_spec, grp=grp, bq=bq, bk=bk,
                               o_shape=(t, n_kv * HEAD_DIM))
    return dq, dk, dv


def _attn_bwd_dkv_out(qkv_r, do_b, lse, dlt, *, grid, q_spec, k_spec, v_spec, o_spec, grp, bq, bk, o_shape):
    ns = grid[2]
    scale = HEAD_DIM ** -0.5

    def body(q_ref, k_ref, v_ref, do_ref, lse_ref, dlt_ref, dk_ref, dv_ref, dk_s, dv_s):
        step = pl.program_id(2)

        @pl.when(step == 0)
        def _():
            dk_s[...] = jnp.zeros_like(dk_s)
            dv_s[...] = jnp.zeros_like(dv_s)

        for hd in range(grp):
            p, ds = _probs(q_ref, k_ref, v_ref, do_ref, lse_ref, dlt_ref, None, hd, _hs(0), bq, bk)
            dv_s[...] += _dot(p, do_ref[:, _hs(hd)], TN)
            dk_s[...] += _dot(ds, q_ref[:, _hs(hd)], TN)

        @pl.when(step == ns - 1)
        def _():
            dk_ref[...] = dk_s[...] * scale
            dv_ref[...] = dv_s[...]

    acc = pltpu.VMEM((bk, HEAD_DIM), F32)
    return pl.pallas_call(
        body, name="a_attn_dkv", grid=grid, in_specs=[q_spec, k_spec, v_spec, q_spec, q_spec, q_spec],
        out_specs=[o_spec, o_spec], out_shape=[jax.ShapeDtypeStruct(o_shape, F32)] * 2,
        scratch_shapes=[acc, acc], compiler_params=_params(("parallel", "parallel", "arbitrary")),
    )(qkv_r, qkv_r, qkv_r, do_b, lse, dlt)


def t5_bucket(rel):
    nb = REL_BUCKETS // 2
    max_exact = nb // 2
    base = jnp.where(rel > 0, nb, 0)
    n = jnp.abs(rel)
    nf = jnp.maximum(n, 1).astype(F32)
    large = max_exact + (jnp.log(nf / max_exact) / math.log(REL_MAX_DISTANCE / max_exact)
                         * (nb - max_exact)).astype(jnp.int32)
    large = jnp.minimum(large, nb - 1)
    return base + jnp.where(n < max_exact, n, large)


def band_stride(t, win, dil):
    return 1 if t % B_BQ == 0 and win // 2 <= B_BQ else dil


def band_tables(rel_bias_g, win, dil, stride, bq):
    a = jnp.arange(bq)[:, None]
    b = jnp.arange(bq)[None, :]
    rel = jnp.stack([(s - 1) * bq + b - a for s in range(3)]) * stride
    ok = (jnp.abs(rel) <= win // 2) & (rel % dil == 0)
    bucket = t5_bucket(rel)
    bias = jnp.zeros((rel_bias_g.shape[1],) + rel.shape, F32)
    for r in range(REL_BUCKETS):
        bias = bias + jnp.where(bucket[None] == r, rel_bias_g[r][:, None, None, None], 0.0)
    return jnp.where(ok[None], bias, NEG_INF), jnp.where(ok, bucket, -1).astype(jnp.int32)


def band_block(t, stride):
    return _tile(t // stride, B_BQ)


def _b_geometry(t, dil, g, n_groups, bq):
    hg = B_HEADS_PER_GROUP
    length = t // dil
    nblk = length // bq
    gw = hg * HEAD_DIM
    per_tok = 3 * n_groups
    return hg, length, bq, nblk, gw, per_tok


def mixer_b_group_fwd(qkv, bias, dil, g, n_groups, tag):
    t = qkv.shape[0]
    hg, length, bq, nblk, gw, per_tok = _b_geometry(t, dil, g, n_groups, bias.shape[2])
    if dil > 1:
        qkv, g, per_tok = qkv[:, 3 * g * gw:3 * (g + 1) * gw], 0, 3
    view = qkv.reshape(length, dil * qkv.shape[1])
    col = lambda c, which: c * per_tok + 3 * g + which
    kblk = lambda i, s: jnp.clip(i - 1 + s, 0, nblk - 1)
    spec = lambda which, streamed: pl.BlockSpec(
        (bq, gw), (lambda c, i, s: (kblk(i, s), col(c, which))) if streamed else (lambda c, i, s: (i, col(c, which))))
    valid = lambda i, s: (i - 1 + s >= 0) & (i - 1 + s < nblk)
    o, lz = attn_fwd(f"b_attn_fwd_d{tag}", view, view, view, bias, grid=(dil, nblk, 3),
                     q_spec=spec(0, False), k_spec=spec(1, True), v_spec=spec(2, True),
                     b_spec=pl.BlockSpec((hg, None, bq, bq), lambda c, i, s: (0, s, 0, 0)),
                     o_spec=pl.BlockSpec((bq, gw), lambda c, i, s: (i, c)), valid=valid, nh=hg,
                     shared_kv=False, bq=bq, bk=bq, o_shape=(length, dil * gw), o_dtype=F32)
    return o.reshape(t, gw), lz.reshape(t, gw)


def mixer_b_group_bwd(qkv, bias, do_g, lz_g, dlt_g, dil, g, n_groups, tag):
    t = qkv.shape[0]
    hg, length, bq, nblk, gw, per_tok = _b_geometry(t, dil, g, n_groups, bias.shape[2])
    if dil > 1:
        qkv, g, per_tok = qkv[:, 3 * g * gw:3 * (g + 1) * gw], 0, 3
    view = qkv.reshape(length, dil * qkv.shape[1])
    dov, lzv, dlv = (x.reshape(length, dil * gw) for x in (do_g, lz_g, dlt_g))
    col = lambda c, which: c * per_tok + 3 * g + which
    nbr = lambda i, s: jnp.clip(i - 1 + s, 0, nblk - 1)
    valid = lambda i, s: (i - 1 + s >= 0) & (i - 1 + s < nblk)
    q_spec = pl.BlockSpec((bq, gw), lambda c, i, s: (i, col(c, 0)))
    k_spec = pl.BlockSpec((bq, gw), lambda c, i, s: (nbr(i, s), col(c, 1)))
    v_spec = pl.BlockSpec((bq, gw), lambda c, i, s: (nbr(i, s), col(c, 2)))
    stat = pl.BlockSpec((bq, gw), lambda c, i, s: (i, c))
    dq, dbias = _band_bwd_dq(f"b_attn_dq_d{tag}", view, dov, lzv, dlv, bias, grid=(dil, nblk, 3),
                             q_spec=q_spec, k_spec=k_spec, v_spec=v_spec, stat_spec=stat,
                             b_spec=pl.BlockSpec((hg, None, bq, bq), lambda c, i, s: (0, s, 0, 0)),
                             valid=valid, nh=hg, bq=bq, o_shape=(length, dil * gw))
    q_spec = pl.BlockSpec((bq, gw), lambda c, i, s: (nbr(i, s), col(c, 0)))
    k_spec = pl.BlockSpec((bq, gw), lambda c, i, s: (i, col(c, 1)))
    v_spec = pl.BlockSpec((bq, gw), lambda c, i, s: (i, col(c, 2)))
    stat = pl.BlockSpec((bq, gw), lambda c, i, s: (nbr(i, s), c))
    dk, dv = _band_bwd_dkv(f"b_attn_dkv_d{tag}", view, dov, lzv, dlv, bias, grid=(dil, nblk, 3),
                           q_spec=q_spec, k_spec=k_spec, v_spec=v_spec, stat_spec=stat,
                           b_spec=pl.BlockSpec((hg, None, bq, bq), lambda c, i, s: (0, 2 - s, 0, 0)),
                           o_spec=pl.BlockSpec((bq, gw), lambda c, i, s: (i, c)),
                           valid=valid, nh=hg, bq=bq, o_shape=(length, dil * gw))
    return dq.reshape(t, gw), dk.reshape(t, gw), dv.reshape(t, gw), dbias


def _band_bwd_dq(name, view, do, lse, dlt, bias, *, grid, q_spec, k_spec, v_spec, stat_spec, b_spec, valid,
                 nh, bq, o_shape):
    scale = HEAD_DIM ** -0.5
    bias_shape = (nh, 3, bq, bq)

    def body(q_ref, k_ref, v_ref, do_ref, lse_ref, dlt_ref, b_ref, dq_ref, db_ref, acc_s):
        step = pl.program_id(2)

        @pl.when((pl.program_id(0) == 0) & (pl.program_id(1) == 0) & (step == 0))
        def _():
            db_ref[...] = jnp.zeros_like(db_ref)

        @pl.when(step == 0)
        def _():
            acc_s[...] = jnp.zeros_like(acc_s)

        @pl.when(valid(pl.program_id(1), step))
        def _():
            for hd in range(nh):
                def add_bias_grad(rows, ds, hd=hd):
                    db_ref[hd, step, rows, :] += ds

                _, ds = _probs(q_ref, k_ref, v_ref, do_ref, lse_ref, dlt_ref, b_ref, hd, _hs(hd), bq, bq,
                               want_p=False, on_ds=add_bias_grad)
                acc_s[hd] += _dot(ds, k_ref[:, _hs(hd)], NN)

        @pl.when(step == 2)
        def _():
            for hd in range(nh):
                dq_ref[:, _hs(hd)] = (acc_s[hd] * scale).astype(BF16)

    return pl.pallas_call(
        body, name=name, grid=grid,
        in_specs=[q_spec, k_spec, v_spec, stat_spec, stat_spec, stat_spec, b_spec],
        out_specs=[stat_spec, pl.BlockSpec(bias_shape, lambda c, i, s: (0, 0, 0, 0))],
        out_shape=[jax.ShapeDtypeStruct(o_shape, BF16), jax.ShapeDtypeStruct(bias_shape, F32)],
        scratch_shapes=[pltpu.VMEM((nh, bq, LANES), F32)], compiler_params=_params(("arbitrary",) * 3),
    )(view, view, view, do, lse, dlt, bias)


def _band_bwd_dkv(name, view, do, lse, dlt, bias, *, grid, q_spec, k_spec, v_spec, stat_spec, b_spec, o_spec,
                  valid, nh, bq, o_shape):
    scale = HEAD_DIM ** -0.5

    def body(q_ref, k_ref, v_ref, do_ref, lse_ref, dlt_ref, b_ref, dk_ref, dv_ref, dk_s, dv_s):
        step = pl.program_id(2)

        @pl.when(step == 0)
        def _():
            dk_s[...] = jnp.zeros_like(dk_s)
            dv_s[...] = jnp.zeros_like(dv_s)

        @pl.when(valid(pl.program_id(1), step))
        def _():
            for hd in range(nh):
                p, ds = _probs(q_ref, k_ref, v_ref, do_ref, lse_ref, dlt_ref, b_ref, hd, _hs(hd), bq, bq)
                dv_s[hd] += _dot(p, do_ref[:, _hs(hd)], TN)
                dk_s[hd] += _dot(ds, q_ref[:, _hs(hd)], TN)

        @pl.when(step == 2)
        def _():
            for hd in range(nh):
                dk_ref[:, _hs(hd)] = (dk_s[hd] * scale).astype(BF16)
                dv_ref[:, _hs(hd)] = dv_s[hd].astype(BF16)

    acc = pltpu.VMEM((nh, bq, LANES), F32)
    return pl.pallas_call(
        body, name=name, grid=grid,
        in_specs=[q_spec, k_spec, v_spec, stat_spec, stat_spec, stat_spec, b_spec],
        out_specs=[o_spec, o_spec], out_shape=[jax.ShapeDtypeStruct(o_shape, BF16)] * 2,
        scratch_shapes=[acc, acc], compiler_params=_params(("parallel", "parallel", "arbitrary")),
    )(view, view, view, do, lse, dlt, bias)


def bias_bucket_sums(name, dbias, bucket):
    nh, _, bq, _ = dbias.shape
    db2 = dbias.reshape(nh, 3 * bq, bq)
    bk2 = bucket.reshape(3 * bq, bq)

    def body(db_ref, bk_ref, o_ref):
        row = lax.broadcasted_iota(jnp.int32, (nh, LANES), 0)
        lane = lax.broadcasted_iota(jnp.int32, (nh, LANES), 1)
        out = jnp.zeros((nh, LANES), F32)
        bkt = bk_ref[...]
        for hd in range(nh):
            x = db_ref[hd]
            for r in range(REL_BUCKETS):
                part = jnp.sum(jnp.where(bkt == r, x, 0.0), axis=1, keepdims=True)
                tot = jnp.sum(part, axis=0, keepdims=True)
                out = out + jnp.where((row == hd) & (lane == r), tot, 0.0)
        o_ref[...] = out

    return pl.pallas_call(
        body, name=name, out_shape=jax.ShapeDtypeStruct((nh, LANES), F32),
        compiler_params=pltpu.CompilerParams(vmem_limit_bytes=VMEM_LIMIT),
    )(db2, bk2)


def combine_fwd(name, outs, lzs):
    n_g = len(outs)
    t, gw = outs[0].shape
    tm = _tile(t, ROW_TILE)

    def body(*refs):
        o_refs, lz_refs, y_ref = refs[:n_g], refs[n_g:2 * n_g], refs[2 * n_g]
        lz = [r[...] for r in lz_refs]
        mx = functools.reduce(jnp.maximum, lz)
        e = [jnp.exp(x - mx) for x in lz]
        den = functools.reduce(lambda a, b: a + b, e)
        for g in range(n_g):
            y_ref[:, g * gw:(g + 1) * gw] = (e[g] / den * o_refs[g][...]).astype(BF16)

    return pl.pallas_call(
        body, name=name, grid=(t // tm,), in_specs=[_rows(gw, tm)] * (2 * n_g), out_specs=_rows(n_g * gw, tm),
        out_shape=jax.ShapeDtypeStruct((t, n_g * gw), BF16), compiler_params=_params(("parallel",)),
    )(*outs, *lzs)


def combine_bwd(name, dy, outs, lzs, after=()):
    n_g = len(outs)
    t, gw = outs[0].shape
    tm = _tile(t, ROW_TILE)
    nh = gw // HEAD_DIM

    def body(*refs):
        dy_ref = refs[0]
        o_refs, lz_refs = refs[1:1 + n_g], refs[1 + n_g:1 + 2 * n_g]
        do_refs, dl_refs = refs[-2 * n_g:-n_g], refs[-n_g:]
        lz = [r[...] for r in lz_refs]
        mx = functools.reduce(jnp.maximum, lz)
        e = [jnp.exp(x - mx) for x in lz]
        den = functools.reduce(lambda a, b: a + b, e)
        wts = [x / den for x in e]
        for g in range(n_g):
            do_refs[g][...] = (wts[g] * dy_ref[:, g * gw:(g + 1) * gw]).astype(BF16)
        for hd in range(nh):
            mix = jnp.zeros((tm, HEAD_DIM), F32)
            for g in range(n_g):
                prod = dy_ref[:, g * gw + hd * HEAD_DIM:g * gw + (hd + 1) * HEAD_DIM] * o_refs[g][:, _hs(hd)]
                dw = jnp.broadcast_to(jnp.sum(prod, axis=-1, keepdims=True), (tm, HEAD_DIM))
                mix = mix + wts[g][:, _hs(hd)] * dw
            for g in range(n_g):
                dl_refs[g][:, _hs(hd)] = wts[g][:, _hs(hd)] * mix

    return pl.pallas_call(
        body, name=name, grid=(t // tm,),
        in_specs=[_rows(n_g * gw, tm)] + [_rows(gw, tm)] * (2 * n_g) + [pl.BlockSpec(memory_space=pl.ANY)] * len(after),
        out_specs=[_rows(gw, tm)] * (2 * n_g),
        out_shape=[jax.ShapeDtypeStruct((t, gw), BF16)] * n_g + [jax.ShapeDtypeStruct((t, gw), F32)] * n_g,
        compiler_params=_params(("parallel",)),
    )(dy, *outs, *lzs, *after)


def _shifted(u):
    t = u.shape[0]
    row = lax.broadcasted_iota(jnp.int32, u.shape, 0)
    prev = jnp.where(row == 0, 0.0, pltpu.roll(u, 1, 0))
    nxt = jnp.where(row == t - 1, 0.0, pltpu.roll(u, t - 1, 0))
    return prev, nxt


def _conv3(u, prev, nxt, w_ref, b):
    return w_ref[0:1, :] * prev + w_ref[1:2, :] * u + w_ref[2:3, :] * nxt + b


def _conv3_t(d, w_ref):
    prev, nxt = _shifted(d)
    return w_ref[0:1, :] * nxt + w_ref[1:2, :] * d + w_ref[2:3, :] * prev


def conv_act_fwd(name, u2, cw2, cb2):
    _, t, dff = u2.shape
    tn = LANES

    def body(u_ref, w_ref, b_ref, o_ref):
        ug, uv = u_ref[0], u_ref[1]
        cg = _conv3(ug, *_shifted(ug), w_ref.at[0], b_ref[0])
        cv = _conv3(uv, *_shifted(uv), w_ref.at[1], b_ref[1])
        o_ref[...] = (cg * jax.nn.sigmoid(cg) * cv).astype(BF16)

    return pl.pallas_call(
        body, name=name, grid=(dff // tn,),
        in_specs=[pl.BlockSpec((2, t, tn), lambda j: (0, 0, j)), pl.BlockSpec((2, 3, tn), lambda j: (0, 0, j)),
                  pl.BlockSpec((2, 1, tn), lambda j: (0, 0, j))],
        out_specs=pl.BlockSpec((t, tn), lambda j: (0, j)), out_shape=jax.ShapeDtypeStruct((t, dff), BF16),
        compiler_params=_params(("parallel",)),
    )(u2, cw2, cb2)


def conv_act_bwd(name, u2, cw2, cb2, dact, after=()):
    _, t, dff = u2.shape
    tn = LANES

    def body(u_ref, w_ref, b_ref, d_ref, *rest):
        du_ref, dw_ref = rest[-2:]
        d = d_ref[...]
        ug, uv = u_ref[0], u_ref[1]
        shifted = (_shifted(ug), _shifted(uv))
        cg = _conv3(ug, *shifted[0], w_ref.at[0], b_ref[0])
        cv = _conv3(uv, *shifted[1], w_ref.at[1], b_ref[1])
        sg = jax.nn.sigmoid(cg)
        dcv = d * (cg * sg)
        dcg = d * cv * (sg * (1.0 + cg * (1.0 - sg)))
        du_ref[0] = _conv3_t(dcg, w_ref.at[0]).astype(BF16)
        du_ref[1] = _conv3_t(dcv, w_ref.at[1]).astype(BF16)
        for half, (dc, u) in enumerate(((dcg, ug), (dcv, uv))):
            prev, nxt = shifted[half]
            for tap, x in enumerate((prev, u, nxt)):
                dw_ref[half, tap:tap + 1, :] = jnp.sum(dc * x, axis=0, keepdims=True)
            dw_ref[half, 3:4, :] = jnp.sum(dc, axis=0, keepdims=True)
            dw_ref[half, 4:8, :] = jnp.zeros((4, tn), F32)

    return pl.pallas_call(
        body, name=name, grid=(dff // tn,),
        in_specs=[pl.BlockSpec((2, t, tn), lambda j: (0, 0, j)), pl.BlockSpec((2, 3, tn), lambda j: (0, 0, j)),
                  pl.BlockSpec((2, 1, tn), lambda j: (0, 0, j)), pl.BlockSpec((t, tn), lambda j: (0, j))]
        + [pl.BlockSpec(memory_space=pl.ANY)] * len(after),
        out_specs=[pl.BlockSpec((2, t, tn), lambda j: (0, 0, j)), pl.BlockSpec((2, 8, tn), lambda j: (0, 0, j))],
        out_shape=[jax.ShapeDtypeStruct((2, t, dff), BF16), jax.ShapeDtypeStruct((2, 8, dff), F32)],
        compiler_params=_params(("parallel",)),
    )(u2, cw2, cb2, dact, *after)


GATHER_ID, SIBLING_ID, CHIPS_ID = 0, 1, 2


def _place():
    x, y, c = lax.axis_index("x"), lax.axis_index("y"), lax.axis_index("c")
    chips = [(1 - x, y), (x, 1 - y), (1 - x, 1 - y)]
    return x, y, c, chips


def _handshake(peers):
    barrier = pltpu.get_barrier_semaphore()
    for peer in peers:
        pl.semaphore_signal(barrier, inc=1, device_id=peer, device_id_type=MESH)
    pl.semaphore_wait(barrier, len(peers))


def _sequencer(name, body, out_type, scratch_types, collective_id):
    return pl.kernel(body, out_type=out_type, mesh=plsc.ScalarSubcoreMesh(axis_name="seq", num_cores=1),
                     scratch_types=scratch_types, name=name,
                     compiler_params=pltpu.CompilerParams(collective_id=collective_id))


def _gather_body(n):
    def body(*refs):
        src, out = refs[:n], refs[n:2 * n]
        send, recv, loc = refs[2 * n:]
        x, y, c, chips = _place()
        sibling = (x, y, 1 - c)
        _handshake([sibling] + [(*chip, c) for chip in chips])

        def slot(a, px, py, pc):
            return out[a].at[4 * px + 2 * py + pc]

        def copy(a, k, block, to, from_src=False):
            return pltpu.make_async_remote_copy(
                src_ref=src[a] if from_src else slot(a, *block), dst_ref=slot(a, *block),
                send_sem=send.at[a, k], recv_sem=recv.at[a, k], device_id=to, device_id_type=MESH)

        mine = [pltpu.make_async_copy(src[a], slot(a, x, y, c), loc.at[a]) for a in range(n)]
        for cp in mine:
            cp.start()
        first = []
        for a in range(n):
            first.append(copy(a, 0, (x, y, c), sibling, True))
            first += [copy(a, 1 + j, (x, y, c), (*chip, c), True) for j, chip in enumerate(chips)]
        for cp in first:
            cp.start()
        passed = []
        for j, chip in enumerate(chips):
            for a in range(n):
                copy(a, 1 + j, (*chip, c), (x, y, c)).wait_recv()
                cp = copy(a, 4 + j, (*chip, c), sibling)
                cp.start()
                passed.append(cp)
        for a in range(n):
            copy(a, 0, sibling, (x, y, c)).wait_recv()
            for j, chip in enumerate(chips):
                copy(a, 4 + j, (*chip, 1 - c), (x, y, c)).wait_recv()
        for cp in first + passed:
            cp.wait_send()
        for cp in mine:
            cp.wait()

    return body


def gather_layer(name, shards):
    n = len(shards)
    out_type = [jax.ShapeDtypeStruct((N_DEV,) + s.shape, s.dtype) for s in shards]
    scratch = [pltpu.SemaphoreType.DMA((n, 7)), pltpu.SemaphoreType.DMA((n, 7)), pltpu.SemaphoreType.DMA((n,))]
    return _sequencer(name, _gather_body(n), out_type, scratch, GATHER_ID)(*shards)


def _to_sibling_body(n):
    def body(*refs):
        src, got = refs[:n], refs[n:2 * n]
        send, recv = refs[2 * n:]
        x, y, c, _ = _place()
        sibling = (x, y, 1 - c)
        _handshake([sibling])
        remote = []
        for a in range(n):
            for q in range(4):
                remote.append(pltpu.make_async_remote_copy(
                    src_ref=src[a].at[2 * q + 1 - c], dst_ref=got[a].at[q], send_sem=send.at[a, q],
                    recv_sem=recv.at[a, q], device_id=sibling, device_id_type=MESH))
        for cp in remote:
            cp.start()
        for cp in remote:
            cp.wait()

    return body


def grads_to_sibling(name, grads):
    n = len(grads)
    out_type = [jax.ShapeDtypeStruct((4,) + g.shape[1:], g.dtype) for g in grads]
    scratch = [pltpu.SemaphoreType.DMA((n, 4)), pltpu.SemaphoreType.DMA((n, 4))]
    return _sequencer(name, _to_sibling_body(n), out_type, scratch, SIBLING_ID)(*grads)


def _to_chips_body(n):
    def body(*refs):
        src, got = refs[:n], refs[n:2 * n]
        send, recv = refs[2 * n:]
        x, y, c, chips = _place()
        _handshake([(*chip, c) for chip in chips])
        remote = []
        for a in range(n):
            for j, (px, py) in enumerate(chips):
                remote.append(pltpu.make_async_remote_copy(
                    src_ref=src[a].at[2 * px + py], dst_ref=got[a].at[j], send_sem=send.at[a, j],
                    recv_sem=recv.at[a, j], device_id=(px, py, c), device_id_type=MESH))
        for cp in remote:
            cp.start()
        for cp in remote:
            cp.wait()

    return body


def grads_to_chips(name, parts):
    n = len(parts)
    out_type = [jax.ShapeDtypeStruct((3,) + p.shape[1:], p.dtype) for p in parts]
    scratch = [pltpu.SemaphoreType.DMA((n, 3)), pltpu.SemaphoreType.DMA((n, 3))]
    return _sequencer(name, _to_chips_body(n), out_type, scratch, CHIPS_ID)(*parts)


def all_reduce_small(name, vec):
    rows, m = vec.shape

    def body(x_ref, o_ref, buf, send, recv):
        x, y, c, chips = _place()
        sibling = (x, y, 1 - c)

        def blk(px, py, pc):
            return buf.at[pl.ds(pl.multiple_of((4 * px + 2 * py + pc) * rows, rows), rows), :]

        def copy(k, block, to):
            return pltpu.make_async_remote_copy(src_ref=blk(*block), dst_ref=blk(*block), send_sem=send.at[k],
                                                recv_sem=recv.at[k], device_id=to, device_id_type=MESH)

        blk(x, y, c)[...] = x_ref[...]
        first = [copy(0, (x, y, c), sibling)] + [copy(1 + j, (x, y, c), (*chip, c)) for j, chip in enumerate(chips)]
        for cp in first:
            cp.start()
        passed = [copy(4 + j, (*chip, c), sibling) for j, chip in enumerate(chips)]
        for j, chip in enumerate(chips):
            copy(1 + j, (*chip, c), (x, y, c)).wait_recv()
            passed[j].start()
        copy(0, sibling, (x, y, c)).wait_recv()
        for j, chip in enumerate(chips):
            copy(4 + j, (*chip, 1 - c), (x, y, c)).wait_recv()
        for cp in first + passed:
            cp.wait_send()
        tot = buf[0:rows, :]
        for dev in range(1, N_DEV):
            tot = tot + buf[dev * rows:(dev + 1) * rows, :]
        o_ref[...] = tot

    return pl.pallas_call(
        body, name=name, in_specs=[pl.BlockSpec(memory_space=pltpu.VMEM)],
        out_specs=pl.BlockSpec(memory_space=pltpu.VMEM), out_shape=jax.ShapeDtypeStruct((rows, m), F32),
        scratch_shapes=[pltpu.VMEM((N_DEV * rows, m), F32), pltpu.SemaphoreType.DMA((7,)),
                        pltpu.SemaphoreType.DMA((7,))],
        compiler_params=pltpu.CompilerParams(vmem_limit_bytes=VMEM_LIMIT),
    )(vec)


def _ew_tiles(rows, cols, max_elems=1 << 18):
    tr = rows
    for cand in (1024, 512, 256, 128, 64, 32, 16):
        if rows % cand == 0 and cand * cols <= max_elems:
            tr = cand
            break
    return tr


def chip_sum(name, full, got, core):
    _, kdim, ncol = full.shape
    tr = _ew_tiles(kdim, ncol, max_elems=1 << 20)
    blk = (None, tr, ncol)
    by_chip = pl.BlockSpec(blk, lambda q, i, c: (q, i, 0))

    def body(c_ref, a_ref, b_ref, o_ref):
        o_ref[...] = (a_ref[...].astype(F32) + b_ref[...].astype(F32)).astype(BF16)

    return pl.pallas_call(
        body, name=name,
        grid_spec=pltpu.PrefetchScalarGridSpec(
            num_scalar_prefetch=1, grid=(4, kdim // tr),
            in_specs=[pl.BlockSpec(blk, lambda q, i, c: (2 * q + c[0], i, 0)), by_chip], out_specs=by_chip),
        out_shape=jax.ShapeDtypeStruct((4, kdim, ncol), BF16),
        compiler_params=_params(("parallel", "parallel")),
    )(core, full, got)


def _adamw_math(w, g, m, v):
    m = ADAM_B1 * m + (1.0 - ADAM_B1) * g
    v = ADAM_B2 * v + (1.0 - ADAM_B2) * (g * g)
    m_hat = m / (1.0 - ADAM_B1 ** ADAM_STEP)
    v_hat = v / (1.0 - ADAM_B2 ** ADAM_STEP)
    delta = -ADAM_LR * (m_hat / (jnp.sqrt(v_hat) + ADAM_EPS) + ADAM_WD * w)
    return delta, m, v


def adamw_layer(name, sums, got, w, m, v, outs, layer, chip):
    _, kdim, ncol = sums.shape
    tr = _ew_tiles(kdim, ncol)
    mine = pl.BlockSpec((None, tr, ncol), lambda i, q: (q[0], i, 0))
    others = pl.BlockSpec((3, tr, ncol), lambda i, q: (0, i, 0))
    param = pl.BlockSpec((None, tr, ncol), lambda i, q: (layer, i, 0))
    whole = pl.BlockSpec(memory_space=pl.ANY)

    def body(q_ref, o_ref, g_ref, w_ref, m_ref, v_ref, *rest):
        go_ref, d_ref, mo_ref, vo_ref = rest[-4:]
        g = o_ref[...].astype(F32)
        for j in range(3):
            g = g + g_ref[j].astype(F32)
        d, mn, vn = _adamw_math(w_ref[...], g, m_ref[...], v_ref[...])
        go_ref[...] = g
        d_ref[...] = d
        mo_ref[...] = mn
        vo_ref[...] = vn

    n_in = 6
    return pl.pallas_call(
        body, name=name,
        grid_spec=pltpu.PrefetchScalarGridSpec(
            num_scalar_prefetch=1, grid=(kdim // tr,),
            in_specs=[mine, others, param, param, param] + [whole] * 4, out_specs=[param] * 4),
        out_shape=[jax.ShapeDtypeStruct(w.shape, F32)] * 4,
        input_output_aliases={n_in + k: k for k in range(4)},
        compiler_params=_params(("parallel",)),
    )(chip, sums, got, w, m, v, *outs)


def adamw_small(name, g, w, m, v):
    def body(g_ref, w_ref, m_ref, v_ref, d_ref, mo_ref, vo_ref):
        d, mn, vn = _adamw_math(w_ref[...], g_ref[...], m_ref[...], v_ref[...])
        d_ref[...] = d
        mo_ref[...] = mn
        vo_ref[...] = vn

    vm = pl.BlockSpec(memory_space=pltpu.VMEM)
    return pl.pallas_call(
        body, name=name, in_specs=[vm] * 4, out_specs=[vm] * 3,
        out_shape=[jax.ShapeDtypeStruct(g.shape, F32)] * 3,
        compiler_params=pltpu.CompilerParams(vmem_limit_bytes=VMEM_LIMIT),
    )(g, w, m, v)


def _pack(parts, width):
    flat = jnp.concatenate([p.reshape(-1).astype(F32) for p in parts])
    pad = (-flat.shape[0]) % width
    return jnp.pad(flat, (0, pad)).reshape(-1, width) if pad else flat.reshape(-1, width)


def _unpack(packed, shapes):
    flat = packed.reshape(-1)
    out, off = [], 0
    for s in shapes:
        size = math.prod(s)
        out.append(flat[off:off + size].reshape(s))
        off += size
    return out


def _local_step(h, target, layers, params, on_grads=None):
    a_q_gain, a_k_gain, rel_bias, mix_norm, ffn_norm, conv_b, final_norm = params
    t, d = h.shape
    depth = len(layers)
    n_groups = len(B_GROUPS)
    hg = B_HEADS_PER_GROUP
    n_kv = A_KV_HEADS
    w_a, w_b, w_u = layers[0][0].shape[2], layers[1][0].shape[2], layers[0][2].shape[2]
    n_q = w_a * N_DEV // HEAD_DIM - 2 * n_kv
    dff = layers[0][3].shape[0]
    n_a = (depth + 1) // 2
    cb_full = conv_b.reshape(depth, 2, 1, dff)

    cos, sin = rope_tables(t)
    strides = [band_stride(t, win, dil) for win, dil in B_GROUPS]
    tables = [band_tables(rel_bias[:, g * hg:(g + 1) * hg], win, dil, strides[g], band_block(t, strides[g]))
              for g, (win, dil) in enumerate(B_GROUPS)]

    saved = []
    for i in range(depth):
        j = i // 2
        w_qkv, w_o, w_up_i, w_down_i, cw = layers[i]
        s = {"h_in": h}
        hn = rms_fwd("mix_norm_fwd", h, mix_norm[i])
        s["hn"] = hn
        if i % 2 == 0:
            qkv = mm_col_fwd("a_qkv_fwd", hn, w_qkv, F32)
            qkv_r = qk_prep_fwd("a_qk_prep_fwd", qkv, a_q_gain[j], a_k_gain[j], cos, sin, n_q, n_kv)
            o, lse = mixer_a_fwd(qkv_r, n_q, n_kv)
            s.update(qkv=qkv, qkv_r=qkv_r, o=o, lse=lse)
            h = mm_row_fwd("a_out_fwd", o, w_o, h)
        else:
            qkv = mm_col_fwd("b_qkv_fwd", hn, w_qkv, BF16)
            outs, lzs = [], []
            for g, (win, dil) in enumerate(B_GROUPS):
                o_g, lz_g = mixer_b_group_fwd(qkv, tables[g][0], strides[g], g, n_groups, dil)
                outs.append(o_g)
                lzs.append(lz_g)
            y = combine_fwd("b_combine_fwd", outs, lzs)
            s.update(qkv=qkv, outs=outs, lzs=lzs, y=y)
            h = mm_row_fwd("b_out_fwd", y, w_o, h)
        s["h_mid"] = h
        hn2 = rms_fwd("ffn_norm_fwd", h, ffn_norm[i])
        u2 = mm_col_fwd("ffn_up_fwd", hn2, w_up_i, F32, split=2)
        act = conv_act_fwd("ffn_conv_act_fwd", u2, cw, cb_full[i])
        s.update(hn2=hn2, u2=u2, act=act)
        h = mm_row_fwd("ffn_down_fwd", act, w_down_i, h)
        saved.append(s)

    dh, dh_b, d_final, loss_part = loss_head("loss_head", h, final_norm, target)

    d_mix, d_ffn, d_cw, d_cb = [None] * depth, [None] * depth, [None] * depth, [None] * depth
    d_qg, d_kg = [None] * n_a, [None] * n_a
    d_rel = jnp.zeros((n_groups * hg, LANES), F32)
    layer_grads = [{} for _ in range(depth)]
    pending = []

    def settle():
        done = []
        while pending:
            i_p, part_p, finish = pending.pop()
            layer_grads[i_p][part_p] = finish()
            done += [upd[0] for upd in layer_grads[i_p][part_p]]
        return done

    early = []

    def register(i_p, part_p, grads):
        if on_grads is None:
            layer_grads[i_p][part_p] = grads
        else:
            first, finish = on_grads(i_p, part_p, grads)
            early.extend(first)
            pending.append((i_p, part_p, finish))

    def take_early():
        first = tuple(early)
        early.clear()
        return first

    for i in reversed(range(depth)):
        j = i // 2
        w_qkv, w_o, w_up_i, w_down_i, cw = layers[i]
        s = saved[i]
        dact = mm_row_dx("ffn_down_dx", dh_b, w_down_i)
        g_down = mm_row_dw("ffn_down_dw", s["act"], dh_b)
        du2, dcw = conv_act_bwd("ffn_conv_act_bwd", s["u2"], cw, cb_full[i], dact, take_early())
        d_cw[i] = dcw[:, 0:3, :].transpose(1, 0, 2).reshape(3, 2 * dff)
        d_cb[i] = dcw[:, 3, :].reshape(2 * dff)
        g_up = mm_col_dw("ffn_up_dw", s["hn2"], du2, w_u, split=2)
        dhn2 = mm_col_dx("ffn_up_dx", du2, w_up_i, split=2)
        dh, dh_b, d_ffn[i] = rms_bwd("ffn_norm_bwd", s["h_mid"], ffn_norm[i], dhn2, dh, settle())
        register(i, "ffn", [g_up, g_down.reshape(N_DEV, -1, d)])
        if i % 2 == 0:
            do = mm_row_dx("a_out_dx", dh_b, w_o)
            g_o = mm_row_dw("a_out_dw", s["o"], dh_b)
            dlt, do_b = row_delta("a_delta", do, s["o"], n_q, take_early())
            dq, dk, dv = mixer_a_bwd(s["qkv_r"], do_b, s["lse"], dlt, n_q, n_kv)
            dqkv, dgain = qk_prep_bwd("a_qk_prep_bwd", s["qkv"], dq, dk, dv, a_q_gain[j], a_k_gain[j], cos, sin,
                                      n_q, n_kv)
            d_qg[j], d_kg[j] = dgain[0], dgain[1]
            g_qkv = mm_col_dw("a_qkv_dw", s["hn"], dqkv, w_a)
            dhn = mm_col_dx("a_qkv_dx", dqkv, w_qkv)
        else:
            dy = mm_row_dx("b_out_dx", dh_b, w_o)
            g_o = mm_row_dw("b_out_dw", s["y"], dh_b)
            res = combine_bwd("b_combine_bwd", dy, s["outs"], s["lzs"], take_early())
            dos, dlts = res[:n_groups], res[n_groups:]
            pieces, rel_rows = [], []
            for g, (win, dil) in enumerate(B_GROUPS):
                dq, dk, dv, dbias = mixer_b_group_bwd(s["qkv"], tables[g][0], dos[g], s["lzs"][g], dlts[g],
                                                      strides[g], g, n_groups, dil)
                pieces += [dq, dk, dv]
                rel_rows.append(bias_bucket_sums(f"b_bias_sums_d{dil}", dbias, tables[g][1]))
            d_rel = d_rel + jnp.concatenate(rel_rows, axis=0)
            dqkv = jnp.concatenate(pieces, axis=1)
            g_qkv = mm_col_dw("b_qkv_dw", s["hn"], dqkv, w_b)
            dhn = mm_col_dx("b_qkv_dx", dqkv, w_qkv)
        dh, dh_b, d_mix[i] = rms_bwd("mix_norm_bwd", s["h_in"], mix_norm[i], dhn, dh, settle())
        register(i, "mix", [g_qkv, g_o.reshape(N_DEV, -1, d)])
    last = pending.pop()[2] if pending else None

    d_rel_bias = d_rel[:, :REL_BUCKETS].T
    small_g = [jnp.stack(d_qg), jnp.stack(d_kg), d_rel_bias, jnp.concatenate(d_mix, 0), jnp.concatenate(d_ffn, 0),
               jnp.stack(d_cb), d_final.reshape(-1), jnp.stack(d_cw), loss_part]
    return dh, layer_grads, small_g, last


def kernel(x, a_w_qkv, a_w_o, a_q_gain, a_k_gain, b_w_qkv, b_w_o, rel_bias, mix_norm, ffn_norm, w_up, conv_w, conv_b, w_down, final_norm, loss_target, m_a_w_qkv, m_a_w_o, m_a_q_gain, m_a_k_gain, m_b_w_qkv, m_b_w_o, m_rel_bias, m_mix_norm, m_ffn_norm, m_w_up, m_conv_w, m_conv_b, m_w_down, m_final_norm, v_a_w_qkv, v_a_w_o, v_a_q_gain, v_a_k_gain, v_b_w_qkv, v_b_w_o, v_rel_bias, v_mix_norm, v_ffn_norm, v_w_up, v_conv_w, v_conv_b, v_w_down, v_final_norm):
    d = x.shape[2]
    depth = mix_norm.shape[0]
    dff = w_down.shape[1] * N_DEV
    w_u = w_up.shape[2]
    mixers = [(a_w_qkv, a_w_o, m_a_w_qkv, m_a_w_o, v_a_w_qkv, v_a_w_o),
              (b_w_qkv, b_w_o, m_b_w_qkv, m_b_w_o, v_b_w_qkv, v_b_w_o)]

    layers = []
    for i in range(depth):
        w_qkv, w_o = mixers[i % 2][0][i // 2], mixers[i % 2][1][i // 2]
        shards = [w_qkv.astype(BF16), w_o.astype(BF16), w_up[i].astype(BF16), w_down[i].astype(BF16), conv_w[i]]
        if i == 0:
            (g_qkv,) = gather_layer("gather_l0_qkv", shards[:1])
            g_o, g_up, g_down, g_cw = gather_layer("gather_l0", shards[1:])
        else:
            g_qkv, g_o, g_up, g_down, g_cw = gather_layer(f"gather_l{i}", shards)
        cw = g_cw.transpose(1, 0, 2).reshape(3, 2, dff).transpose(1, 0, 2)
        layers.append((g_qkv, g_o.reshape(-1, d), g_up, g_down.reshape(dff, d), cw))

    core = lax.axis_index("c").astype(jnp.int32).reshape(1)
    chip = (2 * lax.axis_index("x") + lax.axis_index("y")).astype(jnp.int32).reshape(1)

    def reduce_and_update(i, part, grads):
        w_qkv, w_o, m_qkv, m_o, v_qkv, v_o = mixers[i % 2]
        prefix = ("a_w_", "b_w_")[i % 2]
        state = {"mix": [(prefix + "qkv", w_qkv, m_qkv, v_qkv, i // 2), (prefix + "o", w_o, m_o, v_o, i // 2)],
                 "ffn": [("w_up", w_up, m_w_up, v_w_up, i), ("w_down", w_down, m_w_down, v_w_down, i)]}[part]
        got1 = grads_to_sibling(f"to_sibling_l{i}_{part}", grads)
        sums = [chip_sum(f"chip_sum_l{i}_{part}{a}", grads[a], got1[a], core) for a in range(2)]
        got2 = grads_to_chips(f"to_chips_l{i}_{part}", sums)

        def finish():
            for a, (key, w, m, v, layer) in enumerate(state):
                outs = big_out.get(key) or [lax.empty(w.shape, F32) for _ in range(4)]
                big_out[key] = adamw_layer(f"adamw_l{i}_{part}{a}", sums[a], got2[a], w, m, v, outs, layer, chip)
            return [big_out[key] for key, *_ in state]

        return sums, finish

    big_out = {}
    dh, _, small_g, last = _local_step(x[0], loss_target[0], layers,
                                       (a_q_gain, a_k_gain, rel_bias, mix_norm, ffn_norm, conv_b, final_norm),
                                       reduce_and_update)
    grad_x = dh[None]

    width = 2048
    packed = _pack(small_g, N_DEV * width).reshape(-1, N_DEV, width)
    n_rows = packed.shape[0]
    packed = packed.transpose(1, 0, 2).reshape(N_DEV, n_rows * width)
    red = all_reduce_small("small_all_reduce", packed)
    last()
    red = red.reshape(N_DEV, n_rows, width).transpose(1, 0, 2)
    (g_qg, g_kg, g_rel, g_mix, g_ffn, g_cb, g_fin, g_cw_all, loss) = _unpack(red, [p.shape for p in small_g])
    idx = 4 * lax.axis_index("x") + 2 * lax.axis_index("y") + lax.axis_index("c")
    g_cw_mine = lax.dynamic_slice_in_dim(g_cw_all, idx * w_u, w_u, axis=2)

    small_w = [a_q_gain, a_k_gain, rel_bias, mix_norm, ffn_norm, conv_b, final_norm, conv_w]
    small_m = [m_a_q_gain, m_a_k_gain, m_rel_bias, m_mix_norm, m_ffn_norm, m_conv_b, m_final_norm, m_conv_w]
    small_v = [v_a_q_gain, v_a_k_gain, v_rel_bias, v_mix_norm, v_ffn_norm, v_conv_b, v_final_norm, v_conv_w]
    small_grads = [g_qg, g_kg, g_rel, g_mix, g_ffn, g_cb, g_fin, g_cw_mine]
    shapes = [w.shape for w in small_w]
    pad_rows = (-_pack(small_w, width).shape[0]) % 8

    def pk8(parts):
        p = _pack(parts, width)
        return jnp.pad(p, ((0, pad_rows), (0, 0))) if pad_rows else p

    sd, sm, sv = adamw_small("adamw_small", pk8(small_grads), pk8(small_w), pk8(small_m), pk8(small_v))
    sd, sm, sv = _unpack(sd, shapes), _unpack(sm, shapes), _unpack(sv, shapes)

    names = ["a_w_qkv", "a_w_o", "a_q_gain", "a_k_gain", "b_w_qkv", "b_w_o", "rel_bias", "mix_norm", "ffn_norm",
             "w_up", "conv_w", "conv_b", "w_down", "final_norm"]
    small_names = ["a_q_gain", "a_k_gain", "rel_bias", "mix_norm", "ffn_norm", "conv_b", "final_norm", "conv_w"]
    grads, deltas, new_m, new_v = {}, {}, {}, {}
    for nm, outs in big_out.items():
        grads[nm], deltas[nm], new_m[nm], new_v[nm] = outs
    for a, nm in enumerate(small_names):
        grads[nm] = small_grads[a].reshape(shapes[a])
        deltas[nm], new_m[nm], new_v[nm] = sd[a], sm[a], sv[a]
    return (loss.reshape(()), grad_x, *[grads[n] for n in names], *[deltas[n] for n in names],
            *[new_m[n] for n in names], *[new_v[n] for n in names])
```

```python
import functools
import math

import jax
import jax.numpy as jnp
from jax import lax
from jax.experimental import pallas as pl
from jax.experimental.pallas import tpu as pltpu
from jax.experimental.pallas import tpu_sc as plsc

F32 = jnp.float32
BF16 = jnp.bfloat16
MESH = pl.DeviceIdType.MESH

N_DEV = 8
LANES = 128
HEAD_DIM = 128
VMEM_LIMIT = 56 * 1024 * 1024
GRID_W = 64
ROPE_THETA = 10000.0
A_KV_HEADS = 4
B_GROUPS = ((128, 1), (512, 4), (2048, 16))
B_HEADS_PER_GROUP = 8
REL_BUCKETS = 32
REL_MAX_DISTANCE = 1024
EPS = 1e-6
NEG_INF = -1e30
ADAM_LR = 0.001
ADAM_B1 = 0.9
ADAM_B2 = 0.999
ADAM_EPS = 1e-08
ADAM_WD = 0.01
ADAM_STEP = 10

ROW_TILE = 256
MM_TM = 1024
MM_TK = 2048
A_BQ = 1024
A_BK = 2048
B_BQ = 256
ATTN_ROWS = 16
ATTN_SCALE = HEAD_DIM ** -0.5

NN = (((1,), (0,)), ((), ()))
NT = (((1,), (1,)), ((), ()))
TN = (((0,), (0,)), ((), ()))


def _tile(n, pref):
    return pref if n % pref == 0 else n


def _div_tile(n, pref):
    for cand in range(pref - pref % LANES, 0, -LANES):
        if n % cand == 0:
            return cand
    return n


def _params(sem):
    return pltpu.CompilerParams(dimension_semantics=sem, vmem_limit_bytes=VMEM_LIMIT)


def _dot(a, b, dims):
    return lax.dot_general(a, b, dims, preferred_element_type=F32)


def _mm(name, a, b, *, grid, a_blk, a_map, b_blk, b_map, o_blk, o_map, out_shape, out_dtype, dims,
        res=None, after=()):
    nk = grid[2]
    acc_shape = tuple(d for d in o_blk if d is not None)

    def body(*refs):
        a_ref, b_ref = refs[:2]
        r_ref = None if res is None else refs[2]
        if nk == 1:
            o_ref = refs[-1]
            part = _dot(a_ref[...].astype(BF16), b_ref[...].astype(BF16), dims)
            o_ref[...] = (part if r_ref is None else part + r_ref[...]).astype(out_dtype)
            return
        o_ref, acc = refs[-2:]
        k = pl.program_id(2)

        @pl.when(k == 0)
        def _():
            acc[...] = jnp.zeros_like(acc)

        acc[...] += _dot(a_ref[...].astype(BF16), b_ref[...].astype(BF16), dims)

        @pl.when(k == nk - 1)
        def _():
            r = acc[...]
            if r_ref is not None:
                r = r + r_ref[...]
            o_ref[...] = r.astype(out_dtype)

    in_specs = [pl.BlockSpec(a_blk, a_map), pl.BlockSpec(b_blk, b_map)]
    args = [a, b]
    if res is not None:
        in_specs.append(pl.BlockSpec(o_blk, o_map))
        args.append(res)
    in_specs += [pl.BlockSpec(memory_space=pl.ANY)] * len(after)
    args += list(after)
    return pl.pallas_call(
        body, name=name, grid=grid, in_specs=in_specs, out_specs=pl.BlockSpec(o_blk, o_map),
        out_shape=jax.ShapeDtypeStruct(out_shape, out_dtype),
        scratch_shapes=[] if nk == 1 else [pltpu.VMEM(acc_shape, F32)],
        compiler_params=_params(("parallel", "parallel", "arbitrary")),
    )(*args)


def mm_col_fwd(name, a, wg, out_dtype, split=1):
    m, kdim = a.shape
    n_dev, _, w = wg.shape
    tm, tk = _tile(m, MM_TM), _div_tile(kdim, MM_TK)
    per = n_dev // split
    if split == 1:
        o_blk, o_map, o_shape = (tm, w), (lambda i, j, k: (i, j)), (m, n_dev * w)
    else:
        o_blk, o_map, o_shape = (None, tm, w), (lambda i, j, k: (j // per, i, j % per)), (split, m, per * w)
    return _mm(name, a, wg, grid=(m // tm, n_dev, kdim // tk),
               a_blk=(tm, tk), a_map=lambda i, j, k: (i, k),
               b_blk=(None, tk, w), b_map=lambda i, j, k: (j, k, 0),
               o_blk=o_blk, o_map=o_map, out_shape=o_shape, out_dtype=out_dtype, dims=NN)


def mm_col_dx(name, dy, wg, split=1):
    n_dev, kdim, w = wg.shape
    m = dy.shape[-2]
    tm, tk = _tile(m, MM_TM), _div_tile(kdim, MM_TK)
    per = n_dev // split
    if split == 1:
        a_blk, a_map = (tm, w), (lambda i, j, k: (i, k))
    else:
        a_blk, a_map = (None, tm, w), (lambda i, j, k: (k // per, i, k % per))
    return _mm(name, dy, wg, grid=(m // tm, kdim // tk, n_dev),
               a_blk=a_blk, a_map=a_map,
               b_blk=(None, tk, w), b_map=lambda i, j, k: (k, j, 0),
               o_blk=(tm, tk), o_map=lambda i, j, k: (i, j), out_shape=(m, kdim), out_dtype=F32, dims=NT)


def mm_col_dw(name, x, dy, w, split=1):
    m, kdim = x.shape
    tm, tk = _tile(m, MM_TM), _div_tile(kdim, MM_TK)
    per = N_DEV // split
    if split == 1:
        b_blk, b_map = (tm, w), (lambda i, j, k: (k, j))
    else:
        b_blk, b_map = (None, tm, w), (lambda i, j, k: (j // per, k, j % per))
    return _mm(name, x, dy, grid=(kdim // tk, N_DEV, m // tm),
               a_blk=(tm, tk), a_map=lambda i, j, k: (k, i),
               b_blk=b_blk, b_map=b_map,
               o_blk=(None, tk, w), o_map=lambda i, j, k: (j, i, 0),
               out_shape=(N_DEV, kdim, w), out_dtype=BF16, dims=TN)


def mm_row_fwd(name, a, wg, res):
    m, kdim = a.shape
    n = wg.shape[1]
    tm, tk, tn = _tile(m, MM_TM), _div_tile(kdim, MM_TK), _tile(n, 1024)
    return _mm(name, a, wg, grid=(m // tm, n // tn, kdim // tk),
               a_blk=(tm, tk), a_map=lambda i, j, k: (i, k),
               b_blk=(tk, tn), b_map=lambda i, j, k: (k, j),
               o_blk=(tm, tn), o_map=lambda i, j, k: (i, j), out_shape=(m, n), out_dtype=F32, dims=NN,
               res=res)


def mm_row_dx(name, dy, wg, after=()):
    m, n = dy.shape
    kdim = wg.shape[0]
    tm, tk, tn = _tile(m, MM_TM), _div_tile(kdim, MM_TK), _tile(n, MM_TK)
    return _mm(name, dy, wg, grid=(m // tm, kdim // tk, n // tn),
               a_blk=(tm, tn), a_map=lambda i, j, k: (i, k),
               b_blk=(tk, tn), b_map=lambda i, j, k: (j, k),
               o_blk=(tm, tk), o_map=lambda i, j, k: (i, j), out_shape=(m, kdim), out_dtype=F32, dims=NT,
               after=after)


def mm_row_dw(name, x, dy):
    m, kdim = x.shape
    n = dy.shape[1]
    tm, tk, tn = _tile(m, MM_TM), _div_tile(kdim, MM_TK), _tile(n, 1024)
    return _mm(name, x, dy, grid=(kdim // tk, n // tn, m // tm),
               a_blk=(tm, tk), a_map=lambda i, j, k: (k, i),
               b_blk=(tm, tn), b_map=lambda i, j, k: (k, j),
               o_blk=(tk, tn), o_map=lambda i, j, k: (i, j), out_shape=(kdim, n), out_dtype=BF16, dims=TN)


def _rows(d, tm):
    return pl.BlockSpec((tm, d), lambda i: (i, 0))


def _vec(d):
    return pl.BlockSpec((1, d), lambda i: (0, 0))


def rms_fwd(name, h, gain):
    t, d = h.shape
    tm = _tile(t, ROW_TILE)

    def body(h_ref, g_ref, o_ref):
        x = h_ref[...]
        rstd = lax.rsqrt(jnp.mean(x * x, axis=-1, keepdims=True) + EPS)
        o_ref[...] = (x * rstd * g_ref[...]).astype(BF16)

    return pl.pallas_call(
        body, name=name, grid=(t // tm,), in_specs=[_rows(d, tm), _vec(d)], out_specs=_rows(d, tm),
        out_shape=jax.ShapeDtypeStruct((t, d), BF16), compiler_params=_params(("parallel",)),
    )(h, gain.reshape(1, d))


def rms_bwd(name, h, gain, dy, dres, after=()):
    t, d = h.shape
    tm = _tile(t, ROW_TILE)

    def body(h_ref, g_ref, dy_ref, r_ref, *rest):
        dh_ref, dhb_ref, dg_ref = rest[-3:]

        @pl.when(pl.program_id(0) == 0)
        def _():
            dg_ref[...] = jnp.zeros_like(dg_ref)

        x = h_ref[...]
        rstd = lax.rsqrt(jnp.mean(x * x, axis=-1, keepdims=True) + EPS)
        xhat = x * rstd
        dyv = dy_ref[...]
        dxhat = dyv * g_ref[...]
        dh = r_ref[...] + rstd * (dxhat - xhat * jnp.mean(dxhat * xhat, axis=-1, keepdims=True))
        dh_ref[...] = dh
        dhb_ref[...] = dh.astype(BF16)
        dg_ref[...] += jnp.sum(dyv * xhat, axis=0, keepdims=True)

    return pl.pallas_call(
        body, name=name, grid=(t // tm,),
        in_specs=[_rows(d, tm), _vec(d), _rows(d, tm), _rows(d, tm)]
        + [pl.BlockSpec(memory_space=pl.ANY)] * len(after),
        out_specs=[_rows(d, tm), _rows(d, tm), _vec(d)],
        out_shape=[jax.ShapeDtypeStruct((t, d), F32), jax.ShapeDtypeStruct((t, d), BF16),
                   jax.ShapeDtypeStruct((1, d), F32)],
        compiler_params=_params(("arbitrary",)),
    )(h, gain.reshape(1, d), dy, dres, *after)


def loss_head(name, h, gain, target):
    t, d = h.shape
    tm = _tile(t, ROW_TILE)

    def body(h_ref, g_ref, t_ref, dh_ref, dhb_ref, dg_ref, loss_ref):
        @pl.when(pl.program_id(0) == 0)
        def _():
            dg_ref[...] = jnp.zeros_like(dg_ref)
            loss_ref[...] = jnp.zeros_like(loss_ref)

        x = h_ref[...]
        rstd = lax.rsqrt(jnp.mean(x * x, axis=-1, keepdims=True) + EPS)
        xhat = x * rstd
        err = xhat * g_ref[...] - t_ref[...]
        row = jnp.mean(err * err, axis=-1, keepdims=True)
        loss_ref[...] += 0.5 * jnp.sum(row, axis=0, keepdims=True)
        dyv = err * (1.0 / d)
        dxhat = dyv * g_ref[...]
        dh = rstd * (dxhat - xhat * jnp.mean(dxhat * xhat, axis=-1, keepdims=True))
        dh_ref[...] = dh
        dhb_ref[...] = dh.astype(BF16)
        dg_ref[...] += jnp.sum(dyv * xhat, axis=0, keepdims=True)

    return pl.pallas_call(
        body, name=name, grid=(t // tm,),
        in_specs=[_rows(d, tm), _vec(d), _rows(d, tm)],
        out_specs=[_rows(d, tm), _rows(d, tm), _vec(d), pl.BlockSpec((1, 1), lambda i: (0, 0))],
        out_shape=[jax.ShapeDtypeStruct((t, d), F32), jax.ShapeDtypeStruct((t, d), BF16),
                   jax.ShapeDtypeStruct((1, d), F32), jax.ShapeDtypeStruct((1, 1), F32)],
        compiler_params=_params(("arbitrary",)),
    )(h, gain.reshape(1, d), target)


def rope_tables(seq):
    pos = jnp.arange(seq, dtype=jnp.int32)
    row_ids = (pos // GRID_W).astype(F32)
    col_ids = (pos % GRID_W).astype(F32)
    quarter = HEAD_DIM // 4
    inv_freq = ROPE_THETA ** (-jnp.arange(quarter, dtype=F32) / quarter)
    ar = row_ids[:, None] * inv_freq[None, :]
    ac = col_ids[:, None] * inv_freq[None, :]
    cos = jnp.concatenate([jnp.cos(ar), jnp.cos(ar), jnp.cos(ac), jnp.cos(ac)], axis=-1)
    sin = jnp.concatenate([-jnp.sin(ar), jnp.sin(ar), -jnp.sin(ac), jnp.sin(ac)], axis=-1)
    return cos, sin


def _swap_quarters(x):
    lane = lax.broadcasted_iota(jnp.int32, x.shape, 1)
    q = HEAD_DIM // 4
    return jnp.where((lane % (2 * q)) < q, pltpu.roll(x, HEAD_DIM - q, 1), pltpu.roll(x, q, 1))


def qk_prep_fwd(name, qkv, q_gain, k_gain, cos, sin, n_q, n_kv):
    t, width = qkv.shape
    tm = _tile(t, ROW_TILE)

    def body(x_ref, qg_ref, kg_ref, c_ref, s_ref, o_ref):
        c, s = c_ref[...], s_ref[...]
        for hd in range(n_q + n_kv):
            sl = slice(hd * HEAD_DIM, (hd + 1) * HEAD_DIM)
            x = x_ref[:, sl]
            g = qg_ref[...] if hd < n_q else kg_ref[...]
            xn = x * lax.rsqrt(jnp.mean(x * x, axis=-1, keepdims=True) + EPS) * g
            o_ref[:, sl] = (xn * c + _swap_quarters(xn) * s).astype(BF16)
        vs = slice((n_q + n_kv) * HEAD_DIM, width)
        o_ref[:, vs] = x_ref[:, vs].astype(BF16)

    return pl.pallas_call(
        body, name=name, grid=(t // tm,),
        in_specs=[_rows(width, tm), _vec(HEAD_DIM), _vec(HEAD_DIM), _rows(HEAD_DIM, tm), _rows(HEAD_DIM, tm)],
        out_specs=_rows(width, tm), out_shape=jax.ShapeDtypeStruct((t, width), BF16),
        compiler_params=_params(("parallel",)),
    )(qkv, q_gain.reshape(1, HEAD_DIM), k_gain.reshape(1, HEAD_DIM), cos, sin)


def qk_prep_bwd(name, qkv, dq, dk, dv, q_gain, k_gain, cos, sin, n_q, n_kv):
    t, width = qkv.shape
    tm = _tile(t, ROW_TILE)

    def body(x_ref, dq_ref, dk_ref, dv_ref, qg_ref, kg_ref, c_ref, s_ref, o_ref, dg_ref):
        @pl.when(pl.program_id(0) == 0)
        def _():
            dg_ref[...] = jnp.zeros_like(dg_ref)

        c, s = c_ref[...], s_ref[...]
        dgq = jnp.zeros((1, HEAD_DIM), F32)
        dgk = jnp.zeros((1, HEAD_DIM), F32)
        for hd in range(n_q + n_kv):
            sl = slice(hd * HEAD_DIM, (hd + 1) * HEAD_DIM)
            x = x_ref[:, sl]
            if hd < n_q:
                g, dout = qg_ref[...], dq_ref[:, sl]
            else:
                ks = slice((hd - n_q) * HEAD_DIM, (hd - n_q + 1) * HEAD_DIM)
                g, dout = kg_ref[...], dk_ref[:, ks]
            rstd = lax.rsqrt(jnp.mean(x * x, axis=-1, keepdims=True) + EPS)
            xhat = x * rstd
            dxn = dout * c + _swap_quarters(dout * s)
            part = jnp.sum(dxn * xhat, axis=0, keepdims=True)
            if hd < n_q:
                dgq = dgq + part
            else:
                dgk = dgk + part
            dxhat = dxn * g
            o_ref[:, sl] = (rstd * (dxhat - xhat * jnp.mean(dxhat * xhat, axis=-1, keepdims=True))).astype(BF16)
        o_ref[:, slice((n_q + n_kv) * HEAD_DIM, width)] = dv_ref[...].astype(BF16)
        dg_ref[0:1, :] += dgq
        dg_ref[1:2, :] += dgk

    kvw = n_kv * HEAD_DIM
    return pl.pallas_call(
        body, name=name, grid=(t // tm,),
        in_specs=[_rows(width, tm), _rows(n_q * HEAD_DIM, tm), _rows(kvw, tm), _rows(kvw, tm),
                  _vec(HEAD_DIM), _vec(HEAD_DIM), _rows(HEAD_DIM, tm), _rows(HEAD_DIM, tm)],
        out_specs=[_rows(width, tm), pl.BlockSpec((2, HEAD_DIM), lambda i: (0, 0))],
        out_shape=[jax.ShapeDtypeStruct((t, width), BF16), jax.ShapeDtypeStruct((2, HEAD_DIM), F32)],
        compiler_params=_params(("arbitrary",)),
    )(qkv, dq, dk, dv, q_gain.reshape(1, HEAD_DIM), k_gain.reshape(1, HEAD_DIM), cos, sin)


def _lanes(x, width):
    return jnp.tile(x, (1, width // LANES))


def _hs(hd):
    return slice(hd * HEAD_DIM, (hd + 1) * HEAD_DIM)


def attn_fwd(name, q, k, v, bias, *, grid, q_spec, k_spec, v_spec, b_spec, o_spec, valid, nh, shared_kv,
             bq, bk, o_shape, o_dtype):
    ns = grid[2]

    def body(*refs):
        if bias is None:
            q_ref, k_ref, v_ref, o_ref, lse_ref, m_s, l_s, acc_s = refs
            b_ref = None
        else:
            q_ref, k_ref, v_ref, b_ref, o_ref, lse_ref, m_s, l_s, acc_s = refs
        step = pl.program_id(2)

        @pl.when(step == 0)
        def _():
            m_s[...] = jnp.full_like(m_s, -jnp.inf)
            l_s[...] = jnp.zeros_like(l_s)
            acc_s[...] = jnp.zeros_like(acc_s)

        @pl.when(valid(pl.program_id(1), step))
        def _():
            for hd in range(nh):
                kh = _hs(0 if shared_kv else hd)
                s = _dot(q_ref[:, _hs(hd)], k_ref[:, kh], NT)
                p_rows, a_rows = [], []
                for r0 in range(0, bq, ATTN_ROWS):
                    rows = slice(r0, r0 + ATTN_ROWS)
                    z = s[rows] * ATTN_SCALE
                    if b_ref is not None:
                        z = z + b_ref[hd, rows, :]
                    m_prev = m_s[hd, rows, :]
                    m_new = jnp.maximum(m_prev, jnp.max(z, axis=-1, keepdims=True))
                    alpha = jnp.exp(m_prev - m_new)
                    p = jnp.exp(z - _lanes(m_new, bk))
                    l_s[hd, rows, :] = alpha * l_s[hd, rows, :] + jnp.sum(p, axis=-1, keepdims=True)
                    m_s[hd, rows, :] = m_new
                    p_rows.append(p.astype(BF16))
                    a_rows.append(alpha)
                pv = _dot(jnp.concatenate(p_rows, axis=0), v_ref[:, kh], NN)
                acc_s[hd] = jnp.concatenate(a_rows, axis=0) * acc_s[hd] + pv

        @pl.when(step == ns - 1)
        def _():
            for hd in range(nh):
                o_ref[:, _hs(hd)] = (acc_s[hd] / l_s[hd]).astype(o_dtype)
                lse_ref[:, _hs(hd)] = m_s[hd] + jnp.log(l_s[hd])

    in_specs = [q_spec, k_spec, v_spec] + ([] if bias is None else [b_spec])
    args = [q, k, v] + ([] if bias is None else [bias])
    stat = pltpu.VMEM((nh, bq, LANES), F32)
    return pl.pallas_call(
        body, name=name, grid=grid, in_specs=in_specs, out_specs=[o_spec, o_spec],
        out_shape=[jax.ShapeDtypeStruct(o_shape, o_dtype), jax.ShapeDtypeStruct(o_shape, F32)],
        scratch_shapes=[stat, stat, stat],
        compiler_params=_params(("parallel", "parallel", "arbitrary")),
    )(*args)


def _probs(q_ref, k_ref, v_ref, do_ref, lse_ref, dlt_ref, b_ref, hd, kh, bq, bk, want_p=True, on_ds=None):
    s = _dot(q_ref[:, _hs(hd)], k_ref[:, kh], NT)
    dp = _dot(do_ref[:, _hs(hd)], v_ref[:, kh], NT)
    p_rows, ds_rows = [], []
    for r0 in range(0, bq, ATTN_ROWS):
        rows = slice(r0, r0 + ATTN_ROWS)
        z = s[rows] * ATTN_SCALE
        if b_ref is not None:
            z = z + b_ref[hd, rows, :]
        p = jnp.exp(z - _lanes(lse_ref[rows, _hs(hd)], bk))
        ds = p * (dp[rows] - _lanes(dlt_ref[rows, _hs(hd)], bk))
        if on_ds is not None:
            on_ds(rows, ds)
        if want_p:
            p_rows.append(p.astype(BF16))
        ds_rows.append(ds.astype(BF16))
    return (jnp.concatenate(p_rows, axis=0) if want_p else None), jnp.concatenate(ds_rows, axis=0)


def attn_bwd_dq(name, q, k, v, do, lse, dlt, *, grid, q_spec, k_spec, v_spec, nh, bq, bk, o_shape):
    ns = grid[2]
    scale = HEAD_DIM ** -0.5

    def body(q_ref, k_ref, v_ref, do_ref, lse_ref, dlt_ref, dq_ref, acc_s):
        step = pl.program_id(2)

        @pl.when(step == 0)
        def _():
            acc_s[...] = jnp.zeros_like(acc_s)

        for hd in range(nh):
            _, ds = _probs(q_ref, k_ref, v_ref, do_ref, lse_ref, dlt_ref, None, hd, _hs(0), bq, bk, want_p=False)
            acc_s[hd] += _dot(ds, k_ref[:, _hs(0)], NN)

        @pl.when(step == ns - 1)
        def _():
            for hd in range(nh):
                dq_ref[:, _hs(hd)] = acc_s[hd] * scale

    return pl.pallas_call(
        body, name=name, grid=grid, in_specs=[q_spec, k_spec, v_spec, q_spec, q_spec, q_spec],
        out_specs=q_spec, out_shape=jax.ShapeDtypeStruct(o_shape, F32),
        scratch_shapes=[pltpu.VMEM((nh, bq, LANES), F32)],
        compiler_params=_params(("parallel", "parallel", "arbitrary")),
    )(q, k, v, do, lse, dlt)


def _always(i, s):
    return s >= 0


def row_delta(name, do, o, n_heads):
    t, width = do.shape
    tm = _tile(t, ROW_TILE)

    def body(do_ref, o_ref, dl_ref, dob_ref):
        for hd in range(n_heads):
            d = do_ref[:, _hs(hd)]
            s = jnp.sum(d * o_ref[:, _hs(hd)].astype(F32), axis=-1, keepdims=True)
            dl_ref[:, _hs(hd)] = jnp.broadcast_to(s, (tm, HEAD_DIM))
            dob_ref[:, _hs(hd)] = d.astype(BF16)

    return pl.pallas_call(
        body, name=name, grid=(t // tm,), in_specs=[_rows(width, tm), _rows(width, tm)],
        out_specs=[_rows(width, tm), _rows(width, tm)],
        out_shape=[jax.ShapeDtypeStruct((t, width), F32), jax.ShapeDtypeStruct((t, width), BF16)],
        compiler_params=_params(("parallel",)),
    )(do, o)


def _a_specs(n_q, n_kv, bq, bk, q_major):
    grp = n_q // n_kv
    if q_major:
        qm, km = (lambda b, i, s: (i, b)), (lambda b, i, s: (s, n_q + b))
        vm = lambda b, i, s: (s, n_q + n_kv + b)
    else:
        qm, km = (lambda b, i, s: (s, b)), (lambda b, i, s: (i, n_q + b))
        vm = lambda b, i, s: (i, n_q + n_kv + b)
    return (pl.BlockSpec((bq, grp * HEAD_DIM), qm), pl.BlockSpec((bk, HEAD_DIM), km),
            pl.BlockSpec((bk, HEAD_DIM), vm))


def mixer_a_fwd(qkv_r, n_q, n_kv):
    t = qkv_r.shape[0]
    bq, bk = _tile(t, A_BQ), _tile(t, A_BK)
    q_spec, k_spec, v_spec = _a_specs(n_q, n_kv, bq, bk, True)
    return attn_fwd("a_attn_fwd", qkv_r, qkv_r, qkv_r, None, grid=(n_kv, t // bq, t // bk),
                    q_spec=q_spec, k_spec=k_spec, v_spec=v_spec, b_spec=None, o_spec=q_spec, valid=_always,
                    nh=n_q // n_kv, shared_kv=True, bq=bq, bk=bk, o_shape=(t, n_q * HEAD_DIM), o_dtype=BF16)


def mixer_a_bwd(qkv_r, do_b, lse, dlt, n_q, n_kv):
    t = qkv_r.shape[0]
    bq, bk = _tile(t, A_BQ), _tile(t, A_BK)
    grp = n_q // n_kv
    q_spec, k_spec, v_spec = _a_specs(n_q, n_kv, bq, bk, True)
    dq = attn_bwd_dq("a_attn_dq", qkv_r, qkv_r, qkv_r, do_b, lse, dlt, grid=(n_kv, t // bq, t // bk),
                     q_spec=q_spec, k_spec=k_spec, v_spec=v_spec, nh=grp, bq=bq, bk=bk,
                     o_shape=(t, n_q * HEAD_DIM))
    q_spec, k_spec, v_spec = _a_specs(n_q, n_kv, bq, bk, False)
    o_spec = pl.BlockSpec((bk, HEAD_DIM), lambda b, i, s: (i, b))
    dk, dv = _attn_bwd_dkv_out(qkv_r, do_b, lse, dlt, grid=(n_kv, t // bk, t // bq), q_spec=q_spec,
                               k_spec=k_spec, v_spec=v_spec, o_spec=o_spec, grp=grp, bq=bq, bk=bk,
                               o_shape=(t, n_kv * HEAD_DIM))
    return dq, dk, dv


def _attn_bwd_dkv_out(qkv_r, do_b, lse, dlt, *, grid, q_spec, k_spec, v_spec, o_spec, grp, bq, bk, o_shape):
    ns = grid[2]
    scale = HEAD_DIM ** -0.5

    def body(q_ref, k_ref, v_ref, do_ref, lse_ref, dlt_ref, dk_ref, dv_ref, dk_s, dv_s):
        step = pl.program_id(2)

        @pl.when(step == 0)
        def _():
            dk_s[...] = jnp.zeros_like(dk_s)
            dv_s[...] = jnp.zeros_like(dv_s)

        for hd in range(grp):
            p, ds = _probs(q_ref, k_ref, v_ref, do_ref, lse_ref, dlt_ref, None, hd, _hs(0), bq, bk)
            dv_s[...] += _dot(p, do_ref[:, _hs(hd)], TN)
            dk_s[...] += _dot(ds, q_ref[:, _hs(hd)], TN)

        @pl.when(step == ns - 1)
        def _():
            dk_ref[...] = dk_s[...] * scale
            dv_ref[...] = dv_s[...]

    acc = pltpu.VMEM((bk, HEAD_DIM), F32)
    return pl.pallas_call(
        body, name="a_attn_dkv", grid=grid, in_specs=[q_spec, k_spec, v_spec, q_spec, q_spec, q_spec],
        out_specs=[o_spec, o_spec], out_shape=[jax.ShapeDtypeStruct(o_shape, F32)] * 2,
        scratch_shapes=[acc, acc], compiler_params=_params(("parallel", "parallel", "arbitrary")),
    )(qkv_r, qkv_r, qkv_r, do_b, lse, dlt)


def t5_bucket(rel):
    nb = REL_BUCKETS // 2
    max_exact = nb // 2
    base = jnp.where(rel > 0, nb, 0)
    n = jnp.abs(rel)
    nf = jnp.maximum(n, 1).astype(F32)
    large = max_exact + (jnp.log(nf / max_exact) / math.log(REL_MAX_DISTANCE / max_exact)
                         * (nb - max_exact)).astype(jnp.int32)
    large = jnp.minimum(large, nb - 1)
    return base + jnp.where(n < max_exact, n, large)


def band_stride(t, win, dil):
    return 1 if t % B_BQ == 0 and win // 2 <= B_BQ else dil


def band_tables(rel_bias_g, win, dil, stride, bq):
    a = jnp.arange(bq)[:, None]
    b = jnp.arange(bq)[None, :]
    rel = jnp.stack([(s - 1) * bq + b - a for s in range(3)]) * stride
    ok = (jnp.abs(rel) <= win // 2) & (rel % dil == 0)
    bucket = t5_bucket(rel)
    bias = jnp.zeros((rel_bias_g.shape[1],) + rel.shape, F32)
    for r in range(REL_BUCKETS):
        bias = bias + jnp.where(bucket[None] == r, rel_bias_g[r][:, None, None, None], 0.0)
    return jnp.where(ok[None], bias, NEG_INF), jnp.where(ok, bucket, -1).astype(jnp.int32)


def band_block(t, stride):
    return _tile(t // stride, B_BQ)


def _b_geometry(t, dil, g, n_groups, bq):
    hg = B_HEADS_PER_GROUP
    length = t // dil
    nblk = length // bq
    gw = hg * HEAD_DIM
    per_tok = 3 * n_groups
    return hg, length, bq, nblk, gw, per_tok


def mixer_b_group_fwd(qkv, bias, dil, g, n_groups, tag):
    t = qkv.shape[0]
    hg, length, bq, nblk, gw, per_tok = _b_geometry(t, dil, g, n_groups, bias.shape[2])
    if dil > 1:
        qkv, g, per_tok = qkv[:, 3 * g * gw:3 * (g + 1) * gw], 0, 3
    view = qkv.reshape(length, dil * qkv.shape[1])
    col = lambda c, which: c * per_tok + 3 * g + which
    kblk = lambda i, s: jnp.clip(i - 1 + s, 0, nblk - 1)
    spec = lambda which, streamed: pl.BlockSpec(
        (bq, gw), (lambda c, i, s: (kblk(i, s), col(c, which))) if streamed else (lambda c, i, s: (i, col(c, which))))
    valid = lambda i, s: (i - 1 + s >= 0) & (i - 1 + s < nblk)
    o, lz = attn_fwd(f"b_attn_fwd_d{tag}", view, view, view, bias, grid=(dil, nblk, 3),
                     q_spec=spec(0, False), k_spec=spec(1, True), v_spec=spec(2, True),
                     b_spec=pl.BlockSpec((hg, None, bq, bq), lambda c, i, s: (0, s, 0, 0)),
                     o_spec=pl.BlockSpec((bq, gw), lambda c, i, s: (i, c)), valid=valid, nh=hg,
                     shared_kv=False, bq=bq, bk=bq, o_shape=(length, dil * gw), o_dtype=F32)
    return o.reshape(t, gw), lz.reshape(t, gw)


def mixer_b_group_bwd(qkv, bias, do_g, lz_g, dlt_g, dil, g, n_groups, tag):
    t = qkv.shape[0]
    hg, length, bq, nblk, gw, per_tok = _b_geometry(t, dil, g, n_groups, bias.shape[2])
    if dil > 1:
        qkv, g, per_tok = qkv[:, 3 * g * gw:3 * (g + 1) * gw], 0, 3
    view = qkv.reshape(length, dil * qkv.shape[1])
    dov, lzv, dlv = (x.reshape(length, dil * gw) for x in (do_g, lz_g, dlt_g))
    col = lambda c, which: c * per_tok + 3 * g + which
    nbr = lambda i, s: jnp.clip(i - 1 + s, 0, nblk - 1)
    valid = lambda i, s: (i - 1 + s >= 0) & (i - 1 + s < nblk)
    q_spec = pl.BlockSpec((bq, gw), lambda c, i, s: (i, col(c, 0)))
    k_spec = pl.BlockSpec((bq, gw), lambda c, i, s: (nbr(i, s), col(c, 1)))
    v_spec = pl.BlockSpec((bq, gw), lambda c, i, s: (nbr(i, s), col(c, 2)))
    stat = pl.BlockSpec((bq, gw), lambda c, i, s: (i, c))
    dq, dbias = _band_bwd_dq(f"b_attn_dq_d{tag}", view, dov, lzv, dlv, bias, grid=(dil, nblk, 3),
                             q_spec=q_spec, k_spec=k_spec, v_spec=v_spec, stat_spec=stat,
                             b_spec=pl.BlockSpec((hg, None, bq, bq), lambda c, i, s: (0, s, 0, 0)),
                             valid=valid, nh=hg, bq=bq, o_shape=(length, dil * gw))
    q_spec = pl.BlockSpec((bq, gw), lambda c, i, s: (nbr(i, s), col(c, 0)))
    k_spec = pl.BlockSpec((bq, gw), lambda c, i, s: (i, col(c, 1)))
    v_spec = pl.BlockSpec((bq, gw), lambda c, i, s: (i, col(c, 2)))
    stat = pl.BlockSpec((bq, gw), lambda c, i, s: (nbr(i, s), c))
    dk, dv = _band_bwd_dkv(f"b_attn_dkv_d{tag}", view, dov, lzv, dlv, bias, grid=(dil, nblk, 3),
                           q_spec=q_spec, k_spec=k_spec, v_spec=v_spec, stat_spec=stat,
                           b_spec=pl.BlockSpec((hg, None, bq, bq), lambda c, i, s: (0, 2 - s, 0, 0)),
                           o_spec=pl.BlockSpec((bq, gw), lambda c, i, s: (i, c)),
                           valid=valid, nh=hg, bq=bq, o_shape=(length, dil * gw))
    return dq.reshape(t, gw), dk.reshape(t, gw), dv.reshape(t, gw), dbias


def _band_bwd_dq(name, view, do, lse, dlt, bias, *, grid, q_spec, k_spec, v_spec, stat_spec, b_spec, valid,
                 nh, bq, o_shape):
    scale = HEAD_DIM ** -0.5
    bias_shape = (nh, 3, bq, bq)

    def body(q_ref, k_ref, v_ref, do_ref, lse_ref, dlt_ref, b_ref, dq_ref, db_ref, acc_s):
        step = pl.program_id(2)

        @pl.when((pl.program_id(0) == 0) & (pl.program_id(1) == 0) & (step == 0))
        def _():
            db_ref[...] = jnp.zeros_like(db_ref)

        @pl.when(step == 0)
        def _():
            acc_s[...] = jnp.zeros_like(acc_s)

        @pl.when(valid(pl.program_id(1), step))
        def _():
            for hd in range(nh):
                def add_bias_grad(rows, ds, hd=hd):
                    db_ref[hd, step, rows, :] += ds

                _, ds = _probs(q_ref, k_ref, v_ref, do_ref, lse_ref, dlt_ref, b_ref, hd, _hs(hd), bq, bq,
                               want_p=False, on_ds=add_bias_grad)
                acc_s[hd] += _dot(ds, k_ref[:, _hs(hd)], NN)

        @pl.when(step == 2)
        def _():
            for hd in range(nh):
                dq_ref[:, _hs(hd)] = (acc_s[hd] * scale).astype(BF16)

    return pl.pallas_call(
        body, name=name, grid=grid,
        in_specs=[q_spec, k_spec, v_spec, stat_spec, stat_spec, stat_spec, b_spec],
        out_specs=[stat_spec, pl.BlockSpec(bias_shape, lambda c, i, s: (0, 0, 0, 0))],
        out_shape=[jax.ShapeDtypeStruct(o_shape, BF16), jax.ShapeDtypeStruct(bias_shape, F32)],
        scratch_shapes=[pltpu.VMEM((nh, bq, LANES), F32)], compiler_params=_params(("arbitrary",) * 3),
    )(view, view, view, do, lse, dlt, bias)


def _band_bwd_dkv(name, view, do, lse, dlt, bias, *, grid, q_spec, k_spec, v_spec, stat_spec, b_spec, o_spec,
                  valid, nh, bq, o_shape):
    scale = HEAD_DIM ** -0.5

    def body(q_ref, k_ref, v_ref, do_ref, lse_ref, dlt_ref, b_ref, dk_ref, dv_ref, dk_s, dv_s):
        step = pl.program_id(2)

        @pl.when(step == 0)
        def _():
            dk_s[...] = jnp.zeros_like(dk_s)
            dv_s[...] = jnp.zeros_like(dv_s)

        @pl.when(valid(pl.program_id(1), step))
        def _():
            for hd in range(nh):
                p, ds = _probs(q_ref, k_ref, v_ref, do_ref, lse_ref, dlt_ref, b_ref, hd, _hs(hd), bq, bq)
                dv_s[hd] += _dot(p, do_ref[:, _hs(hd)], TN)
                dk_s[hd] += _dot(ds, q_ref[:, _hs(hd)], TN)

        @pl.when(step == 2)
        def _():
            for hd in range(nh):
                dk_ref[:, _hs(hd)] = (dk_s[hd] * scale).astype(BF16)
                dv_ref[:, _hs(hd)] = dv_s[hd].astype(BF16)

    acc = pltpu.VMEM((nh, bq, LANES), F32)
    return pl.pallas_call(
        body, name=name, grid=grid,
        in_specs=[q_spec, k_spec, v_spec, stat_spec, stat_spec, stat_spec, b_spec],
        out_specs=[o_spec, o_spec], out_shape=[jax.ShapeDtypeStruct(o_shape, BF16)] * 2,
        scratch_shapes=[acc, acc], compiler_params=_params(("parallel", "parallel", "arbitrary")),
    )(view, view, view, do, lse, dlt, bias)


def bias_bucket_sums(name, dbias, bucket):
    nh, _, bq, _ = dbias.shape
    db2 = dbias.reshape(nh, 3 * bq, bq)
    bk2 = bucket.reshape(3 * bq, bq)

    def body(db_ref, bk_ref, o_ref):
        row = lax.broadcasted_iota(jnp.int32, (nh, LANES), 0)
        lane = lax.broadcasted_iota(jnp.int32, (nh, LANES), 1)
        out = jnp.zeros((nh, LANES), F32)
        bkt = bk_ref[...]
        for hd in range(nh):
            x = db_ref[hd]
            for r in range(REL_BUCKETS):
                part = jnp.sum(jnp.where(bkt == r, x, 0.0), axis=1, keepdims=True)
                tot = jnp.sum(part, axis=0, keepdims=True)
                out = out + jnp.where((row == hd) & (lane == r), tot, 0.0)
        o_ref[...] = out

    return pl.pallas_call(
        body, name=name, out_shape=jax.ShapeDtypeStruct((nh, LANES), F32),
        compiler_params=pltpu.CompilerParams(vmem_limit_bytes=VMEM_LIMIT),
    )(db2, bk2)


def combine_fwd(name, outs, lzs):
    n_g = len(outs)
    t, gw = outs[0].shape
    tm = _tile(t, ROW_TILE)

    def body(*refs):
        o_refs, lz_refs, y_ref = refs[:n_g], refs[n_g:2 * n_g], refs[2 * n_g]
        lz = [r[...] for r in lz_refs]
        mx = functools.reduce(jnp.maximum, lz)
        e = [jnp.exp(x - mx) for x in lz]
        den = functools.reduce(lambda a, b: a + b, e)
        for g in range(n_g):
            y_ref[:, g * gw:(g + 1) * gw] = (e[g] / den * o_refs[g][...]).astype(BF16)

    return pl.pallas_call(
        body, name=name, grid=(t // tm,), in_specs=[_rows(gw, tm)] * (2 * n_g), out_specs=_rows(n_g * gw, tm),
        out_shape=jax.ShapeDtypeStruct((t, n_g * gw), BF16), compiler_params=_params(("parallel",)),
    )(*outs, *lzs)


def combine_bwd(name, dy, outs, lzs):
    n_g = len(outs)
    t, gw = outs[0].shape
    tm = _tile(t, ROW_TILE)
    nh = gw // HEAD_DIM

    def body(*refs):
        dy_ref = refs[0]
        o_refs, lz_refs = refs[1:1 + n_g], refs[1 + n_g:1 + 2 * n_g]
        do_refs, dl_refs = refs[1 + 2 * n_g:1 + 3 * n_g], refs[1 + 3 * n_g:]
        lz = [r[...] for r in lz_refs]
        mx = functools.reduce(jnp.maximum, lz)
        e = [jnp.exp(x - mx) for x in lz]
        den = functools.reduce(lambda a, b: a + b, e)
        wts = [x / den for x in e]
        for g in range(n_g):
            do_refs[g][...] = (wts[g] * dy_ref[:, g * gw:(g + 1) * gw]).astype(BF16)
        for hd in range(nh):
            mix = jnp.zeros((tm, HEAD_DIM), F32)
            for g in range(n_g):
                prod = dy_ref[:, g * gw + hd * HEAD_DIM:g * gw + (hd + 1) * HEAD_DIM] * o_refs[g][:, _hs(hd)]
                dw = jnp.broadcast_to(jnp.sum(prod, axis=-1, keepdims=True), (tm, HEAD_DIM))
                mix = mix + wts[g][:, _hs(hd)] * dw
            for g in range(n_g):
                dl_refs[g][:, _hs(hd)] = wts[g][:, _hs(hd)] * mix

    return pl.pallas_call(
        body, name=name, grid=(t // tm,),
        in_specs=[_rows(n_g * gw, tm)] + [_rows(gw, tm)] * (2 * n_g),
        out_specs=[_rows(gw, tm)] * (2 * n_g),
        out_shape=[jax.ShapeDtypeStruct((t, gw), BF16)] * n_g + [jax.ShapeDtypeStruct((t, gw), F32)] * n_g,
        compiler_params=_params(("parallel",)),
    )(dy, *outs, *lzs)


def _shifted(u):
    t = u.shape[0]
    row = lax.broadcasted_iota(jnp.int32, u.shape, 0)
    prev = jnp.where(row == 0, 0.0, pltpu.roll(u, 1, 0))
    nxt = jnp.where(row == t - 1, 0.0, pltpu.roll(u, t - 1, 0))
    return prev, nxt


def _conv3(u, prev, nxt, w_ref, b):
    return w_ref[0:1, :] * prev + w_ref[1:2, :] * u + w_ref[2:3, :] * nxt + b


def _conv3_t(d, w_ref):
    prev, nxt = _shifted(d)
    return w_ref[0:1, :] * nxt + w_ref[1:2, :] * d + w_ref[2:3, :] * prev


def conv_act_fwd(name, u2, cw2, cb2):
    _, t, dff = u2.shape
    tn = LANES

    def body(u_ref, w_ref, b_ref, o_ref):
        ug, uv = u_ref[0], u_ref[1]
        cg = _conv3(ug, *_shifted(ug), w_ref.at[0], b_ref[0])
        cv = _conv3(uv, *_shifted(uv), w_ref.at[1], b_ref[1])
        o_ref[...] = (cg * jax.nn.sigmoid(cg) * cv).astype(BF16)

    return pl.pallas_call(
        body, name=name, grid=(dff // tn,),
        in_specs=[pl.BlockSpec((2, t, tn), lambda j: (0, 0, j)), pl.BlockSpec((2, 3, tn), lambda j: (0, 0, j)),
                  pl.BlockSpec((2, 1, tn), lambda j: (0, 0, j))],
        out_specs=pl.BlockSpec((t, tn), lambda j: (0, j)), out_shape=jax.ShapeDtypeStruct((t, dff), BF16),
        compiler_params=_params(("parallel",)),
    )(u2, cw2, cb2)


def conv_act_bwd(name, u2, cw2, cb2, dact, after=()):
    _, t, dff = u2.shape
    tn = LANES

    def body(u_ref, w_ref, b_ref, d_ref, *rest):
        du_ref, dw_ref = rest[-2:]
        d = d_ref[...]
        ug, uv = u_ref[0], u_ref[1]
        shifted = (_shifted(ug), _shifted(uv))
        cg = _conv3(ug, *shifted[0], w_ref.at[0], b_ref[0])
        cv = _conv3(uv, *shifted[1], w_ref.at[1], b_ref[1])
        sg = jax.nn.sigmoid(cg)
        dcv = d * (cg * sg)
        dcg = d * cv * (sg * (1.0 + cg * (1.0 - sg)))
        du_ref[0] = _conv3_t(dcg, w_ref.at[0]).astype(BF16)
        du_ref[1] = _conv3_t(dcv, w_ref.at[1]).astype(BF16)
        for half, (dc, u) in enumerate(((dcg, ug), (dcv, uv))):
            prev, nxt = shifted[half]
            for tap, x in enumerate((prev, u, nxt)):
                dw_ref[half, tap:tap + 1, :] = jnp.sum(dc * x, axis=0, keepdims=True)
            dw_ref[half, 3:4, :] = jnp.sum(dc, axis=0, keepdims=True)
            dw_ref[half, 4:8, :] = jnp.zeros((4, tn), F32)

    return pl.pallas_call(
        body, name=name, grid=(dff // tn,),
        in_specs=[pl.BlockSpec((2, t, tn), lambda j: (0, 0, j)), pl.BlockSpec((2, 3, tn), lambda j: (0, 0, j)),
                  pl.BlockSpec((2, 1, tn), lambda j: (0, 0, j)), pl.BlockSpec((t, tn), lambda j: (0, j))]
        + [pl.BlockSpec(memory_space=pl.ANY)] * len(after),
        out_specs=[pl.BlockSpec((2, t, tn), lambda j: (0, 0, j)), pl.BlockSpec((2, 8, tn), lambda j: (0, 0, j))],
        out_shape=[jax.ShapeDtypeStruct((2, t, dff), BF16), jax.ShapeDtypeStruct((2, 8, dff), F32)],
        compiler_params=_params(("parallel",)),
    )(u2, cw2, cb2, dact, *after)


GATHER_ID, SIBLING_ID, CHIPS_ID = 0, 1, 2


def _place():
    x, y, c = lax.axis_index("x"), lax.axis_index("y"), lax.axis_index("c")
    chips = [(1 - x, y), (x, 1 - y), (1 - x, 1 - y)]
    return x, y, c, chips


def _handshake(peers):
    barrier = pltpu.get_barrier_semaphore()
    for peer in peers:
        pl.semaphore_signal(barrier, inc=1, device_id=peer, device_id_type=MESH)
    pl.semaphore_wait(barrier, len(peers))


def _sequencer(name, body, out_type, scratch_types, collective_id):
    return pl.kernel(body, out_type=out_type, mesh=plsc.ScalarSubcoreMesh(axis_name="seq", num_cores=1),
                     scratch_types=scratch_types, name=name,
                     compiler_params=pltpu.CompilerParams(collective_id=collective_id))


def _gather_body(n):
    def body(*refs):
        src, out = refs[:n], refs[n:2 * n]
        send, recv, loc = refs[2 * n:]
        x, y, c, chips = _place()
        sibling = (x, y, 1 - c)
        _handshake([sibling] + [(*chip, c) for chip in chips])

        def slot(a, px, py, pc):
            return out[a].at[4 * px + 2 * py + pc]

        def copy(a, k, block, to, from_src=False):
            return pltpu.make_async_remote_copy(
                src_ref=src[a] if from_src else slot(a, *block), dst_ref=slot(a, *block),
                send_sem=send.at[a, k], recv_sem=recv.at[a, k], device_id=to, device_id_type=MESH)

        mine = [pltpu.make_async_copy(src[a], slot(a, x, y, c), loc.at[a]) for a in range(n)]
        for cp in mine:
            cp.start()
        first = []
        for a in range(n):
            first.append(copy(a, 0, (x, y, c), sibling, True))
            first += [copy(a, 1 + j, (x, y, c), (*chip, c), True) for j, chip in enumerate(chips)]
        for cp in first:
            cp.start()
        passed = []
        for j, chip in enumerate(chips):
            for a in range(n):
                copy(a, 1 + j, (*chip, c), (x, y, c)).wait_recv()
                cp = copy(a, 4 + j, (*chip, c), sibling)
                cp.start()
                passed.append(cp)
        for a in range(n):
            copy(a, 0, sibling, (x, y, c)).wait_recv()
            for j, chip in enumerate(chips):
                copy(a, 4 + j, (*chip, 1 - c), (x, y, c)).wait_recv()
        for cp in first + passed:
            cp.wait_send()
        for cp in mine:
            cp.wait()

    return body


def gather_layer(name, shards):
    n = len(shards)
    out_type = [jax.ShapeDtypeStruct((N_DEV,) + s.shape, s.dtype) for s in shards]
    scratch = [pltpu.SemaphoreType.DMA((n, 7)), pltpu.SemaphoreType.DMA((n, 7)), pltpu.SemaphoreType.DMA((n,))]
    return _sequencer(name, _gather_body(n), out_type, scratch, GATHER_ID)(*shards)


def _to_sibling_body(n):
    def body(*refs):
        src, got = refs[:n], refs[n:2 * n]
        send, recv = refs[2 * n:]
        x, y, c, _ = _place()
        sibling = (x, y, 1 - c)
        _handshake([sibling])
        remote = []
        for a in range(n):
            for q in range(4):
                remote.append(pltpu.make_async_remote_copy(
                    src_ref=src[a].at[2 * q + 1 - c], dst_ref=got[a].at[q], send_sem=send.at[a, q],
                    recv_sem=recv.at[a, q], device_id=sibling, device_id_type=MESH))
        for cp in remote:
            cp.start()
        for cp in remote:
            cp.wait()

    return body


def grads_to_sibling(name, grads):
    n = len(grads)
    out_type = [jax.ShapeDtypeStruct((4,) + g.shape[1:], g.dtype) for g in grads]
    scratch = [pltpu.SemaphoreType.DMA((n, 4)), pltpu.SemaphoreType.DMA((n, 4))]
    return _sequencer(name, _to_sibling_body(n), out_type, scratch, SIBLING_ID)(*grads)


def _to_chips_body(n):
    def body(*refs):
        src, got = refs[:n], refs[n:2 * n]
        send, recv = refs[2 * n:]
        x, y, c, chips = _place()
        _handshake([(*chip, c) for chip in chips])
        remote = []
        for a in range(n):
            for j, (px, py) in enumerate(chips):
                remote.append(pltpu.make_async_remote_copy(
                    src_ref=src[a].at[2 * px + py], dst_ref=got[a].at[j], send_sem=send.at[a, j],
                    recv_sem=recv.at[a, j], device_id=(px, py, c), device_id_type=MESH))
        for cp in remote:
            cp.start()
        for cp in remote:
            cp.wait()

    return body


def grads_to_chips(name, parts):
    n = len(parts)
    out_type = [jax.ShapeDtypeStruct((3,) + p.shape[1:], p.dtype) for p in parts]
    scratch = [pltpu.SemaphoreType.DMA((n, 3)), pltpu.SemaphoreType.DMA((n, 3))]
    return _sequencer(name, _to_chips_body(n), out_type, scratch, CHIPS_ID)(*parts)


def all_reduce_small(name, vec):
    rows, m = vec.shape

    def body(x_ref, o_ref, buf, send, recv):
        x, y, c, chips = _place()
        sibling = (x, y, 1 - c)

        def blk(px, py, pc):
            return buf.at[pl.ds(pl.multiple_of((4 * px + 2 * py + pc) * rows, rows), rows), :]

        def copy(k, block, to):
            return pltpu.make_async_remote_copy(src_ref=blk(*block), dst_ref=blk(*block), send_sem=send.at[k],
                                                recv_sem=recv.at[k], device_id=to, device_id_type=MESH)

        blk(x, y, c)[...] = x_ref[...]
        first = [copy(0, (x, y, c), sibling)] + [copy(1 + j, (x, y, c), (*chip, c)) for j, chip in enumerate(chips)]
        for cp in first:
            cp.start()
        passed = [copy(4 + j, (*chip, c), sibling) for j, chip in enumerate(chips)]
        for j, chip in enumerate(chips):
            copy(1 + j, (*chip, c), (x, y, c)).wait_recv()
            passed[j].start()
        copy(0, sibling, (x, y, c)).wait_recv()
        for j, chip in enumerate(chips):
            copy(4 + j, (*chip, 1 - c), (x, y, c)).wait_recv()
        for cp in first + passed:
            cp.wait_send()
        tot = buf[0:rows, :]
        for dev in range(1, N_DEV):
            tot = tot + buf[dev * rows:(dev + 1) * rows, :]
        o_ref[...] = tot

    return pl.pallas_call(
        body, name=name, in_specs=[pl.BlockSpec(memory_space=pltpu.VMEM)],
        out_specs=pl.BlockSpec(memory_space=pltpu.VMEM), out_shape=jax.ShapeDtypeStruct((rows, m), F32),
        scratch_shapes=[pltpu.VMEM((N_DEV * rows, m), F32), pltpu.SemaphoreType.DMA((7,)),
                        pltpu.SemaphoreType.DMA((7,))],
        compiler_params=pltpu.CompilerParams(vmem_limit_bytes=VMEM_LIMIT),
    )(vec)


def _ew_tiles(rows, cols, max_elems=1 << 18):
    tr = rows
    for cand in (1024, 512, 256, 128, 64, 32, 16):
        if rows % cand == 0 and cand * cols <= max_elems:
            tr = cand
            break
    return tr


def chip_sum(name, full, got, core):
    _, kdim, ncol = full.shape
    tr = _ew_tiles(kdim, ncol, max_elems=1 << 20)
    blk = (None, tr, ncol)
    by_chip = pl.BlockSpec(blk, lambda q, i, c: (q, i, 0))

    def body(c_ref, a_ref, b_ref, o_ref):
        o_ref[...] = (a_ref[...].astype(F32) + b_ref[...].astype(F32)).astype(BF16)

    return pl.pallas_call(
        body, name=name,
        grid_spec=pltpu.PrefetchScalarGridSpec(
            num_scalar_prefetch=1, grid=(4, kdim // tr),
            in_specs=[pl.BlockSpec(blk, lambda q, i, c: (2 * q + c[0], i, 0)), by_chip], out_specs=by_chip),
        out_shape=jax.ShapeDtypeStruct((4, kdim, ncol), BF16),
        compiler_params=_params(("parallel", "parallel")),
    )(core, full, got)


def _adamw_math(w, g, m, v):
    m = ADAM_B1 * m + (1.0 - ADAM_B1) * g
    v = ADAM_B2 * v + (1.0 - ADAM_B2) * (g * g)
    m_hat = m / (1.0 - ADAM_B1 ** ADAM_STEP)
    v_hat = v / (1.0 - ADAM_B2 ** ADAM_STEP)
    delta = -ADAM_LR * (m_hat / (jnp.sqrt(v_hat) + ADAM_EPS) + ADAM_WD * w)
    return delta, m, v


def adamw_layer(name, sums, got, w, m, v, outs, layer, chip):
    _, kdim, ncol = sums.shape
    tr = _ew_tiles(kdim, ncol)
    mine = pl.BlockSpec((None, tr, ncol), lambda i, q: (q[0], i, 0))
    others = pl.BlockSpec((3, tr, ncol), lambda i, q: (0, i, 0))
    param = pl.BlockSpec((None, tr, ncol), lambda i, q: (layer, i, 0))
    whole = pl.BlockSpec(memory_space=pl.ANY)

    def body(q_ref, o_ref, g_ref, w_ref, m_ref, v_ref, *rest):
        go_ref, d_ref, mo_ref, vo_ref = rest[-4:]
        g = o_ref[...].astype(F32)
        for j in range(3):
            g = g + g_ref[j].astype(F32)
        d, mn, vn = _adamw_math(w_ref[...], g, m_ref[...], v_ref[...])
        go_ref[...] = g
        d_ref[...] = d
        mo_ref[...] = mn
        vo_ref[...] = vn

    n_in = 6
    return pl.pallas_call(
        body, name=name,
        grid_spec=pltpu.PrefetchScalarGridSpec(
            num_scalar_prefetch=1, grid=(kdim // tr,),
            in_specs=[mine, others, param, param, param] + [whole] * 4, out_specs=[param] * 4),
        out_shape=[jax.ShapeDtypeStruct(w.shape, F32)] * 4,
        input_output_aliases={n_in + k: k for k in range(4)},
        compiler_params=_params(("parallel",)),
    )(chip, sums, got, w, m, v, *outs)


def adamw_small(name, g, w, m, v):
    def body(g_ref, w_ref, m_ref, v_ref, d_ref, mo_ref, vo_ref):
        d, mn, vn = _adamw_math(w_ref[...], g_ref[...], m_ref[...], v_ref[...])
        d_ref[...] = d
        mo_ref[...] = mn
        vo_ref[...] = vn

    vm = pl.BlockSpec(memory_space=pltpu.VMEM)
    return pl.pallas_call(
        body, name=name, in_specs=[vm] * 4, out_specs=[vm] * 3,
        out_shape=[jax.ShapeDtypeStruct(g.shape, F32)] * 3,
        compiler_params=pltpu.CompilerParams(vmem_limit_bytes=VMEM_LIMIT),
    )(g, w, m, v)


def _pack(parts, width):
    flat = jnp.concatenate([p.reshape(-1).astype(F32) for p in parts])
    pad = (-flat.shape[0]) % width
    return jnp.pad(flat, (0, pad)).reshape(-1, width) if pad else flat.reshape(-1, width)


def _unpack(packed, shapes):
    flat = packed.reshape(-1)
    out, off = [], 0
    for s in shapes:
        size = math.prod(s)
        out.append(flat[off:off + size].reshape(s))
        off += size
    return out


def _local_step(h, target, layers, params, on_grads=None):
    a_q_gain, a_k_gain, rel_bias, mix_norm, ffn_norm, conv_b, final_norm = params
    t, d = h.shape
    depth = len(layers)
    n_groups = len(B_GROUPS)
    hg = B_HEADS_PER_GROUP
    n_kv = A_KV_HEADS
    w_a, w_b, w_u = layers[0][0].shape[2], layers[1][0].shape[2], layers[0][2].shape[2]
    n_q = w_a * N_DEV // HEAD_DIM - 2 * n_kv
    dff = layers[0][3].shape[0]
    n_a = (depth + 1) // 2
    cb_full = conv_b.reshape(depth, 2, 1, dff)

    cos, sin = rope_tables(t)
    strides = [band_stride(t, win, dil) for win, dil in B_GROUPS]
    tables = [band_tables(rel_bias[:, g * hg:(g + 1) * hg], win, dil, strides[g], band_block(t, strides[g]))
              for g, (win, dil) in enumerate(B_GROUPS)]

    saved = []
    for i in range(depth):
        j = i // 2
        w_qkv, w_o, w_up_i, w_down_i, cw = layers[i]
        s = {"h_in": h}
        hn = rms_fwd("mix_norm_fwd", h, mix_norm[i])
        s["hn"] = hn
        if i % 2 == 0:
            qkv = mm_col_fwd("a_qkv_fwd", hn, w_qkv, F32)
            qkv_r = qk_prep_fwd("a_qk_prep_fwd", qkv, a_q_gain[j], a_k_gain[j], cos, sin, n_q, n_kv)
            o, lse = mixer_a_fwd(qkv_r, n_q, n_kv)
            s.update(qkv=qkv, qkv_r=qkv_r, o=o, lse=lse)
            h = mm_row_fwd("a_out_fwd", o, w_o, h)
        else:
            qkv = mm_col_fwd("b_qkv_fwd", hn, w_qkv, BF16)
            outs, lzs = [], []
            for g, (win, dil) in enumerate(B_GROUPS):
                o_g, lz_g = mixer_b_group_fwd(qkv, tables[g][0], strides[g], g, n_groups, dil)
                outs.append(o_g)
                lzs.append(lz_g)
            y = combine_fwd("b_combine_fwd", outs, lzs)
            s.update(qkv=qkv, outs=outs, lzs=lzs, y=y)
            h = mm_row_fwd("b_out_fwd", y, w_o, h)
        s["h_mid"] = h
        hn2 = rms_fwd("ffn_norm_fwd", h, ffn_norm[i])
        u2 = mm_col_fwd("ffn_up_fwd", hn2, w_up_i, F32, split=2)
        act = conv_act_fwd("ffn_conv_act_fwd", u2, cw, cb_full[i])
        s.update(hn2=hn2, u2=u2, act=act)
        h = mm_row_fwd("ffn_down_fwd", act, w_down_i, h)
        saved.append(s)

    dh, dh_b, d_final, loss_part = loss_head("loss_head", h, final_norm, target)

    d_mix, d_ffn, d_cw, d_cb = [None] * depth, [None] * depth, [None] * depth, [None] * depth
    d_qg, d_kg = [None] * n_a, [None] * n_a
    d_rel = jnp.zeros((n_groups * hg, LANES), F32)
    layer_grads = [{} for _ in range(depth)]
    pending = []

    def settle():
        done = []
        while pending:
            i_p, part_p, finish = pending.pop()
            layer_grads[i_p][part_p] = finish()
            done += [upd[0] for upd in layer_grads[i_p][part_p]]
        return done

    early = []

    def register(i_p, part_p, grads):
        if on_grads is None:
            layer_grads[i_p][part_p] = grads
        else:
            first, finish = on_grads(i_p, part_p, grads)
            early.extend(first)
            pending.append((i_p, part_p, finish))

    def take_early():
        first = tuple(early)
        early.clear()
        return first

    for i in reversed(range(depth)):
        j = i // 2
        w_qkv, w_o, w_up_i, w_down_i, cw = layers[i]
        s = saved[i]
        dact = mm_row_dx("ffn_down_dx", dh_b, w_down_i)
        g_down = mm_row_dw("ffn_down_dw", s["act"], dh_b)
        du2, dcw = conv_act_bwd("ffn_conv_act_bwd", s["u2"], cw, cb_full[i], dact, take_early())
        d_cw[i] = dcw[:, 0:3, :].transpose(1, 0, 2).reshape(3, 2 * dff)
        d_cb[i] = dcw[:, 3, :].reshape(2 * dff)
        g_up = mm_col_dw("ffn_up_dw", s["hn2"], du2, w_u, split=2)
        dhn2 = mm_col_dx("ffn_up_dx", du2, w_up_i, split=2)
        dh, dh_b, d_ffn[i] = rms_bwd("ffn_norm_bwd", s["h_mid"], ffn_norm[i], dhn2, dh, settle())
        register(i, "ffn", [g_up, g_down.reshape(N_DEV, -1, d)])
        if i % 2 == 0:
            do = mm_row_dx("a_out_dx", dh_b, w_o, take_early())
            g_o = mm_row_dw("a_out_dw", s["o"], dh_b)
            dlt, do_b = row_delta("a_delta", do, s["o"], n_q)
            dq, dk, dv = mixer_a_bwd(s["qkv_r"], do_b, s["lse"], dlt, n_q, n_kv)
            dqkv, dgain = qk_prep_bwd("a_qk_prep_bwd", s["qkv"], dq, dk, dv, a_q_gain[j], a_k_gain[j], cos, sin,
                                      n_q, n_kv)
            d_qg[j], d_kg[j] = dgain[0], dgain[1]
            g_qkv = mm_col_dw("a_qkv_dw", s["hn"], dqkv, w_a)
            dhn = mm_col_dx("a_qkv_dx", dqkv, w_qkv)
        else:
            dy = mm_row_dx("b_out_dx", dh_b, w_o, take_early())
            g_o = mm_row_dw("b_out_dw", s["y"], dh_b)
            res = combine_bwd("b_combine_bwd", dy, s["outs"], s["lzs"])
            dos, dlts = res[:n_groups], res[n_groups:]
            pieces, rel_rows = [], []
            for g, (win, dil) in enumerate(B_GROUPS):
                dq, dk, dv, dbias = mixer_b_group_bwd(s["qkv"], tables[g][0], dos[g], s["lzs"][g], dlts[g],
                                                      strides[g], g, n_groups, dil)
                pieces += [dq, dk, dv]
                rel_rows.append(bias_bucket_sums(f"b_bias_sums_d{dil}", dbias, tables[g][1]))
            d_rel = d_rel + jnp.concatenate(rel_rows, axis=0)
            dqkv = jnp.concatenate(pieces, axis=1)
            g_qkv = mm_col_dw("b_qkv_dw", s["hn"], dqkv, w_b)
            dhn = mm_col_dx("b_qkv_dx", dqkv, w_qkv)
        dh, dh_b, d_mix[i] = rms_bwd("mix_norm_bwd", s["h_in"], mix_norm[i], dhn, dh, settle())
        register(i, "mix", [g_qkv, g_o.reshape(N_DEV, -1, d)])
    last = pending.pop()[2] if pending else None

    d_rel_bias = d_rel[:, :REL_BUCKETS].T
    small_g = [jnp.stack(d_qg), jnp.stack(d_kg), d_rel_bias, jnp.concatenate(d_mix, 0), jnp.concatenate(d_ffn, 0),
               jnp.stack(d_cb), d_final.reshape(-1), jnp.stack(d_cw), loss_part]
    return dh, layer_grads, small_g, last


def kernel(x, a_w_qkv, a_w_o, a_q_gain, a_k_gain, b_w_qkv, b_w_o, rel_bias, mix_norm, ffn_norm, w_up, conv_w, conv_b, w_down, final_norm, loss_target, m_a_w_qkv, m_a_w_o, m_a_q_gain, m_a_k_gain, m_b_w_qkv, m_b_w_o, m_rel_bias, m_mix_norm, m_ffn_norm, m_w_up, m_conv_w, m_conv_b, m_w_down, m_final_norm, v_a_w_qkv, v_a_w_o, v_a_q_gain, v_a_k_gain, v_b_w_qkv, v_b_w_o, v_rel_bias, v_mix_norm, v_ffn_norm, v_w_up, v_conv_w, v_conv_b, v_w_down, v_final_norm):
    d = x.shape[2]
    depth = mix_norm.shape[0]
    dff = w_down.shape[1] * N_DEV
    w_u = w_up.shape[2]
    mixers = [(a_w_qkv, a_w_o, m_a_w_qkv, m_a_w_o, v_a_w_qkv, v_a_w_o),
              (b_w_qkv, b_w_o, m_b_w_qkv, m_b_w_o, v_b_w_qkv, v_b_w_o)]

    layers = []
    for i in range(depth):
        w_qkv, w_o = mixers[i % 2][0][i // 2], mixers[i % 2][1][i // 2]
        shards = [w_qkv.astype(BF16), w_o.astype(BF16), w_up[i].astype(BF16), w_down[i].astype(BF16), conv_w[i]]
        if i == 0:
            (g_qkv,) = gather_layer("gather_l0_qkv", shards[:1])
            g_o, g_up, g_down, g_cw = gather_layer("gather_l0", shards[1:])
        else:
            g_qkv, g_o, g_up, g_down, g_cw = gather_layer(f"gather_l{i}", shards)
        cw = g_cw.transpose(1, 0, 2).reshape(3, 2, dff).transpose(1, 0, 2)
        layers.append((g_qkv, g_o.reshape(-1, d), g_up, g_down.reshape(dff, d), cw))

    core = lax.axis_index("c").astype(jnp.int32).reshape(1)
    chip = (2 * lax.axis_index("x") + lax.axis_index("y")).astype(jnp.int32).reshape(1)

    def reduce_and_update(i, part, grads):
        w_qkv, w_o, m_qkv, m_o, v_qkv, v_o = mixers[i % 2]
        prefix = ("a_w_", "b_w_")[i % 2]
        state = {"mix": [(prefix + "qkv", w_qkv, m_qkv, v_qkv, i // 2), (prefix + "o", w_o, m_o, v_o, i // 2)],
                 "ffn": [("w_up", w_up, m_w_up, v_w_up, i), ("w_down", w_down, m_w_down, v_w_down, i)]}[part]
        got1 = grads_to_sibling(f"to_sibling_l{i}_{part}", grads)
        sums = [chip_sum(f"chip_sum_l{i}_{part}{a}", grads[a], got1[a], core) for a in range(2)]
        got2 = grads_to_chips(f"to_chips_l{i}_{part}", sums)

        def finish():
            for a, (key, w, m, v, layer) in enumerate(state):
                outs = big_out.get(key) or [lax.empty(w.shape, F32) for _ in range(4)]
                big_out[key] = adamw_layer(f"adamw_l{i}_{part}{a}", sums[a], got2[a], w, m, v, outs, layer, chip)
            return [big_out[key] for key, *_ in state]

        return sums, finish

    big_out = {}
    dh, _, small_g, last = _local_step(x[0], loss_target[0], layers,
                                       (a_q_gain, a_k_gain, rel_bias, mix_norm, ffn_norm, conv_b, final_norm),
                                       reduce_and_update)
    grad_x = dh[None]

    width = 2048
    packed = _pack(small_g, N_DEV * width).reshape(-1, N_DEV, width)
    n_rows = packed.shape[0]
    packed = packed.transpose(1, 0, 2).reshape(N_DEV, n_rows * width)
    red = all_reduce_small("small_all_reduce", packed)
    last()
    red = red.reshape(N_DEV, n_rows, width).transpose(1, 0, 2)
    (g_qg, g_kg, g_rel, g_mix, g_ffn, g_cb, g_fin, g_cw_all, loss) = _unpack(red, [p.shape for p in small_g])
    idx = 4 * lax.axis_index("x") + 2 * lax.axis_index("y") + lax.axis_index("c")
    g_cw_mine = lax.dynamic_slice_in_dim(g_cw_all, idx * w_u, w_u, axis=2)

    small_w = [a_q_gain, a_k_gain, rel_bias, mix_norm, ffn_norm, conv_b, final_norm, conv_w]
    small_m = [m_a_q_gain, m_a_k_gain, m_rel_bias, m_mix_norm, m_ffn_norm, m_conv_b, m_final_norm, m_conv_w]
    small_v = [v_a_q_gain, v_a_k_gain, v_rel_bias, v_mix_norm, v_ffn_norm, v_conv_b, v_final_norm, v_conv_w]
    small_grads = [g_qg, g_kg, g_rel, g_mix, g_ffn, g_cb, g_fin, g_cw_mine]
    shapes = [w.shape for w in small_w]
    pad_rows = (-_pack(small_w, width).shape[0]) % 8

    def pk8(parts):
        p = _pack(parts, width)
        return jnp.pad(p, ((0, pad_rows), (0, 0))) if pad_rows else p

    sd, sm, sv = adamw_small("adamw_small", pk8(small_grads), pk8(small_w), pk8(small_m), pk8(small_v))
    sd, sm, sv = _unpack(sd, shapes), _unpack(sm, shapes), _unpack(sv, shapes)

    names = ["a_w_qkv", "a_w_o", "a_q_gain", "a_k_gain", "b_w_qkv", "b_w_o", "rel_bias", "mix_norm", "ffn_norm",
             "w_up", "conv_w", "conv_b", "w_down", "final_norm"]
    small_names = ["a_q_gain", "a_k_gain", "rel_bias", "mix_norm", "ffn_norm", "conv_b", "final_norm", "conv_w"]
    grads, deltas, new_m, new_v = {}, {}, {}, {}
    for nm, outs in big_out.items():
        grads[nm], deltas[nm], new_m[nm], new_v[nm] = outs
    for a, nm in enumerate(small_names):
        grads[nm] = small_grads[a].reshape(shapes[a])
        deltas[nm], new_m[nm], new_v[nm] = sd[a], sm[a], sv[a]
    return (loss.reshape(()), grad_x, *[grads[n] for n in names], *[deltas[n] for n in names],
            *[new_m[n] for n in names], *[new_v[n] for n in names])
```

```python
import functools
import math

import jax
import jax.numpy as jnp
from jax import lax
from jax.experimental import pallas as pl
from jax.experimental.pallas import tpu as pltpu
from jax.experimental.pallas import tpu_sc as plsc

F32 = jnp.float32
BF16 = jnp.bfloat16
MESH = pl.DeviceIdType.MESH

N_DEV = 8
LANES = 128
HEAD_DIM = 128
VMEM_LIMIT = 56 * 1024 * 1024
GRID_W = 64
ROPE_THETA = 10000.0
A_KV_HEADS = 4
B_GROUPS = ((128, 1), (512, 4), (2048, 16))
B_HEADS_PER_GROUP = 8
REL_BUCKETS = 32
REL_MAX_DISTANCE = 1024
EPS = 1e-6
NEG_INF = -1e30
ADAM_LR = 0.001
ADAM_B1 = 0.9
ADAM_B2 = 0.999
ADAM_EPS = 1e-08
ADAM_WD = 0.01
ADAM_STEP = 10

ROW_TILE = 256
MM_TM = 1024
MM_TK = 2048
A_BQ = 1024
A_BK = 2048
B_BQ = 256
ATTN_ROWS = 16
ATTN_SCALE = HEAD_DIM ** -0.5

NN = (((1,), (0,)), ((), ()))
NT = (((1,), (1,)), ((), ()))
TN = (((0,), (0,)), ((), ()))


def _tile(n, pref):
    return pref if n % pref == 0 else n


def _div_tile(n, pref):
    for cand in range(pref - pref % LANES, 0, -LANES):
        if n % cand == 0:
            return cand
    return n


def _params(sem):
    return pltpu.CompilerParams(dimension_semantics=sem, vmem_limit_bytes=VMEM_LIMIT)


def _dot(a, b, dims):
    return lax.dot_general(a, b, dims, preferred_element_type=F32)


def _mm(name, a, b, *, grid, a_blk, a_map, b_blk, b_map, o_blk, o_map, out_shape, out_dtype, dims,
        res=None, after=()):
    nk = grid[2]
    acc_shape = tuple(d for d in o_blk if d is not None)

    def body(*refs):
        a_ref, b_ref = refs[:2]
        r_ref = None if res is None else refs[2]
        if nk == 1:
            o_ref = refs[-1]
            part = _dot(a_ref[...].astype(BF16), b_ref[...].astype(BF16), dims)
            o_ref[...] = (part if r_ref is None else part + r_ref[...]).astype(out_dtype)
            return
        o_ref, acc = refs[-2:]
        k = pl.program_id(2)

        @pl.when(k == 0)
        def _():
            acc[...] = jnp.zeros_like(acc)

        acc[...] += _dot(a_ref[...].astype(BF16), b_ref[...].astype(BF16), dims)

        @pl.when(k == nk - 1)
        def _():
            r = acc[...]
            if r_ref is not None:
                r = r + r_ref[...]
            o_ref[...] = r.astype(out_dtype)

    in_specs = [pl.BlockSpec(a_blk, a_map), pl.BlockSpec(b_blk, b_map)]
    args = [a, b]
    if res is not None:
        in_specs.append(pl.BlockSpec(o_blk, o_map))
        args.append(res)
    in_specs += [pl.BlockSpec(memory_space=pl.ANY)] * len(after)
    args += list(after)
    return pl.pallas_call(
        body, name=name, grid=grid, in_specs=in_specs, out_specs=pl.BlockSpec(o_blk, o_map),
        out_shape=jax.ShapeDtypeStruct(out_shape, out_dtype),
        scratch_shapes=[] if nk == 1 else [pltpu.VMEM(acc_shape, F32)],
        compiler_params=_params(("parallel", "parallel", "arbitrary")),
    )(*args)


def mm_col_fwd(name, a, wg, out_dtype, split=1):
    m, kdim = a.shape
    n_dev, _, w = wg.shape
    tm, tk = _tile(m, MM_TM), _div_tile(kdim, MM_TK)
    per = n_dev // split
    if split == 1:
        o_blk, o_map, o_shape = (tm, w), (lambda i, j, k: (i, j)), (m, n_dev * w)
    else:
        o_blk, o_map, o_shape = (None, tm, w), (lambda i, j, k: (j // per, i, j % per)), (split, m, per * w)
    return _mm(name, a, wg, grid=(m // tm, n_dev, kdim // tk),
               a_blk=(tm, tk), a_map=lambda i, j, k: (i, k),
               b_blk=(None, tk, w), b_map=lambda i, j, k: (j, k, 0),
               o_blk=o_blk, o_map=o_map, out_shape=o_shape, out_dtype=out_dtype, dims=NN)


def mm_col_dx(name, dy, wg, split=1):
    n_dev, kdim, w = wg.shape
    m = dy.shape[-2]
    tm, tk = _tile(m, MM_TM), _div_tile(kdim, MM_TK)
    per = n_dev // split
    if split == 1:
        a_blk, a_map = (tm, w), (lambda i, j, k: (i, k))
    else:
        a_blk, a_map = (None, tm, w), (lambda i, j, k: (k // per, i, k % per))
    return _mm(name, dy, wg, grid=(m // tm, kdim // tk, n_dev),
               a_blk=a_blk, a_map=a_map,
               b_blk=(None, tk, w), b_map=lambda i, j, k: (k, j, 0),
               o_blk=(tm, tk), o_map=lambda i, j, k: (i, j), out_shape=(m, kdim), out_dtype=F32, dims=NT)


def mm_col_dw(name, x, dy, w, split=1):
    m, kdim = x.shape
    tm, tk = _tile(m, MM_TM), _div_tile(kdim, MM_TK)
    per = N_DEV // split
    if split == 1:
        b_blk, b_map = (tm, w), (lambda i, j, k: (k, j))
    else:
        b_blk, b_map = (None, tm, w), (lambda i, j, k: (j // per, k, j % per))
    return _mm(name, x, dy, grid=(kdim // tk, N_DEV, m // tm),
               a_blk=(tm, tk), a_map=lambda i, j, k: (k, i),
               b_blk=b_blk, b_map=b_map,
               o_blk=(None, tk, w), o_map=lambda i, j, k: (j, i, 0),
               out_shape=(N_DEV, kdim, w), out_dtype=BF16, dims=TN)


def mm_row_fwd(name, a, wg, res):
    m, kdim = a.shape
    n = wg.shape[1]
    tm, tk, tn = _tile(m, MM_TM), _div_tile(kdim, MM_TK), _tile(n, 1024)
    return _mm(name, a, wg, grid=(m // tm, n // tn, kdim // tk),
               a_blk=(tm, tk), a_map=lambda i, j, k: (i, k),
               b_blk=(tk, tn), b_map=lambda i, j, k: (k, j),
               o_blk=(tm, tn), o_map=lambda i, j, k: (i, j), out_shape=(m, n), out_dtype=F32, dims=NN,
               res=res)


def mm_row_dx(name, dy, wg, after=()):
    m, n = dy.shape
    kdim = wg.shape[0]
    tm, tk, tn = _tile(m, MM_TM), _div_tile(kdim, MM_TK), _tile(n, MM_TK)
    return _mm(name, dy, wg, grid=(m // tm, kdim // tk, n // tn),
               a_blk=(tm, tn), a_map=lambda i, j, k: (i, k),
               b_blk=(tk, tn), b_map=lambda i, j, k: (j, k),
               o_blk=(tm, tk), o_map=lambda i, j, k: (i, j), out_shape=(m, kdim), out_dtype=F32, dims=NT,
               after=after)


def mm_row_dw(name, x, dy):
    m, kdim = x.shape
    n = dy.shape[1]
    tm, tk, tn = _tile(m, MM_TM), _div_tile(kdim, MM_TK), _tile(n, 1024)
    return _mm(name, x, dy, grid=(kdim // tk, n // tn, m // tm),
               a_blk=(tm, tk), a_map=lambda i, j, k: (k, i),
               b_blk=(tm, tn), b_map=lambda i, j, k: (k, j),
               o_blk=(tk, tn), o_map=lambda i, j, k: (i, j), out_shape=(kdim, n), out_dtype=BF16, dims=TN)


def _rows(d, tm):
    return pl.BlockSpec((tm, d), lambda i: (i, 0))


def _vec(d):
    return pl.BlockSpec((1, d), lambda i: (0, 0))


def rms_fwd(name, h, gain):
    t, d = h.shape
    tm = _tile(t, ROW_TILE)

    def body(h_ref, g_ref, o_ref):
        x = h_ref[...]
        rstd = lax.rsqrt(jnp.mean(x * x, axis=-1, keepdims=True) + EPS)
        o_ref[...] = (x * rstd * g_ref[...]).astype(BF16)

    return pl.pallas_call(
        body, name=name, grid=(t // tm,), in_specs=[_rows(d, tm), _vec(d)], out_specs=_rows(d, tm),
        out_shape=jax.ShapeDtypeStruct((t, d), BF16), compiler_params=_params(("parallel",)),
    )(h, gain.reshape(1, d))


def rms_bwd(name, h, gain, dy, dres, after=()):
    t, d = h.shape
    tm = _tile(t, ROW_TILE)

    def body(h_ref, g_ref, dy_ref, r_ref, *rest):
        dh_ref, dhb_ref, dg_ref = rest[-3:]

        @pl.when(pl.program_id(0) == 0)
        def _():
            dg_ref[...] = jnp.zeros_like(dg_ref)

        x = h_ref[...]
        rstd = lax.rsqrt(jnp.mean(x * x, axis=-1, keepdims=True) + EPS)
        xhat = x * rstd
        dyv = dy_ref[...]
        dxhat = dyv * g_ref[...]
        dh = r_ref[...] + rstd * (dxhat - xhat * jnp.mean(dxhat * xhat, axis=-1, keepdims=True))
        dh_ref[...] = dh
        dhb_ref[...] = dh.astype(BF16)
        dg_ref[...] += jnp.sum(dyv * xhat, axis=0, keepdims=True)

    return pl.pallas_call(
        body, name=name, grid=(t // tm,),
        in_specs=[_rows(d, tm), _vec(d), _rows(d, tm), _rows(d, tm)]
        + [pl.BlockSpec(memory_space=pl.ANY)] * len(after),
        out_specs=[_rows(d, tm), _rows(d, tm), _vec(d)],
        out_shape=[jax.ShapeDtypeStruct((t, d), F32), jax.ShapeDtypeStruct((t, d), BF16),
                   jax.ShapeDtypeStruct((1, d), F32)],
        compiler_params=_params(("arbitrary",)),
    )(h, gain.reshape(1, d), dy, dres, *after)


def loss_head(name, h, gain, target):
    t, d = h.shape
    tm = _tile(t, ROW_TILE)

    def body(h_ref, g_ref, t_ref, dh_ref, dhb_ref, dg_ref, loss_ref):
        @pl.when(pl.program_id(0) == 0)
        def _():
            dg_ref[...] = jnp.zeros_like(dg_ref)
            loss_ref[...] = jnp.zeros_like(loss_ref)

        x = h_ref[...]
        rstd = lax.rsqrt(jnp.mean(x * x, axis=-1, keepdims=True) + EPS)
        xhat = x * rstd
        err = xhat * g_ref[...] - t_ref[...]
        row = jnp.mean(err * err, axis=-1, keepdims=True)
        loss_ref[...] += 0.5 * jnp.sum(row, axis=0, keepdims=True)
        dyv = err * (1.0 / d)
        dxhat = dyv * g_ref[...]
        dh = rstd * (dxhat - xhat * jnp.mean(dxhat * xhat, axis=-1, keepdims=True))
        dh_ref[...] = dh
        dhb_ref[...] = dh.astype(BF16)
        dg_ref[...] += jnp.sum(dyv * xhat, axis=0, keepdims=True)

    return pl.pallas_call(
        body, name=name, grid=(t // tm,),
        in_specs=[_rows(d, tm), _vec(d), _rows(d, tm)],
        out_specs=[_rows(d, tm), _rows(d, tm), _vec(d), pl.BlockSpec((1, 1), lambda i: (0, 0))],
        out_shape=[jax.ShapeDtypeStruct((t, d), F32), jax.ShapeDtypeStruct((t, d), BF16),
                   jax.ShapeDtypeStruct((1, d), F32), jax.ShapeDtypeStruct((1, 1), F32)],
        compiler_params=_params(("arbitrary",)),
    )(h, gain.reshape(1, d), target)


def rope_tables(seq):
    pos = jnp.arange(seq, dtype=jnp.int32)
    row_ids = (pos // GRID_W).astype(F32)
    col_ids = (pos % GRID_W).astype(F32)
    quarter = HEAD_DIM // 4
    inv_freq = ROPE_THETA ** (-jnp.arange(quarter, dtype=F32) / quarter)
    ar = row_ids[:, None] * inv_freq[None, :]
    ac = col_ids[:, None] * inv_freq[None, :]
    cos = jnp.concatenate([jnp.cos(ar), jnp.cos(ar), jnp.cos(ac), jnp.cos(ac)], axis=-1)
    sin = jnp.concatenate([-jnp.sin(ar), jnp.sin(ar), -jnp.sin(ac), jnp.sin(ac)], axis=-1)
    return cos, sin


def _swap_quarters(x):
    lane = lax.broadcasted_iota(jnp.int32, x.shape, 1)
    q = HEAD_DIM // 4
    return jnp.where((lane % (2 * q)) < q, pltpu.roll(x, HEAD_DIM - q, 1), pltpu.roll(x, q, 1))


def qk_prep_fwd(name, qkv, q_gain, k_gain, cos, sin, n_q, n_kv):
    t, width = qkv.shape
    tm = _tile(t, ROW_TILE)

    def body(x_ref, qg_ref, kg_ref, c_ref, s_ref, o_ref):
        c, s = c_ref[...], s_ref[...]
        for hd in range(n_q + n_kv):
            sl = slice(hd * HEAD_DIM, (hd + 1) * HEAD_DIM)
            x = x_ref[:, sl]
            g = qg_ref[...] if hd < n_q else kg_ref[...]
            xn = x * lax.rsqrt(jnp.mean(x * x, axis=-1, keepdims=True) + EPS) * g
            o_ref[:, sl] = (xn * c + _swap_quarters(xn) * s).astype(BF16)
        vs = slice((n_q + n_kv) * HEAD_DIM, width)
        o_ref[:, vs] = x_ref[:, vs].astype(BF16)

    return pl.pallas_call(
        body, name=name, grid=(t // tm,),
        in_specs=[_rows(width, tm), _vec(HEAD_DIM), _vec(HEAD_DIM), _rows(HEAD_DIM, tm), _rows(HEAD_DIM, tm)],
        out_specs=_rows(width, tm), out_shape=jax.ShapeDtypeStruct((t, width), BF16),
        compiler_params=_params(("parallel",)),
    )(qkv, q_gain.reshape(1, HEAD_DIM), k_gain.reshape(1, HEAD_DIM), cos, sin)


def qk_prep_bwd(name, qkv, dq, dk, dv, q_gain, k_gain, cos, sin, n_q, n_kv):
    t, width = qkv.shape
    tm = _tile(t, ROW_TILE)

    def body(x_ref, dq_ref, dk_ref, dv_ref, qg_ref, kg_ref, c_ref, s_ref, o_ref, dg_ref):
        @pl.when(pl.program_id(0) == 0)
        def _():
            dg_ref[...] = jnp.zeros_like(dg_ref)

        c, s = c_ref[...], s_ref[...]
        dgq = jnp.zeros((1, HEAD_DIM), F32)
        dgk = jnp.zeros((1, HEAD_DIM), F32)
        for hd in range(n_q + n_kv):
            sl = slice(hd * HEAD_DIM, (hd + 1) * HEAD_DIM)
            x = x_ref[:, sl]
            if hd < n_q:
                g, dout = qg_ref[...], dq_ref[:, sl]
            else:
                ks = slice((hd - n_q) * HEAD_DIM, (hd - n_q + 1) * HEAD_DIM)
                g, dout = kg_ref[...], dk_ref[:, ks]
            rstd = lax.rsqrt(jnp.mean(x * x, axis=-1, keepdims=True) + EPS)
            xhat = x * rstd
            dxn = dout * c + _swap_quarters(dout * s)
            part = jnp.sum(dxn * xhat, axis=0, keepdims=True)
            if hd < n_q:
                dgq = dgq + part
            else:
                dgk = dgk + part
            dxhat = dxn * g
            o_ref[:, sl] = (rstd * (dxhat - xhat * jnp.mean(dxhat * xhat, axis=-1, keepdims=True))).astype(BF16)
        o_ref[:, slice((n_q + n_kv) * HEAD_DIM, width)] = dv_ref[...].astype(BF16)
        dg_ref[0:1, :] += dgq
        dg_ref[1:2, :] += dgk

    kvw = n_kv * HEAD_DIM
    return pl.pallas_call(
        body, name=name, grid=(t // tm,),
        in_specs=[_rows(width, tm), _rows(n_q * HEAD_DIM, tm), _rows(kvw, tm), _rows(kvw, tm),
                  _vec(HEAD_DIM), _vec(HEAD_DIM), _rows(HEAD_DIM, tm), _rows(HEAD_DIM, tm)],
        out_specs=[_rows(width, tm), pl.BlockSpec((2, HEAD_DIM), lambda i: (0, 0))],
        out_shape=[jax.ShapeDtypeStruct((t, width), BF16), jax.ShapeDtypeStruct((2, HEAD_DIM), F32)],
        compiler_params=_params(("arbitrary",)),
    )(qkv, dq, dk, dv, q_gain.reshape(1, HEAD_DIM), k_gain.reshape(1, HEAD_DIM), cos, sin)


def _lanes(x, width):
    return jnp.tile(x, (1, width // LANES))


def _hs(hd):
    return slice(hd * HEAD_DIM, (hd + 1) * HEAD_DIM)


def attn_fwd(name, q, k, v, bias, *, grid, q_spec, k_spec, v_spec, b_spec, o_spec, valid, nh, shared_kv,
             bq, bk, o_shape, o_dtype):
    ns = grid[2]

    def body(*refs):
        if bias is None:
            q_ref, k_ref, v_ref, o_ref, lse_ref, m_s, l_s, acc_s = refs
            b_ref = None
        else:
            q_ref, k_ref, v_ref, b_ref, o_ref, lse_ref, m_s, l_s, acc_s = refs
        step = pl.program_id(2)

        @pl.when(step == 0)
        def _():
            m_s[...] = jnp.full_like(m_s, -jnp.inf)
            l_s[...] = jnp.zeros_like(l_s)
            acc_s[...] = jnp.zeros_like(acc_s)

        @pl.when(valid(pl.program_id(1), step))
        def _():
            ones = jnp.ones((bk, HEAD_DIM), BF16)
            v_ones = jnp.concatenate([v_ref[:, _hs(0)], ones], axis=1) if shared_kv else None
            for hd in range(nh):
                kh = _hs(0 if shared_kv else hd)
                s = _dot(q_ref[:, _hs(hd)], k_ref[:, kh], NT)
                p_rows, a_rows = [], []
                for r0 in range(0, bq, ATTN_ROWS):
                    rows = slice(r0, r0 + ATTN_ROWS)
                    z = s[rows] * ATTN_SCALE
                    if b_ref is not None:
                        z = z + b_ref[hd, rows, :]
                    m_prev = m_s[hd, rows, :]
                    m_new = jnp.maximum(m_prev, jnp.max(z, axis=-1, keepdims=True))
                    m_s[hd, rows, :] = m_new
                    p_rows.append(jnp.exp(z - _lanes(m_new, bk)).astype(BF16))
                    a_rows.append(jnp.exp(m_prev - m_new))
                rhs = v_ones if shared_kv else jnp.concatenate([v_ref[:, kh], ones], axis=1)
                pv = _dot(jnp.concatenate(p_rows, axis=0), rhs, NN)
                alpha = jnp.concatenate(a_rows, axis=0)
                acc_s[hd] = alpha * acc_s[hd] + pv[:, :HEAD_DIM]
                l_s[hd] = alpha * l_s[hd] + pv[:, HEAD_DIM:]

        @pl.when(step == ns - 1)
        def _():
            for hd in range(nh):
                o_ref[:, _hs(hd)] = (acc_s[hd] / l_s[hd]).astype(o_dtype)
                lse_ref[:, _hs(hd)] = m_s[hd] + jnp.log(l_s[hd])

    in_specs = [q_spec, k_spec, v_spec] + ([] if bias is None else [b_spec])
    args = [q, k, v] + ([] if bias is None else [bias])
    stat = pltpu.VMEM((nh, bq, LANES), F32)
    return pl.pallas_call(
        body, name=name, grid=grid, in_specs=in_specs, out_specs=[o_spec, o_spec],
        out_shape=[jax.ShapeDtypeStruct(o_shape, o_dtype), jax.ShapeDtypeStruct(o_shape, F32)],
        scratch_shapes=[stat, stat, stat],
        compiler_params=_params(("parallel", "parallel", "arbitrary")),
    )(*args)


def _probs(q_ref, k_ref, v_ref, do_ref, lse_ref, dlt_ref, b_ref, hd, kh, bq, bk, want_p=True, on_ds=None):
    s = _dot(q_ref[:, _hs(hd)], k_ref[:, kh], NT)
    dp = _dot(do_ref[:, _hs(hd)], v_ref[:, kh], NT)
    p_rows, ds_rows = [], []
    for r0 in range(0, bq, ATTN_ROWS):
        rows = slice(r0, r0 + ATTN_ROWS)
        z = s[rows] * ATTN_SCALE
        if b_ref is not None:
            z = z + b_ref[hd, rows, :]
        p = jnp.exp(z - _lanes(lse_ref[rows, _hs(hd)], bk))
        ds = p * (dp[rows] - _lanes(dlt_ref[rows, _hs(hd)], bk))
        if on_ds is not None:
            on_ds(rows, ds)
        if want_p:
            p_rows.append(p.astype(BF16))
        ds_rows.append(ds.astype(BF16))
    return (jnp.concatenate(p_rows, axis=0) if want_p else None), jnp.concatenate(ds_rows, axis=0)


def attn_bwd_dq(name, q, k, v, do, lse, dlt, *, grid, q_spec, k_spec, v_spec, nh, bq, bk, o_shape):
    ns = grid[2]
    scale = HEAD_DIM ** -0.5

    def body(q_ref, k_ref, v_ref, do_ref, lse_ref, dlt_ref, dq_ref, acc_s):
        step = pl.program_id(2)

        @pl.when(step == 0)
        def _():
            acc_s[...] = jnp.zeros_like(acc_s)

        for hd in range(nh):
            _, ds = _probs(q_ref, k_ref, v_ref, do_ref, lse_ref, dlt_ref, None, hd, _hs(0), bq, bk, want_p=False)
            acc_s[hd] += _dot(ds, k_ref[:, _hs(0)], NN)

        @pl.when(step == ns - 1)
        def _():
            for hd in range(nh):
                dq_ref[:, _hs(hd)] = acc_s[hd] * scale

    return pl.pallas_call(
        body, name=name, grid=grid, in_specs=[q_spec, k_spec, v_spec, q_spec, q_spec, q_spec],
        out_specs=q_spec, out_shape=jax.ShapeDtypeStruct(o_shape, F32),
        scratch_shapes=[pltpu.VMEM((nh, bq, LANES), F32)],
        compiler_params=_params(("parallel", "parallel", "arbitrary")),
    )(q, k, v, do, lse, dlt)


def _always(i, s):
    return s >= 0


def row_delta(name, do, o, n_heads):
    t, width = do.shape
    tm = _tile(t, ROW_TILE)

    def body(do_ref, o_ref, dl_ref, dob_ref):
        for hd in range(n_heads):
            d = do_ref[:, _hs(hd)]
            s = jnp.sum(d * o_ref[:, _hs(hd)].astype(F32), axis=-1, keepdims=True)
            dl_ref[:, _hs(hd)] = jnp.broadcast_to(s, (tm, HEAD_DIM))
            dob_ref[:, _hs(hd)] = d.astype(BF16)

    return pl.pallas_call(
        body, name=name, grid=(t // tm,), in_specs=[_rows(width, tm), _rows(width, tm)],
        out_specs=[_rows(width, tm), _rows(width, tm)],
        out_shape=[jax.ShapeDtypeStruct((t, width), F32), jax.ShapeDtypeStruct((t, width), BF16)],
        compiler_params=_params(("parallel",)),
    )(do, o)


def _a_specs(n_q, n_kv, bq, bk, q_major):
    grp = n_q // n_kv
    if q_major:
        qm, km = (lambda b, i, s: (i, b)), (lambda b, i, s: (s, n_q + b))
        vm = lambda b, i, s: (s, n_q + n_kv + b)
    else:
        qm, km = (lambda b, i, s: (s, b)), (lambda b, i, s: (i, n_q + b))
        vm = lambda b, i, s: (i, n_q + n_kv + b)
    return (pl.BlockSpec((bq, grp * HEAD_DIM), qm), pl.BlockSpec((bk, HEAD_DIM), km),
            pl.BlockSpec((bk, HEAD_DIM), vm))


def mixer_a_fwd(qkv_r, n_q, n_kv):
    t = qkv_r.shape[0]
    bq, bk = _tile(t, A_BQ), _tile(t, A_BK)
    q_spec, k_spec, v_spec = _a_specs(n_q, n_kv, bq, bk, True)
    return attn_fwd("a_attn_fwd", qkv_r, qkv_r, qkv_r, None, grid=(n_kv, t // bq, t // bk),
                    q_spec=q_spec, k_spec=k_spec, v_spec=v_spec, b_spec=None, o_spec=q_spec, valid=_always,
                    nh=n_q // n_kv, shared_kv=True, bq=bq, bk=bk, o_shape=(t, n_q * HEAD_DIM), o_dtype=BF16)


def mixer_a_bwd(qkv_r, do_b, lse, dlt, n_q, n_kv):
    t = qkv_r.shape[0]
    bq, bk = _tile(t, A_BQ), _tile(t, A_BK)
    grp = n_q // n_kv
    q_spec, k_spec, v_spec = _a_specs(n_q, n_kv, bq, bk, True)
    dq = attn_bwd_dq("a_attn_dq", qkv_r, qkv_r, qkv_r, do_b, lse, dlt, grid=(n_kv, t // bq, t // bk),
                     q_spec=q_spec, k_spec=k_spec, v_spec=v_spec, nh=grp, bq=bq, bk=bk,
                     o_shape=(t, n_q * HEAD_DIM))
    q_spec, k_spec, v_spec = _a_specs(n_q, n_kv, bq, bk, False)
    o_spec = pl.BlockSpec((bk, HEAD_DIM), lambda b, i, s: (i, b))
    dk, dv = _attn_bwd_dkv_out(qkv_r, do_b, lse, dlt, grid=(n_kv, t // bk, t // bq), q_spec=q_spec,
                               k_spec=k_spec, v_spec=v_spec, o_spec=o_spec, grp=grp, bq=bq, bk=bk,
                               o_shape=(t, n_kv * HEAD_DIM))
    return dq, dk, dv


def _attn_bwd_dkv_out(qkv_r, do_b, lse, dlt, *, grid, q_spec, k_spec, v_spec, o_spec, grp, bq, bk, o_shape):
    ns = grid[2]
    scale = HEAD_DIM ** -0.5

    def body(q_ref, k_ref, v_ref, do_ref, lse_ref, dlt_ref, dk_ref, dv_ref, dk_s, dv_s):
        step = pl.program_id(2)

        @pl.when(step == 0)
        def _():
            dk_s[...] = jnp.zeros_like(dk_s)
            dv_s[...] = jnp.zeros_like(dv_s)

        for hd in range(grp):
            p, ds = _probs(q_ref, k_ref, v_ref, do_ref, lse_ref, dlt_ref, None, hd, _hs(0), bq, bk)
            dv_s[...] += _dot(p, do_ref[:, _hs(hd)], TN)
            dk_s[...] += _dot(ds, q_ref[:, _hs(hd)], TN)

        @pl.when(step == ns - 1)
        def _():
            dk_ref[...] = dk_s[...] * scale
            dv_ref[...] = dv_s[...]

    acc = pltpu.VMEM((bk, HEAD_DIM), F32)
    return pl.pallas_call(
        body, name="a_attn_dkv", grid=grid, in_specs=[q_spec, k_spec, v_spec, q_spec, q_spec, q_spec],
        out_specs=[o_spec, o_spec], out_shape=[jax.ShapeDtypeStruct(o_shape, F32)] * 2,
        scratch_shapes=[acc, acc], compiler_params=_params(("parallel", "parallel", "arbitrary")),
    )(qkv_r, qkv_r, qkv_r, do_b, lse, dlt)


def t5_bucket(rel):
    nb = REL_BUCKETS // 2
    max_exact = nb // 2
    base = jnp.where(rel > 0, nb, 0)
    n = jnp.abs(rel)
    nf = jnp.maximum(n, 1).astype(F32)
    large = max_exact + (jnp.log(nf / max_exact) / math.log(REL_MAX_DISTANCE / max_exact)
                         * (nb - max_exact)).astype(jnp.int32)
    large = jnp.minimum(large, nb - 1)
    return base + jnp.where(n < max_exact, n, large)


def band_stride(t, win, dil):
    return 1 if t % B_BQ == 0 and win // 2 <= B_BQ else dil


def band_tables(rel_bias_g, win, dil, stride, bq):
    a = jnp.arange(bq)[:, None]
    b = jnp.arange(bq)[None, :]
    rel = jnp.stack([(s - 1) * bq + b - a for s in range(3)]) * stride
    ok = (jnp.abs(rel) <= win // 2) & (rel % dil == 0)
    bucket = t5_bucket(rel)
    bias = jnp.zeros((rel_bias_g.shape[1],) + rel.shape, F32)
    for r in range(REL_BUCKETS):
        bias = bias + jnp.where(bucket[None] == r, rel_bias_g[r][:, None, None, None], 0.0)
    return jnp.where(ok[None], bias, NEG_INF), jnp.where(ok, bucket, -1).astype(jnp.int32)


def band_block(t, stride):
    return _tile(t // stride, B_BQ)


def _b_geometry(t, dil, g, n_groups, bq):
    hg = B_HEADS_PER_GROUP
    length = t // dil
    nblk = length // bq
    gw = hg * HEAD_DIM
    per_tok = 3 * n_groups
    return hg, length, bq, nblk, gw, per_tok


def mixer_b_group_fwd(qkv, bias, dil, g, n_groups, tag):
    t = qkv.shape[0]
    hg, length, bq, nblk, gw, per_tok = _b_geometry(t, dil, g, n_groups, bias.shape[2])
    if dil > 1:
        qkv, g, per_tok = qkv[:, 3 * g * gw:3 * (g + 1) * gw], 0, 3
    view = qkv.reshape(length, dil * qkv.shape[1])
    col = lambda c, which: c * per_tok + 3 * g + which
    kblk = lambda i, s: jnp.clip(i - 1 + s, 0, nblk - 1)
    spec = lambda which, streamed: pl.BlockSpec(
        (bq, gw), (lambda c, i, s: (kblk(i, s), col(c, which))) if streamed else (lambda c, i, s: (i, col(c, which))))
    valid = lambda i, s: (i - 1 + s >= 0) & (i - 1 + s < nblk)
    o, lz = attn_fwd(f"b_attn_fwd_d{tag}", view, view, view, bias, grid=(dil, nblk, 3),
                     q_spec=spec(0, False), k_spec=spec(1, True), v_spec=spec(2, True),
                     b_spec=pl.BlockSpec((hg, None, bq, bq), lambda c, i, s: (0, s, 0, 0)),
                     o_spec=pl.BlockSpec((bq, gw), lambda c, i, s: (i, c)), valid=valid, nh=hg,
                     shared_kv=False, bq=bq, bk=bq, o_shape=(length, dil * gw), o_dtype=F32)
    return o.reshape(t, gw), lz.reshape(t, gw)


def mixer_b_group_bwd(qkv, bias, do_g, lz_g, dlt_g, dil, g, n_groups, tag):
    t = qkv.shape[0]
    hg, length, bq, nblk, gw, per_tok = _b_geometry(t, dil, g, n_groups, bias.shape[2])
    if dil > 1:
        qkv, g, per_tok = qkv[:, 3 * g * gw:3 * (g + 1) * gw], 0, 3
    view = qkv.reshape(length, dil * qkv.shape[1])
    dov, lzv, dlv = (x.reshape(length, dil * gw) for x in (do_g, lz_g, dlt_g))
    col = lambda c, which: c * per_tok + 3 * g + which
    nbr = lambda i, s: jnp.clip(i - 1 + s, 0, nblk - 1)
    valid = lambda i, s: (i - 1 + s >= 0) & (i - 1 + s < nblk)
    q_spec = pl.BlockSpec((bq, gw), lambda c, i, s: (i, col(c, 0)))
    k_spec = pl.BlockSpec((bq, gw), lambda c, i, s: (nbr(i, s), col(c, 1)))
    v_spec = pl.BlockSpec((bq, gw), lambda c, i, s: (nbr(i, s), col(c, 2)))
    stat = pl.BlockSpec((bq, gw), lambda c, i, s: (i, c))
    dq, dbias = _band_bwd_dq(f"b_attn_dq_d{tag}", view, dov, lzv, dlv, bias, grid=(dil, nblk, 3),
                             q_spec=q_spec, k_spec=k_spec, v_spec=v_spec, stat_spec=stat,
                             b_spec=pl.BlockSpec((hg, None, bq, bq), lambda c, i, s: (0, s, 0, 0)),
                             valid=valid, nh=hg, bq=bq, o_shape=(length, dil * gw))
    q_spec = pl.BlockSpec((bq, gw), lambda c, i, s: (nbr(i, s), col(c, 0)))
    k_spec = pl.BlockSpec((bq, gw), lambda c, i, s: (i, col(c, 1)))
    v_spec = pl.BlockSpec((bq, gw), lambda c, i, s: (i, col(c, 2)))
    stat = pl.BlockSpec((bq, gw), lambda c, i, s: (nbr(i, s), c))
    dk, dv = _band_bwd_dkv(f"b_attn_dkv_d{tag}", view, dov, lzv, dlv, bias, grid=(dil, nblk, 3),
                           q_spec=q_spec, k_spec=k_spec, v_spec=v_spec, stat_spec=stat,
                           b_spec=pl.BlockSpec((hg, None, bq, bq), lambda c, i, s: (0, 2 - s, 0, 0)),
                           o_spec=pl.BlockSpec((bq, gw), lambda c, i, s: (i, c)),
                           valid=valid, nh=hg, bq=bq, o_shape=(length, dil * gw))
    return dq.reshape(t, gw), dk.reshape(t, gw), dv.reshape(t, gw), dbias


def _band_bwd_dq(name, view, do, lse, dlt, bias, *, grid, q_spec, k_spec, v_spec, stat_spec, b_spec, valid,
                 nh, bq, o_shape):
    scale = HEAD_DIM ** -0.5
    bias_shape = (nh, 3, bq, bq)

    def body(q_ref, k_ref, v_ref, do_ref, lse_ref, dlt_ref, b_ref, dq_ref, db_ref, acc_s):
        step = pl.program_id(2)

        @pl.when((pl.program_id(0) == 0) & (pl.program_id(1) == 0) & (step == 0))
        def _():
            db_ref[...] = jnp.zeros_like(db_ref)

        @pl.when(step == 0)
        def _():
            acc_s[...] = jnp.zeros_like(acc_s)

        @pl.when(valid(pl.program_id(1), step))
        def _():
            for hd in range(nh):
                def add_bias_grad(rows, ds, hd=hd):
                    db_ref[hd, step, rows, :] += ds

                _, ds = _probs(q_ref, k_ref, v_ref, do_ref, lse_ref, dlt_ref, b_ref, hd, _hs(hd), bq, bq,
                               want_p=False, on_ds=add_bias_grad)
                acc_s[hd] += _dot(ds, k_ref[:, _hs(hd)], NN)

        @pl.when(step == 2)
        def _():
            for hd in range(nh):
                dq_ref[:, _hs(hd)] = (acc_s[hd] * scale).astype(BF16)

    return pl.pallas_call(
        body, name=name, grid=grid,
        in_specs=[q_spec, k_spec, v_spec, stat_spec, stat_spec, stat_spec, b_spec],
        out_specs=[stat_spec, pl.BlockSpec(bias_shape, lambda c, i, s: (0, 0, 0, 0))],
        out_shape=[jax.ShapeDtypeStruct(o_shape, BF16), jax.ShapeDtypeStruct(bias_shape, F32)],
        scratch_shapes=[pltpu.VMEM((nh, bq, LANES), F32)], compiler_params=_params(("arbitrary",) * 3),
    )(view, view, view, do, lse, dlt, bias)


def _band_bwd_dkv(name, view, do, lse, dlt, bias, *, grid, q_spec, k_spec, v_spec, stat_spec, b_spec, o_spec,
                  valid, nh, bq, o_shape):
    scale = HEAD_DIM ** -0.5

    def body(q_ref, k_ref, v_ref, do_ref, lse_ref, dlt_ref, b_ref, dk_ref, dv_ref, dk_s, dv_s):
        step = pl.program_id(2)

        @pl.when(step == 0)
        def _():
            dk_s[...] = jnp.zeros_like(dk_s)
            dv_s[...] = jnp.zeros_like(dv_s)

        @pl.when(valid(pl.program_id(1), step))
        def _():
            for hd in range(nh):
                p, ds = _probs(q_ref, k_ref, v_ref, do_ref, lse_ref, dlt_ref, b_ref, hd, _hs(hd), bq, bq)
                dv_s[hd] += _dot(p, do_ref[:, _hs(hd)], TN)
                dk_s[hd] += _dot(ds, q_ref[:, _hs(hd)], TN)

        @pl.when(step == 2)
        def _():
            for hd in range(nh):
                dk_ref[:, _hs(hd)] = (dk_s[hd] * scale).astype(BF16)
                dv_ref[:, _hs(hd)] = dv_s[hd].astype(BF16)

    acc = pltpu.VMEM((nh, bq, LANES), F32)
    return pl.pallas_call(
        body, name=name, grid=grid,
        in_specs=[q_spec, k_spec, v_spec, stat_spec, stat_spec, stat_spec, b_spec],
        out_specs=[o_spec, o_spec], out_shape=[jax.ShapeDtypeStruct(o_shape, BF16)] * 2,
        scratch_shapes=[acc, acc], compiler_params=_params(("parallel", "parallel", "arbitrary")),
    )(view, view, view, do, lse, dlt, bias)


def bias_bucket_sums(name, dbias, bucket):
    nh, _, bq, _ = dbias.shape
    db2 = dbias.reshape(nh, 3 * bq, bq)
    bk2 = bucket.reshape(3 * bq, bq)

    def body(db_ref, bk_ref, o_ref):
        row = lax.broadcasted_iota(jnp.int32, (nh, LANES), 0)
        lane = lax.broadcasted_iota(jnp.int32, (nh, LANES), 1)
        out = jnp.zeros((nh, LANES), F32)
        bkt = bk_ref[...]
        for hd in range(nh):
            x = db_ref[hd]
            for r in range(REL_BUCKETS):
                part = jnp.sum(jnp.where(bkt == r, x, 0.0), axis=1, keepdims=True)
                tot = jnp.sum(part, axis=0, keepdims=True)
                out = out + jnp.where((row == hd) & (lane == r), tot, 0.0)
        o_ref[...] = out

    return pl.pallas_call(
        body, name=name, out_shape=jax.ShapeDtypeStruct((nh, LANES), F32),
        compiler_params=pltpu.CompilerParams(vmem_limit_bytes=VMEM_LIMIT),
    )(db2, bk2)


def combine_fwd(name, outs, lzs):
    n_g = len(outs)
    t, gw = outs[0].shape
    tm = _tile(t, ROW_TILE)

    def body(*refs):
        o_refs, lz_refs, y_ref = refs[:n_g], refs[n_g:2 * n_g], refs[2 * n_g]
        lz = [r[...] for r in lz_refs]
        mx = functools.reduce(jnp.maximum, lz)
        e = [jnp.exp(x - mx) for x in lz]
        den = functools.reduce(lambda a, b: a + b, e)
        for g in range(n_g):
            y_ref[:, g * gw:(g + 1) * gw] = (e[g] / den * o_refs[g][...]).astype(BF16)

    return pl.pallas_call(
        body, name=name, grid=(t // tm,), in_specs=[_rows(gw, tm)] * (2 * n_g), out_specs=_rows(n_g * gw, tm),
        out_shape=jax.ShapeDtypeStruct((t, n_g * gw), BF16), compiler_params=_params(("parallel",)),
    )(*outs, *lzs)


def combine_bwd(name, dy, outs, lzs):
    n_g = len(outs)
    t, gw = outs[0].shape
    tm = _tile(t, ROW_TILE)
    nh = gw // HEAD_DIM

    def body(*refs):
        dy_ref = refs[0]
        o_refs, lz_refs = refs[1:1 + n_g], refs[1 + n_g:1 + 2 * n_g]
        do_refs, dl_refs = refs[1 + 2 * n_g:1 + 3 * n_g], refs[1 + 3 * n_g:]
        lz = [r[...] for r in lz_refs]
        mx = functools.reduce(jnp.maximum, lz)
        e = [jnp.exp(x - mx) for x in lz]
        den = functools.reduce(lambda a, b: a + b, e)
        wts = [x / den for x in e]
        for g in range(n_g):
            do_refs[g][...] = (wts[g] * dy_ref[:, g * gw:(g + 1) * gw]).astype(BF16)
        for hd in range(nh):
            mix = jnp.zeros((tm, HEAD_DIM), F32)
            for g in range(n_g):
                prod = dy_ref[:, g * gw + hd * HEAD_DIM:g * gw + (hd + 1) * HEAD_DIM] * o_refs[g][:, _hs(hd)]
                dw = jnp.broadcast_to(jnp.sum(prod, axis=-1, keepdims=True), (tm, HEAD_DIM))
                mix = mix + wts[g][:, _hs(hd)] * dw
            for g in range(n_g):
                dl_refs[g][:, _hs(hd)] = wts[g][:, _hs(hd)] * mix

    return pl.pallas_call(
        body, name=name, grid=(t // tm,),
        in_specs=[_rows(n_g * gw, tm)] + [_rows(gw, tm)] * (2 * n_g),
        out_specs=[_rows(gw, tm)] * (2 * n_g),
        out_shape=[jax.ShapeDtypeStruct((t, gw), BF16)] * n_g + [jax.ShapeDtypeStruct((t, gw), F32)] * n_g,
        compiler_params=_params(("parallel",)),
    )(dy, *outs, *lzs)


def _shifted(u):
    t = u.shape[0]
    row = lax.broadcasted_iota(jnp.int32, u.shape, 0)
    prev = jnp.where(row == 0, 0.0, pltpu.roll(u, 1, 0))
    nxt = jnp.where(row == t - 1, 0.0, pltpu.roll(u, t - 1, 0))
    return prev, nxt


def _conv3(u, prev, nxt, w_ref, b):
    return w_ref[0:1, :] * prev + w_ref[1:2, :] * u + w_ref[2:3, :] * nxt + b


def _conv3_t(d, w_ref):
    prev, nxt = _shifted(d)
    return w_ref[0:1, :] * nxt + w_ref[1:2, :] * d + w_ref[2:3, :] * prev


def conv_act_fwd(name, u2, cw2, cb2):
    _, t, dff = u2.shape
    tn = LANES

    def body(u_ref, w_ref, b_ref, o_ref):
        ug, uv = u_ref[0], u_ref[1]
        cg = _conv3(ug, *_shifted(ug), w_ref.at[0], b_ref[0])
        cv = _conv3(uv, *_shifted(uv), w_ref.at[1], b_ref[1])
        o_ref[...] = (cg * jax.nn.sigmoid(cg) * cv).astype(BF16)

    return pl.pallas_call(
        body, name=name, grid=(dff // tn,),
        in_specs=[pl.BlockSpec((2, t, tn), lambda j: (0, 0, j)), pl.BlockSpec((2, 3, tn), lambda j: (0, 0, j)),
                  pl.BlockSpec((2, 1, tn), lambda j: (0, 0, j))],
        out_specs=pl.BlockSpec((t, tn), lambda j: (0, j)), out_shape=jax.ShapeDtypeStruct((t, dff), BF16),
        compiler_params=_params(("parallel",)),
    )(u2, cw2, cb2)


def conv_act_bwd(name, u2, cw2, cb2, dact, after=()):
    _, t, dff = u2.shape
    tn = LANES

    def body(u_ref, w_ref, b_ref, d_ref, *rest):
        du_ref, dw_ref = rest[-2:]
        d = d_ref[...]
        ug, uv = u_ref[0], u_ref[1]
        shifted = (_shifted(ug), _shifted(uv))
        cg = _conv3(ug, *shifted[0], w_ref.at[0], b_ref[0])
        cv = _conv3(uv, *shifted[1], w_ref.at[1], b_ref[1])
        sg = jax.nn.sigmoid(cg)
        dcv = d * (cg * sg)
        dcg = d * cv * (sg * (1.0 + cg * (1.0 - sg)))
        du_ref[0] = _conv3_t(dcg, w_ref.at[0]).astype(BF16)
        du_ref[1] = _conv3_t(dcv, w_ref.at[1]).astype(BF16)
        for half, (dc, u) in enumerate(((dcg, ug), (dcv, uv))):
            prev, nxt = shifted[half]
            for tap, x in enumerate((prev, u, nxt)):
                dw_ref[half, tap:tap + 1, :] = jnp.sum(dc * x, axis=0, keepdims=True)
            dw_ref[half, 3:4, :] = jnp.sum(dc, axis=0, keepdims=True)
            dw_ref[half, 4:8, :] = jnp.zeros((4, tn), F32)

    return pl.pallas_call(
        body, name=name, grid=(dff // tn,),
        in_specs=[pl.BlockSpec((2, t, tn), lambda j: (0, 0, j)), pl.BlockSpec((2, 3, tn), lambda j: (0, 0, j)),
                  pl.BlockSpec((2, 1, tn), lambda j: (0, 0, j)), pl.BlockSpec((t, tn), lambda j: (0, j))]
        + [pl.BlockSpec(memory_space=pl.ANY)] * len(after),
        out_specs=[pl.BlockSpec((2, t, tn), lambda j: (0, 0, j)), pl.BlockSpec((2, 8, tn), lambda j: (0, 0, j))],
        out_shape=[jax.ShapeDtypeStruct((2, t, dff), BF16), jax.ShapeDtypeStruct((2, 8, dff), F32)],
        compiler_params=_params(("parallel",)),
    )(u2, cw2, cb2, dact, *after)


GATHER_ID, SIBLING_ID, CHIPS_ID = 0, 1, 2


def _place():
    x, y, c = lax.axis_index("x"), lax.axis_index("y"), lax.axis_index("c")
    chips = [(1 - x, y), (x, 1 - y), (1 - x, 1 - y)]
    return x, y, c, chips


def _handshake(peers):
    barrier = pltpu.get_barrier_semaphore()
    for peer in peers:
        pl.semaphore_signal(barrier, inc=1, device_id=peer, device_id_type=MESH)
    pl.semaphore_wait(barrier, len(peers))


def _sequencer(name, body, out_type, scratch_types, collective_id):
    return pl.kernel(body, out_type=out_type, mesh=plsc.ScalarSubcoreMesh(axis_name="seq", num_cores=1),
                     scratch_types=scratch_types, name=name,
                     compiler_params=pltpu.CompilerParams(collective_id=collective_id))


def _gather_body(n):
    def body(*refs):
        src, out = refs[:n], refs[n:2 * n]
        send, recv, loc = refs[2 * n:]
        x, y, c, chips = _place()
        sibling = (x, y, 1 - c)
        _handshake([sibling] + [(*chip, c) for chip in chips])

        def slot(a, px, py, pc):
            return out[a].at[4 * px + 2 * py + pc]

        def copy(a, k, block, to, from_src=False):
            return pltpu.make_async_remote_copy(
                src_ref=src[a] if from_src else slot(a, *block), dst_ref=slot(a, *block),
                send_sem=send.at[a, k], recv_sem=recv.at[a, k], device_id=to, device_id_type=MESH)

        mine = [pltpu.make_async_copy(src[a], slot(a, x, y, c), loc.at[a]) for a in range(n)]
        for cp in mine:
            cp.start()
        first = []
        for a in range(n):
            first.append(copy(a, 0, (x, y, c), sibling, True))
            first += [copy(a, 1 + j, (x, y, c), (*chip, c), True) for j, chip in enumerate(chips)]
        for cp in first:
            cp.start()
        passed = []
        for j, chip in enumerate(chips):
            for a in range(n):
                copy(a, 1 + j, (*chip, c), (x, y, c)).wait_recv()
                cp = copy(a, 4 + j, (*chip, c), sibling)
                cp.start()
                passed.append(cp)
        for a in range(n):
            copy(a, 0, sibling, (x, y, c)).wait_recv()
            for j, chip in enumerate(chips):
                copy(a, 4 + j, (*chip, 1 - c), (x, y, c)).wait_recv()
        for cp in first + passed:
            cp.wait_send()
        for cp in mine:
            cp.wait()

    return body


def gather_layer(name, shards):
    n = len(shards)
    out_type = [jax.ShapeDtypeStruct((N_DEV,) + s.shape, s.dtype) for s in shards]
    scratch = [pltpu.SemaphoreType.DMA((n, 7)), pltpu.SemaphoreType.DMA((n, 7)), pltpu.SemaphoreType.DMA((n,))]
    return _sequencer(name, _gather_body(n), out_type, scratch, GATHER_ID)(*shards)


def _to_sibling_body(n):
    def body(*refs):
        src, got = refs[:n], refs[n:2 * n]
        send, recv = refs[2 * n:]
        x, y, c, _ = _place()
        sibling = (x, y, 1 - c)
        _handshake([sibling])
        remote = []
        for a in range(n):
            for q in range(4):
                remote.append(pltpu.make_async_remote_copy(
                    src_ref=src[a].at[2 * q + 1 - c], dst_ref=got[a].at[q], send_sem=send.at[a, q],
                    recv_sem=recv.at[a, q], device_id=sibling, device_id_type=MESH))
        for cp in remote:
            cp.start()
        for cp in remote:
            cp.wait()

    return body


def grads_to_sibling(name, grads):
    n = len(grads)
    out_type = [jax.ShapeDtypeStruct((4,) + g.shape[1:], g.dtype) for g in grads]
    scratch = [pltpu.SemaphoreType.DMA((n, 4)), pltpu.SemaphoreType.DMA((n, 4))]
    return _sequencer(name, _to_sibling_body(n), out_type, scratch, SIBLING_ID)(*grads)


def _to_chips_body(n):
    def body(*refs):
        src, got = refs[:n], refs[n:2 * n]
        send, recv = refs[2 * n:]
        x, y, c, chips = _place()
        _handshake([(*chip, c) for chip in chips])
        remote = []
        for a in range(n):
            for j, (px, py) in enumerate(chips):
                remote.append(pltpu.make_async_remote_copy(
                    src_ref=src[a].at[2 * px + py], dst_ref=got[a].at[j], send_sem=send.at[a, j],
                    recv_sem=recv.at[a, j], device_id=(px, py, c), device_id_type=MESH))
        for cp in remote:
            cp.start()
        for cp in remote:
            cp.wait()

    return body


def grads_to_chips(name, parts):
    n = len(parts)
    out_type = [jax.ShapeDtypeStruct((3,) + p.shape[1:], p.dtype) for p in parts]
    scratch = [pltpu.SemaphoreType.DMA((n, 3)), pltpu.SemaphoreType.DMA((n, 3))]
    return _sequencer(name, _to_chips_body(n), out_type, scratch, CHIPS_ID)(*parts)


def all_reduce_small(name, vec):
    rows, m = vec.shape

    def body(x_ref, o_ref, buf, send, recv):
        x, y, c, chips = _place()
        sibling = (x, y, 1 - c)

        def blk(px, py, pc):
            return buf.at[pl.ds(pl.multiple_of((4 * px + 2 * py + pc) * rows, rows), rows), :]

        def copy(k, block, to):
            return pltpu.make_async_remote_copy(src_ref=blk(*block), dst_ref=blk(*block), send_sem=send.at[k],
                                                recv_sem=recv.at[k], device_id=to, device_id_type=MESH)

        blk(x, y, c)[...] = x_ref[...]
        first = [copy(0, (x, y, c), sibling)] + [copy(1 + j, (x, y, c), (*chip, c)) for j, chip in enumerate(chips)]
        for cp in first:
            cp.start()
        passed = [copy(4 + j, (*chip, c), sibling) for j, chip in enumerate(chips)]
        for j, chip in enumerate(chips):
            copy(1 + j, (*chip, c), (x, y, c)).wait_recv()
            passed[j].start()
        copy(0, sibling, (x, y, c)).wait_recv()
        for j, chip in enumerate(chips):
            copy(4 + j, (*chip, 1 - c), (x, y, c)).wait_recv()
        for cp in first + passed:
            cp.wait_send()
        tot = buf[0:rows, :]
        for dev in range(1, N_DEV):
            tot = tot + buf[dev * rows:(dev + 1) * rows, :]
        o_ref[...] = tot

    return pl.pallas_call(
        body, name=name, in_specs=[pl.BlockSpec(memory_space=pltpu.VMEM)],
        out_specs=pl.BlockSpec(memory_space=pltpu.VMEM), out_shape=jax.ShapeDtypeStruct((rows, m), F32),
        scratch_shapes=[pltpu.VMEM((N_DEV * rows, m), F32), pltpu.SemaphoreType.DMA((7,)),
                        pltpu.SemaphoreType.DMA((7,))],
        compiler_params=pltpu.CompilerParams(vmem_limit_bytes=VMEM_LIMIT),
    )(vec)


def _ew_tiles(rows, cols, max_elems=1 << 18):
    tr = rows
    for cand in (1024, 512, 256, 128, 64, 32, 16):
        if rows % cand == 0 and cand * cols <= max_elems:
            tr = cand
            break
    return tr


def chip_sum(name, full, got, core):
    _, kdim, ncol = full.shape
    tr = _ew_tiles(kdim, ncol, max_elems=1 << 20)
    blk = (None, tr, ncol)
    by_chip = pl.BlockSpec(blk, lambda q, i, c: (q, i, 0))

    def body(c_ref, a_ref, b_ref, o_ref):
        o_ref[...] = (a_ref[...].astype(F32) + b_ref[...].astype(F32)).astype(BF16)

    return pl.pallas_call(
        body, name=name,
        grid_spec=pltpu.PrefetchScalarGridSpec(
            num_scalar_prefetch=1, grid=(4, kdim // tr),
            in_specs=[pl.BlockSpec(blk, lambda q, i, c: (2 * q + c[0], i, 0)), by_chip], out_specs=by_chip),
        out_shape=jax.ShapeDtypeStruct((4, kdim, ncol), BF16),
        compiler_params=_params(("parallel", "parallel")),
    )(core, full, got)


def _adamw_math(w, g, m, v):
    m = ADAM_B1 * m + (1.0 - ADAM_B1) * g
    v = ADAM_B2 * v + (1.0 - ADAM_B2) * (g * g)
    m_hat = m / (1.0 - ADAM_B1 ** ADAM_STEP)
    v_hat = v / (1.0 - ADAM_B2 ** ADAM_STEP)
    delta = -ADAM_LR * (m_hat / (jnp.sqrt(v_hat) + ADAM_EPS) + ADAM_WD * w)
    return delta, m, v


def adamw_layer(name, sums, got, w, m, v, outs, layer, chip):
    _, kdim, ncol = sums.shape
    tr = _ew_tiles(kdim, ncol)
    mine = pl.BlockSpec((None, tr, ncol), lambda i, q: (q[0], i, 0))
    others = pl.BlockSpec((3, tr, ncol), lambda i, q: (0, i, 0))
    param = pl.BlockSpec((None, tr, ncol), lambda i, q: (layer, i, 0))
    whole = pl.BlockSpec(memory_space=pl.ANY)

    def body(q_ref, o_ref, g_ref, w_ref, m_ref, v_ref, *rest):
        go_ref, d_ref, mo_ref, vo_ref = rest[-4:]
        g = o_ref[...].astype(F32)
        for j in range(3):
            g = g + g_ref[j].astype(F32)
        d, mn, vn = _adamw_math(w_ref[...], g, m_ref[...], v_ref[...])
        go_ref[...] = g
        d_ref[...] = d
        mo_ref[...] = mn
        vo_ref[...] = vn

    n_in = 6
    return pl.pallas_call(
        body, name=name,
        grid_spec=pltpu.PrefetchScalarGridSpec(
            num_scalar_prefetch=1, grid=(kdim // tr,),
            in_specs=[mine, others, param, param, param] + [whole] * 4, out_specs=[param] * 4),
        out_shape=[jax.ShapeDtypeStruct(w.shape, F32)] * 4,
        input_output_aliases={n_in + k: k for k in range(4)},
        compiler_params=_params(("parallel",)),
    )(chip, sums, got, w, m, v, *outs)


def adamw_small(name, g, w, m, v):
    def body(g_ref, w_ref, m_ref, v_ref, d_ref, mo_ref, vo_ref):
        d, mn, vn = _adamw_math(w_ref[...], g_ref[...], m_ref[...], v_ref[...])
        d_ref[...] = d
        mo_ref[...] = mn
        vo_ref[...] = vn

    vm = pl.BlockSpec(memory_space=pltpu.VMEM)
    return pl.pallas_call(
        body, name=name, in_specs=[vm] * 4, out_specs=[vm] * 3,
        out_shape=[jax.ShapeDtypeStruct(g.shape, F32)] * 3,
        compiler_params=pltpu.CompilerParams(vmem_limit_bytes=VMEM_LIMIT),
    )(g, w, m, v)


def _pack(parts, width):
    flat = jnp.concatenate([p.reshape(-1).astype(F32) for p in parts])
    pad = (-flat.shape[0]) % width
    return jnp.pad(flat, (0, pad)).reshape(-1, width) if pad else flat.reshape(-1, width)


def _unpack(packed, shapes):
    flat = packed.reshape(-1)
    out, off = [], 0
    for s in shapes:
        size = math.prod(s)
        out.append(flat[off:off + size].reshape(s))
        off += size
    return out


def _local_step(h, target, layers, params, on_grads=None):
    a_q_gain, a_k_gain, rel_bias, mix_norm, ffn_norm, conv_b, final_norm = params
    t, d = h.shape
    depth = len(layers)
    n_groups = len(B_GROUPS)
    hg = B_HEADS_PER_GROUP
    n_kv = A_KV_HEADS
    w_a, w_b, w_u = layers[0][0].shape[2], layers[1][0].shape[2], layers[0][2].shape[2]
    n_q = w_a * N_DEV // HEAD_DIM - 2 * n_kv
    dff = layers[0][3].shape[0]
    n_a = (depth + 1) // 2
    cb_full = conv_b.reshape(depth, 2, 1, dff)

    cos, sin = rope_tables(t)
    strides = [band_stride(t, win, dil) for win, dil in B_GROUPS]
    tables = [band_tables(rel_bias[:, g * hg:(g + 1) * hg], win, dil, strides[g], band_block(t, strides[g]))
              for g, (win, dil) in enumerate(B_GROUPS)]

    saved = []
    for i in range(depth):
        j = i // 2
        w_qkv, w_o, w_up_i, w_down_i, cw = layers[i]
        s = {"h_in": h}
        hn = rms_fwd("mix_norm_fwd", h, mix_norm[i])
        s["hn"] = hn
        if i % 2 == 0:
            qkv = mm_col_fwd("a_qkv_fwd", hn, w_qkv, F32)
            qkv_r = qk_prep_fwd("a_qk_prep_fwd", qkv, a_q_gain[j], a_k_gain[j], cos, sin, n_q, n_kv)
            o, lse = mixer_a_fwd(qkv_r, n_q, n_kv)
            s.update(qkv=qkv, qkv_r=qkv_r, o=o, lse=lse)
            h = mm_row_fwd("a_out_fwd", o, w_o, h)
        else:
            qkv = mm_col_fwd("b_qkv_fwd", hn, w_qkv, BF16)
            outs, lzs = [], []
            for g, (win, dil) in enumerate(B_GROUPS):
                o_g, lz_g = mixer_b_group_fwd(qkv, tables[g][0], strides[g], g, n_groups, dil)
                outs.append(o_g)
                lzs.append(lz_g)
            y = combine_fwd("b_combine_fwd", outs, lzs)
            s.update(qkv=qkv, outs=outs, lzs=lzs, y=y)
            h = mm_row_fwd("b_out_fwd", y, w_o, h)
        s["h_mid"] = h
        hn2 = rms_fwd("ffn_norm_fwd", h, ffn_norm[i])
        u2 = mm_col_fwd("ffn_up_fwd", hn2, w_up_i, F32, split=2)
        act = conv_act_fwd("ffn_conv_act_fwd", u2, cw, cb_full[i])
        s.update(hn2=hn2, u2=u2, act=act)
        h = mm_row_fwd("ffn_down_fwd", act, w_down_i, h)
        saved.append(s)

    dh, dh_b, d_final, loss_part = loss_head("loss_head", h, final_norm, target)

    d_mix, d_ffn, d_cw, d_cb = [None] * depth, [None] * depth, [None] * depth, [None] * depth
    d_qg, d_kg = [None] * n_a, [None] * n_a
    d_rel = jnp.zeros((n_groups * hg, LANES), F32)
    layer_grads = [{} for _ in range(depth)]
    pending = []

    def settle():
        done = []
        while pending:
            i_p, part_p, finish = pending.pop()
            layer_grads[i_p][part_p] = finish()
            done += [upd[0] for upd in layer_grads[i_p][part_p]]
        return done

    early = []

    def register(i_p, part_p, grads):
        if on_grads is None:
            layer_grads[i_p][part_p] = grads
        else:
            first, finish = on_grads(i_p, part_p, grads)
            early.extend(first)
            pending.append((i_p, part_p, finish))

    def take_early():
        first = tuple(early)
        early.clear()
        return first

    for i in reversed(range(depth)):
        j = i // 2
        w_qkv, w_o, w_up_i, w_down_i, cw = layers[i]
        s = saved[i]
        dact = mm_row_dx("ffn_down_dx", dh_b, w_down_i)
        g_down = mm_row_dw("ffn_down_dw", s["act"], dh_b)
        du2, dcw = conv_act_bwd("ffn_conv_act_bwd", s["u2"], cw, cb_full[i], dact, take_early())
        d_cw[i] = dcw[:, 0:3, :].transpose(1, 0, 2).reshape(3, 2 * dff)
        d_cb[i] = dcw[:, 3, :].reshape(2 * dff)
        g_up = mm_col_dw("ffn_up_dw", s["hn2"], du2, w_u, split=2)
        dhn2 = mm_col_dx("ffn_up_dx", du2, w_up_i, split=2)
        dh, dh_b, d_ffn[i] = rms_bwd("ffn_norm_bwd", s["h_mid"], ffn_norm[i], dhn2, dh, settle())
        register(i, "ffn", [g_up, g_down.reshape(N_DEV, -1, d)])
        if i % 2 == 0:
            do = mm_row_dx("a_out_dx", dh_b, w_o, take_early())
            g_o = mm_row_dw("a_out_dw", s["o"], dh_b)
            dlt, do_b = row_delta("a_delta", do, s["o"], n_q)
            dq, dk, dv = mixer_a_bwd(s["qkv_r"], do_b, s["lse"], dlt, n_q, n_kv)
            dqkv, dgain = qk_prep_bwd("a_qk_prep_bwd", s["qkv"], dq, dk, dv, a_q_gain[j], a_k_gain[j], cos, sin,
                                      n_q, n_kv)
            d_qg[j], d_kg[j] = dgain[0], dgain[1]
            g_qkv = mm_col_dw("a_qkv_dw", s["hn"], dqkv, w_a)
            dhn = mm_col_dx("a_qkv_dx", dqkv, w_qkv)
        else:
            dy = mm_row_dx("b_out_dx", dh_b, w_o, take_early())
            g_o = mm_row_dw("b_out_dw", s["y"], dh_b)
            res = combine_bwd("b_combine_bwd", dy, s["outs"], s["lzs"])
            dos, dlts = res[:n_groups], res[n_groups:]
            pieces, rel_rows = [], []
            for g, (win, dil) in enumerate(B_GROUPS):
                dq, dk, dv, dbias = mixer_b_group_bwd(s["qkv"], tables[g][0], dos[g], s["lzs"][g], dlts[g],
                                                      strides[g], g, n_groups, dil)
                pieces += [dq, dk, dv]
                rel_rows.append(bias_bucket_sums(f"b_bias_sums_d{dil}", dbias, tables[g][1]))
            d_rel = d_rel + jnp.concatenate(rel_rows, axis=0)
            dqkv = jnp.concatenate(pieces, axis=1)
            g_qkv = mm_col_dw("b_qkv_dw", s["hn"], dqkv, w_b)
            dhn = mm_col_dx("b_qkv_dx", dqkv, w_qkv)
        dh, dh_b, d_mix[i] = rms_bwd("mix_norm_bwd", s["h_in"], mix_norm[i], dhn, dh, settle())
        register(i, "mix", [g_qkv, g_o.reshape(N_DEV, -1, d)])
    last = pending.pop()[2] if pending else None

    d_rel_bias = d_rel[:, :REL_BUCKETS].T
    small_g = [jnp.stack(d_qg), jnp.stack(d_kg), d_rel_bias, jnp.concatenate(d_mix, 0), jnp.concatenate(d_ffn, 0),
               jnp.stack(d_cb), d_final.reshape(-1), jnp.stack(d_cw), loss_part]
    return dh, layer_grads, small_g, last


def kernel(x, a_w_qkv, a_w_o, a_q_gain, a_k_gain, b_w_qkv, b_w_o, rel_bias, mix_norm, ffn_norm, w_up, conv_w, conv_b, w_down, final_norm, loss_target, m_a_w_qkv, m_a_w_o, m_a_q_gain, m_a_k_gain, m_b_w_qkv, m_b_w_o, m_rel_bias, m_mix_norm, m_ffn_norm, m_w_up, m_conv_w, m_conv_b, m_w_down, m_final_norm, v_a_w_qkv, v_a_w_o, v_a_q_gain, v_a_k_gain, v_b_w_qkv, v_b_w_o, v_rel_bias, v_mix_norm, v_ffn_norm, v_w_up, v_conv_w, v_conv_b, v_w_down, v_final_norm):
    d = x.shape[2]
    depth = mix_norm.shape[0]
    dff = w_down.shape[1] * N_DEV
    w_u = w_up.shape[2]
    mixers = [(a_w_qkv, a_w_o, m_a_w_qkv, m_a_w_o, v_a_w_qkv, v_a_w_o),
              (b_w_qkv, b_w_o, m_b_w_qkv, m_b_w_o, v_b_w_qkv, v_b_w_o)]

    layers = []
    for i in range(depth):
        w_qkv, w_o = mixers[i % 2][0][i // 2], mixers[i % 2][1][i // 2]
        shards = [w_qkv.astype(BF16), w_o.astype(BF16), w_up[i].astype(BF16), w_down[i].astype(BF16), conv_w[i]]
        if i == 0:
            (g_qkv,) = gather_layer("gather_l0_qkv", shards[:1])
            g_o, g_up, g_down, g_cw = gather_layer("gather_l0", shards[1:])
        else:
            g_qkv, g_o, g_up, g_down, g_cw = gather_layer(f"gather_l{i}", shards)
        cw = g_cw.transpose(1, 0, 2).reshape(3, 2, dff).transpose(1, 0, 2)
        layers.append((g_qkv, g_o.reshape(-1, d), g_up, g_down.reshape(dff, d), cw))

    core = lax.axis_index("c").astype(jnp.int32).reshape(1)
    chip = (2 * lax.axis_index("x") + lax.axis_index("y")).astype(jnp.int32).reshape(1)

    def reduce_and_update(i, part, grads):
        w_qkv, w_o, m_qkv, m_o, v_qkv, v_o = mixers[i % 2]
        prefix = ("a_w_", "b_w_")[i % 2]
        state = {"mix": [(prefix + "qkv", w_qkv, m_qkv, v_qkv, i // 2), (prefix + "o", w_o, m_o, v_o, i // 2)],
                 "ffn": [("w_up", w_up, m_w_up, v_w_up, i), ("w_down", w_down, m_w_down, v_w_down, i)]}[part]
        got1 = grads_to_sibling(f"to_sibling_l{i}_{part}", grads)
        sums = [chip_sum(f"chip_sum_l{i}_{part}{a}", grads[a], got1[a], core) for a in range(2)]
        got2 = grads_to_chips(f"to_chips_l{i}_{part}", sums)

        def finish():
            for a, (key, w, m, v, layer) in enumerate(state):
                outs = big_out.get(key) or [lax.empty(w.shape, F32) for _ in range(4)]
                big_out[key] = adamw_layer(f"adamw_l{i}_{part}{a}", sums[a], got2[a], w, m, v, outs, layer, chip)
            return [big_out[key] for key, *_ in state]

        return sums, finish

    big_out = {}
    dh, _, small_g, last = _local_step(x[0], loss_target[0], layers,
                                       (a_q_gain, a_k_gain, rel_bias, mix_norm, ffn_norm, conv_b, final_norm),
                                       reduce_and_update)
    grad_x = dh[None]

    width = 2048
    packed = _pack(small_g, N_DEV * width).reshape(-1, N_DEV, width)
    n_rows = packed.shape[0]
    packed = packed.transpose(1, 0, 2).reshape(N_DEV, n_rows * width)
    red = all_reduce_small("small_all_reduce", packed)
    last()
    red = red.reshape(N_DEV, n_rows, width).transpose(1, 0, 2)
    (g_qg, g_kg, g_rel, g_mix, g_ffn, g_cb, g_fin, g_cw_all, loss) = _unpack(red, [p.shape for p in small_g])
    idx = 4 * lax.axis_index("x") + 2 * lax.axis_index("y") + lax.axis_index("c")
    g_cw_mine = lax.dynamic_slice_in_dim(g_cw_all, idx * w_u, w_u, axis=2)

    small_w = [a_q_gain, a_k_gain, rel_bias, mix_norm, ffn_norm, conv_b, final_norm, conv_w]
    small_m = [m_a_q_gain, m_a_k_gain, m_rel_bias, m_mix_norm, m_ffn_norm, m_conv_b, m_final_norm, m_conv_w]
    small_v = [v_a_q_gain, v_a_k_gain, v_rel_bias, v_mix_norm, v_ffn_norm, v_conv_b, v_final_norm, v_conv_w]
    small_grads = [g_qg, g_kg, g_rel, g_mix, g_ffn, g_cb, g_fin, g_cw_mine]
    shapes = [w.shape for w in small_w]
    pad_rows = (-_pack(small_w, width).shape[0]) % 8

    def pk8(parts):
        p = _pack(parts, width)
        return jnp.pad(p, ((0, pad_rows), (0, 0))) if pad_rows else p

    sd, sm, sv = adamw_small("adamw_small", pk8(small_grads), pk8(small_w), pk8(small_m), pk8(small_v))
    sd, sm, sv = _unpack(sd, shapes), _unpack(sm, shapes), _unpack(sv, shapes)

    names = ["a_w_qkv", "a_w_o", "a_q_gain", "a_k_gain", "b_w_qkv", "b_w_o", "rel_bias", "mix_norm", "ffn_norm",
             "w_up", "conv_w", "conv_b", "w_down", "final_norm"]
    small_names = ["a_q_gain", "a_k_gain", "rel_bias", "mix_norm", "ffn_norm", "conv_b", "final_norm", "conv_w"]
    grads, deltas, new_m, new_v = {}, {}, {}, {}
    for nm, outs in big_out.items():
        grads[nm], deltas[nm], new_m[nm], new_v[nm] = outs
    for a, nm in enumerate(small_names):
        grads[nm] = small_grads[a].reshape(shapes[a])
        deltas[nm], new_m[nm], new_v[nm] = sd[a], sm[a], sv[a]
    return (loss.reshape(()), grad_x, *[grads[n] for n in names], *[deltas[n] for n in names],
            *[new_m[n] for n in names], *[new_v[n] for n in names])
```

```python
import functools
import math

import jax
import jax.numpy as jnp
from jax import lax
from jax.experimental import pallas as pl
from jax.experimental.pallas import tpu as pltpu
from jax.experimental.pallas import tpu_sc as plsc

F32 = jnp.float32
BF16 = jnp.bfloat16
MESH = pl.DeviceIdType.MESH

N_DEV = 8
LANES = 128
HEAD_DIM = 128
VMEM_LIMIT = 56 * 1024 * 1024
GRID_W = 64
ROPE_THETA = 10000.0
A_KV_HEADS = 4
B_GROUPS = ((128, 1), (512, 4), (2048, 16))
B_HEADS_PER_GROUP = 8
REL_BUCKETS = 32
REL_MAX_DISTANCE = 1024
EPS = 1e-6
NEG_INF = -1e30
ADAM_LR = 0.001
ADAM_B1 = 0.9
ADAM_B2 = 0.999
ADAM_EPS = 1e-08
ADAM_WD = 0.01
ADAM_STEP = 10

ROW_TILE = 256
MM_TM = 1024
MM_TK = 2048
A_BQ = 1024
A_BK = 2048
B_BQ = 256
ATTN_ROWS = 16
ATTN_SCALE = HEAD_DIM ** -0.5

NN = (((1,), (0,)), ((), ()))
NT = (((1,), (1,)), ((), ()))
TN = (((0,), (0,)), ((), ()))


def _tile(n, pref):
    return pref if n % pref == 0 else n


def _div_tile(n, pref):
    for cand in range(pref - pref % LANES, 0, -LANES):
        if n % cand == 0:
            return cand
    return n


def _params(sem):
    return pltpu.CompilerParams(dimension_semantics=sem, vmem_limit_bytes=VMEM_LIMIT)


def _dot(a, b, dims):
    return lax.dot_general(a, b, dims, preferred_element_type=F32)


def _mm(name, a, b, *, grid, a_blk, a_map, b_blk, b_map, o_blk, o_map, out_shape, out_dtype, dims,
        res=None, after=()):
    nk = grid[2]
    acc_shape = tuple(d for d in o_blk if d is not None)

    def body(*refs):
        a_ref, b_ref = refs[:2]
        r_ref = None if res is None else refs[2]
        if nk == 1:
            o_ref = refs[-1]
            part = _dot(a_ref[...].astype(BF16), b_ref[...].astype(BF16), dims)
            o_ref[...] = (part if r_ref is None else part + r_ref[...]).astype(out_dtype)
            return
        o_ref, acc = refs[-2:]
        k = pl.program_id(2)

        @pl.when(k == 0)
        def _():
            acc[...] = jnp.zeros_like(acc)

        acc[...] += _dot(a_ref[...].astype(BF16), b_ref[...].astype(BF16), dims)

        @pl.when(k == nk - 1)
        def _():
            r = acc[...]
            if r_ref is not None:
                r = r + r_ref[...]
            o_ref[...] = r.astype(out_dtype)

    in_specs = [pl.BlockSpec(a_blk, a_map), pl.BlockSpec(b_blk, b_map)]
    args = [a, b]
    if res is not None:
        in_specs.append(pl.BlockSpec(o_blk, o_map))
        args.append(res)
    in_specs += [pl.BlockSpec(memory_space=pl.ANY)] * len(after)
    args += list(after)
    return pl.pallas_call(
        body, name=name, grid=grid, in_specs=in_specs, out_specs=pl.BlockSpec(o_blk, o_map),
        out_shape=jax.ShapeDtypeStruct(out_shape, out_dtype),
        scratch_shapes=[] if nk == 1 else [pltpu.VMEM(acc_shape, F32)],
        compiler_params=_params(("parallel", "parallel", "arbitrary")),
    )(*args)


def mm_col_fwd(name, a, wg, out_dtype, split=1):
    m, kdim = a.shape
    n_dev, _, w = wg.shape
    tm, tk = _tile(m, MM_TM), _div_tile(kdim, MM_TK)
    per = n_dev // split
    if split == 1:
        o_blk, o_map, o_shape = (tm, w), (lambda i, j, k: (i, j)), (m, n_dev * w)
    else:
        o_blk, o_map, o_shape = (None, tm, w), (lambda i, j, k: (j // per, i, j % per)), (split, m, per * w)
    return _mm(name, a, wg, grid=(m // tm, n_dev, kdim // tk),
               a_blk=(tm, tk), a_map=lambda i, j, k: (i, k),
               b_blk=(None, tk, w), b_map=lambda i, j, k: (j, k, 0),
               o_blk=o_blk, o_map=o_map, out_shape=o_shape, out_dtype=out_dtype, dims=NN)


def mm_col_dx(name, dy, wg, split=1):
    n_dev, kdim, w = wg.shape
    m = dy.shape[-2]
    tm, tk = _tile(m, MM_TM), _div_tile(kdim, MM_TK)
    per = n_dev // split
    if split == 1:
        a_blk, a_map = (tm, w), (lambda i, j, k: (i, k))
    else:
        a_blk, a_map = (None, tm, w), (lambda i, j, k: (k // per, i, k % per))
    return _mm(name, dy, wg, grid=(m // tm, kdim // tk, n_dev),
               a_blk=a_blk, a_map=a_map,
               b_blk=(None, tk, w), b_map=lambda i, j, k: (k, j, 0),
               o_blk=(tm, tk), o_map=lambda i, j, k: (i, j), out_shape=(m, kdim), out_dtype=F32, dims=NT)


def mm_col_dw(name, x, dy, w, split=1):
    m, kdim = x.shape
    tm, tk = _tile(m, MM_TM), _div_tile(kdim, MM_TK)
    per = N_DEV // split
    if split == 1:
        b_blk, b_map = (tm, w), (lambda i, j, k: (k, j))
    else:
        b_blk, b_map = (None, tm, w), (lambda i, j, k: (j // per, k, j % per))
    return _mm(name, x, dy, grid=(kdim // tk, N_DEV, m // tm),
               a_blk=(tm, tk), a_map=lambda i, j, k: (k, i),
               b_blk=b_blk, b_map=b_map,
               o_blk=(None, tk, w), o_map=lambda i, j, k: (j, i, 0),
               out_shape=(N_DEV, kdim, w), out_dtype=BF16, dims=TN)


def mm_row_fwd(name, a, wg, res):
    m, kdim = a.shape
    n = wg.shape[1]
    tm, tk, tn = _tile(m, MM_TM), _div_tile(kdim, MM_TK), _tile(n, 1024)
    return _mm(name, a, wg, grid=(m // tm, n // tn, kdim // tk),
               a_blk=(tm, tk), a_map=lambda i, j, k: (i, k),
               b_blk=(tk, tn), b_map=lambda i, j, k: (k, j),
               o_blk=(tm, tn), o_map=lambda i, j, k: (i, j), out_shape=(m, n), out_dtype=F32, dims=NN,
               res=res)


def mm_row_dx(name, dy, wg, after=()):
    m, n = dy.shape
    kdim = wg.shape[0]
    tm, tk, tn = _tile(m, MM_TM), _div_tile(kdim, MM_TK), _tile(n, MM_TK)
    return _mm(name, dy, wg, grid=(m // tm, kdim // tk, n // tn),
               a_blk=(tm, tn), a_map=lambda i, j, k: (i, k),
               b_blk=(tk, tn), b_map=lambda i, j, k: (j, k),
               o_blk=(tm, tk), o_map=lambda i, j, k: (i, j), out_shape=(m, kdim), out_dtype=F32, dims=NT,
               after=after)


def mm_row_dw(name, x, dy):
    m, kdim = x.shape
    n = dy.shape[1]
    tm, tk, tn = _tile(m, MM_TM), _div_tile(kdim, MM_TK), _tile(n, 1024)
    return _mm(name, x, dy, grid=(kdim // tk, n // tn, m // tm),
               a_blk=(tm, tk), a_map=lambda i, j, k: (k, i),
               b_blk=(tm, tn), b_map=lambda i, j, k: (k, j),
               o_blk=(tk, tn), o_map=lambda i, j, k: (i, j), out_shape=(kdim, n), out_dtype=BF16, dims=TN)


def _rows(d, tm):
    return pl.BlockSpec((tm, d), lambda i: (i, 0))


def _vec(d):
    return pl.BlockSpec((1, d), lambda i: (0, 0))


def rms_fwd(name, h, gain):
    t, d = h.shape
    tm = _tile(t, ROW_TILE)

    def body(h_ref, g_ref, o_ref):
        x = h_ref[...]
        rstd = lax.rsqrt(jnp.mean(x * x, axis=-1, keepdims=True) + EPS)
        o_ref[...] = (x * rstd * g_ref[...]).astype(BF16)

    return pl.pallas_call(
        body, name=name, grid=(t // tm,), in_specs=[_rows(d, tm), _vec(d)], out_specs=_rows(d, tm),
        out_shape=jax.ShapeDtypeStruct((t, d), BF16), compiler_params=_params(("parallel",)),
    )(h, gain.reshape(1, d))


def rms_bwd(name, h, gain, dy, dres, after=()):
    t, d = h.shape
    tm = _tile(t, ROW_TILE)

    def body(h_ref, g_ref, dy_ref, r_ref, *rest):
        dh_ref, dhb_ref, dg_ref = rest[-3:]

        @pl.when(pl.program_id(0) == 0)
        def _():
            dg_ref[...] = jnp.zeros_like(dg_ref)

        x = h_ref[...]
        rstd = lax.rsqrt(jnp.mean(x * x, axis=-1, keepdims=True) + EPS)
        xhat = x * rstd
        dyv = dy_ref[...]
        dxhat = dyv * g_ref[...]
        dh = r_ref[...] + rstd * (dxhat - xhat * jnp.mean(dxhat * xhat, axis=-1, keepdims=True))
        dh_ref[...] = dh
        dhb_ref[...] = dh.astype(BF16)
        dg_ref[...] += jnp.sum(dyv * xhat, axis=0, keepdims=True)

    return pl.pallas_call(
        body, name=name, grid=(t // tm,),
        in_specs=[_rows(d, tm), _vec(d), _rows(d, tm), _rows(d, tm)]
        + [pl.BlockSpec(memory_space=pl.ANY)] * len(after),
        out_specs=[_rows(d, tm), _rows(d, tm), _vec(d)],
        out_shape=[jax.ShapeDtypeStruct((t, d), F32), jax.ShapeDtypeStruct((t, d), BF16),
                   jax.ShapeDtypeStruct((1, d), F32)],
        compiler_params=_params(("arbitrary",)),
    )(h, gain.reshape(1, d), dy, dres, *after)


def loss_head(name, h, gain, target):
    t, d = h.shape
    tm = _tile(t, ROW_TILE)

    def body(h_ref, g_ref, t_ref, dh_ref, dhb_ref, dg_ref, loss_ref):
        @pl.when(pl.program_id(0) == 0)
        def _():
            dg_ref[...] = jnp.zeros_like(dg_ref)
            loss_ref[...] = jnp.zeros_like(loss_ref)

        x = h_ref[...]
        rstd = lax.rsqrt(jnp.mean(x * x, axis=-1, keepdims=True) + EPS)
        xhat = x * rstd
        err = xhat * g_ref[...] - t_ref[...]
        row = jnp.mean(err * err, axis=-1, keepdims=True)
        loss_ref[...] += 0.5 * jnp.sum(row, axis=0, keepdims=True)
        dyv = err * (1.0 / d)
        dxhat = dyv * g_ref[...]
        dh = rstd * (dxhat - xhat * jnp.mean(dxhat * xhat, axis=-1, keepdims=True))
        dh_ref[...] = dh
        dhb_ref[...] = dh.astype(BF16)
        dg_ref[...] += jnp.sum(dyv * xhat, axis=0, keepdims=True)

    return pl.pallas_call(
        body, name=name, grid=(t // tm,),
        in_specs=[_rows(d, tm), _vec(d), _rows(d, tm)],
        out_specs=[_rows(d, tm), _rows(d, tm), _vec(d), pl.BlockSpec((1, 1), lambda i: (0, 0))],
        out_shape=[jax.ShapeDtypeStruct((t, d), F32), jax.ShapeDtypeStruct((t, d), BF16),
                   jax.ShapeDtypeStruct((1, d), F32), jax.ShapeDtypeStruct((1, 1), F32)],
        compiler_params=_params(("arbitrary",)),
    )(h, gain.reshape(1, d), target)


def rope_tables(seq):
    pos = jnp.arange(seq, dtype=jnp.int32)
    row_ids = (pos // GRID_W).astype(F32)
    col_ids = (pos % GRID_W).astype(F32)
    quarter = HEAD_DIM // 4
    inv_freq = ROPE_THETA ** (-jnp.arange(quarter, dtype=F32) / quarter)
    ar = row_ids[:, None] * inv_freq[None, :]
    ac = col_ids[:, None] * inv_freq[None, :]
    cos = jnp.concatenate([jnp.cos(ar), jnp.cos(ar), jnp.cos(ac), jnp.cos(ac)], axis=-1)
    sin = jnp.concatenate([-jnp.sin(ar), jnp.sin(ar), -jnp.sin(ac), jnp.sin(ac)], axis=-1)
    return cos, sin


def _swap_quarters(x):
    lane = lax.broadcasted_iota(jnp.int32, x.shape, 1)
    q = HEAD_DIM // 4
    return jnp.where((lane % (2 * q)) < q, pltpu.roll(x, HEAD_DIM - q, 1), pltpu.roll(x, q, 1))


def qk_prep_fwd(name, qkv, q_gain, k_gain, cos, sin, n_q, n_kv):
    t, width = qkv.shape
    tm = _tile(t, ROW_TILE)

    def body(x_ref, qg_ref, kg_ref, c_ref, s_ref, o_ref):
        c, s = c_ref[...], s_ref[...]
        for hd in range(n_q + n_kv):
            sl = slice(hd * HEAD_DIM, (hd + 1) * HEAD_DIM)
            x = x_ref[:, sl]
            g = qg_ref[...] if hd < n_q else kg_ref[...]
            xn = x * lax.rsqrt(jnp.mean(x * x, axis=-1, keepdims=True) + EPS) * g
            o_ref[:, sl] = (xn * c + _swap_quarters(xn) * s).astype(BF16)
        vs = slice((n_q + n_kv) * HEAD_DIM, width)
        o_ref[:, vs] = x_ref[:, vs].astype(BF16)

    return pl.pallas_call(
        body, name=name, grid=(t // tm,),
        in_specs=[_rows(width, tm), _vec(HEAD_DIM), _vec(HEAD_DIM), _rows(HEAD_DIM, tm), _rows(HEAD_DIM, tm)],
        out_specs=_rows(width, tm), out_shape=jax.ShapeDtypeStruct((t, width), BF16),
        compiler_params=_params(("parallel",)),
    )(qkv, q_gain.reshape(1, HEAD_DIM), k_gain.reshape(1, HEAD_DIM), cos, sin)


def qk_prep_bwd(name, qkv, dq, dk, dv, q_gain, k_gain, cos, sin, n_q, n_kv):
    t, width = qkv.shape
    tm = _tile(t, ROW_TILE)

    def body(x_ref, dq_ref, dk_ref, dv_ref, qg_ref, kg_ref, c_ref, s_ref, o_ref, dg_ref):
        @pl.when(pl.program_id(0) == 0)
        def _():
            dg_ref[...] = jnp.zeros_like(dg_ref)

        c, s = c_ref[...], s_ref[...]
        dgq = jnp.zeros((1, HEAD_DIM), F32)
        dgk = jnp.zeros((1, HEAD_DIM), F32)
        for hd in range(n_q + n_kv):
            sl = slice(hd * HEAD_DIM, (hd + 1) * HEAD_DIM)
            x = x_ref[:, sl]
            if hd < n_q:
                g, dout = qg_ref[...], dq_ref[:, sl]
            else:
                ks = slice((hd - n_q) * HEAD_DIM, (hd - n_q + 1) * HEAD_DIM)
                g, dout = kg_ref[...], dk_ref[:, ks]
            rstd = lax.rsqrt(jnp.mean(x * x, axis=-1, keepdims=True) + EPS)
            xhat = x * rstd
            dxn = dout * c + _swap_quarters(dout * s)
            part = jnp.sum(dxn * xhat, axis=0, keepdims=True)
            if hd < n_q:
                dgq = dgq + part
            else:
                dgk = dgk + part
            dxhat = dxn * g
            o_ref[:, sl] = (rstd * (dxhat - xhat * jnp.mean(dxhat * xhat, axis=-1, keepdims=True))).astype(BF16)
        o_ref[:, slice((n_q + n_kv) * HEAD_DIM, width)] = dv_ref[...].astype(BF16)
        dg_ref[0:1, :] += dgq
        dg_ref[1:2, :] += dgk

    kvw = n_kv * HEAD_DIM
    return pl.pallas_call(
        body, name=name, grid=(t // tm,),
        in_specs=[_rows(width, tm), _rows(n_q * HEAD_DIM, tm), _rows(kvw, tm), _rows(kvw, tm),
                  _vec(HEAD_DIM), _vec(HEAD_DIM), _rows(HEAD_DIM, tm), _rows(HEAD_DIM, tm)],
        out_specs=[_rows(width, tm), pl.BlockSpec((2, HEAD_DIM), lambda i: (0, 0))],
        out_shape=[jax.ShapeDtypeStruct((t, width), BF16), jax.ShapeDtypeStruct((2, HEAD_DIM), F32)],
        compiler_params=_params(("arbitrary",)),
    )(qkv, dq, dk, dv, q_gain.reshape(1, HEAD_DIM), k_gain.reshape(1, HEAD_DIM), cos, sin)


def _lanes(x, width):
    return jnp.tile(x, (1, width // LANES))


def _hs(hd):
    return slice(hd * HEAD_DIM, (hd + 1) * HEAD_DIM)


def attn_fwd(name, q, k, v, bias, *, grid, q_spec, k_spec, v_spec, b_spec, o_spec, valid, nh, shared_kv,
             bq, bk, o_shape, o_dtype):
    ns = grid[2]

    def body(*refs):
        if bias is None:
            q_ref, k_ref, v_ref, o_ref, lse_ref, m_s, l_s, acc_s = refs
            b_ref = None
        else:
            q_ref, k_ref, v_ref, b_ref, o_ref, lse_ref, m_s, l_s, acc_s = refs
        step = pl.program_id(2)

        @pl.when(step == 0)
        def _():
            m_s[...] = jnp.full_like(m_s, -jnp.inf)
            l_s[...] = jnp.zeros_like(l_s)
            acc_s[...] = jnp.zeros_like(acc_s)

        @pl.when(valid(pl.program_id(1), step))
        def _():
            ones = jnp.ones((bk, HEAD_DIM), BF16)
            v_ones = jnp.concatenate([v_ref[:, _hs(0)], ones], axis=1) if shared_kv else None
            for hd in range(nh):
                kh = _hs(0 if shared_kv else hd)
                s = _dot(q_ref[:, _hs(hd)], k_ref[:, kh], NT)
                p_rows, a_rows = [], []
                for r0 in range(0, bq, ATTN_ROWS):
                    rows = slice(r0, r0 + ATTN_ROWS)
                    z = s[rows] * ATTN_SCALE
                    if b_ref is not None:
                        z = z + b_ref[hd, rows, :]
                    m_prev = m_s[hd, rows, :]
                    m_new = jnp.maximum(m_prev, jnp.max(z, axis=-1, keepdims=True))
                    m_s[hd, rows, :] = m_new
                    p_rows.append(jnp.exp(z - _lanes(m_new, bk)).astype(BF16))
                    a_rows.append(jnp.exp(m_prev - m_new))
                rhs = v_ones if shared_kv else jnp.concatenate([v_ref[:, kh], ones], axis=1)
                pv = _dot(jnp.concatenate(p_rows, axis=0), rhs, NN)
                alpha = jnp.concatenate(a_rows, axis=0)
                acc_s[hd] = alpha * acc_s[hd] + pv[:, :HEAD_DIM]
                l_s[hd] = alpha * l_s[hd] + pv[:, HEAD_DIM:]

        @pl.when(step == ns - 1)
        def _():
            for hd in range(nh):
                o_ref[:, _hs(hd)] = (acc_s[hd] / l_s[hd]).astype(o_dtype)
                lse_ref[:, _hs(hd)] = m_s[hd] + jnp.log(l_s[hd])

    in_specs = [q_spec, k_spec, v_spec] + ([] if bias is None else [b_spec])
    args = [q, k, v] + ([] if bias is None else [bias])
    stat = pltpu.VMEM((nh, bq, LANES), F32)
    return pl.pallas_call(
        body, name=name, grid=grid, in_specs=in_specs, out_specs=[o_spec, o_spec],
        out_shape=[jax.ShapeDtypeStruct(o_shape, o_dtype), jax.ShapeDtypeStruct(o_shape, F32)],
        scratch_shapes=[stat, stat, stat],
        compiler_params=_params(("parallel", "parallel", "arbitrary")),
    )(*args)


def _probs(q_ref, k_ref, v_ref, do_ref, lse_ref, dlt_ref, b_ref, hd, kh, bq, bk, want_p=True, on_ds=None):
    s = _dot(q_ref[:, _hs(hd)], k_ref[:, kh], NT)
    dp = _dot(do_ref[:, _hs(hd)], v_ref[:, kh], NT)
    p_rows, ds_rows = [], []
    for r0 in range(0, bq, ATTN_ROWS):
        rows = slice(r0, r0 + ATTN_ROWS)
        z = s[rows] * ATTN_SCALE
        if b_ref is not None:
            z = z + b_ref[hd, rows, :]
        p = jnp.exp(z - _lanes(lse_ref[rows, _hs(hd)], bk))
        ds = p * (dp[rows] - _lanes(dlt_ref[rows, _hs(hd)], bk))
        if on_ds is not None:
            on_ds(rows, ds)
        if want_p:
            p_rows.append(p.astype(BF16))
        ds_rows.append(ds.astype(BF16))
    return (jnp.concatenate(p_rows, axis=0) if want_p else None), jnp.concatenate(ds_rows, axis=0)


def attn_bwd_dq(name, q, k, v, do, lse, dlt, *, grid, q_spec, k_spec, v_spec, nh, bq, bk, o_shape):
    ns = grid[2]
    scale = HEAD_DIM ** -0.5

    def body(q_ref, k_ref, v_ref, do_ref, lse_ref, dlt_ref, dq_ref, acc_s):
        step = pl.program_id(2)

        @pl.when(step == 0)
        def _():
            acc_s[...] = jnp.zeros_like(acc_s)

        for hd in range(nh):
            _, ds = _probs(q_ref, k_ref, v_ref, do_ref, lse_ref, dlt_ref, None, hd, _hs(0), bq, bk, want_p=False)
            acc_s[hd] += _dot(ds, k_ref[:, _hs(0)], NN)

        @pl.when(step == ns - 1)
        def _():
            for hd in range(nh):
                dq_ref[:, _hs(hd)] = acc_s[hd] * scale

    return pl.pallas_call(
        body, name=name, grid=grid, in_specs=[q_spec, k_spec, v_spec, q_spec, q_spec, q_spec],
        out_specs=q_spec, out_shape=jax.ShapeDtypeStruct(o_shape, F32),
        scratch_shapes=[pltpu.VMEM((nh, bq, LANES), F32)],
        compiler_params=_params(("parallel", "parallel", "arbitrary")),
    )(q, k, v, do, lse, dlt)


def _always(i, s):
    return s >= 0


def row_delta(name, do, o, n_heads):
    t, width = do.shape
    tm = _tile(t, ROW_TILE)

    def body(do_ref, o_ref, dl_ref, dob_ref):
        for hd in range(n_heads):
            d = do_ref[:, _hs(hd)]
            s = jnp.sum(d * o_ref[:, _hs(hd)].astype(F32), axis=-1, keepdims=True)
            dl_ref[:, _hs(hd)] = jnp.broadcast_to(s, (tm, HEAD_DIM))
            dob_ref[:, _hs(hd)] = d.astype(BF16)

    return pl.pallas_call(
        body, name=name, grid=(t // tm,), in_specs=[_rows(width, tm), _rows(width, tm)],
        out_specs=[_rows(width, tm), _rows(width, tm)],
        out_shape=[jax.ShapeDtypeStruct((t, width), F32), jax.ShapeDtypeStruct((t, width), BF16)],
        compiler_params=_params(("parallel",)),
    )(do, o)


def _a_specs(n_q, n_kv, bq, bk, q_major):
    grp = n_q // n_kv
    if q_major:
        qm, km = (lambda b, i, s: (i, b)), (lambda b, i, s: (s, n_q + b))
        vm = lambda b, i, s: (s, n_q + n_kv + b)
    else:
        qm, km = (lambda b, i, s: (s, b)), (lambda b, i, s: (i, n_q + b))
        vm = lambda b, i, s: (i, n_q + n_kv + b)
    return (pl.BlockSpec((bq, grp * HEAD_DIM), qm), pl.BlockSpec((bk, HEAD_DIM), km),
            pl.BlockSpec((bk, HEAD_DIM), vm))


def mixer_a_fwd(qkv_r, n_q, n_kv):
    t = qkv_r.shape[0]
    bq, bk = _tile(t, A_BQ), _tile(t, A_BK)
    q_spec, k_spec, v_spec = _a_specs(n_q, n_kv, bq, bk, True)
    return attn_fwd("a_attn_fwd", qkv_r, qkv_r, qkv_r, None, grid=(n_kv, t // bq, t // bk),
                    q_spec=q_spec, k_spec=k_spec, v_spec=v_spec, b_spec=None, o_spec=q_spec, valid=_always,
                    nh=n_q // n_kv, shared_kv=True, bq=bq, bk=bk, o_shape=(t, n_q * HEAD_DIM), o_dtype=BF16)


def mixer_a_bwd(qkv_r, do_b, lse, dlt, n_q, n_kv):
    t = qkv_r.shape[0]
    bq, bk = _tile(t, A_BQ), _tile(t, A_BK)
    grp = n_q // n_kv
    q_spec, k_spec, v_spec = _a_specs(n_q, n_kv, bq, bk, True)
    dq = attn_bwd_dq("a_attn_dq", qkv_r, qkv_r, qkv_r, do_b, lse, dlt, grid=(n_kv, t // bq, t // bk),
                     q_spec=q_spec, k_spec=k_spec, v_spec=v_spec, nh=grp, bq=bq, bk=bk,
                     o_shape=(t, n_q * HEAD_DIM))
    bk = bq
    q_spec, k_spec, v_spec = _a_specs(n_q, n_kv, bq, bk, False)
    o_spec = pl.BlockSpec((bk, HEAD_DIM), lambda b, i, s: (i, b))
    dk, dv = _attn_bwd_dkv_out(qkv_r, do_b, lse, dlt, grid=(n_kv, t // bk, t // bq), q_spec=q_spec,
                               k_spec=k_spec, v_spec=v_spec, o_spec=o_spec, grp=grp, bq=bq, bk=bk,
                               o_shape=(t, n_kv * HEAD_DIM))
    return dq, dk, dv


def _attn_bwd_dkv_out(qkv_r, do_b, lse, dlt, *, grid, q_spec, k_spec, v_spec, o_spec, grp, bq, bk, o_shape):
    ns = grid[2]
    scale = HEAD_DIM ** -0.5

    def body(q_ref, k_ref, v_ref, do_ref, lse_ref, dlt_ref, dk_ref, dv_ref, dk_s, dv_s):
        step = pl.program_id(2)

        @pl.when(step == 0)
        def _():
            dk_s[...] = jnp.zeros_like(dk_s)
            dv_s[...] = jnp.zeros_like(dv_s)

        for hd in range(grp):
            p, ds = _probs(q_ref, k_ref, v_ref, do_ref, lse_ref, dlt_ref, None, hd, _hs(0), bq, bk)
            dv_s[...] += _dot(p, do_ref[:, _hs(hd)], TN)
            dk_s[...] += _dot(ds, q_ref[:, _hs(hd)], TN)

        @pl.when(step == ns - 1)
        def _():
            dk_ref[...] = dk_s[...] * scale
            dv_ref[...] = dv_s[...]

    acc = pltpu.VMEM((bk, HEAD_DIM), F32)
    return pl.pallas_call(
        body, name="a_attn_dkv", grid=grid, in_specs=[q_spec, k_spec, v_spec, q_spec, q_spec, q_spec],
        out_specs=[o_spec, o_spec], out_shape=[jax.ShapeDtypeStruct(o_shape, F32)] * 2,
        scratch_shapes=[acc, acc], compiler_params=_params(("parallel", "parallel", "arbitrary")),
    )(qkv_r, qkv_r, qkv_r, do_b, lse, dlt)


def t5_bucket(rel):
    nb = REL_BUCKETS // 2
    max_exact = nb // 2
    base = jnp.where(rel > 0, nb, 0)
    n = jnp.abs(rel)
    nf = jnp.maximum(n, 1).astype(F32)
    large = max_exact + (jnp.log(nf / max_exact) / math.log(REL_MAX_DISTANCE / max_exact)
                         * (nb - max_exact)).astype(jnp.int32)
    large = jnp.minimum(large, nb - 1)
    return base + jnp.where(n < max_exact, n, large)


def band_stride(t, win, dil):
    return 1 if t % B_BQ == 0 and win // 2 <= B_BQ else dil


def band_tables(rel_bias_g, win, dil, stride, bq):
    a = jnp.arange(bq)[:, None]
    b = jnp.arange(bq)[None, :]
    rel = jnp.stack([(s - 1) * bq + b - a for s in range(3)]) * stride
    ok = (jnp.abs(rel) <= win // 2) & (rel % dil == 0)
    bucket = t5_bucket(rel)
    bias = jnp.zeros((rel_bias_g.shape[1],) + rel.shape, F32)
    for r in range(REL_BUCKETS):
        bias = bias + jnp.where(bucket[None] == r, rel_bias_g[r][:, None, None, None], 0.0)
    return jnp.where(ok[None], bias, NEG_INF), jnp.where(ok, bucket, -1).astype(jnp.int32)


def band_block(t, stride):
    return _tile(t // stride, B_BQ)


def _b_geometry(t, dil, g, n_groups, bq):
    hg = B_HEADS_PER_GROUP
    length = t // dil
    nblk = length // bq
    gw = hg * HEAD_DIM
    per_tok = 3 * n_groups
    return hg, length, bq, nblk, gw, per_tok


def mixer_b_group_fwd(qkv, bias, dil, g, n_groups, tag):
    t = qkv.shape[0]
    hg, length, bq, nblk, gw, per_tok = _b_geometry(t, dil, g, n_groups, bias.shape[2])
    if dil > 1:
        qkv, g, per_tok = qkv[:, 3 * g * gw:3 * (g + 1) * gw], 0, 3
    view = qkv.reshape(length, dil * qkv.shape[1])
    col = lambda c, which: c * per_tok + 3 * g + which
    kblk = lambda i, s: jnp.clip(i - 1 + s, 0, nblk - 1)
    spec = lambda which, streamed: pl.BlockSpec(
        (bq, gw), (lambda c, i, s: (kblk(i, s), col(c, which))) if streamed else (lambda c, i, s: (i, col(c, which))))
    valid = lambda i, s: (i - 1 + s >= 0) & (i - 1 + s < nblk)
    o, lz = attn_fwd(f"b_attn_fwd_d{tag}", view, view, view, bias, grid=(dil, nblk, 3),
                     q_spec=spec(0, False), k_spec=spec(1, True), v_spec=spec(2, True),
                     b_spec=pl.BlockSpec((hg, None, bq, bq), lambda c, i, s: (0, s, 0, 0)),
                     o_spec=pl.BlockSpec((bq, gw), lambda c, i, s: (i, c)), valid=valid, nh=hg,
                     shared_kv=False, bq=bq, bk=bq, o_shape=(length, dil * gw), o_dtype=F32)
    return o.reshape(t, gw), lz.reshape(t, gw)


def mixer_b_group_bwd(qkv, bias, do_g, lz_g, dlt_g, dil, g, n_groups, tag):
    t = qkv.shape[0]
    hg, length, bq, nblk, gw, per_tok = _b_geometry(t, dil, g, n_groups, bias.shape[2])
    if dil > 1:
        qkv, g, per_tok = qkv[:, 3 * g * gw:3 * (g + 1) * gw], 0, 3
    view = qkv.reshape(length, dil * qkv.shape[1])
    dov, lzv, dlv = (x.reshape(length, dil * gw) for x in (do_g, lz_g, dlt_g))
    col = lambda c, which: c * per_tok + 3 * g + which
    nbr = lambda i, s: jnp.clip(i - 1 + s, 0, nblk - 1)
    valid = lambda i, s: (i - 1 + s >= 0) & (i - 1 + s < nblk)
    q_spec = pl.BlockSpec((bq, gw), lambda c, i, s: (i, col(c, 0)))
    k_spec = pl.BlockSpec((bq, gw), lambda c, i, s: (nbr(i, s), col(c, 1)))
    v_spec = pl.BlockSpec((bq, gw), lambda c, i, s: (nbr(i, s), col(c, 2)))
    stat = pl.BlockSpec((bq, gw), lambda c, i, s: (i, c))
    dq, dbias = _band_bwd_dq(f"b_attn_dq_d{tag}", view, dov, lzv, dlv, bias, grid=(dil, nblk, 3),
                             q_spec=q_spec, k_spec=k_spec, v_spec=v_spec, stat_spec=stat,
                             b_spec=pl.BlockSpec((hg, None, bq, bq), lambda c, i, s: (0, s, 0, 0)),
                             valid=valid, nh=hg, bq=bq, o_shape=(length, dil * gw))
    q_spec = pl.BlockSpec((bq, gw), lambda c, i, s: (nbr(i, s), col(c, 0)))
    k_spec = pl.BlockSpec((bq, gw), lambda c, i, s: (i, col(c, 1)))
    v_spec = pl.BlockSpec((bq, gw), lambda c, i, s: (i, col(c, 2)))
    stat = pl.BlockSpec((bq, gw), lambda c, i, s: (nbr(i, s), c))
    dk, dv = _band_bwd_dkv(f"b_attn_dkv_d{tag}", view, dov, lzv, dlv, bias, grid=(dil, nblk, 3),
                           q_spec=q_spec, k_spec=k_spec, v_spec=v_spec, stat_spec=stat,
                           b_spec=pl.BlockSpec((hg, None, bq, bq), lambda c, i, s: (0, 2 - s, 0, 0)),
                           o_spec=pl.BlockSpec((bq, gw), lambda c, i, s: (i, c)),
                           valid=valid, nh=hg, bq=bq, o_shape=(length, dil * gw))
    return dq.reshape(t, gw), dk.reshape(t, gw), dv.reshape(t, gw), dbias


def _band_bwd_dq(name, view, do, lse, dlt, bias, *, grid, q_spec, k_spec, v_spec, stat_spec, b_spec, valid,
                 nh, bq, o_shape):
    scale = HEAD_DIM ** -0.5
    bias_shape = (nh, 3, bq, bq)

    def body(q_ref, k_ref, v_ref, do_ref, lse_ref, dlt_ref, b_ref, dq_ref, db_ref, acc_s):
        step = pl.program_id(2)

        @pl.when((pl.program_id(0) == 0) & (pl.program_id(1) == 0) & (step == 0))
        def _():
            db_ref[...] = jnp.zeros_like(db_ref)

        @pl.when(step == 0)
        def _():
            acc_s[...] = jnp.zeros_like(acc_s)

        @pl.when(valid(pl.program_id(1), step))
        def _():
            for hd in range(nh):
                def add_bias_grad(rows, ds, hd=hd):
                    db_ref[hd, step, rows, :] += ds

                _, ds = _probs(q_ref, k_ref, v_ref, do_ref, lse_ref, dlt_ref, b_ref, hd, _hs(hd), bq, bq,
                               want_p=False, on_ds=add_bias_grad)
                acc_s[hd] += _dot(ds, k_ref[:, _hs(hd)], NN)

        @pl.when(step == 2)
        def _():
            for hd in range(nh):
                dq_ref[:, _hs(hd)] = (acc_s[hd] * scale).astype(BF16)

    return pl.pallas_call(
        body, name=name, grid=grid,
        in_specs=[q_spec, k_spec, v_spec, stat_spec, stat_spec, stat_spec, b_spec],
        out_specs=[stat_spec, pl.BlockSpec(bias_shape, lambda c, i, s: (0, 0, 0, 0))],
        out_shape=[jax.ShapeDtypeStruct(o_shape, BF16), jax.ShapeDtypeStruct(bias_shape, F32)],
        scratch_shapes=[pltpu.VMEM((nh, bq, LANES), F32)], compiler_params=_params(("arbitrary",) * 3),
    )(view, view, view, do, lse, dlt, bias)


def _band_bwd_dkv(name, view, do, lse, dlt, bias, *, grid, q_spec, k_spec, v_spec, stat_spec, b_spec, o_spec,
                  valid, nh, bq, o_shape):
    scale = HEAD_DIM ** -0.5

    def body(q_ref, k_ref, v_ref, do_ref, lse_ref, dlt_ref, b_ref, dk_ref, dv_ref, dk_s, dv_s):
        step = pl.program_id(2)

        @pl.when(step == 0)
        def _():
            dk_s[...] = jnp.zeros_like(dk_s)
            dv_s[...] = jnp.zeros_like(dv_s)

        @pl.when(valid(pl.program_id(1), step))
        def _():
            for hd in range(nh):
                p, ds = _probs(q_ref, k_ref, v_ref, do_ref, lse_ref, dlt_ref, b_ref, hd, _hs(hd), bq, bq)
                dv_s[hd] += _dot(p, do_ref[:, _hs(hd)], TN)
                dk_s[hd] += _dot(ds, q_ref[:, _hs(hd)], TN)

        @pl.when(step == 2)
        def _():
            for hd in range(nh):
                dk_ref[:, _hs(hd)] = (dk_s[hd] * scale).astype(BF16)
                dv_ref[:, _hs(hd)] = dv_s[hd].astype(BF16)

    acc = pltpu.VMEM((nh, bq, LANES), F32)
    return pl.pallas_call(
        body, name=name, grid=grid,
        in_specs=[q_spec, k_spec, v_spec, stat_spec, stat_spec, stat_spec, b_spec],
        out_specs=[o_spec, o_spec], out_shape=[jax.ShapeDtypeStruct(o_shape, BF16)] * 2,
        scratch_shapes=[acc, acc], compiler_params=_params(("parallel", "parallel", "arbitrary")),
    )(view, view, view, do, lse, dlt, bias)


def bias_bucket_sums(name, dbias, bucket):
    nh, _, bq, _ = dbias.shape
    db2 = dbias.reshape(nh, 3 * bq, bq)
    bk2 = bucket.reshape(3 * bq, bq)

    def body(db_ref, bk_ref, o_ref):
        row = lax.broadcasted_iota(jnp.int32, (nh, LANES), 0)
        lane = lax.broadcasted_iota(jnp.int32, (nh, LANES), 1)
        out = jnp.zeros((nh, LANES), F32)
        bkt = bk_ref[...]
        for hd in range(nh):
            x = db_ref[hd]
            for r in range(REL_BUCKETS):
                part = jnp.sum(jnp.where(bkt == r, x, 0.0), axis=1, keepdims=True)
                tot = jnp.sum(part, axis=0, keepdims=True)
                out = out + jnp.where((row == hd) & (lane == r), tot, 0.0)
        o_ref[...] = out

    return pl.pallas_call(
        body, name=name, out_shape=jax.ShapeDtypeStruct((nh, LANES), F32),
        compiler_params=pltpu.CompilerParams(vmem_limit_bytes=VMEM_LIMIT),
    )(db2, bk2)


def combine_fwd(name, outs, lzs):
    n_g = len(outs)
    t, gw = outs[0].shape
    tm = _tile(t, ROW_TILE)

    def body(*refs):
        o_refs, lz_refs, y_ref = refs[:n_g], refs[n_g:2 * n_g], refs[2 * n_g]
        lz = [r[...] for r in lz_refs]
        mx = functools.reduce(jnp.maximum, lz)
        e = [jnp.exp(x - mx) for x in lz]
        den = functools.reduce(lambda a, b: a + b, e)
        for g in range(n_g):
            y_ref[:, g * gw:(g + 1) * gw] = (e[g] / den * o_refs[g][...]).astype(BF16)

    return pl.pallas_call(
        body, name=name, grid=(t // tm,), in_specs=[_rows(gw, tm)] * (2 * n_g), out_specs=_rows(n_g * gw, tm),
        out_shape=jax.ShapeDtypeStruct((t, n_g * gw), BF16), compiler_params=_params(("parallel",)),
    )(*outs, *lzs)


def combine_bwd(name, dy, outs, lzs):
    n_g = len(outs)
    t, gw = outs[0].shape
    tm = _tile(t, ROW_TILE)
    nh = gw // HEAD_DIM

    def body(*refs):
        dy_ref = refs[0]
        o_refs, lz_refs = refs[1:1 + n_g], refs[1 + n_g:1 + 2 * n_g]
        do_refs, dl_refs = refs[1 + 2 * n_g:1 + 3 * n_g], refs[1 + 3 * n_g:]
        lz = [r[...] for r in lz_refs]
        mx = functools.reduce(jnp.maximum, lz)
        e = [jnp.exp(x - mx) for x in lz]
        den = functools.reduce(lambda a, b: a + b, e)
        wts = [x / den for x in e]
        for g in range(n_g):
            do_refs[g][...] = (wts[g] * dy_ref[:, g * gw:(g + 1) * gw]).astype(BF16)
        for hd in range(nh):
            mix = jnp.zeros((tm, HEAD_DIM), F32)
            for g in range(n_g):
                prod = dy_ref[:, g * gw + hd * HEAD_DIM:g * gw + (hd + 1) * HEAD_DIM] * o_refs[g][:, _hs(hd)]
                dw = jnp.broadcast_to(jnp.sum(prod, axis=-1, keepdims=True), (tm, HEAD_DIM))
                mix = mix + wts[g][:, _hs(hd)] * dw
            for g in range(n_g):
                dl_refs[g][:, _hs(hd)] = wts[g][:, _hs(hd)] * mix

    return pl.pallas_call(
        body, name=name, grid=(t // tm,),
        in_specs=[_rows(n_g * gw, tm)] + [_rows(gw, tm)] * (2 * n_g),
        out_specs=[_rows(gw, tm)] * (2 * n_g),
        out_shape=[jax.ShapeDtypeStruct((t, gw), BF16)] * n_g + [jax.ShapeDtypeStruct((t, gw), F32)] * n_g,
        compiler_params=_params(("parallel",)),
    )(dy, *outs, *lzs)


def _shifted(u):
    t = u.shape[0]
    row = lax.broadcasted_iota(jnp.int32, u.shape, 0)
    prev = jnp.where(row == 0, 0.0, pltpu.roll(u, 1, 0))
    nxt = jnp.where(row == t - 1, 0.0, pltpu.roll(u, t - 1, 0))
    return prev, nxt


def _conv3(u, prev, nxt, w_ref, b):
    return w_ref[0:1, :] * prev + w_ref[1:2, :] * u + w_ref[2:3, :] * nxt + b


def _conv3_t(d, w_ref):
    prev, nxt = _shifted(d)
    return w_ref[0:1, :] * nxt + w_ref[1:2, :] * d + w_ref[2:3, :] * prev


def conv_act_fwd(name, u2, cw2, cb2):
    _, t, dff = u2.shape
    tn = LANES

    def body(u_ref, w_ref, b_ref, o_ref):
        ug, uv = u_ref[0], u_ref[1]
        cg = _conv3(ug, *_shifted(ug), w_ref.at[0], b_ref[0])
        cv = _conv3(uv, *_shifted(uv), w_ref.at[1], b_ref[1])
        o_ref[...] = (cg * jax.nn.sigmoid(cg) * cv).astype(BF16)

    return pl.pallas_call(
        body, name=name, grid=(dff // tn,),
        in_specs=[pl.BlockSpec((2, t, tn), lambda j: (0, 0, j)), pl.BlockSpec((2, 3, tn), lambda j: (0, 0, j)),
                  pl.BlockSpec((2, 1, tn), lambda j: (0, 0, j))],
        out_specs=pl.BlockSpec((t, tn), lambda j: (0, j)), out_shape=jax.ShapeDtypeStruct((t, dff), BF16),
        compiler_params=_params(("parallel",)),
    )(u2, cw2, cb2)


def conv_act_bwd(name, u2, cw2, cb2, dact, after=()):
    _, t, dff = u2.shape
    tn = LANES

    def body(u_ref, w_ref, b_ref, d_ref, *rest):
        du_ref, dw_ref = rest[-2:]
        d = d_ref[...]
        ug, uv = u_ref[0], u_ref[1]
        shifted = (_shifted(ug), _shifted(uv))
        cg = _conv3(ug, *shifted[0], w_ref.at[0], b_ref[0])
        cv = _conv3(uv, *shifted[1], w_ref.at[1], b_ref[1])
        sg = jax.nn.sigmoid(cg)
        dcv = d * (cg * sg)
        dcg = d * cv * (sg * (1.0 + cg * (1.0 - sg)))
        du_ref[0] = _conv3_t(dcg, w_ref.at[0]).astype(BF16)
        du_ref[1] = _conv3_t(dcv, w_ref.at[1]).astype(BF16)
        for half, (dc, u) in enumerate(((dcg, ug), (dcv, uv))):
            prev, nxt = shifted[half]
            for tap, x in enumerate((prev, u, nxt)):
                dw_ref[half, tap:tap + 1, :] = jnp.sum(dc * x, axis=0, keepdims=True)
            dw_ref[half, 3:4, :] = jnp.sum(dc, axis=0, keepdims=True)
            dw_ref[half, 4:8, :] = jnp.zeros((4, tn), F32)

    return pl.pallas_call(
        body, name=name, grid=(dff // tn,),
        in_specs=[pl.BlockSpec((2, t, tn), lambda j: (0, 0, j)), pl.BlockSpec((2, 3, tn), lambda j: (0, 0, j)),
                  pl.BlockSpec((2, 1, tn), lambda j: (0, 0, j)), pl.BlockSpec((t, tn), lambda j: (0, j))]
        + [pl.BlockSpec(memory_space=pl.ANY)] * len(after),
        out_specs=[pl.BlockSpec((2, t, tn), lambda j: (0, 0, j)), pl.BlockSpec((2, 8, tn), lambda j: (0, 0, j))],
        out_shape=[jax.ShapeDtypeStruct((2, t, dff), BF16), jax.ShapeDtypeStruct((2, 8, dff), F32)],
        compiler_params=_params(("parallel",)),
    )(u2, cw2, cb2, dact, *after)


GATHER_ID, SIBLING_ID, CHIPS_ID = 0, 1, 2


def _place():
    x, y, c = lax.axis_index("x"), lax.axis_index("y"), lax.axis_index("c")
    chips = [(1 - x, y), (x, 1 - y), (1 - x, 1 - y)]
    return x, y, c, chips


def _handshake(peers):
    barrier = pltpu.get_barrier_semaphore()
    for peer in peers:
        pl.semaphore_signal(barrier, inc=1, device_id=peer, device_id_type=MESH)
    pl.semaphore_wait(barrier, len(peers))


def _sequencer(name, body, out_type, scratch_types, collective_id):
    return pl.kernel(body, out_type=out_type, mesh=plsc.ScalarSubcoreMesh(axis_name="seq", num_cores=1),
                     scratch_types=scratch_types, name=name,
                     compiler_params=pltpu.CompilerParams(collective_id=collective_id))


def _gather_body(n):
    def body(*refs):
        src, out = refs[:n], refs[n:2 * n]
        send, recv, loc = refs[2 * n:]
        x, y, c, chips = _place()
        sibling = (x, y, 1 - c)
        _handshake([sibling] + [(*chip, c) for chip in chips])

        def slot(a, px, py, pc):
            return out[a].at[4 * px + 2 * py + pc]

        def copy(a, k, block, to, from_src=False):
            return pltpu.make_async_remote_copy(
                src_ref=src[a] if from_src else slot(a, *block), dst_ref=slot(a, *block),
                send_sem=send.at[a, k], recv_sem=recv.at[a, k], device_id=to, device_id_type=MESH)

        mine = [pltpu.make_async_copy(src[a], slot(a, x, y, c), loc.at[a]) for a in range(n)]
        for cp in mine:
            cp.start()
        first = []
        for a in range(n):
            first.append(copy(a, 0, (x, y, c), sibling, True))
            first += [copy(a, 1 + j, (x, y, c), (*chip, c), True) for j, chip in enumerate(chips)]
        for cp in first:
            cp.start()
        passed = []
        for j, chip in enumerate(chips):
            for a in range(n):
                copy(a, 1 + j, (*chip, c), (x, y, c)).wait_recv()
                cp = copy(a, 4 + j, (*chip, c), sibling)
                cp.start()
                passed.append(cp)
        for a in range(n):
            copy(a, 0, sibling, (x, y, c)).wait_recv()
            for j, chip in enumerate(chips):
                copy(a, 4 + j, (*chip, 1 - c), (x, y, c)).wait_recv()
        for cp in first + passed:
            cp.wait_send()
        for cp in mine:
            cp.wait()

    return body


def gather_layer(name, shards):
    n = len(shards)
    out_type = [jax.ShapeDtypeStruct((N_DEV,) + s.shape, s.dtype) for s in shards]
    scratch = [pltpu.SemaphoreType.DMA((n, 7)), pltpu.SemaphoreType.DMA((n, 7)), pltpu.SemaphoreType.DMA((n,))]
    return _sequencer(name, _gather_body(n), out_type, scratch, GATHER_ID)(*shards)


def _to_sibling_body(n):
    def body(*refs):
        src, got = refs[:n], refs[n:2 * n]
        send, recv = refs[2 * n:]
        x, y, c, _ = _place()
        sibling = (x, y, 1 - c)
        _handshake([sibling])
        remote = []
        for a in range(n):
            for q in range(4):
                remote.append(pltpu.make_async_remote_copy(
                    src_ref=src[a].at[2 * q + 1 - c], dst_ref=got[a].at[q], send_sem=send.at[a, q],
                    recv_sem=recv.at[a, q], device_id=sibling, device_id_type=MESH))
        for cp in remote:
            cp.start()
        for cp in remote:
            cp.wait()

    return body


def grads_to_sibling(name, grads):
    n = len(grads)
    out_type = [jax.ShapeDtypeStruct((4,) + g.shape[1:], g.dtype) for g in grads]
    scratch = [pltpu.SemaphoreType.DMA((n, 4)), pltpu.SemaphoreType.DMA((n, 4))]
    return _sequencer(name, _to_sibling_body(n), out_type, scratch, SIBLING_ID)(*grads)


def _to_chips_body(n):
    def body(*refs):
        src, got = refs[:n], refs[n:2 * n]
        send, recv = refs[2 * n:]
        x, y, c, chips = _place()
        _handshake([(*chip, c) for chip in chips])
        remote = []
        for a in range(n):
            for j, (px, py) in enumerate(chips):
                remote.append(pltpu.make_async_remote_copy(
                    src_ref=src[a].at[2 * px + py], dst_ref=got[a].at[j], send_sem=send.at[a, j],
                    recv_sem=recv.at[a, j], device_id=(px, py, c), device_id_type=MESH))
        for cp in remote:
            cp.start()
        for cp in remote:
            cp.wait()

    return body


def grads_to_chips(name, parts):
    n = len(parts)
    out_type = [jax.ShapeDtypeStruct((3,) + p.shape[1:], p.dtype) for p in parts]
    scratch = [pltpu.SemaphoreType.DMA((n, 3)), pltpu.SemaphoreType.DMA((n, 3))]
    return _sequencer(name, _to_chips_body(n), out_type, scratch, CHIPS_ID)(*parts)


def all_reduce_small(name, vec):
    rows, m = vec.shape

    def body(x_ref, o_ref, buf, send, recv):
        x, y, c, chips = _place()
        sibling = (x, y, 1 - c)

        def blk(px, py, pc):
            return buf.at[pl.ds(pl.multiple_of((4 * px + 2 * py + pc) * rows, rows), rows), :]

        def copy(k, block, to):
            return pltpu.make_async_remote_copy(src_ref=blk(*block), dst_ref=blk(*block), send_sem=send.at[k],
                                                recv_sem=recv.at[k], device_id=to, device_id_type=MESH)

        blk(x, y, c)[...] = x_ref[...]
        first = [copy(0, (x, y, c), sibling)] + [copy(1 + j, (x, y, c), (*chip, c)) for j, chip in enumerate(chips)]
        for cp in first:
            cp.start()
        passed = [copy(4 + j, (*chip, c), sibling) for j, chip in enumerate(chips)]
        for j, chip in enumerate(chips):
            copy(1 + j, (*chip, c), (x, y, c)).wait_recv()
            passed[j].start()
        copy(0, sibling, (x, y, c)).wait_recv()
        for j, chip in enumerate(chips):
            copy(4 + j, (*chip, 1 - c), (x, y, c)).wait_recv()
        for cp in first + passed:
            cp.wait_send()
        tot = buf[0:rows, :]
        for dev in range(1, N_DEV):
            tot = tot + buf[dev * rows:(dev + 1) * rows, :]
        o_ref[...] = tot

    return pl.pallas_call(
        body, name=name, in_specs=[pl.BlockSpec(memory_space=pltpu.VMEM)],
        out_specs=pl.BlockSpec(memory_space=pltpu.VMEM), out_shape=jax.ShapeDtypeStruct((rows, m), F32),
        scratch_shapes=[pltpu.VMEM((N_DEV * rows, m), F32), pltpu.SemaphoreType.DMA((7,)),
                        pltpu.SemaphoreType.DMA((7,))],
        compiler_params=pltpu.CompilerParams(vmem_limit_bytes=VMEM_LIMIT),
    )(vec)


def _ew_tiles(rows, cols, max_elems=1 << 18):
    tr = rows
    for cand in (1024, 512, 256, 128, 64, 32, 16):
        if rows % cand == 0 and cand * cols <= max_elems:
            tr = cand
            break
    return tr


def chip_sum(name, full, got, core):
    _, kdim, ncol = full.shape
    tr = _ew_tiles(kdim, ncol, max_elems=1 << 20)
    blk = (None, tr, ncol)
    by_chip = pl.BlockSpec(blk, lambda q, i, c: (q, i, 0))

    def body(c_ref, a_ref, b_ref, o_ref):
        o_ref[...] = (a_ref[...].astype(F32) + b_ref[...].astype(F32)).astype(BF16)

    return pl.pallas_call(
        body, name=name,
        grid_spec=pltpu.PrefetchScalarGridSpec(
            num_scalar_prefetch=1, grid=(4, kdim // tr),
            in_specs=[pl.BlockSpec(blk, lambda q, i, c: (2 * q + c[0], i, 0)), by_chip], out_specs=by_chip),
        out_shape=jax.ShapeDtypeStruct((4, kdim, ncol), BF16),
        compiler_params=_params(("parallel", "parallel")),
    )(core, full, got)


def _adamw_math(w, g, m, v):
    m = ADAM_B1 * m + (1.0 - ADAM_B1) * g
    v = ADAM_B2 * v + (1.0 - ADAM_B2) * (g * g)
    m_hat = m / (1.0 - ADAM_B1 ** ADAM_STEP)
    v_hat = v / (1.0 - ADAM_B2 ** ADAM_STEP)
    delta = -ADAM_LR * (m_hat / (jnp.sqrt(v_hat) + ADAM_EPS) + ADAM_WD * w)
    return delta, m, v


def adamw_layer(name, sums, got, w, m, v, outs, layer, chip):
    _, kdim, ncol = sums.shape
    tr = _ew_tiles(kdim, ncol)
    mine = pl.BlockSpec((None, tr, ncol), lambda i, q: (q[0], i, 0))
    others = pl.BlockSpec((3, tr, ncol), lambda i, q: (0, i, 0))
    param = pl.BlockSpec((None, tr, ncol), lambda i, q: (layer, i, 0))
    whole = pl.BlockSpec(memory_space=pl.ANY)

    def body(q_ref, o_ref, g_ref, w_ref, m_ref, v_ref, *rest):
        go_ref, d_ref, mo_ref, vo_ref = rest[-4:]
        g = o_ref[...].astype(F32)
        for j in range(3):
            g = g + g_ref[j].astype(F32)
        d, mn, vn = _adamw_math(w_ref[...], g, m_ref[...], v_ref[...])
        go_ref[...] = g
        d_ref[...] = d
        mo_ref[...] = mn
        vo_ref[...] = vn

    n_in = 6
    return pl.pallas_call(
        body, name=name,
        grid_spec=pltpu.PrefetchScalarGridSpec(
            num_scalar_prefetch=1, grid=(kdim // tr,),
            in_specs=[mine, others, param, param, param] + [whole] * 4, out_specs=[param] * 4),
        out_shape=[jax.ShapeDtypeStruct(w.shape, F32)] * 4,
        input_output_aliases={n_in + k: k for k in range(4)},
        compiler_params=_params(("parallel",)),
    )(chip, sums, got, w, m, v, *outs)


def adamw_small(name, g, w, m, v):
    def body(g_ref, w_ref, m_ref, v_ref, d_ref, mo_ref, vo_ref):
        d, mn, vn = _adamw_math(w_ref[...], g_ref[...], m_ref[...], v_ref[...])
        d_ref[...] = d
        mo_ref[...] = mn
        vo_ref[...] = vn

    vm = pl.BlockSpec(memory_space=pltpu.VMEM)
    return pl.pallas_call(
        body, name=name, in_specs=[vm] * 4, out_specs=[vm] * 3,
        out_shape=[jax.ShapeDtypeStruct(g.shape, F32)] * 3,
        compiler_params=pltpu.CompilerParams(vmem_limit_bytes=VMEM_LIMIT),
    )(g, w, m, v)


def _pack(parts, width):
    flat = jnp.concatenate([p.reshape(-1).astype(F32) for p in parts])
    pad = (-flat.shape[0]) % width
    return jnp.pad(flat, (0, pad)).reshape(-1, width) if pad else flat.reshape(-1, width)


def _unpack(packed, shapes):
    flat = packed.reshape(-1)
    out, off = [], 0
    for s in shapes:
        size = math.prod(s)
        out.append(flat[off:off + size].reshape(s))
        off += size
    return out


def _local_step(h, target, layers, params, on_grads=None):
    a_q_gain, a_k_gain, rel_bias, mix_norm, ffn_norm, conv_b, final_norm = params
    t, d = h.shape
    depth = len(layers)
    n_groups = len(B_GROUPS)
    hg = B_HEADS_PER_GROUP
    n_kv = A_KV_HEADS
    w_a, w_b, w_u = layers[0][0].shape[2], layers[1][0].shape[2], layers[0][2].shape[2]
    n_q = w_a * N_DEV // HEAD_DIM - 2 * n_kv
    dff = layers[0][3].shape[0]
    n_a = (depth + 1) // 2
    cb_full = conv_b.reshape(depth, 2, 1, dff)

    cos, sin = rope_tables(t)
    strides = [band_stride(t, win, dil) for win, dil in B_GROUPS]
    tables = [band_tables(rel_bias[:, g * hg:(g + 1) * hg], win, dil, strides[g], band_block(t, strides[g]))
              for g, (win, dil) in enumerate(B_GROUPS)]

    saved = []
    for i in range(depth):
        j = i // 2
        w_qkv, w_o, w_up_i, w_down_i, cw = layers[i]
        s = {"h_in": h}
        hn = rms_fwd("mix_norm_fwd", h, mix_norm[i])
        s["hn"] = hn
        if i % 2 == 0:
            qkv = mm_col_fwd("a_qkv_fwd", hn, w_qkv, F32)
            qkv_r = qk_prep_fwd("a_qk_prep_fwd", qkv, a_q_gain[j], a_k_gain[j], cos, sin, n_q, n_kv)
            o, lse = mixer_a_fwd(qkv_r, n_q, n_kv)
            s.update(qkv=qkv, qkv_r=qkv_r, o=o, lse=lse)
            h = mm_row_fwd("a_out_fwd", o, w_o, h)
        else:
            qkv = mm_col_fwd("b_qkv_fwd", hn, w_qkv, BF16)
            outs, lzs = [], []
            for g, (win, dil) in enumerate(B_GROUPS):
                o_g, lz_g = mixer_b_group_fwd(qkv, tables[g][0], strides[g], g, n_groups, dil)
                outs.append(o_g)
                lzs.append(lz_g)
            y = combine_fwd("b_combine_fwd", outs, lzs)
            s.update(qkv=qkv, outs=outs, lzs=lzs, y=y)
            h = mm_row_fwd("b_out_fwd", y, w_o, h)
        s["h_mid"] = h
        hn2 = rms_fwd("ffn_norm_fwd", h, ffn_norm[i])
        u2 = mm_col_fwd("ffn_up_fwd", hn2, w_up_i, F32, split=2)
        act = conv_act_fwd("ffn_conv_act_fwd", u2, cw, cb_full[i])
        s.update(hn2=hn2, u2=u2, act=act)
        h = mm_row_fwd("ffn_down_fwd", act, w_down_i, h)
        saved.append(s)

    dh, dh_b, d_final, loss_part = loss_head("loss_head", h, final_norm, target)

    d_mix, d_ffn, d_cw, d_cb = [None] * depth, [None] * depth, [None] * depth, [None] * depth
    d_qg, d_kg = [None] * n_a, [None] * n_a
    d_rel = jnp.zeros((n_groups * hg, LANES), F32)
    layer_grads = [{} for _ in range(depth)]
    pending = []

    def settle():
        done = []
        while pending:
            i_p, part_p, finish = pending.pop()
            layer_grads[i_p][part_p] = finish()
            done += [upd[0] for upd in layer_grads[i_p][part_p]]
        return done

    early = []

    def register(i_p, part_p, grads):
        if on_grads is None:
            layer_grads[i_p][part_p] = grads
        else:
            first, finish = on_grads(i_p, part_p, grads)
            early.extend(first)
            pending.append((i_p, part_p, finish))

    def take_early():
        first = tuple(early)
        early.clear()
        return first

    for i in reversed(range(depth)):
        j = i // 2
        w_qkv, w_o, w_up_i, w_down_i, cw = layers[i]
        s = saved[i]
        dact = mm_row_dx("ffn_down_dx", dh_b, w_down_i)
        g_down = mm_row_dw("ffn_down_dw", s["act"], dh_b)
        du2, dcw = conv_act_bwd("ffn_conv_act_bwd", s["u2"], cw, cb_full[i], dact, take_early())
        d_cw[i] = dcw[:, 0:3, :].transpose(1, 0, 2).reshape(3, 2 * dff)
        d_cb[i] = dcw[:, 3, :].reshape(2 * dff)
        g_up = mm_col_dw("ffn_up_dw", s["hn2"], du2, w_u, split=2)
        dhn2 = mm_col_dx("ffn_up_dx", du2, w_up_i, split=2)
        dh, dh_b, d_ffn[i] = rms_bwd("ffn_norm_bwd", s["h_mid"], ffn_norm[i], dhn2, dh, settle())
        register(i, "ffn", [g_up, g_down.reshape(N_DEV, -1, d)])
        if i % 2 == 0:
            do = mm_row_dx("a_out_dx", dh_b, w_o, take_early())
            g_o = mm_row_dw("a_out_dw", s["o"], dh_b)
            dlt, do_b = row_delta("a_delta", do, s["o"], n_q)
            dq, dk, dv = mixer_a_bwd(s["qkv_r"], do_b, s["lse"], dlt, n_q, n_kv)
            dqkv, dgain = qk_prep_bwd("a_qk_prep_bwd", s["qkv"], dq, dk, dv, a_q_gain[j], a_k_gain[j], cos, sin,
                                      n_q, n_kv)
            d_qg[j], d_kg[j] = dgain[0], dgain[1]
            g_qkv = mm_col_dw("a_qkv_dw", s["hn"], dqkv, w_a)
            dhn = mm_col_dx("a_qkv_dx", dqkv, w_qkv)
        else:
            dy = mm_row_dx("b_out_dx", dh_b, w_o, take_early())
            g_o = mm_row_dw("b_out_dw", s["y"], dh_b)
            res = combine_bwd("b_combine_bwd", dy, s["outs"], s["lzs"])
            dos, dlts = res[:n_groups], res[n_groups:]
            pieces, rel_rows = [], []
            for g, (win, dil) in enumerate(B_GROUPS):
                dq, dk, dv, dbias = mixer_b_group_bwd(s["qkv"], tables[g][0], dos[g], s["lzs"][g], dlts[g],
                                                      strides[g], g, n_groups, dil)
                pieces += [dq, dk, dv]
                rel_rows.append(bias_bucket_sums(f"b_bias_sums_d{dil}", dbias, tables[g][1]))
            d_rel = d_rel + jnp.concatenate(rel_rows, axis=0)
            dqkv = jnp.concatenate(pieces, axis=1)
            g_qkv = mm_col_dw("b_qkv_dw", s["hn"], dqkv, w_b)
            dhn = mm_col_dx("b_qkv_dx", dqkv, w_qkv)
        dh, dh_b, d_mix[i] = rms_bwd("mix_norm_bwd", s["h_in"], mix_norm[i], dhn, dh, settle())
        register(i, "mix", [g_qkv, g_o.reshape(N_DEV, -1, d)])
    last = pending.pop()[2] if pending else None

    d_rel_bias = d_rel[:, :REL_BUCKETS].T
    small_g = [jnp.stack(d_qg), jnp.stack(d_kg), d_rel_bias, jnp.concatenate(d_mix, 0), jnp.concatenate(d_ffn, 0),
               jnp.stack(d_cb), d_final.reshape(-1), jnp.stack(d_cw), loss_part]
    return dh, layer_grads, small_g, last


def kernel(x, a_w_qkv, a_w_o, a_q_gain, a_k_gain, b_w_qkv, b_w_o, rel_bias, mix_norm, ffn_norm, w_up, conv_w, conv_b, w_down, final_norm, loss_target, m_a_w_qkv, m_a_w_o, m_a_q_gain, m_a_k_gain, m_b_w_qkv, m_b_w_o, m_rel_bias, m_mix_norm, m_ffn_norm, m_w_up, m_conv_w, m_conv_b, m_w_down, m_final_norm, v_a_w_qkv, v_a_w_o, v_a_q_gain, v_a_k_gain, v_b_w_qkv, v_b_w_o, v_rel_bias, v_mix_norm, v_ffn_norm, v_w_up, v_conv_w, v_conv_b, v_w_down, v_final_norm):
    d = x.shape[2]
    depth = mix_norm.shape[0]
    dff = w_down.shape[1] * N_DEV
    w_u = w_up.shape[2]
    mixers = [(a_w_qkv, a_w_o, m_a_w_qkv, m_a_w_o, v_a_w_qkv, v_a_w_o),
              (b_w_qkv, b_w_o, m_b_w_qkv, m_b_w_o, v_b_w_qkv, v_b_w_o)]

    layers = []
    for i in range(depth):
        w_qkv, w_o = mixers[i % 2][0][i // 2], mixers[i % 2][1][i // 2]
        shards = [w_qkv.astype(BF16), w_o.astype(BF16), w_up[i].astype(BF16), w_down[i].astype(BF16), conv_w[i]]
        if i == 0:
            (g_qkv,) = gather_layer("gather_l0_qkv", shards[:1])
            g_o, g_up, g_down, g_cw = gather_layer("gather_l0", shards[1:])
        else:
            g_qkv, g_o, g_up, g_down, g_cw = gather_layer(f"gather_l{i}", shards)
        cw = g_cw.transpose(1, 0, 2).reshape(3, 2, dff).transpose(1, 0, 2)
        layers.append((g_qkv, g_o.reshape(-1, d), g_up, g_down.reshape(dff, d), cw))

    core = lax.axis_index("c").astype(jnp.int32).reshape(1)
    chip = (2 * lax.axis_index("x") + lax.axis_index("y")).astype(jnp.int32).reshape(1)

    def reduce_and_update(i, part, grads):
        w_qkv, w_o, m_qkv, m_o, v_qkv, v_o = mixers[i % 2]
        prefix = ("a_w_", "b_w_")[i % 2]
        state = {"mix": [(prefix + "qkv", w_qkv, m_qkv, v_qkv, i // 2), (prefix + "o", w_o, m_o, v_o, i // 2)],
                 "ffn": [("w_up", w_up, m_w_up, v_w_up, i), ("w_down", w_down, m_w_down, v_w_down, i)]}[part]
        got1 = grads_to_sibling(f"to_sibling_l{i}_{part}", grads)
        sums = [chip_sum(f"chip_sum_l{i}_{part}{a}", grads[a], got1[a], core) for a in range(2)]
        got2 = grads_to_chips(f"to_chips_l{i}_{part}", sums)

        def finish():
            for a, (key, w, m, v, layer) in enumerate(state):
                outs = big_out.get(key) or [lax.empty(w.shape, F32) for _ in range(4)]
                big_out[key] = adamw_layer(f"adamw_l{i}_{part}{a}", sums[a], got2[a], w, m, v, outs, layer, chip)
            return [big_out[key] for key, *_ in state]

        return sums, finish

    big_out = {}
    dh, _, small_g, last = _local_step(x[0], loss_target[0], layers,
                                       (a_q_gain, a_k_gain, rel_bias, mix_norm, ffn_norm, conv_b, final_norm),
                                       reduce_and_update)
    grad_x = dh[None]

    width = 2048
    packed = _pack(small_g, N_DEV * width).reshape(-1, N_DEV, width)
    n_rows = packed.shape[0]
    packed = packed.transpose(1, 0, 2).reshape(N_DEV, n_rows * width)
    red = all_reduce_small("small_all_reduce", packed)
    last()
    red = red.reshape(N_DEV, n_rows, width).transpose(1, 0, 2)
    (g_qg, g_kg, g_rel, g_mix, g_ffn, g_cb, g_fin, g_cw_all, loss) = _unpack(red, [p.shape for p in small_g])
    idx = 4 * lax.axis_index("x") + 2 * lax.axis_index("y") + lax.axis_index("c")
    g_cw_mine = lax.dynamic_slice_in_dim(g_cw_all, idx * w_u, w_u, axis=2)

    small_w = [a_q_gain, a_k_gain, rel_bias, mix_norm, ffn_norm, conv_b, final_norm, conv_w]
    small_m = [m_a_q_gain, m_a_k_gain, m_rel_bias, m_mix_norm, m_ffn_norm, m_conv_b, m_final_norm, m_conv_w]
    small_v = [v_a_q_gain, v_a_k_gain, v_rel_bias, v_mix_norm, v_ffn_norm, v_conv_b, v_final_norm, v_conv_w]
    small_grads = [g_qg, g_kg, g_rel, g_mix, g_ffn, g_cb, g_fin, g_cw_mine]
    shapes = [w.shape for w in small_w]
    pad_rows = (-_pack(small_w, width).shape[0]) % 8

    def pk8(parts):
        p = _pack(parts, width)
        return jnp.pad(p, ((0, pad_rows), (0, 0))) if pad_rows else p

    sd, sm, sv = adamw_small("adamw_small", pk8(small_grads), pk8(small_w), pk8(small_m), pk8(small_v))
    sd, sm, sv = _unpack(sd, shapes), _unpack(sm, shapes), _unpack(sv, shapes)

    names = ["a_w_qkv", "a_w_o", "a_q_gain", "a_k_gain", "b_w_qkv", "b_w_o", "rel_bias", "mix_norm", "ffn_norm",
             "w_up", "conv_w", "conv_b", "w_down", "final_norm"]
    small_names = ["a_q_gain", "a_k_gain", "rel_bias", "mix_norm", "ffn_norm", "conv_b", "final_norm", "conv_w"]
    grads, deltas, new_m, new_v = {}, {}, {}, {}
    for nm, outs in big_out.items():
        grads[nm], deltas[nm], new_m[nm], new_v[nm] = outs
    for a, nm in enumerate(small_names):
        grads[nm] = small_grads[a].reshape(shapes[a])
        deltas[nm], new_m[nm], new_v[nm] = sd[a], sm[a], sv[a]
    return (loss.reshape(()), grad_x, *[grads[n] for n in names], *[deltas[n] for n in names],
            *[new_m[n] for n in names], *[new_v[n] for n in names])
```
